```python
import jax, jax.numpy as jnp
from jax import lax
import numpy as np

D_MODEL = 1024
BATCH = 16
SEQ = 4096
DEPTH = 2

SSD_EXPAND = 2
SSD_D_INNER = SSD_EXPAND * D_MODEL
SSD_HEAD_DIM = 64
SSD_N_HEADS = SSD_D_INNER // SSD_HEAD_DIM
SSD_N_GROUPS = 4
SSD_HEADS_PER_GROUP = SSD_N_HEADS // SSD_N_GROUPS
SSD_D_STATE = 128
SSD_CONV_WIDTH = 4
SSD_CHUNK = 128
SSD_CONV_DIM = SSD_D_INNER + 2 * SSD_N_GROUPS * SSD_D_STATE
DT_MIN = 1e-3
DT_MAX = 1e-1

SC_WIDTH = D_MODEL
SC_CONV_WIDTH = 3

N_BRANCHES = 2

D_FF = 2816
FFN_CONV_WIDTH = 3

EPS = 1e-6

IN_SPLIT_SIZES = (SSD_D_INNER, SSD_CONV_DIM, SSD_N_HEADS,
                  SC_WIDTH, SC_WIDTH, SC_WIDTH, N_BRANCHES * D_MODEL)
D_IN_PROJ = sum(IN_SPLIT_SIZES)

kernel_name = "hybrid_ssd_shortconv_adaln_block"


def rmsnorm(x, g):
    xf = x.astype(jnp.float32)
    xf = xf * lax.rsqrt(jnp.mean(xf * xf, axis=-1, keepdims=True) + EPS)
    return xf.astype(x.dtype) * g


def grouped_rmsnorm(y, g, n_groups):
    shp = y.shape
    yf = y.astype(jnp.float32).reshape(*shp[:-1], n_groups, shp[-1] // n_groups)
    yf = yf * lax.rsqrt(jnp.mean(yf * yf, axis=-1, keepdims=True) + EPS)
    return yf.reshape(shp) * g


def causal_dwconv(u, w, b=None):
    k_width = w.shape[0]
    seqlen = u.shape[1]
    up = jnp.pad(u, ((0, 0), (k_width - 1, 0), (0, 0)))
    y = up[:, 0:seqlen] * w[0]
    for k in range(1, k_width):
        y = y + up[:, k:k + seqlen] * w[k]
    if b is not None:
        y = y + b
    return y


def ssd_chunked(xh, dt, a, bmat, cmat):
    bsz, seqlen = xh.shape[0], xh.shape[1]
    nc = seqlen // SSD_CHUNK
    L, G, R, P, N = SSD_CHUNK, SSD_N_GROUPS, SSD_HEADS_PER_GROUP, SSD_HEAD_DIM, SSD_D_STATE
    x = (xh * dt[..., None]).reshape(bsz, nc, L, G, R, P)
    adt = jnp.moveaxis((dt * a).reshape(bsz, nc, L, G, R), 2, -1)
    a_cs = jnp.cumsum(adt, axis=-1)
    bc = bmat.reshape(bsz, nc, L, G, N)
    cc = cmat.reshape(bsz, nc, L, G, N)

    causal = jnp.tril(jnp.ones((L, L), dtype=bool))
    decay = jnp.exp(jnp.where(causal, a_cs[..., :, None] - a_cs[..., None, :], -jnp.inf))
    scores = jnp.einsum("bclgn,bcsgn->bcgls", cc, bc)
    m = scores[:, :, :, None] * decay
    y_diag = jnp.einsum("bcgrls,bcsgrp->bclgrp", m, x)

    decay_states = jnp.exp(a_cs[..., -1:] - a_cs)
    states = jnp.einsum("bclgn,bcgrl,bclgrp->bcgrpn", bc, decay_states, x)
    chunk_decay = jnp.exp(a_cs[..., -1])

    def step(h, inp):
        s_c, d_c = inp
        h_new = h * d_c[..., None, None] + s_c
        return h_new, h

    h0 = jnp.zeros((bsz, G, R, P, N), dtype=states.dtype)
    _, prev = lax.scan(step, h0, (jnp.moveaxis(states, 1, 0), jnp.moveaxis(chunk_decay, 1, 0)))
    prev = jnp.moveaxis(prev, 0, 1)

    y_off = jnp.einsum("bclgn,bcgrpn,bcgrl->bclgrp", cc, prev, jnp.exp(a_cs))
    return (y_diag + y_off).reshape(bsz, seqlen, SSD_N_HEADS, P)


def ssd_branch(z, xbc, dt_raw, conv_w, conv_b, dt_bias, a_log, d_skip, norm_g):
    bsz, seqlen = z.shape[0], z.shape[1]
    xbc = jax.nn.silu(causal_dwconv(xbc, conv_w, conv_b))
    gn = SSD_N_GROUPS * SSD_D_STATE
    xs, bm, cm = jnp.split(xbc, [SSD_D_INNER, SSD_D_INNER + gn], axis=-1)
    xh = xs.reshape(bsz, seqlen, SSD_N_HEADS, SSD_HEAD_DIM)
    dt = jax.nn.softplus(dt_raw.astype(jnp.float32) + dt_bias.astype(jnp.float32))
    a = -jnp.exp(a_log.astype(jnp.float32))
    y = ssd_chunked(xh.astype(jnp.float32), dt, a,
                    bm.reshape(bsz, seqlen, SSD_N_GROUPS, SSD_D_STATE).astype(jnp.float32),
                    cm.reshape(bsz, seqlen, SSD_N_GROUPS, SSD_D_STATE).astype(jnp.float32))
    y = y + d_skip.astype(jnp.float32)[:, None] * xh.astype(jnp.float32)
    y = y.reshape(bsz, seqlen, SSD_D_INNER) * jax.nn.silu(z.astype(jnp.float32))
    return grouped_rmsnorm(y, norm_g, SSD_N_GROUPS).astype(z.dtype)


def short_conv_branch(b_gate, c_gate, h, conv_w):
    return b_gate * causal_dwconv(c_gate * h, conv_w)


def hybrid_layer(x, c_act, ada_w, ada_b, mix_pre_g, mix_post_g, w_in,
                 ssd_conv_w, ssd_conv_b, ssd_dt_bias, ssd_a_log, ssd_d, ssd_norm_g,
                 w_ssd_out, sc_conv_w, w_sc_out, w_o,
                 ffn_pre_g, ffn_post_g, w_up, ffn_conv_w, ffn_conv_b, w_down):
    mod = c_act @ ada_w + ada_b
    sh1, sc1, gt1, sh2, sc2, gt2 = [m[:, None, :] for m in jnp.split(mod, 6, axis=-1)]

    h = rmsnorm(x, mix_pre_g) * (1.0 + sc1) + sh1
    proj = h @ w_in
    points, acc = [], 0
    for s in IN_SPLIT_SIZES[:-1]:
        acc += s
        points.append(acc)
    z, xbc, dt_raw, sc_b, sc_c, sc_h, gates = jnp.split(proj, points, axis=-1)
    y_ssd = ssd_branch(z, xbc, dt_raw, ssd_conv_w, ssd_conv_b, ssd_dt_bias,
                       ssd_a_log, ssd_d, ssd_norm_g) @ w_ssd_out
    y_sc = short_conv_branch(sc_b, sc_c, sc_h, sc_conv_w) @ w_sc_out
    g_ssd, g_sc = jnp.split(jax.nn.sigmoid(gates), 2, axis=-1)
    mix = (g_ssd * y_ssd + g_sc * y_sc) @ w_o
    x = x + gt1 * rmsnorm(mix, mix_post_g)

    h = rmsnorm(x, ffn_pre_g) * (1.0 + sc2) + sh2
    u = causal_dwconv(h @ w_up, ffn_conv_w, ffn_conv_b)
    u_gate, u_val = jnp.split(u, 2, axis=-1)
    f = (jax.nn.silu(u_gate) * u_val) @ w_down
    x = x + gt2 * rmsnorm(f, ffn_post_g)
    return x


def _fwd_setup_inputs(seed: int = 0) -> dict:
    key = jax.random.key(seed)
    ks = jax.random.split(key, 32)
    f32 = jnp.float32

    def dense(k, fan_in, fan_out, scale=1.0):
        return jax.random.normal(k, (DEPTH, fan_in, fan_out), f32) * (scale * fan_in ** -0.5)

    def gain(k, n):
        return 1.0 + 0.05 * jax.random.normal(k, (DEPTH, n), f32)

    def small(k, shape, s=0.02):
        return s * jax.random.normal(k, shape, f32)

    u = jax.random.uniform(ks[8], (DEPTH, SSD_N_HEADS), f32)
    dt0 = jnp.exp(u * (np.log(DT_MAX) - np.log(DT_MIN)) + np.log(DT_MIN))
    dt_bias = dt0 + jnp.log(-jnp.expm1(-dt0))
    a_log = jnp.log(jax.random.uniform(ks[9], (DEPTH, SSD_N_HEADS), f32, 1.0, 16.0))

    return {
        "x": jax.random.normal(ks[0], (BATCH, SEQ, D_MODEL), f32),
        "c": jax.random.normal(ks[1], (BATCH, D_MODEL), f32),
        "ada_w": dense(ks[2], D_MODEL, 6 * D_MODEL, 0.5),
        "ada_b": small(ks[3], (DEPTH, 6 * D_MODEL)),
        "mix_pre_g": gain(ks[4], D_MODEL),
        "mix_post_g": gain(ks[5], D_MODEL),
        "w_in": dense(ks[6], D_MODEL, D_IN_PROJ),
        "ssd_conv_w": jax.random.normal(ks[7], (DEPTH, SSD_CONV_WIDTH, SSD_CONV_DIM), f32) * SSD_CONV_WIDTH ** -0.5,
        "ssd_conv_b": small(ks[10], (DEPTH, SSD_CONV_DIM)),
        "ssd_dt_bias": dt_bias,
        "ssd_a_log": a_log,
        "ssd_d": 1.0 + 0.1 * jax.random.normal(ks[11], (DEPTH, SSD_N_HEADS), f32),
        "ssd_norm_g": gain(ks[12], SSD_D_INNER),
        "w_ssd_out": dense(ks[13], SSD_D_INNER, D_MODEL),
        "sc_conv_w": jax.random.normal(ks[14], (DEPTH, SC_CONV_WIDTH, SC_WIDTH), f32) * SC_CONV_WIDTH ** -0.5,
        "w_sc_out": dense(ks[15], SC_WIDTH, D_MODEL),
        "w_o": dense(ks[16], D_MODEL, D_MODEL),
        "ffn_pre_g": gain(ks[17], D_MODEL),
        "ffn_post_g": gain(ks[18], D_MODEL),
        "w_up": dense(ks[19], D_MODEL, 2 * D_FF),
        "ffn_conv_w": jax.random.normal(ks[20], (DEPTH, FFN_CONV_WIDTH, 2 * D_FF), f32) * FFN_CONV_WIDTH ** -0.5,
        "ffn_conv_b": small(ks[21], (DEPTH, 2 * D_FF)),
        "w_down": dense(ks[22], D_FF, D_MODEL),
    }


def _fwd_reference(x, c, ada_w, ada_b, mix_pre_g, mix_post_g, w_in,
              ssd_conv_w, ssd_conv_b, ssd_dt_bias, ssd_a_log, ssd_d, ssd_norm_g,
              w_ssd_out, sc_conv_w, w_sc_out, w_o,
              ffn_pre_g, ffn_post_g, w_up, ffn_conv_w, ffn_conv_b, w_down):
    c_act = jax.nn.silu(c)
    for i in range(DEPTH):
        x = hybrid_layer(x, c_act, ada_w[i], ada_b[i], mix_pre_g[i], mix_post_g[i], w_in[i],
                         ssd_conv_w[i], ssd_conv_b[i], ssd_dt_bias[i], ssd_a_log[i], ssd_d[i],
                         ssd_norm_g[i], w_ssd_out[i], sc_conv_w[i], w_sc_out[i], w_o[i],
                         ffn_pre_g[i], ffn_post_g[i], w_up[i], ffn_conv_w[i], ffn_conv_b[i],
                         w_down[i])
    return x


import jax as _jax
import jax.numpy as _jnp

TWIN_FORMAT = 'train_step'
FWD_PARAMS = ['x', 'c', 'ada_w', 'ada_b', 'mix_pre_g', 'mix_post_g', 'w_in', 'ssd_conv_w', 'ssd_conv_b', 'ssd_dt_bias', 'ssd_a_log', 'ssd_d', 'ssd_norm_g', 'w_ssd_out', 'sc_conv_w', 'w_sc_out', 'w_o', 'ffn_pre_g', 'ffn_post_g', 'w_up', 'ffn_conv_w', 'ffn_conv_b', 'w_down']
TWIN_WEIGHTS = ['ada_w', 'ada_b', 'mix_pre_g', 'mix_post_g', 'w_in', 'ssd_conv_w', 'ssd_conv_b', 'ssd_dt_bias', 'ssd_a_log', 'ssd_d', 'ssd_norm_g', 'w_ssd_out', 'sc_conv_w', 'w_sc_out', 'w_o', 'ffn_pre_g', 'ffn_post_g', 'w_up', 'ffn_conv_w', 'ffn_conv_b', 'w_down']
TWIN_DIFF_INPUT = 'x'
TWIN_INPUTS = ['x', 'c', 'ada_w', 'ada_b', 'mix_pre_g', 'mix_post_g', 'w_in', 'ssd_conv_w', 'ssd_conv_b', 'ssd_dt_bias', 'ssd_a_log', 'ssd_d', 'ssd_norm_g', 'w_ssd_out', 'sc_conv_w', 'w_sc_out', 'w_o', 'ffn_pre_g', 'ffn_post_g', 'w_up', 'ffn_conv_w', 'ffn_conv_b', 'w_down', 'loss_target', 'm_ada_w', 'm_ada_b', 'm_mix_pre_g', 'm_mix_post_g', 'm_w_in', 'm_ssd_conv_w', 'm_ssd_conv_b', 'm_ssd_dt_bias', 'm_ssd_a_log', 'm_ssd_d', 'm_ssd_norm_g', 'm_w_ssd_out', 'm_sc_conv_w', 'm_w_sc_out', 'm_w_o', 'm_ffn_pre_g', 'm_ffn_post_g', 'm_w_up', 'm_ffn_conv_w', 'm_ffn_conv_b', 'm_w_down', 'v_ada_w', 'v_ada_b', 'v_mix_pre_g', 'v_mix_post_g', 'v_w_in', 'v_ssd_conv_w', 'v_ssd_conv_b', 'v_ssd_dt_bias', 'v_ssd_a_log', 'v_ssd_d', 'v_ssd_norm_g', 'v_w_ssd_out', 'v_sc_conv_w', 'v_w_sc_out', 'v_w_o', 'v_ffn_pre_g', 'v_ffn_post_g', 'v_w_up', 'v_ffn_conv_w', 'v_ffn_conv_b', 'v_w_down']
TWIN_OUTPUTS = ['loss', 'grad_x', 'grad_ada_w', 'grad_ada_b', 'grad_mix_pre_g', 'grad_mix_post_g', 'grad_w_in', 'grad_ssd_conv_w', 'grad_ssd_conv_b', 'grad_ssd_dt_bias', 'grad_ssd_a_log', 'grad_ssd_d', 'grad_ssd_norm_g', 'grad_w_ssd_out', 'grad_sc_conv_w', 'grad_w_sc_out', 'grad_w_o', 'grad_ffn_pre_g', 'grad_ffn_post_g', 'grad_w_up', 'grad_ffn_conv_w', 'grad_ffn_conv_b', 'grad_w_down', 'delta_ada_w', 'delta_ada_b', 'delta_mix_pre_g', 'delta_mix_post_g', 'delta_w_in', 'delta_ssd_conv_w', 'delta_ssd_conv_b', 'delta_ssd_dt_bias', 'delta_ssd_a_log', 'delta_ssd_d', 'delta_ssd_norm_g', 'delta_w_ssd_out', 'delta_sc_conv_w', 'delta_w_sc_out', 'delta_w_o', 'delta_ffn_pre_g', 'delta_ffn_post_g', 'delta_w_up', 'delta_ffn_conv_w', 'delta_ffn_conv_b', 'delta_w_down', 'new_m_ada_w', 'new_m_ada_b', 'new_m_mix_pre_g', 'new_m_mix_post_g', 'new_m_w_in', 'new_m_ssd_conv_w', 'new_m_ssd_conv_b', 'new_m_ssd_dt_bias', 'new_m_ssd_a_log', 'new_m_ssd_d', 'new_m_ssd_norm_g', 'new_m_w_ssd_out', 'new_m_sc_conv_w', 'new_m_w_sc_out', 'new_m_w_o', 'new_m_ffn_pre_g', 'new_m_ffn_post_g', 'new_m_w_up', 'new_m_ffn_conv_w', 'new_m_ffn_conv_b', 'new_m_w_down', 'new_v_ada_w', 'new_v_ada_b', 'new_v_mix_pre_g', 'new_v_mix_post_g', 'new_v_w_in', 'new_v_ssd_conv_w', 'new_v_ssd_conv_b', 'new_v_ssd_dt_bias', 'new_v_ssd_a_log', 'new_v_ssd_d', 'new_v_ssd_norm_g', 'new_v_w_ssd_out', 'new_v_sc_conv_w', 'new_v_w_sc_out', 'new_v_w_o', 'new_v_ffn_pre_g', 'new_v_ffn_post_g', 'new_v_w_up', 'new_v_ffn_conv_w', 'new_v_ffn_conv_b', 'new_v_w_down']
TWIN_LEAF_KINDS = {'loss': 'loss', 'grad_x': 'grad_x', 'grad_ada_w': 'grad_w', 'grad_ada_b': 'grad_w', 'grad_mix_pre_g': 'grad_w', 'grad_mix_post_g': 'grad_w', 'grad_w_in': 'grad_w', 'grad_ssd_conv_w': 'grad_w', 'grad_ssd_conv_b': 'grad_w', 'grad_ssd_dt_bias': 'grad_w', 'grad_ssd_a_log': 'grad_w', 'grad_ssd_d': 'grad_w', 'grad_ssd_norm_g': 'grad_w', 'grad_w_ssd_out': 'grad_w', 'grad_sc_conv_w': 'grad_w', 'grad_w_sc_out': 'grad_w', 'grad_w_o': 'grad_w', 'grad_ffn_pre_g': 'grad_w', 'grad_ffn_post_g': 'grad_w', 'grad_w_up': 'grad_w', 'grad_ffn_conv_w': 'grad_w', 'grad_ffn_conv_b': 'grad_w', 'grad_w_down': 'grad_w', 'delta_ada_w': 'delta_w', 'delta_ada_b': 'delta_w', 'delta_mix_pre_g': 'delta_w', 'delta_mix_post_g': 'delta_w', 'delta_w_in': 'delta_w', 'delta_ssd_conv_w': 'delta_w', 'delta_ssd_conv_b': 'delta_w', 'delta_ssd_dt_bias': 'delta_w', 'delta_ssd_a_log': 'delta_w', 'delta_ssd_d': 'delta_w', 'delta_ssd_norm_g': 'delta_w', 'delta_w_ssd_out': 'delta_w', 'delta_sc_conv_w': 'delta_w', 'delta_w_sc_out': 'delta_w', 'delta_w_o': 'delta_w', 'delta_ffn_pre_g': 'delta_w', 'delta_ffn_post_g': 'delta_w', 'delta_w_up': 'delta_w', 'delta_ffn_conv_w': 'delta_w', 'delta_ffn_conv_b': 'delta_w', 'delta_w_down': 'delta_w', 'new_m_ada_w': 'new_m', 'new_m_ada_b': 'new_m', 'new_m_mix_pre_g': 'new_m', 'new_m_mix_post_g': 'new_m', 'new_m_w_in': 'new_m', 'new_m_ssd_conv_w': 'new_m', 'new_m_ssd_conv_b': 'new_m', 'new_m_ssd_dt_bias': 'new_m', 'new_m_ssd_a_log': 'new_m', 'new_m_ssd_d': 'new_m', 'new_m_ssd_norm_g': 'new_m', 'new_m_w_ssd_out': 'new_m', 'new_m_sc_conv_w': 'new_m', 'new_m_w_sc_out': 'new_m', 'new_m_w_o': 'new_m', 'new_m_ffn_pre_g': 'new_m', 'new_m_ffn_post_g': 'new_m', 'new_m_w_up': 'new_m', 'new_m_ffn_conv_w': 'new_m', 'new_m_ffn_conv_b': 'new_m', 'new_m_w_down': 'new_m', 'new_v_ada_w': 'new_v', 'new_v_ada_b': 'new_v', 'new_v_mix_pre_g': 'new_v', 'new_v_mix_post_g': 'new_v', 'new_v_w_in': 'new_v', 'new_v_ssd_conv_w': 'new_v', 'new_v_ssd_conv_b': 'new_v', 'new_v_ssd_dt_bias': 'new_v', 'new_v_ssd_a_log': 'new_v', 'new_v_ssd_d': 'new_v', 'new_v_ssd_norm_g': 'new_v', 'new_v_w_ssd_out': 'new_v', 'new_v_sc_conv_w': 'new_v', 'new_v_w_sc_out': 'new_v', 'new_v_w_o': 'new_v', 'new_v_ffn_pre_g': 'new_v', 'new_v_ffn_post_g': 'new_v', 'new_v_w_up': 'new_v', 'new_v_ffn_conv_w': 'new_v', 'new_v_ffn_conv_b': 'new_v', 'new_v_w_down': 'new_v'}


def _forward(args):
    return _fwd_reference(*[args[k] for k in FWD_PARAMS])


def _output_shape():
    out = _jax.eval_shape(lambda: _forward(_fwd_setup_inputs(0)))
    return out.shape, out.dtype

N_MICROBATCH = 1
ADAM_LR = 0.001
ADAM_B1 = 0.9
ADAM_B2 = 0.999
ADAM_EPS = 1e-08
ADAM_WD = 0.01
ADAM_STEP = 10
PER_EXAMPLE_BATCH_AXIS = {'x': 0, 'c': 0, 'loss_target': 0}
SHARED_INPUTS = []
_WEIGHT_DTYPES = {'ada_w': _jnp.float32, 'ada_b': _jnp.float32, 'mix_pre_g': _jnp.float32, 'mix_post_g': _jnp.float32, 'w_in': _jnp.float32, 'ssd_conv_w': _jnp.float32, 'ssd_conv_b': _jnp.float32, 'ssd_dt_bias': _jnp.float32, 'ssd_a_log': _jnp.float32, 'ssd_d': _jnp.float32, 'ssd_norm_g': _jnp.float32, 'w_ssd_out': _jnp.float32, 'sc_conv_w': _jnp.float32, 'w_sc_out': _jnp.float32, 'w_o': _jnp.float32, 'ffn_pre_g': _jnp.float32, 'ffn_post_g': _jnp.float32, 'w_up': _jnp.float32, 'ffn_conv_w': _jnp.float32, 'ffn_conv_b': _jnp.float32, 'w_down': _jnp.float32}
MOMENT_SCALE = {'ada_w': 3.327848e+00, 'ada_b': 6.362462e+00, 'mix_pre_g': 2.382251e-01, 'mix_post_g': 7.495270e+00, 'w_in': 8.100325e-02, 'ssd_conv_w': 6.374858e-02, 'ssd_conv_b': 1.175940e-01, 'ssd_dt_bias': 1.435141e-01, 'ssd_a_log': 4.916635e-01, 'ssd_d': 3.511300e-01, 'ssd_norm_g': 9.539391e-02, 'w_ssd_out': 1.312080e-01, 'sc_conv_w': 1.173924e-01, 'w_sc_out': 1.211048e-01, 'w_o': 1.883860e-01, 'ffn_pre_g': 1.855835e-01, 'ffn_post_g': 7.444473e+00, 'w_up': 9.009744e-02, 'ffn_conv_w': 9.225496e-02, 'ffn_conv_b': 1.376713e-01, 'w_down': 1.654774e-01}


def _to_microbatches(a, axis):
    t = _jnp.moveaxis(a, axis, 0)
    t = t.reshape((N_MICROBATCH, t.shape[0] // N_MICROBATCH) + t.shape[1:])
    return _jnp.moveaxis(t, 1, axis + 1)


def setup_inputs(seed: int = 0) -> dict:
    inp = _fwd_setup_inputs(seed)
    key = _jax.random.fold_in(_jax.random.key(seed), 7919)
    shape, _ = _output_shape()
    out = dict(inp)
    out["loss_target"] = _jax.random.normal(_jax.random.fold_in(key, 0), shape, _jnp.float32)
    for i, name in enumerate(TWIN_WEIGHTS):
        w = inp[name].astype(_jnp.float32)
        if MOMENT_SCALE is None:
            s = _jnp.sqrt(_jnp.mean(_jnp.square(w)) + 1e-30)
        else:
            s = MOMENT_SCALE[name]
        km, kv = _jax.random.split(_jax.random.fold_in(key, i + 1))
        out[name] = w
        out["m_" + name] = s * _jax.random.normal(km, w.shape, _jnp.float32)
        out["v_" + name] = (s * s) * _jax.random.uniform(kv, w.shape, _jnp.float32, 0.5, 1.5)
    if N_MICROBATCH > 1:
        for name, axis in PER_EXAMPLE_BATCH_AXIS.items():
            out[name] = _to_microbatches(out[name], axis)
    return {'x': out['x'], 'c': out['c'], 'ada_w': out['ada_w'], 'ada_b': out['ada_b'], 'mix_pre_g': out['mix_pre_g'], 'mix_post_g': out['mix_post_g'], 'w_in': out['w_in'], 'ssd_conv_w': out['ssd_conv_w'], 'ssd_conv_b': out['ssd_conv_b'], 'ssd_dt_bias': out['ssd_dt_bias'], 'ssd_a_log': out['ssd_a_log'], 'ssd_d': out['ssd_d'], 'ssd_norm_g': out['ssd_norm_g'], 'w_ssd_out': out['w_ssd_out'], 'sc_conv_w': out['sc_conv_w'], 'w_sc_out': out['w_sc_out'], 'w_o': out['w_o'], 'ffn_pre_g': out['ffn_pre_g'], 'ffn_post_g': out['ffn_post_g'], 'w_up': out['w_up'], 'ffn_conv_w': out['ffn_conv_w'], 'ffn_conv_b': out['ffn_conv_b'], 'w_down': out['w_down'], 'loss_target': out['loss_target'], 'm_ada_w': out['m_ada_w'], 'm_ada_b': out['m_ada_b'], 'm_mix_pre_g': out['m_mix_pre_g'], 'm_mix_post_g': out['m_mix_post_g'], 'm_w_in': out['m_w_in'], 'm_ssd_conv_w': out['m_ssd_conv_w'], 'm_ssd_conv_b': out['m_ssd_conv_b'], 'm_ssd_dt_bias': out['m_ssd_dt_bias'], 'm_ssd_a_log': out['m_ssd_a_log'], 'm_ssd_d': out['m_ssd_d'], 'm_ssd_norm_g': out['m_ssd_norm_g'], 'm_w_ssd_out': out['m_w_ssd_out'], 'm_sc_conv_w': out['m_sc_conv_w'], 'm_w_sc_out': out['m_w_sc_out'], 'm_w_o': out['m_w_o'], 'm_ffn_pre_g': out['m_ffn_pre_g'], 'm_ffn_post_g': out['m_ffn_post_g'], 'm_w_up': out['m_w_up'], 'm_ffn_conv_w': out['m_ffn_conv_w'], 'm_ffn_conv_b': out['m_ffn_conv_b'], 'm_w_down': out['m_w_down'], 'v_ada_w': out['v_ada_w'], 'v_ada_b': out['v_ada_b'], 'v_mix_pre_g': out['v_mix_pre_g'], 'v_mix_post_g': out['v_mix_post_g'], 'v_w_in': out['v_w_in'], 'v_ssd_conv_w': out['v_ssd_conv_w'], 'v_ssd_conv_b': out['v_ssd_conv_b'], 'v_ssd_dt_bias': out['v_ssd_dt_bias'], 'v_ssd_a_log': out['v_ssd_a_log'], 'v_ssd_d': out['v_ssd_d'], 'v_ssd_norm_g': out['v_ssd_norm_g'], 'v_w_ssd_out': out['v_w_ssd_out'], 'v_sc_conv_w': out['v_sc_conv_w'], 'v_w_sc_out': out['v_w_sc_out'], 'v_w_o': out['v_w_o'], 'v_ffn_pre_g': out['v_ffn_pre_g'], 'v_ffn_post_g': out['v_ffn_post_g'], 'v_w_up': out['v_w_up'], 'v_ffn_conv_w': out['v_ffn_conv_w'], 'v_ffn_conv_b': out['v_ffn_conv_b'], 'v_w_down': out['v_w_down']}


def _loss(weights, diff, rest, loss_target):
    with _jax.named_scope("forward"):
        args = {**rest, TWIN_DIFF_INPUT: diff, **{k: w.astype(_WEIGHT_DTYPES[k]) for k, w in weights.items()}}
        y = _forward(args)
    with _jax.named_scope("loss_head"):
        err = _jnp.square(y.astype(_jnp.float32) - loss_target)
        return 0.5 * _jnp.sum(_jnp.mean(err, axis=-1)) if err.ndim else 0.5 * err


def _adamw(w, g, m, v):
    m = ADAM_B1 * m + (1.0 - ADAM_B1) * g
    v = ADAM_B2 * v + (1.0 - ADAM_B2) * _jnp.square(g)
    m_hat = m / (1.0 - ADAM_B1 ** ADAM_STEP)
    v_hat = v / (1.0 - ADAM_B2 ** ADAM_STEP)
    delta = -ADAM_LR * (m_hat / (_jnp.sqrt(v_hat) + ADAM_EPS) + ADAM_WD * w)
    return delta, m, v


def reference(x, c, ada_w, ada_b, mix_pre_g, mix_post_g, w_in, ssd_conv_w, ssd_conv_b, ssd_dt_bias, ssd_a_log, ssd_d, ssd_norm_g, w_ssd_out, sc_conv_w, w_sc_out, w_o, ffn_pre_g, ffn_post_g, w_up, ffn_conv_w, ffn_conv_b, w_down, loss_target, m_ada_w, m_ada_b, m_mix_pre_g, m_mix_post_g, m_w_in, m_ssd_conv_w, m_ssd_conv_b, m_ssd_dt_bias, m_ssd_a_log, m_ssd_d, m_ssd_norm_g, m_w_ssd_out, m_sc_conv_w, m_w_sc_out, m_w_o, m_ffn_pre_g, m_ffn_post_g, m_w_up, m_ffn_conv_w, m_ffn_conv_b, m_w_down, v_ada_w, v_ada_b, v_mix_pre_g, v_mix_post_g, v_w_in, v_ssd_conv_w, v_ssd_conv_b, v_ssd_dt_bias, v_ssd_a_log, v_ssd_d, v_ssd_norm_g, v_w_ssd_out, v_sc_conv_w, v_w_sc_out, v_w_o, v_ffn_pre_g, v_ffn_post_g, v_w_up, v_ffn_conv_w, v_ffn_conv_b, v_w_down):
    given = dict(x=x, c=c, ada_w=ada_w, ada_b=ada_b, mix_pre_g=mix_pre_g, mix_post_g=mix_post_g, w_in=w_in, ssd_conv_w=ssd_conv_w, ssd_conv_b=ssd_conv_b, ssd_dt_bias=ssd_dt_bias, ssd_a_log=ssd_a_log, ssd_d=ssd_d, ssd_norm_g=ssd_norm_g, w_ssd_out=w_ssd_out, sc_conv_w=sc_conv_w, w_sc_out=w_sc_out, w_o=w_o, ffn_pre_g=ffn_pre_g, ffn_post_g=ffn_post_g, w_up=w_up, ffn_conv_w=ffn_conv_w, ffn_conv_b=ffn_conv_b, w_down=w_down, loss_target=loss_target, m_ada_w=m_ada_w, m_ada_b=m_ada_b, m_mix_pre_g=m_mix_pre_g, m_mix_post_g=m_mix_post_g, m_w_in=m_w_in, m_ssd_conv_w=m_ssd_conv_w, m_ssd_conv_b=m_ssd_conv_b, m_ssd_dt_bias=m_ssd_dt_bias, m_ssd_a_log=m_ssd_a_log, m_ssd_d=m_ssd_d, m_ssd_norm_g=m_ssd_norm_g, m_w_ssd_out=m_w_ssd_out, m_sc_conv_w=m_sc_conv_w, m_w_sc_out=m_w_sc_out, m_w_o=m_w_o, m_ffn_pre_g=m_ffn_pre_g, m_ffn_post_g=m_ffn_post_g, m_w_up=m_w_up, m_ffn_conv_w=m_ffn_conv_w, m_ffn_conv_b=m_ffn_conv_b, m_w_down=m_w_down, v_ada_w=v_ada_w, v_ada_b=v_ada_b, v_mix_pre_g=v_mix_pre_g, v_mix_post_g=v_mix_post_g, v_w_in=v_w_in, v_ssd_conv_w=v_ssd_conv_w, v_ssd_conv_b=v_ssd_conv_b, v_ssd_dt_bias=v_ssd_dt_bias, v_ssd_a_log=v_ssd_a_log, v_ssd_d=v_ssd_d, v_ssd_norm_g=v_ssd_norm_g, v_w_ssd_out=v_w_ssd_out, v_sc_conv_w=v_sc_conv_w, v_w_sc_out=v_w_sc_out, v_w_o=v_w_o, v_ffn_pre_g=v_ffn_pre_g, v_ffn_post_g=v_ffn_post_g, v_w_up=v_w_up, v_ffn_conv_w=v_ffn_conv_w, v_ffn_conv_b=v_ffn_conv_b, v_w_down=v_w_down)
    weights = {n: given[n] for n in TWIN_WEIGHTS}
    shared = {n: given[n] for n in SHARED_INPUTS}
    per_example = {n: given[n] for n in ['x', 'c']}
    grad_fn = _jax.value_and_grad(_loss, argnums=(0, 1))

    def one_microbatch(ex, loss_target):
        ex = dict(ex)
        diff = ex.pop(TWIN_DIFF_INPUT)
        return grad_fn(weights, diff, {**shared, **ex}, loss_target)

    if N_MICROBATCH == 1:
        loss, (grad_w, grad_x) = one_microbatch(per_example, given["loss_target"])
    else:
        def body(carry, xs):
            loss_sum, grad_sum = carry
            l_k, (gw_k, gx_k) = one_microbatch(xs[0], xs[1])
            with _jax.named_scope("update"):
                return (loss_sum + l_k, _jax.tree.map(_jnp.add, grad_sum, gw_k)), gx_k

        init = (_jnp.zeros((), _jnp.float32), _jax.tree.map(_jnp.zeros_like, weights))
        (loss, grad_w), grad_x = _jax.lax.scan(body, init, (per_example, given["loss_target"]))
    with _jax.named_scope("update"):
        delta_w, new_m, new_v = {}, {}, {}
        for n in TWIN_WEIGHTS:
            delta_w[n], new_m[n], new_v[n] = _adamw(weights[n], grad_w[n], given["m_" + n], given["v_" + n])
    return (loss, grad_x, *[grad_w[n] for n in TWIN_WEIGHTS], *[delta_w[n] for n in TWIN_WEIGHTS],
            *[new_m[n] for n in TWIN_WEIGHTS], *[new_v[n] for n in TWIN_WEIGHTS])
```

```python
import functools

import jax
import jax.numpy as jnp
from jax import lax
from jax.experimental import pallas as pl
from jax.experimental.pallas import tpu as pltpu

F32, BF16 = jnp.float32, jnp.bfloat16
SDS = jax.ShapeDtypeStruct
MESH = pl.DeviceIdType.MESH
HIGHEST = lax.Precision.HIGHEST

EPS = 1e-6
N_STATE = 128
CHUNK = 128
SSD_CONV_K, SC_CONV_K, FFN_CONV_K = 4, 3, 3
N_DEV = 8
ROW_W = 1024
ROW_PAD = 256
GLU_W = 256
MOD_ROWS = 128
VMEM_LIMIT = 48 * 2**20

ADAM_LR, ADAM_B1, ADAM_B2, ADAM_EPS, ADAM_WD, ADAM_STEP = 0.001, 0.9, 0.999, 1e-08, 0.01, 10

NT = (((1,), (1,)), ((), ()))
TN = (((0,), (0,)), ((), ()))
NN = (((1,), (0,)), ((), ()))


def _tile(n, cap, mult=128):
    best = None
    for t in range(mult, min(n, cap) + 1, mult):
        if n % t == 0:
            best = t
    return best if best is not None else n


def _pc(body, *, grid, in_specs, out_specs, out_shape, name, scratch=()):
    return pl.pallas_call(
        body, grid=grid, in_specs=in_specs, out_specs=out_specs, out_shape=out_shape,
        scratch_shapes=list(scratch), name=name,
        compiler_params=pltpu.CompilerParams(
            dimension_semantics=("arbitrary",) * len(grid), vmem_limit_bytes=VMEM_LIMIT))


def _silu(x):
    return x * jax.nn.sigmoid(x)


def _dsilu(x):
    s = jax.nn.sigmoid(x)
    return s * (1.0 + x * (1.0 - s))


def _softplus(x):
    return jnp.maximum(x, 0.0) + jnp.log(1.0 + jnp.exp(-jnp.abs(x)))


def _dot(a, b, dims=NN):
    return lax.dot_general(a, b, dims, preferred_element_type=F32)


def _bsum(v, rows=8):
    return jnp.broadcast_to(v, (rows, v.shape[1]))


def _mm(a, b, mode, out_dtype, name, caps=(1024, 1024, 2048)):
    if mode == "nn":
        (M, K), (K2, N) = a.shape, b.shape
    elif mode == "nt":
        (M, K), (N, K2) = a.shape, b.shape
    else:
        (K, M), (K2, N) = a.shape, b.shape
    assert K == K2, (a.shape, b.shape, mode)
    tm, tn, tk = _tile(M, caps[0]), _tile(N, caps[1]), _tile(K, caps[2])
    nk = K // tk
    dims = {"nn": NN, "nt": NT, "tn": TN}[mode]
    if mode == "tn":
        a_spec = pl.BlockSpec((tk, tm), lambda i, j, k: (k, i))
    else:
        a_spec = pl.BlockSpec((tm, tk), lambda i, j, k: (i, k))
    if mode == "nt":
        b_spec = pl.BlockSpec((tn, tk), lambda i, j, k: (j, k))
    else:
        b_spec = pl.BlockSpec((tk, tn), lambda i, j, k: (k, j))

    def body(a_ref, b_ref, o_ref, *acc):
        part = _dot(a_ref[...].astype(BF16), b_ref[...].astype(BF16), dims)
        if nk == 1:
            o_ref[...] = part.astype(o_ref.dtype)
        else:
            acc_ref, = acc
            k = pl.program_id(2)

            @pl.when(k == 0)
            def _():
                acc_ref[...] = part

            @pl.when(k > 0)
            def _():
                acc_ref[...] += part

            @pl.when(k == nk - 1)
            def _():
                o_ref[...] = acc_ref[...].astype(o_ref.dtype)

    return _pc(body, grid=(M // tm, N // tn, nk), in_specs=[a_spec, b_spec],
               out_specs=pl.BlockSpec((tm, tn), lambda i, j, k: (i, j)),
               out_shape=SDS((M, N), out_dtype), name=name,
               scratch=() if nk == 1 else (pltpu.VMEM((tm, tn), F32),))(a, b)


def _modk(c8, ada_w, ada_b, name):
    rows, D = c8.shape
    N = ada_w.shape[1]
    tn = _tile(N, 1536)

    def body(c_ref, w_ref, b_ref, mod_ref, ca_ref):
        ca = _silu(c_ref[...]).astype(BF16)
        mod_ref[...] = _dot(ca, w_ref[...]) + b_ref[...]
        ca_ref[...] = ca

    return _pc(body, grid=(N // tn,),
               in_specs=[pl.BlockSpec((rows, D), lambda j: (0, 0)), pl.BlockSpec((D, tn), lambda j: (0, j)),
                         pl.BlockSpec((1, tn), lambda j: (0, j))],
               out_specs=[pl.BlockSpec((rows, tn), lambda j: (0, j)), pl.BlockSpec((rows, D), lambda j: (0, 0))],
               out_shape=[SDS((rows, N), F32), SDS((rows, D), BF16)], name=name)(c8, ada_w, ada_b)


def _row_tile(S):
    return _tile(S, 512, 8)


def _norm_mod(x, g, mod3, sc_seg, sh_seg, S, name):
    T, D = x.shape
    tm = _row_tile(S)
    tpb = S // tm

    def body(x_ref, g_ref, sc_ref, sh_ref, h_ref):
        x_ = x_ref[...]
        r = lax.rsqrt(jnp.mean(x_ * x_, axis=-1, keepdims=True) + EPS)
        h_ref[...] = ((x_ * r) * g_ref[...] * (1.0 + sc_ref[...]) + sh_ref[...]).astype(BF16)

    return _pc(body, grid=(T // tm,),
               in_specs=[pl.BlockSpec((tm, D), lambda i: (i, 0)), pl.BlockSpec((1, D), lambda i: (0, 0)),
                         pl.BlockSpec((None, 1, D), lambda i: (i // tpb, 0, sc_seg)),
                         pl.BlockSpec((None, 1, D), lambda i: (i // tpb, 0, sh_seg))],
               out_specs=pl.BlockSpec((tm, D), lambda i: (i, 0)), out_shape=SDS((T, D), BF16), name=name)(x, g, mod3, mod3)


def _resid_post(x, fo, mod3, gt_seg, pg, S, name):
    T, D = x.shape
    tm = _row_tile(S)
    tpb = S // tm

    def body(x_ref, f_ref, gt_ref, pg_ref, o_ref):
        f = f_ref[...]
        r = lax.rsqrt(jnp.mean(f * f, axis=-1, keepdims=True) + EPS)
        o_ref[...] = x_ref[...] + gt_ref[...] * ((f * r) * pg_ref[...])

    return _pc(body, grid=(T // tm,),
               in_specs=[pl.BlockSpec((tm, D), lambda i: (i, 0)), pl.BlockSpec((tm, D), lambda i: (i, 0)),
                         pl.BlockSpec((None, 1, D), lambda i: (i // tpb, 0, gt_seg)),
                         pl.BlockSpec((1, D), lambda i: (0, 0))],
               out_specs=pl.BlockSpec((tm, D), lambda i: (i, 0)), out_shape=SDS((T, D), F32), name=name)(x, fo, mod3, pg)


def _post_bwd(fo, mod3, gt_seg, pg, dout, S, name):
    T, D = fo.shape
    tm = _row_tile(S)
    tpb = S // tm
    nb = T // S

    def body(f_ref, gt_ref, pg_ref, d_ref, df_ref, dgt_ref, dpg_ref):
        i = pl.program_id(0)

        @pl.when(i == 0)
        def _():
            dpg_ref[...] = jnp.zeros_like(dpg_ref)

        @pl.when(i % tpb == 0)
        def _():
            dgt_ref[...] = jnp.zeros_like(dgt_ref)

        f, d, gt, pg_ = f_ref[...], d_ref[...], gt_ref[...], pg_ref[...]
        r = lax.rsqrt(jnp.mean(f * f, axis=-1, keepdims=True) + EPS)
        n = f * r
        dn = d * gt * pg_
        df_ref[...] = (r * (dn - n * jnp.mean(dn * n, axis=-1, keepdims=True))).astype(df_ref.dtype)
        dn_ = d * n
        dgt_ref[...] += _bsum(jnp.sum(dn_ * pg_, axis=0, keepdims=True))
        dpg_ref[...] += _bsum(jnp.sum(dn_ * gt, axis=0, keepdims=True))

    return _pc(body, grid=(T // tm,),
               in_specs=[pl.BlockSpec((tm, D), lambda i: (i, 0)),
                         pl.BlockSpec((None, 1, D), lambda i: (i // tpb, 0, gt_seg)),
                         pl.BlockSpec((1, D), lambda i: (0, 0)), pl.BlockSpec((tm, D), lambda i: (i, 0))],
               out_specs=[pl.BlockSpec((tm, D), lambda i: (i, 0)), pl.BlockSpec((8, D), lambda i: (i // tpb, 0)),
                          pl.BlockSpec((8, D), lambda i: (0, 0))],
               out_shape=[SDS((T, D), BF16), SDS((nb * 8, D), F32), SDS((8, D), F32)], name=name)(fo, mod3, pg, dout)


def _pre_bwd(x, g, mod3, sc_seg, dh, dout, S, name):
    T, D = x.shape
    tm = _row_tile(S)
    tpb = S // tm
    nb = T // S

    def body(x_ref, g_ref, sc_ref, dh_ref, d_ref, dx_ref, dg_ref, dsc_ref, dsh_ref):
        i = pl.program_id(0)

        @pl.when(i == 0)
        def _():
            dg_ref[...] = jnp.zeros_like(dg_ref)

        @pl.when(i % tpb == 0)
        def _():
            dsc_ref[...] = jnp.zeros_like(dsc_ref)
            dsh_ref[...] = jnp.zeros_like(dsh_ref)

        x_, g_, dh_ = x_ref[...], g_ref[...], dh_ref[...]
        one_sc = 1.0 + sc_ref[...]
        r = lax.rsqrt(jnp.mean(x_ * x_, axis=-1, keepdims=True) + EPS)
        n = x_ * r
        dn = dh_ * g_ * one_sc
        dx_ref[...] = d_ref[...] + r * (dn - n * jnp.mean(dn * n, axis=-1, keepdims=True))
        dhn = dh_ * n
        dg_ref[...] += _bsum(jnp.sum(dhn * one_sc, axis=0, keepdims=True))
        dsc_ref[...] += _bsum(jnp.sum(dhn * g_, axis=0, keepdims=True))
        dsh_ref[...] += _bsum(jnp.sum(dh_, axis=0, keepdims=True))

    row = pl.BlockSpec((tm, D), lambda i: (i, 0))
    return _pc(body, grid=(T // tm,),
               in_specs=[row, pl.BlockSpec((1, D), lambda i: (0, 0)),
                         pl.BlockSpec((None, 1, D), lambda i: (i // tpb, 0, sc_seg)), row, row],
               out_specs=[row, pl.BlockSpec((8, D), lambda i: (0, 0)), pl.BlockSpec((8, D), lambda i: (i // tpb, 0)),
                          pl.BlockSpec((8, D), lambda i: (i // tpb, 0))],
               out_shape=[SDS((T, D), F32), SDS((8, D), F32), SDS((nb * 8, D), F32), SDS((nb * 8, D), F32)],
               name=name)(x, g, mod3, dh, dout)


def _loss(y, target, S, name):
    T, D = y.shape
    tm = _row_tile(S)

    def body(y_ref, t_ref, dy_ref, l_ref):
        @pl.when(pl.program_id(0) == 0)
        def _():
            l_ref[...] = jnp.zeros_like(l_ref)

        e = y_ref[...] - t_ref[...]
        dy_ref[...] = e * (1.0 / D)
        l_ref[...] += jnp.broadcast_to(jnp.sum(e * e, keepdims=True) * (0.5 / D), l_ref.shape)

    row = pl.BlockSpec((tm, D), lambda i: (i, 0))
    return _pc(body, grid=(T // tm,), in_specs=[row, row],
               out_specs=[row, pl.BlockSpec((8, 128), lambda i: (0, 0))],
               out_shape=[SDS((T, D), F32), SDS((8, 128), F32)], name=name)(y, target)


def _conv_geom(view, C, S):
    arr, off = view
    T = arr.shape[0]
    tm = _row_tile(S)
    tc = _tile(C, 512)
    assert off % tc == 0 and C % tc == 0
    return arr, off // tc, T, tm, tc, S // tm


def _prev_spec(tm, tc, ob, order):
    if order == "ij":
        return pl.BlockSpec((8, tc), lambda i, j: (jnp.maximum(i * (tm // 8) - 1, 0), ob + j))
    return pl.BlockSpec((8, tc), lambda j, i: (jnp.maximum(i * (tm // 8) - 1, 0), ob + j))


def _next_spec(T, tm, tc, ob, order):
    last = T // 8 - 1
    if order == "ij":
        return pl.BlockSpec((8, tc), lambda i, j: (jnp.minimum((i + 1) * (tm // 8), last), ob + j))
    return pl.BlockSpec((8, tc), lambda j, i: (jnp.minimum((i + 1) * (tm // 8), last), ob + j))


def _shifted(buf, cur, prev, tm):
    buf[0:8, :] = prev
    buf[8:, :] = cur
    return lambda j: buf[pl.ds(8 - j, tm), :]


def _conv_fwd(view, C, w8, b, K, S, name):
    arr, ob, T, tm, tc, tps = _conv_geom(view, C, S)

    def body(u_ref, p_ref, w_ref, b_ref, o_ref, buf):
        first = (pl.program_id(0) % tps) == 0
        u = u_ref[...]
        sh = _shifted(buf, u, jnp.where(first, 0.0, p_ref[...]), tm)
        acc = u * w_ref[K - 1:K, :] + b_ref[...]
        for j in range(1, K):
            acc = acc + sh(j) * w_ref[K - 1 - j:K - j, :]
        o_ref[...] = acc

    return _pc(body, grid=(T // tm, C // tc),
               in_specs=[pl.BlockSpec((tm, tc), lambda i, j: (i, ob + j)), _prev_spec(tm, tc, ob, "ij"),
                         pl.BlockSpec((8, tc), lambda i, j: (0, j)), pl.BlockSpec((1, tc), lambda i, j: (0, j))],
               out_specs=pl.BlockSpec((tm, tc), lambda i, j: (i, j)), out_shape=SDS((T, C), F32), name=name,
               scratch=(pltpu.VMEM((tm + 8, tc), F32),))(arr, arr, w8, b)


def _conv_bwd_in(dview, C, w8, K, S, out_dtype, name):
    arr, ob, T, tm, tc, tps = _conv_geom(dview, C, S)

    def body(d_ref, n_ref, w_ref, o_ref, buf):
        last = (pl.program_id(0) % tps) == tps - 1
        d = d_ref[...]
        buf[0:tm, :] = d
        buf[tm:tm + 8, :] = jnp.where(last, 0.0, n_ref[...])
        acc = d * w_ref[K - 1:K, :]
        for j in range(1, K):
            acc = acc + buf[pl.ds(j, tm), :] * w_ref[K - 1 - j:K - j, :]
        o_ref[...] = acc.astype(o_ref.dtype)

    return _pc(body, grid=(T // tm, C // tc),
               in_specs=[pl.BlockSpec((tm, tc), lambda i, j: (i, ob + j)), _next_spec(T, tm, tc, ob, "ij"),
                         pl.BlockSpec((8, tc), lambda i, j: (0, j))],
               out_specs=pl.BlockSpec((tm, tc), lambda i, j: (i, j)), out_shape=SDS((T, C), out_dtype), name=name,
               scratch=(pltpu.VMEM((tm + 8, tc), F32),))(arr, arr, w8)


def _conv_bwd_w(dview, uview, C, K, S, name):
    darr, dob, T, tm, tc, tps = _conv_geom(dview, C, S)
    uarr, uob, _, _, _, _ = _conv_geom(uview, C, S)

    def body(d_ref, u_ref, p_ref, o_ref, buf):
        i = pl.program_id(1)

        @pl.when(i == 0)
        def _():
            o_ref[...] = jnp.zeros_like(o_ref)

        first = (i % tps) == 0
        d, u = d_ref[...], u_ref[...]
        sh = _shifted(buf, u, jnp.where(first, 0.0, p_ref[...]), tm)
        rows = []
        for k in range(K):
            j = K - 1 - k
            rows.append(jnp.sum(d * (u if j == 0 else sh(j)), axis=0, keepdims=True))
        rows.append(jnp.sum(d, axis=0, keepdims=True))
        rows.append(jnp.zeros((8 - K - 1, tc), F32))
        o_ref[...] += jnp.concatenate(rows, axis=0)

    return _pc(body, grid=(C // tc, T // tm),
               in_specs=[pl.BlockSpec((tm, tc), lambda j, i: (i, dob + j)),
                         pl.BlockSpec((tm, tc), lambda j, i: (i, uob + j)), _prev_spec(tm, tc, uob, "ji")],
               out_specs=pl.BlockSpec((8, tc), lambda j, i: (0, j)), out_shape=SDS((8, C), F32), name=name,
               scratch=(pltpu.VMEM((tm + 8, tc), F32),))(darr, uarr, uarr)


def _glu_fwd(u, S, name):
    T, F2 = u.shape
    tm = _row_tile(S)
    gw = GLU_W

    def body(u_ref, a_ref):
        u_ = u_ref[...]
        a_ref[...] = (_silu(u_[:, :gw]) * u_[:, gw:]).astype(BF16)

    return _pc(body, grid=(T // tm, F2 // (2 * gw)), in_specs=[pl.BlockSpec((tm, 2 * gw), lambda i, j: (i, j))],
               out_specs=pl.BlockSpec((tm, gw), lambda i, j: (i, j)), out_shape=SDS((T, F2 // 2), BF16), name=name)(u)


def _glu_bwd(u, da, S, name):
    T, F2 = u.shape
    tm = _row_tile(S)
    gw = GLU_W

    def body(u_ref, da_ref, du_ref):
        u_ = u_ref[...]
        g, v, da_ = u_[:, :gw], u_[:, gw:], da_ref[...].astype(F32)
        du_ref[...] = jnp.concatenate([da_ * v * _dsilu(g), da_ * _silu(g)], axis=1)

    return _pc(body, grid=(T // tm, F2 // (2 * gw)),
               in_specs=[pl.BlockSpec((tm, 2 * gw), lambda i, j: (i, j)), pl.BlockSpec((tm, gw), lambda i, j: (i, j))],
               out_specs=pl.BlockSpec((tm, 2 * gw), lambda i, j: (i, j)), out_shape=SDS((T, F2), F32), name=name)(u, da)


def _ssd_common(dtc_raw, dtr_raw, hpc, hpr, L):
    dt_c = _softplus(dtc_raw + hpc[0:1, :])
    a_c = -jnp.exp(hpc[1:2, :])
    dt_r = _softplus(dtr_raw + hpr[:, 0:1])
    a_r = -jnp.exp(hpr[:, 1:2])
    li = lax.broadcasted_iota(jnp.int32, (L, L), 0)
    si = lax.broadcasted_iota(jnp.int32, (L, L), 1)
    low = li >= si
    upp = li <= si
    acs_c = jnp.dot(low.astype(F32), dt_c * a_c, preferred_element_type=F32, precision=HIGHEST)
    acs_r = jnp.dot(dt_r * a_r, upp.astype(F32), preferred_element_type=F32, precision=HIGHEST)
    return dt_c, a_c, acs_c, acs_r, low, upp


def _ssd_specs(pre, off_x, off_b, off_c, G, R, P, nb, nc, rev):
    L, N, RP = CHUNK, N_STATE, R * P
    cidx = (lambda c: nc - 1 - c) if rev else (lambda c: c)
    xb, bb, cb = off_x // RP, off_b // N, off_c // N
    assert off_x % RP == 0 and off_b % N == 0 and off_c % N == 0
    row = lambda b, c: b * nc + cidx(c)
    return dict(
        x=pl.BlockSpec((L, RP), lambda g, b, c: (row(b, c), xb + g)),
        b=pl.BlockSpec((L, N), lambda g, b, c: (row(b, c), bb + g)),
        c=pl.BlockSpec((L, N), lambda g, b, c: (row(b, c), cb + g)),
        dtc=pl.BlockSpec((None, L, R), lambda g, b, c: (g, row(b, c), 0)),
        dtr=pl.BlockSpec((None, R, L), lambda g, b, c: (g, 0, row(b, c))),
        hpc=pl.BlockSpec((None, 8, R), lambda g, b, c: (g, 0, 0)),
        hpr=pl.BlockSpec((None, R, 8), lambda g, b, c: (g, 0, 0)),
        y=pl.BlockSpec((L, RP), lambda g, b, c: (row(b, c), g)),
        bc=pl.BlockSpec((L, N), lambda g, b, c: (row(b, c), g)),
        hs=pl.BlockSpec((None, None, N, RP), lambda g, b, c: (row(b, c), g, 0, 0)),
    )


def _ssd_fwd(pre, offs, dtc, dtr, hpc, hpr, G, R, P, S, name):
    T = pre.shape[0]
    L, N, RP = CHUNK, N_STATE, R * P
    nc, nb = S // L, T // S
    sp = _ssd_specs(pre, *offs, G, R, P, nb, nc, False)

    def body(px_ref, pb_ref, pc_ref, dtc_ref, dtr_ref, hpc_ref, hpr_ref, y_ref, hs_ref, hst):
        @pl.when(pl.program_id(2) == 0)
        def _():
            hst[...] = jnp.zeros_like(hst)

        xs, bm, cm = _silu(px_ref[...]), _silu(pb_ref[...]), _silu(pc_ref[...])
        hpc_ = hpc_ref[...]
        dt_c, _, acs_c, acs_r, low, _ = _ssd_common(dtc_ref[...], dtr_ref[...], hpc_, hpr_ref[...], L)
        bb, cb = bm.astype(BF16), cm.astype(BF16)
        gm = _dot(cb, bb, NT)
        hprev = hst[...]
        hs_ref[...] = hprev.astype(BF16)
        ys, hn = [], []
        for r in range(R):
            col, row = acs_c[:, r:r + 1], acs_r[r:r + 1, :]
            dec = jnp.exp(jnp.where(low, col - row, -jnp.inf))
            x_r = xs[:, r * P:(r + 1) * P]
            xdt = (x_r * dt_c[:, r:r + 1]).astype(BF16)
            h_r = hprev[:, r * P:(r + 1) * P]
            y_r = _dot((gm * dec).astype(BF16), xdt)
            y_r = y_r + _dot((cm * jnp.exp(col)).astype(BF16), h_r.astype(BF16))
            ys.append(y_r + hpc_[2:3, r:r + 1] * x_r)
            a_last = acs_c[L - 1:L, r:r + 1]
            bw = (bm * jnp.exp(a_last - col)).astype(BF16)
            hn.append(h_r * jnp.exp(a_last) + _dot(bw, xdt, TN))
        y_ref[...] = jnp.concatenate(ys, axis=1)
        hst[...] = jnp.concatenate(hn, axis=1)

    return _pc(body, grid=(G, nb, nc),
               in_specs=[sp["x"], sp["b"], sp["c"], sp["dtc"], sp["dtr"], sp["hpc"], sp["hpr"]],
               out_specs=[sp["y"], sp["hs"]],
               out_shape=[SDS((T, G * RP), F32), SDS((nb * nc, G, N, RP), BF16)], name=name,
               scratch=(pltpu.VMEM((N, RP), F32),))(pre, pre, pre, dtc, dtr, hpc, hpr)


def _ssd_bwd(pre, offs, dtc, dtr, hpc, hpr, hs, dy, G, R, P, S, name):
    T = pre.shape[0]
    L, N, RP = CHUNK, N_STATE, R * P
    nc, nb = S // L, T // S
    sp = _ssd_specs(pre, *offs, G, R, P, nb, nc, True)

    def body(px_ref, pb_ref, pc_ref, dtc_ref, dtr_ref, hpc_ref, hpr_ref, hs_ref, dy_ref,
             dpx_ref, dpb_ref, dpc_ref, ddt_ref, hpg_ref, dhst):
        bi, ci = pl.program_id(1), pl.program_id(2)

        @pl.when(ci == 0)
        def _():
            dhst[...] = jnp.zeros_like(dhst)

        @pl.when((bi == 0) & (ci == 0))
        def _():
            hpg_ref[...] = jnp.zeros_like(hpg_ref)

        px, pb, pcc = px_ref[...], pb_ref[...], pc_ref[...]
        xs, bm, cm = _silu(px), _silu(pb), _silu(pcc)
        hpc_ = hpc_ref[...]
        dtc_raw = dtc_ref[...]
        dt_c, a_c, acs_c, acs_r, low, upp = _ssd_common(dtc_raw, dtr_ref[...], hpc_, hpr_ref[...], L)
        bb, cb = bm.astype(BF16), cm.astype(BF16)
        gm = _dot(cb, bb, NT)
        gmt = _dot(bb, cb, NT)
        hprev = hs_ref[...]
        dhn = dhst[...]
        dy = dy_ref[...]
        rowi = lax.broadcasted_iota(jnp.int32, (L, 1), 0)
        dg = jnp.zeros((L, L), F32)
        dgt = jnp.zeros((L, L), F32)
        dcm = jnp.zeros((L, N), F32)
        dbm = jnp.zeros((L, N), F32)
        dxs, dhp, da_cols, ddt_cols, dd_cols = [], [], [], [], []
        for r in range(R):
            col, row = acs_c[:, r:r + 1], acs_r[r:r + 1, :]
            dec = jnp.exp(jnp.where(low, col - row, -jnp.inf))
            dect = jnp.exp(jnp.where(upp, row - col, -jnp.inf))
            x_r = xs[:, r * P:(r + 1) * P]
            dt_col = dt_c[:, r:r + 1]
            xdt = (x_r * dt_col).astype(BF16)
            dy_r = dy[:, r * P:(r + 1) * P]
            dyb = dy_r.astype(BF16)
            h_r = hprev[:, r * P:(r + 1) * P]
            dhn_r = dhn[:, r * P:(r + 1) * P]
            dhnb = dhn_r.astype(BF16)
            m, mt = gm * dec, gmt * dect
            dm = _dot(dyb, xdt, NT)
            dmt = _dot(xdt, dyb, NT)
            dxdt = _dot(mt.astype(BF16), dyb)
            da = jnp.sum(dm * m, axis=1, keepdims=True) - jnp.sum(dmt * mt, axis=1, keepdims=True)
            dg = dg + dm * dec
            dgt = dgt + dmt * dect
            ecol = jnp.exp(col)
            ce = (cm * ecol).astype(BF16)
            dcm = dcm + _dot(dyb, h_r, NT) * ecol
            da = da + jnp.sum(dy_r * _dot(ce, h_r), axis=1, keepdims=True)
            dh_r = _dot(ce, dyb, TN)
            a_last = acs_c[L - 1:L, r:r + 1]
            e_last = jnp.exp(a_last)
            dh_r = dh_r + e_last * dhn_r
            da_last = e_last * jnp.sum(dhn_r * h_r.astype(F32), keepdims=True)
            w = jnp.exp(a_last - col)
            bw = bm * w
            dxdt = dxdt + _dot(bw.astype(BF16), dhnb)
            dbw = _dot(xdt, dhnb, NT)
            dbm = dbm + dbw * w
            q = jnp.sum(dbw * bw, axis=1, keepdims=True)
            da = da - q
            da_last = da_last + jnp.sum(q, keepdims=True)
            da = da + jnp.where(rowi == L - 1, da_last, 0.0)
            d_r = hpc_[2:3, r:r + 1]
            dxs.append(dxdt * dt_col + d_r * dy_r)
            ddt_cols.append(jnp.sum(dxdt * x_r, axis=1, keepdims=True))
            dd_cols.append(jnp.sum(dy_r * x_r, keepdims=True))
            da_cols.append(da)
            dhp.append(dh_r)
        dhst[...] = jnp.concatenate(dhp, axis=1)
        dcm = dcm + _dot(dg.astype(BF16), bb)
        dbm = dbm + _dot(dgt.astype(BF16), cb)
        dpx_ref[...] = jnp.concatenate(dxs, axis=1) * _dsilu(px)
        dpb_ref[...] = dbm * _dsilu(pb)
        dpc_ref[...] = dcm * _dsilu(pcc)
        da_all = jnp.concatenate(da_cols, axis=1)
        dadt = jnp.dot(upp.astype(F32), da_all, preferred_element_type=F32, precision=HIGHEST)
        ddt = jnp.concatenate(ddt_cols, axis=1) + dadt * a_c
        ddt_raw = ddt * jax.nn.sigmoid(dtc_raw + hpc_[0:1, :])
        ddt_ref[...] = ddt_raw
        d_a = jnp.sum(dadt * dt_c, axis=0, keepdims=True)
        rows = [jnp.sum(ddt_raw, axis=0, keepdims=True), d_a * a_c, jnp.concatenate(dd_cols, axis=1),
                jnp.zeros((5, R), F32)]
        hpg_ref[...] += jnp.concatenate(rows, axis=0)

    return _pc(body, grid=(G, nb, nc),
               in_specs=[sp["x"], sp["b"], sp["c"], sp["dtc"], sp["dtr"], sp["hpc"], sp["hpr"], sp["hs"], sp["y"]],
               out_specs=[sp["y"], sp["bc"], sp["bc"], sp["dtc"], pl.BlockSpec((None, 8, R), lambda g, b, c: (g, 0, 0))],
               out_shape=[SDS((T, G * RP), F32), SDS((T, G * N), F32), SDS((T, G * N), F32), SDS((G, T, R), F32),
                          SDS((G, 8, R), F32)], name=name,
               scratch=(pltpu.VMEM((N, RP), F32),))(pre, pre, pre, dtc, dtr, hpc, hpr, hs, dy)


def _gate_norm_fwd(y, zview, ng, G, S, name):
    T, DI = y.shape
    zarr, zoff = zview
    gw = DI // G
    tm = _row_tile(S)
    zb = zoff // gw
    assert zoff % gw == 0

    def body(y_ref, z_ref, g_ref, o_ref):
        yg = y_ref[...] * _silu(z_ref[...])
        r = lax.rsqrt(jnp.mean(yg * yg, axis=-1, keepdims=True) + EPS)
        o_ref[...] = (yg * r * g_ref[...]).astype(BF16)

    return _pc(body, grid=(T // tm, G),
               in_specs=[pl.BlockSpec((tm, gw), lambda i, g: (i, g)), pl.BlockSpec((tm, gw), lambda i, g: (i, zb + g)),
                         pl.BlockSpec((1, gw), lambda i, g: (0, g))],
               out_specs=pl.BlockSpec((tm, gw), lambda i, g: (i, g)), out_shape=SDS((T, DI), BF16), name=name)(y, zarr, ng)


def _gate_norm_bwd(y, zview, ng, dyn, G, S, name):
    T, DI = y.shape
    zarr, zoff = zview
    gw = DI // G
    tm = _row_tile(S)
    zb = zoff // gw

    def body(y_ref, z_ref, g_ref, d_ref, dy_ref, dz_ref, dg_ref):
        @pl.when(pl.program_id(1) == 0)
        def _():
            dg_ref[...] = jnp.zeros_like(dg_ref)

        y_, z, d = y_ref[...], z_ref[...], d_ref[...]
        sz = _silu(z)
        yg = y_ * sz
        r = lax.rsqrt(jnp.mean(yg * yg, axis=-1, keepdims=True) + EPS)
        n = yg * r
        dn = d * g_ref[...]
        dyg = r * (dn - n * jnp.mean(dn * n, axis=-1, keepdims=True))
        dy_ref[...] = dyg * sz
        dz_ref[...] = (dyg * y_ * _dsilu(z)).astype(BF16)
        dg_ref[...] += _bsum(jnp.sum(d * n, axis=0, keepdims=True))

    return _pc(body, grid=(G, T // tm),
               in_specs=[pl.BlockSpec((tm, gw), lambda g, i: (i, g)), pl.BlockSpec((tm, gw), lambda g, i: (i, zb + g)),
                         pl.BlockSpec((1, gw), lambda g, i: (0, g)), pl.BlockSpec((tm, gw), lambda g, i: (i, g))],
               out_specs=[pl.BlockSpec((tm, gw), lambda g, i: (i, g)), pl.BlockSpec((tm, gw), lambda g, i: (i, g)),
                          pl.BlockSpec((8, gw), lambda g, i: (0, g))],
               out_shape=[SDS((T, DI), F32), SDS((T, DI), BF16), SDS((8, DI), F32)], name=name)(y, zarr, ng, dyn)


def _shortconv_fwd(proj, off_b, off_c, off_h, C, w8, S, name):
    K = SC_CONV_K
    _, ob, T, tm, tc, tps = _conv_geom((proj, off_b), C, S)
    oc, oh = off_c // tc, off_h // tc

    def body(b_ref, c_ref, h_ref, cp_ref, hp_ref, w_ref, o_ref, buf):
        first = (pl.program_id(0) % tps) == 0
        v = c_ref[...] * h_ref[...]
        sh = _shifted(buf, v, jnp.where(first, 0.0, cp_ref[...] * hp_ref[...]), tm)
        acc = v * w_ref[K - 1:K, :]
        for j in range(1, K):
            acc = acc + sh(j) * w_ref[K - 1 - j:K - j, :]
        o_ref[...] = (b_ref[...] * acc).astype(BF16)

    blk = lambda o: pl.BlockSpec((tm, tc), lambda i, j: (i, o + j))
    return _pc(body, grid=(T // tm, C // tc),
               in_specs=[blk(ob), blk(oc), blk(oh), _prev_spec(tm, tc, oc, "ij"), _prev_spec(tm, tc, oh, "ij"),
                         pl.BlockSpec((8, tc), lambda i, j: (0, j))],
               out_specs=pl.BlockSpec((tm, tc), lambda i, j: (i, j)), out_shape=SDS((T, C), BF16), name=name,
               scratch=(pltpu.VMEM((tm + 8, tc), F32),))(proj, proj, proj, proj, proj, w8)


def _shortconv_bwd(proj, off_b, off_c, off_h, C, w8, dsc, S, name):
    K = SC_CONV_K
    _, ob, T, tm, tc, tps = _conv_geom((proj, off_b), C, S)
    oc, oh = off_c // tc, off_h // tc

    def body(b_ref, c_ref, h_ref, cp_ref, hp_ref, bn_ref, d_ref, dn_ref, w_ref,
             db_ref, dc_ref, dh_ref, dw_ref, buf, buf2):
        i = pl.program_id(1)

        @pl.when(i == 0)
        def _():
            dw_ref[...] = jnp.zeros_like(dw_ref)

        first = (i % tps) == 0
        last = (i % tps) == tps - 1
        b_, c_, h_, d = b_ref[...], c_ref[...], h_ref[...], d_ref[...]
        v = c_ * h_
        sh = _shifted(buf, v, jnp.where(first, 0.0, cp_ref[...] * hp_ref[...]), tm)
        vs = [v] + [sh(j) for j in range(1, K)]
        conv = vs[0] * w_ref[K - 1:K, :]
        for j in range(1, K):
            conv = conv + vs[j] * w_ref[K - 1 - j:K - j, :]
        db_ref[...] = (d * conv).astype(BF16)
        dconv = d * b_
        buf2[0:tm, :] = dconv
        buf2[tm:tm + 8, :] = jnp.where(last, 0.0, dn_ref[...] * bn_ref[...])
        dv = dconv * w_ref[K - 1:K, :]
        for j in range(1, K):
            dv = dv + buf2[pl.ds(j, tm), :] * w_ref[K - 1 - j:K - j, :]
        dc_ref[...] = (dv * h_).astype(BF16)
        dh_ref[...] = (dv * c_).astype(BF16)
        rows = [jnp.sum(dconv * vs[K - 1 - k], axis=0, keepdims=True) for k in range(K)]
        rows.append(jnp.zeros((8 - K, tc), F32))
        dw_ref[...] += jnp.concatenate(rows, axis=0)

    blk = lambda o: pl.BlockSpec((tm, tc), lambda j, i: (i, o + j))
    out = pl.BlockSpec((tm, tc), lambda j, i: (i, j))
    return _pc(body, grid=(C // tc, T // tm),
               in_specs=[blk(ob), blk(oc), blk(oh), _prev_spec(tm, tc, oc, "ji"), _prev_spec(tm, tc, oh, "ji"),
                         _next_spec(T, tm, tc, ob, "ji"), blk(0), _next_spec(T, tm, tc, 0, "ji"),
                         pl.BlockSpec((8, tc), lambda j, i: (0, j))],
               out_specs=[out, out, out, pl.BlockSpec((8, tc), lambda j, i: (0, j))],
               out_shape=[SDS((T, C), BF16)] * 3 + [SDS((8, C), F32)], name=name,
               scratch=(pltpu.VMEM((tm + 8, tc), F32), pltpu.VMEM((tm + 8, tc), F32)))(
                   proj, proj, proj, proj, proj, proj, dsc, dsc, w8)


def _merge_fwd(proj, off_g1, off_g2, y1, y2, S, name):
    T, D = y1.shape
    tm = _row_tile(S)
    o1, o2 = off_g1 // D, off_g2 // D
    assert off_g1 % D == 0 and off_g2 % D == 0

    def body(g1_ref, g2_ref, y1_ref, y2_ref, o_ref):
        o_ref[...] = (jax.nn.sigmoid(g1_ref[...]) * y1_ref[...] + jax.nn.sigmoid(g2_ref[...]) * y2_ref[...]).astype(BF16)

    row = pl.BlockSpec((tm, D), lambda i: (i, 0))
    return _pc(body, grid=(T // tm,),
               in_specs=[pl.BlockSpec((tm, D), lambda i: (i, o1)), pl.BlockSpec((tm, D), lambda i: (i, o2)), row, row],
               out_specs=row, out_shape=SDS((T, D), BF16), name=name)(proj, proj, y1, y2)


def _merge_bwd(proj, off_g1, off_g2, y1, y2, dm, S, name):
    T, D = y1.shape
    tm = _row_tile(S)
    o1, o2 = off_g1 // D, off_g2 // D

    def body(g1_ref, g2_ref, y1_ref, y2_ref, d_ref, dy1_ref, dy2_ref, dg1_ref, dg2_ref):
        d = d_ref[...]
        s1, s2 = jax.nn.sigmoid(g1_ref[...]), jax.nn.sigmoid(g2_ref[...])
        dy1_ref[...] = (d * s1).astype(BF16)
        dy2_ref[...] = (d * s2).astype(BF16)
        dg1_ref[...] = (d * y1_ref[...] * s1 * (1.0 - s1)).astype(BF16)
        dg2_ref[...] = (d * y2_ref[...] * s2 * (1.0 - s2)).astype(BF16)

    row = pl.BlockSpec((tm, D), lambda i: (i, 0))
    return _pc(body, grid=(T // tm,),
               in_specs=[pl.BlockSpec((tm, D), lambda i: (i, o1)), pl.BlockSpec((tm, D), lambda i: (i, o2)), row, row, row],
               out_specs=[row] * 4, out_shape=[SDS((T, D), BF16)] * 4, name=name)(proj, proj, y1, y2, dm)


def _pad8(w):
    return jnp.pad(w, ((0, 8 - w.shape[0]), (0, 0)))


def _dims(w):
    D = w["mix_pre_g"].shape[-1]
    DI = w["ssd_norm_g"].shape[-1]
    H = w["ssd_dt_bias"].shape[-1]
    conv_dim = w["ssd_conv_b"].shape[-1]
    G = (conv_dim - DI) // (2 * N_STATE)
    F = w["w_down"].shape[0]
    return dict(D=D, DI=DI, H=H, P=DI // H, G=G, R=H // G, GN=G * N_STATE, CD=conv_dim, F=F)


def _proj_layout(d):
    D, DI, CD, H = d["D"], d["DI"], d["CD"], d["H"]
    o = dict(z=0, xbc=DI, scb=DI + CD, scc=DI + CD + D, sch=DI + CD + 2 * D, g1=DI + CD + 3 * D, g2=DI + CD + 4 * D,
             dt=DI + CD + 5 * D)
    used = o["dt"] + H
    o["np"] = -(-used // 128) * 128
    return o


def _glu_perm(a, F, inverse=False):
    lead = a.shape[:-1]
    nb = F // GLU_W
    if not inverse:
        return a.reshape(*lead, 2, nb, GLU_W).swapaxes(-3, -2).reshape(*lead, 2 * F)
    return a.reshape(*lead, nb, 2, GLU_W).swapaxes(-3, -2).reshape(*lead, 2 * F)


def _prep_layer(w):
    d = _dims(w)
    D, DI, CD, H, G, R, F = d["D"], d["DI"], d["CD"], d["H"], d["G"], d["R"], d["F"]
    lay = _proj_layout(d)
    w_in = w["w_in"]
    z, xbc, dt, scb, scc, sch, gates = jnp.split(
        w_in, [DI, DI + CD, DI + CD + H, DI + CD + H + D, DI + CD + H + 2 * D, DI + CD + H + 3 * D], axis=1)
    used = lay["dt"] + H
    wcat = jnp.concatenate([z, xbc, scb, scc, sch, gates, dt, jnp.zeros((D, lay["np"] - used), w_in.dtype)], axis=1)
    hp = jnp.stack([w["ssd_dt_bias"], w["ssd_a_log"], w["ssd_d"]], 0).astype(F32)
    hpc = jnp.pad(hp.reshape(3, G, R).transpose(1, 0, 2), ((0, 0), (0, 5), (0, 0)))
    hpr = jnp.pad(hp[:2].reshape(2, G, R).transpose(1, 2, 0), ((0, 0), (0, 0), (0, 6)))
    row = lambda v: v.reshape(1, -1).astype(F32)
    return dict(
        d=d, lay=lay, ada_w=w["ada_w"].astype(BF16), ada_b=row(w["ada_b"]),
        mix_pre_g=row(w["mix_pre_g"]), mix_post_g=row(w["mix_post_g"]), wcat=wcat.astype(BF16),
        ssd_conv_w=_pad8(w["ssd_conv_w"].astype(F32)), ssd_conv_b=row(w["ssd_conv_b"]), hpc=hpc, hpr=hpr,
        ssd_norm_g=row(w["ssd_norm_g"]), w_ssd_out=w["w_ssd_out"].astype(BF16),
        sc_conv_w=_pad8(w["sc_conv_w"].astype(F32)), w_sc_out=w["w_sc_out"].astype(BF16), w_o=w["w_o"].astype(BF16),
        ffn_pre_g=row(w["ffn_pre_g"]), ffn_post_g=row(w["ffn_post_g"]),
        w_up=_glu_perm(w["w_up"], F).astype(BF16), ffn_conv_w=_pad8(_glu_perm(w["ffn_conv_w"].astype(F32), F)),
        ffn_conv_b=_glu_perm(row(w["ffn_conv_b"]), F), w_down=w["w_down"].astype(BF16))


def _dt_layouts(proj, lay, d):
    T = proj.shape[0]
    dt = proj[:, lay["dt"]:lay["dt"] + d["H"]].reshape(T, d["G"], d["R"])
    return dt.transpose(1, 0, 2), dt.transpose(1, 2, 0)


def _layer_fwd(x, mod3, p, S, li):
    d, lay = p["d"], p["lay"]
    D, DI, G, R, P, GN, CD = d["D"], d["DI"], d["G"], d["R"], d["P"], d["GN"], d["CD"]
    nm = lambda s: f"l{li}_{s}"
    h = _norm_mod(x, p["mix_pre_g"], mod3, 1, 0, S, nm("norm1"))
    proj = _mm(h, p["wcat"], "nn", F32, nm("mm_in"), caps=(1024, 1152, 2048))
    pre = _conv_fwd((proj, lay["xbc"]), CD, p["ssd_conv_w"], p["ssd_conv_b"], SSD_CONV_K, S, nm("ssdconv"))
    dtc, dtr = _dt_layouts(proj, lay, d)
    offs = (0, DI, DI + GN)
    y, hs = _ssd_fwd(pre, offs, dtc, dtr, p["hpc"], p["hpr"], G, R, P, S, nm("ssd"))
    yn = _gate_norm_fwd(y, (proj, lay["z"]), p["ssd_norm_g"], G, S, nm("gnorm"))
    sc = _shortconv_fwd(proj, lay["scb"], lay["scc"], lay["sch"], D, p["sc_conv_w"], S, nm("sconv"))
    y_ssd = _mm(yn, p["w_ssd_out"], "nn", F32, nm("mm_ssdout"))
    y_sc = _mm(sc, p["w_sc_out"], "nn", F32, nm("mm_scout"))
    m = _merge_fwd(proj, lay["g1"], lay["g2"], y_ssd, y_sc, S, nm("merge"))
    mix = _mm(m, p["w_o"], "nn", F32, nm("mm_o"))
    x1 = _resid_post(x, mix, mod3, 2, p["mix_post_g"], S, nm("post1"))
    h2 = _norm_mod(x1, p["ffn_pre_g"], mod3, 4, 3, S, nm("norm2"))
    uu = _mm(h2, p["w_up"], "nn", F32, nm("mm_up"))
    u = _conv_fwd((uu, 0), uu.shape[1], p["ffn_conv_w"], p["ffn_conv_b"], FFN_CONV_K, S, nm("ffnconv"))
    a = _glu_fwd(u, S, nm("glu"))
    f = _mm(a, p["w_down"], "nn", F32, nm("mm_down"), caps=(1024, 1024, 1408))
    x2 = _resid_post(x1, f, mod3, 5, p["ffn_post_g"], S, nm("post2"))
    saved = dict(x=x, h=h, proj=proj, pre=pre, dtc=dtc, dtr=dtr, y=y, hs=hs, yn=yn, sc=sc, y_ssd=y_ssd, y_sc=y_sc,
                 m=m, mix=mix, x1=x1, h2=h2, uu=uu, u=u, a=a, f=f)
    return x2, saved


def _seq_sum(acc, nb):
    return acc.reshape(nb, 8, -1)[:, 0, :]


def _layer_bwd(dx2, mod3, p, s, S, li):
    d, lay = p["d"], p["lay"]
    D, DI, G, R, P, GN, CD, H, F = d["D"], d["DI"], d["G"], d["R"], d["P"], d["GN"], d["CD"], d["H"], d["F"]
    nb = dx2.shape[0] // S
    nm = lambda t: f"l{li}_{t}"
    g = {}
    df, dgt2, dpg2 = _post_bwd(s["f"], mod3, 5, p["ffn_post_g"], dx2, S, nm("post2_b"))
    g["ffn_post_g"] = dpg2[0]
    da = _mm(df, p["w_down"], "nt", BF16, nm("mm_down_bi"), caps=(1024, 1408, 2048))
    g["w_down"] = _mm(s["a"], df, "tn", F32, nm("mm_down_bw"), caps=(1408, 1024, 1024))
    du = _glu_bwd(s["u"], da, S, nm("glu_b"))
    F2 = 2 * F
    cw = _conv_bwd_w((du, 0), (s["uu"], 0), F2, FFN_CONV_K, S, nm("ffnconv_bw"))
    g["ffn_conv_w"] = _glu_perm(cw[:FFN_CONV_K], F, inverse=True)
    g["ffn_conv_b"] = _glu_perm(cw[FFN_CONV_K], F, inverse=True)
    duu = _conv_bwd_in((du, 0), F2, p["ffn_conv_w"], FFN_CONV_K, S, BF16, nm("ffnconv_bi"))
    dh2 = _mm(duu, p["w_up"], "nt", F32, nm("mm_up_bi"), caps=(1024, 1024, 1408))
    g["w_up"] = _glu_perm(_mm(s["h2"], duu, "tn", F32, nm("mm_up_bw"), caps=(1024, 1408, 1024)), F, inverse=True)
    dx1, dg2, dsc2, dsh2 = _pre_bwd(s["x1"], p["ffn_pre_g"], mod3, 4, dh2, dx2, S, nm("norm2_b"))
    g["ffn_pre_g"] = dg2[0]
    dmix, dgt1, dpg1 = _post_bwd(s["mix"], mod3, 2, p["mix_post_g"], dx1, S, nm("post1_b"))
    g["mix_post_g"] = dpg1[0]
    dm = _mm(dmix, p["w_o"], "nt", F32, nm("mm_o_bi"))
    g["w_o"] = _mm(s["m"], dmix, "tn", F32, nm("mm_o_bw"))
    proj = s["proj"]
    dy_ssd, dy_sc, dg1, dg2_ = _merge_bwd(proj, lay["g1"], lay["g2"], s["y_ssd"], s["y_sc"], dm, S, nm("merge_b"))
    dyn = _mm(dy_ssd, p["w_ssd_out"], "nt", F32, nm("mm_ssdout_bi"))
    g["w_ssd_out"] = _mm(s["yn"], dy_ssd, "tn", F32, nm("mm_ssdout_bw"))
    dsc = _mm(dy_sc, p["w_sc_out"], "nt", F32, nm("mm_scout_bi"))
    g["w_sc_out"] = _mm(s["sc"], dy_sc, "tn", F32, nm("mm_scout_bw"))
    dscb, dscc, dsch, scw = _shortconv_bwd(proj, lay["scb"], lay["scc"], lay["sch"], D, p["sc_conv_w"], dsc, S, nm("sconv_b"))
    g["sc_conv_w"] = scw[:SC_CONV_K]
    dy, dz, dng = _gate_norm_bwd(s["y"], (proj, lay["z"]), p["ssd_norm_g"], dyn, G, S, nm("gnorm_b"))
    g["ssd_norm_g"] = dng[0]
    offs = (0, DI, DI + GN)
    dpx, dpb, dpc, ddt, hpg = _ssd_bwd(s["pre"], offs, s["dtc"], s["dtr"], p["hpc"], p["hpr"], s["hs"], dy,
                                       G, R, P, S, nm("ssd_b"))
    g["ssd_dt_bias"], g["ssd_a_log"], g["ssd_d"] = hpg[:, 0, :].reshape(H), hpg[:, 1, :].reshape(H), hpg[:, 2, :].reshape(H)
    cws, dxbc = [], []
    for name, darr, off, C in (("x", dpx, 0, DI), ("b", dpb, DI, GN), ("c", dpc, DI + GN, GN)):
        w8 = p["ssd_conv_w"][:, off:off + C]
        cws.append(_conv_bwd_w((darr, 0), (proj, lay["xbc"] + off), C, SSD_CONV_K, S, nm(f"ssdconv_bw_{name}")))
        dxbc.append(_conv_bwd_in((darr, 0), C, w8, SSD_CONV_K, S, BF16, nm(f"ssdconv_bi_{name}")))
    cws = jnp.concatenate(cws, axis=1)
    g["ssd_conv_w"], g["ssd_conv_b"] = cws[:SSD_CONV_K], cws[SSD_CONV_K]
    T = dx2.shape[0]
    ddt_t = ddt.transpose(1, 0, 2).reshape(T, H).astype(BF16)
    used = lay["dt"] + H
    dproj = jnp.concatenate([dz] + dxbc + [dscb, dscc, dsch, dg1, dg2_, ddt_t,
                                           jnp.zeros((T, lay["np"] - used), BF16)], axis=1)
    dh = _mm(dproj, p["wcat"], "nt", F32, nm("mm_in_bi"), caps=(1024, 1024, 1152))
    dwcat = _mm(s["h"], dproj, "tn", F32, nm("mm_in_bw"), caps=(1024, 1152, 1024))
    o = lay
    g["w_in"] = jnp.concatenate([dwcat[:, o["z"]:o["scb"]], dwcat[:, o["dt"]:o["dt"] + H], dwcat[:, o["scb"]:o["dt"]]], axis=1)
    dx, dg1_, dsc1, dsh1 = _pre_bwd(s["x"], p["mix_pre_g"], mod3, 1, dh, dx1, S, nm("norm1_b"))
    g["mix_pre_g"] = dg1_[0]
    dmod = jnp.concatenate([_seq_sum(t, nb) for t in (dsh1, dsc1, dgt1, dsh2, dsc2, dgt2)], axis=1)
    return dx, dmod, g


def _fwd_bwd(x3, c, target3, layers):
    nb, S, D = x3.shape
    T = nb * S
    x = x3.reshape(T, D)
    c8 = jnp.pad(c, ((0, MOD_ROWS - nb), (0, 0)))
    preps = [_prep_layer(w) for w in layers]
    saved, mods, cact = [], [], None
    for li, p in enumerate(preps):
        mod, cact = _modk(c8, p["ada_w"], p["ada_b"], f"l{li}_mod")
        mod3 = mod[:nb].reshape(nb, 1, 6 * D)
        x, s = _layer_fwd(x, mod3, p, S, li)
        saved.append(s)
        mods.append(mod3)
    dy, lacc = _loss(x, target3.reshape(T, D), S, "loss")
    grads = [None] * len(preps)
    for li in reversed(range(len(preps))):
        dy, dmod, g = _layer_bwd(dy, mods[li], preps[li], saved[li], S, li)
        dmod8 = jnp.pad(dmod, ((0, MOD_ROWS - nb), (0, 0)))
        g["ada_b"] = _colsum(dmod8, f"l{li}_adab")
        g["ada_w"] = _mm(cact, dmod8, "tn", F32, f"l{li}_mm_ada_bw", caps=(1024, 1536, 2048))
        grads[li] = g
    return lacc[0, 0], dy.reshape(nb, S, D), grads


def _colsum(a8, name):
    rows, C = a8.shape
    tc = _tile(C, 2048)

    def body(a_ref, o_ref):
        o_ref[...] = _bsum(jnp.sum(a_ref[...], axis=0, keepdims=True))

    return _pc(body, grid=(C // tc,), in_specs=[pl.BlockSpec((rows, tc), lambda j: (0, j))],
               out_specs=pl.BlockSpec((8, tc), lambda j: (0, j)), out_shape=SDS((8, C), F32), name=name)(a8)[0]


def _adam(gs, w, m, v, name):
    ns, R, W = gs.shape
    tr = _tile(R, 256, 8)
    c1 = 1.0 / (1.0 - ADAM_B1 ** ADAM_STEP)
    c2 = 1.0 / (1.0 - ADAM_B2 ** ADAM_STEP)

    def body(g_ref, w_ref, m_ref, v_ref, go_ref, d_ref, mo_ref, vo_ref):
        g = g_ref[0].astype(F32)
        for k in range(1, ns):
            g = g + g_ref[k].astype(F32)
        m_ = ADAM_B1 * m_ref[...] + (1.0 - ADAM_B1) * g
        v_ = ADAM_B2 * v_ref[...] + (1.0 - ADAM_B2) * (g * g)
        go_ref[...] = g
        mo_ref[...] = m_
        vo_ref[...] = v_
        d_ref[...] = -ADAM_LR * ((m_ * c1) / (jnp.sqrt(v_ * c2) + ADAM_EPS) + ADAM_WD * w_ref[...])

    row = pl.BlockSpec((tr, W), lambda i: (i, 0))
    return _pc(body, grid=(R // tr,), in_specs=[pl.BlockSpec((ns, tr, W), lambda i: (0, i, 0)), row, row, row],
               out_specs=[row] * 4, out_shape=[SDS((R, W), F32)] * 4, name=name)(gs, w, m, v)


HBM_SPEC = pl.BlockSpec(memory_space=pltpu.HBM)
VMEM_SPEC = pl.BlockSpec(memory_space=pltpu.VMEM)


def _dev():
    return lax.axis_index("x"), lax.axis_index("y"), lax.axis_index("c")


def _allgather_big(loc, name):
    R, W = loc.shape

    def body(x_ref, out_ref, send_sems, recv_sems, local_sem):
        x, y, c = _dev()
        me, sibling = (x, y, c), (x, y, 1 - c)
        chips = [(1 - x, y), (x, 1 - y), (1 - x, 1 - y)]

        def slab(px, py, pc):
            return out_ref.at[4 * px + 2 * py + pc]

        def copy(k, block, to, src=None):
            return pltpu.make_async_remote_copy(
                src_ref=slab(*block) if src is None else src, dst_ref=slab(*block),
                send_sem=send_sems.at[k], recv_sem=recv_sems.at[k], device_id=to, device_id_type=MESH)

        mine = pltpu.make_async_copy(x_ref, slab(*me), local_sem)
        mine.start()
        first = [copy(0, me, sibling, src=x_ref)]
        first += [copy(1 + j, me, (*chip, c), src=x_ref) for j, chip in enumerate(chips)]
        for cp in first:
            cp.start()
        passed = [copy(4 + j, (*chip, c), sibling) for j, chip in enumerate(chips)]
        for j, chip in enumerate(chips):
            copy(1 + j, (*chip, c), me).wait_recv()
            passed[j].start()
        copy(0, sibling, me).wait_recv()
        for j, chip in enumerate(chips):
            copy(4 + j, (*chip, 1 - c), me).wait_recv()
        for cp in first + passed:
            cp.wait_send()
        mine.wait()

    return pl.pallas_call(
        body, out_shape=SDS((N_DEV, R, W), loc.dtype), in_specs=[HBM_SPEC], out_specs=HBM_SPEC,
        scratch_shapes=[pltpu.SemaphoreType.DMA((7,)), pltpu.SemaphoreType.DMA((7,)), pltpu.SemaphoreType.DMA],
        name=name)(loc)


def _rs_pair_exchange(gc, name):
    _, nj, R, W = gc.shape

    def body(g_ref, out_ref, send_sem, recv_sem):
        x, y, c = _dev()
        cp = pltpu.make_async_remote_copy(src_ref=g_ref.at[1 - c], dst_ref=out_ref, send_sem=send_sem, recv_sem=recv_sem,
                                          device_id=(x, y, 1 - c), device_id_type=MESH)
        cp.start()
        cp.wait()

    return pl.pallas_call(
        body, out_shape=SDS((nj, R, W), gc.dtype), in_specs=[HBM_SPEC], out_specs=HBM_SPEC,
        scratch_shapes=[pltpu.SemaphoreType.DMA, pltpu.SemaphoreType.DMA], name=name)(gc)


def _add_pairs(gc, ra, name):
    _, nj, R, W = gc.shape
    tr = _tile(R, 256, 8)
    cidx = lax.axis_index("c").astype(jnp.int32).reshape(1)

    def body(c_ref, a_ref, b_ref, o_ref):
        o_ref[...] = a_ref[...] + b_ref[...]

    gs = pltpu.PrefetchScalarGridSpec(
        num_scalar_prefetch=1, grid=(nj, R // tr),
        in_specs=[pl.BlockSpec((None, None, tr, W), lambda j, i, cr: (cr[0], j, i, 0)),
                  pl.BlockSpec((None, tr, W), lambda j, i, cr: (j, i, 0))],
        out_specs=pl.BlockSpec((None, tr, W), lambda j, i, cr: (j, i, 0)))
    return pl.pallas_call(body, grid_spec=gs, out_shape=SDS((nj, R, W), gc.dtype), name=name,
                          compiler_params=pltpu.CompilerParams(vmem_limit_bytes=VMEM_LIMIT))(cidx, gc, ra)


def _rs_chip_exchange(p, name):
    nj, R, W = p.shape

    def body(p_ref, out_ref, send_sems, recv_sems, local_sem):
        x, y, c = _dev()
        j0 = 2 * x + y
        chips = [(1 - x, y), (x, 1 - y), (1 - x, 1 - y)]
        mine = pltpu.make_async_copy(p_ref.at[j0], out_ref.at[j0], local_sem)
        mine.start()

        def copy(k, chip):
            return pltpu.make_async_remote_copy(
                src_ref=p_ref.at[2 * chip[0] + chip[1]], dst_ref=out_ref.at[j0],
                send_sem=send_sems.at[k], recv_sem=recv_sems.at[k], device_id=(*chip, c), device_id_type=MESH)

        sent = [copy(k, chip) for k, chip in enumerate(chips)]
        for cp in sent:
            cp.start()
        for k, chip in enumerate(chips):
            pltpu.make_async_remote_copy(
                src_ref=p_ref.at[j0], dst_ref=out_ref.at[2 * chip[0] + chip[1]],
                send_sem=send_sems.at[k], recv_sem=recv_sems.at[k], device_id=(*chip, c), device_id_type=MESH).wait_recv()
        for cp in sent:
            cp.wait_send()
        mine.wait()

    return pl.pallas_call(
        body, out_shape=SDS((nj, R, W), p.dtype), in_specs=[HBM_SPEC], out_specs=HBM_SPEC,
        scratch_shapes=[pltpu.SemaphoreType.DMA((3,)), pltpu.SemaphoreType.DMA((3,)), pltpu.SemaphoreType.DMA],
        name=name)(p)


def _allgather_small(v, name):
    R, W = v.shape

    def body(v_ref, out_ref, send_sems, recv_sems, local_sem):
        x, y, c = _dev()
        mine = pltpu.make_async_copy(v_ref, out_ref.at[4 * x + 2 * y + c], local_sem)
        mine.start()
        peers = []
        for k in range(1, N_DEV):
            px = 1 - x if k & 4 else x
            py = 1 - y if k & 2 else y
            pc_ = 1 - c if k & 1 else c
            peers.append((px, py, pc_))
        sent = [pltpu.make_async_remote_copy(
            src_ref=v_ref, dst_ref=out_ref.at[4 * x + 2 * y + c], send_sem=send_sems.at[k], recv_sem=recv_sems.at[k],
            device_id=peer, device_id_type=MESH) for k, peer in enumerate(peers)]
        for cp in sent:
            cp.start()
        for k, (px, py, pc_) in enumerate(peers):
            pltpu.make_async_remote_copy(
                src_ref=v_ref, dst_ref=out_ref.at[4 * px + 2 * py + pc_], send_sem=send_sems.at[k],
                recv_sem=recv_sems.at[k], device_id=(px, py, pc_), device_id_type=MESH).wait_recv()
        for cp in sent:
            cp.wait_send()
        mine.wait()

    return pl.pallas_call(
        body, out_shape=SDS((N_DEV, R, W), v.dtype), in_specs=[VMEM_SPEC], out_specs=VMEM_SPEC,
        scratch_shapes=[pltpu.SemaphoreType.DMA((7,)), pltpu.SemaphoreType.DMA((7,)), pltpu.SemaphoreType.DMA],
        name=name)(v)


def _sum_slabs(a, name):
    ns, R, W = a.shape

    def body(a_ref, o_ref):
        acc = a_ref[0]
        for k in range(1, ns):
            acc = acc + a_ref[k]
        o_ref[...] = acc

    return pl.pallas_call(body, out_shape=SDS((R, W), a.dtype), in_specs=[VMEM_SPEC], out_specs=VMEM_SPEC, name=name)(a)


BIG = (("ada_w", "col"), ("w_in", "col"), ("w_ssd_out", "row"), ("w_sc_out", "row"), ("w_o", "row"), ("w_up", "col"),
       ("w_down", "row"))
CONVW = ("ssd_conv_w", "sc_conv_w", "ffn_conv_w")
REPL = ("ada_b", "mix_pre_g", "mix_post_g", "ssd_conv_b", "ssd_dt_bias", "ssd_a_log", "ssd_d", "ssd_norm_g", "ffn_pre_g",
        "ffn_post_g", "ffn_conv_b")
WEIGHTS = ("ada_w", "ada_b", "mix_pre_g", "mix_post_g", "w_in", "ssd_conv_w", "ssd_conv_b", "ssd_dt_bias", "ssd_a_log",
           "ssd_d", "ssd_norm_g", "w_ssd_out", "sc_conv_w", "w_sc_out", "w_o", "ffn_pre_g", "ffn_post_g", "w_up",
           "ffn_conv_w", "ffn_conv_b", "w_down")


def _pad_rows(a, mult):
    r = a.shape[-2]
    pad = -r % mult
    return a if pad == 0 else jnp.pad(a, [(0, 0)] * (a.ndim - 2) + [(0, pad), (0, 0)])


def _flat_rows(parts, mult):
    flat = jnp.concatenate([p.reshape(-1) for p in parts])
    flat = jnp.pad(flat, (0, -flat.shape[0] % ROW_W))
    return _pad_rows(flat.reshape(-1, ROW_W), mult)


def _unflat(buf, shapes):
    flat = buf.reshape(-1)
    out, o = [], 0
    for shp in shapes:
        n = 1
        for s in shp:
            n *= s
        out.append(flat[o:o + n].reshape(shp))
        o += n
    return out


def _pack_big_local(get, depth):
    return jnp.concatenate([get(n)[l].reshape(-1, ROW_W) for l in range(depth) for n, _ in BIG], axis=0)


def _big_rows(shapes, depth):
    out, o = {}, 0
    for l in range(depth):
        for n, _ in BIG:
            r = shapes[n][1] * shapes[n][2] // ROW_W
            out[(l, n)] = (o, o + r)
            o += r
    return out, o


def kernel(x, c, ada_w, ada_b, mix_pre_g, mix_post_g, w_in, ssd_conv_w, ssd_conv_b, ssd_dt_bias, ssd_a_log, ssd_d, ssd_norm_g, w_ssd_out, sc_conv_w, w_sc_out, w_o, ffn_pre_g, ffn_post_g, w_up, ffn_conv_w, ffn_conv_b, w_down, loss_target, m_ada_w, m_ada_b, m_mix_pre_g, m_mix_post_g, m_w_in, m_ssd_conv_w, m_ssd_conv_b, m_ssd_dt_bias, m_ssd_a_log, m_ssd_d, m_ssd_norm_g, m_w_ssd_out, m_sc_conv_w, m_w_sc_out, m_w_o, m_ffn_pre_g, m_ffn_post_g, m_w_up, m_ffn_conv_w, m_ffn_conv_b, m_w_down, v_ada_w, v_ada_b, v_mix_pre_g, v_mix_post_g, v_w_in, v_ssd_conv_w, v_ssd_conv_b, v_ssd_dt_bias, v_ssd_a_log, v_ssd_d, v_ssd_norm_g, v_w_ssd_out, v_sc_conv_w, v_w_sc_out, v_w_o, v_ffn_pre_g, v_ffn_post_g, v_w_up, v_ffn_conv_w, v_ffn_conv_b, v_w_down):
    wl = dict(zip(WEIGHTS, (ada_w, ada_b, mix_pre_g, mix_post_g, w_in, ssd_conv_w, ssd_conv_b, ssd_dt_bias, ssd_a_log,
                            ssd_d, ssd_norm_g, w_ssd_out, sc_conv_w, w_sc_out, w_o, ffn_pre_g, ffn_post_g, w_up,
                            ffn_conv_w, ffn_conv_b, w_down)))
    ml = dict(zip(WEIGHTS, (m_ada_w, m_ada_b, m_mix_pre_g, m_mix_post_g, m_w_in, m_ssd_conv_w, m_ssd_conv_b,
                            m_ssd_dt_bias, m_ssd_a_log, m_ssd_d, m_ssd_norm_g, m_w_ssd_out, m_sc_conv_w, m_w_sc_out, m_w_o,
                            m_ffn_pre_g, m_ffn_post_g, m_w_up, m_ffn_conv_w, m_ffn_conv_b, m_w_down)))
    vl = dict(zip(WEIGHTS, (v_ada_w, v_ada_b, v_mix_pre_g, v_mix_post_g, v_w_in, v_ssd_conv_w, v_ssd_conv_b,
                            v_ssd_dt_bias, v_ssd_a_log, v_ssd_d, v_ssd_norm_g, v_w_ssd_out, v_sc_conv_w, v_w_sc_out, v_w_o,
                            v_ffn_pre_g, v_ffn_post_g, v_w_up, v_ffn_conv_w, v_ffn_conv_b, v_w_down)))
    depth = ada_w.shape[0]
    shapes = {n: wl[n].shape for n in WEIGHTS}
    me = 4 * lax.axis_index("x") + 2 * lax.axis_index("y") + lax.axis_index("c")

    rows, n_big = _big_rows(shapes, depth)
    big_loc = _pack_big_local(lambda n: wl[n].astype(BF16), depth)
    conv_flat = jnp.concatenate([wl[n][l].reshape(-1) for l in range(depth) for n in CONVW])
    n_conv = conv_flat.shape[0]
    conv_flat = jnp.pad(conv_flat, (0, -n_conv % (ROW_W // 2)))
    conv_rows = lax.bitcast_convert_type(conv_flat, BF16).reshape(-1, ROW_W)
    gathered = _allgather_big(_pad_rows(jnp.concatenate([big_loc, conv_rows], axis=0), ROW_PAD), "allgather_weights")
    conv_all = lax.bitcast_convert_type(
        gathered[:, n_big:n_big + conv_rows.shape[0]].reshape(N_DEV, -1, 2), F32)[:, :n_conv]
    conv_full, o = {}, 0
    for l in range(depth):
        for n in CONVW:
            k, cl = shapes[n][1], shapes[n][2]
            conv_full[(l, n)] = conv_all[:, o:o + k * cl].reshape(N_DEV, k, cl).transpose(1, 0, 2).reshape(k, N_DEV * cl)
            o += k * cl
    layers = []
    for l in range(depth):
        w = {n: wl[n][l] for n in REPL}
        for n, kind in BIG:
            r0, r1 = rows[(l, n)]
            a, b = shapes[n][1], shapes[n][2]
            blk = gathered[:, r0:r1]
            if kind == "col":
                w[n] = blk.reshape(N_DEV, a, b).transpose(1, 0, 2).reshape(a, N_DEV * b)
            else:
                w[n] = blk.reshape(N_DEV * a, b)
        for n in CONVW:
            w[n] = conv_full[(l, n)]
        layers.append(w)

    loss_loc, dx, grads = _fwd_bwd(x, c, loss_target, layers)

    slabs = []
    for l in range(depth):
        for n, kind in BIG:
            a, b = shapes[n][1], shapes[n][2]
            gfull = grads[l][n]
            if kind == "col":
                slabs.append(gfull.reshape(a, N_DEV, b).transpose(1, 0, 2).reshape(N_DEV, -1, ROW_W))
            else:
                slabs.append(gfull.reshape(N_DEV, -1, ROW_W))
    gslab = _pad_rows(jnp.concatenate(slabs, axis=1), ROW_PAD)
    rg = gslab.shape[1]
    gc = gslab.reshape(4, 2, rg, ROW_W).transpose(1, 0, 2, 3)
    from_sibling = _rs_pair_exchange(gc, "rs_pair_exchange")
    chip_sums = _add_pairs(gc, from_sibling, "rs_pair_add")
    from_chips = _rs_chip_exchange(chip_sums, "rs_chip_exchange")
    pack_f32 = lambda d: _pad_rows(_pack_big_local(lambda n: d[n], depth), ROW_PAD)
    g_big, d_big, m_big, v_big = _adam(from_chips, pack_f32(wl), pack_f32(ml), pack_f32(vl), "adam_big")

    def unpack_big(buf):
        return {n: jnp.stack([buf[rows[(l, n)][0]:rows[(l, n)][1]].reshape(shapes[n][1:]) for l in range(depth)])
                for n, _ in BIG}

    parts = [jnp.broadcast_to(loss_loc, (ROW_W,))]
    small_shapes = [(ROW_W,)]
    for l in range(depth):
        for n in REPL + CONVW:
            parts.append(grads[l][n])
            small_shapes.append(tuple(grads[l][n].shape))
    total = _sum_slabs(_allgather_small(_flat_rows(parts, 8), "allgather_small"), "sum_small")
    pieces = _unflat(total, small_shapes)
    loss = pieces[0][0]
    g_small, i = {}, 1
    for l in range(depth):
        for n in REPL + CONVW:
            gp = pieces[i]
            i += 1
            if n in CONVW:
                gp = lax.dynamic_slice_in_dim(gp, me * shapes[n][2], shapes[n][2], axis=1)
            g_small[(l, n)] = gp
    order = [(l, n) for l in range(depth) for n in REPL + CONVW]
    loc_shapes = [tuple(shapes[n][1:]) for _, n in order]
    packs = lambda f: _flat_rows([f(l, n) for l, n in order], 8)
    gs_small = packs(lambda l, n: g_small[(l, n)])
    _, d_sm, m_sm, v_sm = _adam(gs_small[None], packs(lambda l, n: wl[n][l]), packs(lambda l, n: ml[n][l]),
                                packs(lambda l, n: vl[n][l]), "adam_small")

    def unpack_small(buf):
        ps = _unflat(buf, loc_shapes)
        return {n: jnp.stack([ps[order.index((l, n))] for l in range(depth)]) for n in REPL + CONVW}

    outs = []
    for big, small in ((unpack_big(g_big), {n: jnp.stack([g_small[(l, n)] for l in range(depth)]) for n in REPL + CONVW}),
                       (unpack_big(d_big), unpack_small(d_sm)), (unpack_big(m_big), unpack_small(m_sm)),
                       (unpack_big(v_big), unpack_small(v_sm))):
        merged = {**big, **small}
        outs += [merged[n] for n in WEIGHTS]
    return (loss, dx, *outs)
```

```python
import functools

import jax
import jax.numpy as jnp
from jax import lax
from jax.experimental import pallas as pl
from jax.experimental.pallas import tpu as pltpu

F32, BF16 = jnp.float32, jnp.bfloat16
SDS = jax.ShapeDtypeStruct
MESH = pl.DeviceIdType.MESH

EPS = 1e-6
N_STATE = 128
CHUNK = 128
SSD_CONV_K, SC_CONV_K, FFN_CONV_K = 4, 3, 3
N_DEV = 8
ROW_W = 1024
ROW_PAD = 256
GLU_W = 256
MOD_ROWS = 128
VMEM_LIMIT = 48 * 2**20

ADAM_LR, ADAM_B1, ADAM_B2, ADAM_EPS, ADAM_WD, ADAM_STEP = 0.001, 0.9, 0.999, 1e-08, 0.01, 10

NT = (((1,), (1,)), ((), ()))
TN = (((0,), (0,)), ((), ()))
NN = (((1,), (0,)), ((), ()))


def _tile(n, cap, mult=128):
    best = None
    for t in range(mult, min(n, cap) + 1, mult):
        if n % t == 0:
            best = t
    return best if best is not None else n


def _pc(body, *, grid, in_specs, out_specs, out_shape, name, scratch=()):
    return pl.pallas_call(
        body, grid=grid, in_specs=in_specs, out_specs=out_specs, out_shape=out_shape,
        scratch_shapes=list(scratch), name=name,
        compiler_params=pltpu.CompilerParams(
            dimension_semantics=("arbitrary",) * len(grid), vmem_limit_bytes=VMEM_LIMIT))


def _silu(x):
    return x * jax.nn.sigmoid(x)


def _dsilu(x):
    s = jax.nn.sigmoid(x)
    return s * (1.0 + x * (1.0 - s))


def _softplus(x):
    return jnp.maximum(x, 0.0) + jnp.log(1.0 + jnp.exp(-jnp.abs(x)))


def _dot(a, b, dims=NN):
    return lax.dot_general(a, b, dims, preferred_element_type=F32)


def _bsum(v, rows=8):
    return jnp.broadcast_to(v, (rows, v.shape[1]))


def _mm(a, b, mode, out_dtype, name, caps=(1024, 1024, 2048)):
    if mode == "nn":
        (M, K), (K2, N) = a.shape, b.shape
    elif mode == "nt":
        (M, K), (N, K2) = a.shape, b.shape
    else:
        (K, M), (K2, N) = a.shape, b.shape
    assert K == K2, (a.shape, b.shape, mode)
    tm, tn, tk = _tile(M, caps[0]), _tile(N, caps[1]), _tile(K, caps[2])
    nk = K // tk
    dims = {"nn": NN, "nt": NT, "tn": TN}[mode]
    if mode == "tn":
        a_spec = pl.BlockSpec((tk, tm), lambda i, j, k: (k, i))
    else:
        a_spec = pl.BlockSpec((tm, tk), lambda i, j, k: (i, k))
    if mode == "nt":
        b_spec = pl.BlockSpec((tn, tk), lambda i, j, k: (j, k))
    else:
        b_spec = pl.BlockSpec((tk, tn), lambda i, j, k: (k, j))

    def body(a_ref, b_ref, o_ref, *acc):
        part = _dot(a_ref[...].astype(BF16), b_ref[...].astype(BF16), dims)
        if nk == 1:
            o_ref[...] = part.astype(o_ref.dtype)
        else:
            acc_ref, = acc
            k = pl.program_id(2)

            @pl.when(k == 0)
            def _():
                acc_ref[...] = part

            @pl.when(k > 0)
            def _():
                acc_ref[...] += part

            @pl.when(k == nk - 1)
            def _():
                o_ref[...] = acc_ref[...].astype(o_ref.dtype)

    return _pc(body, grid=(M // tm, N // tn, nk), in_specs=[a_spec, b_spec],
               out_specs=pl.BlockSpec((tm, tn), lambda i, j, k: (i, j)),
               out_shape=SDS((M, N), out_dtype), name=name,
               scratch=() if nk == 1 else (pltpu.VMEM((tm, tn), F32),))(a, b)


def _modk(c8, ada_w, ada_b, name):
    rows, D = c8.shape
    N = ada_w.shape[0]
    tn = _tile(N, 1536)

    def body(c_ref, w_ref, b_ref, mod_ref, ca_ref):
        ca = _silu(c_ref[...]).astype(BF16)
        mod_ref[...] = _dot(ca, w_ref[...], NT) + b_ref[...]
        ca_ref[...] = ca

    return _pc(body, grid=(N // tn,),
               in_specs=[pl.BlockSpec((rows, D), lambda j: (0, 0)), pl.BlockSpec((tn, D), lambda j: (j, 0)),
                         pl.BlockSpec((1, tn), lambda j: (0, j))],
               out_specs=[pl.BlockSpec((rows, tn), lambda j: (0, j)), pl.BlockSpec((rows, D), lambda j: (0, 0))],
               out_shape=[SDS((rows, N), F32), SDS((rows, D), BF16)], name=name)(c8, ada_w, ada_b)


def _row_tile(S):
    return _tile(S, 512, 8)


def _norm_mod(x, g, mod3, sc_seg, sh_seg, S, name):
    T, D = x.shape
    tm = _row_tile(S)
    tpb = S // tm

    def body(x_ref, g_ref, sc_ref, sh_ref, h_ref):
        x_ = x_ref[...]
        r = lax.rsqrt(jnp.mean(x_ * x_, axis=-1, keepdims=True) + EPS)
        h_ref[...] = ((x_ * r) * g_ref[...] * (1.0 + sc_ref[...]) + sh_ref[...]).astype(BF16)

    return _pc(body, grid=(T // tm,),
               in_specs=[pl.BlockSpec((tm, D), lambda i: (i, 0)), pl.BlockSpec((1, D), lambda i: (0, 0)),
                         pl.BlockSpec((None, 1, D), lambda i: (i // tpb, 0, sc_seg)),
                         pl.BlockSpec((None, 1, D), lambda i: (i // tpb, 0, sh_seg))],
               out_specs=pl.BlockSpec((tm, D), lambda i: (i, 0)), out_shape=SDS((T, D), BF16), name=name)(x, g, mod3, mod3)


def _resid_post(x, fo, mod3, gt_seg, pg, S, name):
    T, D = x.shape
    tm = _row_tile(S)
    tpb = S // tm

    def body(x_ref, f_ref, gt_ref, pg_ref, o_ref):
        f = f_ref[...]
        r = lax.rsqrt(jnp.mean(f * f, axis=-1, keepdims=True) + EPS)
        o_ref[...] = x_ref[...] + gt_ref[...] * ((f * r) * pg_ref[...])

    return _pc(body, grid=(T // tm,),
               in_specs=[pl.BlockSpec((tm, D), lambda i: (i, 0)), pl.BlockSpec((tm, D), lambda i: (i, 0)),
                         pl.BlockSpec((None, 1, D), lambda i: (i // tpb, 0, gt_seg)),
                         pl.BlockSpec((1, D), lambda i: (0, 0))],
               out_specs=pl.BlockSpec((tm, D), lambda i: (i, 0)), out_shape=SDS((T, D), F32), name=name)(x, fo, mod3, pg)


def _post_bwd(fo, mod3, gt_seg, pg, dout, S, name):
    T, D = fo.shape
    tm = _row_tile(S)
    tpb = S // tm
    nb = T // S

    def body(f_ref, gt_ref, pg_ref, d_ref, df_ref, dgt_ref, dpg_ref):
        i = pl.program_id(0)

        @pl.when(i == 0)
        def _():
            dpg_ref[...] = jnp.zeros_like(dpg_ref)

        @pl.when(i % tpb == 0)
        def _():
            dgt_ref[...] = jnp.zeros_like(dgt_ref)

        f, d, gt, pg_ = f_ref[...], d_ref[...], gt_ref[...], pg_ref[...]
        r = lax.rsqrt(jnp.mean(f * f, axis=-1, keepdims=True) + EPS)
        n = f * r
        dn = d * gt * pg_
        df_ref[...] = (r * (dn - n * jnp.mean(dn * n, axis=-1, keepdims=True))).astype(df_ref.dtype)
        dn_ = d * n
        dgt_ref[...] += _bsum(jnp.sum(dn_ * pg_, axis=0, keepdims=True))
        dpg_ref[...] += _bsum(jnp.sum(dn_ * gt, axis=0, keepdims=True))

    return _pc(body, grid=(T // tm,),
               in_specs=[pl.BlockSpec((tm, D), lambda i: (i, 0)),
                         pl.BlockSpec((None, 1, D), lambda i: (i // tpb, 0, gt_seg)),
                         pl.BlockSpec((1, D), lambda i: (0, 0)), pl.BlockSpec((tm, D), lambda i: (i, 0))],
               out_specs=[pl.BlockSpec((tm, D), lambda i: (i, 0)), pl.BlockSpec((8, D), lambda i: (i // tpb, 0)),
                          pl.BlockSpec((8, D), lambda i: (0, 0))],
               out_shape=[SDS((T, D), BF16), SDS((nb * 8, D), F32), SDS((8, D), F32)], name=name)(fo, mod3, pg, dout)


def _pre_bwd(x, g, mod3, sc_seg, dh, dout, S, name):
    T, D = x.shape
    tm = _row_tile(S)
    tpb = S // tm
    nb = T // S

    def body(x_ref, g_ref, sc_ref, dh_ref, d_ref, dx_ref, dg_ref, dsc_ref, dsh_ref):
        i = pl.program_id(0)

        @pl.when(i == 0)
        def _():
            dg_ref[...] = jnp.zeros_like(dg_ref)

        @pl.when(i % tpb == 0)
        def _():
            dsc_ref[...] = jnp.zeros_like(dsc_ref)
            dsh_ref[...] = jnp.zeros_like(dsh_ref)

        x_, g_, dh_ = x_ref[...], g_ref[...], dh_ref[...]
        one_sc = 1.0 + sc_ref[...]
        r = lax.rsqrt(jnp.mean(x_ * x_, axis=-1, keepdims=True) + EPS)
        n = x_ * r
        dn = dh_ * g_ * one_sc
        dx_ref[...] = d_ref[...] + r * (dn - n * jnp.mean(dn * n, axis=-1, keepdims=True))
        dhn = dh_ * n
        dg_ref[...] += _bsum(jnp.sum(dhn * one_sc, axis=0, keepdims=True))
        dsc_ref[...] += _bsum(jnp.sum(dhn * g_, axis=0, keepdims=True))
        dsh_ref[...] += _bsum(jnp.sum(dh_, axis=0, keepdims=True))

    row = pl.BlockSpec((tm, D), lambda i: (i, 0))
    return _pc(body, grid=(T // tm,),
               in_specs=[row, pl.BlockSpec((1, D), lambda i: (0, 0)),
                         pl.BlockSpec((None, 1, D), lambda i: (i // tpb, 0, sc_seg)), row, row],
               out_specs=[row, pl.BlockSpec((8, D), lambda i: (0, 0)), pl.BlockSpec((8, D), lambda i: (i // tpb, 0)),
                          pl.BlockSpec((8, D), lambda i: (i // tpb, 0))],
               out_shape=[SDS((T, D), F32), SDS((8, D), F32), SDS((nb * 8, D), F32), SDS((nb * 8, D), F32)],
               name=name)(x, g, mod3, dh, dout)


def _loss(y, target, S, name):
    T, D = y.shape
    tm = _row_tile(S)

    def body(y_ref, t_ref, dy_ref, l_ref):
        @pl.when(pl.program_id(0) == 0)
        def _():
            l_ref[...] = jnp.zeros_like(l_ref)

        e = y_ref[...] - t_ref[...]
        dy_ref[...] = e * (1.0 / D)
        l_ref[...] += jnp.broadcast_to(jnp.sum(e * e, keepdims=True) * (0.5 / D), l_ref.shape)

    row = pl.BlockSpec((tm, D), lambda i: (i, 0))
    return _pc(body, grid=(T // tm,), in_specs=[row, row],
               out_specs=[row, pl.BlockSpec((8, 128), lambda i: (0, 0))],
               out_shape=[SDS((T, D), F32), SDS((8, 128), F32)], name=name)(y, target)


def _conv_geom(view, C, S):
    arr, off = view
    T = arr.shape[0]
    tm = _row_tile(S)
    tc = _tile(C, 512)
    assert off % tc == 0 and C % tc == 0
    return arr, off // tc, T, tm, tc, S // tm


def _prev_spec(tm, tc, ob, order):
    if order == "ij":
        return pl.BlockSpec((8, tc), lambda i, j: (jnp.maximum(i * (tm // 8) - 1, 0), ob + j))
    return pl.BlockSpec((8, tc), lambda j, i: (jnp.maximum(i * (tm // 8) - 1, 0), ob + j))


def _next_spec(T, tm, tc, ob, order):
    last = T // 8 - 1
    if order == "ij":
        return pl.BlockSpec((8, tc), lambda i, j: (jnp.minimum((i + 1) * (tm // 8), last), ob + j))
    return pl.BlockSpec((8, tc), lambda j, i: (jnp.minimum((i + 1) * (tm // 8), last), ob + j))


def _shifted(buf, cur, prev, tm):
    buf[0:8, :] = prev
    buf[8:, :] = cur
    return lambda j: buf[pl.ds(8 - j, tm), :]


def _conv_fwd(view, C, w8, b, K, S, name):
    arr, ob, T, tm, tc, tps = _conv_geom(view, C, S)

    def body(u_ref, p_ref, w_ref, b_ref, o_ref, buf):
        first = (pl.program_id(0) % tps) == 0
        u = u_ref[...]
        sh = _shifted(buf, u, jnp.where(first, 0.0, p_ref[...]), tm)
        acc = u * w_ref[K - 1:K, :] + b_ref[...]
        for j in range(1, K):
            acc = acc + sh(j) * w_ref[K - 1 - j:K - j, :]
        o_ref[...] = acc

    return _pc(body, grid=(T // tm, C // tc),
               in_specs=[pl.BlockSpec((tm, tc), lambda i, j: (i, ob + j)), _prev_spec(tm, tc, ob, "ij"),
                         pl.BlockSpec((8, tc), lambda i, j: (0, j)), pl.BlockSpec((1, tc), lambda i, j: (0, j))],
               out_specs=pl.BlockSpec((tm, tc), lambda i, j: (i, j)), out_shape=SDS((T, C), F32), name=name,
               scratch=(pltpu.VMEM((tm + 8, tc), F32),))(arr, arr, w8, b)


def _conv_bwd_in(dview, C, w8, K, S, out_dtype, name):
    arr, ob, T, tm, tc, tps = _conv_geom(dview, C, S)

    def body(d_ref, n_ref, w_ref, o_ref, buf):
        last = (pl.program_id(0) % tps) == tps - 1
        d = d_ref[...]
        buf[0:tm, :] = d
        buf[tm:tm + 8, :] = jnp.where(last, 0.0, n_ref[...])
        acc = d * w_ref[K - 1:K, :]
        for j in range(1, K):
            acc = acc + buf[pl.ds(j, tm), :] * w_ref[K - 1 - j:K - j, :]
        o_ref[...] = acc.astype(o_ref.dtype)

    return _pc(body, grid=(T // tm, C // tc),
               in_specs=[pl.BlockSpec((tm, tc), lambda i, j: (i, ob + j)), _next_spec(T, tm, tc, ob, "ij"),
                         pl.BlockSpec((8, tc), lambda i, j: (0, j))],
               out_specs=pl.BlockSpec((tm, tc), lambda i, j: (i, j)), out_shape=SDS((T, C), out_dtype), name=name,
               scratch=(pltpu.VMEM((tm + 8, tc), F32),))(arr, arr, w8)


def _conv_bwd_w(dview, uview, C, K, S, name):
    darr, dob, T, tm, tc, tps = _conv_geom(dview, C, S)
    uarr, uob, _, _, _, _ = _conv_geom(uview, C, S)

    def body(d_ref, u_ref, p_ref, o_ref, buf):
        i = pl.program_id(1)

        @pl.when(i == 0)
        def _():
            o_ref[...] = jnp.zeros_like(o_ref)

        first = (i % tps) == 0
        d, u = d_ref[...], u_ref[...]
        sh = _shifted(buf, u, jnp.where(first, 0.0, p_ref[...]), tm)
        rows = []
        for k in range(K):
            j = K - 1 - k
            rows.append(jnp.sum(d * (u if j == 0 else sh(j)), axis=0, keepdims=True))
        rows.append(jnp.sum(d, axis=0, keepdims=True))
        rows.append(jnp.zeros((8 - K - 1, tc), F32))
        o_ref[...] += jnp.concatenate(rows, axis=0)

    return _pc(body, grid=(C // tc, T // tm),
               in_specs=[pl.BlockSpec((tm, tc), lambda j, i: (i, dob + j)),
                         pl.BlockSpec((tm, tc), lambda j, i: (i, uob + j)), _prev_spec(tm, tc, uob, "ji")],
               out_specs=pl.BlockSpec((8, tc), lambda j, i: (0, j)), out_shape=SDS((8, C), F32), name=name,
               scratch=(pltpu.VMEM((tm + 8, tc), F32),))(darr, uarr, uarr)


def _ffn_act_fwd(uu, w8, b, S, name):
    K, gw = FFN_CONV_K, GLU_W
    T, F2 = uu.shape
    tm, tc = _row_tile(S), 2 * GLU_W
    tps = S // tm

    def body(u_ref, p_ref, w_ref, b_ref, a_ref, buf):
        first = (pl.program_id(0) % tps) == 0
        uu_ = u_ref[...]
        sh = _shifted(buf, uu_, jnp.where(first, 0.0, p_ref[...]), tm)
        u = uu_ * w_ref[K - 1:K, :] + b_ref[...]
        for j in range(1, K):
            u = u + sh(j) * w_ref[K - 1 - j:K - j, :]
        a_ref[...] = (_silu(u[:, :gw]) * u[:, gw:]).astype(BF16)

    return _pc(body, grid=(T // tm, F2 // tc),
               in_specs=[pl.BlockSpec((tm, tc), lambda i, j: (i, j)), _prev_spec(tm, tc, 0, "ij"),
                         pl.BlockSpec((8, tc), lambda i, j: (0, j)), pl.BlockSpec((1, tc), lambda i, j: (0, j))],
               out_specs=pl.BlockSpec((tm, gw), lambda i, j: (i, j)), out_shape=SDS((T, F2 // 2), BF16), name=name,
               scratch=(pltpu.VMEM((tm + 8, tc), F32),))(uu, uu, w8, b)


def _ffn_act_bwd(uu, da, w8, b, S, name):
    K, gw = FFN_CONV_K, GLU_W
    T, F2 = uu.shape
    tm, tc = _row_tile(S), 2 * GLU_W
    tps = S // tm
    last16 = T // 16 - 1

    def body(u_ref, p_ref, n_ref, da_ref, dan_ref, w_ref, b_ref, duu_ref, cw_ref, ubuf, dbuf):
        i = pl.program_id(1)

        @pl.when(i == 0)
        def _():
            cw_ref[...] = jnp.zeros_like(cw_ref)

        first = (i % tps) == 0
        last = (i % tps) == tps - 1
        ubuf[0:8, :] = jnp.where(first, 0.0, p_ref[...])
        ubuf[8:tm + 8, :] = u_ref[...]
        ubuf[tm + 8:tm + 16, :] = n_ref[...]
        u = ubuf[pl.ds(8, tm + 8), :] * w_ref[K - 1:K, :] + b_ref[...]
        for j in range(1, K):
            u = u + ubuf[pl.ds(8 - j, tm + 8), :] * w_ref[K - 1 - j:K - j, :]
        da_n = jnp.where(last, 0.0, dan_ref[...].astype(F32)[0:8, :])
        da_ = jnp.concatenate([da_ref[...].astype(F32), da_n], axis=0)
        g, v = u[:, :gw], u[:, gw:]
        du = jnp.concatenate([da_ * v * _dsilu(g), da_ * _silu(g)], axis=1)
        dbuf[...] = du
        dmain = du[0:tm, :]
        duu = dmain * w_ref[K - 1:K, :]
        for j in range(1, K):
            duu = duu + dbuf[pl.ds(j, tm), :] * w_ref[K - 1 - j:K - j, :]
        duu_ref[...] = duu.astype(BF16)
        rows = [jnp.sum(dmain * ubuf[pl.ds(8 - (K - 1 - k), tm), :], axis=0, keepdims=True) for k in range(K)]
        rows.append(jnp.sum(dmain, axis=0, keepdims=True))
        rows.append(jnp.zeros((8 - K - 1, tc), F32))
        cw_ref[...] += jnp.concatenate(rows, axis=0)

    return _pc(body, grid=(F2 // tc, T // tm),
               in_specs=[pl.BlockSpec((tm, tc), lambda j, i: (i, j)), _prev_spec(tm, tc, 0, "ji"),
                         _next_spec(T, tm, tc, 0, "ji"), pl.BlockSpec((tm, gw), lambda j, i: (i, j)),
                         pl.BlockSpec((16, gw), lambda j, i: (jnp.minimum((i + 1) * (tm // 16), last16), j)),
                         pl.BlockSpec((8, tc), lambda j, i: (0, j)), pl.BlockSpec((1, tc), lambda j, i: (0, j))],
               out_specs=[pl.BlockSpec((tm, tc), lambda j, i: (i, j)), pl.BlockSpec((8, tc), lambda j, i: (0, j))],
               out_shape=[SDS((T, F2), BF16), SDS((8, F2), F32)], name=name,
               scratch=(pltpu.VMEM((tm + 16, tc), F32), pltpu.VMEM((tm + 8, tc), F32)))(uu, uu, uu, da, da, w8, b)


def _ssd_common(dtc_raw, dtr_raw, hpc, hpr, L):
    dt_c = _softplus(dtc_raw + hpc[0:1, :])
    a_c = -jnp.exp(hpc[1:2, :])
    dt_r = _softplus(dtr_raw + hpr[:, 0:1])
    a_r = -jnp.exp(hpr[:, 1:2])
    li = lax.broadcasted_iota(jnp.int32, (L, L), 0)
    si = lax.broadcasted_iota(jnp.int32, (L, L), 1)
    low = li >= si
    upp = li <= si
    acs_c = _dotx(low, dt_c * a_c, split="b")
    acs_r = _dotx(dt_r * a_r, upp)
    return dt_c, a_c, acs_c, acs_r, low, upp


def _dotx(a, b, split="a", parts=3, dims=NN):
    val, one = (a, b) if split == "a" else (b, a)
    one = one.astype(BF16)
    acc, rem = None, val
    for i in range(parts):
        piece = rem.astype(BF16)
        t = _dot(piece, one, dims) if split == "a" else _dot(one, piece, dims)
        acc = t if acc is None else acc + t
        if i + 1 < parts:
            rem = rem - piece.astype(F32)
    return acc


def _head_maps(R, P, L):
    RP = R * P
    sel = (lax.broadcasted_iota(jnp.int32, (RP, R), 0) // P == lax.broadcasted_iota(jnp.int32, (RP, R), 1)).astype(F32)
    selt = (lax.broadcasted_iota(jnp.int32, (R, RP), 1) // P == lax.broadcasted_iota(jnp.int32, (R, RP), 0)).astype(F32)
    colb = (lax.broadcasted_iota(jnp.int32, (R, R * L), 1) // L == lax.broadcasted_iota(jnp.int32, (R, R * L), 0)).astype(F32)
    return sel, selt, colb


def _pair_diag(mats, rhs_b, R, P):
    lanes = 2 * P
    lo = lax.broadcasted_iota(jnp.int32, (mats[0].shape[0], lanes), 1) < P
    out = []
    for q in range(R // 2):
        rp = rhs_b[:, q * lanes:(q + 1) * lanes]
        out.append(jnp.where(lo, _dot(mats[2 * q], rp), _dot(mats[2 * q + 1], rp)))
    return jnp.concatenate(out, axis=1) if len(out) > 1 else out[0]


def _ssd_specs(pre, off_x, off_b, off_c, G, R, P, nb, nc, rev):
    L, N, RP = CHUNK, N_STATE, R * P
    cidx = (lambda c: nc - 1 - c) if rev else (lambda c: c)
    xb, bb, cb = off_x // RP, off_b // N, off_c // N
    assert off_x % RP == 0 and off_b % N == 0 and off_c % N == 0
    row = lambda b, c: b * nc + cidx(c)
    return dict(
        x=pl.BlockSpec((L, RP), lambda g, b, c: (row(b, c), xb + g)),
        b=pl.BlockSpec((L, N), lambda g, b, c: (row(b, c), bb + g)),
        c=pl.BlockSpec((L, N), lambda g, b, c: (row(b, c), cb + g)),
        dtc=pl.BlockSpec((None, L, R), lambda g, b, c: (g, row(b, c), 0)),
        dtr=pl.BlockSpec((None, R, L), lambda g, b, c: (g, 0, row(b, c))),
        hpc=pl.BlockSpec((None, 8, R), lambda g, b, c: (g, 0, 0)),
        hpr=pl.BlockSpec((None, R, 8), lambda g, b, c: (g, 0, 0)),
        y=pl.BlockSpec((L, RP), lambda g, b, c: (row(b, c), g)),
        bc=pl.BlockSpec((L, N), lambda g, b, c: (row(b, c), g)),
        hs=pl.BlockSpec((None, None, N, RP), lambda g, b, c: (row(b, c), g, 0, 0)),
    )


def _ssd_fwd(pre, offs, dtc, dtr, hpc, hpr, G, R, P, S, name):
    T = pre.shape[0]
    L, N, RP = CHUNK, N_STATE, R * P
    nc, nb = S // L, T // S
    sp = _ssd_specs(pre, *offs, G, R, P, nb, nc, False)

    def body(px_ref, pb_ref, pc_ref, dtc_ref, dtr_ref, hpc_ref, hpr_ref, y_ref, hs_ref, hst):
        @pl.when(pl.program_id(2) == 0)
        def _():
            hst[...] = jnp.zeros_like(hst)

        xs, bm, cm = _silu(px_ref[...]), _silu(pb_ref[...]), _silu(pc_ref[...])
        hpc_ = hpc_ref[...]
        dt_c, _, acs_c, acs_r, low, _ = _ssd_common(dtc_ref[...], dtr_ref[...], hpc_, hpr_ref[...], L)
        _, selt, colb = _head_maps(R, P, L)
        dt_e, a_e, hp_e = _dotx(dt_c, selt), _dotx(acs_c, selt), _dotx(hpc_, selt)
        a_bc = _dotx(acs_c, colb)
        a_last = a_e[L - 1:L, :]
        bb, cb = bm.astype(BF16), cm.astype(BF16)
        gm = _dot(cb, bb, NT)
        hprev = hst[...]
        hprev_b = hprev.astype(BF16)
        hs_ref[...] = hprev_b
        xdt = xs * dt_e
        xdt_b = xdt.astype(BF16)
        ms = []
        for r in range(R):
            dec = jnp.exp(jnp.where(low, a_bc[:, r * L:(r + 1) * L] - acs_r[r:r + 1, :], -jnp.inf))
            ms.append((gm * dec).astype(BF16))
        y = _pair_diag(ms, xdt_b, R, P) + _dot(cb, hprev_b) * jnp.exp(a_e) + hp_e[2:3, :] * xs
        y_ref[...] = y
        xw = (xdt * jnp.exp(a_last - a_e)).astype(BF16)
        hst[...] = hprev * jnp.exp(a_last) + _dot(bb, xw, TN)

    return _pc(body, grid=(G, nb, nc),
               in_specs=[sp["x"], sp["b"], sp["c"], sp["dtc"], sp["dtr"], sp["hpc"], sp["hpr"]],
               out_specs=[sp["y"], sp["hs"]],
               out_shape=[SDS((T, G * RP), F32), SDS((nb * nc, G, N, RP), BF16)], name=name,
               scratch=(pltpu.VMEM((N, RP), F32),))(pre, pre, pre, dtc, dtr, hpc, hpr)


def _ssd_bwd(pre, offs, dtc, dtr, hpc, hpr, hs, dy, G, R, P, S, name):
    T = pre.shape[0]
    L, N, RP = CHUNK, N_STATE, R * P
    nc, nb = S // L, T // S
    sp = _ssd_specs(pre, *offs, G, R, P, nb, nc, True)

    def body(px_ref, pb_ref, pc_ref, dtc_ref, dtr_ref, hpc_ref, hpr_ref, hs_ref, dy_ref,
             dpx_ref, dpb_ref, dpc_ref, ddt_ref, hpg_ref, dhst):
        bi, ci = pl.program_id(1), pl.program_id(2)

        @pl.when(ci == 0)
        def _():
            dhst[...] = jnp.zeros_like(dhst)

        @pl.when((bi == 0) & (ci == 0))
        def _():
            hpg_ref[...] = jnp.zeros_like(hpg_ref)

        px, pb, pcc = px_ref[...], pb_ref[...], pc_ref[...]
        xs, bm, cm = _silu(px), _silu(pb), _silu(pcc)
        hpc_ = hpc_ref[...]
        dtc_raw = dtc_ref[...]
        dt_c, a_c, acs_c, acs_r, low, upp = _ssd_common(dtc_raw, dtr_ref[...], hpc_, hpr_ref[...], L)
        sel, selt, colb = _head_maps(R, P, L)
        dt_e, a_e, hp_e = _dotx(dt_c, selt), _dotx(acs_c, selt), _dotx(hpc_, selt)
        a_bc = _dotx(acs_c, colb)
        a_last = a_e[L - 1:L, :]
        e_e, w_e = jnp.exp(a_e), jnp.exp(a_last - a_e)
        bb, cb = bm.astype(BF16), cm.astype(BF16)
        gm = _dot(cb, bb, NT)
        gmt = _dot(bb, cb, NT)
        hprev = hs_ref[...]
        dhn = dhst[...]
        dhn_b = dhn.astype(BF16)
        dy = dy_ref[...]
        dy_b = dy.astype(BF16)
        xdt = xs * dt_e
        xdt_b = xdt.astype(BF16)
        yoff = _dot(cb, hprev) * e_e
        dye_b = (dy * e_e).astype(BF16)
        dcm = _dot(dye_b, hprev, NT)
        dhst[...] = _dot(cb, dye_b, TN) + jnp.exp(a_last) * dhn
        dxdt_st = _dot(bb, dhn_b) * w_e
        dbm = _dot((xdt * w_e).astype(BF16), dhn_b, NT)
        lanes = 2 * P
        lo = lax.broadcasted_iota(jnp.int32, (L, lanes), 1) < P
        dg = jnp.zeros((L, L), F32)
        es, css = [], []
        for r in range(R):
            col_b, row = a_bc[:, r * L:(r + 1) * L], acs_r[r:r + 1, :]
            dec = jnp.exp(jnp.where(low, col_b - row, -jnp.inf))
            q = r // 2
            dyp = dy_b[:, q * lanes:(q + 1) * lanes]
            dyp = jnp.where(lo if r % 2 == 0 else ~lo, dyp, jnp.zeros_like(dyp))
            dm = _dot(dyp, xdt_b[:, q * lanes:(q + 1) * lanes], NT)
            dg = dg + dm * dec
            e = dm * (gm * dec)
            es.append(e)
            css.append(jnp.sum(e, axis=0, keepdims=True))
        dgb = dg.astype(BF16)
        dcm = dcm + _dot(dgb, bb)
        dbm = dbm + _dot(dgb, cb, TN)
        colbt = (lax.broadcasted_iota(jnp.int32, (R * L, R), 0) // L
                 == lax.broadcasted_iota(jnp.int32, (R * L, R), 1)).astype(F32)
        eye = (lax.broadcasted_iota(jnp.int32, (R, R), 0) == lax.broadcasted_iota(jnp.int32, (R, R), 1)).astype(F32)
        row_sums = _dotx(jnp.concatenate(es, axis=1), colbt)
        col_sums = _dotx(jnp.concatenate(css, axis=0), eye, dims=TN)
        mts = []
        for r in range(R):
            dect = jnp.exp(jnp.where(upp, acs_r[r:r + 1, :] - a_bc[:, r * L:(r + 1) * L], -jnp.inf))
            mts.append((gmt * dect).astype(BF16))
        dxdt = _pair_diag(mts, dy_b, R, P) + dxdt_st
        q_st = _dotx(xdt * dxdt_st, sel, parts=2)
        da = row_sums - col_sums + _dotx(dy * yoff, sel, parts=2) - q_st
        hh = jnp.sum(_dotx(dhn * hprev.astype(F32), sel, parts=2), axis=0, keepdims=True)
        da_last = jnp.exp(acs_c[L - 1:L, :]) * hh + jnp.sum(q_st, axis=0, keepdims=True)
        rowi = lax.broadcasted_iota(jnp.int32, (L, R), 0)
        da = da + jnp.where(rowi == L - 1, da_last, 0.0)
        dpx_ref[...] = (dxdt * dt_e + hp_e[2:3, :] * dy) * _dsilu(px)
        dpb_ref[...] = dbm * _dsilu(pb)
        dpc_ref[...] = dcm * _dsilu(pcc)
        dadt = _dotx(upp, da, split="b")
        ddt = _dotx(dxdt * xs, sel, parts=2) + dadt * a_c
        ddt_raw = ddt * jax.nn.sigmoid(dtc_raw + hpc_[0:1, :])
        ddt_ref[...] = ddt_raw
        d_a = jnp.sum(dadt * dt_c, axis=0, keepdims=True)
        d_d = jnp.sum(_dotx(dy * xs, sel, parts=2), axis=0, keepdims=True)
        rows = [jnp.sum(ddt_raw, axis=0, keepdims=True), d_a * a_c, d_d, jnp.zeros((5, R), F32)]
        hpg_ref[...] += jnp.concatenate(rows, axis=0)

    return _pc(body, grid=(G, nb, nc),
               in_specs=[sp["x"], sp["b"], sp["c"], sp["dtc"], sp["dtr"], sp["hpc"], sp["hpr"], sp["hs"], sp["y"]],
               out_specs=[sp["y"], sp["bc"], sp["bc"], sp["dtc"], pl.BlockSpec((None, 8, R), lambda g, b, c: (g, 0, 0))],
               out_shape=[SDS((T, G * RP), F32), SDS((T, G * N), F32), SDS((T, G * N), F32), SDS((G, T, R), F32),
                          SDS((G, 8, R), F32)], name=name,
               scratch=(pltpu.VMEM((N, RP), F32),))(pre, pre, pre, dtc, dtr, hpc, hpr, hs, dy)


def _gate_norm_fwd(y, zview, ng, G, S, name):
    T, DI = y.shape
    zarr, zoff = zview
    gw = DI // G
    tm = _row_tile(S)
    zb = zoff // gw
    assert zoff % gw == 0

    def body(y_ref, z_ref, g_ref, o_ref):
        yg = y_ref[...] * _silu(z_ref[...])
        r = lax.rsqrt(jnp.mean(yg * yg, axis=-1, keepdims=True) + EPS)
        o_ref[...] = (yg * r * g_ref[...]).astype(BF16)

    return _pc(body, grid=(T // tm, G),
               in_specs=[pl.BlockSpec((tm, gw), lambda i, g: (i, g)), pl.BlockSpec((tm, gw), lambda i, g: (i, zb + g)),
                         pl.BlockSpec((1, gw), lambda i, g: (0, g))],
               out_specs=pl.BlockSpec((tm, gw), lambda i, g: (i, g)), out_shape=SDS((T, DI), BF16), name=name)(y, zarr, ng)


def _gate_norm_bwd(y, zview, ng, dyn, G, S, name):
    T, DI = y.shape
    zarr, zoff = zview
    gw = DI // G
    tm = _row_tile(S)
    zb = zoff // gw

    def body(y_ref, z_ref, g_ref, d_ref, dy_ref, dz_ref, dg_ref):
        @pl.when(pl.program_id(1) == 0)
        def _():
            dg_ref[...] = jnp.zeros_like(dg_ref)

        y_, z, d = y_ref[...], z_ref[...], d_ref[...]
        sz = _silu(z)
        yg = y_ * sz
        r = lax.rsqrt(jnp.mean(yg * yg, axis=-1, keepdims=True) + EPS)
        n = yg * r
        dn = d * g_ref[...]
        dyg = r * (dn - n * jnp.mean(dn * n, axis=-1, keepdims=True))
        dy_ref[...] = dyg * sz
        dz_ref[...] = (dyg * y_ * _dsilu(z)).astype(BF16)
        dg_ref[...] += _bsum(jnp.sum(d * n, axis=0, keepdims=True))

    return _pc(body, grid=(G, T // tm),
               in_specs=[pl.BlockSpec((tm, gw), lambda g, i: (i, g)), pl.BlockSpec((tm, gw), lambda g, i: (i, zb + g)),
                         pl.BlockSpec((1, gw), lambda g, i: (0, g)), pl.BlockSpec((tm, gw), lambda g, i: (i, g))],
               out_specs=[pl.BlockSpec((tm, gw), lambda g, i: (i, g)), pl.BlockSpec((tm, gw), lambda g, i: (i, g)),
                          pl.BlockSpec((8, gw), lambda g, i: (0, g))],
               out_shape=[SDS((T, DI), F32), SDS((T, DI), BF16), SDS((8, DI), F32)], name=name)(y, zarr, ng, dyn)


def _shortconv_fwd(proj, off_b, off_c, off_h, C, w8, S, name):
    K = SC_CONV_K
    _, ob, T, tm, tc, tps = _conv_geom((proj, off_b), C, S)
    oc, oh = off_c // tc, off_h // tc

    def body(b_ref, c_ref, h_ref, cp_ref, hp_ref, w_ref, o_ref, buf):
        first = (pl.program_id(0) % tps) == 0
        v = c_ref[...] * h_ref[...]
        sh = _shifted(buf, v, jnp.where(first, 0.0, cp_ref[...] * hp_ref[...]), tm)
        acc = v * w_ref[K - 1:K, :]
        for j in range(1, K):
            acc = acc + sh(j) * w_ref[K - 1 - j:K - j, :]
        o_ref[...] = (b_ref[...] * acc).astype(BF16)

    blk = lambda o: pl.BlockSpec((tm, tc), lambda i, j: (i, o + j))
    return _pc(body, grid=(T // tm, C // tc),
               in_specs=[blk(ob), blk(oc), blk(oh), _prev_spec(tm, tc, oc, "ij"), _prev_spec(tm, tc, oh, "ij"),
                         pl.BlockSpec((8, tc), lambda i, j: (0, j))],
               out_specs=pl.BlockSpec((tm, tc), lambda i, j: (i, j)), out_shape=SDS((T, C), BF16), name=name,
               scratch=(pltpu.VMEM((tm + 8, tc), F32),))(proj, proj, proj, proj, proj, w8)


def _shortconv_bwd(proj, off_b, off_c, off_h, C, w8, dsc, S, name):
    K = SC_CONV_K
    _, ob, T, tm, tc, tps = _conv_geom((proj, off_b), C, S)
    oc, oh = off_c // tc, off_h // tc

    def body(b_ref, c_ref, h_ref, cp_ref, hp_ref, bn_ref, d_ref, dn_ref, w_ref,
             db_ref, dc_ref, dh_ref, dw_ref, buf, buf2):
        i = pl.program_id(1)

        @pl.when(i == 0)
        def _():
            dw_ref[...] = jnp.zeros_like(dw_ref)

        first = (i % tps) == 0
        last = (i % tps) == tps - 1
        b_, c_, h_, d = b_ref[...], c_ref[...], h_ref[...], d_ref[...]
        v = c_ * h_
        sh = _shifted(buf, v, jnp.where(first, 0.0, cp_ref[...] * hp_ref[...]), tm)
        vs = [v] + [sh(j) for j in range(1, K)]
        conv = vs[0] * w_ref[K - 1:K, :]
        for j in range(1, K):
            conv = conv + vs[j] * w_ref[K - 1 - j:K - j, :]
        db_ref[...] = (d * conv).astype(BF16)
        dconv = d * b_
        buf2[0:tm, :] = dconv
        buf2[tm:tm + 8, :] = jnp.where(last, 0.0, dn_ref[...] * bn_ref[...])
        dv = dconv * w_ref[K - 1:K, :]
        for j in range(1, K):
            dv = dv + buf2[pl.ds(j, tm), :] * w_ref[K - 1 - j:K - j, :]
        dc_ref[...] = (dv * h_).astype(BF16)
        dh_ref[...] = (dv * c_).astype(BF16)
        rows = [jnp.sum(dconv * vs[K - 1 - k], axis=0, keepdims=True) for k in range(K)]
        rows.append(jnp.zeros((8 - K, tc), F32))
        dw_ref[...] += jnp.concatenate(rows, axis=0)

    blk = lambda o: pl.BlockSpec((tm, tc), lambda j, i: (i, o + j))
    out = pl.BlockSpec((tm, tc), lambda j, i: (i, j))
    return _pc(body, grid=(C // tc, T // tm),
               in_specs=[blk(ob), blk(oc), blk(oh), _prev_spec(tm, tc, oc, "ji"), _prev_spec(tm, tc, oh, "ji"),
                         _next_spec(T, tm, tc, ob, "ji"), blk(0), _next_spec(T, tm, tc, 0, "ji"),
                         pl.BlockSpec((8, tc), lambda j, i: (0, j))],
               out_specs=[out, out, out, pl.BlockSpec((8, tc), lambda j, i: (0, j))],
               out_shape=[SDS((T, C), BF16)] * 3 + [SDS((8, C), F32)], name=name,
               scratch=(pltpu.VMEM((tm + 8, tc), F32), pltpu.VMEM((tm + 8, tc), F32)))(
                   proj, proj, proj, proj, proj, proj, dsc, dsc, w8)


def _merge_fwd(proj, off_g1, off_g2, y1, y2, S, name):
    T, D = y1.shape
    tm = _row_tile(S)
    o1, o2 = off_g1 // D, off_g2 // D
    assert off_g1 % D == 0 and off_g2 % D == 0

    def body(g1_ref, g2_ref, y1_ref, y2_ref, o_ref):
        o_ref[...] = (jax.nn.sigmoid(g1_ref[...]) * y1_ref[...] + jax.nn.sigmoid(g2_ref[...]) * y2_ref[...]).astype(BF16)

    row = pl.BlockSpec((tm, D), lambda i: (i, 0))
    return _pc(body, grid=(T // tm,),
               in_specs=[pl.BlockSpec((tm, D), lambda i: (i, o1)), pl.BlockSpec((tm, D), lambda i: (i, o2)), row, row],
               out_specs=row, out_shape=SDS((T, D), BF16), name=name)(proj, proj, y1, y2)


def _merge_bwd(proj, off_g1, off_g2, y1, y2, dm, S, name):
    T, D = y1.shape
    tm = _row_tile(S)
    o1, o2 = off_g1 // D, off_g2 // D

    def body(g1_ref, g2_ref, y1_ref, y2_ref, d_ref, dy1_ref, dy2_ref, dg1_ref, dg2_ref):
        d = d_ref[...]
        s1, s2 = jax.nn.sigmoid(g1_ref[...]), jax.nn.sigmoid(g2_ref[...])
        dy1_ref[...] = (d * s1).astype(BF16)
        dy2_ref[...] = (d * s2).astype(BF16)
        dg1_ref[...] = (d * y1_ref[...] * s1 * (1.0 - s1)).astype(BF16)
        dg2_ref[...] = (d * y2_ref[...] * s2 * (1.0 - s2)).astype(BF16)

    row = pl.BlockSpec((tm, D), lambda i: (i, 0))
    return _pc(body, grid=(T // tm,),
               in_specs=[pl.BlockSpec((tm, D), lambda i: (i, o1)), pl.BlockSpec((tm, D), lambda i: (i, o2)), row, row, row],
               out_specs=[row] * 4, out_shape=[SDS((T, D), BF16)] * 4, name=name)(proj, proj, y1, y2, dm)


def _pad8(w):
    return jnp.pad(w, ((0, 8 - w.shape[0]), (0, 0)))


def _dims(w):
    D = w["mix_pre_g"].shape[-1]
    DI = w["ssd_norm_g"].shape[-1]
    H = w["ssd_dt_bias"].shape[-1]
    conv_dim = w["ssd_conv_b"].shape[-1]
    G = (conv_dim - DI) // (2 * N_STATE)
    F = w["w_down"].shape[0]
    return dict(D=D, DI=DI, H=H, P=DI // H, G=G, R=H // G, GN=G * N_STATE, CD=conv_dim, F=F)


def _proj_layout(d):
    D, DI, CD, H = d["D"], d["DI"], d["CD"], d["H"]
    o = dict(z=0, xbc=DI, scb=DI + CD, scc=DI + CD + D, sch=DI + CD + 2 * D, g1=DI + CD + 3 * D, g2=DI + CD + 4 * D,
             dt=DI + CD + 5 * D)
    used = o["dt"] + H
    o["np"] = -(-used // 128) * 128
    return o


def _glu_perm(a, F, inverse=False):
    lead = a.shape[:-1]
    nb = F // GLU_W
    if not inverse:
        return a.reshape(*lead, 2, nb, GLU_W).swapaxes(-3, -2).reshape(*lead, 2 * F)
    return a.reshape(*lead, nb, 2, GLU_W).swapaxes(-3, -2).reshape(*lead, 2 * F)


def _glu_perm_rows(a, F, inverse=False):
    nb, D = F // GLU_W, a.shape[1]
    shape = (nb, 2, GLU_W, D) if inverse else (2, nb, GLU_W, D)
    return a.reshape(shape).swapaxes(0, 1).reshape(2 * F, D)


def _prep_layer(w):
    d = _dims(w)
    D, DI, CD, H, G, R, F = d["D"], d["DI"], d["CD"], d["H"], d["G"], d["R"], d["F"]
    lay = _proj_layout(d)
    w_in = w["w_in"]
    used = lay["dt"] + H
    wcat = jnp.concatenate([w_in[:DI + CD], w_in[DI + CD + H:], w_in[DI + CD:DI + CD + H],
                            jnp.zeros((lay["np"] - used, D), w_in.dtype)], axis=0)
    hp = jnp.stack([w["ssd_dt_bias"], w["ssd_a_log"], w["ssd_d"]], 0).astype(F32)
    hpc = jnp.pad(hp.reshape(3, G, R).transpose(1, 0, 2), ((0, 0), (0, 5), (0, 0)))
    hpr = jnp.pad(hp[:2].reshape(2, G, R).transpose(1, 2, 0), ((0, 0), (0, 0), (0, 6)))
    row = lambda v: v.reshape(1, -1).astype(F32)
    return dict(
        d=d, lay=lay, ada_w=w["ada_w"].astype(BF16), ada_b=row(w["ada_b"]),
        mix_pre_g=row(w["mix_pre_g"]), mix_post_g=row(w["mix_post_g"]), wcat=wcat.astype(BF16),
        ssd_conv_w=_pad8(w["ssd_conv_w"].astype(F32)), ssd_conv_b=row(w["ssd_conv_b"]), hpc=hpc, hpr=hpr,
        ssd_norm_g=row(w["ssd_norm_g"]), w_ssd_out=w["w_ssd_out"].astype(BF16),
        sc_conv_w=_pad8(w["sc_conv_w"].astype(F32)), w_sc_out=w["w_sc_out"].astype(BF16), w_o=w["w_o"].astype(BF16),
        ffn_pre_g=row(w["ffn_pre_g"]), ffn_post_g=row(w["ffn_post_g"]),
        w_up=_glu_perm_rows(w["w_up"], F).astype(BF16), ffn_conv_w=_pad8(_glu_perm(w["ffn_conv_w"].astype(F32), F)),
        ffn_conv_b=_glu_perm(row(w["ffn_conv_b"]), F), w_down=w["w_down"].astype(BF16))


def _dt_layouts(proj, lay, d):
    T = proj.shape[0]
    dt = proj[:, lay["dt"]:lay["dt"] + d["H"]].reshape(T, d["G"], d["R"])
    return dt.transpose(1, 0, 2), dt.transpose(1, 2, 0)


def _layer_fwd(x, mod3, p, S, li):
    d, lay = p["d"], p["lay"]
    D, DI, G, R, P, GN, CD = d["D"], d["DI"], d["G"], d["R"], d["P"], d["GN"], d["CD"]
    nm = lambda s: f"l{li}_{s}"
    h = _norm_mod(x, p["mix_pre_g"], mod3, 1, 0, S, nm("norm1"))
    proj = _mm(h, p["wcat"], "nt", F32, nm("mm_in"), caps=(1024, 1152, 2048))
    pre = _conv_fwd((proj, lay["xbc"]), CD, p["ssd_conv_w"], p["ssd_conv_b"], SSD_CONV_K, S, nm("ssdconv"))
    dtc, dtr = _dt_layouts(proj, lay, d)
    offs = (0, DI, DI + GN)
    y, hs = _ssd_fwd(pre, offs, dtc, dtr, p["hpc"], p["hpr"], G, R, P, S, nm("ssd"))
    yn = _gate_norm_fwd(y, (proj, lay["z"]), p["ssd_norm_g"], G, S, nm("gnorm"))
    sc = _shortconv_fwd(proj, lay["scb"], lay["scc"], lay["sch"], D, p["sc_conv_w"], S, nm("sconv"))
    y_ssd = _mm(yn, p["w_ssd_out"], "nn", F32, nm("mm_ssdout"))
    y_sc = _mm(sc, p["w_sc_out"], "nn", F32, nm("mm_scout"))
    m = _merge_fwd(proj, lay["g1"], lay["g2"], y_ssd, y_sc, S, nm("merge"))
    mix = _mm(m, p["w_o"], "nn", F32, nm("mm_o"))
    x1 = _resid_post(x, mix, mod3, 2, p["mix_post_g"], S, nm("post1"))
    h2 = _norm_mod(x1, p["ffn_pre_g"], mod3, 4, 3, S, nm("norm2"))
    uu = _mm(h2, p["w_up"], "nt", F32, nm("mm_up"), caps=(1024, 1408, 2048))
    a = _ffn_act_fwd(uu, p["ffn_conv_w"], p["ffn_conv_b"], S, nm("ffnact"))
    f = _mm(a, p["w_down"], "nn", F32, nm("mm_down"), caps=(1024, 1024, 1408))
    x2 = _resid_post(x1, f, mod3, 5, p["ffn_post_g"], S, nm("post2"))
    saved = dict(x=x, h=h, proj=proj, pre=pre, dtc=dtc, dtr=dtr, y=y, hs=hs, yn=yn, sc=sc, y_ssd=y_ssd, y_sc=y_sc,
                 m=m, mix=mix, x1=x1, h2=h2, uu=uu, a=a, f=f)
    return x2, saved


def _seq_sum(acc, nb):
    return acc.reshape(nb, 8, -1)[:, 0, :]


def _layer_bwd(dx2, mod3, p, s, S, li):
    d, lay = p["d"], p["lay"]
    D, DI, G, R, P, GN, CD, H, F = d["D"], d["DI"], d["G"], d["R"], d["P"], d["GN"], d["CD"], d["H"], d["F"]
    nb = dx2.shape[0] // S
    nm = lambda t: f"l{li}_{t}"
    g = {}
    df, dgt2, dpg2 = _post_bwd(s["f"], mod3, 5, p["ffn_post_g"], dx2, S, nm("post2_b"))
    g["ffn_post_g"] = dpg2[0]
    da = _mm(df, p["w_down"], "nt", BF16, nm("mm_down_bi"), caps=(1024, 1408, 2048))
    g["w_down"] = _mm(s["a"], df, "tn", F32, nm("mm_down_bw"), caps=(1408, 1024, 1024))
    duu, cw = _ffn_act_bwd(s["uu"], da, p["ffn_conv_w"], p["ffn_conv_b"], S, nm("ffnact_b"))
    g["ffn_conv_w"] = _glu_perm(cw[:FFN_CONV_K], F, inverse=True)
    g["ffn_conv_b"] = _glu_perm(cw[FFN_CONV_K], F, inverse=True)
    dh2 = _mm(duu, p["w_up"], "nn", F32, nm("mm_up_bi"), caps=(1024, 1024, 1408))
    g["w_up"] = _glu_perm_rows(_mm(duu, s["h2"], "tn", F32, nm("mm_up_bw"), caps=(1408, 1024, 1024)), F, inverse=True)
    dx1, dg2, dsc2, dsh2 = _pre_bwd(s["x1"], p["ffn_pre_g"], mod3, 4, dh2, dx2, S, nm("norm2_b"))
    g["ffn_pre_g"] = dg2[0]
    dmix, dgt1, dpg1 = _post_bwd(s["mix"], mod3, 2, p["mix_post_g"], dx1, S, nm("post1_b"))
    g["mix_post_g"] = dpg1[0]
    dm = _mm(dmix, p["w_o"], "nt", F32, nm("mm_o_bi"))
    g["w_o"] = _mm(s["m"], dmix, "tn", F32, nm("mm_o_bw"))
    proj = s["proj"]
    dy_ssd, dy_sc, dg1, dg2_ = _merge_bwd(proj, lay["g1"], lay["g2"], s["y_ssd"], s["y_sc"], dm, S, nm("merge_b"))
    dyn = _mm(dy_ssd, p["w_ssd_out"], "nt", F32, nm("mm_ssdout_bi"))
    g["w_ssd_out"] = _mm(s["yn"], dy_ssd, "tn", F32, nm("mm_ssdout_bw"))
    dsc = _mm(dy_sc, p["w_sc_out"], "nt", F32, nm("mm_scout_bi"))
    g["w_sc_out"] = _mm(s["sc"], dy_sc, "tn", F32, nm("mm_scout_bw"))
    dscb, dscc, dsch, scw = _shortconv_bwd(proj, lay["scb"], lay["scc"], lay["sch"], D, p["sc_conv_w"], dsc, S, nm("sconv_b"))
    g["sc_conv_w"] = scw[:SC_CONV_K]
    dy, dz, dng = _gate_norm_bwd(s["y"], (proj, lay["z"]), p["ssd_norm_g"], dyn, G, S, nm("gnorm_b"))
    g["ssd_norm_g"] = dng[0]
    offs = (0, DI, DI + GN)
    dpx, dpb, dpc, ddt, hpg = _ssd_bwd(s["pre"], offs, s["dtc"], s["dtr"], p["hpc"], p["hpr"], s["hs"], dy,
                                       G, R, P, S, nm("ssd_b"))
    g["ssd_dt_bias"], g["ssd_a_log"], g["ssd_d"] = hpg[:, 0, :].reshape(H), hpg[:, 1, :].reshape(H), hpg[:, 2, :].reshape(H)
    cws, dxbc = [], []
    for name, darr, off, C in (("x", dpx, 0, DI), ("b", dpb, DI, GN), ("c", dpc, DI + GN, GN)):
        w8 = p["ssd_conv_w"][:, off:off + C]
        cws.append(_conv_bwd_w((darr, 0), (proj, lay["xbc"] + off), C, SSD_CONV_K, S, nm(f"ssdconv_bw_{name}")))
        dxbc.append(_conv_bwd_in((darr, 0), C, w8, SSD_CONV_K, S, BF16, nm(f"ssdconv_bi_{name}")))
    cws = jnp.concatenate(cws, axis=1)
    g["ssd_conv_w"], g["ssd_conv_b"] = cws[:SSD_CONV_K], cws[SSD_CONV_K]
    T = dx2.shape[0]
    ddt_t = ddt.transpose(1, 0, 2).reshape(T, H).astype(BF16)
    used = lay["dt"] + H
    dproj = jnp.concatenate([dz] + dxbc + [dscb, dscc, dsch, dg1, dg2_, ddt_t,
                                           jnp.zeros((T, lay["np"] - used), BF16)], axis=1)
    dh = _mm(dproj, p["wcat"], "nn", F32, nm("mm_in_bi"), caps=(1024, 1024, 1152))
    dwcat = _mm(dproj, s["h"], "tn", F32, nm("mm_in_bw"), caps=(1152, 1024, 1024))
    o = lay
    g["w_in"] = jnp.concatenate([dwcat[o["z"]:o["scb"]], dwcat[o["dt"]:o["dt"] + H], dwcat[o["scb"]:o["dt"]]], axis=0)
    dx, dg1_, dsc1, dsh1 = _pre_bwd(s["x"], p["mix_pre_g"], mod3, 1, dh, dx1, S, nm("norm1_b"))
    g["mix_pre_g"] = dg1_[0]
    dmod = jnp.concatenate([_seq_sum(t, nb) for t in (dsh1, dsc1, dgt1, dsh2, dsc2, dgt2)], axis=1)
    return dx, dmod, g


def _fwd_bwd(x3, c, target3, layers):
    nb, S, D = x3.shape
    T = nb * S
    x = x3.reshape(T, D)
    c8 = jnp.pad(c, ((0, MOD_ROWS - nb), (0, 0)))
    preps = [_prep_layer(w) for w in layers]
    saved, mods, cact = [], [], None
    for li, p in enumerate(preps):
        mod, cact = _modk(c8, p["ada_w"], p["ada_b"], f"l{li}_mod")
        mod3 = mod[:nb].reshape(nb, 1, 6 * D)
        x, s = _layer_fwd(x, mod3, p, S, li)
        saved.append(s)
        mods.append(mod3)
    dy, lacc = _loss(x, target3.reshape(T, D), S, "loss")
    grads = [None] * len(preps)
    for li in reversed(range(len(preps))):
        dy, dmod, g = _layer_bwd(dy, mods[li], preps[li], saved[li], S, li)
        dmod8 = jnp.pad(dmod, ((0, MOD_ROWS - nb), (0, 0)))
        g["ada_b"] = _colsum(dmod8, f"l{li}_adab")
        g["ada_w"] = _mm(dmod8, cact, "tn", F32, f"l{li}_mm_ada_bw", caps=(1536, 1024, 2048))
        grads[li] = g
    return lacc[0, 0], dy.reshape(nb, S, D), grads


def _colsum(a8, name):
    rows, C = a8.shape
    tc = _tile(C, 2048)

    def body(a_ref, o_ref):
        o_ref[...] = _bsum(jnp.sum(a_ref[...], axis=0, keepdims=True))

    return _pc(body, grid=(C // tc,), in_specs=[pl.BlockSpec((rows, tc), lambda j: (0, j))],
               out_specs=pl.BlockSpec((8, tc), lambda j: (0, j)), out_shape=SDS((8, C), F32), name=name)(a8)[0]


def _adam(gs, w, m, v, name):
    ns, R, W = gs.shape
    tr = _tile(R, 256, 8)
    c1 = 1.0 / (1.0 - ADAM_B1 ** ADAM_STEP)
    c2 = 1.0 / (1.0 - ADAM_B2 ** ADAM_STEP)

    def body(g_ref, w_ref, m_ref, v_ref, go_ref, d_ref, mo_ref, vo_ref):
        g = g_ref[0].astype(F32)
        for k in range(1, ns):
            g = g + g_ref[k].astype(F32)
        m_ = ADAM_B1 * m_ref[...] + (1.0 - ADAM_B1) * g
        v_ = ADAM_B2 * v_ref[...] + (1.0 - ADAM_B2) * (g * g)
        go_ref[...] = g
        mo_ref[...] = m_
        vo_ref[...] = v_
        d_ref[...] = -ADAM_LR * ((m_ * c1) / (jnp.sqrt(v_ * c2) + ADAM_EPS) + ADAM_WD * w_ref[...])

    row = pl.BlockSpec((tr, W), lambda i: (i, 0))
    return _pc(body, grid=(R // tr,), in_specs=[pl.BlockSpec((ns, tr, W), lambda i: (0, i, 0)), row, row, row],
               out_specs=[row] * 4, out_shape=[SDS((R, W), F32)] * 4, name=name)(gs, w, m, v)


HBM_SPEC = pl.BlockSpec(memory_space=pltpu.HBM)
VMEM_SPEC = pl.BlockSpec(memory_space=pltpu.VMEM)


def _dev():
    return lax.axis_index("x"), lax.axis_index("y"), lax.axis_index("c")


def _allgather_big(loc, name):
    R, W = loc.shape

    def body(x_ref, out_ref, send_sems, recv_sems, local_sem):
        x, y, c = _dev()
        me, sibling = (x, y, c), (x, y, 1 - c)
        chips = [(1 - x, y), (x, 1 - y), (1 - x, 1 - y)]

        def slab(px, py, pc):
            return out_ref.at[4 * px + 2 * py + pc]

        def copy(k, block, to, src=None):
            return pltpu.make_async_remote_copy(
                src_ref=slab(*block) if src is None else src, dst_ref=slab(*block),
                send_sem=send_sems.at[k], recv_sem=recv_sems.at[k], device_id=to, device_id_type=MESH)

        mine = pltpu.make_async_copy(x_ref, slab(*me), local_sem)
        mine.start()
        first = [copy(0, me, sibling, src=x_ref)]
        first += [copy(1 + j, me, (*chip, c), src=x_ref) for j, chip in enumerate(chips)]
        for cp in first:
            cp.start()
        passed = [copy(4 + j, (*chip, c), sibling) for j, chip in enumerate(chips)]
        for j, chip in enumerate(chips):
            copy(1 + j, (*chip, c), me).wait_recv()
            passed[j].start()
        copy(0, sibling, me).wait_recv()
        for j, chip in enumerate(chips):
            copy(4 + j, (*chip, 1 - c), me).wait_recv()
        for cp in first + passed:
            cp.wait_send()
        mine.wait()

    return pl.pallas_call(
        body, out_shape=SDS((N_DEV, R, W), loc.dtype), in_specs=[HBM_SPEC], out_specs=HBM_SPEC,
        scratch_shapes=[pltpu.SemaphoreType.DMA((7,)), pltpu.SemaphoreType.DMA((7,)), pltpu.SemaphoreType.DMA],
        name=name)(loc)


def _rs_pair_exchange(gc, name):
    _, nj, R, W = gc.shape

    def body(g_ref, out_ref, send_sem, recv_sem):
        x, y, c = _dev()
        cp = pltpu.make_async_remote_copy(src_ref=g_ref.at[1 - c], dst_ref=out_ref, send_sem=send_sem, recv_sem=recv_sem,
                                          device_id=(x, y, 1 - c), device_id_type=MESH)
        cp.start()
        cp.wait()

    return pl.pallas_call(
        body, out_shape=SDS((nj, R, W), gc.dtype), in_specs=[HBM_SPEC], out_specs=HBM_SPEC,
        scratch_shapes=[pltpu.SemaphoreType.DMA, pltpu.SemaphoreType.DMA], name=name)(gc)


def _add_pairs(gc, ra, name):
    _, nj, R, W = gc.shape
    tr = _tile(R, 256, 8)
    cidx = lax.axis_index("c").astype(jnp.int32).reshape(1)

    def body(c_ref, a_ref, b_ref, o_ref):
        o_ref[...] = (a_ref[...].astype(F32) + b_ref[...].astype(F32)).astype(o_ref.dtype)

    gs = pltpu.PrefetchScalarGridSpec(
        num_scalar_prefetch=1, grid=(nj, R // tr),
        in_specs=[pl.BlockSpec((None, None, tr, W), lambda j, i, cr: (cr[0], j, i, 0)),
                  pl.BlockSpec((None, tr, W), lambda j, i, cr: (j, i, 0))],
        out_specs=pl.BlockSpec((None, tr, W), lambda j, i, cr: (j, i, 0)))
    return pl.pallas_call(body, grid_spec=gs, out_shape=SDS((nj, R, W), gc.dtype), name=name,
                          compiler_params=pltpu.CompilerParams(vmem_limit_bytes=VMEM_LIMIT))(cidx, gc, ra)


def _rs_chip_exchange(p, name):
    nj, R, W = p.shape

    def body(p_ref, out_ref, send_sems, recv_sems, local_sem):
        x, y, c = _dev()
        j0 = 2 * x + y
        chips = [(1 - x, y), (x, 1 - y), (1 - x, 1 - y)]
        mine = pltpu.make_async_copy(p_ref.at[j0], out_ref.at[j0], local_sem)
        mine.start()

        def copy(k, chip):
            return pltpu.make_async_remote_copy(
                src_ref=p_ref.at[2 * chip[0] + chip[1]], dst_ref=out_ref.at[j0],
                send_sem=send_sems.at[k], recv_sem=recv_sems.at[k], device_id=(*chip, c), device_id_type=MESH)

        sent = [copy(k, chip) for k, chip in enumerate(chips)]
        for cp in sent:
            cp.start()
        for k, chip in enumerate(chips):
            pltpu.make_async_remote_copy(
                src_ref=p_ref.at[j0], dst_ref=out_ref.at[2 * chip[0] + chip[1]],
                send_sem=send_sems.at[k], recv_sem=recv_sems.at[k], device_id=(*chip, c), device_id_type=MESH).wait_recv()
        for cp in sent:
            cp.wait_send()
        mine.wait()

    return pl.pallas_call(
        body, out_shape=SDS((nj, R, W), p.dtype), in_specs=[HBM_SPEC], out_specs=HBM_SPEC,
        scratch_shapes=[pltpu.SemaphoreType.DMA((3,)), pltpu.SemaphoreType.DMA((3,)), pltpu.SemaphoreType.DMA],
        name=name)(p)


def _allgather_small(v, name):
    R, W = v.shape

    def body(v_ref, out_ref, send_sems, recv_sems, local_sem):
        x, y, c = _dev()
        mine = pltpu.make_async_copy(v_ref, out_ref.at[4 * x + 2 * y + c], local_sem)
        mine.start()
        peers = []
        for k in range(1, N_DEV):
            px = 1 - x if k & 4 else x
            py = 1 - y if k & 2 else y
            pc_ = 1 - c if k & 1 else c
            peers.append((px, py, pc_))
        sent = [pltpu.make_async_remote_copy(
            src_ref=v_ref, dst_ref=out_ref.at[4 * x + 2 * y + c], send_sem=send_sems.at[k], recv_sem=recv_sems.at[k],
            device_id=peer, device_id_type=MESH) for k, peer in enumerate(peers)]
        for cp in sent:
            cp.start()
        for k, (px, py, pc_) in enumerate(peers):
            pltpu.make_async_remote_copy(
                src_ref=v_ref, dst_ref=out_ref.at[4 * px + 2 * py + pc_], send_sem=send_sems.at[k],
                recv_sem=recv_sems.at[k], device_id=(px, py, pc_), device_id_type=MESH).wait_recv()
        for cp in sent:
            cp.wait_send()
        mine.wait()

    return pl.pallas_call(
        body, out_shape=SDS((N_DEV, R, W), v.dtype), in_specs=[VMEM_SPEC], out_specs=VMEM_SPEC,
        scratch_shapes=[pltpu.SemaphoreType.DMA((7,)), pltpu.SemaphoreType.DMA((7,)), pltpu.SemaphoreType.DMA],
        name=name)(v)


def _sum_slabs(a, name):
    ns, R, W = a.shape

    def body(a_ref, o_ref):
        acc = a_ref[0]
        for k in range(1, ns):
            acc = acc + a_ref[k]
        o_ref[...] = acc

    return pl.pallas_call(body, out_shape=SDS((R, W), a.dtype), in_specs=[VMEM_SPEC], out_specs=VMEM_SPEC, name=name)(a)


BIG = (("ada_w", "col"), ("w_in", "col"), ("w_ssd_out", "row"), ("w_sc_out", "row"), ("w_o", "row"), ("w_up", "col"),
       ("w_down", "row"))
CONVW = ("ssd_conv_w", "sc_conv_w", "ffn_conv_w")
REPL = ("ada_b", "mix_pre_g", "mix_post_g", "ssd_conv_b", "ssd_dt_bias", "ssd_a_log", "ssd_d", "ssd_norm_g", "ffn_pre_g",
        "ffn_post_g", "ffn_conv_b")
WEIGHTS = ("ada_w", "ada_b", "mix_pre_g", "mix_post_g", "w_in", "ssd_conv_w", "ssd_conv_b", "ssd_dt_bias", "ssd_a_log",
           "ssd_d", "ssd_norm_g", "w_ssd_out", "sc_conv_w", "w_sc_out", "w_o", "ffn_pre_g", "ffn_post_g", "w_up",
           "ffn_conv_w", "ffn_conv_b", "w_down")


def _pad_rows(a, mult):
    r = a.shape[-2]
    pad = -r % mult
    return a if pad == 0 else jnp.pad(a, [(0, 0)] * (a.ndim - 2) + [(0, pad), (0, 0)])


def _flat_rows(parts, mult):
    flat = jnp.concatenate([p.reshape(-1) for p in parts])
    flat = jnp.pad(flat, (0, -flat.shape[0] % ROW_W))
    return _pad_rows(flat.reshape(-1, ROW_W), mult)


def _unflat(buf, shapes):
    flat = buf.reshape(-1)
    out, o = [], 0
    for shp in shapes:
        n = 1
        for s in shp:
            n *= s
        out.append(flat[o:o + n].reshape(shp))
        o += n
    return out


def _pack_big_local(get, depth):
    return jnp.concatenate([(get(n)[l].T if kind == "col" else get(n)[l]).reshape(-1, ROW_W)
                            for l in range(depth) for n, kind in BIG], axis=0)


def _big_rows(shapes, depth):
    out, o = {}, 0
    for l in range(depth):
        for n, _ in BIG:
            r = shapes[n][1] * shapes[n][2] // ROW_W
            out[(l, n)] = (o, o + r)
            o += r
    return out, o


def kernel(x, c, ada_w, ada_b, mix_pre_g, mix_post_g, w_in, ssd_conv_w, ssd_conv_b, ssd_dt_bias, ssd_a_log, ssd_d, ssd_norm_g, w_ssd_out, sc_conv_w, w_sc_out, w_o, ffn_pre_g, ffn_post_g, w_up, ffn_conv_w, ffn_conv_b, w_down, loss_target, m_ada_w, m_ada_b, m_mix_pre_g, m_mix_post_g, m_w_in, m_ssd_conv_w, m_ssd_conv_b, m_ssd_dt_bias, m_ssd_a_log, m_ssd_d, m_ssd_norm_g, m_w_ssd_out, m_sc_conv_w, m_w_sc_out, m_w_o, m_ffn_pre_g, m_ffn_post_g, m_w_up, m_ffn_conv_w, m_ffn_conv_b, m_w_down, v_ada_w, v_ada_b, v_mix_pre_g, v_mix_post_g, v_w_in, v_ssd_conv_w, v_ssd_conv_b, v_ssd_dt_bias, v_ssd_a_log, v_ssd_d, v_ssd_norm_g, v_w_ssd_out, v_sc_conv_w, v_w_sc_out, v_w_o, v_ffn_pre_g, v_ffn_post_g, v_w_up, v_ffn_conv_w, v_ffn_conv_b, v_w_down):
    wl = dict(zip(WEIGHTS, (ada_w, ada_b, mix_pre_g, mix_post_g, w_in, ssd_conv_w, ssd_conv_b, ssd_dt_bias, ssd_a_log,
                            ssd_d, ssd_norm_g, w_ssd_out, sc_conv_w, w_sc_out, w_o, ffn_pre_g, ffn_post_g, w_up,
                            ffn_conv_w, ffn_conv_b, w_down)))
    ml = dict(zip(WEIGHTS, (m_ada_w, m_ada_b, m_mix_pre_g, m_mix_post_g, m_w_in, m_ssd_conv_w, m_ssd_conv_b,
                            m_ssd_dt_bias, m_ssd_a_log, m_ssd_d, m_ssd_norm_g, m_w_ssd_out, m_sc_conv_w, m_w_sc_out, m_w_o,
                            m_ffn_pre_g, m_ffn_post_g, m_w_up, m_ffn_conv_w, m_ffn_conv_b, m_w_down)))
    vl = dict(zip(WEIGHTS, (v_ada_w, v_ada_b, v_mix_pre_g, v_mix_post_g, v_w_in, v_ssd_conv_w, v_ssd_conv_b,
                            v_ssd_dt_bias, v_ssd_a_log, v_ssd_d, v_ssd_norm_g, v_w_ssd_out, v_sc_conv_w, v_w_sc_out, v_w_o,
                            v_ffn_pre_g, v_ffn_post_g, v_w_up, v_ffn_conv_w, v_ffn_conv_b, v_w_down)))
    depth = ada_w.shape[0]
    shapes = {n: wl[n].shape for n in WEIGHTS}
    me = 4 * lax.axis_index("x") + 2 * lax.axis_index("y") + lax.axis_index("c")

    rows, n_big = _big_rows(shapes, depth)
    big_loc = _pack_big_local(lambda n: wl[n].astype(BF16), depth)
    conv_flat = jnp.concatenate([wl[n][l].reshape(-1) for l in range(depth) for n in CONVW])
    n_conv = conv_flat.shape[0]
    conv_flat = jnp.pad(conv_flat, (0, -n_conv % (ROW_W // 2)))
    conv_rows = lax.bitcast_convert_type(conv_flat, BF16).reshape(-1, ROW_W)
    gathered = _allgather_big(_pad_rows(jnp.concatenate([big_loc, conv_rows], axis=0), ROW_PAD), "allgather_weights")
    conv_all = lax.bitcast_convert_type(
        gathered[:, n_big:n_big + conv_rows.shape[0]].reshape(N_DEV, -1, 2), F32)[:, :n_conv]
    conv_full, o = {}, 0
    for l in range(depth):
        for n in CONVW:
            k, cl = shapes[n][1], shapes[n][2]
            conv_full[(l, n)] = conv_all[:, o:o + k * cl].reshape(N_DEV, k, cl).transpose(1, 0, 2).reshape(k, N_DEV * cl)
            o += k * cl
    layers = []
    for l in range(depth):
        w = {n: wl[n][l] for n in REPL}
        for n, kind in BIG:
            r0, r1 = rows[(l, n)]
            a, b = shapes[n][1], shapes[n][2]
            blk = gathered[:, r0:r1]
            w[n] = blk.reshape(N_DEV * b, a) if kind == "col" else blk.reshape(N_DEV * a, b)
        for n in CONVW:
            w[n] = conv_full[(l, n)]
        layers.append(w)

    loss_loc, dx, grads = _fwd_bwd(x, c, loss_target, layers)

    slabs = []
    for l in range(depth):
        for n, kind in BIG:
            slabs.append(grads[l][n].astype(BF16).reshape(N_DEV, -1, ROW_W))
    gslab = _pad_rows(jnp.concatenate(slabs, axis=1), ROW_PAD)
    rg = gslab.shape[1]
    gc = gslab.reshape(4, 2, rg, ROW_W).transpose(1, 0, 2, 3)
    from_sibling = _rs_pair_exchange(gc, "rs_pair_exchange")
    chip_sums = _add_pairs(gc, from_sibling, "rs_pair_add")
    from_chips = _rs_chip_exchange(chip_sums, "rs_chip_exchange")
    pack_f32 = lambda d: _pad_rows(_pack_big_local(lambda n: d[n], depth), ROW_PAD)
    g_big, d_big, m_big, v_big = _adam(from_chips, pack_f32(wl), pack_f32(ml), pack_f32(vl), "adam_big")

    def unpack_big(buf):
        def one(l, n, kind):
            blk = buf[rows[(l, n)][0]:rows[(l, n)][1]]
            a, b = shapes[n][1], shapes[n][2]
            return blk.reshape(b, a).T if kind == "col" else blk.reshape(a, b)
        return {n: jnp.stack([one(l, n, kind) for l in range(depth)]) for n, kind in BIG}

    parts = [jnp.broadcast_to(loss_loc, (ROW_W,))]
    small_shapes = [(ROW_W,)]
    for l in range(depth):
        for n in REPL + CONVW:
            parts.append(grads[l][n])
            small_shapes.append(tuple(grads[l][n].shape))
    total = _sum_slabs(_allgather_small(_flat_rows(parts, 8), "allgather_small"), "sum_small")
    pieces = _unflat(total, small_shapes)
    loss = pieces[0][0]
    g_small, i = {}, 1
    for l in range(depth):
        for n in REPL + CONVW:
            gp = pieces[i]
            i += 1
            if n in CONVW:
                gp = lax.dynamic_slice_in_dim(gp, me * shapes[n][2], shapes[n][2], axis=1)
            g_small[(l, n)] = gp
    order = [(l, n) for l in range(depth) for n in REPL + CONVW]
    loc_shapes = [tuple(shapes[n][1:]) for _, n in order]
    packs = lambda f: _flat_rows([f(l, n) for l, n in order], 8)
    gs_small = packs(lambda l, n: g_small[(l, n)])
    _, d_sm, m_sm, v_sm = _adam(gs_small[None], packs(lambda l, n: wl[n][l]), packs(lambda l, n: ml[n][l]),
                                packs(lambda l, n: vl[n][l]), "adam_small")

    def unpack_small(buf):
        ps = _unflat(buf, loc_shapes)
        return {n: jnp.stack([ps[order.index((l, n))] for l in range(depth)]) for n in REPL + CONVW}

    outs = []
    for big, small in ((unpack_big(g_big), {n: jnp.stack([g_small[(l, n)] for l in range(depth)]) for n in REPL + CONVW}),
                       (unpack_big(d_big), unpack_small(d_sm)), (unpack_big(m_big), unpack_small(m_sm)),
                       (unpack_big(v_big), unpack_small(v_sm))):
        merged = {**big, **small}
        outs += [merged[n] for n in WEIGHTS]
    return (loss, dx, *outs)
```

```python
import functools

import jax
import jax.numpy as jnp
from jax import lax
from jax.experimental import pallas as pl
from jax.experimental.pallas import tpu as pltpu

F32, BF16 = jnp.float32, jnp.bfloat16
WGRAD = BF16
SDS = jax.ShapeDtypeStruct
MESH = pl.DeviceIdType.MESH

EPS = 1e-6
N_STATE = 128
CHUNK = 128
SSD_CONV_K, SC_CONV_K, FFN_CONV_K = 4, 3, 3
N_DEV = 8
ROW_W = 1024
ROW_PAD = 256
SLAB_ALIGN = 16
GLU_W = 256
MOD_ROWS = 128
VMEM_LIMIT = 48 * 2**20

ADAM_LR, ADAM_B1, ADAM_B2, ADAM_EPS, ADAM_WD, ADAM_STEP = 0.001, 0.9, 0.999, 1e-08, 0.01, 10

NT = (((1,), (1,)), ((), ()))
TN = (((0,), (0,)), ((), ()))
NN = (((1,), (0,)), ((), ()))


def _tile(n, cap, mult=128):
    best = None
    for t in range(mult, min(n, cap) + 1, mult):
        if n % t == 0:
            best = t
    return best if best is not None else n


def _pc(body, *, grid, in_specs, out_specs, out_shape, name, scratch=()):
    return pl.pallas_call(
        body, grid=grid, in_specs=in_specs, out_specs=out_specs, out_shape=out_shape,
        scratch_shapes=list(scratch), name=name,
        compiler_params=pltpu.CompilerParams(
            dimension_semantics=("arbitrary",) * len(grid), vmem_limit_bytes=VMEM_LIMIT))


def _silu(x):
    return x * jax.nn.sigmoid(x)


def _dsilu(x):
    s = jax.nn.sigmoid(x)
    return s * (1.0 + x * (1.0 - s))


def _softplus(x):
    return jnp.maximum(x, 0.0) + jnp.log(1.0 + jnp.exp(-jnp.abs(x)))


def _dot(a, b, dims=NN):
    return lax.dot_general(a, b, dims, preferred_element_type=F32)


def _bsum(v, rows=8):
    return jnp.broadcast_to(v, (rows, v.shape[1]))


def _mm(a, b, mode, out_dtype, name, caps=(1024, 1024, 2048)):
    if mode == "nn":
        (M, K), (K2, N) = a.shape, b.shape
    elif mode == "nt":
        (M, K), (N, K2) = a.shape, b.shape
    else:
        (K, M), (K2, N) = a.shape, b.shape
    assert K == K2, (a.shape, b.shape, mode)
    tm, tn, tk = _tile(M, caps[0]), _tile(N, caps[1]), _tile(K, caps[2])
    nk = K // tk
    dims = {"nn": NN, "nt": NT, "tn": TN}[mode]
    if mode == "tn":
        a_spec = pl.BlockSpec((tk, tm), lambda i, j, k: (k, i))
    else:
        a_spec = pl.BlockSpec((tm, tk), lambda i, j, k: (i, k))
    if mode == "nt":
        b_spec = pl.BlockSpec((tn, tk), lambda i, j, k: (j, k))
    else:
        b_spec = pl.BlockSpec((tk, tn), lambda i, j, k: (k, j))

    def body(a_ref, b_ref, o_ref, *acc):
        part = _dot(a_ref[...].astype(BF16), b_ref[...].astype(BF16), dims)
        if nk == 1:
            o_ref[...] = part.astype(o_ref.dtype)
        else:
            acc_ref, = acc
            k = pl.program_id(2)

            @pl.when(k == 0)
            def _():
                acc_ref[...] = part

            @pl.when(k > 0)
            def _():
                acc_ref[...] += part

            @pl.when(k == nk - 1)
            def _():
                o_ref[...] = acc_ref[...].astype(o_ref.dtype)

    return _pc(body, grid=(M // tm, N // tn, nk), in_specs=[a_spec, b_spec],
               out_specs=pl.BlockSpec((tm, tn), lambda i, j, k: (i, j)),
               out_shape=SDS((M, N), out_dtype), name=name,
               scratch=() if nk == 1 else (pltpu.VMEM((tm, tn), F32),))(a, b)


def _modk(c8, ada_w, ada_b, name):
    rows, D = c8.shape
    N = ada_w.shape[0]
    tn = _tile(N, 1536)

    def body(c_ref, w_ref, b_ref, mod_ref, ca_ref):
        ca = _silu(c_ref[...]).astype(BF16)
        mod_ref[...] = _dot(ca, w_ref[...], NT) + b_ref[...]
        ca_ref[...] = ca

    return _pc(body, grid=(N // tn,),
               in_specs=[pl.BlockSpec((rows, D), lambda j: (0, 0)), pl.BlockSpec((tn, D), lambda j: (j, 0)),
                         pl.BlockSpec((1, tn), lambda j: (0, j))],
               out_specs=[pl.BlockSpec((rows, tn), lambda j: (0, j)), pl.BlockSpec((rows, D), lambda j: (0, 0))],
               out_shape=[SDS((rows, N), F32), SDS((rows, D), BF16)], name=name)(c8, ada_w, ada_b)


def _row_tile(S):
    return _tile(S, 512, 8)


def _norm_mod(x, g, mod3, sc_seg, sh_seg, S, name):
    T, D = x.shape
    tm = _row_tile(S)
    tpb = S // tm

    def body(x_ref, g_ref, sc_ref, sh_ref, h_ref):
        x_ = x_ref[...]
        r = lax.rsqrt(jnp.mean(x_ * x_, axis=-1, keepdims=True) + EPS)
        h_ref[...] = ((x_ * r) * g_ref[...] * (1.0 + sc_ref[...]) + sh_ref[...]).astype(BF16)

    return _pc(body, grid=(T // tm,),
               in_specs=[pl.BlockSpec((tm, D), lambda i: (i, 0)), pl.BlockSpec((1, D), lambda i: (0, 0)),
                         pl.BlockSpec((None, 1, D), lambda i: (i // tpb, 0, sc_seg)),
                         pl.BlockSpec((None, 1, D), lambda i: (i // tpb, 0, sh_seg))],
               out_specs=pl.BlockSpec((tm, D), lambda i: (i, 0)), out_shape=SDS((T, D), BF16), name=name)(x, g, mod3, mod3)


def _resid_post(x, fo, mod3, gt_seg, pg, S, name):
    T, D = x.shape
    tm = _row_tile(S)
    tpb = S // tm

    def body(x_ref, f_ref, gt_ref, pg_ref, o_ref):
        f = f_ref[...]
        r = lax.rsqrt(jnp.mean(f * f, axis=-1, keepdims=True) + EPS)
        o_ref[...] = x_ref[...] + gt_ref[...] * ((f * r) * pg_ref[...])

    return _pc(body, grid=(T // tm,),
               in_specs=[pl.BlockSpec((tm, D), lambda i: (i, 0)), pl.BlockSpec((tm, D), lambda i: (i, 0)),
                         pl.BlockSpec((None, 1, D), lambda i: (i // tpb, 0, gt_seg)),
                         pl.BlockSpec((1, D), lambda i: (0, 0))],
               out_specs=pl.BlockSpec((tm, D), lambda i: (i, 0)), out_shape=SDS((T, D), F32), name=name)(x, fo, mod3, pg)


def _post_bwd(fo, mod3, gt_seg, pg, dout, S, name):
    T, D = fo.shape
    tm = _row_tile(S)
    tpb = S // tm
    nb = T // S

    def body(f_ref, gt_ref, pg_ref, d_ref, df_ref, dgt_ref, dpg_ref):
        i = pl.program_id(0)

        @pl.when(i == 0)
        def _():
            dpg_ref[...] = jnp.zeros_like(dpg_ref)

        @pl.when(i % tpb == 0)
        def _():
            dgt_ref[...] = jnp.zeros_like(dgt_ref)

        f, d, gt, pg_ = f_ref[...], d_ref[...], gt_ref[...], pg_ref[...]
        r = lax.rsqrt(jnp.mean(f * f, axis=-1, keepdims=True) + EPS)
        n = f * r
        dn = d * gt * pg_
        df_ref[...] = (r * (dn - n * jnp.mean(dn * n, axis=-1, keepdims=True))).astype(df_ref.dtype)
        dn_ = d * n
        dgt_ref[...] += _bsum(jnp.sum(dn_ * pg_, axis=0, keepdims=True))
        dpg_ref[...] += _bsum(jnp.sum(dn_ * gt, axis=0, keepdims=True))

    return _pc(body, grid=(T // tm,),
               in_specs=[pl.BlockSpec((tm, D), lambda i: (i, 0)),
                         pl.BlockSpec((None, 1, D), lambda i: (i // tpb, 0, gt_seg)),
                         pl.BlockSpec((1, D), lambda i: (0, 0)), pl.BlockSpec((tm, D), lambda i: (i, 0))],
               out_specs=[pl.BlockSpec((tm, D), lambda i: (i, 0)), pl.BlockSpec((8, D), lambda i: (i // tpb, 0)),
                          pl.BlockSpec((8, D), lambda i: (0, 0))],
               out_shape=[SDS((T, D), BF16), SDS((nb * 8, D), F32), SDS((8, D), F32)], name=name)(fo, mod3, pg, dout)


def _pre_bwd(x, g, mod3, sc_seg, dh, dout, S, name):
    T, D = x.shape
    tm = _row_tile(S)
    tpb = S // tm
    nb = T // S

    def body(x_ref, g_ref, sc_ref, dh_ref, d_ref, dx_ref, dg_ref, dsc_ref, dsh_ref):
        i = pl.program_id(0)

        @pl.when(i == 0)
        def _():
            dg_ref[...] = jnp.zeros_like(dg_ref)

        @pl.when(i % tpb == 0)
        def _():
            dsc_ref[...] = jnp.zeros_like(dsc_ref)
            dsh_ref[...] = jnp.zeros_like(dsh_ref)

        x_, g_, dh_ = x_ref[...], g_ref[...], dh_ref[...]
        one_sc = 1.0 + sc_ref[...]
        r = lax.rsqrt(jnp.mean(x_ * x_, axis=-1, keepdims=True) + EPS)
        n = x_ * r
        dn = dh_ * g_ * one_sc
        dx_ref[...] = d_ref[...] + r * (dn - n * jnp.mean(dn * n, axis=-1, keepdims=True))
        dhn = dh_ * n
        dg_ref[...] += _bsum(jnp.sum(dhn * one_sc, axis=0, keepdims=True))
        dsc_ref[...] += _bsum(jnp.sum(dhn * g_, axis=0, keepdims=True))
        dsh_ref[...] += _bsum(jnp.sum(dh_, axis=0, keepdims=True))

    row = pl.BlockSpec((tm, D), lambda i: (i, 0))
    return _pc(body, grid=(T // tm,),
               in_specs=[row, pl.BlockSpec((1, D), lambda i: (0, 0)),
                         pl.BlockSpec((None, 1, D), lambda i: (i // tpb, 0, sc_seg)), row, row],
               out_specs=[row, pl.BlockSpec((8, D), lambda i: (0, 0)), pl.BlockSpec((8, D), lambda i: (i // tpb, 0)),
                          pl.BlockSpec((8, D), lambda i: (i // tpb, 0))],
               out_shape=[SDS((T, D), F32), SDS((8, D), F32), SDS((nb * 8, D), F32), SDS((nb * 8, D), F32)],
               name=name)(x, g, mod3, dh, dout)


def _loss(y, target, S, name):
    T, D = y.shape
    tm = _row_tile(S)

    def body(y_ref, t_ref, dy_ref, l_ref):
        @pl.when(pl.program_id(0) == 0)
        def _():
            l_ref[...] = jnp.zeros_like(l_ref)

        e = y_ref[...] - t_ref[...]
        dy_ref[...] = e * (1.0 / D)
        l_ref[...] += jnp.broadcast_to(jnp.sum(e * e, keepdims=True) * (0.5 / D), l_ref.shape)

    row = pl.BlockSpec((tm, D), lambda i: (i, 0))
    return _pc(body, grid=(T // tm,), in_specs=[row, row],
               out_specs=[row, pl.BlockSpec((8, 128), lambda i: (0, 0))],
               out_shape=[SDS((T, D), F32), SDS((8, 128), F32)], name=name)(y, target)


def _conv_geom(view, C, S):
    arr, off = view
    T = arr.shape[0]
    tm = _row_tile(S)
    tc = _tile(C, 512)
    assert off % tc == 0 and C % tc == 0
    return arr, off // tc, T, tm, tc, S // tm


def _prev_spec(tm, tc, ob, order):
    if order == "ij":
        return pl.BlockSpec((8, tc), lambda i, j: (jnp.maximum(i * (tm // 8) - 1, 0), ob + j))
    return pl.BlockSpec((8, tc), lambda j, i: (jnp.maximum(i * (tm // 8) - 1, 0), ob + j))


def _next_spec(T, tm, tc, ob, order):
    last = T // 8 - 1
    if order == "ij":
        return pl.BlockSpec((8, tc), lambda i, j: (jnp.minimum((i + 1) * (tm // 8), last), ob + j))
    return pl.BlockSpec((8, tc), lambda j, i: (jnp.minimum((i + 1) * (tm // 8), last), ob + j))


def _shifted(buf, cur, prev, tm):
    buf[0:8, :] = prev
    buf[8:, :] = cur
    return lambda j: buf[pl.ds(8 - j, tm), :]


def _conv_fwd(view, C, w8, b, K, S, name):
    arr, ob, T, tm, tc, tps = _conv_geom(view, C, S)

    def body(u_ref, p_ref, w_ref, b_ref, o_ref, buf):
        first = (pl.program_id(0) % tps) == 0
        u = u_ref[...]
        sh = _shifted(buf, u, jnp.where(first, 0.0, p_ref[...]), tm)
        acc = u * w_ref[K - 1:K, :] + b_ref[...]
        for j in range(1, K):
            acc = acc + sh(j) * w_ref[K - 1 - j:K - j, :]
        o_ref[...] = acc

    return _pc(body, grid=(T // tm, C // tc),
               in_specs=[pl.BlockSpec((tm, tc), lambda i, j: (i, ob + j)), _prev_spec(tm, tc, ob, "ij"),
                         pl.BlockSpec((8, tc), lambda i, j: (0, j)), pl.BlockSpec((1, tc), lambda i, j: (0, j))],
               out_specs=pl.BlockSpec((tm, tc), lambda i, j: (i, j)), out_shape=SDS((T, C), F32), name=name,
               scratch=(pltpu.VMEM((tm + 8, tc), F32),))(arr, arr, w8, b)


def _conv_bwd_in(dview, C, w8, K, S, out_dtype, name):
    arr, ob, T, tm, tc, tps = _conv_geom(dview, C, S)

    def body(d_ref, n_ref, w_ref, o_ref, buf):
        last = (pl.program_id(0) % tps) == tps - 1
        d = d_ref[...]
        buf[0:tm, :] = d
        buf[tm:tm + 8, :] = jnp.where(last, 0.0, n_ref[...])
        acc = d * w_ref[K - 1:K, :]
        for j in range(1, K):
            acc = acc + buf[pl.ds(j, tm), :] * w_ref[K - 1 - j:K - j, :]
        o_ref[...] = acc.astype(o_ref.dtype)

    return _pc(body, grid=(T // tm, C // tc),
               in_specs=[pl.BlockSpec((tm, tc), lambda i, j: (i, ob + j)), _next_spec(T, tm, tc, ob, "ij"),
                         pl.BlockSpec((8, tc), lambda i, j: (0, j))],
               out_specs=pl.BlockSpec((tm, tc), lambda i, j: (i, j)), out_shape=SDS((T, C), out_dtype), name=name,
               scratch=(pltpu.VMEM((tm + 8, tc), F32),))(arr, arr, w8)


def _conv_bwd_w(dview, uview, C, K, S, name):
    darr, dob, T, tm, tc, tps = _conv_geom(dview, C, S)
    uarr, uob, _, _, _, _ = _conv_geom(uview, C, S)

    def body(d_ref, u_ref, p_ref, o_ref, buf):
        i = pl.program_id(1)

        @pl.when(i == 0)
        def _():
            o_ref[...] = jnp.zeros_like(o_ref)

        first = (i % tps) == 0
        d, u = d_ref[...], u_ref[...]
        sh = _shifted(buf, u, jnp.where(first, 0.0, p_ref[...]), tm)
        rows = []
        for k in range(K):
            j = K - 1 - k
            rows.append(jnp.sum(d * (u if j == 0 else sh(j)), axis=0, keepdims=True))
        rows.append(jnp.sum(d, axis=0, keepdims=True))
        rows.append(jnp.zeros((8 - K - 1, tc), F32))
        o_ref[...] += jnp.concatenate(rows, axis=0)

    return _pc(body, grid=(C // tc, T // tm),
               in_specs=[pl.BlockSpec((tm, tc), lambda j, i: (i, dob + j)),
                         pl.BlockSpec((tm, tc), lambda j, i: (i, uob + j)), _prev_spec(tm, tc, uob, "ji")],
               out_specs=pl.BlockSpec((8, tc), lambda j, i: (0, j)), out_shape=SDS((8, C), F32), name=name,
               scratch=(pltpu.VMEM((tm + 8, tc), F32),))(darr, uarr, uarr)


def _ffn_act_fwd(uu, w8, b, S, name):
    K, gw = FFN_CONV_K, GLU_W
    T, F2 = uu.shape
    tm, tc = _row_tile(S), 2 * GLU_W
    tps = S // tm

    def body(u_ref, p_ref, w_ref, b_ref, a_ref, buf):
        first = (pl.program_id(0) % tps) == 0
        uu_ = u_ref[...]
        sh = _shifted(buf, uu_, jnp.where(first, 0.0, p_ref[...]), tm)
        u = uu_ * w_ref[K - 1:K, :] + b_ref[...]
        for j in range(1, K):
            u = u + sh(j) * w_ref[K - 1 - j:K - j, :]
        a_ref[...] = (_silu(u[:, :gw]) * u[:, gw:]).astype(BF16)

    return _pc(body, grid=(T // tm, F2 // tc),
               in_specs=[pl.BlockSpec((tm, tc), lambda i, j: (i, j)), _prev_spec(tm, tc, 0, "ij"),
                         pl.BlockSpec((8, tc), lambda i, j: (0, j)), pl.BlockSpec((1, tc), lambda i, j: (0, j))],
               out_specs=pl.BlockSpec((tm, gw), lambda i, j: (i, j)), out_shape=SDS((T, F2 // 2), BF16), name=name,
               scratch=(pltpu.VMEM((tm + 8, tc), F32),))(uu, uu, w8, b)


def _ffn_act_bwd(uu, da, w8, b, S, name):
    K, gw = FFN_CONV_K, GLU_W
    T, F2 = uu.shape
    tm, tc = _row_tile(S), 2 * GLU_W
    tps = S // tm
    last16 = T // 16 - 1

    def body(u_ref, p_ref, n_ref, da_ref, dan_ref, w_ref, b_ref, duu_ref, cw_ref, ubuf, dbuf):
        i = pl.program_id(1)

        @pl.when(i == 0)
        def _():
            cw_ref[...] = jnp.zeros_like(cw_ref)

        first = (i % tps) == 0
        last = (i % tps) == tps - 1
        ubuf[0:8, :] = jnp.where(first, 0.0, p_ref[...])
        ubuf[8:tm + 8, :] = u_ref[...]
        ubuf[tm + 8:tm + 16, :] = n_ref[...]
        u = ubuf[pl.ds(8, tm + 8), :] * w_ref[K - 1:K, :] + b_ref[...]
        for j in range(1, K):
            u = u + ubuf[pl.ds(8 - j, tm + 8), :] * w_ref[K - 1 - j:K - j, :]
        da_n = jnp.where(last, 0.0, dan_ref[...].astype(F32)[0:8, :])
        da_ = jnp.concatenate([da_ref[...].astype(F32), da_n], axis=0)
        g, v = u[:, :gw], u[:, gw:]
        du = jnp.concatenate([da_ * v * _dsilu(g), da_ * _silu(g)], axis=1)
        dbuf[...] = du
        dmain = du[0:tm, :]
        duu = dmain * w_ref[K - 1:K, :]
        for j in range(1, K):
            duu = duu + dbuf[pl.ds(j, tm), :] * w_ref[K - 1 - j:K - j, :]
        duu_ref[...] = duu.astype(BF16)
        rows = [jnp.sum(dmain * ubuf[pl.ds(8 - (K - 1 - k), tm), :], axis=0, keepdims=True) for k in range(K)]
        rows.append(jnp.sum(dmain, axis=0, keepdims=True))
        rows.append(jnp.zeros((8 - K - 1, tc), F32))
        cw_ref[...] += jnp.concatenate(rows, axis=0)

    return _pc(body, grid=(F2 // tc, T // tm),
               in_specs=[pl.BlockSpec((tm, tc), lambda j, i: (i, j)), _prev_spec(tm, tc, 0, "ji"),
                         _next_spec(T, tm, tc, 0, "ji"), pl.BlockSpec((tm, gw), lambda j, i: (i, j)),
                         pl.BlockSpec((16, gw), lambda j, i: (jnp.minimum((i + 1) * (tm // 16), last16), j)),
                         pl.BlockSpec((8, tc), lambda j, i: (0, j)), pl.BlockSpec((1, tc), lambda j, i: (0, j))],
               out_specs=[pl.BlockSpec((tm, tc), lambda j, i: (i, j)), pl.BlockSpec((8, tc), lambda j, i: (0, j))],
               out_shape=[SDS((T, F2), BF16), SDS((8, F2), F32)], name=name,
               scratch=(pltpu.VMEM((tm + 16, tc), F32), pltpu.VMEM((tm + 8, tc), F32)))(uu, uu, uu, da, da, w8, b)


def _ssd_common(dtc_raw, dtr_raw, hpc, hpr, L):
    dt_c = _softplus(dtc_raw + hpc[0:1, :])
    a_c = -jnp.exp(hpc[1:2, :])
    dt_r = _softplus(dtr_raw + hpr[:, 0:1])
    a_r = -jnp.exp(hpr[:, 1:2])
    li = lax.broadcasted_iota(jnp.int32, (L, L), 0)
    si = lax.broadcasted_iota(jnp.int32, (L, L), 1)
    low = li >= si
    upp = li <= si
    acs_c = _dotx(low, dt_c * a_c, split="b")
    acs_r = _dotx(dt_r * a_r, upp)
    return dt_c, a_c, acs_c, acs_r, low, upp


def _dotx(a, b, split="a", parts=3, dims=NN):
    val, one = (a, b) if split == "a" else (b, a)
    one = one.astype(BF16)
    acc, rem = None, val
    for i in range(parts):
        piece = rem.astype(BF16)
        t = _dot(piece, one, dims) if split == "a" else _dot(one, piece, dims)
        acc = t if acc is None else acc + t
        if i + 1 < parts:
            rem = rem - piece.astype(F32)
    return acc


def _head_maps(R, P, L):
    RP = R * P
    sel = (lax.broadcasted_iota(jnp.int32, (RP, R), 0) // P == lax.broadcasted_iota(jnp.int32, (RP, R), 1)).astype(F32)
    selt = (lax.broadcasted_iota(jnp.int32, (R, RP), 1) // P == lax.broadcasted_iota(jnp.int32, (R, RP), 0)).astype(F32)
    colb = (lax.broadcasted_iota(jnp.int32, (R, R * L), 1) // L == lax.broadcasted_iota(jnp.int32, (R, R * L), 0)).astype(F32)
    return sel, selt, colb


def _pair_diag(mats, rhs_b, R, P):
    lanes = 2 * P
    lo = lax.broadcasted_iota(jnp.int32, (mats[0].shape[0], lanes), 1) < P
    out = []
    for q in range(R // 2):
        rp = rhs_b[:, q * lanes:(q + 1) * lanes]
        out.append(jnp.where(lo, _dot(mats[2 * q], rp), _dot(mats[2 * q + 1], rp)))
    return jnp.concatenate(out, axis=1) if len(out) > 1 else out[0]


def _ssd_specs(pre, off_x, off_b, off_c, G, R, P, nb, nc, rev):
    L, N, RP = CHUNK, N_STATE, R * P
    cidx = (lambda c: nc - 1 - c) if rev else (lambda c: c)
    xb, bb, cb = off_x // RP, off_b // N, off_c // N
    assert off_x % RP == 0 and off_b % N == 0 and off_c % N == 0
    row = lambda b, c: b * nc + cidx(c)
    return dict(
        x=pl.BlockSpec((L, RP), lambda g, b, c: (row(b, c), xb + g)),
        b=pl.BlockSpec((L, N), lambda g, b, c: (row(b, c), bb + g)),
        c=pl.BlockSpec((L, N), lambda g, b, c: (row(b, c), cb + g)),
        dtc=pl.BlockSpec((None, L, R), lambda g, b, c: (g, row(b, c), 0)),
        dtr=pl.BlockSpec((None, R, L), lambda g, b, c: (g, 0, row(b, c))),
        hpc=pl.BlockSpec((None, 8, R), lambda g, b, c: (g, 0, 0)),
        hpr=pl.BlockSpec((None, R, 8), lambda g, b, c: (g, 0, 0)),
        y=pl.BlockSpec((L, RP), lambda g, b, c: (row(b, c), g)),
        bc=pl.BlockSpec((L, N), lambda g, b, c: (row(b, c), g)),
        hs=pl.BlockSpec((None, None, N, RP), lambda g, b, c: (row(b, c), g, 0, 0)),
    )


def _ssd_fwd(pre, offs, dtc, dtr, hpc, hpr, G, R, P, S, name):
    T = pre.shape[0]
    L, N, RP = CHUNK, N_STATE, R * P
    nc, nb = S // L, T // S
    sp = _ssd_specs(pre, *offs, G, R, P, nb, nc, False)

    def body(px_ref, pb_ref, pc_ref, dtc_ref, dtr_ref, hpc_ref, hpr_ref, y_ref, hs_ref, hst):
        @pl.when(pl.program_id(2) == 0)
        def _():
            hst[...] = jnp.zeros_like(hst)

        xs, bm, cm = _silu(px_ref[...]), _silu(pb_ref[...]), _silu(pc_ref[...])
        hpc_ = hpc_ref[...]
        dt_c, _, acs_c, acs_r, low, _ = _ssd_common(dtc_ref[...], dtr_ref[...], hpc_, hpr_ref[...], L)
        _, selt, colb = _head_maps(R, P, L)
        dt_e, a_e, hp_e = _dotx(dt_c, selt), _dotx(acs_c, selt), _dotx(hpc_, selt)
        a_bc = _dotx(acs_c, colb)
        a_last = a_e[L - 1:L, :]
        bb, cb = bm.astype(BF16), cm.astype(BF16)
        gm = _dot(cb, bb, NT)
        hprev = hst[...]
        hprev_b = hprev.astype(BF16)
        hs_ref[...] = hprev_b
        xdt = xs * dt_e
        xdt_b = xdt.astype(BF16)
        ms = []
        for r in range(R):
            dec = jnp.exp(jnp.where(low, a_bc[:, r * L:(r + 1) * L] - acs_r[r:r + 1, :], -jnp.inf))
            ms.append((gm * dec).astype(BF16))
        y = _pair_diag(ms, xdt_b, R, P) + _dot(cb, hprev_b) * jnp.exp(a_e) + hp_e[2:3, :] * xs
        y_ref[...] = y
        xw = (xdt * jnp.exp(a_last - a_e)).astype(BF16)
        hst[...] = hprev * jnp.exp(a_last) + _dot(bb, xw, TN)

    return _pc(body, grid=(G, nb, nc),
               in_specs=[sp["x"], sp["b"], sp["c"], sp["dtc"], sp["dtr"], sp["hpc"], sp["hpr"]],
               out_specs=[sp["y"], sp["hs"]],
               out_shape=[SDS((T, G * RP), F32), SDS((nb * nc, G, N, RP), BF16)], name=name,
               scratch=(pltpu.VMEM((N, RP), F32),))(pre, pre, pre, dtc, dtr, hpc, hpr)


def _ssd_bwd(pre, offs, dtc, dtr, hpc, hpr, hs, dy, G, R, P, S, name):
    T = pre.shape[0]
    L, N, RP = CHUNK, N_STATE, R * P
    nc, nb = S // L, T // S
    sp = _ssd_specs(pre, *offs, G, R, P, nb, nc, True)

    def body(px_ref, pb_ref, pc_ref, dtc_ref, dtr_ref, hpc_ref, hpr_ref, hs_ref, dy_ref,
             dpx_ref, dpb_ref, dpc_ref, ddt_ref, hpg_ref, dhst):
        bi, ci = pl.program_id(1), pl.program_id(2)

        @pl.when(ci == 0)
        def _():
            dhst[...] = jnp.zeros_like(dhst)

        @pl.when((bi == 0) & (ci == 0))
        def _():
            hpg_ref[...] = jnp.zeros_like(hpg_ref)

        px, pb, pcc = px_ref[...], pb_ref[...], pc_ref[...]
        xs, bm, cm = _silu(px), _silu(pb), _silu(pcc)
        hpc_ = hpc_ref[...]
        dtc_raw = dtc_ref[...]
        dt_c, a_c, acs_c, acs_r, low, upp = _ssd_common(dtc_raw, dtr_ref[...], hpc_, hpr_ref[...], L)
        sel, selt, colb = _head_maps(R, P, L)
        dt_e, a_e, hp_e = _dotx(dt_c, selt), _dotx(acs_c, selt), _dotx(hpc_, selt)
        a_bc = _dotx(acs_c, colb)
        a_last = a_e[L - 1:L, :]
        e_e, w_e = jnp.exp(a_e), jnp.exp(a_last - a_e)
        bb, cb = bm.astype(BF16), cm.astype(BF16)
        gm = _dot(cb, bb, NT)
        gmt = _dot(bb, cb, NT)
        hprev = hs_ref[...]
        dhn = dhst[...]
        dhn_b = dhn.astype(BF16)
        dy = dy_ref[...]
        dy_b = dy.astype(BF16)
        xdt = xs * dt_e
        xdt_b = xdt.astype(BF16)
        yoff = _dot(cb, hprev) * e_e
        dye_b = (dy * e_e).astype(BF16)
        dcm = _dot(dye_b, hprev, NT)
        dhst[...] = _dot(cb, dye_b, TN) + jnp.exp(a_last) * dhn
        dxdt_st = _dot(bb, dhn_b) * w_e
        dbm = _dot((xdt * w_e).astype(BF16), dhn_b, NT)
        lanes = 2 * P
        lo = lax.broadcasted_iota(jnp.int32, (L, lanes), 1) < P
        dg = jnp.zeros((L, L), F32)
        es, css = [], []
        for r in range(R):
            col_b, row = a_bc[:, r * L:(r + 1) * L], acs_r[r:r + 1, :]
            dec = jnp.exp(jnp.where(low, col_b - row, -jnp.inf))
            q = r // 2
            dyp = dy_b[:, q * lanes:(q + 1) * lanes]
            dyp = jnp.where(lo if r % 2 == 0 else ~lo, dyp, jnp.zeros_like(dyp))
            dm = _dot(dyp, xdt_b[:, q * lanes:(q + 1) * lanes], NT)
            dg = dg + dm * dec
            e = dm * (gm * dec)
            es.append(e)
            css.append(jnp.sum(e, axis=0, keepdims=True))
        dgb = dg.astype(BF16)
        dcm = dcm + _dot(dgb, bb)
        dbm = dbm + _dot(dgb, cb, TN)
        colbt = (lax.broadcasted_iota(jnp.int32, (R * L, R), 0) // L
                 == lax.broadcasted_iota(jnp.int32, (R * L, R), 1)).astype(F32)
        eye = (lax.broadcasted_iota(jnp.int32, (R, R), 0) == lax.broadcasted_iota(jnp.int32, (R, R), 1)).astype(F32)
        row_sums = _dotx(jnp.concatenate(es, axis=1), colbt)
        col_sums = _dotx(jnp.concatenate(css, axis=0), eye, dims=TN)
        mts = []
        for r in range(R):
            dect = jnp.exp(jnp.where(upp, acs_r[r:r + 1, :] - a_bc[:, r * L:(r + 1) * L], -jnp.inf))
            mts.append((gmt * dect).astype(BF16))
        dxdt = _pair_diag(mts, dy_b, R, P) + dxdt_st
        q_st = _dotx(xdt * dxdt_st, sel, parts=2)
        da = row_sums - col_sums + _dotx(dy * yoff, sel, parts=2) - q_st
        hh = jnp.sum(_dotx(dhn * hprev.astype(F32), sel, parts=2), axis=0, keepdims=True)
        da_last = jnp.exp(acs_c[L - 1:L, :]) * hh + jnp.sum(q_st, axis=0, keepdims=True)
        rowi = lax.broadcasted_iota(jnp.int32, (L, R), 0)
        da = da + jnp.where(rowi == L - 1, da_last, 0.0)
        dpx_ref[...] = (dxdt * dt_e + hp_e[2:3, :] * dy) * _dsilu(px)
        dpb_ref[...] = dbm * _dsilu(pb)
        dpc_ref[...] = dcm * _dsilu(pcc)
        dadt = _dotx(upp, da, split="b")
        ddt = _dotx(dxdt * xs, sel, parts=2) + dadt * a_c
        ddt_raw = ddt * jax.nn.sigmoid(dtc_raw + hpc_[0:1, :])
        ddt_ref[...] = ddt_raw
        d_a = jnp.sum(dadt * dt_c, axis=0, keepdims=True)
        d_d = jnp.sum(_dotx(dy * xs, sel, parts=2), axis=0, keepdims=True)
        rows = [jnp.sum(ddt_raw, axis=0, keepdims=True), d_a * a_c, d_d, jnp.zeros((5, R), F32)]
        hpg_ref[...] += jnp.concatenate(rows, axis=0)

    return _pc(body, grid=(G, nb, nc),
               in_specs=[sp["x"], sp["b"], sp["c"], sp["dtc"], sp["dtr"], sp["hpc"], sp["hpr"], sp["hs"], sp["y"]],
               out_specs=[sp["y"], sp["bc"], sp["bc"], sp["dtc"], pl.BlockSpec((None, 8, R), lambda g, b, c: (g, 0, 0))],
               out_shape=[SDS((T, G * RP), F32), SDS((T, G * N), F32), SDS((T, G * N), F32), SDS((G, T, R), F32),
                          SDS((G, 8, R), F32)], name=name,
               scratch=(pltpu.VMEM((N, RP), F32),))(pre, pre, pre, dtc, dtr, hpc, hpr, hs, dy)


def _gate_norm_fwd(y, zview, ng, G, S, name):
    T, DI = y.shape
    zarr, zoff = zview
    gw = DI // G
    tm = _row_tile(S)
    zb = zoff // gw
    assert zoff % gw == 0

    def body(y_ref, z_ref, g_ref, o_ref):
        yg = y_ref[...] * _silu(z_ref[...])
        r = lax.rsqrt(jnp.mean(yg * yg, axis=-1, keepdims=True) + EPS)
        o_ref[...] = (yg * r * g_ref[...]).astype(BF16)

    return _pc(body, grid=(T // tm, G),
               in_specs=[pl.BlockSpec((tm, gw), lambda i, g: (i, g)), pl.BlockSpec((tm, gw), lambda i, g: (i, zb + g)),
                         pl.BlockSpec((1, gw), lambda i, g: (0, g))],
               out_specs=pl.BlockSpec((tm, gw), lambda i, g: (i, g)), out_shape=SDS((T, DI), BF16), name=name)(y, zarr, ng)


def _gate_norm_bwd(y, zview, ng, dyn, G, S, name):
    T, DI = y.shape
    zarr, zoff = zview
    gw = DI // G
    tm = _row_tile(S)
    zb = zoff // gw

    def body(y_ref, z_ref, g_ref, d_ref, dy_ref, dz_ref, dg_ref):
        @pl.when(pl.program_id(1) == 0)
        def _():
            dg_ref[...] = jnp.zeros_like(dg_ref)

        y_, z, d = y_ref[...], z_ref[...], d_ref[...]
        sz = _silu(z)
        yg = y_ * sz
        r = lax.rsqrt(jnp.mean(yg * yg, axis=-1, keepdims=True) + EPS)
        n = yg * r
        dn = d * g_ref[...]
        dyg = r * (dn - n * jnp.mean(dn * n, axis=-1, keepdims=True))
        dy_ref[...] = dyg * sz
        dz_ref[...] = (dyg * y_ * _dsilu(z)).astype(BF16)
        dg_ref[...] += _bsum(jnp.sum(d * n, axis=0, keepdims=True))

    return _pc(body, grid=(G, T // tm),
               in_specs=[pl.BlockSpec((tm, gw), lambda g, i: (i, g)), pl.BlockSpec((tm, gw), lambda g, i: (i, zb + g)),
                         pl.BlockSpec((1, gw), lambda g, i: (0, g)), pl.BlockSpec((tm, gw), lambda g, i: (i, g))],
               out_specs=[pl.BlockSpec((tm, gw), lambda g, i: (i, g)), pl.BlockSpec((tm, gw), lambda g, i: (i, g)),
                          pl.BlockSpec((8, gw), lambda g, i: (0, g))],
               out_shape=[SDS((T, DI), F32), SDS((T, DI), BF16), SDS((8, DI), F32)], name=name)(y, zarr, ng, dyn)


def _shortconv_fwd(proj, off_b, off_c, off_h, C, w8, S, name):
    K = SC_CONV_K
    _, ob, T, tm, tc, tps = _conv_geom((proj, off_b), C, S)
    oc, oh = off_c // tc, off_h // tc

    def body(b_ref, c_ref, h_ref, cp_ref, hp_ref, w_ref, o_ref, buf):
        first = (pl.program_id(0) % tps) == 0
        v = c_ref[...] * h_ref[...]
        sh = _shifted(buf, v, jnp.where(first, 0.0, cp_ref[...] * hp_ref[...]), tm)
        acc = v * w_ref[K - 1:K, :]
        for j in range(1, K):
            acc = acc + sh(j) * w_ref[K - 1 - j:K - j, :]
        o_ref[...] = (b_ref[...] * acc).astype(BF16)

    blk = lambda o: pl.BlockSpec((tm, tc), lambda i, j: (i, o + j))
    return _pc(body, grid=(T // tm, C // tc),
               in_specs=[blk(ob), blk(oc), blk(oh), _prev_spec(tm, tc, oc, "ij"), _prev_spec(tm, tc, oh, "ij"),
                         pl.BlockSpec((8, tc), lambda i, j: (0, j))],
               out_specs=pl.BlockSpec((tm, tc), lambda i, j: (i, j)), out_shape=SDS((T, C), BF16), name=name,
               scratch=(pltpu.VMEM((tm + 8, tc), F32),))(proj, proj, proj, proj, proj, w8)


def _shortconv_bwd(proj, off_b, off_c, off_h, C, w8, dsc, S, name):
    K = SC_CONV_K
    _, ob, T, tm, tc, tps = _conv_geom((proj, off_b), C, S)
    oc, oh = off_c // tc, off_h // tc

    def body(b_ref, c_ref, h_ref, cp_ref, hp_ref, bn_ref, d_ref, dn_ref, w_ref,
             db_ref, dc_ref, dh_ref, dw_ref, buf, buf2):
        i = pl.program_id(1)

        @pl.when(i == 0)
        def _():
            dw_ref[...] = jnp.zeros_like(dw_ref)

        first = (i % tps) == 0
        last = (i % tps) == tps - 1
        b_, c_, h_, d = b_ref[...], c_ref[...], h_ref[...], d_ref[...]
        v = c_ * h_
        sh = _shifted(buf, v, jnp.where(first, 0.0, cp_ref[...] * hp_ref[...]), tm)
        vs = [v] + [sh(j) for j in range(1, K)]
        conv = vs[0] * w_ref[K - 1:K, :]
        for j in range(1, K):
            conv = conv + vs[j] * w_ref[K - 1 - j:K - j, :]
        db_ref[...] = (d * conv).astype(BF16)
        dconv = d * b_
        buf2[0:tm, :] = dconv
        buf2[tm:tm + 8, :] = jnp.where(last, 0.0, dn_ref[...] * bn_ref[...])
        dv = dconv * w_ref[K - 1:K, :]
        for j in range(1, K):
            dv = dv + buf2[pl.ds(j, tm), :] * w_ref[K - 1 - j:K - j, :]
        dc_ref[...] = (dv * h_).astype(BF16)
        dh_ref[...] = (dv * c_).astype(BF16)
        rows = [jnp.sum(dconv * vs[K - 1 - k], axis=0, keepdims=True) for k in range(K)]
        rows.append(jnp.zeros((8 - K, tc), F32))
        dw_ref[...] += jnp.concatenate(rows, axis=0)

    blk = lambda o: pl.BlockSpec((tm, tc), lambda j, i: (i, o + j))
    out = pl.BlockSpec((tm, tc), lambda j, i: (i, j))
    return _pc(body, grid=(C // tc, T // tm),
               in_specs=[blk(ob), blk(oc), blk(oh), _prev_spec(tm, tc, oc, "ji"), _prev_spec(tm, tc, oh, "ji"),
                         _next_spec(T, tm, tc, ob, "ji"), blk(0), _next_spec(T, tm, tc, 0, "ji"),
                         pl.BlockSpec((8, tc), lambda j, i: (0, j))],
               out_specs=[out, out, out, pl.BlockSpec((8, tc), lambda j, i: (0, j))],
               out_shape=[SDS((T, C), BF16)] * 3 + [SDS((8, C), F32)], name=name,
               scratch=(pltpu.VMEM((tm + 8, tc), F32), pltpu.VMEM((tm + 8, tc), F32)))(
                   proj, proj, proj, proj, proj, proj, dsc, dsc, w8)


def _merge_fwd(proj, off_g1, off_g2, y1, y2, S, name):
    T, D = y1.shape
    tm = _row_tile(S)
    o1, o2 = off_g1 // D, off_g2 // D
    assert off_g1 % D == 0 and off_g2 % D == 0

    def body(g1_ref, g2_ref, y1_ref, y2_ref, o_ref):
        o_ref[...] = (jax.nn.sigmoid(g1_ref[...]) * y1_ref[...] + jax.nn.sigmoid(g2_ref[...]) * y2_ref[...]).astype(BF16)

    row = pl.BlockSpec((tm, D), lambda i: (i, 0))
    return _pc(body, grid=(T // tm,),
               in_specs=[pl.BlockSpec((tm, D), lambda i: (i, o1)), pl.BlockSpec((tm, D), lambda i: (i, o2)), row, row],
               out_specs=row, out_shape=SDS((T, D), BF16), name=name)(proj, proj, y1, y2)


def _merge_bwd(proj, off_g1, off_g2, y1, y2, dm, S, name):
    T, D = y1.shape
    tm = _row_tile(S)
    o1, o2 = off_g1 // D, off_g2 // D

    def body(g1_ref, g2_ref, y1_ref, y2_ref, d_ref, dy1_ref, dy2_ref, dg1_ref, dg2_ref):
        d = d_ref[...]
        s1, s2 = jax.nn.sigmoid(g1_ref[...]), jax.nn.sigmoid(g2_ref[...])
        dy1_ref[...] = (d * s1).astype(BF16)
        dy2_ref[...] = (d * s2).astype(BF16)
        dg1_ref[...] = (d * y1_ref[...] * s1 * (1.0 - s1)).astype(BF16)
        dg2_ref[...] = (d * y2_ref[...] * s2 * (1.0 - s2)).astype(BF16)

    row = pl.BlockSpec((tm, D), lambda i: (i, 0))
    return _pc(body, grid=(T // tm,),
               in_specs=[pl.BlockSpec((tm, D), lambda i: (i, o1)), pl.BlockSpec((tm, D), lambda i: (i, o2)), row, row, row],
               out_specs=[row] * 4, out_shape=[SDS((T, D), BF16)] * 4, name=name)(proj, proj, y1, y2, dm)


def _pad8(w):
    return jnp.pad(w, ((0, 8 - w.shape[0]), (0, 0)))


def _dims(w):
    D = w["mix_pre_g"].shape[-1]
    DI = w["ssd_norm_g"].shape[-1]
    H = w["ssd_dt_bias"].shape[-1]
    conv_dim = w["ssd_conv_b"].shape[-1]
    G = (conv_dim - DI) // (2 * N_STATE)
    F = w["w_down"].shape[0]
    return dict(D=D, DI=DI, H=H, P=DI // H, G=G, R=H // G, GN=G * N_STATE, CD=conv_dim, F=F)


def _proj_layout(d):
    D, DI, CD, H = d["D"], d["DI"], d["CD"], d["H"]
    o = dict(z=0, xbc=DI, scb=DI + CD, scc=DI + CD + D, sch=DI + CD + 2 * D, g1=DI + CD + 3 * D, g2=DI + CD + 4 * D,
             dt=DI + CD + 5 * D)
    used = o["dt"] + H
    o["np"] = -(-used // 128) * 128
    return o


def _glu_perm(a, F, inverse=False):
    lead = a.shape[:-1]
    nb = F // GLU_W
    if not inverse:
        return a.reshape(*lead, 2, nb, GLU_W).swapaxes(-3, -2).reshape(*lead, 2 * F)
    return a.reshape(*lead, nb, 2, GLU_W).swapaxes(-3, -2).reshape(*lead, 2 * F)


def _glu_perm_rows(a, F, inverse=False):
    nb, D = F // GLU_W, a.shape[1]
    shape = (nb, 2, GLU_W, D) if inverse else (2, nb, GLU_W, D)
    return a.reshape(shape).swapaxes(0, 1).reshape(2 * F, D)


def _prep_layer(w):
    d = _dims(w)
    D, DI, CD, H, G, R, F = d["D"], d["DI"], d["CD"], d["H"], d["G"], d["R"], d["F"]
    lay = _proj_layout(d)
    w_in = w["w_in"]
    used = lay["dt"] + H
    wcat = jnp.concatenate([w_in[:DI + CD], w_in[DI + CD + H:], w_in[DI + CD:DI + CD + H],
                            jnp.zeros((lay["np"] - used, D), w_in.dtype)], axis=0)
    hp = jnp.stack([w["ssd_dt_bias"], w["ssd_a_log"], w["ssd_d"]], 0).astype(F32)
    hpc = jnp.pad(hp.reshape(3, G, R).transpose(1, 0, 2), ((0, 0), (0, 5), (0, 0)))
    hpr = jnp.pad(hp[:2].reshape(2, G, R).transpose(1, 2, 0), ((0, 0), (0, 0), (0, 6)))
    row = lambda v: v.reshape(1, -1).astype(F32)
    return dict(
        d=d, lay=lay, ada_w=w["ada_w"].astype(BF16), ada_b=row(w["ada_b"]),
        mix_pre_g=row(w["mix_pre_g"]), mix_post_g=row(w["mix_post_g"]), wcat=wcat.astype(BF16),
        ssd_conv_w=_pad8(w["ssd_conv_w"].astype(F32)), ssd_conv_b=row(w["ssd_conv_b"]), hpc=hpc, hpr=hpr,
        ssd_norm_g=row(w["ssd_norm_g"]), w_ssd_out=w["w_ssd_out"].astype(BF16),
        sc_conv_w=_pad8(w["sc_conv_w"].astype(F32)), w_sc_out=w["w_sc_out"].astype(BF16), w_o=w["w_o"].astype(BF16),
        ffn_pre_g=row(w["ffn_pre_g"]), ffn_post_g=row(w["ffn_post_g"]),
        w_up=_glu_perm_rows(w["w_up"], F).astype(BF16), ffn_conv_w=_pad8(_glu_perm(w["ffn_conv_w"].astype(F32), F)),
        ffn_conv_b=_glu_perm(row(w["ffn_conv_b"]), F), w_down=w["w_down"].astype(BF16))


def _dt_layouts(proj, lay, d):
    T = proj.shape[0]
    dt = proj[:, lay["dt"]:lay["dt"] + d["H"]].reshape(T, d["G"], d["R"])
    return dt.transpose(1, 0, 2), dt.transpose(1, 2, 0)


def _layer_fwd(x, mod3, p, S, li):
    d, lay = p["d"], p["lay"]
    D, DI, G, R, P, GN, CD = d["D"], d["DI"], d["G"], d["R"], d["P"], d["GN"], d["CD"]
    nm = lambda s: f"l{li}_{s}"
    h = _norm_mod(x, p["mix_pre_g"], mod3, 1, 0, S, nm("norm1"))
    proj = _mm(h, p["wcat"], "nt", F32, nm("mm_in"), caps=(1024, 1152, 2048))
    pre = _conv_fwd((proj, lay["xbc"]), CD, p["ssd_conv_w"], p["ssd_conv_b"], SSD_CONV_K, S, nm("ssdconv"))
    dtc, dtr = _dt_layouts(proj, lay, d)
    offs = (0, DI, DI + GN)
    y, hs = _ssd_fwd(pre, offs, dtc, dtr, p["hpc"], p["hpr"], G, R, P, S, nm("ssd"))
    yn = _gate_norm_fwd(y, (proj, lay["z"]), p["ssd_norm_g"], G, S, nm("gnorm"))
    sc = _shortconv_fwd(proj, lay["scb"], lay["scc"], lay["sch"], D, p["sc_conv_w"], S, nm("sconv"))
    y_ssd = _mm(yn, p["w_ssd_out"], "nn", F32, nm("mm_ssdout"))
    y_sc = _mm(sc, p["w_sc_out"], "nn", F32, nm("mm_scout"))
    m = _merge_fwd(proj, lay["g1"], lay["g2"], y_ssd, y_sc, S, nm("merge"))
    mix = _mm(m, p["w_o"], "nn", F32, nm("mm_o"))
    x1 = _resid_post(x, mix, mod3, 2, p["mix_post_g"], S, nm("post1"))
    h2 = _norm_mod(x1, p["ffn_pre_g"], mod3, 4, 3, S, nm("norm2"))
    uu = _mm(h2, p["w_up"], "nt", F32, nm("mm_up"), caps=(1024, 1408, 2048))
    a = _ffn_act_fwd(uu, p["ffn_conv_w"], p["ffn_conv_b"], S, nm("ffnact"))
    f = _mm(a, p["w_down"], "nn", F32, nm("mm_down"), caps=(1024, 1024, 1408))
    x2 = _resid_post(x1, f, mod3, 5, p["ffn_post_g"], S, nm("post2"))
    saved = dict(x=x, h=h, proj=proj, pre=pre, dtc=dtc, dtr=dtr, y=y, hs=hs, yn=yn, sc=sc, y_ssd=y_ssd, y_sc=y_sc,
                 m=m, mix=mix, x1=x1, h2=h2, uu=uu, a=a, f=f)
    return x2, saved


def _seq_sum(acc, nb):
    return acc.reshape(nb, 8, -1)[:, 0, :]


def _layer_bwd(dx2, mod3, p, s, S, li):
    d, lay = p["d"], p["lay"]
    D, DI, G, R, P, GN, CD, H, F = d["D"], d["DI"], d["G"], d["R"], d["P"], d["GN"], d["CD"], d["H"], d["F"]
    nb = dx2.shape[0] // S
    nm = lambda t: f"l{li}_{t}"
    g = {}
    df, dgt2, dpg2 = _post_bwd(s["f"], mod3, 5, p["ffn_post_g"], dx2, S, nm("post2_b"))
    g["ffn_post_g"] = dpg2[0]
    da = _mm(df, p["w_down"], "nt", BF16, nm("mm_down_bi"), caps=(1024, 1408, 2048))
    g["w_down"] = _mm(s["a"], df, "tn", WGRAD,nm("mm_down_bw"), caps=(1408, 1024, 1024))
    duu, cw = _ffn_act_bwd(s["uu"], da, p["ffn_conv_w"], p["ffn_conv_b"], S, nm("ffnact_b"))
    g["ffn_conv_w"] = _glu_perm(cw[:FFN_CONV_K], F, inverse=True)
    g["ffn_conv_b"] = _glu_perm(cw[FFN_CONV_K], F, inverse=True)
    dh2 = _mm(duu, p["w_up"], "nn", F32, nm("mm_up_bi"), caps=(1024, 1024, 1408))
    g["w_up"] = _glu_perm_rows(_mm(duu, s["h2"], "tn", WGRAD,nm("mm_up_bw"), caps=(1408, 1024, 1024)), F, inverse=True)
    dx1, dg2, dsc2, dsh2 = _pre_bwd(s["x1"], p["ffn_pre_g"], mod3, 4, dh2, dx2, S, nm("norm2_b"))
    g["ffn_pre_g"] = dg2[0]
    dmix, dgt1, dpg1 = _post_bwd(s["mix"], mod3, 2, p["mix_post_g"], dx1, S, nm("post1_b"))
    g["mix_post_g"] = dpg1[0]
    dm = _mm(dmix, p["w_o"], "nt", F32, nm("mm_o_bi"))
    g["w_o"] = _mm(s["m"], dmix, "tn", WGRAD,nm("mm_o_bw"))
    proj = s["proj"]
    dy_ssd, dy_sc, dg1, dg2_ = _merge_bwd(proj, lay["g1"], lay["g2"], s["y_ssd"], s["y_sc"], dm, S, nm("merge_b"))
    dyn = _mm(dy_ssd, p["w_ssd_out"], "nt", F32, nm("mm_ssdout_bi"))
    g["w_ssd_out"] = _mm(s["yn"], dy_ssd, "tn", WGRAD,nm("mm_ssdout_bw"))
    dsc = _mm(dy_sc, p["w_sc_out"], "nt", F32, nm("mm_scout_bi"))
    g["w_sc_out"] = _mm(s["sc"], dy_sc, "tn", WGRAD,nm("mm_scout_bw"))
    dscb, dscc, dsch, scw = _shortconv_bwd(proj, lay["scb"], lay["scc"], lay["sch"], D, p["sc_conv_w"], dsc, S, nm("sconv_b"))
    g["sc_conv_w"] = scw[:SC_CONV_K]
    dy, dz, dng = _gate_norm_bwd(s["y"], (proj, lay["z"]), p["ssd_norm_g"], dyn, G, S, nm("gnorm_b"))
    g["ssd_norm_g"] = dng[0]
    offs = (0, DI, DI + GN)
    dpx, dpb, dpc, ddt, hpg = _ssd_bwd(s["pre"], offs, s["dtc"], s["dtr"], p["hpc"], p["hpr"], s["hs"], dy,
                                       G, R, P, S, nm("ssd_b"))
    g["ssd_dt_bias"], g["ssd_a_log"], g["ssd_d"] = hpg[:, 0, :].reshape(H), hpg[:, 1, :].reshape(H), hpg[:, 2, :].reshape(H)
    cws, dxbc = [], []
    for name, darr, off, C in (("x", dpx, 0, DI), ("b", dpb, DI, GN), ("c", dpc, DI + GN, GN)):
        w8 = p["ssd_conv_w"][:, off:off + C]
        cws.append(_conv_bwd_w((darr, 0), (proj, lay["xbc"] + off), C, SSD_CONV_K, S, nm(f"ssdconv_bw_{name}")))
        dxbc.append(_conv_bwd_in((darr, 0), C, w8, SSD_CONV_K, S, BF16, nm(f"ssdconv_bi_{name}")))
    cws = jnp.concatenate(cws, axis=1)
    g["ssd_conv_w"], g["ssd_conv_b"] = cws[:SSD_CONV_K], cws[SSD_CONV_K]
    T = dx2.shape[0]
    ddt_t = ddt.transpose(1, 0, 2).reshape(T, H).astype(BF16)
    used = lay["dt"] + H
    dproj = jnp.concatenate([dz] + dxbc + [dscb, dscc, dsch, dg1, dg2_, ddt_t,
                                           jnp.zeros((T, lay["np"] - used), BF16)], axis=1)
    dh = _mm(dproj, p["wcat"], "nn", F32, nm("mm_in_bi"), caps=(1024, 1024, 1152))
    dwcat = _mm(dproj, s["h"], "tn", WGRAD,nm("mm_in_bw"), caps=(1152, 1024, 1024))
    o = lay
    g["w_in"] = jnp.concatenate([dwcat[o["z"]:o["scb"]], dwcat[o["dt"]:o["dt"] + H], dwcat[o["scb"]:o["dt"]]], axis=0)
    dx, dg1_, dsc1, dsh1 = _pre_bwd(s["x"], p["mix_pre_g"], mod3, 1, dh, dx1, S, nm("norm1_b"))
    g["mix_pre_g"] = dg1_[0]
    dmod = jnp.concatenate([_seq_sum(t, nb) for t in (dsh1, dsc1, dgt1, dsh2, dsc2, dgt2)], axis=1)
    return dx, dmod, g


def _fwd_bwd(x3, c, target3, layers):
    nb, S, D = x3.shape
    T = nb * S
    x = x3.reshape(T, D)
    c8 = jnp.pad(c, ((0, MOD_ROWS - nb), (0, 0)))
    preps = [_prep_layer(w) for w in layers]
    saved, mods, cact = [], [], None
    for li, p in enumerate(preps):
        mod, cact = _modk(c8, p["ada_w"], p["ada_b"], f"l{li}_mod")
        mod3 = mod[:nb].reshape(nb, 1, 6 * D)
        x, s = _layer_fwd(x, mod3, p, S, li)
        saved.append(s)
        mods.append(mod3)
    dy, lacc = _loss(x, target3.reshape(T, D), S, "loss")
    grads = [None] * len(preps)
    for li in reversed(range(len(preps))):
        dy, dmod, g = _layer_bwd(dy, mods[li], preps[li], saved[li], S, li)
        dmod8 = jnp.pad(dmod, ((0, MOD_ROWS - nb), (0, 0)))
        g["ada_b"] = _colsum(dmod8, f"l{li}_adab")
        g["ada_w"] = _mm(dmod8, cact, "tn", WGRAD,f"l{li}_mm_ada_bw", caps=(1536, 1024, 2048))
        grads[li] = g
    return lacc[0, 0], dy.reshape(nb, S, D), grads


def _colsum(a8, name):
    rows, C = a8.shape
    tc = _tile(C, 2048)

    def body(a_ref, o_ref):
        o_ref[...] = _bsum(jnp.sum(a_ref[...], axis=0, keepdims=True))

    return _pc(body, grid=(C // tc,), in_specs=[pl.BlockSpec((rows, tc), lambda j: (0, j))],
               out_specs=pl.BlockSpec((8, tc), lambda j: (0, j)), out_shape=SDS((8, C), F32), name=name)(a8)[0]


def _adam(gs, w, m, v, name):
    ns, R, W = gs.shape
    tr = _tile(R, 256, 8)

    def body(g_ref, w_ref, m_ref, v_ref, go_ref, d_ref, mo_ref, vo_ref):
        g = g_ref[0].astype(F32)
        for k in range(1, ns):
            g = g + g_ref[k].astype(F32)
        go_ref[...] = g
        d_ref[...], mo_ref[...], vo_ref[...] = _adam_update(g, w_ref[...], m_ref[...], v_ref[...])

    row = pl.BlockSpec((tr, W), lambda i: (i, 0))
    return _pc(body, grid=(R // tr,), in_specs=[pl.BlockSpec((ns, tr, W), lambda i: (0, i, 0)), row, row, row],
               out_specs=[row] * 4, out_shape=[SDS((R, W), F32)] * 4, name=name)(gs, w, m, v)


def _adam_update(g, w, m, v):
    c1 = 1.0 / (1.0 - ADAM_B1 ** ADAM_STEP)
    c2 = 1.0 / (1.0 - ADAM_B2 ** ADAM_STEP)
    m_ = ADAM_B1 * m + (1.0 - ADAM_B1) * g
    v_ = ADAM_B2 * v + (1.0 - ADAM_B2) * (g * g)
    return -ADAM_LR * ((m_ * c1) / (jnp.sqrt(v_ * c2) + ADAM_EPS) + ADAM_WD * w), m_, v_


def _adam_nat(g, w, m, v, name):
    depth, a, b = w.shape
    tr = _tile(a, 256, 8)

    def body(g_ref, w_ref, m_ref, v_ref, d_ref, mo_ref, vo_ref):
        d_ref[...], mo_ref[...], vo_ref[...] = _adam_update(g_ref[...], w_ref[...], m_ref[...], v_ref[...])

    blk = pl.BlockSpec((None, tr, b), lambda l, i: (l, i, 0))
    return _pc(body, grid=(depth, a // tr), in_specs=[blk] * 4, out_specs=[blk] * 3,
               out_shape=[SDS(w.shape, F32)] * 3, name=name)(g, w, m, v)


def _sum_chips(gs, name):
    ns, R, W = gs.shape
    tr = _tile(R, 256, 16)

    def body(g_ref, o_ref):
        acc = g_ref[0].astype(F32)
        for k in range(1, ns):
            acc = acc + g_ref[k].astype(F32)
        o_ref[...] = acc

    return _pc(body, grid=(R // tr,), in_specs=[pl.BlockSpec((ns, tr, W), lambda i: (0, i, 0))],
               out_specs=pl.BlockSpec((tr, W), lambda i: (i, 0)), out_shape=SDS((R, W), F32), name=name)(gs)


HBM_SPEC = pl.BlockSpec(memory_space=pltpu.HBM)
VMEM_SPEC = pl.BlockSpec(memory_space=pltpu.VMEM)


def _dev():
    return lax.axis_index("x"), lax.axis_index("y"), lax.axis_index("c")


def _allgather_big(loc, name):
    R, W = loc.shape

    def body(x_ref, out_ref, send_sems, recv_sems, local_sem):
        x, y, c = _dev()
        me, sibling = (x, y, c), (x, y, 1 - c)
        chips = [(1 - x, y), (x, 1 - y), (1 - x, 1 - y)]

        def slab(px, py, pc):
            return out_ref.at[4 * px + 2 * py + pc]

        def copy(k, block, to, src=None):
            return pltpu.make_async_remote_copy(
                src_ref=slab(*block) if src is None else src, dst_ref=slab(*block),
                send_sem=send_sems.at[k], recv_sem=recv_sems.at[k], device_id=to, device_id_type=MESH)

        mine = pltpu.make_async_copy(x_ref, slab(*me), local_sem)
        mine.start()
        first = [copy(0, me, sibling, src=x_ref)]
        first += [copy(1 + j, me, (*chip, c), src=x_ref) for j, chip in enumerate(chips)]
        for cp in first:
            cp.start()
        passed = [copy(4 + j, (*chip, c), sibling) for j, chip in enumerate(chips)]
        for j, chip in enumerate(chips):
            copy(1 + j, (*chip, c), me).wait_recv()
            passed[j].start()
        copy(0, sibling, me).wait_recv()
        for j, chip in enumerate(chips):
            copy(4 + j, (*chip, 1 - c), me).wait_recv()
        for cp in first + passed:
            cp.wait_send()
        mine.wait()

    return pl.pallas_call(
        body, out_shape=SDS((N_DEV, R, W), loc.dtype), in_specs=[HBM_SPEC], out_specs=HBM_SPEC,
        scratch_shapes=[pltpu.SemaphoreType.DMA((7,)), pltpu.SemaphoreType.DMA((7,)), pltpu.SemaphoreType.DMA],
        name=name)(loc)


def _rs_pair_exchange(gc, name):
    _, nj, R, W = gc.shape

    def body(g_ref, out_ref, send_sem, recv_sem):
        x, y, c = _dev()
        cp = pltpu.make_async_remote_copy(src_ref=g_ref.at[1 - c], dst_ref=out_ref, send_sem=send_sem, recv_sem=recv_sem,
                                          device_id=(x, y, 1 - c), device_id_type=MESH)
        cp.start()
        cp.wait()

    return pl.pallas_call(
        body, out_shape=SDS((nj, R, W), gc.dtype), in_specs=[HBM_SPEC], out_specs=HBM_SPEC,
        scratch_shapes=[pltpu.SemaphoreType.DMA, pltpu.SemaphoreType.DMA], name=name)(gc)


def _add_pairs(gc, ra, name):
    _, nj, R, W = gc.shape
    tr = _tile(R, 256, 8)
    cidx = lax.axis_index("c").astype(jnp.int32).reshape(1)

    def body(c_ref, a_ref, b_ref, o_ref):
        o_ref[...] = (a_ref[...].astype(F32) + b_ref[...].astype(F32)).astype(o_ref.dtype)

    gs = pltpu.PrefetchScalarGridSpec(
        num_scalar_prefetch=1, grid=(nj, R // tr),
        in_specs=[pl.BlockSpec((None, None, tr, W), lambda j, i, cr: (cr[0], j, i, 0)),
                  pl.BlockSpec((None, tr, W), lambda j, i, cr: (j, i, 0))],
        out_specs=pl.BlockSpec((None, tr, W), lambda j, i, cr: (j, i, 0)))
    return pl.pallas_call(body, grid_spec=gs, out_shape=SDS((nj, R, W), gc.dtype), name=name,
                          compiler_params=pltpu.CompilerParams(vmem_limit_bytes=VMEM_LIMIT))(cidx, gc, ra)


def _rs_chip_exchange(p, name):
    nj, R, W = p.shape

    def body(p_ref, out_ref, send_sems, recv_sems, local_sem):
        x, y, c = _dev()
        j0 = 2 * x + y
        chips = [(1 - x, y), (x, 1 - y), (1 - x, 1 - y)]
        mine = pltpu.make_async_copy(p_ref.at[j0], out_ref.at[j0], local_sem)
        mine.start()

        def copy(k, chip):
            return pltpu.make_async_remote_copy(
                src_ref=p_ref.at[2 * chip[0] + chip[1]], dst_ref=out_ref.at[j0],
                send_sem=send_sems.at[k], recv_sem=recv_sems.at[k], device_id=(*chip, c), device_id_type=MESH)

        sent = [copy(k, chip) for k, chip in enumerate(chips)]
        for cp in sent:
            cp.start()
        for k, chip in enumerate(chips):
            pltpu.make_async_remote_copy(
                src_ref=p_ref.at[j0], dst_ref=out_ref.at[2 * chip[0] + chip[1]],
                send_sem=send_sems.at[k], recv_sem=recv_sems.at[k], device_id=(*chip, c), device_id_type=MESH).wait_recv()
        for cp in sent:
            cp.wait_send()
        mine.wait()

    return pl.pallas_call(
        body, out_shape=SDS((nj, R, W), p.dtype), in_specs=[HBM_SPEC], out_specs=HBM_SPEC,
        scratch_shapes=[pltpu.SemaphoreType.DMA((3,)), pltpu.SemaphoreType.DMA((3,)), pltpu.SemaphoreType.DMA],
        name=name)(p)


def _allgather_small(v, name):
    R, W = v.shape

    def body(v_ref, out_ref, send_sems, recv_sems, local_sem):
        x, y, c = _dev()
        mine = pltpu.make_async_copy(v_ref, out_ref.at[4 * x + 2 * y + c], local_sem)
        mine.start()
        peers = []
        for k in range(1, N_DEV):
            px = 1 - x if k & 4 else x
            py = 1 - y if k & 2 else y
            pc_ = 1 - c if k & 1 else c
            peers.append((px, py, pc_))
        sent = [pltpu.make_async_remote_copy(
            src_ref=v_ref, dst_ref=out_ref.at[4 * x + 2 * y + c], send_sem=send_sems.at[k], recv_sem=recv_sems.at[k],
            device_id=peer, device_id_type=MESH) for k, peer in enumerate(peers)]
        for cp in sent:
            cp.start()
        for k, (px, py, pc_) in enumerate(peers):
            pltpu.make_async_remote_copy(
                src_ref=v_ref, dst_ref=out_ref.at[4 * px + 2 * py + pc_], send_sem=send_sems.at[k],
                recv_sem=recv_sems.at[k], device_id=(px, py, pc_), device_id_type=MESH).wait_recv()
        for cp in sent:
            cp.wait_send()
        mine.wait()

    return pl.pallas_call(
        body, out_shape=SDS((N_DEV, R, W), v.dtype), in_specs=[VMEM_SPEC], out_specs=VMEM_SPEC,
        scratch_shapes=[pltpu.SemaphoreType.DMA((7,)), pltpu.SemaphoreType.DMA((7,)), pltpu.SemaphoreType.DMA],
        name=name)(v)


def _sum_slabs(a, name):
    ns, R, W = a.shape

    def body(a_ref, o_ref):
        acc = a_ref[0]
        for k in range(1, ns):
            acc = acc + a_ref[k]
        o_ref[...] = acc

    return pl.pallas_call(body, out_shape=SDS((R, W), a.dtype), in_specs=[VMEM_SPEC], out_specs=VMEM_SPEC, name=name)(a)


BIG = (("ada_w", "col"), ("w_in", "col"), ("w_ssd_out", "row"), ("w_sc_out", "row"), ("w_o", "row"), ("w_up", "col"),
       ("w_down", "row"))
CONVW = ("ssd_conv_w", "sc_conv_w", "ffn_conv_w")
REPL = ("ada_b", "mix_pre_g", "mix_post_g", "ssd_conv_b", "ssd_dt_bias", "ssd_a_log", "ssd_d", "ssd_norm_g", "ffn_pre_g",
        "ffn_post_g", "ffn_conv_b")
WEIGHTS = ("ada_w", "ada_b", "mix_pre_g", "mix_post_g", "w_in", "ssd_conv_w", "ssd_conv_b", "ssd_dt_bias", "ssd_a_log",
           "ssd_d", "ssd_norm_g", "w_ssd_out", "sc_conv_w", "w_sc_out", "w_o", "ffn_pre_g", "ffn_post_g", "w_up",
           "ffn_conv_w", "ffn_conv_b", "w_down")


def _pad_rows(a, mult):
    r = a.shape[-2]
    pad = -r % mult
    return a if pad == 0 else jnp.pad(a, [(0, 0)] * (a.ndim - 2) + [(0, pad), (0, 0)])


def _flat_rows(parts, mult):
    flat = jnp.concatenate([p.reshape(-1) for p in parts])
    flat = jnp.pad(flat, (0, -flat.shape[0] % ROW_W))
    return _pad_rows(flat.reshape(-1, ROW_W), mult)


def _unflat(buf, shapes):
    flat = buf.reshape(-1)
    out, o = [], 0
    for shp in shapes:
        n = 1
        for s in shp:
            n *= s
        out.append(flat[o:o + n].reshape(shp))
        o += n
    return out


def _pack_big_local(get, depth):
    return jnp.concatenate([_pad_rows((get(n)[l].T if kind == "col" else get(n)[l]).reshape(-1, ROW_W), SLAB_ALIGN)
                            for l in range(depth) for n, kind in BIG], axis=0)


def _big_rows(shapes, depth):
    out, o = {}, 0
    for l in range(depth):
        for n, _ in BIG:
            r = shapes[n][1] * shapes[n][2] // ROW_W
            out[(l, n)] = (o, o + r)
            o += -(-r // SLAB_ALIGN) * SLAB_ALIGN
    return out, o


def kernel(x, c, ada_w, ada_b, mix_pre_g, mix_post_g, w_in, ssd_conv_w, ssd_conv_b, ssd_dt_bias, ssd_a_log, ssd_d, ssd_norm_g, w_ssd_out, sc_conv_w, w_sc_out, w_o, ffn_pre_g, ffn_post_g, w_up, ffn_conv_w, ffn_conv_b, w_down, loss_target, m_ada_w, m_ada_b, m_mix_pre_g, m_mix_post_g, m_w_in, m_ssd_conv_w, m_ssd_conv_b, m_ssd_dt_bias, m_ssd_a_log, m_ssd_d, m_ssd_norm_g, m_w_ssd_out, m_sc_conv_w, m_w_sc_out, m_w_o, m_ffn_pre_g, m_ffn_post_g, m_w_up, m_ffn_conv_w, m_ffn_conv_b, m_w_down, v_ada_w, v_ada_b, v_mix_pre_g, v_mix_post_g, v_w_in, v_ssd_conv_w, v_ssd_conv_b, v_ssd_dt_bias, v_ssd_a_log, v_ssd_d, v_ssd_norm_g, v_w_ssd_out, v_sc_conv_w, v_w_sc_out, v_w_o, v_ffn_pre_g, v_ffn_post_g, v_w_up, v_ffn_conv_w, v_ffn_conv_b, v_w_down):
    wl = dict(zip(WEIGHTS, (ada_w, ada_b, mix_pre_g, mix_post_g, w_in, ssd_conv_w, ssd_conv_b, ssd_dt_bias, ssd_a_log,
                            ssd_d, ssd_norm_g, w_ssd_out, sc_conv_w, w_sc_out, w_o, ffn_pre_g, ffn_post_g, w_up,
                            ffn_conv_w, ffn_conv_b, w_down)))
    ml = dict(zip(WEIGHTS, (m_ada_w, m_ada_b, m_mix_pre_g, m_mix_post_g, m_w_in, m_ssd_conv_w, m_ssd_conv_b,
                            m_ssd_dt_bias, m_ssd_a_log, m_ssd_d, m_ssd_norm_g, m_w_ssd_out, m_sc_conv_w, m_w_sc_out, m_w_o,
                            m_ffn_pre_g, m_ffn_post_g, m_w_up, m_ffn_conv_w, m_ffn_conv_b, m_w_down)))
    vl = dict(zip(WEIGHTS, (v_ada_w, v_ada_b, v_mix_pre_g, v_mix_post_g, v_w_in, v_ssd_conv_w, v_ssd_conv_b,
                            v_ssd_dt_bias, v_ssd_a_log, v_ssd_d, v_ssd_norm_g, v_w_ssd_out, v_sc_conv_w, v_w_sc_out, v_w_o,
                            v_ffn_pre_g, v_ffn_post_g, v_w_up, v_ffn_conv_w, v_ffn_conv_b, v_w_down)))
    depth = ada_w.shape[0]
    shapes = {n: wl[n].shape for n in WEIGHTS}
    me = 4 * lax.axis_index("x") + 2 * lax.axis_index("y") + lax.axis_index("c")

    rows, n_big = _big_rows(shapes, depth)
    big_loc = _pack_big_local(lambda n: wl[n].astype(BF16), depth)
    conv_flat = jnp.concatenate([wl[n][l].reshape(-1) for l in range(depth) for n in CONVW])
    n_conv = conv_flat.shape[0]
    conv_flat = jnp.pad(conv_flat, (0, -n_conv % (ROW_W // 2)))
    conv_rows = lax.bitcast_convert_type(conv_flat, BF16).reshape(-1, ROW_W)
    gathered = _allgather_big(_pad_rows(jnp.concatenate([big_loc, conv_rows], axis=0), ROW_PAD), "allgather_weights")
    conv_all = lax.bitcast_convert_type(
        gathered[:, n_big:n_big + conv_rows.shape[0]].reshape(N_DEV, -1, 2), F32)[:, :n_conv]
    conv_full, o = {}, 0
    for l in range(depth):
        for n in CONVW:
            k, cl = shapes[n][1], shapes[n][2]
            conv_full[(l, n)] = conv_all[:, o:o + k * cl].reshape(N_DEV, k, cl).transpose(1, 0, 2).reshape(k, N_DEV * cl)
            o += k * cl
    layers = []
    for l in range(depth):
        w = {n: wl[n][l] for n in REPL}
        for n, kind in BIG:
            r0, r1 = rows[(l, n)]
            a, b = shapes[n][1], shapes[n][2]
            blk = gathered[:, r0:r1]
            w[n] = blk.reshape(N_DEV * b, a) if kind == "col" else blk.reshape(N_DEV * a, b)
        for n in CONVW:
            w[n] = conv_full[(l, n)]
        layers.append(w)

    loss_loc, dx, grads = _fwd_bwd(x, c, loss_target, layers)

    slabs = []
    for l in range(depth):
        for n, kind in BIG:
            slabs.append(_pad_rows(grads[l][n].astype(BF16).reshape(N_DEV, -1, ROW_W), SLAB_ALIGN))
    gslab = _pad_rows(jnp.concatenate(slabs, axis=1), ROW_PAD)
    rg = gslab.shape[1]
    gc = gslab.reshape(4, 2, rg, ROW_W).transpose(1, 0, 2, 3)
    from_sibling = _rs_pair_exchange(gc, "rs_pair_exchange")
    chip_sums = _add_pairs(gc, from_sibling, "rs_pair_add")
    from_chips = _rs_chip_exchange(chip_sums, "rs_chip_exchange")
    g_sum = _sum_chips(from_chips, "rs_chip_sum")

    def grad_of(l, n, kind):
        blk = g_sum[rows[(l, n)][0]:rows[(l, n)][1]]
        a, b = shapes[n][1], shapes[n][2]
        return blk.reshape(b, a).T if kind == "col" else blk.reshape(a, b)

    g_big = {n: jnp.stack([grad_of(l, n, kind) for l in range(depth)]) for n, kind in BIG}
    d_big, m_big, v_big = {}, {}, {}
    for n, _ in BIG:
        d_big[n], m_big[n], v_big[n] = _adam_nat(g_big[n], wl[n], ml[n], vl[n], f"adam_{n}")

    parts = [jnp.broadcast_to(loss_loc, (ROW_W,))]
    small_shapes = [(ROW_W,)]
    for l in range(depth):
        for n in REPL + CONVW:
            parts.append(grads[l][n])
            small_shapes.append(tuple(grads[l][n].shape))
    total = _sum_slabs(_allgather_small(_flat_rows(parts, 8), "allgather_small"), "sum_small")
    pieces = _unflat(total, small_shapes)
    loss = pieces[0][0]
    g_small, i = {}, 1
    for l in range(depth):
        for n in REPL + CONVW:
            gp = pieces[i]
            i += 1
            if n in CONVW:
                gp = lax.dynamic_slice_in_dim(gp, me * shapes[n][2], shapes[n][2], axis=1)
            g_small[(l, n)] = gp
    order = [(l, n) for l in range(depth) for n in REPL + CONVW]
    loc_shapes = [tuple(shapes[n][1:]) for _, n in order]
    packs = lambda f: _flat_rows([f(l, n) for l, n in order], 8)
    gs_small = packs(lambda l, n: g_small[(l, n)])
    _, d_sm, m_sm, v_sm = _adam(gs_small[None], packs(lambda l, n: wl[n][l]), packs(lambda l, n: ml[n][l]),
                                packs(lambda l, n: vl[n][l]), "adam_small")

    def unpack_small(buf):
        ps = _unflat(buf, loc_shapes)
        return {n: jnp.stack([ps[order.index((l, n))] for l in range(depth)]) for n in REPL + CONVW}

    outs = []
    for big, small in ((g_big, {n: jnp.stack([g_small[(l, n)] for l in range(depth)]) for n in REPL + CONVW}),
                       (d_big, unpack_small(d_sm)), (m_big, unpack_small(m_sm)), (v_big, unpack_small(v_sm))):
        merged = {**big, **small}
        outs += [merged[n] for n in WEIGHTS]
    return (loss, dx, *outs)
```

```python
import math

import jax
import jax.numpy as jnp
from jax import lax
from jax.experimental import pallas as pl
from jax.experimental.pallas import tpu as pltpu

F32, BF16 = jnp.float32, jnp.bfloat16
WGRAD = BF16
SDS = jax.ShapeDtypeStruct
MESH = pl.DeviceIdType.MESH

EPS = 1e-6
N_STATE = 128
CHUNK = 128
SSD_CONV_K, SC_CONV_K, FFN_CONV_K = 4, 3, 3
N_DEV = 8
ROW_W = 1024
ROW_PAD = 256
SLAB_ALIGN = 16
SEG_BLK = 512
GLU_W = 256
MOD_ROWS = 128
VMEM_LIMIT = 48 * 2**20

ADAM_LR, ADAM_B1, ADAM_B2, ADAM_EPS, ADAM_WD, ADAM_STEP = 0.001, 0.9, 0.999, 1e-08, 0.01, 10

NT = (((1,), (1,)), ((), ()))
TN = (((0,), (0,)), ((), ()))
NN = (((1,), (0,)), ((), ()))


def _tile(n, cap, mult=128):
    best = None
    for t in range(mult, min(n, cap) + 1, mult):
        if n % t == 0:
            best = t
    return best if best is not None else n


def _pc(body, *, grid, in_specs, out_specs, out_shape, name, scratch=()):
    return pl.pallas_call(
        body, grid=grid, in_specs=in_specs, out_specs=out_specs, out_shape=out_shape,
        scratch_shapes=list(scratch), name=name,
        compiler_params=pltpu.CompilerParams(
            dimension_semantics=("arbitrary",) * len(grid), vmem_limit_bytes=VMEM_LIMIT))


def _silu(x):
    return x * jax.nn.sigmoid(x)


def _dsilu(x):
    s = jax.nn.sigmoid(x)
    return s * (1.0 + x * (1.0 - s))


def _softplus(x):
    return jnp.maximum(x, 0.0) + jnp.log(1.0 + jnp.exp(-jnp.abs(x)))


def _dot(a, b, dims=NN):
    return lax.dot_general(a, b, dims, preferred_element_type=F32)


def _bsum(v, rows=8):
    return jnp.broadcast_to(v, (rows, v.shape[1]))


def _mm(a, b, mode, out_dtype, name, caps=(1024, 1024, 2048)):
    if mode == "nn":
        (M, K), (K2, N) = a.shape, b.shape
    elif mode == "nt":
        (M, K), (N, K2) = a.shape, b.shape
    else:
        (K, M), (K2, N) = a.shape, b.shape
    assert K == K2, (a.shape, b.shape, mode)
    tm, tn, tk = _tile(M, caps[0]), _tile(N, caps[1]), _tile(K, caps[2])
    nk = K // tk
    dims = {"nn": NN, "nt": NT, "tn": TN}[mode]
    if mode == "tn":
        a_spec = pl.BlockSpec((tk, tm), lambda i, j, k: (k, i))
    else:
        a_spec = pl.BlockSpec((tm, tk), lambda i, j, k: (i, k))
    if mode == "nt":
        b_spec = pl.BlockSpec((tn, tk), lambda i, j, k: (j, k))
    else:
        b_spec = pl.BlockSpec((tk, tn), lambda i, j, k: (k, j))

    def body(a_ref, b_ref, o_ref, *acc):
        part = _dot(a_ref[...].astype(BF16), b_ref[...].astype(BF16), dims)
        if nk == 1:
            o_ref[...] = part.astype(o_ref.dtype)
        else:
            acc_ref, = acc
            k = pl.program_id(2)

            @pl.when(k == 0)
            def _():
                acc_ref[...] = part

            @pl.when(k > 0)
            def _():
                acc_ref[...] += part

            @pl.when(k == nk - 1)
            def _():
                o_ref[...] = acc_ref[...].astype(o_ref.dtype)

    return _pc(body, grid=(M // tm, N // tn, nk), in_specs=[a_spec, b_spec],
               out_specs=pl.BlockSpec((tm, tn), lambda i, j, k: (i, j)),
               out_shape=SDS((M, N), out_dtype), name=name,
               scratch=() if nk == 1 else (pltpu.VMEM((tm, tn), F32),))(a, b)


def _mm_seg(segs, b, mode, out_dtype, name, blk, tile=1024, tk=2048):
    nblk = [a.shape[1] // blk for a in segs]
    assert all(a.shape[1] % blk == 0 for a in segs)
    start = [sum(nblk[:s]) for s in range(len(segs))]
    total = sum(nblk)
    ns = len(segs)
    N = b.shape[1]
    tn = _tile(N, tile)
    if mode == "nn":
        M = segs[0].shape[0]
        tm = _tile(M, tile)
        grid = (M // tm, N // tn, total)
        a_specs = [pl.BlockSpec((tm, blk), lambda i, j, k, k0=k0, n=n: (i, jnp.clip(k - k0, 0, n - 1)))
                   for k0, n in zip(start, nblk)]
        b_spec = pl.BlockSpec((blk, tn), lambda i, j, k: (k, j))
        out_rows, tmo, dims, seg_axis = M, tm, NN, 2
    else:
        K = segs[0].shape[0]
        tkk = _tile(K, tk)
        grid = (total, N // tn, K // tkk)
        a_specs = [pl.BlockSpec((tkk, blk), lambda i, j, k, i0=i0, n=n: (k, jnp.clip(i - i0, 0, n - 1)))
                   for i0, n in zip(start, nblk)]
        b_spec = pl.BlockSpec((tkk, tn), lambda i, j, k: (k, j))
        out_rows, tmo, dims, seg_axis = total * blk, blk, TN, 0
    nk = grid[2]

    def body(*refs):
        a_refs, b_ref, o_ref, acc_ref = refs[:ns], refs[ns], refs[ns + 1], refs[ns + 2]
        k = pl.program_id(2)
        sel = pl.program_id(seg_axis)

        @pl.when(k == 0)
        def _():
            acc_ref[...] = jnp.zeros_like(acc_ref)

        for s in range(ns):
            @pl.when((sel >= start[s]) & (sel < start[s] + nblk[s]))
            def _(s=s):
                acc_ref[...] += _dot(a_refs[s][...].astype(BF16), b_ref[...].astype(BF16), dims)

        @pl.when(k == nk - 1)
        def _():
            o_ref[...] = acc_ref[...].astype(o_ref.dtype)

    return _pc(body, grid=grid, in_specs=a_specs + [b_spec], out_specs=pl.BlockSpec((tmo, tn), lambda i, j, k: (i, j)),
               out_shape=SDS((out_rows, N), out_dtype), name=name, scratch=(pltpu.VMEM((tmo, tn), F32),))(*segs, b)


def _modk(c8, ada_w, ada_b, name):
    rows, D = c8.shape
    N = ada_w.shape[0]
    tn = _tile(N, 1536)

    def body(c_ref, w_ref, b_ref, mod_ref, ca_ref):
        ca = _silu(c_ref[...]).astype(BF16)
        mod_ref[...] = _dot(ca, w_ref[...], NT) + b_ref[...]
        ca_ref[...] = ca

    return _pc(body, grid=(N // tn,),
               in_specs=[pl.BlockSpec((rows, D), lambda j: (0, 0)), pl.BlockSpec((tn, D), lambda j: (j, 0)),
                         pl.BlockSpec((1, tn), lambda j: (0, j))],
               out_specs=[pl.BlockSpec((rows, tn), lambda j: (0, j)), pl.BlockSpec((rows, D), lambda j: (0, 0))],
               out_shape=[SDS((rows, N), F32), SDS((rows, D), BF16)], name=name)(c8, ada_w, ada_b)


def _row_tile(S):
    return _tile(S, 512, 8)


def _norm_mod(x, g, mod3, sc_seg, sh_seg, S, name):
    T, D = x.shape
    tm = _row_tile(S)
    tpb = S // tm

    def body(x_ref, g_ref, sc_ref, sh_ref, h_ref):
        x_ = x_ref[...]
        r = lax.rsqrt(jnp.mean(x_ * x_, axis=-1, keepdims=True) + EPS)
        h_ref[...] = ((x_ * r) * g_ref[...] * (1.0 + sc_ref[...]) + sh_ref[...]).astype(BF16)

    return _pc(body, grid=(T // tm,),
               in_specs=[pl.BlockSpec((tm, D), lambda i: (i, 0)), pl.BlockSpec((1, D), lambda i: (0, 0)),
                         pl.BlockSpec((None, 1, D), lambda i: (i // tpb, 0, sc_seg)),
                         pl.BlockSpec((None, 1, D), lambda i: (i // tpb, 0, sh_seg))],
               out_specs=pl.BlockSpec((tm, D), lambda i: (i, 0)), out_shape=SDS((T, D), BF16), name=name)(x, g, mod3, mod3)


def _resid_post(x, fo, mod3, gt_seg, pg, S, name):
    T, D = x.shape
    tm = _row_tile(S)
    tpb = S // tm

    def body(x_ref, f_ref, gt_ref, pg_ref, o_ref):
        f = f_ref[...]
        r = lax.rsqrt(jnp.mean(f * f, axis=-1, keepdims=True) + EPS)
        o_ref[...] = x_ref[...] + gt_ref[...] * ((f * r) * pg_ref[...])

    return _pc(body, grid=(T // tm,),
               in_specs=[pl.BlockSpec((tm, D), lambda i: (i, 0)), pl.BlockSpec((tm, D), lambda i: (i, 0)),
                         pl.BlockSpec((None, 1, D), lambda i: (i // tpb, 0, gt_seg)),
                         pl.BlockSpec((1, D), lambda i: (0, 0))],
               out_specs=pl.BlockSpec((tm, D), lambda i: (i, 0)), out_shape=SDS((T, D), F32), name=name)(x, fo, mod3, pg)


def _post_bwd(fo, mod3, gt_seg, pg, dout, S, name):
    T, D = fo.shape
    tm = _row_tile(S)
    tpb = S // tm
    nb = T // S

    def body(f_ref, gt_ref, pg_ref, d_ref, df_ref, dgt_ref, dpg_ref):
        i = pl.program_id(0)

        @pl.when(i == 0)
        def _():
            dpg_ref[...] = jnp.zeros_like(dpg_ref)

        @pl.when(i % tpb == 0)
        def _():
            dgt_ref[...] = jnp.zeros_like(dgt_ref)

        f, d, gt, pg_ = f_ref[...], d_ref[...], gt_ref[...], pg_ref[...]
        r = lax.rsqrt(jnp.mean(f * f, axis=-1, keepdims=True) + EPS)
        n = f * r
        dn = d * gt * pg_
        df_ref[...] = (r * (dn - n * jnp.mean(dn * n, axis=-1, keepdims=True))).astype(df_ref.dtype)
        dn_ = d * n
        dgt_ref[...] += _bsum(jnp.sum(dn_ * pg_, axis=0, keepdims=True))
        dpg_ref[...] += _bsum(jnp.sum(dn_ * gt, axis=0, keepdims=True))

    return _pc(body, grid=(T // tm,),
               in_specs=[pl.BlockSpec((tm, D), lambda i: (i, 0)),
                         pl.BlockSpec((None, 1, D), lambda i: (i // tpb, 0, gt_seg)),
                         pl.BlockSpec((1, D), lambda i: (0, 0)), pl.BlockSpec((tm, D), lambda i: (i, 0))],
               out_specs=[pl.BlockSpec((tm, D), lambda i: (i, 0)), pl.BlockSpec((8, D), lambda i: (i // tpb, 0)),
                          pl.BlockSpec((8, D), lambda i: (0, 0))],
               out_shape=[SDS((T, D), BF16), SDS((nb * 8, D), F32), SDS((8, D), F32)], name=name)(fo, mod3, pg, dout)


def _pre_bwd(x, g, mod3, sc_seg, dh, dout, S, name):
    T, D = x.shape
    tm = _row_tile(S)
    tpb = S // tm
    nb = T // S

    def body(x_ref, g_ref, sc_ref, dh_ref, d_ref, dx_ref, dg_ref, dsc_ref, dsh_ref):
        i = pl.program_id(0)

        @pl.when(i == 0)
        def _():
            dg_ref[...] = jnp.zeros_like(dg_ref)

        @pl.when(i % tpb == 0)
        def _():
            dsc_ref[...] = jnp.zeros_like(dsc_ref)
            dsh_ref[...] = jnp.zeros_like(dsh_ref)

        x_, g_, dh_ = x_ref[...], g_ref[...], dh_ref[...]
        one_sc = 1.0 + sc_ref[...]
        r = lax.rsqrt(jnp.mean(x_ * x_, axis=-1, keepdims=True) + EPS)
        n = x_ * r
        dn = dh_ * g_ * one_sc
        dx_ref[...] = d_ref[...] + r * (dn - n * jnp.mean(dn * n, axis=-1, keepdims=True))
        dhn = dh_ * n
        dg_ref[...] += _bsum(jnp.sum(dhn * one_sc, axis=0, keepdims=True))
        dsc_ref[...] += _bsum(jnp.sum(dhn * g_, axis=0, keepdims=True))
        dsh_ref[...] += _bsum(jnp.sum(dh_, axis=0, keepdims=True))

    row = pl.BlockSpec((tm, D), lambda i: (i, 0))
    return _pc(body, grid=(T // tm,),
               in_specs=[row, pl.BlockSpec((1, D), lambda i: (0, 0)),
                         pl.BlockSpec((None, 1, D), lambda i: (i // tpb, 0, sc_seg)), row, row],
               out_specs=[row, pl.BlockSpec((8, D), lambda i: (0, 0)), pl.BlockSpec((8, D), lambda i: (i // tpb, 0)),
                          pl.BlockSpec((8, D), lambda i: (i // tpb, 0))],
               out_shape=[SDS((T, D), F32), SDS((8, D), F32), SDS((nb * 8, D), F32), SDS((nb * 8, D), F32)],
               name=name)(x, g, mod3, dh, dout)


def _loss(y, target, S, name):
    T, D = y.shape
    tm = _row_tile(S)

    def body(y_ref, t_ref, dy_ref, l_ref):
        @pl.when(pl.program_id(0) == 0)
        def _():
            l_ref[...] = jnp.zeros_like(l_ref)

        e = y_ref[...] - t_ref[...]
        dy_ref[...] = e * (1.0 / D)
        l_ref[...] += jnp.broadcast_to(jnp.sum(e * e, keepdims=True) * (0.5 / D), l_ref.shape)

    row = pl.BlockSpec((tm, D), lambda i: (i, 0))
    return _pc(body, grid=(T // tm,), in_specs=[row, row],
               out_specs=[row, pl.BlockSpec((8, 128), lambda i: (0, 0))],
               out_shape=[SDS((T, D), F32), SDS((8, 128), F32)], name=name)(y, target)


def _conv_geom(view, C, S):
    arr, off = view
    T = arr.shape[0]
    tm = _row_tile(S)
    tc = _tile(C, 512)
    assert off % tc == 0 and C % tc == 0
    return arr, off // tc, T, tm, tc, S // tm


def _prev_spec(tm, tc, ob, order):
    if order == "ij":
        return pl.BlockSpec((8, tc), lambda i, j: (jnp.maximum(i * (tm // 8) - 1, 0), ob + j))
    return pl.BlockSpec((8, tc), lambda j, i: (jnp.maximum(i * (tm // 8) - 1, 0), ob + j))


def _next_spec(T, tm, tc, ob, order):
    last = T // 8 - 1
    if order == "ij":
        return pl.BlockSpec((8, tc), lambda i, j: (jnp.minimum((i + 1) * (tm // 8), last), ob + j))
    return pl.BlockSpec((8, tc), lambda j, i: (jnp.minimum((i + 1) * (tm // 8), last), ob + j))


def _shifted(buf, cur, prev, tm):
    buf[0:8, :] = prev
    buf[8:, :] = cur
    return lambda j: buf[pl.ds(8 - j, tm), :]


def _conv_fwd(view, C, w8, b, K, S, name):
    arr, ob, T, tm, tc, tps = _conv_geom(view, C, S)

    def body(u_ref, p_ref, w_ref, b_ref, o_ref, buf):
        first = (pl.program_id(0) % tps) == 0
        u = u_ref[...]
        sh = _shifted(buf, u, jnp.where(first, 0.0, p_ref[...]), tm)
        acc = u * w_ref[K - 1:K, :] + b_ref[...]
        for j in range(1, K):
            acc = acc + sh(j) * w_ref[K - 1 - j:K - j, :]
        o_ref[...] = acc

    return _pc(body, grid=(T // tm, C // tc),
               in_specs=[pl.BlockSpec((tm, tc), lambda i, j: (i, ob + j)), _prev_spec(tm, tc, ob, "ij"),
                         pl.BlockSpec((8, tc), lambda i, j: (0, j)), pl.BlockSpec((1, tc), lambda i, j: (0, j))],
               out_specs=pl.BlockSpec((tm, tc), lambda i, j: (i, j)), out_shape=SDS((T, C), F32), name=name,
               scratch=(pltpu.VMEM((tm + 8, tc), F32),))(arr, arr, w8, b)


def _conv_bwd_in(dview, C, w8, K, S, out_dtype, name):
    arr, ob, T, tm, tc, tps = _conv_geom(dview, C, S)

    def body(d_ref, n_ref, w_ref, o_ref, buf):
        last = (pl.program_id(0) % tps) == tps - 1
        d = d_ref[...]
        buf[0:tm, :] = d
        buf[tm:tm + 8, :] = jnp.where(last, 0.0, n_ref[...])
        acc = d * w_ref[K - 1:K, :]
        for j in range(1, K):
            acc = acc + buf[pl.ds(j, tm), :] * w_ref[K - 1 - j:K - j, :]
        o_ref[...] = acc.astype(o_ref.dtype)

    return _pc(body, grid=(T // tm, C // tc),
               in_specs=[pl.BlockSpec((tm, tc), lambda i, j: (i, ob + j)), _next_spec(T, tm, tc, ob, "ij"),
                         pl.BlockSpec((8, tc), lambda i, j: (0, j))],
               out_specs=pl.BlockSpec((tm, tc), lambda i, j: (i, j)), out_shape=SDS((T, C), out_dtype), name=name,
               scratch=(pltpu.VMEM((tm + 8, tc), F32),))(arr, arr, w8)


def _conv_bwd_w(dview, uview, C, K, S, name):
    darr, dob, T, tm, tc, tps = _conv_geom(dview, C, S)
    uarr, uob, _, _, _, _ = _conv_geom(uview, C, S)

    def body(d_ref, u_ref, p_ref, o_ref, buf):
        i = pl.program_id(1)

        @pl.when(i == 0)
        def _():
            o_ref[...] = jnp.zeros_like(o_ref)

        first = (i % tps) == 0
        d, u = d_ref[...], u_ref[...]
        sh = _shifted(buf, u, jnp.where(first, 0.0, p_ref[...]), tm)
        rows = []
        for k in range(K):
            j = K - 1 - k
            rows.append(jnp.sum(d * (u if j == 0 else sh(j)), axis=0, keepdims=True))
        rows.append(jnp.sum(d, axis=0, keepdims=True))
        rows.append(jnp.zeros((8 - K - 1, tc), F32))
        o_ref[...] += jnp.concatenate(rows, axis=0)

    return _pc(body, grid=(C // tc, T // tm),
               in_specs=[pl.BlockSpec((tm, tc), lambda j, i: (i, dob + j)),
                         pl.BlockSpec((tm, tc), lambda j, i: (i, uob + j)), _prev_spec(tm, tc, uob, "ji")],
               out_specs=pl.BlockSpec((8, tc), lambda j, i: (0, j)), out_shape=SDS((8, C), F32), name=name,
               scratch=(pltpu.VMEM((tm + 8, tc), F32),))(darr, uarr, uarr)


def _ffn_act_fwd(uu, w8, b, S, name):
    K, gw = FFN_CONV_K, GLU_W
    T, F2 = uu.shape
    tm, tc = _row_tile(S), 2 * GLU_W
    tps = S // tm

    def body(u_ref, p_ref, w_ref, b_ref, a_ref, buf):
        first = (pl.program_id(0) % tps) == 0
        uu_ = u_ref[...]
        sh = _shifted(buf, uu_, jnp.where(first, 0.0, p_ref[...]), tm)
        u = uu_ * w_ref[K - 1:K, :] + b_ref[...]
        for j in range(1, K):
            u = u + sh(j) * w_ref[K - 1 - j:K - j, :]
        a_ref[...] = (_silu(u[:, :gw]) * u[:, gw:]).astype(BF16)

    return _pc(body, grid=(T // tm, F2 // tc),
               in_specs=[pl.BlockSpec((tm, tc), lambda i, j: (i, j)), _prev_spec(tm, tc, 0, "ij"),
                         pl.BlockSpec((8, tc), lambda i, j: (0, j)), pl.BlockSpec((1, tc), lambda i, j: (0, j))],
               out_specs=pl.BlockSpec((tm, gw), lambda i, j: (i, j)), out_shape=SDS((T, F2 // 2), BF16), name=name,
               scratch=(pltpu.VMEM((tm + 8, tc), F32),))(uu, uu, w8, b)


def _ffn_act_bwd(uu, da, w8, b, S, name):
    K, gw = FFN_CONV_K, GLU_W
    T, F2 = uu.shape
    tm, tc = _row_tile(S), 2 * GLU_W
    tps = S // tm
    last16 = T // 16 - 1

    def body(u_ref, p_ref, n_ref, da_ref, dan_ref, w_ref, b_ref, duu_ref, cw_ref, ubuf, dbuf):
        i = pl.program_id(1)

        @pl.when(i == 0)
        def _():
            cw_ref[...] = jnp.zeros_like(cw_ref)

        first = (i % tps) == 0
        last = (i % tps) == tps - 1
        ubuf[0:8, :] = jnp.where(first, 0.0, p_ref[...])
        ubuf[8:tm + 8, :] = u_ref[...]
        ubuf[tm + 8:tm + 16, :] = n_ref[...]
        u = ubuf[pl.ds(8, tm + 8), :] * w_ref[K - 1:K, :] + b_ref[...]
        for j in range(1, K):
            u = u + ubuf[pl.ds(8 - j, tm + 8), :] * w_ref[K - 1 - j:K - j, :]
        da_n = jnp.where(last, 0.0, dan_ref[...].astype(F32)[0:8, :])
        da_ = jnp.concatenate([da_ref[...].astype(F32), da_n], axis=0)
        g, v = u[:, :gw], u[:, gw:]
        du = jnp.concatenate([da_ * v * _dsilu(g), da_ * _silu(g)], axis=1)
        dbuf[...] = du
        dmain = du[0:tm, :]
        duu = dmain * w_ref[K - 1:K, :]
        for j in range(1, K):
            duu = duu + dbuf[pl.ds(j, tm), :] * w_ref[K - 1 - j:K - j, :]
        duu_ref[...] = duu.astype(BF16)
        rows = [jnp.sum(dmain * ubuf[pl.ds(8 - (K - 1 - k), tm), :], axis=0, keepdims=True) for k in range(K)]
        rows.append(jnp.sum(dmain, axis=0, keepdims=True))
        rows.append(jnp.zeros((8 - K - 1, tc), F32))
        cw_ref[...] += jnp.concatenate(rows, axis=0)

    return _pc(body, grid=(F2 // tc, T // tm),
               in_specs=[pl.BlockSpec((tm, tc), lambda j, i: (i, j)), _prev_spec(tm, tc, 0, "ji"),
                         _next_spec(T, tm, tc, 0, "ji"), pl.BlockSpec((tm, gw), lambda j, i: (i, j)),
                         pl.BlockSpec((16, gw), lambda j, i: (jnp.minimum((i + 1) * (tm // 16), last16), j)),
                         pl.BlockSpec((8, tc), lambda j, i: (0, j)), pl.BlockSpec((1, tc), lambda j, i: (0, j))],
               out_specs=[pl.BlockSpec((tm, tc), lambda j, i: (i, j)), pl.BlockSpec((8, tc), lambda j, i: (0, j))],
               out_shape=[SDS((T, F2), BF16), SDS((8, F2), F32)], name=name,
               scratch=(pltpu.VMEM((tm + 16, tc), F32), pltpu.VMEM((tm + 8, tc), F32)))(uu, uu, uu, da, da, w8, b)


def _ssd_common(dtc_raw, dtr_raw, hpc, hpr, L):
    dt_c = _softplus(dtc_raw + hpc[0:1, :])
    a_c = -jnp.exp(hpc[1:2, :])
    dt_r = _softplus(dtr_raw + hpr[:, 0:1])
    a_r = -jnp.exp(hpr[:, 1:2])
    li = lax.broadcasted_iota(jnp.int32, (L, L), 0)
    si = lax.broadcasted_iota(jnp.int32, (L, L), 1)
    low = li >= si
    upp = li <= si
    acs_c = _dotx(low, dt_c * a_c, split="b")
    acs_r = _dotx(dt_r * a_r, upp)
    return dt_c, a_c, acs_c, acs_r, low, upp


def _dotx(a, b, split="a", parts=3, dims=NN):
    val, one = (a, b) if split == "a" else (b, a)
    one = one.astype(BF16)
    acc, rem = None, val
    for i in range(parts):
        piece = rem.astype(BF16)
        t = _dot(piece, one, dims) if split == "a" else _dot(one, piece, dims)
        acc = t if acc is None else acc + t
        if i + 1 < parts:
            rem = rem - piece.astype(F32)
    return acc


def _head_maps(R, P, L):
    RP = R * P
    sel = (lax.broadcasted_iota(jnp.int32, (RP, R), 0) // P == lax.broadcasted_iota(jnp.int32, (RP, R), 1)).astype(F32)
    selt = (lax.broadcasted_iota(jnp.int32, (R, RP), 1) // P == lax.broadcasted_iota(jnp.int32, (R, RP), 0)).astype(F32)
    colb = (lax.broadcasted_iota(jnp.int32, (R, R * L), 1) // L == lax.broadcasted_iota(jnp.int32, (R, R * L), 0)).astype(F32)
    return sel, selt, colb


def _pair_diag(mats, rhs_b, R, P):
    lanes = 2 * P
    lo = lax.broadcasted_iota(jnp.int32, (mats[0].shape[0], lanes), 1) < P
    out = []
    for q in range(R // 2):
        rp = rhs_b[:, q * lanes:(q + 1) * lanes]
        out.append(jnp.where(lo, _dot(mats[2 * q], rp), _dot(mats[2 * q + 1], rp)))
    return jnp.concatenate(out, axis=1) if len(out) > 1 else out[0]


def _ssd_specs(pre, off_x, off_b, off_c, G, R, P, nb, nc, rev):
    L, N, RP = CHUNK, N_STATE, R * P
    cidx = (lambda c: nc - 1 - c) if rev else (lambda c: c)
    xb, bb, cb = off_x // RP, off_b // N, off_c // N
    assert off_x % RP == 0 and off_b % N == 0 and off_c % N == 0
    row = lambda b, c: b * nc + cidx(c)
    return dict(
        x=pl.BlockSpec((L, RP), lambda g, b, c: (row(b, c), xb + g)),
        b=pl.BlockSpec((L, N), lambda g, b, c: (row(b, c), bb + g)),
        c=pl.BlockSpec((L, N), lambda g, b, c: (row(b, c), cb + g)),
        dtc=pl.BlockSpec((None, L, R), lambda g, b, c: (g, row(b, c), 0)),
        dtr=pl.BlockSpec((None, R, L), lambda g, b, c: (g, 0, row(b, c))),
        hpc=pl.BlockSpec((None, 8, R), lambda g, b, c: (g, 0, 0)),
        hpr=pl.BlockSpec((None, R, 8), lambda g, b, c: (g, 0, 0)),
        y=pl.BlockSpec((L, RP), lambda g, b, c: (row(b, c), g)),
        bc=pl.BlockSpec((L, N), lambda g, b, c: (row(b, c), g)),
        hs=pl.BlockSpec((None, None, N, RP), lambda g, b, c: (row(b, c), g, 0, 0)),
    )


def _ssd_fwd(pre, offs, dtc, dtr, hpc, hpr, G, R, P, S, name):
    T = pre.shape[0]
    L, N, RP = CHUNK, N_STATE, R * P
    nc, nb = S // L, T // S
    sp = _ssd_specs(pre, *offs, G, R, P, nb, nc, False)

    def body(px_ref, pb_ref, pc_ref, dtc_ref, dtr_ref, hpc_ref, hpr_ref, y_ref, hs_ref, hst):
        @pl.when(pl.program_id(2) == 0)
        def _():
            hst[...] = jnp.zeros_like(hst)

        xs, bm, cm = _silu(px_ref[...]), _silu(pb_ref[...]), _silu(pc_ref[...])
        hpc_ = hpc_ref[...]
        dt_c, _, acs_c, acs_r, low, _ = _ssd_common(dtc_ref[...], dtr_ref[...], hpc_, hpr_ref[...], L)
        _, selt, colb = _head_maps(R, P, L)
        dt_e, a_e, hp_e = _dotx(dt_c, selt), _dotx(acs_c, selt), _dotx(hpc_, selt)
        a_bc = _dotx(acs_c, colb)
        a_last = a_e[L - 1:L, :]
        bb, cb = bm.astype(BF16), cm.astype(BF16)
        gm = _dot(cb, bb, NT)
        hprev = hst[...]
        hprev_b = hprev.astype(BF16)
        hs_ref[...] = hprev_b
        xdt = xs * dt_e
        xdt_b = xdt.astype(BF16)
        ms = []
        for r in range(R):
            dec = jnp.exp(jnp.where(low, a_bc[:, r * L:(r + 1) * L] - acs_r[r:r + 1, :], -jnp.inf))
            ms.append((gm * dec).astype(BF16))
        y = _pair_diag(ms, xdt_b, R, P) + _dot(cb, hprev_b) * jnp.exp(a_e) + hp_e[2:3, :] * xs
        y_ref[...] = y
        xw = (xdt * jnp.exp(a_last - a_e)).astype(BF16)
        hst[...] = hprev * jnp.exp(a_last) + _dot(bb, xw, TN)

    return _pc(body, grid=(G, nb, nc),
               in_specs=[sp["x"], sp["b"], sp["c"], sp["dtc"], sp["dtr"], sp["hpc"], sp["hpr"]],
               out_specs=[sp["y"], sp["hs"]],
               out_shape=[SDS((T, G * RP), F32), SDS((nb * nc, G, N, RP), BF16)], name=name,
               scratch=(pltpu.VMEM((N, RP), F32),))(pre, pre, pre, dtc, dtr, hpc, hpr)


def _ssd_bwd(pre, offs, dtc, dtr, hpc, hpr, hs, dy, G, R, P, S, name):
    T = pre.shape[0]
    L, N, RP = CHUNK, N_STATE, R * P
    nc, nb = S // L, T // S
    sp = _ssd_specs(pre, *offs, G, R, P, nb, nc, True)

    def body(px_ref, pb_ref, pc_ref, dtc_ref, dtr_ref, hpc_ref, hpr_ref, hs_ref, dy_ref,
             dpx_ref, dpb_ref, dpc_ref, ddt_ref, hpg_ref, dhst):
        bi, ci = pl.program_id(1), pl.program_id(2)

        @pl.when(ci == 0)
        def _():
            dhst[...] = jnp.zeros_like(dhst)

        @pl.when((bi == 0) & (ci == 0))
        def _():
            hpg_ref[...] = jnp.zeros_like(hpg_ref)

        px, pb, pcc = px_ref[...], pb_ref[...], pc_ref[...]
        xs, bm, cm = _silu(px), _silu(pb), _silu(pcc)
        hpc_ = hpc_ref[...]
        dtc_raw = dtc_ref[...]
        dt_c, a_c, acs_c, acs_r, low, upp = _ssd_common(dtc_raw, dtr_ref[...], hpc_, hpr_ref[...], L)
        sel, selt, colb = _head_maps(R, P, L)
        dt_e, a_e, hp_e = _dotx(dt_c, selt), _dotx(acs_c, selt), _dotx(hpc_, selt)
        a_bc = _dotx(acs_c, colb)
        a_last = a_e[L - 1:L, :]
        e_e, w_e = jnp.exp(a_e), jnp.exp(a_last - a_e)
        bb, cb = bm.astype(BF16), cm.astype(BF16)
        gm = _dot(cb, bb, NT)
        gmt = _dot(bb, cb, NT)
        hprev = hs_ref[...]
        dhn = dhst[...]
        dhn_b = dhn.astype(BF16)
        dy = dy_ref[...]
        dy_b = dy.astype(BF16)
        xdt = xs * dt_e
        xdt_b = xdt.astype(BF16)
        yoff = _dot(cb, hprev) * e_e
        dye_b = (dy * e_e).astype(BF16)
        dcm = _dot(dye_b, hprev, NT)
        dhst[...] = _dot(cb, dye_b, TN) + jnp.exp(a_last) * dhn
        dxdt_st = _dot(bb, dhn_b) * w_e
        dbm = _dot((xdt * w_e).astype(BF16), dhn_b, NT)
        lanes = 2 * P
        lo = lax.broadcasted_iota(jnp.int32, (L, lanes), 1) < P
        dg = jnp.zeros((L, L), F32)
        es, css = [], []
        for r in range(R):
            col_b, row = a_bc[:, r * L:(r + 1) * L], acs_r[r:r + 1, :]
            dec = jnp.exp(jnp.where(low, col_b - row, -jnp.inf))
            q = r // 2
            dyp = dy_b[:, q * lanes:(q + 1) * lanes]
            dyp = jnp.where(lo if r % 2 == 0 else ~lo, dyp, jnp.zeros_like(dyp))
            dm = _dot(dyp, xdt_b[:, q * lanes:(q + 1) * lanes], NT)
            dg = dg + dm * dec
            e = dm * (gm * dec)
            es.append(e)
            css.append(jnp.sum(e, axis=0, keepdims=True))
        dgb = dg.astype(BF16)
        dcm = dcm + _dot(dgb, bb)
        dbm = dbm + _dot(dgb, cb, TN)
        colbt = (lax.broadcasted_iota(jnp.int32, (R * L, R), 0) // L
                 == lax.broadcasted_iota(jnp.int32, (R * L, R), 1)).astype(F32)
        eye = (lax.broadcasted_iota(jnp.int32, (R, R), 0) == lax.broadcasted_iota(jnp.int32, (R, R), 1)).astype(F32)
        row_sums = _dotx(jnp.concatenate(es, axis=1), colbt)
        col_sums = _dotx(jnp.concatenate(css, axis=0), eye, dims=TN)
        mts = []
        for r in range(R):
            dect = jnp.exp(jnp.where(upp, acs_r[r:r + 1, :] - a_bc[:, r * L:(r + 1) * L], -jnp.inf))
            mts.append((gmt * dect).astype(BF16))
        dxdt = _pair_diag(mts, dy_b, R, P) + dxdt_st
        q_st = _dotx(xdt * dxdt_st, sel, parts=2)
        da = row_sums - col_sums + _dotx(dy * yoff, sel, parts=2) - q_st
        hh = jnp.sum(_dotx(dhn * hprev.astype(F32), sel, parts=2), axis=0, keepdims=True)
        da_last = jnp.exp(acs_c[L - 1:L, :]) * hh + jnp.sum(q_st, axis=0, keepdims=True)
        rowi = lax.broadcasted_iota(jnp.int32, (L, R), 0)
        da = da + jnp.where(rowi == L - 1, da_last, 0.0)
        dpx_ref[...] = (dxdt * dt_e + hp_e[2:3, :] * dy) * _dsilu(px)
        dpb_ref[...] = dbm * _dsilu(pb)
        dpc_ref[...] = dcm * _dsilu(pcc)
        dadt = _dotx(upp, da, split="b")
        ddt = _dotx(dxdt * xs, sel, parts=2) + dadt * a_c
        ddt_raw = ddt * jax.nn.sigmoid(dtc_raw + hpc_[0:1, :])
        ddt_ref[...] = ddt_raw
        d_a = jnp.sum(dadt * dt_c, axis=0, keepdims=True)
        d_d = jnp.sum(_dotx(dy * xs, sel, parts=2), axis=0, keepdims=True)
        rows = [jnp.sum(ddt_raw, axis=0, keepdims=True), d_a * a_c, d_d, jnp.zeros((5, R), F32)]
        hpg_ref[...] += jnp.concatenate(rows, axis=0)

    return _pc(body, grid=(G, nb, nc),
               in_specs=[sp["x"], sp["b"], sp["c"], sp["dtc"], sp["dtr"], sp["hpc"], sp["hpr"], sp["hs"], sp["y"]],
               out_specs=[sp["y"], sp["bc"], sp["bc"], sp["dtc"], pl.BlockSpec((None, 8, R), lambda g, b, c: (g, 0, 0))],
               out_shape=[SDS((T, G * RP), F32), SDS((T, G * N), F32), SDS((T, G * N), F32), SDS((G, T, R), F32),
                          SDS((G, 8, R), F32)], name=name,
               scratch=(pltpu.VMEM((N, RP), F32),))(pre, pre, pre, dtc, dtr, hpc, hpr, hs, dy)


def _gate_norm_fwd(y, zview, ng, G, S, name):
    T, DI = y.shape
    zarr, zoff = zview
    gw = DI // G
    tm = _row_tile(S)
    zb = zoff // gw
    assert zoff % gw == 0

    def body(y_ref, z_ref, g_ref, o_ref):
        yg = y_ref[...] * _silu(z_ref[...])
        r = lax.rsqrt(jnp.mean(yg * yg, axis=-1, keepdims=True) + EPS)
        o_ref[...] = (yg * r * g_ref[...]).astype(BF16)

    return _pc(body, grid=(T // tm, G),
               in_specs=[pl.BlockSpec((tm, gw), lambda i, g: (i, g)), pl.BlockSpec((tm, gw), lambda i, g: (i, zb + g)),
                         pl.BlockSpec((1, gw), lambda i, g: (0, g))],
               out_specs=pl.BlockSpec((tm, gw), lambda i, g: (i, g)), out_shape=SDS((T, DI), BF16), name=name)(y, zarr, ng)


def _gate_norm_bwd(y, zview, ng, dyn, G, S, name):
    T, DI = y.shape
    zarr, zoff = zview
    gw = DI // G
    tm = _row_tile(S)
    zb = zoff // gw

    def body(y_ref, z_ref, g_ref, d_ref, dy_ref, dz_ref, dg_ref):
        @pl.when(pl.program_id(1) == 0)
        def _():
            dg_ref[...] = jnp.zeros_like(dg_ref)

        y_, z, d = y_ref[...], z_ref[...], d_ref[...]
        sz = _silu(z)
        yg = y_ * sz
        r = lax.rsqrt(jnp.mean(yg * yg, axis=-1, keepdims=True) + EPS)
        n = yg * r
        dn = d * g_ref[...]
        dyg = r * (dn - n * jnp.mean(dn * n, axis=-1, keepdims=True))
        dy_ref[...] = dyg * sz
        dz_ref[...] = (dyg * y_ * _dsilu(z)).astype(BF16)
        dg_ref[...] += _bsum(jnp.sum(d * n, axis=0, keepdims=True))

    return _pc(body, grid=(G, T // tm),
               in_specs=[pl.BlockSpec((tm, gw), lambda g, i: (i, g)), pl.BlockSpec((tm, gw), lambda g, i: (i, zb + g)),
                         pl.BlockSpec((1, gw), lambda g, i: (0, g)), pl.BlockSpec((tm, gw), lambda g, i: (i, g))],
               out_specs=[pl.BlockSpec((tm, gw), lambda g, i: (i, g)), pl.BlockSpec((tm, gw), lambda g, i: (i, g)),
                          pl.BlockSpec((8, gw), lambda g, i: (0, g))],
               out_shape=[SDS((T, DI), F32), SDS((T, DI), BF16), SDS((8, DI), F32)], name=name)(y, zarr, ng, dyn)


def _shortconv_fwd(proj, off_b, off_c, off_h, C, w8, S, name):
    K = SC_CONV_K
    _, ob, T, tm, tc, tps = _conv_geom((proj, off_b), C, S)
    oc, oh = off_c // tc, off_h // tc

    def body(b_ref, c_ref, h_ref, cp_ref, hp_ref, w_ref, o_ref, buf):
        first = (pl.program_id(0) % tps) == 0
        v = c_ref[...] * h_ref[...]
        sh = _shifted(buf, v, jnp.where(first, 0.0, cp_ref[...] * hp_ref[...]), tm)
        acc = v * w_ref[K - 1:K, :]
        for j in range(1, K):
            acc = acc + sh(j) * w_ref[K - 1 - j:K - j, :]
        o_ref[...] = (b_ref[...] * acc).astype(BF16)

    blk = lambda o: pl.BlockSpec((tm, tc), lambda i, j: (i, o + j))
    return _pc(body, grid=(T // tm, C // tc),
               in_specs=[blk(ob), blk(oc), blk(oh), _prev_spec(tm, tc, oc, "ij"), _prev_spec(tm, tc, oh, "ij"),
                         pl.BlockSpec((8, tc), lambda i, j: (0, j))],
               out_specs=pl.BlockSpec((tm, tc), lambda i, j: (i, j)), out_shape=SDS((T, C), BF16), name=name,
               scratch=(pltpu.VMEM((tm + 8, tc), F32),))(proj, proj, proj, proj, proj, w8)


def _shortconv_bwd(proj, off_b, off_c, off_h, C, w8, dsc, S, name):
    K = SC_CONV_K
    _, ob, T, tm, tc, tps = _conv_geom((proj, off_b), C, S)
    oc, oh = off_c // tc, off_h // tc

    def body(b_ref, c_ref, h_ref, cp_ref, hp_ref, bn_ref, d_ref, dn_ref, w_ref,
             db_ref, dc_ref, dh_ref, dw_ref, buf, buf2):
        i = pl.program_id(1)

        @pl.when(i == 0)
        def _():
            dw_ref[...] = jnp.zeros_like(dw_ref)

        first = (i % tps) == 0
        last = (i % tps) == tps - 1
        b_, c_, h_, d = b_ref[...], c_ref[...], h_ref[...], d_ref[...]
        v = c_ * h_
        sh = _shifted(buf, v, jnp.where(first, 0.0, cp_ref[...] * hp_ref[...]), tm)
        vs = [v] + [sh(j) for j in range(1, K)]
        conv = vs[0] * w_ref[K - 1:K, :]
        for j in range(1, K):
            conv = conv + vs[j] * w_ref[K - 1 - j:K - j, :]
        db_ref[...] = (d * conv).astype(BF16)
        dconv = d * b_
        buf2[0:tm, :] = dconv
        buf2[tm:tm + 8, :] = jnp.where(last, 0.0, dn_ref[...] * bn_ref[...])
        dv = dconv * w_ref[K - 1:K, :]
        for j in range(1, K):
            dv = dv + buf2[pl.ds(j, tm), :] * w_ref[K - 1 - j:K - j, :]
        dc_ref[...] = (dv * h_).astype(BF16)
        dh_ref[...] = (dv * c_).astype(BF16)
        rows = [jnp.sum(dconv * vs[K - 1 - k], axis=0, keepdims=True) for k in range(K)]
        rows.append(jnp.zeros((8 - K, tc), F32))
        dw_ref[...] += jnp.concatenate(rows, axis=0)

    blk = lambda o: pl.BlockSpec((tm, tc), lambda j, i: (i, o + j))
    out = pl.BlockSpec((tm, tc), lambda j, i: (i, j))
    return _pc(body, grid=(C // tc, T // tm),
               in_specs=[blk(ob), blk(oc), blk(oh), _prev_spec(tm, tc, oc, "ji"), _prev_spec(tm, tc, oh, "ji"),
                         _next_spec(T, tm, tc, ob, "ji"), blk(0), _next_spec(T, tm, tc, 0, "ji"),
                         pl.BlockSpec((8, tc), lambda j, i: (0, j))],
               out_specs=[out, out, out, pl.BlockSpec((8, tc), lambda j, i: (0, j))],
               out_shape=[SDS((T, C), BF16)] * 3 + [SDS((8, C), F32)], name=name,
               scratch=(pltpu.VMEM((tm + 8, tc), F32), pltpu.VMEM((tm + 8, tc), F32)))(
                   proj, proj, proj, proj, proj, proj, dsc, dsc, w8)


def _merge_fwd(proj, off_g1, off_g2, y1, y2, S, name):
    T, D = y1.shape
    tm = _row_tile(S)
    o1, o2 = off_g1 // D, off_g2 // D
    assert off_g1 % D == 0 and off_g2 % D == 0

    def body(g1_ref, g2_ref, y1_ref, y2_ref, o_ref):
        o_ref[...] = (jax.nn.sigmoid(g1_ref[...]) * y1_ref[...] + jax.nn.sigmoid(g2_ref[...]) * y2_ref[...]).astype(BF16)

    row = pl.BlockSpec((tm, D), lambda i: (i, 0))
    return _pc(body, grid=(T // tm,),
               in_specs=[pl.BlockSpec((tm, D), lambda i: (i, o1)), pl.BlockSpec((tm, D), lambda i: (i, o2)), row, row],
               out_specs=row, out_shape=SDS((T, D), BF16), name=name)(proj, proj, y1, y2)


def _merge_bwd(proj, off_g1, off_g2, y1, y2, dm, S, name):
    T, D = y1.shape
    tm = _row_tile(S)
    o1, o2 = off_g1 // D, off_g2 // D

    def body(g1_ref, g2_ref, y1_ref, y2_ref, d_ref, dy1_ref, dy2_ref, dg1_ref, dg2_ref):
        d = d_ref[...]
        s1, s2 = jax.nn.sigmoid(g1_ref[...]), jax.nn.sigmoid(g2_ref[...])
        dy1_ref[...] = (d * s1).astype(BF16)
        dy2_ref[...] = (d * s2).astype(BF16)
        dg1_ref[...] = (d * y1_ref[...] * s1 * (1.0 - s1)).astype(BF16)
        dg2_ref[...] = (d * y2_ref[...] * s2 * (1.0 - s2)).astype(BF16)

    row = pl.BlockSpec((tm, D), lambda i: (i, 0))
    return _pc(body, grid=(T // tm,),
               in_specs=[pl.BlockSpec((tm, D), lambda i: (i, o1)), pl.BlockSpec((tm, D), lambda i: (i, o2)), row, row, row],
               out_specs=[row] * 4, out_shape=[SDS((T, D), BF16)] * 4, name=name)(proj, proj, y1, y2, dm)


def _pad8(w):
    return jnp.pad(w, ((0, 8 - w.shape[0]), (0, 0)))


def _dims(w):
    D = w["mix_pre_g"].shape[-1]
    DI = w["ssd_norm_g"].shape[-1]
    H = w["ssd_dt_bias"].shape[-1]
    conv_dim = w["ssd_conv_b"].shape[-1]
    G = (conv_dim - DI) // (2 * N_STATE)
    F = w["w_down"].shape[0]
    return dict(D=D, DI=DI, H=H, P=DI // H, G=G, R=H // G, GN=G * N_STATE, CD=conv_dim, F=F)


def _proj_layout(d):
    D, DI, CD, H = d["D"], d["DI"], d["CD"], d["H"]
    o = dict(z=0, xbc=DI, scb=DI + CD, scc=DI + CD + D, sch=DI + CD + 2 * D, g1=DI + CD + 3 * D, g2=DI + CD + 4 * D,
             dt=DI + CD + 5 * D)
    o["sb"] = math.gcd(SEG_BLK, D, DI, d["GN"])
    assert o["sb"] % 128 == 0 and H <= o["sb"]
    o["np"] = o["dt"] + o["sb"]
    return o


def _glu_perm(a, F, inverse=False):
    lead = a.shape[:-1]
    nb = F // GLU_W
    if not inverse:
        return a.reshape(*lead, 2, nb, GLU_W).swapaxes(-3, -2).reshape(*lead, 2 * F)
    return a.reshape(*lead, nb, 2, GLU_W).swapaxes(-3, -2).reshape(*lead, 2 * F)


def _glu_perm_rows(a, F, inverse=False):
    nb, D = F // GLU_W, a.shape[1]
    shape = (nb, 2, GLU_W, D) if inverse else (2, nb, GLU_W, D)
    return a.reshape(shape).swapaxes(0, 1).reshape(2 * F, D)


def _prep_layer(w):
    d = _dims(w)
    D, DI, CD, H, G, R, F = d["D"], d["DI"], d["CD"], d["H"], d["G"], d["R"], d["F"]
    lay = _proj_layout(d)
    w_in = w["w_in"]
    used = lay["dt"] + H
    wcat = jnp.concatenate([w_in[:DI + CD], w_in[DI + CD + H:], w_in[DI + CD:DI + CD + H],
                            jnp.zeros((lay["np"] - used, D), w_in.dtype)], axis=0)
    hp = jnp.stack([w["ssd_dt_bias"], w["ssd_a_log"], w["ssd_d"]], 0).astype(F32)
    hpc = jnp.pad(hp.reshape(3, G, R).transpose(1, 0, 2), ((0, 0), (0, 5), (0, 0)))
    hpr = jnp.pad(hp[:2].reshape(2, G, R).transpose(1, 2, 0), ((0, 0), (0, 0), (0, 6)))
    row = lambda v: v.reshape(1, -1).astype(F32)
    return dict(
        d=d, lay=lay, ada_w=w["ada_w"].astype(BF16), ada_b=row(w["ada_b"]),
        mix_pre_g=row(w["mix_pre_g"]), mix_post_g=row(w["mix_post_g"]), wcat=wcat.astype(BF16),
        ssd_conv_w=_pad8(w["ssd_conv_w"].astype(F32)), ssd_conv_b=row(w["ssd_conv_b"]), hpc=hpc, hpr=hpr,
        ssd_norm_g=row(w["ssd_norm_g"]), w_ssd_out=w["w_ssd_out"].astype(BF16),
        sc_conv_w=_pad8(w["sc_conv_w"].astype(F32)), w_sc_out=w["w_sc_out"].astype(BF16), w_o=w["w_o"].astype(BF16),
        ffn_pre_g=row(w["ffn_pre_g"]), ffn_post_g=row(w["ffn_post_g"]),
        w_up=_glu_perm_rows(w["w_up"], F).astype(BF16), ffn_conv_w=_pad8(_glu_perm(w["ffn_conv_w"].astype(F32), F)),
        ffn_conv_b=_glu_perm(row(w["ffn_conv_b"]), F), w_down=w["w_down"].astype(BF16))


def _dt_layouts(proj, lay, d):
    T = proj.shape[0]
    dt = proj[:, lay["dt"]:lay["dt"] + d["H"]].reshape(T, d["G"], d["R"])
    return dt.transpose(1, 0, 2), dt.transpose(1, 2, 0)


def _layer_fwd(x, mod3, p, S, li):
    d, lay = p["d"], p["lay"]
    D, DI, G, R, P, GN, CD = d["D"], d["DI"], d["G"], d["R"], d["P"], d["GN"], d["CD"]
    nm = lambda s: f"l{li}_{s}"
    h = _norm_mod(x, p["mix_pre_g"], mod3, 1, 0, S, nm("norm1"))
    proj = _mm(h, p["wcat"], "nt", F32, nm("mm_in"), caps=(1024, 1536, 2048))
    pre = _conv_fwd((proj, lay["xbc"]), CD, p["ssd_conv_w"], p["ssd_conv_b"], SSD_CONV_K, S, nm("ssdconv"))
    dtc, dtr = _dt_layouts(proj, lay, d)
    offs = (0, DI, DI + GN)
    y, hs = _ssd_fwd(pre, offs, dtc, dtr, p["hpc"], p["hpr"], G, R, P, S, nm("ssd"))
    yn = _gate_norm_fwd(y, (proj, lay["z"]), p["ssd_norm_g"], G, S, nm("gnorm"))
    sc = _shortconv_fwd(proj, lay["scb"], lay["scc"], lay["sch"], D, p["sc_conv_w"], S, nm("sconv"))
    y_ssd = _mm(yn, p["w_ssd_out"], "nn", F32, nm("mm_ssdout"))
    y_sc = _mm(sc, p["w_sc_out"], "nn", F32, nm("mm_scout"))
    m = _merge_fwd(proj, lay["g1"], lay["g2"], y_ssd, y_sc, S, nm("merge"))
    mix = _mm(m, p["w_o"], "nn", F32, nm("mm_o"))
    x1 = _resid_post(x, mix, mod3, 2, p["mix_post_g"], S, nm("post1"))
    h2 = _norm_mod(x1, p["ffn_pre_g"], mod3, 4, 3, S, nm("norm2"))
    uu = _mm(h2, p["w_up"], "nt", F32, nm("mm_up"), caps=(1024, 1408, 2048))
    a = _ffn_act_fwd(uu, p["ffn_conv_w"], p["ffn_conv_b"], S, nm("ffnact"))
    f = _mm(a, p["w_down"], "nn", F32, nm("mm_down"), caps=(1024, 1024, 1408))
    x2 = _resid_post(x1, f, mod3, 5, p["ffn_post_g"], S, nm("post2"))
    saved = dict(x=x, h=h, proj=proj, pre=pre, dtc=dtc, dtr=dtr, y=y, hs=hs, yn=yn, sc=sc, y_ssd=y_ssd, y_sc=y_sc,
                 m=m, mix=mix, x1=x1, h2=h2, uu=uu, a=a, f=f)
    return x2, saved


def _seq_sum(acc, nb):
    return acc.reshape(nb, 8, -1)[:, 0, :]


def _layer_bwd(dx2, mod3, p, s, S, li):
    d, lay = p["d"], p["lay"]
    D, DI, G, R, P, GN, CD, H, F = d["D"], d["DI"], d["G"], d["R"], d["P"], d["GN"], d["CD"], d["H"], d["F"]
    nb = dx2.shape[0] // S
    nm = lambda t: f"l{li}_{t}"
    g = {}
    df, dgt2, dpg2 = _post_bwd(s["f"], mod3, 5, p["ffn_post_g"], dx2, S, nm("post2_b"))
    g["ffn_post_g"] = dpg2[0]
    da = _mm(df, p["w_down"], "nt", BF16, nm("mm_down_bi"), caps=(1024, 1408, 2048))
    g["w_down"] = _mm(s["a"], df, "tn", WGRAD,nm("mm_down_bw"), caps=(1408, 1024, 1024))
    duu, cw = _ffn_act_bwd(s["uu"], da, p["ffn_conv_w"], p["ffn_conv_b"], S, nm("ffnact_b"))
    g["ffn_conv_w"] = _glu_perm(cw[:FFN_CONV_K], F, inverse=True)
    g["ffn_conv_b"] = _glu_perm(cw[FFN_CONV_K], F, inverse=True)
    dh2 = _mm(duu, p["w_up"], "nn", F32, nm("mm_up_bi"), caps=(1024, 1024, 2816))
    g["w_up"] = _glu_perm_rows(_mm(duu, s["h2"], "tn", WGRAD,nm("mm_up_bw"), caps=(1408, 1024, 1024)), F, inverse=True)
    dx1, dg2, dsc2, dsh2 = _pre_bwd(s["x1"], p["ffn_pre_g"], mod3, 4, dh2, dx2, S, nm("norm2_b"))
    g["ffn_pre_g"] = dg2[0]
    dmix, dgt1, dpg1 = _post_bwd(s["mix"], mod3, 2, p["mix_post_g"], dx1, S, nm("post1_b"))
    g["mix_post_g"] = dpg1[0]
    dm = _mm(dmix, p["w_o"], "nt", F32, nm("mm_o_bi"))
    g["w_o"] = _mm(s["m"], dmix, "tn", WGRAD,nm("mm_o_bw"))
    proj = s["proj"]
    dy_ssd, dy_sc, dg1, dg2_ = _merge_bwd(proj, lay["g1"], lay["g2"], s["y_ssd"], s["y_sc"], dm, S, nm("merge_b"))
    dyn = _mm(dy_ssd, p["w_ssd_out"], "nt", F32, nm("mm_ssdout_bi"))
    g["w_ssd_out"] = _mm(s["yn"], dy_ssd, "tn", WGRAD,nm("mm_ssdout_bw"))
    dsc = _mm(dy_sc, p["w_sc_out"], "nt", F32, nm("mm_scout_bi"))
    g["w_sc_out"] = _mm(s["sc"], dy_sc, "tn", WGRAD,nm("mm_scout_bw"))
    dscb, dscc, dsch, scw = _shortconv_bwd(proj, lay["scb"], lay["scc"], lay["sch"], D, p["sc_conv_w"], dsc, S, nm("sconv_b"))
    g["sc_conv_w"] = scw[:SC_CONV_K]
    dy, dz, dng = _gate_norm_bwd(s["y"], (proj, lay["z"]), p["ssd_norm_g"], dyn, G, S, nm("gnorm_b"))
    g["ssd_norm_g"] = dng[0]
    offs = (0, DI, DI + GN)
    dpx, dpb, dpc, ddt, hpg = _ssd_bwd(s["pre"], offs, s["dtc"], s["dtr"], p["hpc"], p["hpr"], s["hs"], dy,
                                       G, R, P, S, nm("ssd_b"))
    g["ssd_dt_bias"], g["ssd_a_log"], g["ssd_d"] = hpg[:, 0, :].reshape(H), hpg[:, 1, :].reshape(H), hpg[:, 2, :].reshape(H)
    cws, dxbc = [], []
    for name, darr, off, C in (("x", dpx, 0, DI), ("b", dpb, DI, GN), ("c", dpc, DI + GN, GN)):
        w8 = p["ssd_conv_w"][:, off:off + C]
        cws.append(_conv_bwd_w((darr, 0), (proj, lay["xbc"] + off), C, SSD_CONV_K, S, nm(f"ssdconv_bw_{name}")))
        dxbc.append(_conv_bwd_in((darr, 0), C, w8, SSD_CONV_K, S, BF16, nm(f"ssdconv_bi_{name}")))
    cws = jnp.concatenate(cws, axis=1)
    g["ssd_conv_w"], g["ssd_conv_b"] = cws[:SSD_CONV_K], cws[SSD_CONV_K]
    T = dx2.shape[0]
    ddt_t = jnp.pad(ddt.transpose(1, 0, 2).reshape(T, H).astype(BF16), ((0, 0), (0, lay["sb"] - H)))
    dproj = [dz] + dxbc + [dscb, dscc, dsch, dg1, dg2_, ddt_t]
    dh = _mm_seg(dproj, p["wcat"], "nn", F32, nm("mm_in_bi"), lay["sb"])
    dwcat = _mm_seg(dproj, s["h"], "tn", WGRAD, nm("mm_in_bw"), lay["sb"], tk=1024)
    o = lay
    g["w_in"] = jnp.concatenate([dwcat[o["z"]:o["scb"]], dwcat[o["dt"]:o["dt"] + H], dwcat[o["scb"]:o["dt"]]], axis=0)
    dx, dg1_, dsc1, dsh1 = _pre_bwd(s["x"], p["mix_pre_g"], mod3, 1, dh, dx1, S, nm("norm1_b"))
    g["mix_pre_g"] = dg1_[0]
    dmod = jnp.concatenate([_seq_sum(t, nb) for t in (dsh1, dsc1, dgt1, dsh2, dsc2, dgt2)], axis=1)
    return dx, dmod, g


def _fwd_bwd(x3, c, target3, layers):
    nb, S, D = x3.shape
    T = nb * S
    x = x3.reshape(T, D)
    c8 = jnp.pad(c, ((0, MOD_ROWS - nb), (0, 0)))
    preps = [_prep_layer(w) for w in layers]
    saved, mods, cact = [], [], None
    for li, p in enumerate(preps):
        mod, cact = _modk(c8, p["ada_w"], p["ada_b"], f"l{li}_mod")
        mod3 = mod[:nb].reshape(nb, 1, 6 * D)
        x, s = _layer_fwd(x, mod3, p, S, li)
        saved.append(s)
        mods.append(mod3)
    dy, lacc = _loss(x, target3.reshape(T, D), S, "loss")
    grads = [None] * len(preps)
    for li in reversed(range(len(preps))):
        dy, dmod, g = _layer_bwd(dy, mods[li], preps[li], saved[li], S, li)
        dmod8 = jnp.pad(dmod, ((0, MOD_ROWS - nb), (0, 0)))
        g["ada_b"] = _colsum(dmod8, f"l{li}_adab")
        g["ada_w"] = _mm(dmod8, cact, "tn", WGRAD,f"l{li}_mm_ada_bw", caps=(1536, 1024, 2048))
        grads[li] = g
    return lacc[0, 0], dy.reshape(nb, S, D), grads


def _colsum(a8, name):
    rows, C = a8.shape
    tc = _tile(C, 2048)

    def body(a_ref, o_ref):
        o_ref[...] = _bsum(jnp.sum(a_ref[...], axis=0, keepdims=True))

    return _pc(body, grid=(C // tc,), in_specs=[pl.BlockSpec((rows, tc), lambda j: (0, j))],
               out_specs=pl.BlockSpec((8, tc), lambda j: (0, j)), out_shape=SDS((8, C), F32), name=name)(a8)[0]


def _adam(gs, w, m, v, name):
    ns, R, W = gs.shape
    tr = _tile(R, 256, 8)

    def body(g_ref, w_ref, m_ref, v_ref, go_ref, d_ref, mo_ref, vo_ref):
        g = g_ref[0].astype(F32)
        for k in range(1, ns):
            g = g + g_ref[k].astype(F32)
        go_ref[...] = g
        d_ref[...], mo_ref[...], vo_ref[...] = _adam_update(g, w_ref[...], m_ref[...], v_ref[...])

    row = pl.BlockSpec((tr, W), lambda i: (i, 0))
    return _pc(body, grid=(R // tr,), in_specs=[pl.BlockSpec((ns, tr, W), lambda i: (0, i, 0)), row, row, row],
               out_specs=[row] * 4, out_shape=[SDS((R, W), F32)] * 4, name=name)(gs, w, m, v)


def _adam_update(g, w, m, v):
    c1 = 1.0 / (1.0 - ADAM_B1 ** ADAM_STEP)
    c2 = 1.0 / (1.0 - ADAM_B2 ** ADAM_STEP)
    m_ = ADAM_B1 * m + (1.0 - ADAM_B1) * g
    v_ = ADAM_B2 * v + (1.0 - ADAM_B2) * (g * g)
    return -ADAM_LR * ((m_ * c1) / (jnp.sqrt(v_ * c2) + ADAM_EPS) + ADAM_WD * w), m_, v_


def _adam_nat(g, w, m, v, name):
    depth, a, b = w.shape
    tr = _tile(a, 256, 8)

    def body(g_ref, w_ref, m_ref, v_ref, d_ref, mo_ref, vo_ref):
        d_ref[...], mo_ref[...], vo_ref[...] = _adam_update(g_ref[...], w_ref[...], m_ref[...], v_ref[...])

    blk = pl.BlockSpec((None, tr, b), lambda l, i: (l, i, 0))
    return _pc(body, grid=(depth, a // tr), in_specs=[blk] * 4, out_specs=[blk] * 3,
               out_shape=[SDS(w.shape, F32)] * 3, name=name)(g, w, m, v)


def _sum_chips(gs, name):
    ns, R, W = gs.shape
    tr = _tile(R, 256, 16)

    def body(g_ref, o_ref):
        acc = g_ref[0].astype(F32)
        for k in range(1, ns):
            acc = acc + g_ref[k].astype(F32)
        o_ref[...] = acc

    return _pc(body, grid=(R // tr,), in_specs=[pl.BlockSpec((ns, tr, W), lambda i: (0, i, 0))],
               out_specs=pl.BlockSpec((tr, W), lambda i: (i, 0)), out_shape=SDS((R, W), F32), name=name)(gs)


HBM_SPEC = pl.BlockSpec(memory_space=pltpu.HBM)
VMEM_SPEC = pl.BlockSpec(memory_space=pltpu.VMEM)


def _dev():
    return lax.axis_index("x"), lax.axis_index("y"), lax.axis_index("c")


def _allgather_big(loc, name):
    R, W = loc.shape

    def body(x_ref, out_ref, send_sems, recv_sems, local_sem):
        x, y, c = _dev()
        me, sibling = (x, y, c), (x, y, 1 - c)
        chips = [(1 - x, y), (x, 1 - y), (1 - x, 1 - y)]

        def slab(px, py, pc):
            return out_ref.at[4 * px + 2 * py + pc]

        def copy(k, block, to, src=None):
            return pltpu.make_async_remote_copy(
                src_ref=slab(*block) if src is None else src, dst_ref=slab(*block),
                send_sem=send_sems.at[k], recv_sem=recv_sems.at[k], device_id=to, device_id_type=MESH)

        mine = pltpu.make_async_copy(x_ref, slab(*me), local_sem)
        mine.start()
        first = [copy(0, me, sibling, src=x_ref)]
        first += [copy(1 + j, me, (*chip, c), src=x_ref) for j, chip in enumerate(chips)]
        for cp in first:
            cp.start()
        passed = [copy(4 + j, (*chip, c), sibling) for j, chip in enumerate(chips)]
        for j, chip in enumerate(chips):
            copy(1 + j, (*chip, c), me).wait_recv()
            passed[j].start()
        copy(0, sibling, me).wait_recv()
        for j, chip in enumerate(chips):
            copy(4 + j, (*chip, 1 - c), me).wait_recv()
        for cp in first + passed:
            cp.wait_send()
        mine.wait()

    return pl.pallas_call(
        body, out_shape=SDS((N_DEV, R, W), loc.dtype), in_specs=[HBM_SPEC], out_specs=HBM_SPEC,
        scratch_shapes=[pltpu.SemaphoreType.DMA((7,)), pltpu.SemaphoreType.DMA((7,)), pltpu.SemaphoreType.DMA],
        name=name)(loc)


def _rs_pair_exchange(g, name):
    nd, R, W = g.shape
    nj = nd // 2

    def body(g_ref, out_ref, send_sems, recv_sems):
        x, y, c = _dev()
        cps = [pltpu.make_async_remote_copy(src_ref=g_ref.at[2 * j + (1 - c)], dst_ref=out_ref.at[j],
                                            send_sem=send_sems.at[j], recv_sem=recv_sems.at[j],
                                            device_id=(x, y, 1 - c), device_id_type=MESH) for j in range(nj)]
        for cp in cps:
            cp.start()
        for cp in cps:
            cp.wait()

    return pl.pallas_call(
        body, out_shape=SDS((nj, R, W), g.dtype), in_specs=[HBM_SPEC], out_specs=HBM_SPEC,
        scratch_shapes=[pltpu.SemaphoreType.DMA((nj,)), pltpu.SemaphoreType.DMA((nj,))], name=name)(g)


def _add_pairs(g, ra, name):
    nd, R, W = g.shape
    nj = nd // 2
    tr = _tile(R, 256, 8)
    cidx = lax.axis_index("c").astype(jnp.int32).reshape(1)

    def body(c_ref, a_ref, b_ref, o_ref):
        o_ref[...] = (a_ref[...].astype(F32) + b_ref[...].astype(F32)).astype(o_ref.dtype)

    gs = pltpu.PrefetchScalarGridSpec(
        num_scalar_prefetch=1, grid=(nj, R // tr),
        in_specs=[pl.BlockSpec((None, tr, W), lambda j, i, cr: (2 * j + cr[0], i, 0)),
                  pl.BlockSpec((None, tr, W), lambda j, i, cr: (j, i, 0))],
        out_specs=pl.BlockSpec((None, tr, W), lambda j, i, cr: (j, i, 0)))
    return pl.pallas_call(body, grid_spec=gs, out_shape=SDS((nj, R, W), g.dtype), name=name,
                          compiler_params=pltpu.CompilerParams(vmem_limit_bytes=VMEM_LIMIT))(cidx, g, ra)


def _rs_chip_exchange(p, name):
    nj, R, W = p.shape

    def body(p_ref, out_ref, send_sems, recv_sems, local_sem):
        x, y, c = _dev()
        j0 = 2 * x + y
        chips = [(1 - x, y), (x, 1 - y), (1 - x, 1 - y)]
        mine = pltpu.make_async_copy(p_ref.at[j0], out_ref.at[j0], local_sem)
        mine.start()

        def copy(k, chip):
            return pltpu.make_async_remote_copy(
                src_ref=p_ref.at[2 * chip[0] + chip[1]], dst_ref=out_ref.at[j0],
                send_sem=send_sems.at[k], recv_sem=recv_sems.at[k], device_id=(*chip, c), device_id_type=MESH)

        sent = [copy(k, chip) for k, chip in enumerate(chips)]
        for cp in sent:
            cp.start()
        for k, chip in enumerate(chips):
            pltpu.make_async_remote_copy(
                src_ref=p_ref.at[j0], dst_ref=out_ref.at[2 * chip[0] + chip[1]],
                send_sem=send_sems.at[k], recv_sem=recv_sems.at[k], device_id=(*chip, c), device_id_type=MESH).wait_recv()
        for cp in sent:
            cp.wait_send()
        mine.wait()

    return pl.pallas_call(
        body, out_shape=SDS((nj, R, W), p.dtype), in_specs=[HBM_SPEC], out_specs=HBM_SPEC,
        scratch_shapes=[pltpu.SemaphoreType.DMA((3,)), pltpu.SemaphoreType.DMA((3,)), pltpu.SemaphoreType.DMA],
        name=name)(p)


def _allgather_small(v, name):
    R, W = v.shape

    def body(v_ref, out_ref, send_sems, recv_sems, local_sem):
        x, y, c = _dev()
        mine = pltpu.make_async_copy(v_ref, out_ref.at[4 * x + 2 * y + c], local_sem)
        mine.start()
        peers = []
        for k in range(1, N_DEV):
            px = 1 - x if k & 4 else x
            py = 1 - y if k & 2 else y
            pc_ = 1 - c if k & 1 else c
            peers.append((px, py, pc_))
        sent = [pltpu.make_async_remote_copy(
            src_ref=v_ref, dst_ref=out_ref.at[4 * x + 2 * y + c], send_sem=send_sems.at[k], recv_sem=recv_sems.at[k],
            device_id=peer, device_id_type=MESH) for k, peer in enumerate(peers)]
        for cp in sent:
            cp.start()
        for k, (px, py, pc_) in enumerate(peers):
            pltpu.make_async_remote_copy(
                src_ref=v_ref, dst_ref=out_ref.at[4 * px + 2 * py + pc_], send_sem=send_sems.at[k],
                recv_sem=recv_sems.at[k], device_id=(px, py, pc_), device_id_type=MESH).wait_recv()
        for cp in sent:
            cp.wait_send()
        mine.wait()

    return pl.pallas_call(
        body, out_shape=SDS((N_DEV, R, W), v.dtype), in_specs=[VMEM_SPEC], out_specs=VMEM_SPEC,
        scratch_shapes=[pltpu.SemaphoreType.DMA((7,)), pltpu.SemaphoreType.DMA((7,)), pltpu.SemaphoreType.DMA],
        name=name)(v)


def _sum_slabs(a, name):
    ns, R, W = a.shape

    def body(a_ref, o_ref):
        acc = a_ref[0]
        for k in range(1, ns):
            acc = acc + a_ref[k]
        o_ref[...] = acc

    return pl.pallas_call(body, out_shape=SDS((R, W), a.dtype), in_specs=[VMEM_SPEC], out_specs=VMEM_SPEC, name=name)(a)


BIG = (("ada_w", "col"), ("w_in", "col"), ("w_ssd_out", "row"), ("w_sc_out", "row"), ("w_o", "row"), ("w_up", "col"),
       ("w_down", "row"))
CONVW = ("ssd_conv_w", "sc_conv_w", "ffn_conv_w")
REPL = ("ada_b", "mix_pre_g", "mix_post_g", "ssd_conv_b", "ssd_dt_bias", "ssd_a_log", "ssd_d", "ssd_norm_g", "ffn_pre_g",
        "ffn_post_g", "ffn_conv_b")
WEIGHTS = ("ada_w", "ada_b", "mix_pre_g", "mix_post_g", "w_in", "ssd_conv_w", "ssd_conv_b", "ssd_dt_bias", "ssd_a_log",
           "ssd_d", "ssd_norm_g", "w_ssd_out", "sc_conv_w", "w_sc_out", "w_o", "ffn_pre_g", "ffn_post_g", "w_up",
           "ffn_conv_w", "ffn_conv_b", "w_down")


def _pad_rows(a, mult):
    r = a.shape[-2]
    pad = -r % mult
    return a if pad == 0 else jnp.pad(a, [(0, 0)] * (a.ndim - 2) + [(0, pad), (0, 0)])


def _flat_rows(parts, mult):
    flat = jnp.concatenate([p.reshape(-1) for p in parts])
    flat = jnp.pad(flat, (0, -flat.shape[0] % ROW_W))
    return _pad_rows(flat.reshape(-1, ROW_W), mult)


def _unflat(buf, shapes):
    flat = buf.reshape(-1)
    out, o = [], 0
    for shp in shapes:
        n = 1
        for s in shp:
            n *= s
        out.append(flat[o:o + n].reshape(shp))
        o += n
    return out


def _pack_big_local(get, depth):
    return jnp.concatenate([_pad_rows((get(n)[l].T if kind == "col" else get(n)[l]).reshape(-1, ROW_W), SLAB_ALIGN)
                            for l in range(depth) for n, kind in BIG], axis=0)


def _big_rows(shapes, depth):
    out, o = {}, 0
    for l in range(depth):
        for n, _ in BIG:
            r = shapes[n][1] * shapes[n][2] // ROW_W
            out[(l, n)] = (o, o + r)
            o += -(-r // SLAB_ALIGN) * SLAB_ALIGN
    return out, o


def kernel(x, c, ada_w, ada_b, mix_pre_g, mix_post_g, w_in, ssd_conv_w, ssd_conv_b, ssd_dt_bias, ssd_a_log, ssd_d, ssd_norm_g, w_ssd_out, sc_conv_w, w_sc_out, w_o, ffn_pre_g, ffn_post_g, w_up, ffn_conv_w, ffn_conv_b, w_down, loss_target, m_ada_w, m_ada_b, m_mix_pre_g, m_mix_post_g, m_w_in, m_ssd_conv_w, m_ssd_conv_b, m_ssd_dt_bias, m_ssd_a_log, m_ssd_d, m_ssd_norm_g, m_w_ssd_out, m_sc_conv_w, m_w_sc_out, m_w_o, m_ffn_pre_g, m_ffn_post_g, m_w_up, m_ffn_conv_w, m_ffn_conv_b, m_w_down, v_ada_w, v_ada_b, v_mix_pre_g, v_mix_post_g, v_w_in, v_ssd_conv_w, v_ssd_conv_b, v_ssd_dt_bias, v_ssd_a_log, v_ssd_d, v_ssd_norm_g, v_w_ssd_out, v_sc_conv_w, v_w_sc_out, v_w_o, v_ffn_pre_g, v_ffn_post_g, v_w_up, v_ffn_conv_w, v_ffn_conv_b, v_w_down):
    wl = dict(zip(WEIGHTS, (ada_w, ada_b, mix_pre_g, mix_post_g, w_in, ssd_conv_w, ssd_conv_b, ssd_dt_bias, ssd_a_log,
                            ssd_d, ssd_norm_g, w_ssd_out, sc_conv_w, w_sc_out, w_o, ffn_pre_g, ffn_post_g, w_up,
                            ffn_conv_w, ffn_conv_b, w_down)))
    ml = dict(zip(WEIGHTS, (m_ada_w, m_ada_b, m_mix_pre_g, m_mix_post_g, m_w_in, m_ssd_conv_w, m_ssd_conv_b,
                            m_ssd_dt_bias, m_ssd_a_log, m_ssd_d, m_ssd_norm_g, m_w_ssd_out, m_sc_conv_w, m_w_sc_out, m_w_o,
                            m_ffn_pre_g, m_ffn_post_g, m_w_up, m_ffn_conv_w, m_ffn_conv_b, m_w_down)))
    vl = dict(zip(WEIGHTS, (v_ada_w, v_ada_b, v_mix_pre_g, v_mix_post_g, v_w_in, v_ssd_conv_w, v_ssd_conv_b,
                            v_ssd_dt_bias, v_ssd_a_log, v_ssd_d, v_ssd_norm_g, v_w_ssd_out, v_sc_conv_w, v_w_sc_out, v_w_o,
                            v_ffn_pre_g, v_ffn_post_g, v_w_up, v_ffn_conv_w, v_ffn_conv_b, v_w_down)))
    depth = ada_w.shape[0]
    shapes = {n: wl[n].shape for n in WEIGHTS}
    me = 4 * lax.axis_index("x") + 2 * lax.axis_index("y") + lax.axis_index("c")

    rows, n_big = _big_rows(shapes, depth)
    big_loc = _pack_big_local(lambda n: wl[n].astype(BF16), depth)
    conv_flat = jnp.concatenate([wl[n][l].reshape(-1) for l in range(depth) for n in CONVW])
    n_conv = conv_flat.shape[0]
    conv_flat = jnp.pad(conv_flat, (0, -n_conv % (ROW_W // 2)))
    conv_rows = lax.bitcast_convert_type(conv_flat, BF16).reshape(-1, ROW_W)
    gathered = _allgather_big(_pad_rows(jnp.concatenate([big_loc, conv_rows], axis=0), ROW_PAD), "allgather_weights")
    conv_all = lax.bitcast_convert_type(
        gathered[:, n_big:n_big + conv_rows.shape[0]].reshape(N_DEV, -1, 2), F32)[:, :n_conv]
    conv_full, o = {}, 0
    for l in range(depth):
        for n in CONVW:
            k, cl = shapes[n][1], shapes[n][2]
            conv_full[(l, n)] = conv_all[:, o:o + k * cl].reshape(N_DEV, k, cl).transpose(1, 0, 2).reshape(k, N_DEV * cl)
            o += k * cl
    layers = []
    for l in range(depth):
        w = {n: wl[n][l] for n in REPL}
        for n, kind in BIG:
            r0, r1 = rows[(l, n)]
            a, b = shapes[n][1], shapes[n][2]
            blk = gathered[:, r0:r1]
            w[n] = blk.reshape(N_DEV * b, a) if kind == "col" else blk.reshape(N_DEV * a, b)
        for n in CONVW:
            w[n] = conv_full[(l, n)]
        layers.append(w)

    loss_loc, dx, grads = _fwd_bwd(x, c, loss_target, layers)

    slabs = []
    for l in range(depth):
        for n, kind in BIG:
            slabs.append(_pad_rows(grads[l][n].astype(BF16).reshape(N_DEV, -1, ROW_W), SLAB_ALIGN))
    used_rows = sum(s.shape[1] for s in slabs)
    slabs.append(jnp.zeros((N_DEV, -used_rows % ROW_PAD, ROW_W), BF16))
    gslab = jnp.concatenate(slabs, axis=1)
    from_sibling = _rs_pair_exchange(gslab, "rs_pair_exchange")
    chip_sums = _add_pairs(gslab, from_sibling, "rs_pair_add")
    from_chips = _rs_chip_exchange(chip_sums, "rs_chip_exchange")
    g_sum = _sum_chips(from_chips, "rs_chip_sum")

    def grad_of(l, n, kind):
        blk = g_sum[rows[(l, n)][0]:rows[(l, n)][1]]
        a, b = shapes[n][1], shapes[n][2]
        return blk.reshape(b, a).T if kind == "col" else blk.reshape(a, b)

    g_big = {n: jnp.stack([grad_of(l, n, kind) for l in range(depth)]) for n, kind in BIG}
    d_big, m_big, v_big = {}, {}, {}
    for n, _ in BIG:
        d_big[n], m_big[n], v_big[n] = _adam_nat(g_big[n], wl[n], ml[n], vl[n], f"adam_{n}")

    parts = [jnp.broadcast_to(loss_loc, (ROW_W,))]
    small_shapes = [(ROW_W,)]
    for l in range(depth):
        for n in REPL + CONVW:
            parts.append(grads[l][n])
            small_shapes.append(tuple(grads[l][n].shape))
    total = _sum_slabs(_allgather_small(_flat_rows(parts, 8), "allgather_small"), "sum_small")
    pieces = _unflat(total, small_shapes)
    loss = pieces[0][0]
    g_small, i = {}, 1
    for l in range(depth):
        for n in REPL + CONVW:
            gp = pieces[i]
            i += 1
            if n in CONVW:
                gp = lax.dynamic_slice_in_dim(gp, me * shapes[n][2], shapes[n][2], axis=1)
            g_small[(l, n)] = gp
    order = [(l, n) for l in range(depth) for n in REPL + CONVW]
    loc_shapes = [tuple(shapes[n][1:]) for _, n in order]
    packs = lambda f: _flat_rows([f(l, n) for l, n in order], 8)
    gs_small = packs(lambda l, n: g_small[(l, n)])
    _, d_sm, m_sm, v_sm = _adam(gs_small[None], packs(lambda l, n: wl[n][l]), packs(lambda l, n: ml[n][l]),
                                packs(lambda l, n: vl[n][l]), "adam_small")

    def unpack_small(buf):
        ps = _unflat(buf, loc_shapes)
        return {n: jnp.stack([ps[order.index((l, n))] for l in range(depth)]) for n in REPL + CONVW}

    outs = []
    for big, small in ((g_big, {n: jnp.stack([g_small[(l, n)] for l in range(depth)]) for n in REPL + CONVW}),
                       (d_big, unpack_small(d_sm)), (m_big, unpack_small(m_sm)), (v_big, unpack_small(v_sm))):
        merged = {**big, **small}
        outs += [merged[n] for n in WEIGHTS]
    return (loss, dx, *outs)
```

```python
import math

import jax
import jax.numpy as jnp
from jax import lax
from jax.experimental import pallas as pl
from jax.experimental.pallas import tpu as pltpu

F32, BF16 = jnp.float32, jnp.bfloat16
WGRAD = BF16
SDS = jax.ShapeDtypeStruct
MESH = pl.DeviceIdType.MESH

EPS = 1e-6
N_STATE = 128
CHUNK = 128
SSD_CONV_K, SC_CONV_K, FFN_CONV_K = 4, 3, 3
N_DEV = 8
ROW_W = 1024
ROW_PAD = 256
SLAB_ALIGN = 16
SEG_BLK = 512
STRIP = 32
FFN_STRIP = 64
GLU_W = 256
MOD_ROWS = 128
VMEM_LIMIT = 48 * 2**20

ADAM_LR, ADAM_B1, ADAM_B2, ADAM_EPS, ADAM_WD, ADAM_STEP = 0.001, 0.9, 0.999, 1e-08, 0.01, 10

NT = (((1,), (1,)), ((), ()))
TN = (((0,), (0,)), ((), ()))
NN = (((1,), (0,)), ((), ()))


def _tile(n, cap, mult=128):
    best = None
    for t in range(mult, min(n, cap) + 1, mult):
        if n % t == 0:
            best = t
    return best if best is not None else n


def _pc(body, *, grid, in_specs, out_specs, out_shape, name, scratch=()):
    return pl.pallas_call(
        body, grid=grid, in_specs=in_specs, out_specs=out_specs, out_shape=out_shape,
        scratch_shapes=list(scratch), name=name,
        compiler_params=pltpu.CompilerParams(
            dimension_semantics=("arbitrary",) * len(grid), vmem_limit_bytes=VMEM_LIMIT))


def _silu(x):
    return x * jax.nn.sigmoid(x)


def _dsilu(x):
    s = jax.nn.sigmoid(x)
    return s * (1.0 + x * (1.0 - s))


def _softplus(x):
    return jnp.maximum(x, 0.0) + jnp.log(1.0 + jnp.exp(-jnp.abs(x)))


def _dot(a, b, dims=NN):
    return lax.dot_general(a, b, dims, preferred_element_type=F32)


def _bsum(v, rows=8):
    return jnp.broadcast_to(v, (rows, v.shape[1]))


def _mm(a, b, mode, out_dtype, name, caps=(1024, 1024, 2048)):
    if mode == "nn":
        (M, K), (K2, N) = a.shape, b.shape
    elif mode == "nt":
        (M, K), (N, K2) = a.shape, b.shape
    else:
        (K, M), (K2, N) = a.shape, b.shape
    assert K == K2, (a.shape, b.shape, mode)
    tm, tn, tk = _tile(M, caps[0]), _tile(N, caps[1]), _tile(K, caps[2])
    nk = K // tk
    dims = {"nn": NN, "nt": NT, "tn": TN}[mode]
    if mode == "tn":
        a_spec = pl.BlockSpec((tk, tm), lambda i, j, k: (k, i))
    else:
        a_spec = pl.BlockSpec((tm, tk), lambda i, j, k: (i, k))
    if mode == "nt":
        b_spec = pl.BlockSpec((tn, tk), lambda i, j, k: (j, k))
    else:
        b_spec = pl.BlockSpec((tk, tn), lambda i, j, k: (k, j))

    def body(a_ref, b_ref, o_ref, *acc):
        part = _dot(a_ref[...].astype(BF16), b_ref[...].astype(BF16), dims)
        if nk == 1:
            o_ref[...] = part.astype(o_ref.dtype)
        else:
            acc_ref, = acc
            k = pl.program_id(2)

            @pl.when(k == 0)
            def _():
                acc_ref[...] = part

            @pl.when(k > 0)
            def _():
                acc_ref[...] += part

            @pl.when(k == nk - 1)
            def _():
                o_ref[...] = acc_ref[...].astype(o_ref.dtype)

    return _pc(body, grid=(M // tm, N // tn, nk), in_specs=[a_spec, b_spec],
               out_specs=pl.BlockSpec((tm, tn), lambda i, j, k: (i, j)),
               out_shape=SDS((M, N), out_dtype), name=name,
               scratch=() if nk == 1 else (pltpu.VMEM((tm, tn), F32),))(a, b)


def _mm_seg(segs, b, mode, out_dtype, name, blk, tile=1024, tk=2048):
    nblk = [a.shape[1] // blk for a in segs]
    assert all(a.shape[1] % blk == 0 for a in segs)
    start = [sum(nblk[:s]) for s in range(len(segs))]
    total = sum(nblk)
    ns = len(segs)
    N = b.shape[1]
    tn = _tile(N, tile)
    if mode == "nn":
        M = segs[0].shape[0]
        tm = _tile(M, tile)
        grid = (M // tm, N // tn, total)
        a_specs = [pl.BlockSpec((tm, blk), lambda i, j, k, k0=k0, n=n: (i, jnp.clip(k - k0, 0, n - 1)))
                   for k0, n in zip(start, nblk)]
        b_spec = pl.BlockSpec((blk, tn), lambda i, j, k: (k, j))
        out_rows, tmo, dims, seg_axis = M, tm, NN, 2
    else:
        K = segs[0].shape[0]
        tkk = _tile(K, tk)
        grid = (total, N // tn, K // tkk)
        a_specs = [pl.BlockSpec((tkk, blk), lambda i, j, k, i0=i0, n=n: (k, jnp.clip(i - i0, 0, n - 1)))
                   for i0, n in zip(start, nblk)]
        b_spec = pl.BlockSpec((tkk, tn), lambda i, j, k: (k, j))
        out_rows, tmo, seg_axis = total * blk, blk, 0
    nk = grid[2]
    acc_shape = (tm, tn) if mode == "nn" else (tn, blk)

    def body(*refs):
        a_refs, b_ref, o_ref, acc_ref = refs[:ns], refs[ns], refs[ns + 1], refs[ns + 2]
        k = pl.program_id(2)
        sel = pl.program_id(seg_axis)

        @pl.when(k == 0)
        def _():
            acc_ref[...] = jnp.zeros_like(acc_ref)

        for s in range(ns):
            @pl.when((sel >= start[s]) & (sel < start[s] + nblk[s]))
            def _(s=s):
                a_, b_ = a_refs[s][...].astype(BF16), b_ref[...].astype(BF16)
                acc_ref[...] += _dot(a_, b_, NN) if mode == "nn" else _dot(b_, a_, TN)

        @pl.when(k == nk - 1)
        def _():
            acc = acc_ref[...]
            o_ref[...] = (acc if mode == "nn" else acc.T).astype(o_ref.dtype)

    return _pc(body, grid=grid, in_specs=a_specs + [b_spec], out_specs=pl.BlockSpec((tmo, tn), lambda i, j, k: (i, j)),
               out_shape=SDS((out_rows, N), out_dtype), name=name, scratch=(pltpu.VMEM(acc_shape, F32),))(*segs, b)


def _modk(c8, ada_w, ada_b, name):
    rows, D = c8.shape
    N = ada_w.shape[0]
    tn = _tile(N, 1536)

    def body(c_ref, w_ref, b_ref, mod_ref, ca_ref):
        ca = _silu(c_ref[...]).astype(BF16)
        mod_ref[...] = _dot(ca, w_ref[...], NT) + b_ref[...]
        ca_ref[...] = ca

    return _pc(body, grid=(N // tn,),
               in_specs=[pl.BlockSpec((rows, D), lambda j: (0, 0)), pl.BlockSpec((tn, D), lambda j: (j, 0)),
                         pl.BlockSpec((1, tn), lambda j: (0, j))],
               out_specs=[pl.BlockSpec((rows, tn), lambda j: (0, j)), pl.BlockSpec((rows, D), lambda j: (0, 0))],
               out_shape=[SDS((rows, N), F32), SDS((rows, D), BF16)], name=name)(c8, ada_w, ada_b)


def _row_tile(S):
    return _tile(S, 512, 8)


def _strips(tm, fn, init=0, rows=None):
    rows = STRIP if rows is None else rows
    assert tm % rows == 0
    return lax.fori_loop(0, tm // rows, lambda r, c: fn(pl.multiple_of(r * rows, rows), c), init)


def _rows8(rows):
    pad = 8 - len(rows)
    return jnp.concatenate(rows + ([jnp.zeros((pad, rows[0].shape[1]), F32)] if pad else []), axis=0)


def _fold8(v):
    return jnp.sum(v.reshape(v.shape[0] // 8, 8, v.shape[1]), axis=0)


def _norm_mod(x, g, mod3, sc_seg, sh_seg, S, name):
    T, D = x.shape
    tm = _row_tile(S)
    tpb = S // tm

    def body(x_ref, g_ref, sc_ref, sh_ref, h_ref):
        def strip(r0, carry):
            x_ = x_ref[pl.ds(r0, STRIP), :]
            r = lax.rsqrt(jnp.mean(x_ * x_, axis=-1, keepdims=True) + EPS)
            h_ref[pl.ds(r0, STRIP), :] = ((x_ * r) * (g_ref[...] * (1.0 + sc_ref[...])) + sh_ref[...]).astype(BF16)
            return carry

        _strips(tm, strip)

    return _pc(body, grid=(T // tm,),
               in_specs=[pl.BlockSpec((tm, D), lambda i: (i, 0)), pl.BlockSpec((1, D), lambda i: (0, 0)),
                         pl.BlockSpec((None, 1, D), lambda i: (i // tpb, 0, sc_seg)),
                         pl.BlockSpec((None, 1, D), lambda i: (i // tpb, 0, sh_seg))],
               out_specs=pl.BlockSpec((tm, D), lambda i: (i, 0)), out_shape=SDS((T, D), BF16), name=name)(x, g, mod3, mod3)


def _resid_post(x, fo, mod3, gt_seg, pg, S, name):
    T, D = x.shape
    tm = _row_tile(S)
    tpb = S // tm

    def body(x_ref, f_ref, gt_ref, pg_ref, o_ref):
        def strip(r0, carry):
            rows = pl.ds(r0, STRIP)
            f = f_ref[rows, :]
            r = lax.rsqrt(jnp.mean(f * f, axis=-1, keepdims=True) + EPS)
            o_ref[rows, :] = x_ref[rows, :] + (f * r) * (gt_ref[...] * pg_ref[...])
            return carry

        _strips(tm, strip)

    return _pc(body, grid=(T // tm,),
               in_specs=[pl.BlockSpec((tm, D), lambda i: (i, 0)), pl.BlockSpec((tm, D), lambda i: (i, 0)),
                         pl.BlockSpec((None, 1, D), lambda i: (i // tpb, 0, gt_seg)),
                         pl.BlockSpec((1, D), lambda i: (0, 0))],
               out_specs=pl.BlockSpec((tm, D), lambda i: (i, 0)), out_shape=SDS((T, D), F32), name=name)(x, fo, mod3, pg)


def _post_bwd(fo, mod3, gt_seg, pg, dout, S, name):
    T, D = fo.shape
    tm = _row_tile(S)
    tpb = S // tm
    nb = T // S

    def body(f_ref, gt_ref, pg_ref, d_ref, df_ref, dgt_ref, dpg_ref):
        i = pl.program_id(0)

        @pl.when(i == 0)
        def _():
            dpg_ref[...] = jnp.zeros_like(dpg_ref)

        @pl.when(i % tpb == 0)
        def _():
            dgt_ref[...] = jnp.zeros_like(dgt_ref)

        def strip(r0, carry):
            rows = pl.ds(r0, STRIP)
            f, d = f_ref[rows, :], d_ref[rows, :]
            r = lax.rsqrt(jnp.mean(f * f, axis=-1, keepdims=True) + EPS)
            n = f * r
            dn = d * (gt_ref[...] * pg_ref[...])
            df_ref[rows, :] = (r * (dn - n * jnp.mean(dn * n, axis=-1, keepdims=True))).astype(df_ref.dtype)
            return carry + _fold8(d * n)

        acc = _strips(tm, strip, jnp.zeros((8, D), F32))
        tot = jnp.sum(acc, axis=0, keepdims=True)
        dgt_ref[...] += _bsum(tot * pg_ref[...])
        dpg_ref[...] += _bsum(tot * gt_ref[...])

    return _pc(body, grid=(T // tm,),
               in_specs=[pl.BlockSpec((tm, D), lambda i: (i, 0)),
                         pl.BlockSpec((None, 1, D), lambda i: (i // tpb, 0, gt_seg)),
                         pl.BlockSpec((1, D), lambda i: (0, 0)), pl.BlockSpec((tm, D), lambda i: (i, 0))],
               out_specs=[pl.BlockSpec((tm, D), lambda i: (i, 0)), pl.BlockSpec((8, D), lambda i: (i // tpb, 0)),
                          pl.BlockSpec((8, D), lambda i: (0, 0))],
               out_shape=[SDS((T, D), BF16), SDS((nb * 8, D), F32), SDS((8, D), F32)], name=name)(fo, mod3, pg, dout)


def _pre_bwd(x, g, mod3, sc_seg, dh, dout, S, name):
    T, D = x.shape
    tm = _row_tile(S)
    tpb = S // tm
    nb = T // S

    def body(x_ref, g_ref, sc_ref, dh_ref, d_ref, dx_ref, dg_ref, dsc_ref, dsh_ref):
        i = pl.program_id(0)

        @pl.when(i == 0)
        def _():
            dg_ref[...] = jnp.zeros_like(dg_ref)

        @pl.when(i % tpb == 0)
        def _():
            dsc_ref[...] = jnp.zeros_like(dsc_ref)
            dsh_ref[...] = jnp.zeros_like(dsh_ref)

        def strip(r0, carry):
            rows = pl.ds(r0, STRIP)
            x_, dh_ = x_ref[rows, :], dh_ref[rows, :]
            r = lax.rsqrt(jnp.mean(x_ * x_, axis=-1, keepdims=True) + EPS)
            n = x_ * r
            dn = dh_ * (g_ref[...] * (1.0 + sc_ref[...]))
            dx_ref[rows, :] = d_ref[rows, :] + r * (dn - n * jnp.mean(dn * n, axis=-1, keepdims=True))
            return carry[0] + _fold8(dh_ * n), carry[1] + _fold8(dh_)

        zero = jnp.zeros((8, D), F32)
        dhn, dhs = _strips(tm, strip, (zero, zero))
        dhn = jnp.sum(dhn, axis=0, keepdims=True)
        dg_ref[...] += _bsum(dhn * (1.0 + sc_ref[...]))
        dsc_ref[...] += _bsum(dhn * g_ref[...])
        dsh_ref[...] += _bsum(jnp.sum(dhs, axis=0, keepdims=True))

    row = pl.BlockSpec((tm, D), lambda i: (i, 0))
    return _pc(body, grid=(T // tm,),
               in_specs=[row, pl.BlockSpec((1, D), lambda i: (0, 0)),
                         pl.BlockSpec((None, 1, D), lambda i: (i // tpb, 0, sc_seg)), row, row],
               out_specs=[row, pl.BlockSpec((8, D), lambda i: (0, 0)), pl.BlockSpec((8, D), lambda i: (i // tpb, 0)),
                          pl.BlockSpec((8, D), lambda i: (i // tpb, 0))],
               out_shape=[SDS((T, D), F32), SDS((8, D), F32), SDS((nb * 8, D), F32), SDS((nb * 8, D), F32)],
               name=name)(x, g, mod3, dh, dout)


def _loss(y, target, S, name):
    T, D = y.shape
    tm = _row_tile(S)

    def body(y_ref, t_ref, dy_ref, l_ref):
        @pl.when(pl.program_id(0) == 0)
        def _():
            l_ref[...] = jnp.zeros_like(l_ref)

        def strip(r0, carry):
            rows = pl.ds(r0, STRIP)
            e = y_ref[rows, :] - t_ref[rows, :]
            dy_ref[rows, :] = e * (1.0 / D)
            return carry + _fold8(e * e)

        acc = _strips(tm, strip, jnp.zeros((8, D), F32))
        l_ref[...] += jnp.broadcast_to(jnp.sum(acc, keepdims=True) * (0.5 / D), l_ref.shape)

    row = pl.BlockSpec((tm, D), lambda i: (i, 0))
    return _pc(body, grid=(T // tm,), in_specs=[row, row],
               out_specs=[row, pl.BlockSpec((8, 128), lambda i: (0, 0))],
               out_shape=[SDS((T, D), F32), SDS((8, 128), F32)], name=name)(y, target)


def _conv_geom(view, C, S):
    arr, off = view
    T = arr.shape[0]
    tm = _row_tile(S)
    tc = _tile(C, 512)
    assert off % tc == 0 and C % tc == 0
    return arr, off // tc, T, tm, tc, S // tm


def _prev_spec(tm, tc, ob, order):
    if order == "ij":
        return pl.BlockSpec((8, tc), lambda i, j: (jnp.maximum(i * (tm // 8) - 1, 0), ob + j))
    return pl.BlockSpec((8, tc), lambda j, i: (jnp.maximum(i * (tm // 8) - 1, 0), ob + j))


def _next_spec(T, tm, tc, ob, order):
    last = T // 8 - 1
    if order == "ij":
        return pl.BlockSpec((8, tc), lambda i, j: (jnp.minimum((i + 1) * (tm // 8), last), ob + j))
    return pl.BlockSpec((8, tc), lambda j, i: (jnp.minimum((i + 1) * (tm // 8), last), ob + j))


def _taps(win, w_ref, K, lead, rows):
    acc = win[lead:lead + rows] * w_ref[K - 1:K, :]
    for j in range(1, K):
        acc = acc + win[lead - j:lead - j + rows] * w_ref[K - 1 - j:K - j, :]
    return acc


def _taps_t(win, w_ref, K, rows):
    acc = win[0:rows] * w_ref[K - 1:K, :]
    for j in range(1, K):
        acc = acc + win[j:j + rows] * w_ref[K - 1 - j:K - j, :]
    return acc


def _conv_fwd(view, C, w8, b, K, S, name):
    arr, ob, T, tm, tc, tps = _conv_geom(view, C, S)

    def body(u_ref, p_ref, w_ref, b_ref, o_ref, buf):
        first = (pl.program_id(0) % tps) == 0
        buf[0:8, :] = jnp.where(first, 0.0, p_ref[...])
        buf[8:, :] = u_ref[...]

        def strip(r0, carry):
            win = buf[pl.ds(r0, STRIP + 8), :]
            o_ref[pl.ds(r0, STRIP), :] = _taps(win, w_ref, K, 8, STRIP) + b_ref[...]
            return carry

        _strips(tm, strip)

    return _pc(body, grid=(T // tm, C // tc),
               in_specs=[pl.BlockSpec((tm, tc), lambda i, j: (i, ob + j)), _prev_spec(tm, tc, ob, "ij"),
                         pl.BlockSpec((8, tc), lambda i, j: (0, j)), pl.BlockSpec((1, tc), lambda i, j: (0, j))],
               out_specs=pl.BlockSpec((tm, tc), lambda i, j: (i, j)), out_shape=SDS((T, C), F32), name=name,
               scratch=(pltpu.VMEM((tm + 8, tc), F32),))(arr, arr, w8, b)


def _conv_bwd_in(dview, C, w8, K, S, out_dtype, name):
    arr, ob, T, tm, tc, tps = _conv_geom(dview, C, S)

    def body(d_ref, n_ref, w_ref, o_ref, buf):
        last = (pl.program_id(0) % tps) == tps - 1
        buf[0:tm, :] = d_ref[...]
        buf[tm:tm + 8, :] = jnp.where(last, 0.0, n_ref[...])

        def strip(r0, carry):
            win = buf[pl.ds(r0, STRIP + 8), :]
            o_ref[pl.ds(r0, STRIP), :] = _taps_t(win, w_ref, K, STRIP).astype(o_ref.dtype)
            return carry

        _strips(tm, strip)

    return _pc(body, grid=(T // tm, C // tc),
               in_specs=[pl.BlockSpec((tm, tc), lambda i, j: (i, ob + j)), _next_spec(T, tm, tc, ob, "ij"),
                         pl.BlockSpec((8, tc), lambda i, j: (0, j))],
               out_specs=pl.BlockSpec((tm, tc), lambda i, j: (i, j)), out_shape=SDS((T, C), out_dtype), name=name,
               scratch=(pltpu.VMEM((tm + 8, tc), F32),))(arr, arr, w8)


def _conv_bwd_w(dview, uview, C, K, S, name):
    darr, dob, T, tm, tc, tps = _conv_geom(dview, C, S)
    uarr, uob, _, _, _, _ = _conv_geom(uview, C, S)

    def body(d_ref, u_ref, p_ref, o_ref, buf):
        i = pl.program_id(1)

        @pl.when(i == 0)
        def _():
            o_ref[...] = jnp.zeros_like(o_ref)

        first = (i % tps) == 0
        buf[0:8, :] = jnp.where(first, 0.0, p_ref[...])
        buf[8:, :] = u_ref[...]

        def strip(r0, carry):
            win = buf[pl.ds(r0, STRIP + 8), :]
            d = d_ref[pl.ds(r0, STRIP), :]
            sums = [_fold8(d * win[8 - (K - 1 - k):8 - (K - 1 - k) + STRIP]) for k in range(K)] + [_fold8(d)]
            return tuple(c + s for c, s in zip(carry, sums))

        acc = _strips(tm, strip, tuple(jnp.zeros((8, tc), F32) for _ in range(K + 1)))
        o_ref[...] += _rows8([jnp.sum(a, axis=0, keepdims=True) for a in acc])

    return _pc(body, grid=(C // tc, T // tm),
               in_specs=[pl.BlockSpec((tm, tc), lambda j, i: (i, dob + j)),
                         pl.BlockSpec((tm, tc), lambda j, i: (i, uob + j)), _prev_spec(tm, tc, uob, "ji")],
               out_specs=pl.BlockSpec((8, tc), lambda j, i: (0, j)), out_shape=SDS((8, C), F32), name=name,
               scratch=(pltpu.VMEM((tm + 8, tc), F32),))(darr, uarr, uarr)


def _ffn_act_fwd(uu, w8, b, S, name):
    K, gw = FFN_CONV_K, GLU_W
    T, F2 = uu.shape
    tm, tc = _row_tile(S), 2 * GLU_W
    tps = S // tm

    def body(u_ref, p_ref, w_ref, b_ref, a_ref, buf):
        first = (pl.program_id(0) % tps) == 0
        buf[0:8, :] = jnp.where(first, 0.0, p_ref[...])
        buf[8:, :] = u_ref[...]

        def strip(r0, carry):
            u = _taps(buf[pl.ds(r0, STRIP + 8), :], w_ref, K, 8, STRIP) + b_ref[...]
            a_ref[pl.ds(r0, STRIP), :] = (_silu(u[:, :gw]) * u[:, gw:]).astype(BF16)
            return carry

        _strips(tm, strip)

    return _pc(body, grid=(T // tm, F2 // tc),
               in_specs=[pl.BlockSpec((tm, tc), lambda i, j: (i, j)), _prev_spec(tm, tc, 0, "ij"),
                         pl.BlockSpec((8, tc), lambda i, j: (0, j)), pl.BlockSpec((1, tc), lambda i, j: (0, j))],
               out_specs=pl.BlockSpec((tm, gw), lambda i, j: (i, j)), out_shape=SDS((T, F2 // 2), BF16), name=name,
               scratch=(pltpu.VMEM((tm + 8, tc), F32),))(uu, uu, w8, b)


def _ffn_act_bwd(uu, da, w8, b, S, name):
    K, gw = FFN_CONV_K, GLU_W
    T, F2 = uu.shape
    tm, tc = _row_tile(S), 2 * GLU_W
    tps = S // tm
    last16 = T // 16 - 1

    def body(u_ref, p_ref, n_ref, da_ref, dan_ref, w_ref, b_ref, duu_ref, cw_ref, ubuf, dabuf):
        i = pl.program_id(1)

        @pl.when(i == 0)
        def _():
            cw_ref[...] = jnp.zeros_like(cw_ref)

        first = (i % tps) == 0
        last = (i % tps) == tps - 1
        ubuf[0:8, :] = jnp.where(first, 0.0, p_ref[...])
        ubuf[8:tm + 8, :] = u_ref[...]
        ubuf[tm + 8:tm + 16, :] = n_ref[...]
        dabuf[0:tm, :] = da_ref[...].astype(F32)
        dabuf[tm:tm + 8, :] = jnp.where(last, 0.0, dan_ref[...].astype(F32)[0:8, :])

        def strip(r0, carry):
            ext = FFN_STRIP + 8
            win = ubuf[pl.ds(r0, FFN_STRIP + 16), :]
            u = _taps(win, w_ref, K, 8, ext) + b_ref[...]
            da_ = dabuf[pl.ds(r0, ext), :]
            g, v = u[:, :gw], u[:, gw:]
            du = jnp.concatenate([da_ * v * _dsilu(g), da_ * _silu(g)], axis=1)
            duu_ref[pl.ds(r0, FFN_STRIP), :] = _taps_t(du, w_ref, K, FFN_STRIP).astype(BF16)
            dmain = du[0:FFN_STRIP]
            sums = [_fold8(dmain * win[8 - (K - 1 - k):8 - (K - 1 - k) + FFN_STRIP]) for k in range(K)] + [_fold8(dmain)]
            return tuple(c + s for c, s in zip(carry, sums))

        acc = _strips(tm, strip, tuple(jnp.zeros((8, tc), F32) for _ in range(K + 1)), rows=FFN_STRIP)
        cw_ref[...] += _rows8([jnp.sum(a, axis=0, keepdims=True) for a in acc])

    return _pc(body, grid=(F2 // tc, T // tm),
               in_specs=[pl.BlockSpec((tm, tc), lambda j, i: (i, j)), _prev_spec(tm, tc, 0, "ji"),
                         _next_spec(T, tm, tc, 0, "ji"), pl.BlockSpec((tm, gw), lambda j, i: (i, j)),
                         pl.BlockSpec((16, gw), lambda j, i: (jnp.minimum((i + 1) * (tm // 16), last16), j)),
                         pl.BlockSpec((8, tc), lambda j, i: (0, j)), pl.BlockSpec((1, tc), lambda j, i: (0, j))],
               out_specs=[pl.BlockSpec((tm, tc), lambda j, i: (i, j)), pl.BlockSpec((8, tc), lambda j, i: (0, j))],
               out_shape=[SDS((T, F2), BF16), SDS((8, F2), F32)], name=name,
               scratch=(pltpu.VMEM((tm + 16, tc), F32), pltpu.VMEM((tm + 8, gw), F32)))(uu, uu, uu, da, da, w8, b)


def _ssd_common(dtc_raw, dtr_raw, hpc, hpr, L):
    dt_c = _softplus(dtc_raw + hpc[0:1, :])
    a_c = -jnp.exp(hpc[1:2, :])
    dt_r = _softplus(dtr_raw + hpr[:, 0:1])
    a_r = -jnp.exp(hpr[:, 1:2])
    li = lax.broadcasted_iota(jnp.int32, (L, L), 0)
    si = lax.broadcasted_iota(jnp.int32, (L, L), 1)
    low = li >= si
    upp = li <= si
    acs_c = _dotx(low, dt_c * a_c, split="b")
    acs_r = _dotx(dt_r * a_r, upp)
    return dt_c, a_c, acs_c, acs_r, low, upp


def _dotx(a, b, split="a", parts=3, dims=NN):
    val, one = (a, b) if split == "a" else (b, a)
    one = one.astype(BF16)
    acc, rem = None, val
    for i in range(parts):
        piece = rem.astype(BF16)
        t = _dot(piece, one, dims) if split == "a" else _dot(one, piece, dims)
        acc = t if acc is None else acc + t
        if i + 1 < parts:
            rem = rem - piece.astype(F32)
    return acc


def _head_maps(R, P, L):
    RP = R * P
    sel = (lax.broadcasted_iota(jnp.int32, (RP, R), 0) // P == lax.broadcasted_iota(jnp.int32, (RP, R), 1)).astype(F32)
    selt = (lax.broadcasted_iota(jnp.int32, (R, RP), 1) // P == lax.broadcasted_iota(jnp.int32, (R, RP), 0)).astype(F32)
    colb = (lax.broadcasted_iota(jnp.int32, (R, R * L), 1) // L == lax.broadcasted_iota(jnp.int32, (R, R * L), 0)).astype(F32)
    return sel, selt, colb


def _pair_diag(mats, rhs_b, R, P):
    lanes = 2 * P
    lo = lax.broadcasted_iota(jnp.int32, (mats[0].shape[0], lanes), 1) < P
    out = []
    for q in range(R // 2):
        rp = rhs_b[:, q * lanes:(q + 1) * lanes]
        out.append(jnp.where(lo, _dot(mats[2 * q], rp), _dot(mats[2 * q + 1], rp)))
    return jnp.concatenate(out, axis=1) if len(out) > 1 else out[0]


def _ssd_specs(pre, off_x, off_b, off_c, G, R, P, nb, nc, rev):
    L, N, RP = CHUNK, N_STATE, R * P
    cidx = (lambda c: nc - 1 - c) if rev else (lambda c: c)
    xb, bb, cb = off_x // RP, off_b // N, off_c // N
    assert off_x % RP == 0 and off_b % N == 0 and off_c % N == 0
    row = lambda b, c: b * nc + cidx(c)
    return dict(
        x=pl.BlockSpec((L, RP), lambda g, b, c: (row(b, c), xb + g)),
        b=pl.BlockSpec((L, N), lambda g, b, c: (row(b, c), bb + g)),
        c=pl.BlockSpec((L, N), lambda g, b, c: (row(b, c), cb + g)),
        dtc=pl.BlockSpec((None, L, R), lambda g, b, c: (g, row(b, c), 0)),
        dtr=pl.BlockSpec((None, R, L), lambda g, b, c: (g, 0, row(b, c))),
        hpc=pl.BlockSpec((None, 8, R), lambda g, b, c: (g, 0, 0)),
        hpr=pl.BlockSpec((None, R, 8), lambda g, b, c: (g, 0, 0)),
        y=pl.BlockSpec((L, RP), lambda g, b, c: (row(b, c), g)),
        bc=pl.BlockSpec((L, N), lambda g, b, c: (row(b, c), g)),
        hs=pl.BlockSpec((None, None, N, RP), lambda g, b, c: (row(b, c), g, 0, 0)),
    )


def _ssd_fwd(pre, offs, dtc, dtr, hpc, hpr, G, R, P, S, name):
    T = pre.shape[0]
    L, N, RP = CHUNK, N_STATE, R * P
    nc, nb = S // L, T // S
    sp = _ssd_specs(pre, *offs, G, R, P, nb, nc, False)

    def body(px_ref, pb_ref, pc_ref, dtc_ref, dtr_ref, hpc_ref, hpr_ref, y_ref, hs_ref, hst):
        @pl.when(pl.program_id(2) == 0)
        def _():
            hst[...] = jnp.zeros_like(hst)

        xs, bm, cm = _silu(px_ref[...]), _silu(pb_ref[...]), _silu(pc_ref[...])
        hpc_ = hpc_ref[...]
        dt_c, _, acs_c, acs_r, low, _ = _ssd_common(dtc_ref[...], dtr_ref[...], hpc_, hpr_ref[...], L)
        _, selt, colb = _head_maps(R, P, L)
        dt_e, a_e, hp_e = _dotx(dt_c, selt), _dotx(acs_c, selt), _dotx(hpc_, selt)
        a_bc = _dotx(acs_c, colb)
        a_last = a_e[L - 1:L, :]
        bb, cb = bm.astype(BF16), cm.astype(BF16)
        gm = _dot(cb, bb, NT)
        hprev = hst[...]
        hprev_b = hprev.astype(BF16)
        hs_ref[...] = hprev_b
        xdt = xs * dt_e
        xdt_b = xdt.astype(BF16)
        ms = []
        for r in range(R):
            dec = jnp.exp(jnp.where(low, a_bc[:, r * L:(r + 1) * L] - acs_r[r:r + 1, :], -jnp.inf))
            ms.append((gm * dec).astype(BF16))
        y = _pair_diag(ms, xdt_b, R, P) + _dot(cb, hprev_b) * jnp.exp(a_e) + hp_e[2:3, :] * xs
        y_ref[...] = y
        xw = (xdt * jnp.exp(a_last - a_e)).astype(BF16)
        hst[...] = hprev * jnp.exp(a_last) + _dot(bb, xw, TN)

    return _pc(body, grid=(G, nb, nc),
               in_specs=[sp["x"], sp["b"], sp["c"], sp["dtc"], sp["dtr"], sp["hpc"], sp["hpr"]],
               out_specs=[sp["y"], sp["hs"]],
               out_shape=[SDS((T, G * RP), F32), SDS((nb * nc, G, N, RP), BF16)], name=name,
               scratch=(pltpu.VMEM((N, RP), F32),))(pre, pre, pre, dtc, dtr, hpc, hpr)


def _ssd_bwd(pre, offs, dtc, dtr, hpc, hpr, hs, dy, G, R, P, S, name):
    T = pre.shape[0]
    L, N, RP = CHUNK, N_STATE, R * P
    nc, nb = S // L, T // S
    sp = _ssd_specs(pre, *offs, G, R, P, nb, nc, True)

    def body(px_ref, pb_ref, pc_ref, dtc_ref, dtr_ref, hpc_ref, hpr_ref, hs_ref, dy_ref,
             dpx_ref, dpb_ref, dpc_ref, ddt_ref, hpg_ref, dhst):
        bi, ci = pl.program_id(1), pl.program_id(2)

        @pl.when(ci == 0)
        def _():
            dhst[...] = jnp.zeros_like(dhst)

        @pl.when((bi == 0) & (ci == 0))
        def _():
            hpg_ref[...] = jnp.zeros_like(hpg_ref)

        px, pb, pcc = px_ref[...], pb_ref[...], pc_ref[...]
        xs, bm, cm = _silu(px), _silu(pb), _silu(pcc)
        hpc_ = hpc_ref[...]
        dtc_raw = dtc_ref[...]
        dt_c, a_c, acs_c, acs_r, low, upp = _ssd_common(dtc_raw, dtr_ref[...], hpc_, hpr_ref[...], L)
        sel, selt, colb = _head_maps(R, P, L)
        dt_e, a_e, hp_e = _dotx(dt_c, selt), _dotx(acs_c, selt), _dotx(hpc_, selt)
        a_bc = _dotx(acs_c, colb)
        a_last = a_e[L - 1:L, :]
        e_e, w_e = jnp.exp(a_e), jnp.exp(a_last - a_e)
        bb, cb = bm.astype(BF16), cm.astype(BF16)
        gm = _dot(cb, bb, NT)
        gmt = _dot(bb, cb, NT)
        hprev = hs_ref[...]
        dhn = dhst[...]
        dhn_b = dhn.astype(BF16)
        dy = dy_ref[...]
        dy_b = dy.astype(BF16)
        xdt = xs * dt_e
        xdt_b = xdt.astype(BF16)
        yoff = _dot(cb, hprev) * e_e
        dye_b = (dy * e_e).astype(BF16)
        dcm = _dot(dye_b, hprev, NT)
        dhst[...] = _dot(cb, dye_b, TN) + jnp.exp(a_last) * dhn
        dxdt_st = _dot(bb, dhn_b) * w_e
        dbm = _dot((xdt * w_e).astype(BF16), dhn_b, NT)
        lanes = 2 * P
        lo = lax.broadcasted_iota(jnp.int32, (L, lanes), 1) < P
        dg = jnp.zeros((L, L), F32)
        es, css = [], []
        for r in range(R):
            col_b, row = a_bc[:, r * L:(r + 1) * L], acs_r[r:r + 1, :]
            dec = jnp.exp(jnp.where(low, col_b - row, -jnp.inf))
            q = r // 2
            dyp = dy_b[:, q * lanes:(q + 1) * lanes]
            dyp = jnp.where(lo if r % 2 == 0 else ~lo, dyp, jnp.zeros_like(dyp))
            dm = _dot(dyp, xdt_b[:, q * lanes:(q + 1) * lanes], NT)
            dg = dg + dm * dec
            e = dm * (gm * dec)
            es.append(e)
            css.append(jnp.sum(e, axis=0, keepdims=True))
        dgb = dg.astype(BF16)
        dcm = dcm + _dot(dgb, bb)
        dbm = dbm + _dot(dgb, cb, TN)
        colbt = (lax.broadcasted_iota(jnp.int32, (R * L, R), 0) // L
                 == lax.broadcasted_iota(jnp.int32, (R * L, R), 1)).astype(F32)
        eye = (lax.broadcasted_iota(jnp.int32, (R, R), 0) == lax.broadcasted_iota(jnp.int32, (R, R), 1)).astype(F32)
        row_sums = _dotx(jnp.concatenate(es, axis=1), colbt)
        col_sums = _dotx(jnp.concatenate(css, axis=0), eye, dims=TN)
        mts = []
        for r in range(R):
            dect = jnp.exp(jnp.where(upp, acs_r[r:r + 1, :] - a_bc[:, r * L:(r + 1) * L], -jnp.inf))
            mts.append((gmt * dect).astype(BF16))
        dxdt = _pair_diag(mts, dy_b, R, P) + dxdt_st
        q_st = _dotx(xdt * dxdt_st, sel, parts=2)
        da = row_sums - col_sums + _dotx(dy * yoff, sel, parts=2) - q_st
        hh = jnp.sum(_dotx(dhn * hprev.astype(F32), sel, parts=2), axis=0, keepdims=True)
        da_last = jnp.exp(acs_c[L - 1:L, :]) * hh + jnp.sum(q_st, axis=0, keepdims=True)
        rowi = lax.broadcasted_iota(jnp.int32, (L, R), 0)
        da = da + jnp.where(rowi == L - 1, da_last, 0.0)
        dpx_ref[...] = (dxdt * dt_e + hp_e[2:3, :] * dy) * _dsilu(px)
        dpb_ref[...] = dbm * _dsilu(pb)
        dpc_ref[...] = dcm * _dsilu(pcc)
        dadt = _dotx(upp, da, split="b")
        ddt = _dotx(dxdt * xs, sel, parts=2) + dadt * a_c
        ddt_raw = ddt * jax.nn.sigmoid(dtc_raw + hpc_[0:1, :])
        ddt_ref[...] = ddt_raw
        d_a = jnp.sum(dadt * dt_c, axis=0, keepdims=True)
        d_d = jnp.sum(_dotx(dy * xs, sel, parts=2), axis=0, keepdims=True)
        rows = [jnp.sum(ddt_raw, axis=0, keepdims=True), d_a * a_c, d_d, jnp.zeros((5, R), F32)]
        hpg_ref[...] += jnp.concatenate(rows, axis=0)

    return _pc(body, grid=(G, nb, nc),
               in_specs=[sp["x"], sp["b"], sp["c"], sp["dtc"], sp["dtr"], sp["hpc"], sp["hpr"], sp["hs"], sp["y"]],
               out_specs=[sp["y"], sp["bc"], sp["bc"], sp["dtc"], pl.BlockSpec((None, 8, R), lambda g, b, c: (g, 0, 0))],
               out_shape=[SDS((T, G * RP), F32), SDS((T, G * N), F32), SDS((T, G * N), F32), SDS((G, T, R), F32),
                          SDS((G, 8, R), F32)], name=name,
               scratch=(pltpu.VMEM((N, RP), F32),))(pre, pre, pre, dtc, dtr, hpc, hpr, hs, dy)


def _gate_norm_fwd(y, zview, ng, G, S, name):
    T, DI = y.shape
    zarr, zoff = zview
    gw = DI // G
    tm = _row_tile(S)
    zb = zoff // gw
    assert zoff % gw == 0

    def body(y_ref, z_ref, g_ref, o_ref):
        def strip(r0, carry):
            rows = pl.ds(r0, STRIP)
            yg = y_ref[rows, :] * _silu(z_ref[rows, :])
            r = lax.rsqrt(jnp.mean(yg * yg, axis=-1, keepdims=True) + EPS)
            o_ref[rows, :] = (yg * r * g_ref[...]).astype(BF16)
            return carry

        _strips(tm, strip)

    return _pc(body, grid=(T // tm, G),
               in_specs=[pl.BlockSpec((tm, gw), lambda i, g: (i, g)), pl.BlockSpec((tm, gw), lambda i, g: (i, zb + g)),
                         pl.BlockSpec((1, gw), lambda i, g: (0, g))],
               out_specs=pl.BlockSpec((tm, gw), lambda i, g: (i, g)), out_shape=SDS((T, DI), BF16), name=name)(y, zarr, ng)


def _gate_norm_bwd(y, zview, ng, dyn, G, S, name):
    T, DI = y.shape
    zarr, zoff = zview
    gw = DI // G
    tm = _row_tile(S)
    zb = zoff // gw

    def body(y_ref, z_ref, g_ref, d_ref, dy_ref, dz_ref, dg_ref):
        @pl.when(pl.program_id(1) == 0)
        def _():
            dg_ref[...] = jnp.zeros_like(dg_ref)

        def strip(r0, carry):
            rows = pl.ds(r0, STRIP)
            y_, z, d = y_ref[rows, :], z_ref[rows, :], d_ref[rows, :]
            sz = _silu(z)
            yg = y_ * sz
            r = lax.rsqrt(jnp.mean(yg * yg, axis=-1, keepdims=True) + EPS)
            n = yg * r
            dn = d * g_ref[...]
            dyg = r * (dn - n * jnp.mean(dn * n, axis=-1, keepdims=True))
            dy_ref[rows, :] = dyg * sz
            dz_ref[rows, :] = (dyg * y_ * _dsilu(z)).astype(BF16)
            return carry + _fold8(d * n)

        acc = _strips(tm, strip, jnp.zeros((8, gw), F32))
        dg_ref[...] += _bsum(jnp.sum(acc, axis=0, keepdims=True))

    return _pc(body, grid=(G, T // tm),
               in_specs=[pl.BlockSpec((tm, gw), lambda g, i: (i, g)), pl.BlockSpec((tm, gw), lambda g, i: (i, zb + g)),
                         pl.BlockSpec((1, gw), lambda g, i: (0, g)), pl.BlockSpec((tm, gw), lambda g, i: (i, g))],
               out_specs=[pl.BlockSpec((tm, gw), lambda g, i: (i, g)), pl.BlockSpec((tm, gw), lambda g, i: (i, g)),
                          pl.BlockSpec((8, gw), lambda g, i: (0, g))],
               out_shape=[SDS((T, DI), F32), SDS((T, DI), BF16), SDS((8, DI), F32)], name=name)(y, zarr, ng, dyn)


def _shortconv_fwd(proj, off_b, off_c, off_h, C, w8, S, name):
    K = SC_CONV_K
    _, ob, T, tm, tc, tps = _conv_geom((proj, off_b), C, S)
    oc, oh = off_c // tc, off_h // tc

    def body(b_ref, c_ref, h_ref, cp_ref, hp_ref, w_ref, o_ref, buf):
        first = (pl.program_id(0) % tps) == 0
        buf[0:8, :] = jnp.where(first, 0.0, cp_ref[...] * hp_ref[...])
        buf[8:, :] = c_ref[...] * h_ref[...]

        def strip(r0, carry):
            conv = _taps(buf[pl.ds(r0, STRIP + 8), :], w_ref, K, 8, STRIP)
            o_ref[pl.ds(r0, STRIP), :] = (b_ref[pl.ds(r0, STRIP), :] * conv).astype(BF16)
            return carry

        _strips(tm, strip)

    blk = lambda o: pl.BlockSpec((tm, tc), lambda i, j: (i, o + j))
    return _pc(body, grid=(T // tm, C // tc),
               in_specs=[blk(ob), blk(oc), blk(oh), _prev_spec(tm, tc, oc, "ij"), _prev_spec(tm, tc, oh, "ij"),
                         pl.BlockSpec((8, tc), lambda i, j: (0, j))],
               out_specs=pl.BlockSpec((tm, tc), lambda i, j: (i, j)), out_shape=SDS((T, C), BF16), name=name,
               scratch=(pltpu.VMEM((tm + 8, tc), F32),))(proj, proj, proj, proj, proj, w8)


def _shortconv_bwd(proj, off_b, off_c, off_h, C, w8, dsc, S, name):
    K = SC_CONV_K
    _, ob, T, tm, tc, tps = _conv_geom((proj, off_b), C, S)
    oc, oh = off_c // tc, off_h // tc

    def body(b_ref, c_ref, h_ref, cp_ref, hp_ref, bn_ref, d_ref, dn_ref, w_ref,
             db_ref, dc_ref, dh_ref, dw_ref, buf, buf2):
        i = pl.program_id(1)

        @pl.when(i == 0)
        def _():
            dw_ref[...] = jnp.zeros_like(dw_ref)

        first = (i % tps) == 0
        last = (i % tps) == tps - 1
        buf[0:8, :] = jnp.where(first, 0.0, cp_ref[...] * hp_ref[...])
        buf[8:, :] = c_ref[...] * h_ref[...]
        buf2[0:tm, :] = d_ref[...] * b_ref[...]
        buf2[tm:tm + 8, :] = jnp.where(last, 0.0, dn_ref[...] * bn_ref[...])

        def strip(r0, carry):
            rows = pl.ds(r0, STRIP)
            vwin = buf[pl.ds(r0, STRIP + 8), :]
            db_ref[rows, :] = (d_ref[rows, :] * _taps(vwin, w_ref, K, 8, STRIP)).astype(BF16)
            dwin = buf2[pl.ds(r0, STRIP + 8), :]
            dv = _taps_t(dwin, w_ref, K, STRIP)
            dc_ref[rows, :] = (dv * h_ref[rows, :]).astype(BF16)
            dh_ref[rows, :] = (dv * c_ref[rows, :]).astype(BF16)
            dconv = dwin[0:STRIP]
            sums = [_fold8(dconv * vwin[8 - (K - 1 - k):8 - (K - 1 - k) + STRIP]) for k in range(K)]
            return tuple(c + s for c, s in zip(carry, sums))

        acc = _strips(tm, strip, tuple(jnp.zeros((8, tc), F32) for _ in range(K)))
        dw_ref[...] += _rows8([jnp.sum(a, axis=0, keepdims=True) for a in acc])

    blk = lambda o: pl.BlockSpec((tm, tc), lambda j, i: (i, o + j))
    out = pl.BlockSpec((tm, tc), lambda j, i: (i, j))
    return _pc(body, grid=(C // tc, T // tm),
               in_specs=[blk(ob), blk(oc), blk(oh), _prev_spec(tm, tc, oc, "ji"), _prev_spec(tm, tc, oh, "ji"),
                         _next_spec(T, tm, tc, ob, "ji"), blk(0), _next_spec(T, tm, tc, 0, "ji"),
                         pl.BlockSpec((8, tc), lambda j, i: (0, j))],
               out_specs=[out, out, out, pl.BlockSpec((8, tc), lambda j, i: (0, j))],
               out_shape=[SDS((T, C), BF16)] * 3 + [SDS((8, C), F32)], name=name,
               scratch=(pltpu.VMEM((tm + 8, tc), F32), pltpu.VMEM((tm + 8, tc), F32)))(
                   proj, proj, proj, proj, proj, proj, dsc, dsc, w8)


def _merge_fwd(proj, off_g1, off_g2, y1, y2, S, name):
    T, D = y1.shape
    tm = _row_tile(S)
    o1, o2 = off_g1 // D, off_g2 // D
    assert off_g1 % D == 0 and off_g2 % D == 0

    def body(g1_ref, g2_ref, y1_ref, y2_ref, o_ref):
        def strip(r0, carry):
            rows = pl.ds(r0, STRIP)
            o_ref[rows, :] = (jax.nn.sigmoid(g1_ref[rows, :]) * y1_ref[rows, :]
                              + jax.nn.sigmoid(g2_ref[rows, :]) * y2_ref[rows, :]).astype(BF16)
            return carry

        _strips(tm, strip)

    row = pl.BlockSpec((tm, D), lambda i: (i, 0))
    return _pc(body, grid=(T // tm,),
               in_specs=[pl.BlockSpec((tm, D), lambda i: (i, o1)), pl.BlockSpec((tm, D), lambda i: (i, o2)), row, row],
               out_specs=row, out_shape=SDS((T, D), BF16), name=name)(proj, proj, y1, y2)


def _merge_bwd(proj, off_g1, off_g2, y1, y2, dm, S, name):
    T, D = y1.shape
    tm = _row_tile(S)
    o1, o2 = off_g1 // D, off_g2 // D

    def body(g1_ref, g2_ref, y1_ref, y2_ref, d_ref, dy1_ref, dy2_ref, dg1_ref, dg2_ref):
        def strip(r0, carry):
            rows = pl.ds(r0, STRIP)
            d = d_ref[rows, :]
            s1, s2 = jax.nn.sigmoid(g1_ref[rows, :]), jax.nn.sigmoid(g2_ref[rows, :])
            dy1_ref[rows, :] = (d * s1).astype(BF16)
            dy2_ref[rows, :] = (d * s2).astype(BF16)
            dg1_ref[rows, :] = (d * y1_ref[rows, :] * s1 * (1.0 - s1)).astype(BF16)
            dg2_ref[rows, :] = (d * y2_ref[rows, :] * s2 * (1.0 - s2)).astype(BF16)
            return carry

        _strips(tm, strip)

    row = pl.BlockSpec((tm, D), lambda i: (i, 0))
    return _pc(body, grid=(T // tm,),
               in_specs=[pl.BlockSpec((tm, D), lambda i: (i, o1)), pl.BlockSpec((tm, D), lambda i: (i, o2)), row, row, row],
               out_specs=[row] * 4, out_shape=[SDS((T, D), BF16)] * 4, name=name)(proj, proj, y1, y2, dm)


def _pad8(w):
    return jnp.pad(w, ((0, 8 - w.shape[0]), (0, 0)))


def _dims(w):
    D = w["mix_pre_g"].shape[-1]
    DI = w["ssd_norm_g"].shape[-1]
    H = w["ssd_dt_bias"].shape[-1]
    conv_dim = w["ssd_conv_b"].shape[-1]
    G = (conv_dim - DI) // (2 * N_STATE)
    F = w["w_down"].shape[0]
    return dict(D=D, DI=DI, H=H, P=DI // H, G=G, R=H // G, GN=G * N_STATE, CD=conv_dim, F=F)


def _proj_layout(d):
    D, DI, CD, H = d["D"], d["DI"], d["CD"], d["H"]
    o = dict(z=0, xbc=DI, scb=DI + CD, scc=DI + CD + D, sch=DI + CD + 2 * D, g1=DI + CD + 3 * D, g2=DI + CD + 4 * D,
             dt=DI + CD + 5 * D)
    o["sb"] = math.gcd(SEG_BLK, D, DI, d["GN"])
    assert o["sb"] % 128 == 0 and H <= o["sb"]
    o["np"] = o["dt"] + o["sb"]
    return o


def _glu_perm(a, F, inverse=False):
    lead = a.shape[:-1]
    nb = F // GLU_W
    if not inverse:
        return a.reshape(*lead, 2, nb, GLU_W).swapaxes(-3, -2).reshape(*lead, 2 * F)
    return a.reshape(*lead, nb, 2, GLU_W).swapaxes(-3, -2).reshape(*lead, 2 * F)


def _glu_perm_rows(a, F, inverse=False):
    nb, D = F // GLU_W, a.shape[1]
    shape = (nb, 2, GLU_W, D) if inverse else (2, nb, GLU_W, D)
    return a.reshape(shape).swapaxes(0, 1).reshape(2 * F, D)


def _prep_layer(w):
    d = _dims(w)
    D, DI, CD, H, G, R, F = d["D"], d["DI"], d["CD"], d["H"], d["G"], d["R"], d["F"]
    lay = _proj_layout(d)
    w_in = w["w_in"]
    used = lay["dt"] + H
    wcat = jnp.concatenate([w_in[:DI + CD], w_in[DI + CD + H:], w_in[DI + CD:DI + CD + H],
                            jnp.zeros((lay["np"] - used, D), w_in.dtype)], axis=0)
    hp = jnp.stack([w["ssd_dt_bias"], w["ssd_a_log"], w["ssd_d"]], 0).astype(F32)
    hpc = jnp.pad(hp.reshape(3, G, R).transpose(1, 0, 2), ((0, 0), (0, 5), (0, 0)))
    hpr = jnp.pad(hp[:2].reshape(2, G, R).transpose(1, 2, 0), ((0, 0), (0, 0), (0, 6)))
    row = lambda v: v.reshape(1, -1).astype(F32)
    return dict(
        d=d, lay=lay, ada_w=w["ada_w"].astype(BF16), ada_b=row(w["ada_b"]),
        mix_pre_g=row(w["mix_pre_g"]), mix_post_g=row(w["mix_post_g"]), wcat=wcat.astype(BF16),
        ssd_conv_w=_pad8(w["ssd_conv_w"].astype(F32)), ssd_conv_b=row(w["ssd_conv_b"]), hpc=hpc, hpr=hpr,
        ssd_norm_g=row(w["ssd_norm_g"]), w_ssd_out=w["w_ssd_out"].astype(BF16),
        sc_conv_w=_pad8(w["sc_conv_w"].astype(F32)), w_sc_out=w["w_sc_out"].astype(BF16), w_o=w["w_o"].astype(BF16),
        ffn_pre_g=row(w["ffn_pre_g"]), ffn_post_g=row(w["ffn_post_g"]),
        w_up=_glu_perm_rows(w["w_up"], F).astype(BF16), ffn_conv_w=_pad8(_glu_perm(w["ffn_conv_w"].astype(F32), F)),
        ffn_conv_b=_glu_perm(row(w["ffn_conv_b"]), F), w_down=w["w_down"].astype(BF16))


def _dt_layouts(proj, lay, d):
    T = proj.shape[0]
    dt = proj[:, lay["dt"]:lay["dt"] + d["H"]].reshape(T, d["G"], d["R"])
    return dt.transpose(1, 0, 2), dt.transpose(1, 2, 0)


def _layer_fwd(x, mod3, p, S, li):
    d, lay = p["d"], p["lay"]
    D, DI, G, R, P, GN, CD = d["D"], d["DI"], d["G"], d["R"], d["P"], d["GN"], d["CD"]
    nm = lambda s: f"l{li}_{s}"
    h = _norm_mod(x, p["mix_pre_g"], mod3, 1, 0, S, nm("norm1"))
    proj = _mm(h, p["wcat"], "nt", F32, nm("mm_in"), caps=(1024, 1536, 2048))
    pre = _conv_fwd((proj, lay["xbc"]), CD, p["ssd_conv_w"], p["ssd_conv_b"], SSD_CONV_K, S, nm("ssdconv"))
    dtc, dtr = _dt_layouts(proj, lay, d)
    offs = (0, DI, DI + GN)
    y, hs = _ssd_fwd(pre, offs, dtc, dtr, p["hpc"], p["hpr"], G, R, P, S, nm("ssd"))
    yn = _gate_norm_fwd(y, (proj, lay["z"]), p["ssd_norm_g"], G, S, nm("gnorm"))
    sc = _shortconv_fwd(proj, lay["scb"], lay["scc"], lay["sch"], D, p["sc_conv_w"], S, nm("sconv"))
    y_ssd = _mm(yn, p["w_ssd_out"], "nn", F32, nm("mm_ssdout"))
    y_sc = _mm(sc, p["w_sc_out"], "nn", F32, nm("mm_scout"))
    m = _merge_fwd(proj, lay["g1"], lay["g2"], y_ssd, y_sc, S, nm("merge"))
    mix = _mm(m, p["w_o"], "nn", F32, nm("mm_o"))
    x1 = _resid_post(x, mix, mod3, 2, p["mix_post_g"], S, nm("post1"))
    h2 = _norm_mod(x1, p["ffn_pre_g"], mod3, 4, 3, S, nm("norm2"))
    uu = _mm(h2, p["w_up"], "nt", F32, nm("mm_up"), caps=(1024, 1408, 2048))
    a = _ffn_act_fwd(uu, p["ffn_conv_w"], p["ffn_conv_b"], S, nm("ffnact"))
    f = _mm(a, p["w_down"], "nn", F32, nm("mm_down"), caps=(1024, 1024, 1408))
    x2 = _resid_post(x1, f, mod3, 5, p["ffn_post_g"], S, nm("post2"))
    saved = dict(x=x, h=h, proj=proj, pre=pre, dtc=dtc, dtr=dtr, y=y, hs=hs, yn=yn, sc=sc, y_ssd=y_ssd, y_sc=y_sc,
                 m=m, mix=mix, x1=x1, h2=h2, uu=uu, a=a, f=f)
    return x2, saved


def _seq_sum(acc, nb):
    return acc.reshape(nb, 8, -1)[:, 0, :]


def _layer_bwd(dx2, mod3, p, s, S, li):
    d, lay = p["d"], p["lay"]
    D, DI, G, R, P, GN, CD, H, F = d["D"], d["DI"], d["G"], d["R"], d["P"], d["GN"], d["CD"], d["H"], d["F"]
    nb = dx2.shape[0] // S
    nm = lambda t: f"l{li}_{t}"
    g = {}
    df, dgt2, dpg2 = _post_bwd(s["f"], mod3, 5, p["ffn_post_g"], dx2, S, nm("post2_b"))
    g["ffn_post_g"] = dpg2[0]
    da = _mm(df, p["w_down"], "nt", BF16, nm("mm_down_bi"), caps=(1024, 1408, 2048))
    g["w_down"] = _mm(s["a"], df, "tn", WGRAD,nm("mm_down_bw"), caps=(1408, 1024, 1024))
    duu, cw = _ffn_act_bwd(s["uu"], da, p["ffn_conv_w"], p["ffn_conv_b"], S, nm("ffnact_b"))
    g["ffn_conv_w"] = _glu_perm(cw[:FFN_CONV_K], F, inverse=True)
    g["ffn_conv_b"] = _glu_perm(cw[FFN_CONV_K], F, inverse=True)
    dh2 = _mm(duu, p["w_up"], "nn", F32, nm("mm_up_bi"), caps=(1024, 1024, 2816))
    g["w_up"] = _glu_perm_rows(_mm(duu, s["h2"], "tn", WGRAD,nm("mm_up_bw"), caps=(1408, 1024, 1024)), F, inverse=True)
    dx1, dg2, dsc2, dsh2 = _pre_bwd(s["x1"], p["ffn_pre_g"], mod3, 4, dh2, dx2, S, nm("norm2_b"))
    g["ffn_pre_g"] = dg2[0]
    dmix, dgt1, dpg1 = _post_bwd(s["mix"], mod3, 2, p["mix_post_g"], dx1, S, nm("post1_b"))
    g["mix_post_g"] = dpg1[0]
    dm = _mm(dmix, p["w_o"], "nt", F32, nm("mm_o_bi"))
    g["w_o"] = _mm(s["m"], dmix, "tn", WGRAD,nm("mm_o_bw"))
    proj = s["proj"]
    dy_ssd, dy_sc, dg1, dg2_ = _merge_bwd(proj, lay["g1"], lay["g2"], s["y_ssd"], s["y_sc"], dm, S, nm("merge_b"))
    dyn = _mm(dy_ssd, p["w_ssd_out"], "nt", F32, nm("mm_ssdout_bi"))
    g["w_ssd_out"] = _mm(s["yn"], dy_ssd, "tn", WGRAD,nm("mm_ssdout_bw"))
    dsc = _mm(dy_sc, p["w_sc_out"], "nt", F32, nm("mm_scout_bi"))
    g["w_sc_out"] = _mm(s["sc"], dy_sc, "tn", WGRAD,nm("mm_scout_bw"))
    dscb, dscc, dsch, scw = _shortconv_bwd(proj, lay["scb"], lay["scc"], lay["sch"], D, p["sc_conv_w"], dsc, S, nm("sconv_b"))
    g["sc_conv_w"] = scw[:SC_CONV_K]
    dy, dz, dng = _gate_norm_bwd(s["y"], (proj, lay["z"]), p["ssd_norm_g"], dyn, G, S, nm("gnorm_b"))
    g["ssd_norm_g"] = dng[0]
    offs = (0, DI, DI + GN)
    dpx, dpb, dpc, ddt, hpg = _ssd_bwd(s["pre"], offs, s["dtc"], s["dtr"], p["hpc"], p["hpr"], s["hs"], dy,
                                       G, R, P, S, nm("ssd_b"))
    g["ssd_dt_bias"], g["ssd_a_log"], g["ssd_d"] = hpg[:, 0, :].reshape(H), hpg[:, 1, :].reshape(H), hpg[:, 2, :].reshape(H)
    cws, dxbc = [], []
    for name, darr, off, C in (("x", dpx, 0, DI), ("b", dpb, DI, GN), ("c", dpc, DI + GN, GN)):
        w8 = p["ssd_conv_w"][:, off:off + C]
        cws.append(_conv_bwd_w((darr, 0), (proj, lay["xbc"] + off), C, SSD_CONV_K, S, nm(f"ssdconv_bw_{name}")))
        dxbc.append(_conv_bwd_in((darr, 0), C, w8, SSD_CONV_K, S, BF16, nm(f"ssdconv_bi_{name}")))
    cws = jnp.concatenate(cws, axis=1)
    g["ssd_conv_w"], g["ssd_conv_b"] = cws[:SSD_CONV_K], cws[SSD_CONV_K]
    T = dx2.shape[0]
    ddt_t = jnp.pad(ddt.transpose(1, 0, 2).reshape(T, H).astype(BF16), ((0, 0), (0, lay["sb"] - H)))
    dproj = [dz] + dxbc + [dscb, dscc, dsch, dg1, dg2_, ddt_t]
    dh = _mm_seg(dproj, p["wcat"], "nn", F32, nm("mm_in_bi"), lay["sb"])
    dwcat = _mm_seg(dproj, s["h"], "tn", WGRAD, nm("mm_in_bw"), lay["sb"], tk=1024)
    o = lay
    g["w_in"] = jnp.concatenate([dwcat[o["z"]:o["scb"]], dwcat[o["dt"]:o["dt"] + H], dwcat[o["scb"]:o["dt"]]], axis=0)
    dx, dg1_, dsc1, dsh1 = _pre_bwd(s["x"], p["mix_pre_g"], mod3, 1, dh, dx1, S, nm("norm1_b"))
    g["mix_pre_g"] = dg1_[0]
    dmod = jnp.concatenate([_seq_sum(t, nb) for t in (dsh1, dsc1, dgt1, dsh2, dsc2, dgt2)], axis=1)
    return dx, dmod, g


def _fwd_bwd(x3, c, target3, layers):
    nb, S, D = x3.shape
    T = nb * S
    x = x3.reshape(T, D)
    c8 = jnp.pad(c, ((0, MOD_ROWS - nb), (0, 0)))
    preps = [_prep_layer(w) for w in layers]
    saved, mods, cact = [], [], None
    for li, p in enumerate(preps):
        mod, cact = _modk(c8, p["ada_w"], p["ada_b"], f"l{li}_mod")
        mod3 = mod[:nb].reshape(nb, 1, 6 * D)
        x, s = _layer_fwd(x, mod3, p, S, li)
        saved.append(s)
        mods.append(mod3)
    dy, lacc = _loss(x, target3.reshape(T, D), S, "loss")
    grads = [None] * len(preps)
    for li in reversed(range(len(preps))):
        dy, dmod, g = _layer_bwd(dy, mods[li], preps[li], saved[li], S, li)
        dmod8 = jnp.pad(dmod, ((0, MOD_ROWS - nb), (0, 0)))
        g["ada_b"] = _colsum(dmod8, f"l{li}_adab")
        g["ada_w"] = _mm(dmod8, cact, "tn", WGRAD,f"l{li}_mm_ada_bw", caps=(1536, 1024, 2048))
        grads[li] = g
    return lacc[0, 0], dy.reshape(nb, S, D), grads


def _colsum(a8, name):
    rows, C = a8.shape
    tc = _tile(C, 2048)

    def body(a_ref, o_ref):
        o_ref[...] = _bsum(jnp.sum(a_ref[...], axis=0, keepdims=True))

    return _pc(body, grid=(C // tc,), in_specs=[pl.BlockSpec((rows, tc), lambda j: (0, j))],
               out_specs=pl.BlockSpec((8, tc), lambda j: (0, j)), out_shape=SDS((8, C), F32), name=name)(a8)[0]


def _adam(gs, w, m, v, name):
    ns, R, W = gs.shape
    tr = _tile(R, 256, 8)

    def body(g_ref, w_ref, m_ref, v_ref, go_ref, d_ref, mo_ref, vo_ref):
        g = g_ref[0].astype(F32)
        for k in range(1, ns):
            g = g + g_ref[k].astype(F32)
        go_ref[...] = g
        d_ref[...], mo_ref[...], vo_ref[...] = _adam_update(g, w_ref[...], m_ref[...], v_ref[...])

    row = pl.BlockSpec((tr, W), lambda i: (i, 0))
    return _pc(body, grid=(R // tr,), in_specs=[pl.BlockSpec((ns, tr, W), lambda i: (0, i, 0)), row, row, row],
               out_specs=[row] * 4, out_shape=[SDS((R, W), F32)] * 4, name=name)(gs, w, m, v)


def _adam_update(g, w, m, v):
    c1 = 1.0 / (1.0 - ADAM_B1 ** ADAM_STEP)
    c2 = 1.0 / (1.0 - ADAM_B2 ** ADAM_STEP)
    m_ = ADAM_B1 * m + (1.0 - ADAM_B1) * g
    v_ = ADAM_B2 * v + (1.0 - ADAM_B2) * (g * g)
    return -ADAM_LR * ((m_ * c1) / (jnp.sqrt(v_ * c2) + ADAM_EPS) + ADAM_WD * w), m_, v_


def _adam_nat(g, w, m, v, name):
    depth, a, b = w.shape
    tr = _tile(a, 256, 8)

    def body(g_ref, w_ref, m_ref, v_ref, d_ref, mo_ref, vo_ref):
        d_ref[...], mo_ref[...], vo_ref[...] = _adam_update(g_ref[...], w_ref[...], m_ref[...], v_ref[...])

    blk = pl.BlockSpec((None, tr, b), lambda l, i: (l, i, 0))
    return _pc(body, grid=(depth, a // tr), in_specs=[blk] * 4, out_specs=[blk] * 3,
               out_shape=[SDS(w.shape, F32)] * 3, name=name)(g, w, m, v)


def _sum_chips(gs, name):
    ns, R, W = gs.shape
    tr = _tile(R, 256, 16)

    def body(g_ref, o_ref):
        acc = g_ref[0].astype(F32)
        for k in range(1, ns):
            acc = acc + g_ref[k].astype(F32)
        o_ref[...] = acc

    return _pc(body, grid=(R // tr,), in_specs=[pl.BlockSpec((ns, tr, W), lambda i: (0, i, 0))],
               out_specs=pl.BlockSpec((tr, W), lambda i: (i, 0)), out_shape=SDS((R, W), F32), name=name)(gs)


HBM_SPEC = pl.BlockSpec(memory_space=pltpu.HBM)
VMEM_SPEC = pl.BlockSpec(memory_space=pltpu.VMEM)


def _dev():
    return lax.axis_index("x"), lax.axis_index("y"), lax.axis_index("c")


def _allgather_big(loc, name):
    R, W = loc.shape

    def body(x_ref, out_ref, send_sems, recv_sems, local_sem):
        x, y, c = _dev()
        me, sibling = (x, y, c), (x, y, 1 - c)
        chips = [(1 - x, y), (x, 1 - y), (1 - x, 1 - y)]

        def slab(px, py, pc):
            return out_ref.at[4 * px + 2 * py + pc]

        def copy(k, block, to, src=None):
            return pltpu.make_async_remote_copy(
                src_ref=slab(*block) if src is None else src, dst_ref=slab(*block),
                send_sem=send_sems.at[k], recv_sem=recv_sems.at[k], device_id=to, device_id_type=MESH)

        mine = pltpu.make_async_copy(x_ref, slab(*me), local_sem)
        mine.start()
        first = [copy(0, me, sibling, src=x_ref)]
        first += [copy(1 + j, me, (*chip, c), src=x_ref) for j, chip in enumerate(chips)]
        for cp in first:
            cp.start()
        passed = [copy(4 + j, (*chip, c), sibling) for j, chip in enumerate(chips)]
        for j, chip in enumerate(chips):
            copy(1 + j, (*chip, c), me).wait_recv()
            passed[j].start()
        copy(0, sibling, me).wait_recv()
        for j, chip in enumerate(chips):
            copy(4 + j, (*chip, 1 - c), me).wait_recv()
        for cp in first + passed:
            cp.wait_send()
        mine.wait()

    return pl.pallas_call(
        body, out_shape=SDS((N_DEV, R, W), loc.dtype), in_specs=[HBM_SPEC], out_specs=HBM_SPEC,
        scratch_shapes=[pltpu.SemaphoreType.DMA((7,)), pltpu.SemaphoreType.DMA((7,)), pltpu.SemaphoreType.DMA],
        name=name)(loc)


def _rs_pair_exchange(g, name):
    nd, R, W = g.shape
    nj = nd // 2

    def body(g_ref, out_ref, send_sems, recv_sems):
        x, y, c = _dev()
        cps = [pltpu.make_async_remote_copy(src_ref=g_ref.at[2 * j + (1 - c)], dst_ref=out_ref.at[j],
                                            send_sem=send_sems.at[j], recv_sem=recv_sems.at[j],
                                            device_id=(x, y, 1 - c), device_id_type=MESH) for j in range(nj)]
        for cp in cps:
            cp.start()
        for cp in cps:
            cp.wait()

    return pl.pallas_call(
        body, out_shape=SDS((nj, R, W), g.dtype), in_specs=[HBM_SPEC], out_specs=HBM_SPEC,
        scratch_shapes=[pltpu.SemaphoreType.DMA((nj,)), pltpu.SemaphoreType.DMA((nj,))], name=name)(g)


def _add_pairs(g, ra, name):
    nd, R, W = g.shape
    nj = nd // 2
    tr = _tile(R, 256, 8)
    cidx = lax.axis_index("c").astype(jnp.int32).reshape(1)

    def body(c_ref, a_ref, b_ref, o_ref):
        o_ref[...] = (a_ref[...].astype(F32) + b_ref[...].astype(F32)).astype(o_ref.dtype)

    gs = pltpu.PrefetchScalarGridSpec(
        num_scalar_prefetch=1, grid=(nj, R // tr),
        in_specs=[pl.BlockSpec((None, tr, W), lambda j, i, cr: (2 * j + cr[0], i, 0)),
                  pl.BlockSpec((None, tr, W), lambda j, i, cr: (j, i, 0))],
        out_specs=pl.BlockSpec((None, tr, W), lambda j, i, cr: (j, i, 0)))
    return pl.pallas_call(body, grid_spec=gs, out_shape=SDS((nj, R, W), g.dtype), name=name,
                          compiler_params=pltpu.CompilerParams(vmem_limit_bytes=VMEM_LIMIT))(cidx, g, ra)


def _rs_chip_exchange(p, name):
    nj, R, W = p.shape

    def body(p_ref, out_ref, send_sems, recv_sems, local_sem):
        x, y, c = _dev()
        j0 = 2 * x + y
        chips = [(1 - x, y), (x, 1 - y), (1 - x, 1 - y)]
        mine = pltpu.make_async_copy(p_ref.at[j0], out_ref.at[j0], local_sem)
        mine.start()

        def copy(k, chip):
            return pltpu.make_async_remote_copy(
                src_ref=p_ref.at[2 * chip[0] + chip[1]], dst_ref=out_ref.at[j0],
                send_sem=send_sems.at[k], recv_sem=recv_sems.at[k], device_id=(*chip, c), device_id_type=MESH)

        sent = [copy(k, chip) for k, chip in enumerate(chips)]
        for cp in sent:
            cp.start()
        for k, chip in enumerate(chips):
            pltpu.make_async_remote_copy(
                src_ref=p_ref.at[j0], dst_ref=out_ref.at[2 * chip[0] + chip[1]],
                send_sem=send_sems.at[k], recv_sem=recv_sems.at[k], device_id=(*chip, c), device_id_type=MESH).wait_recv()
        for cp in sent:
            cp.wait_send()
        mine.wait()

    return pl.pallas_call(
        body, out_shape=SDS((nj, R, W), p.dtype), in_specs=[HBM_SPEC], out_specs=HBM_SPEC,
        scratch_shapes=[pltpu.SemaphoreType.DMA((3,)), pltpu.SemaphoreType.DMA((3,)), pltpu.SemaphoreType.DMA],
        name=name)(p)


def _allgather_small(v, name):
    R, W = v.shape

    def body(v_ref, out_ref, send_sems, recv_sems, local_sem):
        x, y, c = _dev()
        mine = pltpu.make_async_copy(v_ref, out_ref.at[4 * x + 2 * y + c], local_sem)
        mine.start()
        peers = []
        for k in range(1, N_DEV):
            px = 1 - x if k & 4 else x
            py = 1 - y if k & 2 else y
            pc_ = 1 - c if k & 1 else c
            peers.append((px, py, pc_))
        sent = [pltpu.make_async_remote_copy(
            src_ref=v_ref, dst_ref=out_ref.at[4 * x + 2 * y + c], send_sem=send_sems.at[k], recv_sem=recv_sems.at[k],
            device_id=peer, device_id_type=MESH) for k, peer in enumerate(peers)]
        for cp in sent:
            cp.start()
        for k, (px, py, pc_) in enumerate(peers):
            pltpu.make_async_remote_copy(
                src_ref=v_ref, dst_ref=out_ref.at[4 * px + 2 * py + pc_], send_sem=send_sems.at[k],
                recv_sem=recv_sems.at[k], device_id=(px, py, pc_), device_id_type=MESH).wait_recv()
        for cp in sent:
            cp.wait_send()
        mine.wait()

    return pl.pallas_call(
        body, out_shape=SDS((N_DEV, R, W), v.dtype), in_specs=[VMEM_SPEC], out_specs=VMEM_SPEC,
        scratch_shapes=[pltpu.SemaphoreType.DMA((7,)), pltpu.SemaphoreType.DMA((7,)), pltpu.SemaphoreType.DMA],
        name=name)(v)


def _sum_slabs(a, name):
    ns, R, W = a.shape

    def body(a_ref, o_ref):
        acc = a_ref[0]
        for k in range(1, ns):
            acc = acc + a_ref[k]
        o_ref[...] = acc

    return pl.pallas_call(body, out_shape=SDS((R, W), a.dtype), in_specs=[VMEM_SPEC], out_specs=VMEM_SPEC, name=name)(a)


BIG = (("ada_w", "col"), ("w_in", "col"), ("w_ssd_out", "row"), ("w_sc_out", "row"), ("w_o", "row"), ("w_up", "col"),
       ("w_down", "row"))
CONVW = ("ssd_conv_w", "sc_conv_w", "ffn_conv_w")
REPL = ("ada_b", "mix_pre_g", "mix_post_g", "ssd_conv_b", "ssd_dt_bias", "ssd_a_log", "ssd_d", "ssd_norm_g", "ffn_pre_g",
        "ffn_post_g", "ffn_conv_b")
WEIGHTS = ("ada_w", "ada_b", "mix_pre_g", "mix_post_g", "w_in", "ssd_conv_w", "ssd_conv_b", "ssd_dt_bias", "ssd_a_log",
           "ssd_d", "ssd_norm_g", "w_ssd_out", "sc_conv_w", "w_sc_out", "w_o", "ffn_pre_g", "ffn_post_g", "w_up",
           "ffn_conv_w", "ffn_conv_b", "w_down")


def _pad_rows(a, mult):
    r = a.shape[-2]
    pad = -r % mult
    return a if pad == 0 else jnp.pad(a, [(0, 0)] * (a.ndim - 2) + [(0, pad), (0, 0)])


def _flat_rows(parts, mult):
    flat = jnp.concatenate([p.reshape(-1) for p in parts])
    flat = jnp.pad(flat, (0, -flat.shape[0] % ROW_W))
    return _pad_rows(flat.reshape(-1, ROW_W), mult)


def _unflat(buf, shapes):
    flat = buf.reshape(-1)
    out, o = [], 0
    for shp in shapes:
        n = 1
        for s in shp:
            n *= s
        out.append(flat[o:o + n].reshape(shp))
        o += n
    return out


def _pack_big_local(get, depth):
    return jnp.concatenate([_pad_rows((get(n)[l].T if kind == "col" else get(n)[l]).reshape(-1, ROW_W), SLAB_ALIGN)
                            for l in range(depth) for n, kind in BIG], axis=0)


def _big_rows(shapes, depth):
    out, o = {}, 0
    for l in range(depth):
        for n, _ in BIG:
            r = shapes[n][1] * shapes[n][2] // ROW_W
            out[(l, n)] = (o, o + r)
            o += -(-r // SLAB_ALIGN) * SLAB_ALIGN
    return out, o


def kernel(x, c, ada_w, ada_b, mix_pre_g, mix_post_g, w_in, ssd_conv_w, ssd_conv_b, ssd_dt_bias, ssd_a_log, ssd_d, ssd_norm_g, w_ssd_out, sc_conv_w, w_sc_out, w_o, ffn_pre_g, ffn_post_g, w_up, ffn_conv_w, ffn_conv_b, w_down, loss_target, m_ada_w, m_ada_b, m_mix_pre_g, m_mix_post_g, m_w_in, m_ssd_conv_w, m_ssd_conv_b, m_ssd_dt_bias, m_ssd_a_log, m_ssd_d, m_ssd_norm_g, m_w_ssd_out, m_sc_conv_w, m_w_sc_out, m_w_o, m_ffn_pre_g, m_ffn_post_g, m_w_up, m_ffn_conv_w, m_ffn_conv_b, m_w_down, v_ada_w, v_ada_b, v_mix_pre_g, v_mix_post_g, v_w_in, v_ssd_conv_w, v_ssd_conv_b, v_ssd_dt_bias, v_ssd_a_log, v_ssd_d, v_ssd_norm_g, v_w_ssd_out, v_sc_conv_w, v_w_sc_out, v_w_o, v_ffn_pre_g, v_ffn_post_g, v_w_up, v_ffn_conv_w, v_ffn_conv_b, v_w_down):
    wl = dict(zip(WEIGHTS, (ada_w, ada_b, mix_pre_g, mix_post_g, w_in, ssd_conv_w, ssd_conv_b, ssd_dt_bias, ssd_a_log,
                            ssd_d, ssd_norm_g, w_ssd_out, sc_conv_w, w_sc_out, w_o, ffn_pre_g, ffn_post_g, w_up,
                            ffn_conv_w, ffn_conv_b, w_down)))
    ml = dict(zip(WEIGHTS, (m_ada_w, m_ada_b, m_mix_pre_g, m_mix_post_g, m_w_in, m_ssd_conv_w, m_ssd_conv_b,
                            m_ssd_dt_bias, m_ssd_a_log, m_ssd_d, m_ssd_norm_g, m_w_ssd_out, m_sc_conv_w, m_w_sc_out, m_w_o,
                            m_ffn_pre_g, m_ffn_post_g, m_w_up, m_ffn_conv_w, m_ffn_conv_b, m_w_down)))
    vl = dict(zip(WEIGHTS, (v_ada_w, v_ada_b, v_mix_pre_g, v_mix_post_g, v_w_in, v_ssd_conv_w, v_ssd_conv_b,
                            v_ssd_dt_bias, v_ssd_a_log, v_ssd_d, v_ssd_norm_g, v_w_ssd_out, v_sc_conv_w, v_w_sc_out, v_w_o,
                            v_ffn_pre_g, v_ffn_post_g, v_w_up, v_ffn_conv_w, v_ffn_conv_b, v_w_down)))
    depth = ada_w.shape[0]
    shapes = {n: wl[n].shape for n in WEIGHTS}
    me = 4 * lax.axis_index("x") + 2 * lax.axis_index("y") + lax.axis_index("c")

    rows, n_big = _big_rows(shapes, depth)
    big_loc = _pack_big_local(lambda n: wl[n].astype(BF16), depth)
    conv_flat = jnp.concatenate([wl[n][l].reshape(-1) for l in range(depth) for n in CONVW])
    n_conv = conv_flat.shape[0]
    conv_flat = jnp.pad(conv_flat, (0, -n_conv % (ROW_W // 2)))
    conv_rows = lax.bitcast_convert_type(conv_flat, BF16).reshape(-1, ROW_W)
    gathered = _allgather_big(_pad_rows(jnp.concatenate([big_loc, conv_rows], axis=0), ROW_PAD), "allgather_weights")
    conv_all = lax.bitcast_convert_type(
        gathered[:, n_big:n_big + conv_rows.shape[0]].reshape(N_DEV, -1, 2), F32)[:, :n_conv]
    conv_full, o = {}, 0
    for l in range(depth):
        for n in CONVW:
            k, cl = shapes[n][1], shapes[n][2]
            conv_full[(l, n)] = conv_all[:, o:o + k * cl].reshape(N_DEV, k, cl).transpose(1, 0, 2).reshape(k, N_DEV * cl)
            o += k * cl
    layers = []
    for l in range(depth):
        w = {n: wl[n][l] for n in REPL}
        for n, kind in BIG:
            r0, r1 = rows[(l, n)]
            a, b = shapes[n][1], shapes[n][2]
            blk = gathered[:, r0:r1]
            w[n] = blk.reshape(N_DEV * b, a) if kind == "col" else blk.reshape(N_DEV * a, b)
        for n in CONVW:
            w[n] = conv_full[(l, n)]
        layers.append(w)

    loss_loc, dx, grads = _fwd_bwd(x, c, loss_target, layers)

    slabs = []
    for l in range(depth):
        for n, kind in BIG:
            slabs.append(_pad_rows(grads[l][n].astype(BF16).reshape(N_DEV, -1, ROW_W), SLAB_ALIGN))
    used_rows = sum(s.shape[1] for s in slabs)
    slabs.append(jnp.zeros((N_DEV, -used_rows % ROW_PAD, ROW_W), BF16))
    gslab = jnp.concatenate(slabs, axis=1)
    from_sibling = _rs_pair_exchange(gslab, "rs_pair_exchange")
    chip_sums = _add_pairs(gslab, from_sibling, "rs_pair_add")
    from_chips = _rs_chip_exchange(chip_sums, "rs_chip_exchange")
    g_sum = _sum_chips(from_chips, "rs_chip_sum")

    def grad_of(l, n, kind):
        blk = g_sum[rows[(l, n)][0]:rows[(l, n)][1]]
        a, b = shapes[n][1], shapes[n][2]
        return blk.reshape(b, a).T if kind == "col" else blk.reshape(a, b)

    g_big = {n: jnp.stack([grad_of(l, n, kind) for l in range(depth)]) for n, kind in BIG}
    d_big, m_big, v_big = {}, {}, {}
    for n, _ in BIG:
        d_big[n], m_big[n], v_big[n] = _adam_nat(g_big[n], wl[n], ml[n], vl[n], f"adam_{n}")

    parts = [jnp.broadcast_to(loss_loc, (ROW_W,))]
    small_shapes = [(ROW_W,)]
    for l in range(depth):
        for n in REPL + CONVW:
            parts.append(grads[l][n])
            small_shapes.append(tuple(grads[l][n].shape))
    total = _sum_slabs(_allgather_small(_flat_rows(parts, 8), "allgather_small"), "sum_small")
    pieces = _unflat(total, small_shapes)
    loss = pieces[0][0]
    g_small, i = {}, 1
    for l in range(depth):
        for n in REPL + CONVW:
            gp = pieces[i]
            i += 1
            if n in CONVW:
                gp = lax.dynamic_slice_in_dim(gp, me * shapes[n][2], shapes[n][2], axis=1)
            g_small[(l, n)] = gp
    order = [(l, n) for l in range(depth) for n in REPL + CONVW]
    loc_shapes = [tuple(shapes[n][1:]) for _, n in order]
    packs = lambda f: _flat_rows([f(l, n) for l, n in order], 8)
    gs_small = packs(lambda l, n: g_small[(l, n)])
    _, d_sm, m_sm, v_sm = _adam(gs_small[None], packs(lambda l, n: wl[n][l]), packs(lambda l, n: ml[n][l]),
                                packs(lambda l, n: vl[n][l]), "adam_small")

    def unpack_small(buf):
        ps = _unflat(buf, loc_shapes)
        return {n: jnp.stack([ps[order.index((l, n))] for l in range(depth)]) for n in REPL + CONVW}

    outs = []
    for big, small in ((g_big, {n: jnp.stack([g_small[(l, n)] for l in range(depth)]) for n in REPL + CONVW}),
                       (d_big, unpack_small(d_sm)), (m_big, unpack_small(m_sm)), (v_big, unpack_small(v_sm))):
        merged = {**big, **small}
        outs += [merged[n] for n in WEIGHTS]
    return (loss, dx, *outs)
```

```python
import math

import jax
import jax.numpy as jnp
from jax import lax
from jax.experimental import pallas as pl
from jax.experimental.pallas import tpu as pltpu

F32, BF16 = jnp.float32, jnp.bfloat16
WGRAD = BF16
SDS = jax.ShapeDtypeStruct
MESH = pl.DeviceIdType.MESH

EPS = 1e-6
N_STATE = 128
CHUNK = 128
SSD_CONV_K, SC_CONV_K, FFN_CONV_K = 4, 3, 3
N_DEV = 8
ROW_W = 1024
ROW_PAD = 256
SLAB_ALIGN = 16
SEG_BLK = 512
STRIP = 32
FFN_STRIP = 64
GLU_W = 256
MOD_ROWS = 128
VMEM_LIMIT = 48 * 2**20

ADAM_LR, ADAM_B1, ADAM_B2, ADAM_EPS, ADAM_WD, ADAM_STEP = 0.001, 0.9, 0.999, 1e-08, 0.01, 10

NT = (((1,), (1,)), ((), ()))
TN = (((0,), (0,)), ((), ()))
NN = (((1,), (0,)), ((), ()))


def _tile(n, cap, mult=128):
    best = None
    for t in range(mult, min(n, cap) + 1, mult):
        if n % t == 0:
            best = t
    return best if best is not None else n


def _pc(body, *, grid, in_specs, out_specs, out_shape, name, scratch=()):
    return pl.pallas_call(
        body, grid=grid, in_specs=in_specs, out_specs=out_specs, out_shape=out_shape,
        scratch_shapes=list(scratch), name=name,
        compiler_params=pltpu.CompilerParams(
            dimension_semantics=("arbitrary",) * len(grid), vmem_limit_bytes=VMEM_LIMIT))


def _silu(x):
    return x * jax.nn.sigmoid(x)


def _dsilu(x):
    s = jax.nn.sigmoid(x)
    return s * (1.0 + x * (1.0 - s))


def _softplus(x):
    return jnp.maximum(x, 0.0) + jnp.log(1.0 + jnp.exp(-jnp.abs(x)))


def _dot(a, b, dims=NN):
    return lax.dot_general(a, b, dims, preferred_element_type=F32)


def _bsum(v, rows=8):
    return jnp.broadcast_to(v, (rows, v.shape[1]))


def _mm(a, b, mode, out_dtype, name, caps=(1024, 1024, 2048)):
    if mode == "nn":
        (M, K), (K2, N) = a.shape, b.shape
    elif mode == "nt":
        (M, K), (N, K2) = a.shape, b.shape
    else:
        (K, M), (K2, N) = a.shape, b.shape
    assert K == K2, (a.shape, b.shape, mode)
    tm, tn, tk = _tile(M, caps[0]), _tile(N, caps[1]), _tile(K, caps[2])
    nk = K // tk
    dims = {"nn": NN, "nt": NT, "tn": TN}[mode]
    if mode == "tn":
        a_spec = pl.BlockSpec((tk, tm), lambda i, j, k: (k, i))
    else:
        a_spec = pl.BlockSpec((tm, tk), lambda i, j, k: (i, k))
    if mode == "nt":
        b_spec = pl.BlockSpec((tn, tk), lambda i, j, k: (j, k))
    else:
        b_spec = pl.BlockSpec((tk, tn), lambda i, j, k: (k, j))

    def body(a_ref, b_ref, o_ref, *acc):
        part = _dot(a_ref[...].astype(BF16), b_ref[...].astype(BF16), dims)
        if nk == 1:
            o_ref[...] = part.astype(o_ref.dtype)
        else:
            acc_ref, = acc
            k = pl.program_id(2)

            @pl.when(k == 0)
            def _():
                acc_ref[...] = part

            @pl.when(k > 0)
            def _():
                acc_ref[...] += part

            @pl.when(k == nk - 1)
            def _():
                o_ref[...] = acc_ref[...].astype(o_ref.dtype)

    return _pc(body, grid=(M // tm, N // tn, nk), in_specs=[a_spec, b_spec],
               out_specs=pl.BlockSpec((tm, tn), lambda i, j, k: (i, j)),
               out_shape=SDS((M, N), out_dtype), name=name,
               scratch=() if nk == 1 else (pltpu.VMEM((tm, tn), F32),))(a, b)


def _mm_seg(segs, b, mode, out_dtype, name, blk, tile=1024, tk=2048):
    nblk = [a.shape[1] // blk for a in segs]
    assert all(a.shape[1] % blk == 0 for a in segs)
    start = [sum(nblk[:s]) for s in range(len(segs))]
    total = sum(nblk)
    ns = len(segs)
    N = b.shape[1]
    tn = _tile(N, tile)
    if mode == "nn":
        M = segs[0].shape[0]
        tm = _tile(M, tile)
        grid = (M // tm, N // tn, total)
        a_specs = [pl.BlockSpec((tm, blk), lambda i, j, k, k0=k0, n=n: (i, jnp.clip(k - k0, 0, n - 1)))
                   for k0, n in zip(start, nblk)]
        b_spec = pl.BlockSpec((blk, tn), lambda i, j, k: (k, j))
        out_rows, tmo, dims, seg_axis = M, tm, NN, 2
    else:
        K = segs[0].shape[0]
        tkk = _tile(K, tk)
        grid = (total, N // tn, K // tkk)
        a_specs = [pl.BlockSpec((tkk, blk), lambda i, j, k, i0=i0, n=n: (
            jnp.where((i >= i0) & (i < i0 + n), k, 0), jnp.clip(i - i0, 0, n - 1))) for i0, n in zip(start, nblk)]
        b_spec = pl.BlockSpec((tkk, tn), lambda i, j, k: (k, j))
        out_rows, tmo, seg_axis = total * blk, blk, 0
    nk = grid[2]
    acc_shape = (tm, tn) if mode == "nn" else (tn, blk)

    def body(*refs):
        a_refs, b_ref, o_ref, acc_ref = refs[:ns], refs[ns], refs[ns + 1], refs[ns + 2]
        k = pl.program_id(2)
        sel = pl.program_id(seg_axis)

        @pl.when(k == 0)
        def _():
            acc_ref[...] = jnp.zeros_like(acc_ref)

        for s in range(ns):
            @pl.when((sel >= start[s]) & (sel < start[s] + nblk[s]))
            def _(s=s):
                a_, b_ = a_refs[s][...].astype(BF16), b_ref[...].astype(BF16)
                acc_ref[...] += _dot(a_, b_, NN) if mode == "nn" else _dot(b_, a_, TN)

        @pl.when(k == nk - 1)
        def _():
            acc = acc_ref[...]
            o_ref[...] = (acc if mode == "nn" else acc.T).astype(o_ref.dtype)

    return _pc(body, grid=grid, in_specs=a_specs + [b_spec], out_specs=pl.BlockSpec((tmo, tn), lambda i, j, k: (i, j)),
               out_shape=SDS((out_rows, N), out_dtype), name=name, scratch=(pltpu.VMEM(acc_shape, F32),))(*segs, b)


def _modk(c8, ada_w, ada_b, name):
    rows, D = c8.shape
    N = ada_w.shape[0]
    tn = _tile(N, 1536)

    def body(c_ref, w_ref, b_ref, mod_ref, ca_ref):
        ca = _silu(c_ref[...]).astype(BF16)
        mod_ref[...] = _dot(ca, w_ref[...], NT) + b_ref[...]
        ca_ref[...] = ca

    return _pc(body, grid=(N // tn,),
               in_specs=[pl.BlockSpec((rows, D), lambda j: (0, 0)), pl.BlockSpec((tn, D), lambda j: (j, 0)),
                         pl.BlockSpec((1, tn), lambda j: (0, j))],
               out_specs=[pl.BlockSpec((rows, tn), lambda j: (0, j)), pl.BlockSpec((rows, D), lambda j: (0, 0))],
               out_shape=[SDS((rows, N), F32), SDS((rows, D), BF16)], name=name)(c8, ada_w, ada_b)


def _row_tile(S):
    return _tile(S, 512, 8)


def _strips(tm, fn, init=0, rows=None):
    rows = STRIP if rows is None else rows
    assert tm % rows == 0
    return lax.fori_loop(0, tm // rows, lambda r, c: fn(pl.multiple_of(r * rows, rows), c), init)


def _rows8(rows):
    pad = 8 - len(rows)
    return jnp.concatenate(rows + ([jnp.zeros((pad, rows[0].shape[1]), F32)] if pad else []), axis=0)


def _fold8(v):
    return jnp.sum(v.reshape(v.shape[0] // 8, 8, v.shape[1]), axis=0)


def _norm_mod(x, g, mod3, sc_seg, sh_seg, S, name):
    T, D = x.shape
    tm = _row_tile(S)
    tpb = S // tm

    def body(x_ref, g_ref, sc_ref, sh_ref, h_ref):
        x_ = x_ref[...]
        r = lax.rsqrt(jnp.mean(x_ * x_, axis=-1, keepdims=True) + EPS)
        h_ref[...] = ((x_ * r) * (g_ref[...] * (1.0 + sc_ref[...])) + sh_ref[...]).astype(BF16)

    return _pc(body, grid=(T // tm,),
               in_specs=[pl.BlockSpec((tm, D), lambda i: (i, 0)), pl.BlockSpec((1, D), lambda i: (0, 0)),
                         pl.BlockSpec((None, 1, D), lambda i: (i // tpb, 0, sc_seg)),
                         pl.BlockSpec((None, 1, D), lambda i: (i // tpb, 0, sh_seg))],
               out_specs=pl.BlockSpec((tm, D), lambda i: (i, 0)), out_shape=SDS((T, D), BF16), name=name)(x, g, mod3, mod3)


def _resid_post(x, fo, mod3, gt_seg, pg, S, name):
    T, D = x.shape
    tm = _row_tile(S)
    tpb = S // tm

    def body(x_ref, f_ref, gt_ref, pg_ref, o_ref):
        f = f_ref[...]
        r = lax.rsqrt(jnp.mean(f * f, axis=-1, keepdims=True) + EPS)
        o_ref[...] = x_ref[...] + (f * r) * (gt_ref[...] * pg_ref[...])

    return _pc(body, grid=(T // tm,),
               in_specs=[pl.BlockSpec((tm, D), lambda i: (i, 0)), pl.BlockSpec((tm, D), lambda i: (i, 0)),
                         pl.BlockSpec((None, 1, D), lambda i: (i // tpb, 0, gt_seg)),
                         pl.BlockSpec((1, D), lambda i: (0, 0))],
               out_specs=pl.BlockSpec((tm, D), lambda i: (i, 0)), out_shape=SDS((T, D), F32), name=name)(x, fo, mod3, pg)


def _post_bwd(fo, mod3, gt_seg, pg, dout, S, name):
    T, D = fo.shape
    tm = _row_tile(S)
    tpb = S // tm
    nb = T // S

    def body(f_ref, gt_ref, pg_ref, d_ref, df_ref, dgt_ref, dpg_ref):
        i = pl.program_id(0)

        @pl.when(i == 0)
        def _():
            dpg_ref[...] = jnp.zeros_like(dpg_ref)

        @pl.when(i % tpb == 0)
        def _():
            dgt_ref[...] = jnp.zeros_like(dgt_ref)

        f, d = f_ref[...], d_ref[...]
        r = lax.rsqrt(jnp.mean(f * f, axis=-1, keepdims=True) + EPS)
        n = f * r
        dn = d * (gt_ref[...] * pg_ref[...])
        df_ref[...] = (r * (dn - n * jnp.mean(dn * n, axis=-1, keepdims=True))).astype(df_ref.dtype)
        tot = jnp.sum(d * n, axis=0, keepdims=True)
        dgt_ref[...] += _bsum(tot * pg_ref[...])
        dpg_ref[...] += _bsum(tot * gt_ref[...])

    return _pc(body, grid=(T // tm,),
               in_specs=[pl.BlockSpec((tm, D), lambda i: (i, 0)),
                         pl.BlockSpec((None, 1, D), lambda i: (i // tpb, 0, gt_seg)),
                         pl.BlockSpec((1, D), lambda i: (0, 0)), pl.BlockSpec((tm, D), lambda i: (i, 0))],
               out_specs=[pl.BlockSpec((tm, D), lambda i: (i, 0)), pl.BlockSpec((8, D), lambda i: (i // tpb, 0)),
                          pl.BlockSpec((8, D), lambda i: (0, 0))],
               out_shape=[SDS((T, D), BF16), SDS((nb * 8, D), F32), SDS((8, D), F32)], name=name)(fo, mod3, pg, dout)


def _pre_bwd(x, g, mod3, sc_seg, dh, dout, S, name):
    T, D = x.shape
    tm = _row_tile(S)
    tpb = S // tm
    nb = T // S

    def body(x_ref, g_ref, sc_ref, dh_ref, d_ref, dx_ref, dg_ref, dsc_ref, dsh_ref):
        i = pl.program_id(0)

        @pl.when(i == 0)
        def _():
            dg_ref[...] = jnp.zeros_like(dg_ref)

        @pl.when(i % tpb == 0)
        def _():
            dsc_ref[...] = jnp.zeros_like(dsc_ref)
            dsh_ref[...] = jnp.zeros_like(dsh_ref)

        x_, dh_ = x_ref[...], dh_ref[...]
        r = lax.rsqrt(jnp.mean(x_ * x_, axis=-1, keepdims=True) + EPS)
        n = x_ * r
        dn = dh_ * (g_ref[...] * (1.0 + sc_ref[...]))
        dx_ref[...] = d_ref[...] + r * (dn - n * jnp.mean(dn * n, axis=-1, keepdims=True))
        dhn = jnp.sum(dh_ * n, axis=0, keepdims=True)
        dg_ref[...] += _bsum(dhn * (1.0 + sc_ref[...]))
        dsc_ref[...] += _bsum(dhn * g_ref[...])
        dsh_ref[...] += _bsum(jnp.sum(dh_, axis=0, keepdims=True))

    row = pl.BlockSpec((tm, D), lambda i: (i, 0))
    return _pc(body, grid=(T // tm,),
               in_specs=[row, pl.BlockSpec((1, D), lambda i: (0, 0)),
                         pl.BlockSpec((None, 1, D), lambda i: (i // tpb, 0, sc_seg)), row, row],
               out_specs=[row, pl.BlockSpec((8, D), lambda i: (0, 0)), pl.BlockSpec((8, D), lambda i: (i // tpb, 0)),
                          pl.BlockSpec((8, D), lambda i: (i // tpb, 0))],
               out_shape=[SDS((T, D), F32), SDS((8, D), F32), SDS((nb * 8, D), F32), SDS((nb * 8, D), F32)],
               name=name)(x, g, mod3, dh, dout)


def _loss(y, target, S, name):
    T, D = y.shape
    tm = _row_tile(S)

    def body(y_ref, t_ref, dy_ref, l_ref):
        @pl.when(pl.program_id(0) == 0)
        def _():
            l_ref[...] = jnp.zeros_like(l_ref)

        def strip(r0, carry):
            rows = pl.ds(r0, STRIP)
            e = y_ref[rows, :] - t_ref[rows, :]
            dy_ref[rows, :] = e * (1.0 / D)
            return carry + _fold8(e * e)

        acc = _strips(tm, strip, jnp.zeros((8, D), F32))
        l_ref[...] += jnp.broadcast_to(jnp.sum(acc, keepdims=True) * (0.5 / D), l_ref.shape)

    row = pl.BlockSpec((tm, D), lambda i: (i, 0))
    return _pc(body, grid=(T // tm,), in_specs=[row, row],
               out_specs=[row, pl.BlockSpec((8, 128), lambda i: (0, 0))],
               out_shape=[SDS((T, D), F32), SDS((8, 128), F32)], name=name)(y, target)


def _conv_geom(view, C, S):
    arr, off = view
    T = arr.shape[0]
    tm = _row_tile(S)
    tc = _tile(C, 512)
    assert off % tc == 0 and C % tc == 0
    return arr, off // tc, T, tm, tc, S // tm


def _prev_spec(tm, tc, ob, order):
    if order == "ij":
        return pl.BlockSpec((8, tc), lambda i, j: (jnp.maximum(i * (tm // 8) - 1, 0), ob + j))
    return pl.BlockSpec((8, tc), lambda j, i: (jnp.maximum(i * (tm // 8) - 1, 0), ob + j))


def _next_spec(T, tm, tc, ob, order):
    last = T // 8 - 1
    if order == "ij":
        return pl.BlockSpec((8, tc), lambda i, j: (jnp.minimum((i + 1) * (tm // 8), last), ob + j))
    return pl.BlockSpec((8, tc), lambda j, i: (jnp.minimum((i + 1) * (tm // 8), last), ob + j))


def _taps(win, w_ref, K, lead, rows):
    acc = win[lead:lead + rows] * w_ref[K - 1:K, :]
    for j in range(1, K):
        acc = acc + win[lead - j:lead - j + rows] * w_ref[K - 1 - j:K - j, :]
    return acc


def _taps_t(win, w_ref, K, rows):
    acc = win[0:rows] * w_ref[K - 1:K, :]
    for j in range(1, K):
        acc = acc + win[j:j + rows] * w_ref[K - 1 - j:K - j, :]
    return acc


def _conv_fwd(view, C, w8, b, K, S, name):
    arr, ob, T, tm, tc, tps = _conv_geom(view, C, S)

    def body(u_ref, p_ref, w_ref, b_ref, o_ref, buf):
        first = (pl.program_id(0) % tps) == 0
        buf[0:8, :] = jnp.where(first, 0.0, p_ref[...])
        buf[8:, :] = u_ref[...]

        def strip(r0, carry):
            win = buf[pl.ds(r0, STRIP + 8), :]
            o_ref[pl.ds(r0, STRIP), :] = _taps(win, w_ref, K, 8, STRIP) + b_ref[...]
            return carry

        _strips(tm, strip)

    return _pc(body, grid=(T // tm, C // tc),
               in_specs=[pl.BlockSpec((tm, tc), lambda i, j: (i, ob + j)), _prev_spec(tm, tc, ob, "ij"),
                         pl.BlockSpec((8, tc), lambda i, j: (0, j)), pl.BlockSpec((1, tc), lambda i, j: (0, j))],
               out_specs=pl.BlockSpec((tm, tc), lambda i, j: (i, j)), out_shape=SDS((T, C), F32), name=name,
               scratch=(pltpu.VMEM((tm + 8, tc), F32),))(arr, arr, w8, b)


def _conv_bwd_in(dview, C, w8, K, S, out_dtype, name):
    arr, ob, T, tm, tc, tps = _conv_geom(dview, C, S)

    def body(d_ref, n_ref, w_ref, o_ref, buf):
        last = (pl.program_id(0) % tps) == tps - 1
        buf[0:tm, :] = d_ref[...]
        buf[tm:tm + 8, :] = jnp.where(last, 0.0, n_ref[...])

        def strip(r0, carry):
            win = buf[pl.ds(r0, STRIP + 8), :]
            o_ref[pl.ds(r0, STRIP), :] = _taps_t(win, w_ref, K, STRIP).astype(o_ref.dtype)
            return carry

        _strips(tm, strip)

    return _pc(body, grid=(T // tm, C // tc),
               in_specs=[pl.BlockSpec((tm, tc), lambda i, j: (i, ob + j)), _next_spec(T, tm, tc, ob, "ij"),
                         pl.BlockSpec((8, tc), lambda i, j: (0, j))],
               out_specs=pl.BlockSpec((tm, tc), lambda i, j: (i, j)), out_shape=SDS((T, C), out_dtype), name=name,
               scratch=(pltpu.VMEM((tm + 8, tc), F32),))(arr, arr, w8)


def _conv_bwd_w(dview, uview, C, K, S, name):
    darr, dob, T, tm, tc, tps = _conv_geom(dview, C, S)
    uarr, uob, _, _, _, _ = _conv_geom(uview, C, S)

    def body(d_ref, u_ref, p_ref, o_ref, buf):
        i = pl.program_id(1)

        @pl.when(i == 0)
        def _():
            o_ref[...] = jnp.zeros_like(o_ref)

        first = (i % tps) == 0
        buf[0:8, :] = jnp.where(first, 0.0, p_ref[...])
        buf[8:, :] = u_ref[...]

        def strip(r0, carry):
            win = buf[pl.ds(r0, STRIP + 8), :]
            d = d_ref[pl.ds(r0, STRIP), :]
            sums = [_fold8(d * win[8 - (K - 1 - k):8 - (K - 1 - k) + STRIP]) for k in range(K)] + [_fold8(d)]
            return tuple(c + s for c, s in zip(carry, sums))

        acc = _strips(tm, strip, tuple(jnp.zeros((8, tc), F32) for _ in range(K + 1)))
        o_ref[...] += _rows8([jnp.sum(a, axis=0, keepdims=True) for a in acc])

    return _pc(body, grid=(C // tc, T // tm),
               in_specs=[pl.BlockSpec((tm, tc), lambda j, i: (i, dob + j)),
                         pl.BlockSpec((tm, tc), lambda j, i: (i, uob + j)), _prev_spec(tm, tc, uob, "ji")],
               out_specs=pl.BlockSpec((8, tc), lambda j, i: (0, j)), out_shape=SDS((8, C), F32), name=name,
               scratch=(pltpu.VMEM((tm + 8, tc), F32),))(darr, uarr, uarr)


def _ffn_act_fwd(uu, w8, b, S, name):
    K, gw = FFN_CONV_K, GLU_W
    T, F2 = uu.shape
    tm, tc = _row_tile(S), 2 * GLU_W
    tps = S // tm

    def body(u_ref, p_ref, w_ref, b_ref, a_ref, buf):
        first = (pl.program_id(0) % tps) == 0
        buf[0:8, :] = jnp.where(first, 0.0, p_ref[...])
        buf[8:, :] = u_ref[...]

        def strip(r0, carry):
            u = _taps(buf[pl.ds(r0, STRIP + 8), :], w_ref, K, 8, STRIP) + b_ref[...]
            a_ref[pl.ds(r0, STRIP), :] = (_silu(u[:, :gw]) * u[:, gw:]).astype(BF16)
            return carry

        _strips(tm, strip)

    return _pc(body, grid=(T // tm, F2 // tc),
               in_specs=[pl.BlockSpec((tm, tc), lambda i, j: (i, j)), _prev_spec(tm, tc, 0, "ij"),
                         pl.BlockSpec((8, tc), lambda i, j: (0, j)), pl.BlockSpec((1, tc), lambda i, j: (0, j))],
               out_specs=pl.BlockSpec((tm, gw), lambda i, j: (i, j)), out_shape=SDS((T, F2 // 2), BF16), name=name,
               scratch=(pltpu.VMEM((tm + 8, tc), F32),))(uu, uu, w8, b)


def _ffn_act_bwd(uu, da, w8, b, S, name):
    K, gw = FFN_CONV_K, GLU_W
    T, F2 = uu.shape
    tm, tc = _row_tile(S), 2 * GLU_W
    tps = S // tm
    last16 = T // 16 - 1

    def body(u_ref, p_ref, n_ref, da_ref, dan_ref, w_ref, b_ref, duu_ref, cw_ref, ubuf, dabuf):
        i = pl.program_id(1)

        @pl.when(i == 0)
        def _():
            cw_ref[...] = jnp.zeros_like(cw_ref)

        first = (i % tps) == 0
        last = (i % tps) == tps - 1
        ubuf[0:8, :] = jnp.where(first, 0.0, p_ref[...])
        ubuf[8:tm + 8, :] = u_ref[...]
        ubuf[tm + 8:tm + 16, :] = n_ref[...]
        dabuf[0:tm, :] = da_ref[...].astype(F32)
        dabuf[tm:tm + 8, :] = jnp.where(last, 0.0, dan_ref[...].astype(F32)[0:8, :])

        def strip(r0, carry):
            ext = FFN_STRIP + 8
            win = ubuf[pl.ds(r0, FFN_STRIP + 16), :]
            u = _taps(win, w_ref, K, 8, ext) + b_ref[...]
            da_ = dabuf[pl.ds(r0, ext), :]
            g, v = u[:, :gw], u[:, gw:]
            du = jnp.concatenate([da_ * v * _dsilu(g), da_ * _silu(g)], axis=1)
            duu_ref[pl.ds(r0, FFN_STRIP), :] = _taps_t(du, w_ref, K, FFN_STRIP).astype(BF16)
            dmain = du[0:FFN_STRIP]
            sums = [_fold8(dmain * win[8 - (K - 1 - k):8 - (K - 1 - k) + FFN_STRIP]) for k in range(K)] + [_fold8(dmain)]
            return tuple(c + s for c, s in zip(carry, sums))

        acc = _strips(tm, strip, tuple(jnp.zeros((8, tc), F32) for _ in range(K + 1)), rows=FFN_STRIP)
        cw_ref[...] += _rows8([jnp.sum(a, axis=0, keepdims=True) for a in acc])

    return _pc(body, grid=(F2 // tc, T // tm),
               in_specs=[pl.BlockSpec((tm, tc), lambda j, i: (i, j)), _prev_spec(tm, tc, 0, "ji"),
                         _next_spec(T, tm, tc, 0, "ji"), pl.BlockSpec((tm, gw), lambda j, i: (i, j)),
                         pl.BlockSpec((16, gw), lambda j, i: (jnp.minimum((i + 1) * (tm // 16), last16), j)),
                         pl.BlockSpec((8, tc), lambda j, i: (0, j)), pl.BlockSpec((1, tc), lambda j, i: (0, j))],
               out_specs=[pl.BlockSpec((tm, tc), lambda j, i: (i, j)), pl.BlockSpec((8, tc), lambda j, i: (0, j))],
               out_shape=[SDS((T, F2), BF16), SDS((8, F2), F32)], name=name,
               scratch=(pltpu.VMEM((tm + 16, tc), F32), pltpu.VMEM((tm + 8, gw), F32)))(uu, uu, uu, da, da, w8, b)


def _ssd_common(dtc_raw, dtr_raw, hpc, hpr, L):
    dt_c = _softplus(dtc_raw + hpc[0:1, :])
    a_c = -jnp.exp(hpc[1:2, :])
    dt_r = _softplus(dtr_raw + hpr[:, 0:1])
    a_r = -jnp.exp(hpr[:, 1:2])
    li = lax.broadcasted_iota(jnp.int32, (L, L), 0)
    si = lax.broadcasted_iota(jnp.int32, (L, L), 1)
    low = li >= si
    upp = li <= si
    acs_c = _dotx(low, dt_c * a_c, split="b")
    acs_r = _dotx(dt_r * a_r, upp)
    return dt_c, a_c, acs_c, acs_r, low, upp


def _dotx(a, b, split="a", parts=3, dims=NN):
    val, one = (a, b) if split == "a" else (b, a)
    one = one.astype(BF16)
    acc, rem = None, val
    for i in range(parts):
        piece = rem.astype(BF16)
        t = _dot(piece, one, dims) if split == "a" else _dot(one, piece, dims)
        acc = t if acc is None else acc + t
        if i + 1 < parts:
            rem = rem - piece.astype(F32)
    return acc


def _head_maps(R, P, L):
    RP = R * P
    sel = (lax.broadcasted_iota(jnp.int32, (RP, R), 0) // P == lax.broadcasted_iota(jnp.int32, (RP, R), 1)).astype(F32)
    selt = (lax.broadcasted_iota(jnp.int32, (R, RP), 1) // P == lax.broadcasted_iota(jnp.int32, (R, RP), 0)).astype(F32)
    colb = (lax.broadcasted_iota(jnp.int32, (R, R * L), 1) // L == lax.broadcasted_iota(jnp.int32, (R, R * L), 0)).astype(F32)
    return sel, selt, colb


def _pair_diag(mats, rhs_b, R, P):
    lanes = 2 * P
    lo = lax.broadcasted_iota(jnp.int32, (mats[0].shape[0], lanes), 1) < P
    out = []
    for q in range(R // 2):
        rp = rhs_b[:, q * lanes:(q + 1) * lanes]
        out.append(jnp.where(lo, _dot(mats[2 * q], rp), _dot(mats[2 * q + 1], rp)))
    return jnp.concatenate(out, axis=1) if len(out) > 1 else out[0]


def _ssd_specs(pre, off_x, off_b, off_c, G, R, P, nb, nc, rev):
    L, N, RP = CHUNK, N_STATE, R * P
    cidx = (lambda c: nc - 1 - c) if rev else (lambda c: c)
    xb, bb, cb = off_x // RP, off_b // N, off_c // N
    assert off_x % RP == 0 and off_b % N == 0 and off_c % N == 0
    row = lambda b, c: b * nc + cidx(c)
    return dict(
        x=pl.BlockSpec((L, RP), lambda g, b, c: (row(b, c), xb + g)),
        b=pl.BlockSpec((L, N), lambda g, b, c: (row(b, c), bb + g)),
        c=pl.BlockSpec((L, N), lambda g, b, c: (row(b, c), cb + g)),
        dtc=pl.BlockSpec((None, L, R), lambda g, b, c: (g, row(b, c), 0)),
        dtr=pl.BlockSpec((None, R, L), lambda g, b, c: (g, 0, row(b, c))),
        hpc=pl.BlockSpec((None, 8, R), lambda g, b, c: (g, 0, 0)),
        hpr=pl.BlockSpec((None, R, 8), lambda g, b, c: (g, 0, 0)),
        y=pl.BlockSpec((L, RP), lambda g, b, c: (row(b, c), g)),
        bc=pl.BlockSpec((L, N), lambda g, b, c: (row(b, c), g)),
        hs=pl.BlockSpec((None, None, N, RP), lambda g, b, c: (row(b, c), g, 0, 0)),
    )


def _ssd_fwd(pre, offs, dtc, dtr, hpc, hpr, G, R, P, S, name):
    T = pre.shape[0]
    L, N, RP = CHUNK, N_STATE, R * P
    nc, nb = S // L, T // S
    sp = _ssd_specs(pre, *offs, G, R, P, nb, nc, False)

    def body(px_ref, pb_ref, pc_ref, dtc_ref, dtr_ref, hpc_ref, hpr_ref, y_ref, hs_ref, hst):
        @pl.when(pl.program_id(2) == 0)
        def _():
            hst[...] = jnp.zeros_like(hst)

        xs, bm, cm = _silu(px_ref[...]), _silu(pb_ref[...]), _silu(pc_ref[...])
        hpc_ = hpc_ref[...]
        dt_c, _, acs_c, acs_r, low, _ = _ssd_common(dtc_ref[...], dtr_ref[...], hpc_, hpr_ref[...], L)
        _, selt, colb = _head_maps(R, P, L)
        dt_e, a_e, hp_e = _dotx(dt_c, selt), _dotx(acs_c, selt), _dotx(hpc_, selt)
        a_bc = _dotx(acs_c, colb)
        a_last = a_e[L - 1:L, :]
        bb, cb = bm.astype(BF16), cm.astype(BF16)
        gm = _dot(cb, bb, NT)
        hprev = hst[...]
        hprev_b = hprev.astype(BF16)
        hs_ref[...] = hprev_b
        xdt = xs * dt_e
        xdt_b = xdt.astype(BF16)
        ms = []
        for r in range(R):
            dec = jnp.exp(jnp.where(low, a_bc[:, r * L:(r + 1) * L] - acs_r[r:r + 1, :], -jnp.inf))
            ms.append((gm * dec).astype(BF16))
        y = _pair_diag(ms, xdt_b, R, P) + _dot(cb, hprev_b) * jnp.exp(a_e) + hp_e[2:3, :] * xs
        y_ref[...] = y
        xw = (xdt * jnp.exp(a_last - a_e)).astype(BF16)
        hst[...] = hprev * jnp.exp(a_last) + _dot(bb, xw, TN)

    return _pc(body, grid=(G, nb, nc),
               in_specs=[sp["x"], sp["b"], sp["c"], sp["dtc"], sp["dtr"], sp["hpc"], sp["hpr"]],
               out_specs=[sp["y"], sp["hs"]],
               out_shape=[SDS((T, G * RP), F32), SDS((nb * nc, G, N, RP), BF16)], name=name,
               scratch=(pltpu.VMEM((N, RP), F32),))(pre, pre, pre, dtc, dtr, hpc, hpr)


def _ssd_bwd(pre, offs, dtc, dtr, hpc, hpr, hs, dy, G, R, P, S, name):
    T = pre.shape[0]
    L, N, RP = CHUNK, N_STATE, R * P
    nc, nb = S // L, T // S
    sp = _ssd_specs(pre, *offs, G, R, P, nb, nc, True)

    def body(px_ref, pb_ref, pc_ref, dtc_ref, dtr_ref, hpc_ref, hpr_ref, hs_ref, dy_ref,
             dpx_ref, dpb_ref, dpc_ref, ddt_ref, hpg_ref, dhst):
        bi, ci = pl.program_id(1), pl.program_id(2)

        @pl.when(ci == 0)
        def _():
            dhst[...] = jnp.zeros_like(dhst)

        @pl.when((bi == 0) & (ci == 0))
        def _():
            hpg_ref[...] = jnp.zeros_like(hpg_ref)

        px, pb, pcc = px_ref[...], pb_ref[...], pc_ref[...]
        xs, bm, cm = _silu(px), _silu(pb), _silu(pcc)
        hpc_ = hpc_ref[...]
        dtc_raw = dtc_ref[...]
        dt_c, a_c, acs_c, acs_r, low, upp = _ssd_common(dtc_raw, dtr_ref[...], hpc_, hpr_ref[...], L)
        sel, selt, colb = _head_maps(R, P, L)
        dt_e, a_e, hp_e = _dotx(dt_c, selt), _dotx(acs_c, selt), _dotx(hpc_, selt)
        a_bc = _dotx(acs_c, colb)
        a_last = a_e[L - 1:L, :]
        e_e, w_e = jnp.exp(a_e), jnp.exp(a_last - a_e)
        bb, cb = bm.astype(BF16), cm.astype(BF16)
        gm = _dot(cb, bb, NT)
        gmt = _dot(bb, cb, NT)
        hprev = hs_ref[...]
        dhn = dhst[...]
        dhn_b = dhn.astype(BF16)
        dy = dy_ref[...]
        dy_b = dy.astype(BF16)
        xdt = xs * dt_e
        xdt_b = xdt.astype(BF16)
        yoff = _dot(cb, hprev) * e_e
        dye_b = (dy * e_e).astype(BF16)
        dcm = _dot(dye_b, hprev, NT)
        dhst[...] = _dot(cb, dye_b, TN) + jnp.exp(a_last) * dhn
        dxdt_st = _dot(bb, dhn_b) * w_e
        dbm = _dot((xdt * w_e).astype(BF16), dhn_b, NT)
        lanes = 2 * P
        lo = lax.broadcasted_iota(jnp.int32, (L, lanes), 1) < P
        dg = jnp.zeros((L, L), F32)
        es, css = [], []
        for r in range(R):
            col_b, row = a_bc[:, r * L:(r + 1) * L], acs_r[r:r + 1, :]
            dec = jnp.exp(jnp.where(low, col_b - row, -jnp.inf))
            q = r // 2
            dyp = dy_b[:, q * lanes:(q + 1) * lanes]
            dyp = jnp.where(lo if r % 2 == 0 else ~lo, dyp, jnp.zeros_like(dyp))
            dm = _dot(dyp, xdt_b[:, q * lanes:(q + 1) * lanes], NT)
            dg = dg + dm * dec
            e = dm * (gm * dec)
            es.append(e)
            css.append(jnp.sum(e, axis=0, keepdims=True))
        dgb = dg.astype(BF16)
        dcm = dcm + _dot(dgb, bb)
        dbm = dbm + _dot(dgb, cb, TN)
        colbt = (lax.broadcasted_iota(jnp.int32, (R * L, R), 0) // L
                 == lax.broadcasted_iota(jnp.int32, (R * L, R), 1)).astype(F32)
        eye = (lax.broadcasted_iota(jnp.int32, (R, R), 0) == lax.broadcasted_iota(jnp.int32, (R, R), 1)).astype(F32)
        row_sums = _dotx(jnp.concatenate(es, axis=1), colbt)
        col_sums = _dotx(jnp.concatenate(css, axis=0), eye, dims=TN)
        mts = []
        for r in range(R):
            dect = jnp.exp(jnp.where(upp, acs_r[r:r + 1, :] - a_bc[:, r * L:(r + 1) * L], -jnp.inf))
            mts.append((gmt * dect).astype(BF16))
        dxdt = _pair_diag(mts, dy_b, R, P) + dxdt_st
        q_st = _dotx(xdt * dxdt_st, sel, parts=2)
        da = row_sums - col_sums + _dotx(dy * yoff, sel, parts=2) - q_st
        hh = jnp.sum(_dotx(dhn * hprev.astype(F32), sel, parts=2), axis=0, keepdims=True)
        da_last = jnp.exp(acs_c[L - 1:L, :]) * hh + jnp.sum(q_st, axis=0, keepdims=True)
        rowi = lax.broadcasted_iota(jnp.int32, (L, R), 0)
        da = da + jnp.where(rowi == L - 1, da_last, 0.0)
        dpx_ref[...] = (dxdt * dt_e + hp_e[2:3, :] * dy) * _dsilu(px)
        dpb_ref[...] = dbm * _dsilu(pb)
        dpc_ref[...] = dcm * _dsilu(pcc)
        dadt = _dotx(upp, da, split="b")
        ddt = _dotx(dxdt * xs, sel, parts=2) + dadt * a_c
        ddt_raw = ddt * jax.nn.sigmoid(dtc_raw + hpc_[0:1, :])
        ddt_ref[...] = ddt_raw
        d_a = jnp.sum(dadt * dt_c, axis=0, keepdims=True)
        d_d = jnp.sum(_dotx(dy * xs, sel, parts=2), axis=0, keepdims=True)
        rows = [jnp.sum(ddt_raw, axis=0, keepdims=True), d_a * a_c, d_d, jnp.zeros((5, R), F32)]
        hpg_ref[...] += jnp.concatenate(rows, axis=0)

    return _pc(body, grid=(G, nb, nc),
               in_specs=[sp["x"], sp["b"], sp["c"], sp["dtc"], sp["dtr"], sp["hpc"], sp["hpr"], sp["hs"], sp["y"]],
               out_specs=[sp["y"], sp["bc"], sp["bc"], sp["dtc"], pl.BlockSpec((None, 8, R), lambda g, b, c: (g, 0, 0))],
               out_shape=[SDS((T, G * RP), F32), SDS((T, G * N), F32), SDS((T, G * N), F32), SDS((G, T, R), F32),
                          SDS((G, 8, R), F32)], name=name,
               scratch=(pltpu.VMEM((N, RP), F32),))(pre, pre, pre, dtc, dtr, hpc, hpr, hs, dy)


def _gate_norm_fwd(y, zview, ng, G, S, name):
    T, DI = y.shape
    zarr, zoff = zview
    gw = DI // G
    tm = _row_tile(S)
    zb = zoff // gw
    assert zoff % gw == 0

    def body(y_ref, z_ref, g_ref, o_ref):
        yg = y_ref[...] * _silu(z_ref[...])
        r = lax.rsqrt(jnp.mean(yg * yg, axis=-1, keepdims=True) + EPS)
        o_ref[...] = (yg * r * g_ref[...]).astype(BF16)

    return _pc(body, grid=(T // tm, G),
               in_specs=[pl.BlockSpec((tm, gw), lambda i, g: (i, g)), pl.BlockSpec((tm, gw), lambda i, g: (i, zb + g)),
                         pl.BlockSpec((1, gw), lambda i, g: (0, g))],
               out_specs=pl.BlockSpec((tm, gw), lambda i, g: (i, g)), out_shape=SDS((T, DI), BF16), name=name)(y, zarr, ng)


def _gate_norm_bwd(y, zview, ng, dyn, G, S, name):
    T, DI = y.shape
    zarr, zoff = zview
    gw = DI // G
    tm = _row_tile(S)
    zb = zoff // gw

    def body(y_ref, z_ref, g_ref, d_ref, dy_ref, dz_ref, dg_ref):
        @pl.when(pl.program_id(1) == 0)
        def _():
            dg_ref[...] = jnp.zeros_like(dg_ref)

        y_, z, d = y_ref[...], z_ref[...], d_ref[...]
        sz = _silu(z)
        yg = y_ * sz
        r = lax.rsqrt(jnp.mean(yg * yg, axis=-1, keepdims=True) + EPS)
        n = yg * r
        dn = d * g_ref[...]
        dyg = r * (dn - n * jnp.mean(dn * n, axis=-1, keepdims=True))
        dy_ref[...] = dyg * sz
        dz_ref[...] = (dyg * y_ * _dsilu(z)).astype(BF16)
        dg_ref[...] += _bsum(jnp.sum(d * n, axis=0, keepdims=True))

    return _pc(body, grid=(G, T // tm),
               in_specs=[pl.BlockSpec((tm, gw), lambda g, i: (i, g)), pl.BlockSpec((tm, gw), lambda g, i: (i, zb + g)),
                         pl.BlockSpec((1, gw), lambda g, i: (0, g)), pl.BlockSpec((tm, gw), lambda g, i: (i, g))],
               out_specs=[pl.BlockSpec((tm, gw), lambda g, i: (i, g)), pl.BlockSpec((tm, gw), lambda g, i: (i, g)),
                          pl.BlockSpec((8, gw), lambda g, i: (0, g))],
               out_shape=[SDS((T, DI), F32), SDS((T, DI), BF16), SDS((8, DI), F32)], name=name)(y, zarr, ng, dyn)


def _shortconv_fwd(proj, off_b, off_c, off_h, C, w8, S, name):
    K = SC_CONV_K
    _, ob, T, tm, tc, tps = _conv_geom((proj, off_b), C, S)
    oc, oh = off_c // tc, off_h // tc

    def body(b_ref, c_ref, h_ref, cp_ref, hp_ref, w_ref, o_ref, buf):
        first = (pl.program_id(0) % tps) == 0
        buf[0:8, :] = jnp.where(first, 0.0, cp_ref[...] * hp_ref[...])
        buf[8:, :] = c_ref[...] * h_ref[...]

        def strip(r0, carry):
            conv = _taps(buf[pl.ds(r0, STRIP + 8), :], w_ref, K, 8, STRIP)
            o_ref[pl.ds(r0, STRIP), :] = (b_ref[pl.ds(r0, STRIP), :] * conv).astype(BF16)
            return carry

        _strips(tm, strip)

    blk = lambda o: pl.BlockSpec((tm, tc), lambda i, j: (i, o + j))
    return _pc(body, grid=(T // tm, C // tc),
               in_specs=[blk(ob), blk(oc), blk(oh), _prev_spec(tm, tc, oc, "ij"), _prev_spec(tm, tc, oh, "ij"),
                         pl.BlockSpec((8, tc), lambda i, j: (0, j))],
               out_specs=pl.BlockSpec((tm, tc), lambda i, j: (i, j)), out_shape=SDS((T, C), BF16), name=name,
               scratch=(pltpu.VMEM((tm + 8, tc), F32),))(proj, proj, proj, proj, proj, w8)


def _shortconv_bwd(proj, off_b, off_c, off_h, C, w8, dsc, S, name):
    K = SC_CONV_K
    _, ob, T, tm, tc, tps = _conv_geom((proj, off_b), C, S)
    oc, oh = off_c // tc, off_h // tc

    def body(b_ref, c_ref, h_ref, cp_ref, hp_ref, bn_ref, d_ref, dn_ref, w_ref,
             db_ref, dc_ref, dh_ref, dw_ref, buf, buf2):
        i = pl.program_id(1)

        @pl.when(i == 0)
        def _():
            dw_ref[...] = jnp.zeros_like(dw_ref)

        first = (i % tps) == 0
        last = (i % tps) == tps - 1
        buf[0:8, :] = jnp.where(first, 0.0, cp_ref[...] * hp_ref[...])
        buf[8:, :] = c_ref[...] * h_ref[...]
        buf2[0:tm, :] = d_ref[...] * b_ref[...]
        buf2[tm:tm + 8, :] = jnp.where(last, 0.0, dn_ref[...] * bn_ref[...])

        def strip(r0, carry):
            rows = pl.ds(r0, STRIP)
            vwin = buf[pl.ds(r0, STRIP + 8), :]
            db_ref[rows, :] = (d_ref[rows, :] * _taps(vwin, w_ref, K, 8, STRIP)).astype(BF16)
            dwin = buf2[pl.ds(r0, STRIP + 8), :]
            dv = _taps_t(dwin, w_ref, K, STRIP)
            dc_ref[rows, :] = (dv * h_ref[rows, :]).astype(BF16)
            dh_ref[rows, :] = (dv * c_ref[rows, :]).astype(BF16)
            dconv = dwin[0:STRIP]
            sums = [_fold8(dconv * vwin[8 - (K - 1 - k):8 - (K - 1 - k) + STRIP]) for k in range(K)]
            return tuple(c + s for c, s in zip(carry, sums))

        acc = _strips(tm, strip, tuple(jnp.zeros((8, tc), F32) for _ in range(K)))
        dw_ref[...] += _rows8([jnp.sum(a, axis=0, keepdims=True) for a in acc])

    blk = lambda o: pl.BlockSpec((tm, tc), lambda j, i: (i, o + j))
    out = pl.BlockSpec((tm, tc), lambda j, i: (i, j))
    return _pc(body, grid=(C // tc, T // tm),
               in_specs=[blk(ob), blk(oc), blk(oh), _prev_spec(tm, tc, oc, "ji"), _prev_spec(tm, tc, oh, "ji"),
                         _next_spec(T, tm, tc, ob, "ji"), blk(0), _next_spec(T, tm, tc, 0, "ji"),
                         pl.BlockSpec((8, tc), lambda j, i: (0, j))],
               out_specs=[out, out, out, pl.BlockSpec((8, tc), lambda j, i: (0, j))],
               out_shape=[SDS((T, C), BF16)] * 3 + [SDS((8, C), F32)], name=name,
               scratch=(pltpu.VMEM((tm + 8, tc), F32), pltpu.VMEM((tm + 8, tc), F32)))(
                   proj, proj, proj, proj, proj, proj, dsc, dsc, w8)


def _merge_fwd(proj, off_g1, off_g2, y1, y2, S, name):
    T, D = y1.shape
    tm = _row_tile(S)
    o1, o2 = off_g1 // D, off_g2 // D
    assert off_g1 % D == 0 and off_g2 % D == 0

    def body(g1_ref, g2_ref, y1_ref, y2_ref, o_ref):
        def strip(r0, carry):
            rows = pl.ds(r0, STRIP)
            o_ref[rows, :] = (jax.nn.sigmoid(g1_ref[rows, :]) * y1_ref[rows, :]
                              + jax.nn.sigmoid(g2_ref[rows, :]) * y2_ref[rows, :]).astype(BF16)
            return carry

        _strips(tm, strip)

    row = pl.BlockSpec((tm, D), lambda i: (i, 0))
    return _pc(body, grid=(T // tm,),
               in_specs=[pl.BlockSpec((tm, D), lambda i: (i, o1)), pl.BlockSpec((tm, D), lambda i: (i, o2)), row, row],
               out_specs=row, out_shape=SDS((T, D), BF16), name=name)(proj, proj, y1, y2)


def _merge_bwd(proj, off_g1, off_g2, y1, y2, dm, S, name):
    T, D = y1.shape
    tm = _row_tile(S)
    o1, o2 = off_g1 // D, off_g2 // D

    def body(g1_ref, g2_ref, y1_ref, y2_ref, d_ref, dy1_ref, dy2_ref, dg1_ref, dg2_ref):
        def strip(r0, carry):
            rows = pl.ds(r0, STRIP)
            d = d_ref[rows, :]
            s1, s2 = jax.nn.sigmoid(g1_ref[rows, :]), jax.nn.sigmoid(g2_ref[rows, :])
            dy1_ref[rows, :] = (d * s1).astype(BF16)
            dy2_ref[rows, :] = (d * s2).astype(BF16)
            dg1_ref[rows, :] = (d * y1_ref[rows, :] * s1 * (1.0 - s1)).astype(BF16)
            dg2_ref[rows, :] = (d * y2_ref[rows, :] * s2 * (1.0 - s2)).astype(BF16)
            return carry

        _strips(tm, strip)

    row = pl.BlockSpec((tm, D), lambda i: (i, 0))
    return _pc(body, grid=(T // tm,),
               in_specs=[pl.BlockSpec((tm, D), lambda i: (i, o1)), pl.BlockSpec((tm, D), lambda i: (i, o2)), row, row, row],
               out_specs=[row] * 4, out_shape=[SDS((T, D), BF16)] * 4, name=name)(proj, proj, y1, y2, dm)


def _pad8(w):
    return jnp.pad(w, ((0, 8 - w.shape[0]), (0, 0)))


def _dims(w):
    D = w["mix_pre_g"].shape[-1]
    DI = w["ssd_norm_g"].shape[-1]
    H = w["ssd_dt_bias"].shape[-1]
    conv_dim = w["ssd_conv_b"].shape[-1]
    G = (conv_dim - DI) // (2 * N_STATE)
    F = w["w_down"].shape[0]
    return dict(D=D, DI=DI, H=H, P=DI // H, G=G, R=H // G, GN=G * N_STATE, CD=conv_dim, F=F)


def _proj_layout(d):
    D, DI, CD, H = d["D"], d["DI"], d["CD"], d["H"]
    o = dict(z=0, xbc=DI, scb=DI + CD, scc=DI + CD + D, sch=DI + CD + 2 * D, g1=DI + CD + 3 * D, g2=DI + CD + 4 * D,
             dt=DI + CD + 5 * D)
    o["sb"] = math.gcd(SEG_BLK, D, DI, d["GN"])
    assert o["sb"] % 128 == 0 and H <= o["sb"]
    o["np"] = o["dt"] + o["sb"]
    return o


def _glu_perm(a, F, inverse=False):
    lead = a.shape[:-1]
    nb = F // GLU_W
    if not inverse:
        return a.reshape(*lead, 2, nb, GLU_W).swapaxes(-3, -2).reshape(*lead, 2 * F)
    return a.reshape(*lead, nb, 2, GLU_W).swapaxes(-3, -2).reshape(*lead, 2 * F)


def _glu_perm_rows(a, F, inverse=False):
    nb, D = F // GLU_W, a.shape[1]
    shape = (nb, 2, GLU_W, D) if inverse else (2, nb, GLU_W, D)
    return a.reshape(shape).swapaxes(0, 1).reshape(2 * F, D)


def _prep_layer(w):
    d = _dims(w)
    D, DI, CD, H, G, R, F = d["D"], d["DI"], d["CD"], d["H"], d["G"], d["R"], d["F"]
    lay = _proj_layout(d)
    w_in = w["w_in"]
    used = lay["dt"] + H
    wcat = jnp.concatenate([w_in[:DI + CD], w_in[DI + CD + H:], w_in[DI + CD:DI + CD + H],
                            jnp.zeros((lay["np"] - used, D), w_in.dtype)], axis=0)
    hp = jnp.stack([w["ssd_dt_bias"], w["ssd_a_log"], w["ssd_d"]], 0).astype(F32)
    hpc = jnp.pad(hp.reshape(3, G, R).transpose(1, 0, 2), ((0, 0), (0, 5), (0, 0)))
    hpr = jnp.pad(hp[:2].reshape(2, G, R).transpose(1, 2, 0), ((0, 0), (0, 0), (0, 6)))
    row = lambda v: v.reshape(1, -1).astype(F32)
    return dict(
        d=d, lay=lay, ada_w=w["ada_w"].astype(BF16), ada_b=row(w["ada_b"]),
        mix_pre_g=row(w["mix_pre_g"]), mix_post_g=row(w["mix_post_g"]), wcat=wcat.astype(BF16),
        ssd_conv_w=_pad8(w["ssd_conv_w"].astype(F32)), ssd_conv_b=row(w["ssd_conv_b"]), hpc=hpc, hpr=hpr,
        ssd_norm_g=row(w["ssd_norm_g"]), w_ssd_out=w["w_ssd_out"].astype(BF16),
        sc_conv_w=_pad8(w["sc_conv_w"].astype(F32)), w_sc_out=w["w_sc_out"].astype(BF16), w_o=w["w_o"].astype(BF16),
        ffn_pre_g=row(w["ffn_pre_g"]), ffn_post_g=row(w["ffn_post_g"]),
        w_up=_glu_perm_rows(w["w_up"], F).astype(BF16), ffn_conv_w=_pad8(_glu_perm(w["ffn_conv_w"].astype(F32), F)),
        ffn_conv_b=_glu_perm(row(w["ffn_conv_b"]), F), w_down=w["w_down"].astype(BF16))


def _dt_layouts(proj, lay, d):
    T = proj.shape[0]
    dt = proj[:, lay["dt"]:lay["dt"] + d["H"]].reshape(T, d["G"], d["R"])
    return dt.transpose(1, 0, 2), dt.transpose(1, 2, 0)


def _layer_fwd(x, mod3, p, S, li):
    d, lay = p["d"], p["lay"]
    D, DI, G, R, P, GN, CD = d["D"], d["DI"], d["G"], d["R"], d["P"], d["GN"], d["CD"]
    nm = lambda s: f"l{li}_{s}"
    h = _norm_mod(x, p["mix_pre_g"], mod3, 1, 0, S, nm("norm1"))
    proj = _mm(h, p["wcat"], "nt", F32, nm("mm_in"), caps=(1024, 1536, 2048))
    pre = _conv_fwd((proj, lay["xbc"]), CD, p["ssd_conv_w"], p["ssd_conv_b"], SSD_CONV_K, S, nm("ssdconv"))
    dtc, dtr = _dt_layouts(proj, lay, d)
    offs = (0, DI, DI + GN)
    y, hs = _ssd_fwd(pre, offs, dtc, dtr, p["hpc"], p["hpr"], G, R, P, S, nm("ssd"))
    yn = _gate_norm_fwd(y, (proj, lay["z"]), p["ssd_norm_g"], G, S, nm("gnorm"))
    sc = _shortconv_fwd(proj, lay["scb"], lay["scc"], lay["sch"], D, p["sc_conv_w"], S, nm("sconv"))
    y_ssd = _mm(yn, p["w_ssd_out"], "nn", F32, nm("mm_ssdout"))
    y_sc = _mm(sc, p["w_sc_out"], "nn", F32, nm("mm_scout"))
    m = _merge_fwd(proj, lay["g1"], lay["g2"], y_ssd, y_sc, S, nm("merge"))
    mix = _mm(m, p["w_o"], "nn", F32, nm("mm_o"))
    x1 = _resid_post(x, mix, mod3, 2, p["mix_post_g"], S, nm("post1"))
    h2 = _norm_mod(x1, p["ffn_pre_g"], mod3, 4, 3, S, nm("norm2"))
    uu = _mm(h2, p["w_up"], "nt", F32, nm("mm_up"), caps=(1024, 1408, 2048))
    a = _ffn_act_fwd(uu, p["ffn_conv_w"], p["ffn_conv_b"], S, nm("ffnact"))
    f = _mm(a, p["w_down"], "nn", F32, nm("mm_down"), caps=(1024, 1024, 1408))
    x2 = _resid_post(x1, f, mod3, 5, p["ffn_post_g"], S, nm("post2"))
    saved = dict(x=x, h=h, proj=proj, pre=pre, dtc=dtc, dtr=dtr, y=y, hs=hs, yn=yn, sc=sc, y_ssd=y_ssd, y_sc=y_sc,
                 m=m, mix=mix, x1=x1, h2=h2, uu=uu, a=a, f=f)
    return x2, saved


def _seq_sum(acc, nb):
    return acc.reshape(nb, 8, -1)[:, 0, :]


def _layer_bwd(dx2, mod3, p, s, S, li):
    d, lay = p["d"], p["lay"]
    D, DI, G, R, P, GN, CD, H, F = d["D"], d["DI"], d["G"], d["R"], d["P"], d["GN"], d["CD"], d["H"], d["F"]
    nb = dx2.shape[0] // S
    nm = lambda t: f"l{li}_{t}"
    g = {}
    df, dgt2, dpg2 = _post_bwd(s["f"], mod3, 5, p["ffn_post_g"], dx2, S, nm("post2_b"))
    g["ffn_post_g"] = dpg2[0]
    da = _mm(df, p["w_down"], "nt", BF16, nm("mm_down_bi"), caps=(1024, 1408, 2048))
    g["w_down"] = _mm(s["a"], df, "tn", WGRAD,nm("mm_down_bw"), caps=(1408, 1024, 1024))
    duu, cw = _ffn_act_bwd(s["uu"], da, p["ffn_conv_w"], p["ffn_conv_b"], S, nm("ffnact_b"))
    g["ffn_conv_w"] = _glu_perm(cw[:FFN_CONV_K], F, inverse=True)
    g["ffn_conv_b"] = _glu_perm(cw[FFN_CONV_K], F, inverse=True)
    dh2 = _mm(duu, p["w_up"], "nn", F32, nm("mm_up_bi"), caps=(1024, 1024, 2816))
    g["w_up"] = _glu_perm_rows(_mm(duu, s["h2"], "tn", WGRAD,nm("mm_up_bw"), caps=(1408, 1024, 1024)), F, inverse=True)
    dx1, dg2, dsc2, dsh2 = _pre_bwd(s["x1"], p["ffn_pre_g"], mod3, 4, dh2, dx2, S, nm("norm2_b"))
    g["ffn_pre_g"] = dg2[0]
    dmix, dgt1, dpg1 = _post_bwd(s["mix"], mod3, 2, p["mix_post_g"], dx1, S, nm("post1_b"))
    g["mix_post_g"] = dpg1[0]
    dm = _mm(dmix, p["w_o"], "nt", F32, nm("mm_o_bi"))
    g["w_o"] = _mm(s["m"], dmix, "tn", WGRAD,nm("mm_o_bw"))
    proj = s["proj"]
    dy_ssd, dy_sc, dg1, dg2_ = _merge_bwd(proj, lay["g1"], lay["g2"], s["y_ssd"], s["y_sc"], dm, S, nm("merge_b"))
    dyn = _mm(dy_ssd, p["w_ssd_out"], "nt", F32, nm("mm_ssdout_bi"))
    g["w_ssd_out"] = _mm(s["yn"], dy_ssd, "tn", WGRAD,nm("mm_ssdout_bw"))
    dsc = _mm(dy_sc, p["w_sc_out"], "nt", F32, nm("mm_scout_bi"))
    g["w_sc_out"] = _mm(s["sc"], dy_sc, "tn", WGRAD,nm("mm_scout_bw"))
    dscb, dscc, dsch, scw = _shortconv_bwd(proj, lay["scb"], lay["scc"], lay["sch"], D, p["sc_conv_w"], dsc, S, nm("sconv_b"))
    g["sc_conv_w"] = scw[:SC_CONV_K]
    dy, dz, dng = _gate_norm_bwd(s["y"], (proj, lay["z"]), p["ssd_norm_g"], dyn, G, S, nm("gnorm_b"))
    g["ssd_norm_g"] = dng[0]
    offs = (0, DI, DI + GN)
    dpx, dpb, dpc, ddt, hpg = _ssd_bwd(s["pre"], offs, s["dtc"], s["dtr"], p["hpc"], p["hpr"], s["hs"], dy,
                                       G, R, P, S, nm("ssd_b"))
    g["ssd_dt_bias"], g["ssd_a_log"], g["ssd_d"] = hpg[:, 0, :].reshape(H), hpg[:, 1, :].reshape(H), hpg[:, 2, :].reshape(H)
    cws, dxbc = [], []
    for name, darr, off, C in (("x", dpx, 0, DI), ("b", dpb, DI, GN), ("c", dpc, DI + GN, GN)):
        w8 = p["ssd_conv_w"][:, off:off + C]
        cws.append(_conv_bwd_w((darr, 0), (proj, lay["xbc"] + off), C, SSD_CONV_K, S, nm(f"ssdconv_bw_{name}")))
        dxbc.append(_conv_bwd_in((darr, 0), C, w8, SSD_CONV_K, S, BF16, nm(f"ssdconv_bi_{name}")))
    cws = jnp.concatenate(cws, axis=1)
    g["ssd_conv_w"], g["ssd_conv_b"] = cws[:SSD_CONV_K], cws[SSD_CONV_K]
    T = dx2.shape[0]
    ddt_t = jnp.pad(ddt.transpose(1, 0, 2).reshape(T, H).astype(BF16), ((0, 0), (0, lay["sb"] - H)))
    dproj = [dz] + dxbc + [dscb, dscc, dsch, dg1, dg2_, ddt_t]
    dh = _mm_seg(dproj, p["wcat"], "nn", F32, nm("mm_in_bi"), lay["sb"])
    dwcat = _mm_seg(dproj, s["h"], "tn", WGRAD, nm("mm_in_bw"), lay["sb"], tk=1024)
    o = lay
    g["w_in"] = jnp.concatenate([dwcat[o["z"]:o["scb"]], dwcat[o["dt"]:o["dt"] + H], dwcat[o["scb"]:o["dt"]]], axis=0)
    dx, dg1_, dsc1, dsh1 = _pre_bwd(s["x"], p["mix_pre_g"], mod3, 1, dh, dx1, S, nm("norm1_b"))
    g["mix_pre_g"] = dg1_[0]
    dmod = jnp.concatenate([_seq_sum(t, nb) for t in (dsh1, dsc1, dgt1, dsh2, dsc2, dgt2)], axis=1)
    return dx, dmod, g


def _fwd_bwd(x3, c, target3, layers):
    nb, S, D = x3.shape
    T = nb * S
    x = x3.reshape(T, D)
    c8 = jnp.pad(c, ((0, MOD_ROWS - nb), (0, 0)))
    preps = [_prep_layer(w) for w in layers]
    saved, mods, cact = [], [], None
    for li, p in enumerate(preps):
        mod, cact = _modk(c8, p["ada_w"], p["ada_b"], f"l{li}_mod")
        mod3 = mod[:nb].reshape(nb, 1, 6 * D)
        x, s = _layer_fwd(x, mod3, p, S, li)
        saved.append(s)
        mods.append(mod3)
    dy, lacc = _loss(x, target3.reshape(T, D), S, "loss")
    grads = [None] * len(preps)
    for li in reversed(range(len(preps))):
        dy, dmod, g = _layer_bwd(dy, mods[li], preps[li], saved[li], S, li)
        dmod8 = jnp.pad(dmod, ((0, MOD_ROWS - nb), (0, 0)))
        g["ada_b"] = _colsum(dmod8, f"l{li}_adab")
        g["ada_w"] = _mm(dmod8, cact, "tn", WGRAD,f"l{li}_mm_ada_bw", caps=(1536, 1024, 2048))
        grads[li] = g
    return lacc[0, 0], dy.reshape(nb, S, D), grads


def _colsum(a8, name):
    rows, C = a8.shape
    tc = _tile(C, 2048)

    def body(a_ref, o_ref):
        o_ref[...] = _bsum(jnp.sum(a_ref[...], axis=0, keepdims=True))

    return _pc(body, grid=(C // tc,), in_specs=[pl.BlockSpec((rows, tc), lambda j: (0, j))],
               out_specs=pl.BlockSpec((8, tc), lambda j: (0, j)), out_shape=SDS((8, C), F32), name=name)(a8)[0]


def _adam(gs, w, m, v, name):
    ns, R, W = gs.shape
    tr = _tile(R, 256, 8)

    def body(g_ref, w_ref, m_ref, v_ref, go_ref, d_ref, mo_ref, vo_ref):
        g = g_ref[0].astype(F32)
        for k in range(1, ns):
            g = g + g_ref[k].astype(F32)
        go_ref[...] = g
        d_ref[...], mo_ref[...], vo_ref[...] = _adam_update(g, w_ref[...], m_ref[...], v_ref[...])

    row = pl.BlockSpec((tr, W), lambda i: (i, 0))
    return _pc(body, grid=(R // tr,), in_specs=[pl.BlockSpec((ns, tr, W), lambda i: (0, i, 0)), row, row, row],
               out_specs=[row] * 4, out_shape=[SDS((R, W), F32)] * 4, name=name)(gs, w, m, v)


def _adam_update(g, w, m, v):
    c1 = 1.0 / (1.0 - ADAM_B1 ** ADAM_STEP)
    c2 = 1.0 / (1.0 - ADAM_B2 ** ADAM_STEP)
    m_ = ADAM_B1 * m + (1.0 - ADAM_B1) * g
    v_ = ADAM_B2 * v + (1.0 - ADAM_B2) * (g * g)
    return -ADAM_LR * ((m_ * c1) / (jnp.sqrt(v_ * c2) + ADAM_EPS) + ADAM_WD * w), m_, v_


def _adam_nat(g, w, m, v, name):
    depth, a, b = w.shape
    tr = _tile(a, 256, 8)

    def body(g_ref, w_ref, m_ref, v_ref, d_ref, mo_ref, vo_ref):
        d_ref[...], mo_ref[...], vo_ref[...] = _adam_update(g_ref[...], w_ref[...], m_ref[...], v_ref[...])

    blk = pl.BlockSpec((None, tr, b), lambda l, i: (l, i, 0))
    return _pc(body, grid=(depth, a // tr), in_specs=[blk] * 4, out_specs=[blk] * 3,
               out_shape=[SDS(w.shape, F32)] * 3, name=name)(g, w, m, v)


def _sum_chips(gs, name):
    ns, R, W = gs.shape
    tr = _tile(R, 256, 16)

    def body(g_ref, o_ref):
        acc = g_ref[0].astype(F32)
        for k in range(1, ns):
            acc = acc + g_ref[k].astype(F32)
        o_ref[...] = acc

    return _pc(body, grid=(R // tr,), in_specs=[pl.BlockSpec((ns, tr, W), lambda i: (0, i, 0))],
               out_specs=pl.BlockSpec((tr, W), lambda i: (i, 0)), out_shape=SDS((R, W), F32), name=name)(gs)


HBM_SPEC = pl.BlockSpec(memory_space=pltpu.HBM)
VMEM_SPEC = pl.BlockSpec(memory_space=pltpu.VMEM)


def _dev():
    return lax.axis_index("x"), lax.axis_index("y"), lax.axis_index("c")


def _allgather_big(loc, name):
    R, W = loc.shape

    def body(x_ref, out_ref, send_sems, recv_sems, local_sem):
        x, y, c = _dev()
        me, sibling = (x, y, c), (x, y, 1 - c)
        chips = [(1 - x, y), (x, 1 - y), (1 - x, 1 - y)]

        def slab(px, py, pc):
            return out_ref.at[4 * px + 2 * py + pc]

        def copy(k, block, to, src=None):
            return pltpu.make_async_remote_copy(
                src_ref=slab(*block) if src is None else src, dst_ref=slab(*block),
                send_sem=send_sems.at[k], recv_sem=recv_sems.at[k], device_id=to, device_id_type=MESH)

        mine = pltpu.make_async_copy(x_ref, slab(*me), local_sem)
        mine.start()
        first = [copy(0, me, sibling, src=x_ref)]
        first += [copy(1 + j, me, (*chip, c), src=x_ref) for j, chip in enumerate(chips)]
        for cp in first:
            cp.start()
        passed = [copy(4 + j, (*chip, c), sibling) for j, chip in enumerate(chips)]
        for j, chip in enumerate(chips):
            copy(1 + j, (*chip, c), me).wait_recv()
            passed[j].start()
        copy(0, sibling, me).wait_recv()
        for j, chip in enumerate(chips):
            copy(4 + j, (*chip, 1 - c), me).wait_recv()
        for cp in first + passed:
            cp.wait_send()
        mine.wait()

    return pl.pallas_call(
        body, out_shape=SDS((N_DEV, R, W), loc.dtype), in_specs=[HBM_SPEC], out_specs=HBM_SPEC,
        scratch_shapes=[pltpu.SemaphoreType.DMA((7,)), pltpu.SemaphoreType.DMA((7,)), pltpu.SemaphoreType.DMA],
        name=name)(loc)


def _rs_pair_exchange(g, name):
    nd, R, W = g.shape
    nj = nd // 2

    def body(g_ref, out_ref, send_sems, recv_sems):
        x, y, c = _dev()
        cps = [pltpu.make_async_remote_copy(src_ref=g_ref.at[2 * j + (1 - c)], dst_ref=out_ref.at[j],
                                            send_sem=send_sems.at[j], recv_sem=recv_sems.at[j],
                                            device_id=(x, y, 1 - c), device_id_type=MESH) for j in range(nj)]
        for cp in cps:
            cp.start()
        for cp in cps:
            cp.wait()

    return pl.pallas_call(
        body, out_shape=SDS((nj, R, W), g.dtype), in_specs=[HBM_SPEC], out_specs=HBM_SPEC,
        scratch_shapes=[pltpu.SemaphoreType.DMA((nj,)), pltpu.SemaphoreType.DMA((nj,))], name=name)(g)


def _add_pairs(g, ra, name):
    nd, R, W = g.shape
    nj = nd // 2
    tr = _tile(R, 256, 8)
    cidx = lax.axis_index("c").astype(jnp.int32).reshape(1)

    def body(c_ref, a_ref, b_ref, o_ref):
        o_ref[...] = (a_ref[...].astype(F32) + b_ref[...].astype(F32)).astype(o_ref.dtype)

    gs = pltpu.PrefetchScalarGridSpec(
        num_scalar_prefetch=1, grid=(nj, R // tr),
        in_specs=[pl.BlockSpec((None, tr, W), lambda j, i, cr: (2 * j + cr[0], i, 0)),
                  pl.BlockSpec((None, tr, W), lambda j, i, cr: (j, i, 0))],
        out_specs=pl.BlockSpec((None, tr, W), lambda j, i, cr: (j, i, 0)))
    return pl.pallas_call(body, grid_spec=gs, out_shape=SDS((nj, R, W), g.dtype), name=name,
                          compiler_params=pltpu.CompilerParams(vmem_limit_bytes=VMEM_LIMIT))(cidx, g, ra)


def _rs_chip_exchange(p, name):
    nj, R, W = p.shape

    def body(p_ref, out_ref, send_sems, recv_sems, local_sem):
        x, y, c = _dev()
        j0 = 2 * x + y
        chips = [(1 - x, y), (x, 1 - y), (1 - x, 1 - y)]
        mine = pltpu.make_async_copy(p_ref.at[j0], out_ref.at[j0], local_sem)
        mine.start()

        def copy(k, chip):
            return pltpu.make_async_remote_copy(
                src_ref=p_ref.at[2 * chip[0] + chip[1]], dst_ref=out_ref.at[j0],
                send_sem=send_sems.at[k], recv_sem=recv_sems.at[k], device_id=(*chip, c), device_id_type=MESH)

        sent = [copy(k, chip) for k, chip in enumerate(chips)]
        for cp in sent:
            cp.start()
        for k, chip in enumerate(chips):
            pltpu.make_async_remote_copy(
                src_ref=p_ref.at[j0], dst_ref=out_ref.at[2 * chip[0] + chip[1]],
                send_sem=send_sems.at[k], recv_sem=recv_sems.at[k], device_id=(*chip, c), device_id_type=MESH).wait_recv()
        for cp in sent:
            cp.wait_send()
        mine.wait()

    return pl.pallas_call(
        body, out_shape=SDS((nj, R, W), p.dtype), in_specs=[HBM_SPEC], out_specs=HBM_SPEC,
        scratch_shapes=[pltpu.SemaphoreType.DMA((3,)), pltpu.SemaphoreType.DMA((3,)), pltpu.SemaphoreType.DMA],
        name=name)(p)


def _allgather_small(v, name):
    R, W = v.shape

    def body(v_ref, out_ref, send_sems, recv_sems, local_sem):
        x, y, c = _dev()
        mine = pltpu.make_async_copy(v_ref, out_ref.at[4 * x + 2 * y + c], local_sem)
        mine.start()
        peers = []
        for k in range(1, N_DEV):
            px = 1 - x if k & 4 else x
            py = 1 - y if k & 2 else y
            pc_ = 1 - c if k & 1 else c
            peers.append((px, py, pc_))
        sent = [pltpu.make_async_remote_copy(
            src_ref=v_ref, dst_ref=out_ref.at[4 * x + 2 * y + c], send_sem=send_sems.at[k], recv_sem=recv_sems.at[k],
            device_id=peer, device_id_type=MESH) for k, peer in enumerate(peers)]
        for cp in sent:
            cp.start()
        for k, (px, py, pc_) in enumerate(peers):
            pltpu.make_async_remote_copy(
                src_ref=v_ref, dst_ref=out_ref.at[4 * px + 2 * py + pc_], send_sem=send_sems.at[k],
                recv_sem=recv_sems.at[k], device_id=(px, py, pc_), device_id_type=MESH).wait_recv()
        for cp in sent:
            cp.wait_send()
        mine.wait()

    return pl.pallas_call(
        body, out_shape=SDS((N_DEV, R, W), v.dtype), in_specs=[VMEM_SPEC], out_specs=VMEM_SPEC,
        scratch_shapes=[pltpu.SemaphoreType.DMA((7,)), pltpu.SemaphoreType.DMA((7,)), pltpu.SemaphoreType.DMA],
        name=name)(v)


def _sum_slabs(a, name):
    ns, R, W = a.shape

    def body(a_ref, o_ref):
        acc = a_ref[0]
        for k in range(1, ns):
            acc = acc + a_ref[k]
        o_ref[...] = acc

    return pl.pallas_call(body, out_shape=SDS((R, W), a.dtype), in_specs=[VMEM_SPEC], out_specs=VMEM_SPEC, name=name)(a)


BIG = (("ada_w", "col"), ("w_in", "col"), ("w_ssd_out", "row"), ("w_sc_out", "row"), ("w_o", "row"), ("w_up", "col"),
       ("w_down", "row"))
CONVW = ("ssd_conv_w", "sc_conv_w", "ffn_conv_w")
REPL = ("ada_b", "mix_pre_g", "mix_post_g", "ssd_conv_b", "ssd_dt_bias", "ssd_a_log", "ssd_d", "ssd_norm_g", "ffn_pre_g",
        "ffn_post_g", "ffn_conv_b")
WEIGHTS = ("ada_w", "ada_b", "mix_pre_g", "mix_post_g", "w_in", "ssd_conv_w", "ssd_conv_b", "ssd_dt_bias", "ssd_a_log",
           "ssd_d", "ssd_norm_g", "w_ssd_out", "sc_conv_w", "w_sc_out", "w_o", "ffn_pre_g", "ffn_post_g", "w_up",
           "ffn_conv_w", "ffn_conv_b", "w_down")


def _pad_rows(a, mult):
    r = a.shape[-2]
    pad = -r % mult
    return a if pad == 0 else jnp.pad(a, [(0, 0)] * (a.ndim - 2) + [(0, pad), (0, 0)])


def _flat_rows(parts, mult):
    flat = jnp.concatenate([p.reshape(-1) for p in parts])
    flat = jnp.pad(flat, (0, -flat.shape[0] % ROW_W))
    return _pad_rows(flat.reshape(-1, ROW_W), mult)


def _unflat(buf, shapes):
    flat = buf.reshape(-1)
    out, o = [], 0
    for shp in shapes:
        n = 1
        for s in shp:
            n *= s
        out.append(flat[o:o + n].reshape(shp))
        o += n
    return out


def _pack_big_local(get, depth):
    return jnp.concatenate([_pad_rows((get(n)[l].T if kind == "col" else get(n)[l]).reshape(-1, ROW_W), SLAB_ALIGN)
                            for l in range(depth) for n, kind in BIG], axis=0)


def _big_rows(shapes, depth):
    out, o = {}, 0
    for l in range(depth):
        for n, _ in BIG:
            r = shapes[n][1] * shapes[n][2] // ROW_W
            out[(l, n)] = (o, o + r)
            o += -(-r // SLAB_ALIGN) * SLAB_ALIGN
    return out, o


def kernel(x, c, ada_w, ada_b, mix_pre_g, mix_post_g, w_in, ssd_conv_w, ssd_conv_b, ssd_dt_bias, ssd_a_log, ssd_d, ssd_norm_g, w_ssd_out, sc_conv_w, w_sc_out, w_o, ffn_pre_g, ffn_post_g, w_up, ffn_conv_w, ffn_conv_b, w_down, loss_target, m_ada_w, m_ada_b, m_mix_pre_g, m_mix_post_g, m_w_in, m_ssd_conv_w, m_ssd_conv_b, m_ssd_dt_bias, m_ssd_a_log, m_ssd_d, m_ssd_norm_g, m_w_ssd_out, m_sc_conv_w, m_w_sc_out, m_w_o, m_ffn_pre_g, m_ffn_post_g, m_w_up, m_ffn_conv_w, m_ffn_conv_b, m_w_down, v_ada_w, v_ada_b, v_mix_pre_g, v_mix_post_g, v_w_in, v_ssd_conv_w, v_ssd_conv_b, v_ssd_dt_bias, v_ssd_a_log, v_ssd_d, v_ssd_norm_g, v_w_ssd_out, v_sc_conv_w, v_w_sc_out, v_w_o, v_ffn_pre_g, v_ffn_post_g, v_w_up, v_ffn_conv_w, v_ffn_conv_b, v_w_down):
    wl = dict(zip(WEIGHTS, (ada_w, ada_b, mix_pre_g, mix_post_g, w_in, ssd_conv_w, ssd_conv_b, ssd_dt_bias, ssd_a_log,
                            ssd_d, ssd_norm_g, w_ssd_out, sc_conv_w, w_sc_out, w_o, ffn_pre_g, ffn_post_g, w_up,
                            ffn_conv_w, ffn_conv_b, w_down)))
    ml = dict(zip(WEIGHTS, (m_ada_w, m_ada_b, m_mix_pre_g, m_mix_post_g, m_w_in, m_ssd_conv_w, m_ssd_conv_b,
                            m_ssd_dt_bias, m_ssd_a_log, m_ssd_d, m_ssd_norm_g, m_w_ssd_out, m_sc_conv_w, m_w_sc_out, m_w_o,
                            m_ffn_pre_g, m_ffn_post_g, m_w_up, m_ffn_conv_w, m_ffn_conv_b, m_w_down)))
    vl = dict(zip(WEIGHTS, (v_ada_w, v_ada_b, v_mix_pre_g, v_mix_post_g, v_w_in, v_ssd_conv_w, v_ssd_conv_b,
                            v_ssd_dt_bias, v_ssd_a_log, v_ssd_d, v_ssd_norm_g, v_w_ssd_out, v_sc_conv_w, v_w_sc_out, v_w_o,
                            v_ffn_pre_g, v_ffn_post_g, v_w_up, v_ffn_conv_w, v_ffn_conv_b, v_w_down)))
    depth = ada_w.shape[0]
    shapes = {n: wl[n].shape for n in WEIGHTS}
    me = 4 * lax.axis_index("x") + 2 * lax.axis_index("y") + lax.axis_index("c")

    rows, n_big = _big_rows(shapes, depth)
    big_loc = _pack_big_local(lambda n: wl[n].astype(BF16), depth)
    conv_flat = jnp.concatenate([wl[n][l].reshape(-1) for l in range(depth) for n in CONVW])
    n_conv = conv_flat.shape[0]
    conv_flat = jnp.pad(conv_flat, (0, -n_conv % (ROW_W // 2)))
    conv_rows = lax.bitcast_convert_type(conv_flat, BF16).reshape(-1, ROW_W)
    gathered = _allgather_big(_pad_rows(jnp.concatenate([big_loc, conv_rows], axis=0), ROW_PAD), "allgather_weights")
    conv_all = lax.bitcast_convert_type(
        gathered[:, n_big:n_big + conv_rows.shape[0]].reshape(N_DEV, -1, 2), F32)[:, :n_conv]
    conv_full, o = {}, 0
    for l in range(depth):
        for n in CONVW:
            k, cl = shapes[n][1], shapes[n][2]
            conv_full[(l, n)] = conv_all[:, o:o + k * cl].reshape(N_DEV, k, cl).transpose(1, 0, 2).reshape(k, N_DEV * cl)
            o += k * cl
    layers = []
    for l in range(depth):
        w = {n: wl[n][l] for n in REPL}
        for n, kind in BIG:
            r0, r1 = rows[(l, n)]
            a, b = shapes[n][1], shapes[n][2]
            blk = gathered[:, r0:r1]
            w[n] = blk.reshape(N_DEV * b, a) if kind == "col" else blk.reshape(N_DEV * a, b)
        for n in CONVW:
            w[n] = conv_full[(l, n)]
        layers.append(w)

    loss_loc, dx, grads = _fwd_bwd(x, c, loss_target, layers)

    slabs = []
    for l in range(depth):
        for n, kind in BIG:
            slabs.append(_pad_rows(grads[l][n].astype(BF16).reshape(N_DEV, -1, ROW_W), SLAB_ALIGN))
    used_rows = sum(s.shape[1] for s in slabs)
    slabs.append(jnp.zeros((N_DEV, -used_rows % ROW_PAD, ROW_W), BF16))
    gslab = jnp.concatenate(slabs, axis=1)
    from_sibling = _rs_pair_exchange(gslab, "rs_pair_exchange")
    chip_sums = _add_pairs(gslab, from_sibling, "rs_pair_add")
    from_chips = _rs_chip_exchange(chip_sums, "rs_chip_exchange")
    g_sum = _sum_chips(from_chips, "rs_chip_sum")

    def grad_of(l, n, kind):
        blk = g_sum[rows[(l, n)][0]:rows[(l, n)][1]]
        a, b = shapes[n][1], shapes[n][2]
        return blk.reshape(b, a).T if kind == "col" else blk.reshape(a, b)

    g_big = {n: jnp.stack([grad_of(l, n, kind) for l in range(depth)]) for n, kind in BIG}
    d_big, m_big, v_big = {}, {}, {}
    for n, _ in BIG:
        d_big[n], m_big[n], v_big[n] = _adam_nat(g_big[n], wl[n], ml[n], vl[n], f"adam_{n}")

    parts = [jnp.broadcast_to(loss_loc, (ROW_W,))]
    small_shapes = [(ROW_W,)]
    for l in range(depth):
        for n in REPL + CONVW:
            parts.append(grads[l][n])
            small_shapes.append(tuple(grads[l][n].shape))
    total = _sum_slabs(_allgather_small(_flat_rows(parts, 8), "allgather_small"), "sum_small")
    pieces = _unflat(total, small_shapes)
    loss = pieces[0][0]
    g_small, i = {}, 1
    for l in range(depth):
        for n in REPL + CONVW:
            gp = pieces[i]
            i += 1
            if n in CONVW:
                gp = lax.dynamic_slice_in_dim(gp, me * shapes[n][2], shapes[n][2], axis=1)
            g_small[(l, n)] = gp
    order = [(l, n) for l in range(depth) for n in REPL + CONVW]
    loc_shapes = [tuple(shapes[n][1:]) for _, n in order]
    packs = lambda f: _flat_rows([f(l, n) for l, n in order], 8)
    gs_small = packs(lambda l, n: g_small[(l, n)])
    _, d_sm, m_sm, v_sm = _adam(gs_small[None], packs(lambda l, n: wl[n][l]), packs(lambda l, n: ml[n][l]),
                                packs(lambda l, n: vl[n][l]), "adam_small")

    def unpack_small(buf):
        ps = _unflat(buf, loc_shapes)
        return {n: jnp.stack([ps[order.index((l, n))] for l in range(depth)]) for n in REPL + CONVW}

    outs = []
    for big, small in ((g_big, {n: jnp.stack([g_small[(l, n)] for l in range(depth)]) for n in REPL + CONVW}),
                       (d_big, unpack_small(d_sm)), (m_big, unpack_small(m_sm)), (v_big, unpack_small(v_sm))):
        merged = {**big, **small}
        outs += [merged[n] for n in WEIGHTS]
    return (loss, dx, *outs)
```

```python
import functools
import math
from typing import Callable, NamedTuple

import jax
import jax.numpy as jnp
from jax import lax
from jax.experimental import pallas as pl
from jax.experimental.pallas import tpu as pltpu

F32, BF16 = jnp.float32, jnp.bfloat16
WGRAD = BF16
SDS = jax.ShapeDtypeStruct
MESH = pl.DeviceIdType.MESH

EPS = 1e-6
N_STATE = 128
CHUNK = 128
SSD_CONV_K, SC_CONV_K, FFN_CONV_K = 4, 3, 3
N_DEV = 8
ROW_W = 1024
ROW_PAD = 256
SLAB_ALIGN = 16
SEG_BLK = 512
STRIP = 32
FFN_STRIP = 64
GLU_W = 256
MOD_ROWS = 128
VMEM_LIMIT = 48 * 2**20

ADAM_LR, ADAM_B1, ADAM_B2, ADAM_EPS, ADAM_WD, ADAM_STEP = 0.001, 0.9, 0.999, 1e-08, 0.01, 10

NT = (((1,), (1,)), ((), ()))
TN = (((0,), (0,)), ((), ()))
NN = (((1,), (0,)), ((), ()))


def _tile(n, cap, mult=128):
    best = None
    for t in range(mult, min(n, cap) + 1, mult):
        if n % t == 0:
            best = t
    return best if best is not None else n


class _Side(NamedTuple):
    operands: tuple
    out_shape: tuple
    scratch: tuple
    start: Callable
    wait: Callable
    aliases: dict = {}


def _pc(body, *, grid, in_specs, out_specs, out_shape, name, scratch=(), side=None):
    params = pltpu.CompilerParams(dimension_semantics=("arbitrary",) * len(grid), vmem_limit_bytes=VMEM_LIMIT)
    if side is None:
        return pl.pallas_call(body, grid=grid, in_specs=in_specs, out_specs=out_specs, out_shape=out_shape,
                              scratch_shapes=list(scratch), name=name, compiler_params=params)
    single = not isinstance(out_shape, (list, tuple))
    outs = [out_shape] if single else list(out_shape)
    ospecs = [out_specs] if single else list(out_specs)
    n_in, n_out, n_scr = len(in_specs), len(outs), len(scratch)
    s_in, s_out = len(side.operands), len(side.out_shape)

    def hosted(*refs):
        ins, refs = refs[:n_in], refs[n_in:]
        sins, refs = refs[:s_in], refs[s_in:]
        mouts, refs = refs[:n_out], refs[n_out:]
        souts, refs = refs[:s_out], refs[s_out:]
        scr, sems = refs[:n_scr], refs[n_scr:]
        first = functools.reduce(lambda a, b: a & b, [pl.program_id(a) == 0 for a in range(len(grid))])
        last = functools.reduce(lambda a, b: a & b, [pl.program_id(a) == grid[a] - 1 for a in range(len(grid))])

        @pl.when(first)
        def _():
            side.start(sins, souts, sems)

        body(*ins, *mouts, *scr)

        @pl.when(last)
        def _():
            side.wait(sins, souts, sems)

    call = pl.pallas_call(
        hosted, grid=grid, in_specs=list(in_specs) + [HBM_SPEC] * s_in, out_specs=ospecs + [HBM_SPEC] * s_out,
        out_shape=outs + list(side.out_shape), scratch_shapes=list(scratch) + list(side.scratch), name=name,
        input_output_aliases={n_in + k: n_out + v for k, v in side.aliases.items()}, compiler_params=params)

    def run(*args):
        res = call(*args, *side.operands)
        main = res[0] if single else list(res[:n_out])
        return main, list(res[n_out:])

    return run


def _silu(x):
    return x * jax.nn.sigmoid(x)


def _dsilu(x):
    s = jax.nn.sigmoid(x)
    return s * (1.0 + x * (1.0 - s))


def _softplus(x):
    return jnp.maximum(x, 0.0) + jnp.log(1.0 + jnp.exp(-jnp.abs(x)))


def _dot(a, b, dims=NN):
    return lax.dot_general(a, b, dims, preferred_element_type=F32)


def _bsum(v, rows=8):
    return jnp.broadcast_to(v, (rows, v.shape[1]))


def _mm(a, b, mode, out_dtype, name, caps=(1024, 1024, 2048), side=None):
    if mode == "nn":
        (M, K), (K2, N) = a.shape, b.shape
    elif mode == "nt":
        (M, K), (N, K2) = a.shape, b.shape
    else:
        (K, M), (K2, N) = a.shape, b.shape
    assert K == K2, (a.shape, b.shape, mode)
    tm, tn, tk = _tile(M, caps[0]), _tile(N, caps[1]), _tile(K, caps[2])
    nk = K // tk
    dims = {"nn": NN, "nt": NT, "tn": TN}[mode]
    if mode == "tn":
        a_spec = pl.BlockSpec((tk, tm), lambda i, j, k: (k, i))
    else:
        a_spec = pl.BlockSpec((tm, tk), lambda i, j, k: (i, k))
    if mode == "nt":
        b_spec = pl.BlockSpec((tn, tk), lambda i, j, k: (j, k))
    else:
        b_spec = pl.BlockSpec((tk, tn), lambda i, j, k: (k, j))

    def body(a_ref, b_ref, o_ref, *acc):
        part = _dot(a_ref[...].astype(BF16), b_ref[...].astype(BF16), dims)
        if nk == 1:
            o_ref[...] = part.astype(o_ref.dtype)
        else:
            acc_ref, = acc
            k = pl.program_id(2)

            @pl.when(k == 0)
            def _():
                acc_ref[...] = part

            @pl.when(k > 0)
            def _():
                acc_ref[...] += part

            @pl.when(k == nk - 1)
            def _():
                o_ref[...] = acc_ref[...].astype(o_ref.dtype)

    return _pc(body, grid=(M // tm, N // tn, nk), in_specs=[a_spec, b_spec],
               out_specs=pl.BlockSpec((tm, tn), lambda i, j, k: (i, j)),
               out_shape=SDS((M, N), out_dtype), name=name,
               scratch=() if nk == 1 else (pltpu.VMEM((tm, tn), F32),), side=side)(a, b)


def _mm_seg(segs, b, mode, out_dtype, name, blk, tile=1024, tk=2048):
    nblk = [a.shape[1] // blk for a in segs]
    assert all(a.shape[1] % blk == 0 for a in segs)
    start = [sum(nblk[:s]) for s in range(len(segs))]
    total = sum(nblk)
    ns = len(segs)
    N = b.shape[1]
    tn = _tile(N, tile)
    if mode == "nn":
        M = segs[0].shape[0]
        tm = _tile(M, tile)
        grid = (M // tm, N // tn, total)
        a_specs = [pl.BlockSpec((tm, blk), lambda i, j, k, k0=k0, n=n: (i, jnp.clip(k - k0, 0, n - 1)))
                   for k0, n in zip(start, nblk)]
        b_spec = pl.BlockSpec((blk, tn), lambda i, j, k: (k, j))
        out_rows, tmo, dims, seg_axis = M, tm, NN, 2
    else:
        K = segs[0].shape[0]
        tkk = _tile(K, tk)
        grid = (total, N // tn, K // tkk)
        a_specs = [pl.BlockSpec((tkk, blk), lambda i, j, k, i0=i0, n=n: (
            jnp.where((i >= i0) & (i < i0 + n), k, 0), jnp.clip(i - i0, 0, n - 1))) for i0, n in zip(start, nblk)]
        b_spec = pl.BlockSpec((tkk, tn), lambda i, j, k: (k, j))
        out_rows, tmo, seg_axis = total * blk, blk, 0
    nk = grid[2]
    acc_shape = (tm, tn) if mode == "nn" else (tn, blk)

    def body(*refs):
        a_refs, b_ref, o_ref, acc_ref = refs[:ns], refs[ns], refs[ns + 1], refs[ns + 2]
        k = pl.program_id(2)
        sel = pl.program_id(seg_axis)

        @pl.when(k == 0)
        def _():
            acc_ref[...] = jnp.zeros_like(acc_ref)

        for s in range(ns):
            @pl.when((sel >= start[s]) & (sel < start[s] + nblk[s]))
            def _(s=s):
                a_, b_ = a_refs[s][...].astype(BF16), b_ref[...].astype(BF16)
                acc_ref[...] += _dot(a_, b_, NN) if mode == "nn" else _dot(b_, a_, TN)

        @pl.when(k == nk - 1)
        def _():
            acc = acc_ref[...]
            o_ref[...] = (acc if mode == "nn" else acc.T).astype(o_ref.dtype)

    return _pc(body, grid=grid, in_specs=a_specs + [b_spec], out_specs=pl.BlockSpec((tmo, tn), lambda i, j, k: (i, j)),
               out_shape=SDS((out_rows, N), out_dtype), name=name, scratch=(pltpu.VMEM(acc_shape, F32),))(*segs, b)


def _modk(c8, ada_w, ada_b, name):
    rows, D = c8.shape
    N = ada_w.shape[0]
    tn = _tile(N, 1536)

    def body(c_ref, w_ref, b_ref, mod_ref, ca_ref):
        ca = _silu(c_ref[...]).astype(BF16)
        mod_ref[...] = _dot(ca, w_ref[...], NT) + b_ref[...]
        ca_ref[...] = ca

    return _pc(body, grid=(N // tn,),
               in_specs=[pl.BlockSpec((rows, D), lambda j: (0, 0)), pl.BlockSpec((tn, D), lambda j: (j, 0)),
                         pl.BlockSpec((1, tn), lambda j: (0, j))],
               out_specs=[pl.BlockSpec((rows, tn), lambda j: (0, j)), pl.BlockSpec((rows, D), lambda j: (0, 0))],
               out_shape=[SDS((rows, N), F32), SDS((rows, D), BF16)], name=name)(c8, ada_w, ada_b)


def _row_tile(S):
    return _tile(S, 512, 8)


def _strips(tm, fn, init=0, rows=None):
    rows = STRIP if rows is None else rows
    assert tm % rows == 0
    return lax.fori_loop(0, tm // rows, lambda r, c: fn(pl.multiple_of(r * rows, rows), c), init)


def _rows8(rows):
    pad = 8 - len(rows)
    return jnp.concatenate(rows + ([jnp.zeros((pad, rows[0].shape[1]), F32)] if pad else []), axis=0)


def _fold8(v):
    return jnp.sum(v.reshape(v.shape[0] // 8, 8, v.shape[1]), axis=0)


def _norm_mod(x, g, mod3, sc_seg, sh_seg, S, name):
    T, D = x.shape
    tm = _row_tile(S)
    tpb = S // tm

    def body(x_ref, g_ref, sc_ref, sh_ref, h_ref):
        x_ = x_ref[...]
        r = lax.rsqrt(jnp.mean(x_ * x_, axis=-1, keepdims=True) + EPS)
        h_ref[...] = ((x_ * r) * (g_ref[...] * (1.0 + sc_ref[...])) + sh_ref[...]).astype(BF16)

    return _pc(body, grid=(T // tm,),
               in_specs=[pl.BlockSpec((tm, D), lambda i: (i, 0)), pl.BlockSpec((1, D), lambda i: (0, 0)),
                         pl.BlockSpec((None, 1, D), lambda i: (i // tpb, 0, sc_seg)),
                         pl.BlockSpec((None, 1, D), lambda i: (i // tpb, 0, sh_seg))],
               out_specs=pl.BlockSpec((tm, D), lambda i: (i, 0)), out_shape=SDS((T, D), BF16), name=name)(x, g, mod3, mod3)


def _resid_post(x, fo, mod3, gt_seg, pg, S, name):
    T, D = x.shape
    tm = _row_tile(S)
    tpb = S // tm

    def body(x_ref, f_ref, gt_ref, pg_ref, o_ref):
        f = f_ref[...]
        r = lax.rsqrt(jnp.mean(f * f, axis=-1, keepdims=True) + EPS)
        o_ref[...] = x_ref[...] + (f * r) * (gt_ref[...] * pg_ref[...])

    return _pc(body, grid=(T // tm,),
               in_specs=[pl.BlockSpec((tm, D), lambda i: (i, 0)), pl.BlockSpec((tm, D), lambda i: (i, 0)),
                         pl.BlockSpec((None, 1, D), lambda i: (i // tpb, 0, gt_seg)),
                         pl.BlockSpec((1, D), lambda i: (0, 0))],
               out_specs=pl.BlockSpec((tm, D), lambda i: (i, 0)), out_shape=SDS((T, D), F32), name=name)(x, fo, mod3, pg)


def _post_bwd(fo, mod3, gt_seg, pg, dout, S, name):
    T, D = fo.shape
    tm = _row_tile(S)
    tpb = S // tm
    nb = T // S

    def body(f_ref, gt_ref, pg_ref, d_ref, df_ref, dgt_ref, dpg_ref):
        i = pl.program_id(0)

        @pl.when(i == 0)
        def _():
            dpg_ref[...] = jnp.zeros_like(dpg_ref)

        @pl.when(i % tpb == 0)
        def _():
            dgt_ref[...] = jnp.zeros_like(dgt_ref)

        f, d = f_ref[...], d_ref[...]
        r = lax.rsqrt(jnp.mean(f * f, axis=-1, keepdims=True) + EPS)
        n = f * r
        dn = d * (gt_ref[...] * pg_ref[...])
        df_ref[...] = (r * (dn - n * jnp.mean(dn * n, axis=-1, keepdims=True))).astype(df_ref.dtype)
        tot = jnp.sum(d * n, axis=0, keepdims=True)
        dgt_ref[...] += _bsum(tot * pg_ref[...])
        dpg_ref[...] += _bsum(tot * gt_ref[...])

    return _pc(body, grid=(T // tm,),
               in_specs=[pl.BlockSpec((tm, D), lambda i: (i, 0)),
                         pl.BlockSpec((None, 1, D), lambda i: (i // tpb, 0, gt_seg)),
                         pl.BlockSpec((1, D), lambda i: (0, 0)), pl.BlockSpec((tm, D), lambda i: (i, 0))],
               out_specs=[pl.BlockSpec((tm, D), lambda i: (i, 0)), pl.BlockSpec((8, D), lambda i: (i // tpb, 0)),
                          pl.BlockSpec((8, D), lambda i: (0, 0))],
               out_shape=[SDS((T, D), BF16), SDS((nb * 8, D), F32), SDS((8, D), F32)], name=name)(fo, mod3, pg, dout)


def _pre_bwd(x, g, mod3, sc_seg, dh, dout, S, name):
    T, D = x.shape
    tm = _row_tile(S)
    tpb = S // tm
    nb = T // S

    def body(x_ref, g_ref, sc_ref, dh_ref, d_ref, dx_ref, dg_ref, dsc_ref, dsh_ref):
        i = pl.program_id(0)

        @pl.when(i == 0)
        def _():
            dg_ref[...] = jnp.zeros_like(dg_ref)

        @pl.when(i % tpb == 0)
        def _():
            dsc_ref[...] = jnp.zeros_like(dsc_ref)
            dsh_ref[...] = jnp.zeros_like(dsh_ref)

        x_, dh_ = x_ref[...], dh_ref[...]
        r = lax.rsqrt(jnp.mean(x_ * x_, axis=-1, keepdims=True) + EPS)
        n = x_ * r
        dn = dh_ * (g_ref[...] * (1.0 + sc_ref[...]))
        dx_ref[...] = d_ref[...] + r * (dn - n * jnp.mean(dn * n, axis=-1, keepdims=True))
        dhn = jnp.sum(dh_ * n, axis=0, keepdims=True)
        dg_ref[...] += _bsum(dhn * (1.0 + sc_ref[...]))
        dsc_ref[...] += _bsum(dhn * g_ref[...])
        dsh_ref[...] += _bsum(jnp.sum(dh_, axis=0, keepdims=True))

    row = pl.BlockSpec((tm, D), lambda i: (i, 0))
    return _pc(body, grid=(T // tm,),
               in_specs=[row, pl.BlockSpec((1, D), lambda i: (0, 0)),
                         pl.BlockSpec((None, 1, D), lambda i: (i // tpb, 0, sc_seg)), row, row],
               out_specs=[row, pl.BlockSpec((8, D), lambda i: (0, 0)), pl.BlockSpec((8, D), lambda i: (i // tpb, 0)),
                          pl.BlockSpec((8, D), lambda i: (i // tpb, 0))],
               out_shape=[SDS((T, D), F32), SDS((8, D), F32), SDS((nb * 8, D), F32), SDS((nb * 8, D), F32)],
               name=name)(x, g, mod3, dh, dout)


def _loss(y, target, S, name):
    T, D = y.shape
    tm = _row_tile(S)

    def body(y_ref, t_ref, dy_ref, l_ref):
        @pl.when(pl.program_id(0) == 0)
        def _():
            l_ref[...] = jnp.zeros_like(l_ref)

        def strip(r0, carry):
            rows = pl.ds(r0, STRIP)
            e = y_ref[rows, :] - t_ref[rows, :]
            dy_ref[rows, :] = e * (1.0 / D)
            return carry + _fold8(e * e)

        acc = _strips(tm, strip, jnp.zeros((8, D), F32))
        l_ref[...] += jnp.broadcast_to(jnp.sum(acc, keepdims=True) * (0.5 / D), l_ref.shape)

    row = pl.BlockSpec((tm, D), lambda i: (i, 0))
    return _pc(body, grid=(T // tm,), in_specs=[row, row],
               out_specs=[row, pl.BlockSpec((8, 128), lambda i: (0, 0))],
               out_shape=[SDS((T, D), F32), SDS((8, 128), F32)], name=name)(y, target)


def _conv_geom(view, C, S):
    arr, off = view
    T = arr.shape[0]
    tm = _row_tile(S)
    tc = _tile(C, 512)
    assert off % tc == 0 and C % tc == 0
    return arr, off // tc, T, tm, tc, S // tm


def _prev_spec(tm, tc, ob, order):
    if order == "ij":
        return pl.BlockSpec((8, tc), lambda i, j: (jnp.maximum(i * (tm // 8) - 1, 0), ob + j))
    return pl.BlockSpec((8, tc), lambda j, i: (jnp.maximum(i * (tm // 8) - 1, 0), ob + j))


def _next_spec(T, tm, tc, ob, order):
    last = T // 8 - 1
    if order == "ij":
        return pl.BlockSpec((8, tc), lambda i, j: (jnp.minimum((i + 1) * (tm // 8), last), ob + j))
    return pl.BlockSpec((8, tc), lambda j, i: (jnp.minimum((i + 1) * (tm // 8), last), ob + j))


def _taps(win, w_ref, K, lead, rows):
    acc = win[lead:lead + rows] * w_ref[K - 1:K, :]
    for j in range(1, K):
        acc = acc + win[lead - j:lead - j + rows] * w_ref[K - 1 - j:K - j, :]
    return acc


def _taps_t(win, w_ref, K, rows):
    acc = win[0:rows] * w_ref[K - 1:K, :]
    for j in range(1, K):
        acc = acc + win[j:j + rows] * w_ref[K - 1 - j:K - j, :]
    return acc


def _conv_fwd(view, C, w8, b, K, S, name):
    arr, ob, T, tm, tc, tps = _conv_geom(view, C, S)

    def body(u_ref, p_ref, w_ref, b_ref, o_ref, buf):
        first = (pl.program_id(0) % tps) == 0
        buf[0:8, :] = jnp.where(first, 0.0, p_ref[...])
        buf[8:, :] = u_ref[...]

        def strip(r0, carry):
            win = buf[pl.ds(r0, STRIP + 8), :]
            o_ref[pl.ds(r0, STRIP), :] = _taps(win, w_ref, K, 8, STRIP) + b_ref[...]
            return carry

        _strips(tm, strip)

    return _pc(body, grid=(T // tm, C // tc),
               in_specs=[pl.BlockSpec((tm, tc), lambda i, j: (i, ob + j)), _prev_spec(tm, tc, ob, "ij"),
                         pl.BlockSpec((8, tc), lambda i, j: (0, j)), pl.BlockSpec((1, tc), lambda i, j: (0, j))],
               out_specs=pl.BlockSpec((tm, tc), lambda i, j: (i, j)), out_shape=SDS((T, C), F32), name=name,
               scratch=(pltpu.VMEM((tm + 8, tc), F32),))(arr, arr, w8, b)


def _conv_bwd_in(dview, C, w8, K, S, out_dtype, name):
    arr, ob, T, tm, tc, tps = _conv_geom(dview, C, S)

    def body(d_ref, n_ref, w_ref, o_ref, buf):
        last = (pl.program_id(0) % tps) == tps - 1
        buf[0:tm, :] = d_ref[...]
        buf[tm:tm + 8, :] = jnp.where(last, 0.0, n_ref[...])

        def strip(r0, carry):
            win = buf[pl.ds(r0, STRIP + 8), :]
            o_ref[pl.ds(r0, STRIP), :] = _taps_t(win, w_ref, K, STRIP).astype(o_ref.dtype)
            return carry

        _strips(tm, strip)

    return _pc(body, grid=(T // tm, C // tc),
               in_specs=[pl.BlockSpec((tm, tc), lambda i, j: (i, ob + j)), _next_spec(T, tm, tc, ob, "ij"),
                         pl.BlockSpec((8, tc), lambda i, j: (0, j))],
               out_specs=pl.BlockSpec((tm, tc), lambda i, j: (i, j)), out_shape=SDS((T, C), out_dtype), name=name,
               scratch=(pltpu.VMEM((tm + 8, tc), F32),))(arr, arr, w8)


def _conv_bwd_w(dview, uview, C, K, S, name):
    darr, dob, T, tm, tc, tps = _conv_geom(dview, C, S)
    uarr, uob, _, _, _, _ = _conv_geom(uview, C, S)

    def body(d_ref, u_ref, p_ref, o_ref, buf):
        i = pl.program_id(1)

        @pl.when(i == 0)
        def _():
            o_ref[...] = jnp.zeros_like(o_ref)

        first = (i % tps) == 0
        buf[0:8, :] = jnp.where(first, 0.0, p_ref[...])
        buf[8:, :] = u_ref[...]

        def strip(r0, carry):
            win = buf[pl.ds(r0, STRIP + 8), :]
            d = d_ref[pl.ds(r0, STRIP), :]
            sums = [_fold8(d * win[8 - (K - 1 - k):8 - (K - 1 - k) + STRIP]) for k in range(K)] + [_fold8(d)]
            return tuple(c + s for c, s in zip(carry, sums))

        acc = _strips(tm, strip, tuple(jnp.zeros((8, tc), F32) for _ in range(K + 1)))
        o_ref[...] += _rows8([jnp.sum(a, axis=0, keepdims=True) for a in acc])

    return _pc(body, grid=(C // tc, T // tm),
               in_specs=[pl.BlockSpec((tm, tc), lambda j, i: (i, dob + j)),
                         pl.BlockSpec((tm, tc), lambda j, i: (i, uob + j)), _prev_spec(tm, tc, uob, "ji")],
               out_specs=pl.BlockSpec((8, tc), lambda j, i: (0, j)), out_shape=SDS((8, C), F32), name=name,
               scratch=(pltpu.VMEM((tm + 8, tc), F32),))(darr, uarr, uarr)


def _ffn_act_fwd(uu, w8, b, S, name):
    K, gw = FFN_CONV_K, GLU_W
    T, F2 = uu.shape
    tm, tc = _row_tile(S), 2 * GLU_W
    tps = S // tm

    def body(u_ref, p_ref, w_ref, b_ref, a_ref, buf):
        first = (pl.program_id(0) % tps) == 0
        buf[0:8, :] = jnp.where(first, 0.0, p_ref[...])
        buf[8:, :] = u_ref[...]

        def strip(r0, carry):
            u = _taps(buf[pl.ds(r0, STRIP + 8), :], w_ref, K, 8, STRIP) + b_ref[...]
            a_ref[pl.ds(r0, STRIP), :] = (_silu(u[:, :gw]) * u[:, gw:]).astype(BF16)
            return carry

        _strips(tm, strip)

    return _pc(body, grid=(T // tm, F2 // tc),
               in_specs=[pl.BlockSpec((tm, tc), lambda i, j: (i, j)), _prev_spec(tm, tc, 0, "ij"),
                         pl.BlockSpec((8, tc), lambda i, j: (0, j)), pl.BlockSpec((1, tc), lambda i, j: (0, j))],
               out_specs=pl.BlockSpec((tm, gw), lambda i, j: (i, j)), out_shape=SDS((T, F2 // 2), BF16), name=name,
               scratch=(pltpu.VMEM((tm + 8, tc), F32),))(uu, uu, w8, b)


def _ffn_act_bwd(uu, da, w8, b, S, name, side=None):
    K, gw = FFN_CONV_K, GLU_W
    T, F2 = uu.shape
    tm, tc = _row_tile(S), 2 * GLU_W
    tps = S // tm
    last16 = T // 16 - 1

    def body(u_ref, p_ref, n_ref, da_ref, dan_ref, w_ref, b_ref, duu_ref, cw_ref, ubuf, dabuf):
        i = pl.program_id(1)

        @pl.when(i == 0)
        def _():
            cw_ref[...] = jnp.zeros_like(cw_ref)

        first = (i % tps) == 0
        last = (i % tps) == tps - 1
        ubuf[0:8, :] = jnp.where(first, 0.0, p_ref[...])
        ubuf[8:tm + 8, :] = u_ref[...]
        ubuf[tm + 8:tm + 16, :] = n_ref[...]
        dabuf[0:tm, :] = da_ref[...].astype(F32)
        dabuf[tm:tm + 8, :] = jnp.where(last, 0.0, dan_ref[...].astype(F32)[0:8, :])

        def strip(r0, carry):
            ext = FFN_STRIP + 8
            win = ubuf[pl.ds(r0, FFN_STRIP + 16), :]
            u = _taps(win, w_ref, K, 8, ext) + b_ref[...]
            da_ = dabuf[pl.ds(r0, ext), :]
            g, v = u[:, :gw], u[:, gw:]
            du = jnp.concatenate([da_ * v * _dsilu(g), da_ * _silu(g)], axis=1)
            duu_ref[pl.ds(r0, FFN_STRIP), :] = _taps_t(du, w_ref, K, FFN_STRIP).astype(BF16)
            dmain = du[0:FFN_STRIP]
            sums = [_fold8(dmain * win[8 - (K - 1 - k):8 - (K - 1 - k) + FFN_STRIP]) for k in range(K)] + [_fold8(dmain)]
            return tuple(c + s for c, s in zip(carry, sums))

        acc = _strips(tm, strip, tuple(jnp.zeros((8, tc), F32) for _ in range(K + 1)), rows=FFN_STRIP)
        cw_ref[...] += _rows8([jnp.sum(a, axis=0, keepdims=True) for a in acc])

    return _pc(body, grid=(F2 // tc, T // tm),
               in_specs=[pl.BlockSpec((tm, tc), lambda j, i: (i, j)), _prev_spec(tm, tc, 0, "ji"),
                         _next_spec(T, tm, tc, 0, "ji"), pl.BlockSpec((tm, gw), lambda j, i: (i, j)),
                         pl.BlockSpec((16, gw), lambda j, i: (jnp.minimum((i + 1) * (tm // 16), last16), j)),
                         pl.BlockSpec((8, tc), lambda j, i: (0, j)), pl.BlockSpec((1, tc), lambda j, i: (0, j))],
               out_specs=[pl.BlockSpec((tm, tc), lambda j, i: (i, j)), pl.BlockSpec((8, tc), lambda j, i: (0, j))],
               out_shape=[SDS((T, F2), BF16), SDS((8, F2), F32)], name=name,
               scratch=(pltpu.VMEM((tm + 16, tc), F32), pltpu.VMEM((tm + 8, gw), F32)), side=side)(
                   uu, uu, uu, da, da, w8, b)


def _ssd_common(dtc_raw, dtr_raw, hpc, hpr, L):
    dt_c = _softplus(dtc_raw + hpc[0:1, :])
    a_c = -jnp.exp(hpc[1:2, :])
    dt_r = _softplus(dtr_raw + hpr[:, 0:1])
    a_r = -jnp.exp(hpr[:, 1:2])
    li = lax.broadcasted_iota(jnp.int32, (L, L), 0)
    si = lax.broadcasted_iota(jnp.int32, (L, L), 1)
    low = li >= si
    upp = li <= si
    acs_c = _dotx(low, dt_c * a_c, split="b")
    acs_r = _dotx(dt_r * a_r, upp)
    return dt_c, a_c, acs_c, acs_r, low, upp


def _dotx(a, b, split="a", parts=3, dims=NN):
    val, one = (a, b) if split == "a" else (b, a)
    one = one.astype(BF16)
    acc, rem = None, val
    for i in range(parts):
        piece = rem.astype(BF16)
        t = _dot(piece, one, dims) if split == "a" else _dot(one, piece, dims)
        acc = t if acc is None else acc + t
        if i + 1 < parts:
            rem = rem - piece.astype(F32)
    return acc


def _head_maps(R, P, L):
    RP = R * P
    sel = (lax.broadcasted_iota(jnp.int32, (RP, R), 0) // P == lax.broadcasted_iota(jnp.int32, (RP, R), 1)).astype(F32)
    selt = (lax.broadcasted_iota(jnp.int32, (R, RP), 1) // P == lax.broadcasted_iota(jnp.int32, (R, RP), 0)).astype(F32)
    colb = (lax.broadcasted_iota(jnp.int32, (R, R * L), 1) // L == lax.broadcasted_iota(jnp.int32, (R, R * L), 0)).astype(F32)
    return sel, selt, colb


def _pair_diag(mats, rhs_b, R, P):
    lanes = 2 * P
    lo = lax.broadcasted_iota(jnp.int32, (mats[0].shape[0], lanes), 1) < P
    out = []
    for q in range(R // 2):
        rp = rhs_b[:, q * lanes:(q + 1) * lanes]
        out.append(jnp.where(lo, _dot(mats[2 * q], rp), _dot(mats[2 * q + 1], rp)))
    return jnp.concatenate(out, axis=1) if len(out) > 1 else out[0]


def _ssd_specs(pre, off_x, off_b, off_c, G, R, P, nb, nc, rev):
    L, N, RP = CHUNK, N_STATE, R * P
    cidx = (lambda c: nc - 1 - c) if rev else (lambda c: c)
    xb, bb, cb = off_x // RP, off_b // N, off_c // N
    assert off_x % RP == 0 and off_b % N == 0 and off_c % N == 0
    row = lambda b, c: b * nc + cidx(c)
    return dict(
        x=pl.BlockSpec((L, RP), lambda g, b, c: (row(b, c), xb + g)),
        b=pl.BlockSpec((L, N), lambda g, b, c: (row(b, c), bb + g)),
        c=pl.BlockSpec((L, N), lambda g, b, c: (row(b, c), cb + g)),
        dtc=pl.BlockSpec((None, L, R), lambda g, b, c: (g, row(b, c), 0)),
        dtr=pl.BlockSpec((None, R, L), lambda g, b, c: (g, 0, row(b, c))),
        hpc=pl.BlockSpec((None, 8, R), lambda g, b, c: (g, 0, 0)),
        hpr=pl.BlockSpec((None, R, 8), lambda g, b, c: (g, 0, 0)),
        y=pl.BlockSpec((L, RP), lambda g, b, c: (row(b, c), g)),
        bc=pl.BlockSpec((L, N), lambda g, b, c: (row(b, c), g)),
        hs=pl.BlockSpec((None, None, N, RP), lambda g, b, c: (row(b, c), g, 0, 0)),
    )


def _ssd_fwd(pre, offs, dtc, dtr, hpc, hpr, G, R, P, S, name, side=None):
    T = pre.shape[0]
    L, N, RP = CHUNK, N_STATE, R * P
    nc, nb = S // L, T // S
    sp = _ssd_specs(pre, *offs, G, R, P, nb, nc, False)

    def body(px_ref, pb_ref, pc_ref, dtc_ref, dtr_ref, hpc_ref, hpr_ref, y_ref, hs_ref, hst):
        @pl.when(pl.program_id(2) == 0)
        def _():
            hst[...] = jnp.zeros_like(hst)

        xs, bm, cm = _silu(px_ref[...]), _silu(pb_ref[...]), _silu(pc_ref[...])
        hpc_ = hpc_ref[...]
        dt_c, _, acs_c, acs_r, low, _ = _ssd_common(dtc_ref[...], dtr_ref[...], hpc_, hpr_ref[...], L)
        _, selt, colb = _head_maps(R, P, L)
        dt_e, a_e, hp_e = _dotx(dt_c, selt), _dotx(acs_c, selt), _dotx(hpc_, selt)
        a_bc = _dotx(acs_c, colb)
        a_last = a_e[L - 1:L, :]
        bb, cb = bm.astype(BF16), cm.astype(BF16)
        gm = _dot(cb, bb, NT)
        hprev = hst[...]
        hprev_b = hprev.astype(BF16)
        hs_ref[...] = hprev_b
        xdt = xs * dt_e
        xdt_b = xdt.astype(BF16)
        ms = []
        for r in range(R):
            dec = jnp.exp(jnp.where(low, a_bc[:, r * L:(r + 1) * L] - acs_r[r:r + 1, :], -jnp.inf))
            ms.append((gm * dec).astype(BF16))
        y = _pair_diag(ms, xdt_b, R, P) + _dot(cb, hprev_b) * jnp.exp(a_e) + hp_e[2:3, :] * xs
        y_ref[...] = y
        xw = (xdt * jnp.exp(a_last - a_e)).astype(BF16)
        hst[...] = hprev * jnp.exp(a_last) + _dot(bb, xw, TN)

    return _pc(body, grid=(G, nb, nc),
               in_specs=[sp["x"], sp["b"], sp["c"], sp["dtc"], sp["dtr"], sp["hpc"], sp["hpr"]],
               out_specs=[sp["y"], sp["hs"]],
               out_shape=[SDS((T, G * RP), F32), SDS((nb * nc, G, N, RP), BF16)], name=name,
               scratch=(pltpu.VMEM((N, RP), F32),), side=side)(pre, pre, pre, dtc, dtr, hpc, hpr)


def _ssd_bwd(pre, offs, dtc, dtr, hpc, hpr, hs, dy, G, R, P, S, name):
    T = pre.shape[0]
    L, N, RP = CHUNK, N_STATE, R * P
    nc, nb = S // L, T // S
    sp = _ssd_specs(pre, *offs, G, R, P, nb, nc, True)

    def body(px_ref, pb_ref, pc_ref, dtc_ref, dtr_ref, hpc_ref, hpr_ref, hs_ref, dy_ref,
             dpx_ref, dpb_ref, dpc_ref, ddt_ref, hpg_ref, dhst):
        bi, ci = pl.program_id(1), pl.program_id(2)

        @pl.when(ci == 0)
        def _():
            dhst[...] = jnp.zeros_like(dhst)

        @pl.when((bi == 0) & (ci == 0))
        def _():
            hpg_ref[...] = jnp.zeros_like(hpg_ref)

        px, pb, pcc = px_ref[...], pb_ref[...], pc_ref[...]
        xs, bm, cm = _silu(px), _silu(pb), _silu(pcc)
        hpc_ = hpc_ref[...]
        dtc_raw = dtc_ref[...]
        dt_c, a_c, acs_c, acs_r, low, upp = _ssd_common(dtc_raw, dtr_ref[...], hpc_, hpr_ref[...], L)
        sel, selt, colb = _head_maps(R, P, L)
        dt_e, a_e, hp_e = _dotx(dt_c, selt), _dotx(acs_c, selt), _dotx(hpc_, selt)
        a_bc = _dotx(acs_c, colb)
        a_last = a_e[L - 1:L, :]
        e_e, w_e = jnp.exp(a_e), jnp.exp(a_last - a_e)
        bb, cb = bm.astype(BF16), cm.astype(BF16)
        gm = _dot(cb, bb, NT)
        gmt = _dot(bb, cb, NT)
        hprev = hs_ref[...]
        dhn = dhst[...]
        dhn_b = dhn.astype(BF16)
        dy = dy_ref[...]
        dy_b = dy.astype(BF16)
        xdt = xs * dt_e
        xdt_b = xdt.astype(BF16)
        yoff = _dot(cb, hprev) * e_e
        dye_b = (dy * e_e).astype(BF16)
        dcm = _dot(dye_b, hprev, NT)
        dhst[...] = _dot(cb, dye_b, TN) + jnp.exp(a_last) * dhn
        dxdt_st = _dot(bb, dhn_b) * w_e
        dbm = _dot((xdt * w_e).astype(BF16), dhn_b, NT)
        lanes = 2 * P
        lo = lax.broadcasted_iota(jnp.int32, (L, lanes), 1) < P
        dg = jnp.zeros((L, L), F32)
        es, css = [], []
        for r in range(R):
            col_b, row = a_bc[:, r * L:(r + 1) * L], acs_r[r:r + 1, :]
            dec = jnp.exp(jnp.where(low, col_b - row, -jnp.inf))
            q = r // 2
            dyp = dy_b[:, q * lanes:(q + 1) * lanes]
            dyp = jnp.where(lo if r % 2 == 0 else ~lo, dyp, jnp.zeros_like(dyp))
            dm = _dot(dyp, xdt_b[:, q * lanes:(q + 1) * lanes], NT)
            dg = dg + dm * dec
            e = dm * (gm * dec)
            es.append(e)
            css.append(jnp.sum(e, axis=0, keepdims=True))
        dgb = dg.astype(BF16)
        dcm = dcm + _dot(dgb, bb)
        dbm = dbm + _dot(dgb, cb, TN)
        colbt = (lax.broadcasted_iota(jnp.int32, (R * L, R), 0) // L
                 == lax.broadcasted_iota(jnp.int32, (R * L, R), 1)).astype(F32)
        eye = (lax.broadcasted_iota(jnp.int32, (R, R), 0) == lax.broadcasted_iota(jnp.int32, (R, R), 1)).astype(F32)
        row_sums = _dotx(jnp.concatenate(es, axis=1), colbt)
        col_sums = _dotx(jnp.concatenate(css, axis=0), eye, dims=TN)
        mts = []
        for r in range(R):
            dect = jnp.exp(jnp.where(upp, acs_r[r:r + 1, :] - a_bc[:, r * L:(r + 1) * L], -jnp.inf))
            mts.append((gmt * dect).astype(BF16))
        dxdt = _pair_diag(mts, dy_b, R, P) + dxdt_st
        q_st = _dotx(xdt * dxdt_st, sel, parts=2)
        da = row_sums - col_sums + _dotx(dy * yoff, sel, parts=2) - q_st
        hh = jnp.sum(_dotx(dhn * hprev.astype(F32), sel, parts=2), axis=0, keepdims=True)
        da_last = jnp.exp(acs_c[L - 1:L, :]) * hh + jnp.sum(q_st, axis=0, keepdims=True)
        rowi = lax.broadcasted_iota(jnp.int32, (L, R), 0)
        da = da + jnp.where(rowi == L - 1, da_last, 0.0)
        dpx_ref[...] = (dxdt * dt_e + hp_e[2:3, :] * dy) * _dsilu(px)
        dpb_ref[...] = dbm * _dsilu(pb)
        dpc_ref[...] = dcm * _dsilu(pcc)
        dadt = _dotx(upp, da, split="b")
        ddt = _dotx(dxdt * xs, sel, parts=2) + dadt * a_c
        ddt_raw = ddt * jax.nn.sigmoid(dtc_raw + hpc_[0:1, :])
        ddt_ref[...] = ddt_raw
        d_a = jnp.sum(dadt * dt_c, axis=0, keepdims=True)
        d_d = jnp.sum(_dotx(dy * xs, sel, parts=2), axis=0, keepdims=True)
        rows = [jnp.sum(ddt_raw, axis=0, keepdims=True), d_a * a_c, d_d, jnp.zeros((5, R), F32)]
        hpg_ref[...] += jnp.concatenate(rows, axis=0)

    return _pc(body, grid=(G, nb, nc),
               in_specs=[sp["x"], sp["b"], sp["c"], sp["dtc"], sp["dtr"], sp["hpc"], sp["hpr"], sp["hs"], sp["y"]],
               out_specs=[sp["y"], sp["bc"], sp["bc"], sp["dtc"], pl.BlockSpec((None, 8, R), lambda g, b, c: (g, 0, 0))],
               out_shape=[SDS((T, G * RP), F32), SDS((T, G * N), F32), SDS((T, G * N), F32), SDS((G, T, R), F32),
                          SDS((G, 8, R), F32)], name=name,
               scratch=(pltpu.VMEM((N, RP), F32),))(pre, pre, pre, dtc, dtr, hpc, hpr, hs, dy)


def _gate_norm_fwd(y, zview, ng, G, S, name):
    T, DI = y.shape
    zarr, zoff = zview
    gw = DI // G
    tm = _row_tile(S)
    zb = zoff // gw
    assert zoff % gw == 0

    def body(y_ref, z_ref, g_ref, o_ref):
        yg = y_ref[...] * _silu(z_ref[...])
        r = lax.rsqrt(jnp.mean(yg * yg, axis=-1, keepdims=True) + EPS)
        o_ref[...] = (yg * r * g_ref[...]).astype(BF16)

    return _pc(body, grid=(T // tm, G),
               in_specs=[pl.BlockSpec((tm, gw), lambda i, g: (i, g)), pl.BlockSpec((tm, gw), lambda i, g: (i, zb + g)),
                         pl.BlockSpec((1, gw), lambda i, g: (0, g))],
               out_specs=pl.BlockSpec((tm, gw), lambda i, g: (i, g)), out_shape=SDS((T, DI), BF16), name=name)(y, zarr, ng)


def _gate_norm_bwd(y, zview, ng, dyn, G, S, name):
    T, DI = y.shape
    zarr, zoff = zview
    gw = DI // G
    tm = _row_tile(S)
    zb = zoff // gw

    def body(y_ref, z_ref, g_ref, d_ref, dy_ref, dz_ref, dg_ref):
        @pl.when(pl.program_id(1) == 0)
        def _():
            dg_ref[...] = jnp.zeros_like(dg_ref)

        y_, z, d = y_ref[...], z_ref[...], d_ref[...]
        sz = _silu(z)
        yg = y_ * sz
        r = lax.rsqrt(jnp.mean(yg * yg, axis=-1, keepdims=True) + EPS)
        n = yg * r
        dn = d * g_ref[...]
        dyg = r * (dn - n * jnp.mean(dn * n, axis=-1, keepdims=True))
        dy_ref[...] = dyg * sz
        dz_ref[...] = (dyg * y_ * _dsilu(z)).astype(BF16)
        dg_ref[...] += _bsum(jnp.sum(d * n, axis=0, keepdims=True))

    return _pc(body, grid=(G, T // tm),
               in_specs=[pl.BlockSpec((tm, gw), lambda g, i: (i, g)), pl.BlockSpec((tm, gw), lambda g, i: (i, zb + g)),
                         pl.BlockSpec((1, gw), lambda g, i: (0, g)), pl.BlockSpec((tm, gw), lambda g, i: (i, g))],
               out_specs=[pl.BlockSpec((tm, gw), lambda g, i: (i, g)), pl.BlockSpec((tm, gw), lambda g, i: (i, g)),
                          pl.BlockSpec((8, gw), lambda g, i: (0, g))],
               out_shape=[SDS((T, DI), F32), SDS((T, DI), BF16), SDS((8, DI), F32)], name=name)(y, zarr, ng, dyn)


def _shortconv_fwd(proj, off_b, off_c, off_h, C, w8, S, name):
    K = SC_CONV_K
    _, ob, T, tm, tc, tps = _conv_geom((proj, off_b), C, S)
    oc, oh = off_c // tc, off_h // tc

    def body(b_ref, c_ref, h_ref, cp_ref, hp_ref, w_ref, o_ref, buf):
        first = (pl.program_id(0) % tps) == 0
        buf[0:8, :] = jnp.where(first, 0.0, cp_ref[...] * hp_ref[...])
        buf[8:, :] = c_ref[...] * h_ref[...]

        def strip(r0, carry):
            conv = _taps(buf[pl.ds(r0, STRIP + 8), :], w_ref, K, 8, STRIP)
            o_ref[pl.ds(r0, STRIP), :] = (b_ref[pl.ds(r0, STRIP), :] * conv).astype(BF16)
            return carry

        _strips(tm, strip)

    blk = lambda o: pl.BlockSpec((tm, tc), lambda i, j: (i, o + j))
    return _pc(body, grid=(T // tm, C // tc),
               in_specs=[blk(ob), blk(oc), blk(oh), _prev_spec(tm, tc, oc, "ij"), _prev_spec(tm, tc, oh, "ij"),
                         pl.BlockSpec((8, tc), lambda i, j: (0, j))],
               out_specs=pl.BlockSpec((tm, tc), lambda i, j: (i, j)), out_shape=SDS((T, C), BF16), name=name,
               scratch=(pltpu.VMEM((tm + 8, tc), F32),))(proj, proj, proj, proj, proj, w8)


def _shortconv_bwd(proj, off_b, off_c, off_h, C, w8, dsc, S, name):
    K = SC_CONV_K
    _, ob, T, tm, tc, tps = _conv_geom((proj, off_b), C, S)
    oc, oh = off_c // tc, off_h // tc

    def body(b_ref, c_ref, h_ref, cp_ref, hp_ref, bn_ref, d_ref, dn_ref, w_ref,
             db_ref, dc_ref, dh_ref, dw_ref, buf, buf2):
        i = pl.program_id(1)

        @pl.when(i == 0)
        def _():
            dw_ref[...] = jnp.zeros_like(dw_ref)

        first = (i % tps) == 0
        last = (i % tps) == tps - 1
        buf[0:8, :] = jnp.where(first, 0.0, cp_ref[...] * hp_ref[...])
        buf[8:, :] = c_ref[...] * h_ref[...]
        buf2[0:tm, :] = d_ref[...] * b_ref[...]
        buf2[tm:tm + 8, :] = jnp.where(last, 0.0, dn_ref[...] * bn_ref[...])

        def strip(r0, carry):
            rows = pl.ds(r0, STRIP)
            vwin = buf[pl.ds(r0, STRIP + 8), :]
            db_ref[rows, :] = (d_ref[rows, :] * _taps(vwin, w_ref, K, 8, STRIP)).astype(BF16)
            dwin = buf2[pl.ds(r0, STRIP + 8), :]
            dv = _taps_t(dwin, w_ref, K, STRIP)
            dc_ref[rows, :] = (dv * h_ref[rows, :]).astype(BF16)
            dh_ref[rows, :] = (dv * c_ref[rows, :]).astype(BF16)
            dconv = dwin[0:STRIP]
            sums = [_fold8(dconv * vwin[8 - (K - 1 - k):8 - (K - 1 - k) + STRIP]) for k in range(K)]
            return tuple(c + s for c, s in zip(carry, sums))

        acc = _strips(tm, strip, tuple(jnp.zeros((8, tc), F32) for _ in range(K)))
        dw_ref[...] += _rows8([jnp.sum(a, axis=0, keepdims=True) for a in acc])

    blk = lambda o: pl.BlockSpec((tm, tc), lambda j, i: (i, o + j))
    out = pl.BlockSpec((tm, tc), lambda j, i: (i, j))
    return _pc(body, grid=(C // tc, T // tm),
               in_specs=[blk(ob), blk(oc), blk(oh), _prev_spec(tm, tc, oc, "ji"), _prev_spec(tm, tc, oh, "ji"),
                         _next_spec(T, tm, tc, ob, "ji"), blk(0), _next_spec(T, tm, tc, 0, "ji"),
                         pl.BlockSpec((8, tc), lambda j, i: (0, j))],
               out_specs=[out, out, out, pl.BlockSpec((8, tc), lambda j, i: (0, j))],
               out_shape=[SDS((T, C), BF16)] * 3 + [SDS((8, C), F32)], name=name,
               scratch=(pltpu.VMEM((tm + 8, tc), F32), pltpu.VMEM((tm + 8, tc), F32)))(
                   proj, proj, proj, proj, proj, proj, dsc, dsc, w8)


def _merge_fwd(proj, off_g1, off_g2, y1, y2, S, name):
    T, D = y1.shape
    tm = _row_tile(S)
    o1, o2 = off_g1 // D, off_g2 // D
    assert off_g1 % D == 0 and off_g2 % D == 0

    def body(g1_ref, g2_ref, y1_ref, y2_ref, o_ref):
        def strip(r0, carry):
            rows = pl.ds(r0, STRIP)
            o_ref[rows, :] = (jax.nn.sigmoid(g1_ref[rows, :]) * y1_ref[rows, :]
                              + jax.nn.sigmoid(g2_ref[rows, :]) * y2_ref[rows, :]).astype(BF16)
            return carry

        _strips(tm, strip)

    row = pl.BlockSpec((tm, D), lambda i: (i, 0))
    return _pc(body, grid=(T // tm,),
               in_specs=[pl.BlockSpec((tm, D), lambda i: (i, o1)), pl.BlockSpec((tm, D), lambda i: (i, o2)), row, row],
               out_specs=row, out_shape=SDS((T, D), BF16), name=name)(proj, proj, y1, y2)


def _merge_bwd(proj, off_g1, off_g2, y1, y2, dm, S, name):
    T, D = y1.shape
    tm = _row_tile(S)
    o1, o2 = off_g1 // D, off_g2 // D

    def body(g1_ref, g2_ref, y1_ref, y2_ref, d_ref, dy1_ref, dy2_ref, dg1_ref, dg2_ref):
        def strip(r0, carry):
            rows = pl.ds(r0, STRIP)
            d = d_ref[rows, :]
            s1, s2 = jax.nn.sigmoid(g1_ref[rows, :]), jax.nn.sigmoid(g2_ref[rows, :])
            dy1_ref[rows, :] = (d * s1).astype(BF16)
            dy2_ref[rows, :] = (d * s2).astype(BF16)
            dg1_ref[rows, :] = (d * y1_ref[rows, :] * s1 * (1.0 - s1)).astype(BF16)
            dg2_ref[rows, :] = (d * y2_ref[rows, :] * s2 * (1.0 - s2)).astype(BF16)
            return carry

        _strips(tm, strip)

    row = pl.BlockSpec((tm, D), lambda i: (i, 0))
    return _pc(body, grid=(T // tm,),
               in_specs=[pl.BlockSpec((tm, D), lambda i: (i, o1)), pl.BlockSpec((tm, D), lambda i: (i, o2)), row, row, row],
               out_specs=[row] * 4, out_shape=[SDS((T, D), BF16)] * 4, name=name)(proj, proj, y1, y2, dm)


def _pad8(w):
    return jnp.pad(w, ((0, 8 - w.shape[0]), (0, 0)))


def _dims(w):
    D = w["mix_pre_g"].shape[-1]
    DI = w["ssd_norm_g"].shape[-1]
    H = w["ssd_dt_bias"].shape[-1]
    conv_dim = w["ssd_conv_b"].shape[-1]
    G = (conv_dim - DI) // (2 * N_STATE)
    F = w["w_down"].shape[0]
    return dict(D=D, DI=DI, H=H, P=DI // H, G=G, R=H // G, GN=G * N_STATE, CD=conv_dim, F=F)


def _proj_layout(d):
    D, DI, CD, H = d["D"], d["DI"], d["CD"], d["H"]
    o = dict(z=0, xbc=DI, scb=DI + CD, scc=DI + CD + D, sch=DI + CD + 2 * D, g1=DI + CD + 3 * D, g2=DI + CD + 4 * D,
             dt=DI + CD + 5 * D)
    o["sb"] = math.gcd(SEG_BLK, D, DI, d["GN"])
    assert o["sb"] % 128 == 0 and H <= o["sb"]
    o["np"] = o["dt"] + o["sb"]
    return o


def _glu_perm(a, F, inverse=False):
    lead = a.shape[:-1]
    nb = F // GLU_W
    if not inverse:
        return a.reshape(*lead, 2, nb, GLU_W).swapaxes(-3, -2).reshape(*lead, 2 * F)
    return a.reshape(*lead, nb, 2, GLU_W).swapaxes(-3, -2).reshape(*lead, 2 * F)


def _glu_perm_rows(a, F, inverse=False):
    nb, D = F // GLU_W, a.shape[1]
    shape = (nb, 2, GLU_W, D) if inverse else (2, nb, GLU_W, D)
    return a.reshape(shape).swapaxes(0, 1).reshape(2 * F, D)


def _prep_layer(w):
    d = _dims(w)
    D, DI, CD, H, G, R, F = d["D"], d["DI"], d["CD"], d["H"], d["G"], d["R"], d["F"]
    lay = _proj_layout(d)
    w_in = w["w_in"]
    used = lay["dt"] + H
    wcat = jnp.concatenate([w_in[:DI + CD], w_in[DI + CD + H:], w_in[DI + CD:DI + CD + H],
                            jnp.zeros((lay["np"] - used, D), w_in.dtype)], axis=0)
    hp = jnp.stack([w["ssd_dt_bias"], w["ssd_a_log"], w["ssd_d"]], 0).astype(F32)
    hpc = jnp.pad(hp.reshape(3, G, R).transpose(1, 0, 2), ((0, 0), (0, 5), (0, 0)))
    hpr = jnp.pad(hp[:2].reshape(2, G, R).transpose(1, 2, 0), ((0, 0), (0, 0), (0, 6)))
    row = lambda v: v.reshape(1, -1).astype(F32)
    return dict(
        d=d, lay=lay, ada_w=w["ada_w"].astype(BF16), ada_b=row(w["ada_b"]),
        mix_pre_g=row(w["mix_pre_g"]), mix_post_g=row(w["mix_post_g"]), wcat=wcat.astype(BF16),
        ssd_conv_w=_pad8(w["ssd_conv_w"].astype(F32)), ssd_conv_b=row(w["ssd_conv_b"]), hpc=hpc, hpr=hpr,
        ssd_norm_g=row(w["ssd_norm_g"]), w_ssd_out=w["w_ssd_out"].astype(BF16),
        sc_conv_w=_pad8(w["sc_conv_w"].astype(F32)), w_sc_out=w["w_sc_out"].astype(BF16), w_o=w["w_o"].astype(BF16),
        ffn_pre_g=row(w["ffn_pre_g"]), ffn_post_g=row(w["ffn_post_g"]),
        w_up=_glu_perm_rows(w["w_up"], F).astype(BF16), ffn_conv_w=_pad8(_glu_perm(w["ffn_conv_w"].astype(F32), F)),
        ffn_conv_b=_glu_perm(row(w["ffn_conv_b"]), F), w_down=w["w_down"].astype(BF16))


def _dt_layouts(proj, lay, d):
    T = proj.shape[0]
    dt = proj[:, lay["dt"]:lay["dt"] + d["H"]].reshape(T, d["G"], d["R"])
    return dt.transpose(1, 0, 2), dt.transpose(1, 2, 0)


def _layer_fwd(x, c8, p, S, li, gather=None):
    d, lay = p["d"], p["lay"]
    D, DI, G, R, P, GN, CD = d["D"], d["DI"], d["G"], d["R"], d["P"], d["GN"], d["CD"]
    nb = x.shape[0] // S
    nm = lambda s: f"l{li}_{s}"
    mod, cact = _modk(c8, p["ada_w"], p["ada_b"], nm("mod"))
    mod3 = mod[:nb].reshape(nb, 1, 6 * D)
    h = _norm_mod(x, p["mix_pre_g"], mod3, 1, 0, S, nm("norm1"))
    proj = _mm(h, p["wcat"], "nt", F32, nm("mm_in"), caps=(1024, 1536, 2048))
    pre = _conv_fwd((proj, lay["xbc"]), CD, p["ssd_conv_w"], p["ssd_conv_b"], SSD_CONV_K, S, nm("ssdconv"))
    dtc, dtr = _dt_layouts(proj, lay, d)
    offs = (0, DI, DI + GN)
    gathered = None
    if gather is None:
        y, hs = _ssd_fwd(pre, offs, dtc, dtr, p["hpc"], p["hpr"], G, R, P, S, nm("ssd"))
    else:
        (y, hs), (gathered,) = _ssd_fwd(pre, offs, dtc, dtr, p["hpc"], p["hpr"], G, R, P, S, nm("ssd"),
                                        side=_side_gather_direct(gather))
    yn = _gate_norm_fwd(y, (proj, lay["z"]), p["ssd_norm_g"], G, S, nm("gnorm"))
    sc = _shortconv_fwd(proj, lay["scb"], lay["scc"], lay["sch"], D, p["sc_conv_w"], S, nm("sconv"))
    if gather is None:
        y_ssd = _mm(yn, p["w_ssd_out"], "nn", F32, nm("mm_ssdout"))
    else:
        y_ssd, (gathered,) = _mm(yn, p["w_ssd_out"], "nn", F32, nm("mm_ssdout"), side=_side_gather_forward(gathered))
    y_sc = _mm(sc, p["w_sc_out"], "nn", F32, nm("mm_scout"))
    m = _merge_fwd(proj, lay["g1"], lay["g2"], y_ssd, y_sc, S, nm("merge"))
    mix = _mm(m, p["w_o"], "nn", F32, nm("mm_o"))
    x1 = _resid_post(x, mix, mod3, 2, p["mix_post_g"], S, nm("post1"))
    h2 = _norm_mod(x1, p["ffn_pre_g"], mod3, 4, 3, S, nm("norm2"))
    uu = _mm(h2, p["w_up"], "nt", F32, nm("mm_up"), caps=(1024, 1408, 2048))
    a = _ffn_act_fwd(uu, p["ffn_conv_w"], p["ffn_conv_b"], S, nm("ffnact"))
    f = _mm(a, p["w_down"], "nn", F32, nm("mm_down"), caps=(1024, 1024, 1408))
    x2 = _resid_post(x1, f, mod3, 5, p["ffn_post_g"], S, nm("post2"))
    saved = dict(x=x, h=h, proj=proj, pre=pre, dtc=dtc, dtr=dtr, y=y, hs=hs, yn=yn, sc=sc, y_ssd=y_ssd, y_sc=y_sc,
                 m=m, mix=mix, x1=x1, h2=h2, uu=uu, a=a, f=f, mod3=mod3, cact=cact)
    return x2, saved, gathered


def _seq_sum(acc, nb):
    return acc.reshape(nb, 8, -1)[:, 0, :]


def _layer_bwd(dx2, p, s, S, li, chip_sums=None):
    d, lay = p["d"], p["lay"]
    D, DI, G, R, P, GN, CD, H, F = d["D"], d["DI"], d["G"], d["R"], d["P"], d["GN"], d["CD"], d["H"], d["F"]
    nb = dx2.shape[0] // S
    nm = lambda t: f"l{li}_{t}"
    mod3 = s["mod3"]
    g = {}
    exchanged = None
    df, dgt2, dpg2 = _post_bwd(s["f"], mod3, 5, p["ffn_post_g"], dx2, S, nm("post2_b"))
    g["ffn_post_g"] = dpg2[0]
    da = _mm(df, p["w_down"], "nt", BF16, nm("mm_down_bi"), caps=(1024, 1408, 2048))
    g["w_down"] = _mm(s["a"], df, "tn", WGRAD,nm("mm_down_bw"), caps=(1408, 1024, 1024))
    if chip_sums is None:
        duu, cw = _ffn_act_bwd(s["uu"], da, p["ffn_conv_w"], p["ffn_conv_b"], S, nm("ffnact_b"))
    else:
        (duu, cw), (exchanged,) = _ffn_act_bwd(s["uu"], da, p["ffn_conv_w"], p["ffn_conv_b"], S, nm("ffnact_b"),
                                               side=_side_chip_exchange(chip_sums))
    g["ffn_conv_w"] = _glu_perm(cw[:FFN_CONV_K], F, inverse=True)
    g["ffn_conv_b"] = _glu_perm(cw[FFN_CONV_K], F, inverse=True)
    dh2 = _mm(duu, p["w_up"], "nn", F32, nm("mm_up_bi"), caps=(1024, 1024, 2816))
    g["w_up"] = _glu_perm_rows(_mm(duu, s["h2"], "tn", WGRAD,nm("mm_up_bw"), caps=(1408, 1024, 1024)), F, inverse=True)
    dx1, dg2, dsc2, dsh2 = _pre_bwd(s["x1"], p["ffn_pre_g"], mod3, 4, dh2, dx2, S, nm("norm2_b"))
    g["ffn_pre_g"] = dg2[0]
    dmix, dgt1, dpg1 = _post_bwd(s["mix"], mod3, 2, p["mix_post_g"], dx1, S, nm("post1_b"))
    g["mix_post_g"] = dpg1[0]
    dm = _mm(dmix, p["w_o"], "nt", F32, nm("mm_o_bi"))
    g["w_o"] = _mm(s["m"], dmix, "tn", WGRAD,nm("mm_o_bw"))
    proj = s["proj"]
    dy_ssd, dy_sc, dg1, dg2_ = _merge_bwd(proj, lay["g1"], lay["g2"], s["y_ssd"], s["y_sc"], dm, S, nm("merge_b"))
    dyn = _mm(dy_ssd, p["w_ssd_out"], "nt", F32, nm("mm_ssdout_bi"))
    g["w_ssd_out"] = _mm(s["yn"], dy_ssd, "tn", WGRAD,nm("mm_ssdout_bw"))
    dsc = _mm(dy_sc, p["w_sc_out"], "nt", F32, nm("mm_scout_bi"))
    g["w_sc_out"] = _mm(s["sc"], dy_sc, "tn", WGRAD,nm("mm_scout_bw"))
    dscb, dscc, dsch, scw = _shortconv_bwd(proj, lay["scb"], lay["scc"], lay["sch"], D, p["sc_conv_w"], dsc, S, nm("sconv_b"))
    g["sc_conv_w"] = scw[:SC_CONV_K]
    dy, dz, dng = _gate_norm_bwd(s["y"], (proj, lay["z"]), p["ssd_norm_g"], dyn, G, S, nm("gnorm_b"))
    g["ssd_norm_g"] = dng[0]
    offs = (0, DI, DI + GN)
    dpx, dpb, dpc, ddt, hpg = _ssd_bwd(s["pre"], offs, s["dtc"], s["dtr"], p["hpc"], p["hpr"], s["hs"], dy,
                                       G, R, P, S, nm("ssd_b"))
    g["ssd_dt_bias"], g["ssd_a_log"], g["ssd_d"] = hpg[:, 0, :].reshape(H), hpg[:, 1, :].reshape(H), hpg[:, 2, :].reshape(H)
    cws, dxbc = [], []
    for name, darr, off, C in (("x", dpx, 0, DI), ("b", dpb, DI, GN), ("c", dpc, DI + GN, GN)):
        w8 = p["ssd_conv_w"][:, off:off + C]
        cws.append(_conv_bwd_w((darr, 0), (proj, lay["xbc"] + off), C, SSD_CONV_K, S, nm(f"ssdconv_bw_{name}")))
        dxbc.append(_conv_bwd_in((darr, 0), C, w8, SSD_CONV_K, S, BF16, nm(f"ssdconv_bi_{name}")))
    cws = jnp.concatenate(cws, axis=1)
    g["ssd_conv_w"], g["ssd_conv_b"] = cws[:SSD_CONV_K], cws[SSD_CONV_K]
    T = dx2.shape[0]
    ddt_t = jnp.pad(ddt.transpose(1, 0, 2).reshape(T, H).astype(BF16), ((0, 0), (0, lay["sb"] - H)))
    dproj = [dz] + dxbc + [dscb, dscc, dsch, dg1, dg2_, ddt_t]
    dh = _mm_seg(dproj, p["wcat"], "nn", F32, nm("mm_in_bi"), lay["sb"])
    dwcat = _mm_seg(dproj, s["h"], "tn", WGRAD, nm("mm_in_bw"), lay["sb"], tk=1024)
    o = lay
    g["w_in"] = jnp.concatenate([dwcat[o["z"]:o["scb"]], dwcat[o["dt"]:o["dt"] + H], dwcat[o["scb"]:o["dt"]]], axis=0)
    dx, dg1_, dsc1, dsh1 = _pre_bwd(s["x"], p["mix_pre_g"], mod3, 1, dh, dx1, S, nm("norm1_b"))
    g["mix_pre_g"] = dg1_[0]
    dmod = jnp.concatenate([_seq_sum(t, nb) for t in (dsh1, dsc1, dgt1, dsh2, dsc2, dgt2)], axis=1)
    dmod8 = jnp.pad(dmod, ((0, MOD_ROWS - nb), (0, 0)))
    g["ada_b"] = _colsum(dmod8, nm("adab"))
    g["ada_w"] = _mm(dmod8, s["cact"], "tn", WGRAD, nm("mm_ada_bw"), caps=(1536, 1024, 2048))
    return dx, g, exchanged


def _colsum(a8, name):
    rows, C = a8.shape
    tc = _tile(C, 2048)

    def body(a_ref, o_ref):
        o_ref[...] = _bsum(jnp.sum(a_ref[...], axis=0, keepdims=True))

    return _pc(body, grid=(C // tc,), in_specs=[pl.BlockSpec((rows, tc), lambda j: (0, j))],
               out_specs=pl.BlockSpec((8, tc), lambda j: (0, j)), out_shape=SDS((8, C), F32), name=name)(a8)[0]


def _adam(gs, w, m, v, name):
    ns, R, W = gs.shape
    tr = _tile(R, 256, 8)

    def body(g_ref, w_ref, m_ref, v_ref, go_ref, d_ref, mo_ref, vo_ref):
        g = g_ref[0].astype(F32)
        for k in range(1, ns):
            g = g + g_ref[k].astype(F32)
        go_ref[...] = g
        d_ref[...], mo_ref[...], vo_ref[...] = _adam_update(g, w_ref[...], m_ref[...], v_ref[...])

    row = pl.BlockSpec((tr, W), lambda i: (i, 0))
    return _pc(body, grid=(R // tr,), in_specs=[pl.BlockSpec((ns, tr, W), lambda i: (0, i, 0)), row, row, row],
               out_specs=[row] * 4, out_shape=[SDS((R, W), F32)] * 4, name=name)(gs, w, m, v)


def _adam_update(g, w, m, v):
    c1 = 1.0 / (1.0 - ADAM_B1 ** ADAM_STEP)
    c2 = 1.0 / (1.0 - ADAM_B2 ** ADAM_STEP)
    m_ = ADAM_B1 * m + (1.0 - ADAM_B1) * g
    v_ = ADAM_B2 * v + (1.0 - ADAM_B2) * (g * g)
    return -ADAM_LR * ((m_ * c1) / (jnp.sqrt(v_ * c2) + ADAM_EPS) + ADAM_WD * w), m_, v_


def _adam_nat(g, w, m, v, name):
    depth, a, b = w.shape
    tr = _tile(a, 256, 8)

    def body(g_ref, w_ref, m_ref, v_ref, d_ref, mo_ref, vo_ref):
        d_ref[...], mo_ref[...], vo_ref[...] = _adam_update(g_ref[...], w_ref[...], m_ref[...], v_ref[...])

    blk = pl.BlockSpec((None, tr, b), lambda l, i: (l, i, 0))
    return _pc(body, grid=(depth, a // tr), in_specs=[blk] * 4, out_specs=[blk] * 3,
               out_shape=[SDS(w.shape, F32)] * 3, name=name)(g, w, m, v)


def _sum_chips(gs, name):
    ns, R, W = gs.shape
    tr = _tile(R, 256, 16)

    def body(g_ref, o_ref):
        acc = g_ref[0].astype(F32)
        for k in range(1, ns):
            acc = acc + g_ref[k].astype(F32)
        o_ref[...] = acc

    return _pc(body, grid=(R // tr,), in_specs=[pl.BlockSpec((ns, tr, W), lambda i: (0, i, 0))],
               out_specs=pl.BlockSpec((tr, W), lambda i: (i, 0)), out_shape=SDS((R, W), F32), name=name)(gs)


HBM_SPEC = pl.BlockSpec(memory_space=pltpu.HBM)
VMEM_SPEC = pl.BlockSpec(memory_space=pltpu.VMEM)


def _dev():
    return lax.axis_index("x"), lax.axis_index("y"), lax.axis_index("c")


def _allgather_big(loc, name):
    R, W = loc.shape

    def body(x_ref, out_ref, send_sems, recv_sems, local_sem):
        x, y, c = _dev()
        me, sibling = (x, y, c), (x, y, 1 - c)
        chips = [(1 - x, y), (x, 1 - y), (1 - x, 1 - y)]

        def slab(px, py, pc):
            return out_ref.at[4 * px + 2 * py + pc]

        def copy(k, block, to, src=None):
            return pltpu.make_async_remote_copy(
                src_ref=slab(*block) if src is None else src, dst_ref=slab(*block),
                send_sem=send_sems.at[k], recv_sem=recv_sems.at[k], device_id=to, device_id_type=MESH)

        mine = pltpu.make_async_copy(x_ref, slab(*me), local_sem)
        mine.start()
        first = [copy(0, me, sibling, src=x_ref)]
        first += [copy(1 + j, me, (*chip, c), src=x_ref) for j, chip in enumerate(chips)]
        for cp in first:
            cp.start()
        passed = [copy(4 + j, (*chip, c), sibling) for j, chip in enumerate(chips)]
        for j, chip in enumerate(chips):
            copy(1 + j, (*chip, c), me).wait_recv()
            passed[j].start()
        copy(0, sibling, me).wait_recv()
        for j, chip in enumerate(chips):
            copy(4 + j, (*chip, 1 - c), me).wait_recv()
        for cp in first + passed:
            cp.wait_send()
        mine.wait()

    return pl.pallas_call(
        body, out_shape=SDS((N_DEV, R, W), loc.dtype), in_specs=[HBM_SPEC], out_specs=HBM_SPEC,
        scratch_shapes=[pltpu.SemaphoreType.DMA((7,)), pltpu.SemaphoreType.DMA((7,)), pltpu.SemaphoreType.DMA],
        name=name)(loc)


def _dma_sems(n):
    return (pltpu.SemaphoreType.DMA((n,)), pltpu.SemaphoreType.DMA((n,)), pltpu.SemaphoreType.DMA)


def _side_gather_direct(loc):
    R, W = loc.shape

    def copies(ins, outs, sems):
        x, y, c = _dev()
        x_ref, out = ins[0], outs[0]
        me = 4 * x + 2 * y + c
        peers = [(x, y, 1 - c), (1 - x, y, c), (x, 1 - y, c), (1 - x, 1 - y, c)]
        mk = lambda k, p, dst: pltpu.make_async_remote_copy(
            src_ref=x_ref, dst_ref=out.at[dst], send_sem=sems[0].at[k], recv_sem=sems[1].at[k], device_id=p,
            device_id_type=MESH)
        sends = [mk(k, p, me) for k, p in enumerate(peers)]
        recvs = [mk(k, p, 4 * p[0] + 2 * p[1] + p[2]) for k, p in enumerate(peers)]
        return sends, recvs, pltpu.make_async_copy(x_ref, out.at[me], sems[2])

    def start(ins, outs, sems):
        sends, _, mine = copies(ins, outs, sems)
        mine.start()
        for cp in sends:
            cp.start()

    def wait(ins, outs, sems):
        sends, recvs, mine = copies(ins, outs, sems)
        for cp in recvs:
            cp.wait_recv()
        for cp in sends:
            cp.wait_send()
        mine.wait()

    return _Side((loc,), (SDS((N_DEV, R, W), loc.dtype),), _dma_sems(4), start, wait)


def _side_gather_forward(buf):
    def copies(ins, outs, sems):
        x, y, c = _dev()
        out = outs[0]
        chips = [(1 - x, y), (x, 1 - y), (1 - x, 1 - y)]
        mk = lambda k, src, dst: pltpu.make_async_remote_copy(
            src_ref=out.at[src], dst_ref=out.at[dst], send_sem=sems[0].at[k], recv_sem=sems[1].at[k],
            device_id=(x, y, 1 - c), device_id_type=MESH)
        mine = [4 * px + 2 * py + c for px, py in chips]
        theirs = [4 * px + 2 * py + (1 - c) for px, py in chips]
        return [mk(k, s, s) for k, s in enumerate(mine)], [mk(k, s, t) for k, (s, t) in enumerate(zip(mine, theirs))]

    def start(ins, outs, sems):
        for cp in copies(ins, outs, sems)[0]:
            cp.start()

    def wait(ins, outs, sems):
        sends, recvs = copies(ins, outs, sems)
        for cp in recvs:
            cp.wait_recv()
        for cp in sends:
            cp.wait_send()

    return _Side((buf,), (SDS(buf.shape, buf.dtype),), _dma_sems(3)[:2], start, wait, {0: 0})


def _side_chip_exchange(p):
    def copies(ins, outs, sems):
        x, y, c = _dev()
        p_ref, out = ins[0], outs[0]
        j0 = 2 * x + y
        chips = [(1 - x, y), (x, 1 - y), (1 - x, 1 - y)]
        mk = lambda k, chip, src, dst: pltpu.make_async_remote_copy(
            src_ref=p_ref.at[src], dst_ref=out.at[dst], send_sem=sems[0].at[k], recv_sem=sems[1].at[k],
            device_id=(*chip, c), device_id_type=MESH)
        sends = [mk(k, chip, 2 * chip[0] + chip[1], j0) for k, chip in enumerate(chips)]
        recvs = [mk(k, chip, j0, 2 * chip[0] + chip[1]) for k, chip in enumerate(chips)]
        return sends, recvs, pltpu.make_async_copy(p_ref.at[j0], out.at[j0], sems[2])

    def start(ins, outs, sems):
        sends, _, mine = copies(ins, outs, sems)
        mine.start()
        for cp in sends:
            cp.start()

    def wait(ins, outs, sems):
        sends, recvs, mine = copies(ins, outs, sems)
        for cp in recvs:
            cp.wait_recv()
        for cp in sends:
            cp.wait_send()
        mine.wait()

    return _Side((p,), (SDS(p.shape, p.dtype),), _dma_sems(3), start, wait)


def _rs_pair_exchange(g, name):
    nd, R, W = g.shape
    nj = nd // 2

    def body(g_ref, out_ref, send_sems, recv_sems):
        x, y, c = _dev()
        cps = [pltpu.make_async_remote_copy(src_ref=g_ref.at[2 * j + (1 - c)], dst_ref=out_ref.at[j],
                                            send_sem=send_sems.at[j], recv_sem=recv_sems.at[j],
                                            device_id=(x, y, 1 - c), device_id_type=MESH) for j in range(nj)]
        for cp in cps:
            cp.start()
        for cp in cps:
            cp.wait()

    return pl.pallas_call(
        body, out_shape=SDS((nj, R, W), g.dtype), in_specs=[HBM_SPEC], out_specs=HBM_SPEC,
        scratch_shapes=[pltpu.SemaphoreType.DMA((nj,)), pltpu.SemaphoreType.DMA((nj,))], name=name)(g)


def _add_pairs(g, ra, name):
    nd, R, W = g.shape
    nj = nd // 2
    tr = _tile(R, 256, 8)
    cidx = lax.axis_index("c").astype(jnp.int32).reshape(1)

    def body(c_ref, a_ref, b_ref, o_ref):
        o_ref[...] = (a_ref[...].astype(F32) + b_ref[...].astype(F32)).astype(o_ref.dtype)

    gs = pltpu.PrefetchScalarGridSpec(
        num_scalar_prefetch=1, grid=(nj, R // tr),
        in_specs=[pl.BlockSpec((None, tr, W), lambda j, i, cr: (2 * j + cr[0], i, 0)),
                  pl.BlockSpec((None, tr, W), lambda j, i, cr: (j, i, 0))],
        out_specs=pl.BlockSpec((None, tr, W), lambda j, i, cr: (j, i, 0)))
    return pl.pallas_call(body, grid_spec=gs, out_shape=SDS((nj, R, W), g.dtype), name=name,
                          compiler_params=pltpu.CompilerParams(vmem_limit_bytes=VMEM_LIMIT))(cidx, g, ra)


def _rs_chip_exchange(p, name):
    nj, R, W = p.shape

    def body(p_ref, out_ref, send_sems, recv_sems, local_sem):
        x, y, c = _dev()
        j0 = 2 * x + y
        chips = [(1 - x, y), (x, 1 - y), (1 - x, 1 - y)]
        mine = pltpu.make_async_copy(p_ref.at[j0], out_ref.at[j0], local_sem)
        mine.start()

        def copy(k, chip):
            return pltpu.make_async_remote_copy(
                src_ref=p_ref.at[2 * chip[0] + chip[1]], dst_ref=out_ref.at[j0],
                send_sem=send_sems.at[k], recv_sem=recv_sems.at[k], device_id=(*chip, c), device_id_type=MESH)

        sent = [copy(k, chip) for k, chip in enumerate(chips)]
        for cp in sent:
            cp.start()
        for k, chip in enumerate(chips):
            pltpu.make_async_remote_copy(
                src_ref=p_ref.at[j0], dst_ref=out_ref.at[2 * chip[0] + chip[1]],
                send_sem=send_sems.at[k], recv_sem=recv_sems.at[k], device_id=(*chip, c), device_id_type=MESH).wait_recv()
        for cp in sent:
            cp.wait_send()
        mine.wait()

    return pl.pallas_call(
        body, out_shape=SDS((nj, R, W), p.dtype), in_specs=[HBM_SPEC], out_specs=HBM_SPEC,
        scratch_shapes=[pltpu.SemaphoreType.DMA((3,)), pltpu.SemaphoreType.DMA((3,)), pltpu.SemaphoreType.DMA],
        name=name)(p)


def _allgather_small(v, name):
    R, W = v.shape

    def body(v_ref, out_ref, send_sems, recv_sems, local_sem):
        x, y, c = _dev()
        mine = pltpu.make_async_copy(v_ref, out_ref.at[4 * x + 2 * y + c], local_sem)
        mine.start()
        peers = []
        for k in range(1, N_DEV):
            px = 1 - x if k & 4 else x
            py = 1 - y if k & 2 else y
            pc_ = 1 - c if k & 1 else c
            peers.append((px, py, pc_))
        sent = [pltpu.make_async_remote_copy(
            src_ref=v_ref, dst_ref=out_ref.at[4 * x + 2 * y + c], send_sem=send_sems.at[k], recv_sem=recv_sems.at[k],
            device_id=peer, device_id_type=MESH) for k, peer in enumerate(peers)]
        for cp in sent:
            cp.start()
        for k, (px, py, pc_) in enumerate(peers):
            pltpu.make_async_remote_copy(
                src_ref=v_ref, dst_ref=out_ref.at[4 * px + 2 * py + pc_], send_sem=send_sems.at[k],
                recv_sem=recv_sems.at[k], device_id=(px, py, pc_), device_id_type=MESH).wait_recv()
        for cp in sent:
            cp.wait_send()
        mine.wait()

    return pl.pallas_call(
        body, out_shape=SDS((N_DEV, R, W), v.dtype), in_specs=[VMEM_SPEC], out_specs=VMEM_SPEC,
        scratch_shapes=[pltpu.SemaphoreType.DMA((7,)), pltpu.SemaphoreType.DMA((7,)), pltpu.SemaphoreType.DMA],
        name=name)(v)


def _sum_slabs(a, name):
    ns, R, W = a.shape

    def body(a_ref, o_ref):
        acc = a_ref[0]
        for k in range(1, ns):
            acc = acc + a_ref[k]
        o_ref[...] = acc

    return pl.pallas_call(body, out_shape=SDS((R, W), a.dtype), in_specs=[VMEM_SPEC], out_specs=VMEM_SPEC, name=name)(a)


BIG = (("ada_w", "col"), ("w_in", "col"), ("w_ssd_out", "row"), ("w_sc_out", "row"), ("w_o", "row"), ("w_up", "col"),
       ("w_down", "row"))
CONVW = ("ssd_conv_w", "sc_conv_w", "ffn_conv_w")
REPL = ("ada_b", "mix_pre_g", "mix_post_g", "ssd_conv_b", "ssd_dt_bias", "ssd_a_log", "ssd_d", "ssd_norm_g", "ffn_pre_g",
        "ffn_post_g", "ffn_conv_b")
WEIGHTS = ("ada_w", "ada_b", "mix_pre_g", "mix_post_g", "w_in", "ssd_conv_w", "ssd_conv_b", "ssd_dt_bias", "ssd_a_log",
           "ssd_d", "ssd_norm_g", "w_ssd_out", "sc_conv_w", "w_sc_out", "w_o", "ffn_pre_g", "ffn_post_g", "w_up",
           "ffn_conv_w", "ffn_conv_b", "w_down")


def _pad_rows(a, mult):
    r = a.shape[-2]
    pad = -r % mult
    return a if pad == 0 else jnp.pad(a, [(0, 0)] * (a.ndim - 2) + [(0, pad), (0, 0)])


def _flat_rows(parts, mult):
    flat = jnp.concatenate([p.reshape(-1) for p in parts])
    flat = jnp.pad(flat, (0, -flat.shape[0] % ROW_W))
    return _pad_rows(flat.reshape(-1, ROW_W), mult)


def _unflat(buf, shapes):
    flat = buf.reshape(-1)
    out, o = [], 0
    for shp in shapes:
        n = 1
        for s in shp:
            n *= s
        out.append(flat[o:o + n].reshape(shp))
        o += n
    return out


def _pack_big_local(get, l):
    return [_pad_rows((get(n)[l].T if kind == "col" else get(n)[l]).reshape(-1, ROW_W), SLAB_ALIGN) for n, kind in BIG]


def _big_rows(shapes):
    out, o = {}, 0
    for n, _ in BIG:
        r = shapes[n][1] * shapes[n][2] // ROW_W
        out[n] = (o, o + r)
        o += -(-r // SLAB_ALIGN) * SLAB_ALIGN
    return out, o


def kernel(x, c, ada_w, ada_b, mix_pre_g, mix_post_g, w_in, ssd_conv_w, ssd_conv_b, ssd_dt_bias, ssd_a_log, ssd_d, ssd_norm_g, w_ssd_out, sc_conv_w, w_sc_out, w_o, ffn_pre_g, ffn_post_g, w_up, ffn_conv_w, ffn_conv_b, w_down, loss_target, m_ada_w, m_ada_b, m_mix_pre_g, m_mix_post_g, m_w_in, m_ssd_conv_w, m_ssd_conv_b, m_ssd_dt_bias, m_ssd_a_log, m_ssd_d, m_ssd_norm_g, m_w_ssd_out, m_sc_conv_w, m_w_sc_out, m_w_o, m_ffn_pre_g, m_ffn_post_g, m_w_up, m_ffn_conv_w, m_ffn_conv_b, m_w_down, v_ada_w, v_ada_b, v_mix_pre_g, v_mix_post_g, v_w_in, v_ssd_conv_w, v_ssd_conv_b, v_ssd_dt_bias, v_ssd_a_log, v_ssd_d, v_ssd_norm_g, v_w_ssd_out, v_sc_conv_w, v_w_sc_out, v_w_o, v_ffn_pre_g, v_ffn_post_g, v_w_up, v_ffn_conv_w, v_ffn_conv_b, v_w_down):
    wl = dict(zip(WEIGHTS, (ada_w, ada_b, mix_pre_g, mix_post_g, w_in, ssd_conv_w, ssd_conv_b, ssd_dt_bias, ssd_a_log,
                            ssd_d, ssd_norm_g, w_ssd_out, sc_conv_w, w_sc_out, w_o, ffn_pre_g, ffn_post_g, w_up,
                            ffn_conv_w, ffn_conv_b, w_down)))
    ml = dict(zip(WEIGHTS, (m_ada_w, m_ada_b, m_mix_pre_g, m_mix_post_g, m_w_in, m_ssd_conv_w, m_ssd_conv_b,
                            m_ssd_dt_bias, m_ssd_a_log, m_ssd_d, m_ssd_norm_g, m_w_ssd_out, m_sc_conv_w, m_w_sc_out, m_w_o,
                            m_ffn_pre_g, m_ffn_post_g, m_w_up, m_ffn_conv_w, m_ffn_conv_b, m_w_down)))
    vl = dict(zip(WEIGHTS, (v_ada_w, v_ada_b, v_mix_pre_g, v_mix_post_g, v_w_in, v_ssd_conv_w, v_ssd_conv_b,
                            v_ssd_dt_bias, v_ssd_a_log, v_ssd_d, v_ssd_norm_g, v_w_ssd_out, v_sc_conv_w, v_w_sc_out, v_w_o,
                            v_ffn_pre_g, v_ffn_post_g, v_w_up, v_ffn_conv_w, v_ffn_conv_b, v_w_down)))
    depth = ada_w.shape[0]
    shapes = {n: wl[n].shape for n in WEIGHTS}
    me = 4 * lax.axis_index("x") + 2 * lax.axis_index("y") + lax.axis_index("c")

    rows, n_big = _big_rows(shapes)
    conv_flat = jnp.concatenate([wl[n][l].reshape(-1) for l in range(depth) for n in CONVW])
    n_conv = conv_flat.shape[0]
    conv_flat = jnp.pad(conv_flat, (0, -n_conv % (ROW_W // 2)))
    conv_rows = lax.bitcast_convert_type(conv_flat, BF16).reshape(-1, ROW_W)

    def local_rows(l):
        pieces = _pack_big_local(lambda n: wl[n].astype(BF16), l) + ([conv_rows] if l == 0 else [])
        return _pad_rows(jnp.concatenate(pieces, axis=0), ROW_PAD)

    def layer_weights(l, gathered):
        w = {n: wl[n][l] for n in REPL}
        for n, kind in BIG:
            a, b = shapes[n][1], shapes[n][2]
            blk = gathered[:, rows[n][0]:rows[n][1]]
            w[n] = blk.reshape(N_DEV * b, a) if kind == "col" else blk.reshape(N_DEV * a, b)
        for n in CONVW:
            w[n] = conv_full[(l, n)]
        return w

    gathered = _allgather_big(local_rows(0), "allgather_weights")
    conv_all = lax.bitcast_convert_type(
        gathered[:, n_big:n_big + conv_rows.shape[0]].reshape(N_DEV, -1, 2), F32)[:, :n_conv]
    conv_full, o = {}, 0
    for l in range(depth):
        for n in CONVW:
            k, cl = shapes[n][1], shapes[n][2]
            conv_full[(l, n)] = conv_all[:, o:o + k * cl].reshape(N_DEV, k, cl).transpose(1, 0, 2).reshape(k, N_DEV * cl)
            o += k * cl

    nb, S, D = x.shape
    T = nb * S
    act = x.reshape(T, D)
    c8 = jnp.pad(c, ((0, MOD_ROWS - nb), (0, 0)))
    preps, saved = [], []
    for l in range(depth):
        preps.append(_prep_layer(layer_weights(l, gathered)))
        act, s, gathered = _layer_fwd(act, c8, preps[l], S, l, gather=local_rows(l + 1) if l + 1 < depth else None)
        saved.append(s)
    dy, lacc = _loss(act, loss_target.reshape(T, D), S, "loss")
    loss_loc = lacc[0, 0]

    grads, from_chips, pending = [None] * depth, [None] * depth, None
    for l in reversed(range(depth)):
        dy, grads[l], got = _layer_bwd(dy, preps[l], saved[l], S, l, chip_sums=pending)
        if pending is not None:
            from_chips[l + 1] = got
        slabs = [_pad_rows(grads[l][n].astype(BF16).reshape(N_DEV, -1, ROW_W), SLAB_ALIGN) for n, _ in BIG]
        slabs.append(jnp.zeros((N_DEV, -n_big % ROW_PAD, ROW_W), BF16))
        gslab = jnp.concatenate(slabs, axis=1)
        from_sibling = _rs_pair_exchange(gslab, f"rs_pair_exchange_l{l}")
        pending = _add_pairs(gslab, from_sibling, f"rs_pair_add_l{l}")
    from_chips[0] = _rs_chip_exchange(pending, "rs_chip_exchange")
    dx = dy.reshape(nb, S, D)
    g_sums = [_sum_chips(from_chips[l], f"rs_chip_sum_l{l}") for l in range(depth)]

    def grad_of(l, n, kind):
        blk = g_sums[l][rows[n][0]:rows[n][1]]
        a, b = shapes[n][1], shapes[n][2]
        return blk.reshape(b, a).T if kind == "col" else blk.reshape(a, b)

    g_big = {n: jnp.stack([grad_of(l, n, kind) for l in range(depth)]) for n, kind in BIG}
    d_big, m_big, v_big = {}, {}, {}
    for n, _ in BIG:
        d_big[n], m_big[n], v_big[n] = _adam_nat(g_big[n], wl[n], ml[n], vl[n], f"adam_{n}")

    parts = [jnp.broadcast_to(loss_loc, (ROW_W,))]
    small_shapes = [(ROW_W,)]
    for l in range(depth):
        for n in REPL + CONVW:
            parts.append(grads[l][n])
            small_shapes.append(tuple(grads[l][n].shape))
    total = _sum_slabs(_allgather_small(_flat_rows(parts, 8), "allgather_small"), "sum_small")
    pieces = _unflat(total, small_shapes)
    loss = pieces[0][0]
    g_small, i = {}, 1
    for l in range(depth):
        for n in REPL + CONVW:
            gp = pieces[i]
            i += 1
            if n in CONVW:
                gp = lax.dynamic_slice_in_dim(gp, me * shapes[n][2], shapes[n][2], axis=1)
            g_small[(l, n)] = gp
    order = [(l, n) for l in range(depth) for n in REPL + CONVW]
    loc_shapes = [tuple(shapes[n][1:]) for _, n in order]
    packs = lambda f: _flat_rows([f(l, n) for l, n in order], 8)
    gs_small = packs(lambda l, n: g_small[(l, n)])
    _, d_sm, m_sm, v_sm = _adam(gs_small[None], packs(lambda l, n: wl[n][l]), packs(lambda l, n: ml[n][l]),
                                packs(lambda l, n: vl[n][l]), "adam_small")

    def unpack_small(buf):
        ps = _unflat(buf, loc_shapes)
        return {n: jnp.stack([ps[order.index((l, n))] for l in range(depth)]) for n in REPL + CONVW}

    outs = []
    for big, small in ((g_big, {n: jnp.stack([g_small[(l, n)] for l in range(depth)]) for n in REPL + CONVW}),
                       (d_big, unpack_small(d_sm)), (m_big, unpack_small(m_sm)), (v_big, unpack_small(v_sm))):
        merged = {**big, **small}
        outs += [merged[n] for n in WEIGHTS]
    return (loss, dx, *outs)
```

```python
import functools
import math
from typing import Callable, NamedTuple

import jax
import jax.numpy as jnp
from jax import lax
from jax.experimental import pallas as pl
from jax.experimental.pallas import tpu as pltpu

F32, BF16 = jnp.float32, jnp.bfloat16
WGRAD = BF16
SDS = jax.ShapeDtypeStruct
MESH = pl.DeviceIdType.MESH

EPS = 1e-6
N_STATE = 128
CHUNK = 128
SSD_CONV_K, SC_CONV_K, FFN_CONV_K = 4, 3, 3
N_DEV = 8
ROW_W = 1024
ROW_PAD = 256
SLAB_ALIGN = 16
SEG_BLK = 512
STRIP = 32
FFN_STRIP = 64
GLU_W = 256
MOD_ROWS = 128
VMEM_LIMIT = 48 * 2**20

ADAM_LR, ADAM_B1, ADAM_B2, ADAM_EPS, ADAM_WD, ADAM_STEP = 0.001, 0.9, 0.999, 1e-08, 0.01, 10

NT = (((1,), (1,)), ((), ()))
TN = (((0,), (0,)), ((), ()))
NN = (((1,), (0,)), ((), ()))


def _tile(n, cap, mult=128):
    best = None
    for t in range(mult, min(n, cap) + 1, mult):
        if n % t == 0:
            best = t
    return best if best is not None else n


class _Side(NamedTuple):
    operands: tuple
    out_shape: tuple
    scratch: tuple
    start: Callable
    wait: Callable
    aliases: dict = {}


def _pc(body, *, grid, in_specs, out_specs, out_shape, name, scratch=(), side=None):
    params = pltpu.CompilerParams(dimension_semantics=("arbitrary",) * len(grid), vmem_limit_bytes=VMEM_LIMIT)
    if side is None:
        return pl.pallas_call(body, grid=grid, in_specs=in_specs, out_specs=out_specs, out_shape=out_shape,
                              scratch_shapes=list(scratch), name=name, compiler_params=params)
    single = not isinstance(out_shape, (list, tuple))
    outs = [out_shape] if single else list(out_shape)
    ospecs = [out_specs] if single else list(out_specs)
    n_in, n_out, n_scr = len(in_specs), len(outs), len(scratch)
    s_in, s_out = len(side.operands), len(side.out_shape)

    def hosted(*refs):
        ins, refs = refs[:n_in], refs[n_in:]
        sins, refs = refs[:s_in], refs[s_in:]
        mouts, refs = refs[:n_out], refs[n_out:]
        souts, refs = refs[:s_out], refs[s_out:]
        scr, sems = refs[:n_scr], refs[n_scr:]
        first = functools.reduce(lambda a, b: a & b, [pl.program_id(a) == 0 for a in range(len(grid))])
        last = functools.reduce(lambda a, b: a & b, [pl.program_id(a) == grid[a] - 1 for a in range(len(grid))])

        @pl.when(first)
        def _():
            side.start(sins, souts, sems)

        body(*ins, *mouts, *scr)

        @pl.when(last)
        def _():
            side.wait(sins, souts, sems)

    call = pl.pallas_call(
        hosted, grid=grid, in_specs=list(in_specs) + [HBM_SPEC] * s_in, out_specs=ospecs + [HBM_SPEC] * s_out,
        out_shape=outs + list(side.out_shape), scratch_shapes=list(scratch) + list(side.scratch), name=name,
        input_output_aliases={n_in + k: n_out + v for k, v in side.aliases.items()}, compiler_params=params)

    def run(*args):
        res = call(*args, *side.operands)
        main = res[0] if single else list(res[:n_out])
        return main, list(res[n_out:])

    return run


def _silu(x):
    return x * jax.nn.sigmoid(x)


def _dsilu(x):
    s = jax.nn.sigmoid(x)
    return s * (1.0 + x * (1.0 - s))


def _softplus(x):
    return jnp.maximum(x, 0.0) + jnp.log(1.0 + jnp.exp(-jnp.abs(x)))


def _dot(a, b, dims=NN):
    return lax.dot_general(a, b, dims, preferred_element_type=F32)


def _bsum(v, rows=8):
    return jnp.broadcast_to(v, (rows, v.shape[1]))


def _mm(a, b, mode, out_dtype, name, caps=(1024, 1024, 2048), side=None):
    if mode == "nn":
        (M, K), (K2, N) = a.shape, b.shape
    elif mode == "nt":
        (M, K), (N, K2) = a.shape, b.shape
    else:
        (K, M), (K2, N) = a.shape, b.shape
    assert K == K2, (a.shape, b.shape, mode)
    tm, tn, tk = _tile(M, caps[0]), _tile(N, caps[1]), _tile(K, caps[2])
    nk = K // tk
    dims = {"nn": NN, "nt": NT, "tn": TN}[mode]
    if mode == "tn":
        a_spec = pl.BlockSpec((tk, tm), lambda i, j, k: (k, i))
    else:
        a_spec = pl.BlockSpec((tm, tk), lambda i, j, k: (i, k))
    if mode == "nt":
        b_spec = pl.BlockSpec((tn, tk), lambda i, j, k: (j, k))
    else:
        b_spec = pl.BlockSpec((tk, tn), lambda i, j, k: (k, j))

    def body(a_ref, b_ref, o_ref, *acc):
        part = _dot(a_ref[...].astype(BF16), b_ref[...].astype(BF16), dims)
        if nk == 1:
            o_ref[...] = part.astype(o_ref.dtype)
        else:
            acc_ref, = acc
            k = pl.program_id(2)

            @pl.when(k == 0)
            def _():
                acc_ref[...] = part

            @pl.when(k > 0)
            def _():
                acc_ref[...] += part

            @pl.when(k == nk - 1)
            def _():
                o_ref[...] = acc_ref[...].astype(o_ref.dtype)

    return _pc(body, grid=(M // tm, N // tn, nk), in_specs=[a_spec, b_spec],
               out_specs=pl.BlockSpec((tm, tn), lambda i, j, k: (i, j)),
               out_shape=SDS((M, N), out_dtype), name=name,
               scratch=() if nk == 1 else (pltpu.VMEM((tm, tn), F32),), side=side)(a, b)


def _mm_seg(segs, b, mode, out_dtype, name, blk, tile=1024, tk=2048):
    nblk = [a.shape[1] // blk for a in segs]
    assert all(a.shape[1] % blk == 0 for a in segs)
    start = [sum(nblk[:s]) for s in range(len(segs))]
    total = sum(nblk)
    ns = len(segs)
    N = b.shape[1]
    tn = _tile(N, tile)
    if mode == "nn":
        M = segs[0].shape[0]
        tm = _tile(M, tile)
        grid = (M // tm, N // tn, total)
        a_specs = [pl.BlockSpec((tm, blk), lambda i, j, k, k0=k0, n=n: (i, jnp.clip(k - k0, 0, n - 1)))
                   for k0, n in zip(start, nblk)]
        b_spec = pl.BlockSpec((blk, tn), lambda i, j, k: (k, j))
        out_rows, tmo, dims, seg_axis = M, tm, NN, 2
    else:
        K = segs[0].shape[0]
        tkk = _tile(K, tk)
        grid = (total, N // tn, K // tkk)
        a_specs = [pl.BlockSpec((tkk, blk), lambda i, j, k, i0=i0, n=n: (
            jnp.where((i >= i0) & (i < i0 + n), k, 0), jnp.clip(i - i0, 0, n - 1))) for i0, n in zip(start, nblk)]
        b_spec = pl.BlockSpec((tkk, tn), lambda i, j, k: (k, j))
        out_rows, tmo, seg_axis = total * blk, blk, 0
    nk = grid[2]
    acc_shape = (tm, tn) if mode == "nn" else (tn, blk)

    def body(*refs):
        a_refs, b_ref, o_ref, acc_ref = refs[:ns], refs[ns], refs[ns + 1], refs[ns + 2]
        k = pl.program_id(2)
        sel = pl.program_id(seg_axis)

        @pl.when(k == 0)
        def _():
            acc_ref[...] = jnp.zeros_like(acc_ref)

        for s in range(ns):
            @pl.when((sel >= start[s]) & (sel < start[s] + nblk[s]))
            def _(s=s):
                a_, b_ = a_refs[s][...].astype(BF16), b_ref[...].astype(BF16)
                acc_ref[...] += _dot(a_, b_, NN) if mode == "nn" else _dot(b_, a_, TN)

        @pl.when(k == nk - 1)
        def _():
            acc = acc_ref[...]
            o_ref[...] = (acc if mode == "nn" else acc.T).astype(o_ref.dtype)

    return _pc(body, grid=grid, in_specs=a_specs + [b_spec], out_specs=pl.BlockSpec((tmo, tn), lambda i, j, k: (i, j)),
               out_shape=SDS((out_rows, N), out_dtype), name=name, scratch=(pltpu.VMEM(acc_shape, F32),))(*segs, b)


def _modk(c8, ada_w, ada_b, name):
    rows, D = c8.shape
    N = ada_w.shape[0]
    tn = _tile(N, 1536)

    def body(c_ref, w_ref, b_ref, mod_ref, ca_ref):
        ca = _silu(c_ref[...]).astype(BF16)
        mod_ref[...] = _dot(ca, w_ref[...], NT) + b_ref[...]
        ca_ref[...] = ca

    return _pc(body, grid=(N // tn,),
               in_specs=[pl.BlockSpec((rows, D), lambda j: (0, 0)), pl.BlockSpec((tn, D), lambda j: (j, 0)),
                         pl.BlockSpec((1, tn), lambda j: (0, j))],
               out_specs=[pl.BlockSpec((rows, tn), lambda j: (0, j)), pl.BlockSpec((rows, D), lambda j: (0, 0))],
               out_shape=[SDS((rows, N), F32), SDS((rows, D), BF16)], name=name)(c8, ada_w, ada_b)


def _row_tile(S):
    return _tile(S, 512, 8)


def _strips(tm, fn, init=0, rows=None):
    rows = STRIP if rows is None else rows
    assert tm % rows == 0
    return lax.fori_loop(0, tm // rows, lambda r, c: fn(pl.multiple_of(r * rows, rows), c), init)


def _rows8(rows):
    pad = 8 - len(rows)
    return jnp.concatenate(rows + ([jnp.zeros((pad, rows[0].shape[1]), F32)] if pad else []), axis=0)


def _fold8(v):
    return jnp.sum(v.reshape(v.shape[0] // 8, 8, v.shape[1]), axis=0)


def _norm_mod(x, g, mod3, sc_seg, sh_seg, S, name):
    T, D = x.shape
    tm = _row_tile(S)
    tpb = S // tm

    def body(x_ref, g_ref, sc_ref, sh_ref, h_ref):
        x_ = x_ref[...]
        r = lax.rsqrt(jnp.mean(x_ * x_, axis=-1, keepdims=True) + EPS)
        h_ref[...] = ((x_ * r) * (g_ref[...] * (1.0 + sc_ref[...])) + sh_ref[...]).astype(BF16)

    return _pc(body, grid=(T // tm,),
               in_specs=[pl.BlockSpec((tm, D), lambda i: (i, 0)), pl.BlockSpec((1, D), lambda i: (0, 0)),
                         pl.BlockSpec((None, 1, D), lambda i: (i // tpb, 0, sc_seg)),
                         pl.BlockSpec((None, 1, D), lambda i: (i // tpb, 0, sh_seg))],
               out_specs=pl.BlockSpec((tm, D), lambda i: (i, 0)), out_shape=SDS((T, D), BF16), name=name)(x, g, mod3, mod3)


def _resid_post(x, fo, mod3, gt_seg, pg, S, name):
    T, D = x.shape
    tm = _row_tile(S)
    tpb = S // tm

    def body(x_ref, f_ref, gt_ref, pg_ref, o_ref):
        f = f_ref[...]
        r = lax.rsqrt(jnp.mean(f * f, axis=-1, keepdims=True) + EPS)
        o_ref[...] = x_ref[...] + (f * r) * (gt_ref[...] * pg_ref[...])

    return _pc(body, grid=(T // tm,),
               in_specs=[pl.BlockSpec((tm, D), lambda i: (i, 0)), pl.BlockSpec((tm, D), lambda i: (i, 0)),
                         pl.BlockSpec((None, 1, D), lambda i: (i // tpb, 0, gt_seg)),
                         pl.BlockSpec((1, D), lambda i: (0, 0))],
               out_specs=pl.BlockSpec((tm, D), lambda i: (i, 0)), out_shape=SDS((T, D), F32), name=name)(x, fo, mod3, pg)


def _post_bwd(fo, mod3, gt_seg, pg, dout, S, name):
    T, D = fo.shape
    tm = _row_tile(S)
    tpb = S // tm
    nb = T // S

    def body(f_ref, gt_ref, pg_ref, d_ref, df_ref, dgt_ref, dpg_ref):
        i = pl.program_id(0)

        @pl.when(i == 0)
        def _():
            dpg_ref[...] = jnp.zeros_like(dpg_ref)

        @pl.when(i % tpb == 0)
        def _():
            dgt_ref[...] = jnp.zeros_like(dgt_ref)

        f, d = f_ref[...], d_ref[...]
        r = lax.rsqrt(jnp.mean(f * f, axis=-1, keepdims=True) + EPS)
        n = f * r
        dn = d * (gt_ref[...] * pg_ref[...])
        df_ref[...] = (r * (dn - n * jnp.mean(dn * n, axis=-1, keepdims=True))).astype(df_ref.dtype)
        tot = jnp.sum(d * n, axis=0, keepdims=True)
        dgt_ref[...] += _bsum(tot * pg_ref[...])
        dpg_ref[...] += _bsum(tot * gt_ref[...])

    return _pc(body, grid=(T // tm,),
               in_specs=[pl.BlockSpec((tm, D), lambda i: (i, 0)),
                         pl.BlockSpec((None, 1, D), lambda i: (i // tpb, 0, gt_seg)),
                         pl.BlockSpec((1, D), lambda i: (0, 0)), pl.BlockSpec((tm, D), lambda i: (i, 0))],
               out_specs=[pl.BlockSpec((tm, D), lambda i: (i, 0)), pl.BlockSpec((8, D), lambda i: (i // tpb, 0)),
                          pl.BlockSpec((8, D), lambda i: (0, 0))],
               out_shape=[SDS((T, D), BF16), SDS((nb * 8, D), F32), SDS((8, D), F32)], name=name)(fo, mod3, pg, dout)


def _pre_bwd(x, g, mod3, sc_seg, dh, dout, S, name):
    T, D = x.shape
    tm = _row_tile(S)
    tpb = S // tm
    nb = T // S

    def body(x_ref, g_ref, sc_ref, dh_ref, d_ref, dx_ref, dg_ref, dsc_ref, dsh_ref):
        i = pl.program_id(0)

        @pl.when(i == 0)
        def _():
            dg_ref[...] = jnp.zeros_like(dg_ref)

        @pl.when(i % tpb == 0)
        def _():
            dsc_ref[...] = jnp.zeros_like(dsc_ref)
            dsh_ref[...] = jnp.zeros_like(dsh_ref)

        x_, dh_ = x_ref[...], dh_ref[...]
        r = lax.rsqrt(jnp.mean(x_ * x_, axis=-1, keepdims=True) + EPS)
        n = x_ * r
        dn = dh_ * (g_ref[...] * (1.0 + sc_ref[...]))
        dx_ref[...] = d_ref[...] + r * (dn - n * jnp.mean(dn * n, axis=-1, keepdims=True))
        dhn = jnp.sum(dh_ * n, axis=0, keepdims=True)
        dg_ref[...] += _bsum(dhn * (1.0 + sc_ref[...]))
        dsc_ref[...] += _bsum(dhn * g_ref[...])
        dsh_ref[...] += _bsum(jnp.sum(dh_, axis=0, keepdims=True))

    row = pl.BlockSpec((tm, D), lambda i: (i, 0))
    return _pc(body, grid=(T // tm,),
               in_specs=[row, pl.BlockSpec((1, D), lambda i: (0, 0)),
                         pl.BlockSpec((None, 1, D), lambda i: (i // tpb, 0, sc_seg)), row, row],
               out_specs=[row, pl.BlockSpec((8, D), lambda i: (0, 0)), pl.BlockSpec((8, D), lambda i: (i // tpb, 0)),
                          pl.BlockSpec((8, D), lambda i: (i // tpb, 0))],
               out_shape=[SDS((T, D), F32), SDS((8, D), F32), SDS((nb * 8, D), F32), SDS((nb * 8, D), F32)],
               name=name)(x, g, mod3, dh, dout)


def _loss(y, target, S, name):
    T, D = y.shape
    tm = _row_tile(S)

    def body(y_ref, t_ref, dy_ref, l_ref):
        @pl.when(pl.program_id(0) == 0)
        def _():
            l_ref[...] = jnp.zeros_like(l_ref)

        def strip(r0, carry):
            rows = pl.ds(r0, STRIP)
            e = y_ref[rows, :] - t_ref[rows, :]
            dy_ref[rows, :] = e * (1.0 / D)
            return carry + _fold8(e * e)

        acc = _strips(tm, strip, jnp.zeros((8, D), F32))
        l_ref[...] += jnp.broadcast_to(jnp.sum(acc, keepdims=True) * (0.5 / D), l_ref.shape)

    row = pl.BlockSpec((tm, D), lambda i: (i, 0))
    return _pc(body, grid=(T // tm,), in_specs=[row, row],
               out_specs=[row, pl.BlockSpec((8, 128), lambda i: (0, 0))],
               out_shape=[SDS((T, D), F32), SDS((8, 128), F32)], name=name)(y, target)


def _conv_geom(view, C, S):
    arr, off = view
    T = arr.shape[0]
    tm = _row_tile(S)
    tc = _tile(C, 512)
    assert off % tc == 0 and C % tc == 0
    return arr, off // tc, T, tm, tc, S // tm


def _prev_spec(tm, tc, ob, order):
    if order == "ij":
        return pl.BlockSpec((8, tc), lambda i, j: (jnp.maximum(i * (tm // 8) - 1, 0), ob + j))
    return pl.BlockSpec((8, tc), lambda j, i: (jnp.maximum(i * (tm // 8) - 1, 0), ob + j))


def _next_spec(T, tm, tc, ob, order):
    last = T // 8 - 1
    if order == "ij":
        return pl.BlockSpec((8, tc), lambda i, j: (jnp.minimum((i + 1) * (tm // 8), last), ob + j))
    return pl.BlockSpec((8, tc), lambda j, i: (jnp.minimum((i + 1) * (tm // 8), last), ob + j))


def _taps(win, w_ref, K, lead, rows):
    acc = win[lead:lead + rows] * w_ref[K - 1:K, :]
    for j in range(1, K):
        acc = acc + win[lead - j:lead - j + rows] * w_ref[K - 1 - j:K - j, :]
    return acc


def _taps_t(win, w_ref, K, rows):
    acc = win[0:rows] * w_ref[K - 1:K, :]
    for j in range(1, K):
        acc = acc + win[j:j + rows] * w_ref[K - 1 - j:K - j, :]
    return acc


def _conv_fwd(view, C, w8, b, K, S, name):
    arr, ob, T, tm, tc, tps = _conv_geom(view, C, S)

    def body(u_ref, p_ref, w_ref, b_ref, o_ref, buf):
        first = (pl.program_id(0) % tps) == 0
        buf[0:8, :] = jnp.where(first, 0.0, p_ref[...])
        buf[8:, :] = u_ref[...]

        def strip(r0, carry):
            win = buf[pl.ds(r0, STRIP + 8), :]
            o_ref[pl.ds(r0, STRIP), :] = _taps(win, w_ref, K, 8, STRIP) + b_ref[...]
            return carry

        _strips(tm, strip)

    return _pc(body, grid=(T // tm, C // tc),
               in_specs=[pl.BlockSpec((tm, tc), lambda i, j: (i, ob + j)), _prev_spec(tm, tc, ob, "ij"),
                         pl.BlockSpec((8, tc), lambda i, j: (0, j)), pl.BlockSpec((1, tc), lambda i, j: (0, j))],
               out_specs=pl.BlockSpec((tm, tc), lambda i, j: (i, j)), out_shape=SDS((T, C), F32), name=name,
               scratch=(pltpu.VMEM((tm + 8, tc), F32),))(arr, arr, w8, b)


def _conv_bwd_in(dview, C, w8, K, S, out_dtype, name):
    arr, ob, T, tm, tc, tps = _conv_geom(dview, C, S)

    def body(d_ref, n_ref, w_ref, o_ref, buf):
        last = (pl.program_id(0) % tps) == tps - 1
        buf[0:tm, :] = d_ref[...]
        buf[tm:tm + 8, :] = jnp.where(last, 0.0, n_ref[...])

        def strip(r0, carry):
            win = buf[pl.ds(r0, STRIP + 8), :]
            o_ref[pl.ds(r0, STRIP), :] = _taps_t(win, w_ref, K, STRIP).astype(o_ref.dtype)
            return carry

        _strips(tm, strip)

    return _pc(body, grid=(T // tm, C // tc),
               in_specs=[pl.BlockSpec((tm, tc), lambda i, j: (i, ob + j)), _next_spec(T, tm, tc, ob, "ij"),
                         pl.BlockSpec((8, tc), lambda i, j: (0, j))],
               out_specs=pl.BlockSpec((tm, tc), lambda i, j: (i, j)), out_shape=SDS((T, C), out_dtype), name=name,
               scratch=(pltpu.VMEM((tm + 8, tc), F32),))(arr, arr, w8)


def _conv_bwd_w(dview, uview, C, K, S, name):
    darr, dob, T, tm, tc, tps = _conv_geom(dview, C, S)
    uarr, uob, _, _, _, _ = _conv_geom(uview, C, S)

    def body(d_ref, u_ref, p_ref, o_ref, buf):
        i = pl.program_id(1)

        @pl.when(i == 0)
        def _():
            o_ref[...] = jnp.zeros_like(o_ref)

        first = (i % tps) == 0
        buf[0:8, :] = jnp.where(first, 0.0, p_ref[...])
        buf[8:, :] = u_ref[...]

        def strip(r0, carry):
            win = buf[pl.ds(r0, STRIP + 8), :]
            d = d_ref[pl.ds(r0, STRIP), :]
            sums = [_fold8(d * win[8 - (K - 1 - k):8 - (K - 1 - k) + STRIP]) for k in range(K)] + [_fold8(d)]
            return tuple(c + s for c, s in zip(carry, sums))

        acc = _strips(tm, strip, tuple(jnp.zeros((8, tc), F32) for _ in range(K + 1)))
        o_ref[...] += _rows8([jnp.sum(a, axis=0, keepdims=True) for a in acc])

    return _pc(body, grid=(C // tc, T // tm),
               in_specs=[pl.BlockSpec((tm, tc), lambda j, i: (i, dob + j)),
                         pl.BlockSpec((tm, tc), lambda j, i: (i, uob + j)), _prev_spec(tm, tc, uob, "ji")],
               out_specs=pl.BlockSpec((8, tc), lambda j, i: (0, j)), out_shape=SDS((8, C), F32), name=name,
               scratch=(pltpu.VMEM((tm + 8, tc), F32),))(darr, uarr, uarr)


def _ffn_act_fwd(uu, w8, b, S, name):
    K, gw = FFN_CONV_K, GLU_W
    T, F2 = uu.shape
    tm, tc = _row_tile(S), 2 * GLU_W
    tps = S // tm

    def body(u_ref, p_ref, w_ref, b_ref, a_ref, buf):
        first = (pl.program_id(0) % tps) == 0
        buf[0:8, :] = jnp.where(first, 0.0, p_ref[...])
        buf[8:, :] = u_ref[...]

        def strip(r0, carry):
            u = _taps(buf[pl.ds(r0, STRIP + 8), :], w_ref, K, 8, STRIP) + b_ref[...]
            a_ref[pl.ds(r0, STRIP), :] = (_silu(u[:, :gw]) * u[:, gw:]).astype(BF16)
            return carry

        _strips(tm, strip)

    return _pc(body, grid=(T // tm, F2 // tc),
               in_specs=[pl.BlockSpec((tm, tc), lambda i, j: (i, j)), _prev_spec(tm, tc, 0, "ij"),
                         pl.BlockSpec((8, tc), lambda i, j: (0, j)), pl.BlockSpec((1, tc), lambda i, j: (0, j))],
               out_specs=pl.BlockSpec((tm, gw), lambda i, j: (i, j)), out_shape=SDS((T, F2 // 2), BF16), name=name,
               scratch=(pltpu.VMEM((tm + 8, tc), F32),))(uu, uu, w8, b)


def _ffn_act_bwd(uu, da, w8, b, S, name, side=None):
    K, gw = FFN_CONV_K, GLU_W
    T, F2 = uu.shape
    tm, tc = _row_tile(S), 2 * GLU_W
    tps = S // tm
    last16 = T // 16 - 1

    def body(u_ref, p_ref, n_ref, da_ref, dan_ref, w_ref, b_ref, duu_ref, cw_ref, ubuf, dabuf):
        i = pl.program_id(1)

        @pl.when(i == 0)
        def _():
            cw_ref[...] = jnp.zeros_like(cw_ref)

        first = (i % tps) == 0
        last = (i % tps) == tps - 1
        ubuf[0:8, :] = jnp.where(first, 0.0, p_ref[...])
        ubuf[8:tm + 8, :] = u_ref[...]
        ubuf[tm + 8:tm + 16, :] = n_ref[...]
        dabuf[0:tm, :] = da_ref[...].astype(F32)
        dabuf[tm:tm + 8, :] = jnp.where(last, 0.0, dan_ref[...].astype(F32)[0:8, :])

        def strip(r0, carry):
            ext = FFN_STRIP + 8
            win = ubuf[pl.ds(r0, FFN_STRIP + 16), :]
            shifted = [win[8 - j:8 - j + ext] for j in range(K)]
            u = b_ref[...] + shifted[0] * w_ref[K - 1:K, :]
            for j in range(1, K):
                u = u + shifted[j] * w_ref[K - 1 - j:K - j, :]
            da_ = dabuf[pl.ds(r0, ext), :]
            g, v = u[:, :gw], u[:, gw:]
            du = jnp.concatenate([da_ * v * _dsilu(g), da_ * _silu(g)], axis=1)
            duu_ref[pl.ds(r0, FFN_STRIP), :] = _taps_t(du, w_ref, K, FFN_STRIP).astype(BF16)
            dmain = du[0:FFN_STRIP]
            sums = [_fold8(dmain * shifted[K - 1 - k][0:FFN_STRIP]) for k in range(K)] + [_fold8(dmain)]
            return tuple(c + s for c, s in zip(carry, sums))

        acc = _strips(tm, strip, tuple(jnp.zeros((8, tc), F32) for _ in range(K + 1)), rows=FFN_STRIP)
        cw_ref[...] += _rows8([jnp.sum(a, axis=0, keepdims=True) for a in acc])

    return _pc(body, grid=(F2 // tc, T // tm),
               in_specs=[pl.BlockSpec((tm, tc), lambda j, i: (i, j)), _prev_spec(tm, tc, 0, "ji"),
                         _next_spec(T, tm, tc, 0, "ji"), pl.BlockSpec((tm, gw), lambda j, i: (i, j)),
                         pl.BlockSpec((16, gw), lambda j, i: (jnp.minimum((i + 1) * (tm // 16), last16), j)),
                         pl.BlockSpec((8, tc), lambda j, i: (0, j)), pl.BlockSpec((1, tc), lambda j, i: (0, j))],
               out_specs=[pl.BlockSpec((tm, tc), lambda j, i: (i, j)), pl.BlockSpec((8, tc), lambda j, i: (0, j))],
               out_shape=[SDS((T, F2), BF16), SDS((8, F2), F32)], name=name,
               scratch=(pltpu.VMEM((tm + 16, tc), F32), pltpu.VMEM((tm + 8, gw), F32)), side=side)(
                   uu, uu, uu, da, da, w8, b)


def _ssd_common(dtc_raw, dtr_raw, hpc, hpr, L):
    dt_c = _softplus(dtc_raw + hpc[0:1, :])
    a_c = -jnp.exp(hpc[1:2, :])
    dt_r = _softplus(dtr_raw + hpr[:, 0:1])
    a_r = -jnp.exp(hpr[:, 1:2])
    li = lax.broadcasted_iota(jnp.int32, (L, L), 0)
    si = lax.broadcasted_iota(jnp.int32, (L, L), 1)
    low = li >= si
    upp = li <= si
    acs_c = _dotx(low, dt_c * a_c, split="b")
    acs_r = _dotx(dt_r * a_r, upp)
    return dt_c, a_c, acs_c, acs_r, low, upp


def _dotx(a, b, split="a", parts=3, dims=NN):
    val, one = (a, b) if split == "a" else (b, a)
    one = one.astype(BF16)
    acc, rem = None, val
    for i in range(parts):
        piece = rem.astype(BF16)
        t = _dot(piece, one, dims) if split == "a" else _dot(one, piece, dims)
        acc = t if acc is None else acc + t
        if i + 1 < parts:
            rem = rem - piece.astype(F32)
    return acc


def _head_maps(R, P, L):
    RP = R * P
    sel = (lax.broadcasted_iota(jnp.int32, (RP, R), 0) // P == lax.broadcasted_iota(jnp.int32, (RP, R), 1)).astype(F32)
    selt = (lax.broadcasted_iota(jnp.int32, (R, RP), 1) // P == lax.broadcasted_iota(jnp.int32, (R, RP), 0)).astype(F32)
    colb = (lax.broadcasted_iota(jnp.int32, (R, R * L), 1) // L == lax.broadcasted_iota(jnp.int32, (R, R * L), 0)).astype(F32)
    return sel, selt, colb


def _pair_diag(mats, rhs_b, R, P):
    lanes = 2 * P
    lo = lax.broadcasted_iota(jnp.int32, (mats[0].shape[0], lanes), 1) < P
    out = []
    for q in range(R // 2):
        rp = rhs_b[:, q * lanes:(q + 1) * lanes]
        out.append(jnp.where(lo, _dot(mats[2 * q], rp), _dot(mats[2 * q + 1], rp)))
    return jnp.concatenate(out, axis=1) if len(out) > 1 else out[0]


def _ssd_specs(pre, off_x, off_b, off_c, G, R, P, nb, nc, rev):
    L, N, RP = CHUNK, N_STATE, R * P
    cidx = (lambda c: nc - 1 - c) if rev else (lambda c: c)
    xb, bb, cb = off_x // RP, off_b // N, off_c // N
    assert off_x % RP == 0 and off_b % N == 0 and off_c % N == 0
    row = lambda b, c: b * nc + cidx(c)
    return dict(
        x=pl.BlockSpec((L, RP), lambda g, b, c: (row(b, c), xb + g)),
        b=pl.BlockSpec((L, N), lambda g, b, c: (row(b, c), bb + g)),
        c=pl.BlockSpec((L, N), lambda g, b, c: (row(b, c), cb + g)),
        dtc=pl.BlockSpec((None, L, R), lambda g, b, c: (g, row(b, c), 0)),
        dtr=pl.BlockSpec((None, R, L), lambda g, b, c: (g, 0, row(b, c))),
        hpc=pl.BlockSpec((None, 8, R), lambda g, b, c: (g, 0, 0)),
        hpr=pl.BlockSpec((None, R, 8), lambda g, b, c: (g, 0, 0)),
        y=pl.BlockSpec((L, RP), lambda g, b, c: (row(b, c), g)),
        bc=pl.BlockSpec((L, N), lambda g, b, c: (row(b, c), g)),
        hs=pl.BlockSpec((None, None, N, RP), lambda g, b, c: (row(b, c), g, 0, 0)),
    )


def _ssd_fwd(pre, offs, dtc, dtr, hpc, hpr, G, R, P, S, name, side=None):
    T = pre.shape[0]
    L, N, RP = CHUNK, N_STATE, R * P
    nc, nb = S // L, T // S
    sp = _ssd_specs(pre, *offs, G, R, P, nb, nc, False)

    def body(px_ref, pb_ref, pc_ref, dtc_ref, dtr_ref, hpc_ref, hpr_ref, y_ref, hs_ref, hst):
        @pl.when(pl.program_id(2) == 0)
        def _():
            hst[...] = jnp.zeros_like(hst)

        xs, bm, cm = _silu(px_ref[...]), _silu(pb_ref[...]), _silu(pc_ref[...])
        hpc_ = hpc_ref[...]
        dt_c, _, acs_c, acs_r, low, _ = _ssd_common(dtc_ref[...], dtr_ref[...], hpc_, hpr_ref[...], L)
        _, selt, colb = _head_maps(R, P, L)
        dt_e, a_e, hp_e = _dotx(dt_c, selt), _dotx(acs_c, selt), _dotx(hpc_, selt)
        a_bc = _dotx(acs_c, colb)
        a_last = a_e[L - 1:L, :]
        bb, cb = bm.astype(BF16), cm.astype(BF16)
        gm = _dot(cb, bb, NT)
        hprev = hst[...]
        hprev_b = hprev.astype(BF16)
        hs_ref[...] = hprev_b
        xdt = xs * dt_e
        xdt_b = xdt.astype(BF16)
        ms = []
        for r in range(R):
            dec = jnp.exp(jnp.where(low, a_bc[:, r * L:(r + 1) * L] - acs_r[r:r + 1, :], -jnp.inf))
            ms.append((gm * dec).astype(BF16))
        y = _pair_diag(ms, xdt_b, R, P) + _dot(cb, hprev_b) * jnp.exp(a_e) + hp_e[2:3, :] * xs
        y_ref[...] = y
        xw = (xdt * jnp.exp(a_last - a_e)).astype(BF16)
        hst[...] = hprev * jnp.exp(a_last) + _dot(bb, xw, TN)

    return _pc(body, grid=(G, nb, nc),
               in_specs=[sp["x"], sp["b"], sp["c"], sp["dtc"], sp["dtr"], sp["hpc"], sp["hpr"]],
               out_specs=[sp["y"], sp["hs"]],
               out_shape=[SDS((T, G * RP), F32), SDS((nb * nc, G, N, RP), BF16)], name=name,
               scratch=(pltpu.VMEM((N, RP), F32),), side=side)(pre, pre, pre, dtc, dtr, hpc, hpr)


def _ssd_bwd(pre, offs, dtc, dtr, hpc, hpr, hs, dy, G, R, P, S, name):
    T = pre.shape[0]
    L, N, RP = CHUNK, N_STATE, R * P
    nc, nb = S // L, T // S
    sp = _ssd_specs(pre, *offs, G, R, P, nb, nc, True)

    def body(px_ref, pb_ref, pc_ref, dtc_ref, dtr_ref, hpc_ref, hpr_ref, hs_ref, dy_ref,
             dpx_ref, dpb_ref, dpc_ref, ddt_ref, hpg_ref, dhst):
        bi, ci = pl.program_id(1), pl.program_id(2)

        @pl.when(ci == 0)
        def _():
            dhst[...] = jnp.zeros_like(dhst)

        @pl.when((bi == 0) & (ci == 0))
        def _():
            hpg_ref[...] = jnp.zeros_like(hpg_ref)

        px, pb, pcc = px_ref[...], pb_ref[...], pc_ref[...]
        xs, bm, cm = _silu(px), _silu(pb), _silu(pcc)
        hpc_ = hpc_ref[...]
        dtc_raw = dtc_ref[...]
        dt_c, a_c, acs_c, acs_r, low, upp = _ssd_common(dtc_raw, dtr_ref[...], hpc_, hpr_ref[...], L)
        sel, selt, colb = _head_maps(R, P, L)
        dt_e, a_e, hp_e = _dotx(dt_c, selt), _dotx(acs_c, selt), _dotx(hpc_, selt)
        a_bc = _dotx(acs_c, colb)
        a_last = a_e[L - 1:L, :]
        e_e, w_e = jnp.exp(a_e), jnp.exp(a_last - a_e)
        bb, cb = bm.astype(BF16), cm.astype(BF16)
        gm = _dot(cb, bb, NT)
        gmt = _dot(bb, cb, NT)
        hprev = hs_ref[...]
        dhn = dhst[...]
        dhn_b = dhn.astype(BF16)
        dy = dy_ref[...]
        dy_b = dy.astype(BF16)
        xdt = xs * dt_e
        xdt_b = xdt.astype(BF16)
        yoff = _dot(cb, hprev) * e_e
        dye_b = (dy * e_e).astype(BF16)
        dcm = _dot(dye_b, hprev, NT)
        dhst[...] = _dot(cb, dye_b, TN) + jnp.exp(a_last) * dhn
        dxdt_st = _dot(bb, dhn_b) * w_e
        dbm = _dot((xdt * w_e).astype(BF16), dhn_b, NT)
        lanes = 2 * P
        lo = lax.broadcasted_iota(jnp.int32, (L, lanes), 1) < P
        dg = jnp.zeros((L, L), F32)
        es, css = [], []
        for r in range(R):
            col_b, row = a_bc[:, r * L:(r + 1) * L], acs_r[r:r + 1, :]
            dec = jnp.exp(jnp.where(low, col_b - row, -jnp.inf))
            q = r // 2
            dyp = dy_b[:, q * lanes:(q + 1) * lanes]
            dyp = jnp.where(lo if r % 2 == 0 else ~lo, dyp, jnp.zeros_like(dyp))
            dm = _dot(dyp, xdt_b[:, q * lanes:(q + 1) * lanes], NT)
            dg = dg + dm * dec
            e = dm * (gm * dec)
            es.append(e)
            css.append(jnp.sum(e, axis=0, keepdims=True))
        dgb = dg.astype(BF16)
        dcm = dcm + _dot(dgb, bb)
        dbm = dbm + _dot(dgb, cb, TN)
        colbt = (lax.broadcasted_iota(jnp.int32, (R * L, R), 0) // L
                 == lax.broadcasted_iota(jnp.int32, (R * L, R), 1)).astype(F32)
        eye = (lax.broadcasted_iota(jnp.int32, (R, R), 0) == lax.broadcasted_iota(jnp.int32, (R, R), 1)).astype(F32)
        row_sums = _dotx(jnp.concatenate(es, axis=1), colbt)
        col_sums = _dotx(jnp.concatenate(css, axis=0), eye, dims=TN)
        mts = []
        for r in range(R):
            dect = jnp.exp(jnp.where(upp, acs_r[r:r + 1, :] - a_bc[:, r * L:(r + 1) * L], -jnp.inf))
            mts.append((gmt * dect).astype(BF16))
        dxdt = _pair_diag(mts, dy_b, R, P) + dxdt_st
        q_st = _dotx(xdt * dxdt_st, sel, parts=1)
        da = row_sums - col_sums + _dotx(dy * yoff, sel, parts=1) - q_st
        hh = jnp.sum(_dotx(dhn * hprev.astype(F32), sel, parts=1), axis=0, keepdims=True)
        da_last = jnp.exp(acs_c[L - 1:L, :]) * hh + jnp.sum(q_st, axis=0, keepdims=True)
        rowi = lax.broadcasted_iota(jnp.int32, (L, R), 0)
        da = da + jnp.where(rowi == L - 1, da_last, 0.0)
        dpx_ref[...] = (dxdt * dt_e + hp_e[2:3, :] * dy) * _dsilu(px)
        dpb_ref[...] = dbm * _dsilu(pb)
        dpc_ref[...] = dcm * _dsilu(pcc)
        dadt = _dotx(upp, da, split="b")
        ddt = _dotx(dxdt * xs, sel, parts=1) + dadt * a_c
        ddt_raw = ddt * jax.nn.sigmoid(dtc_raw + hpc_[0:1, :])
        ddt_ref[...] = ddt_raw
        d_a = jnp.sum(dadt * dt_c, axis=0, keepdims=True)
        d_d = jnp.sum(_dotx(dy * xs, sel, parts=1), axis=0, keepdims=True)
        rows = [jnp.sum(ddt_raw, axis=0, keepdims=True), d_a * a_c, d_d, jnp.zeros((5, R), F32)]
        hpg_ref[...] += jnp.concatenate(rows, axis=0)

    return _pc(body, grid=(G, nb, nc),
               in_specs=[sp["x"], sp["b"], sp["c"], sp["dtc"], sp["dtr"], sp["hpc"], sp["hpr"], sp["hs"], sp["y"]],
               out_specs=[sp["y"], sp["bc"], sp["bc"], sp["dtc"], pl.BlockSpec((None, 8, R), lambda g, b, c: (g, 0, 0))],
               out_shape=[SDS((T, G * RP), F32), SDS((T, G * N), F32), SDS((T, G * N), F32), SDS((G, T, R), F32),
                          SDS((G, 8, R), F32)], name=name,
               scratch=(pltpu.VMEM((N, RP), F32),))(pre, pre, pre, dtc, dtr, hpc, hpr, hs, dy)


def _gate_norm_fwd(y, zview, ng, G, S, name):
    T, DI = y.shape
    zarr, zoff = zview
    gw = DI // G
    tm = _row_tile(S)
    zb = zoff // gw
    assert zoff % gw == 0

    def body(y_ref, z_ref, g_ref, o_ref):
        yg = y_ref[...] * _silu(z_ref[...])
        r = lax.rsqrt(jnp.mean(yg * yg, axis=-1, keepdims=True) + EPS)
        o_ref[...] = (yg * r * g_ref[...]).astype(BF16)

    return _pc(body, grid=(T // tm, G),
               in_specs=[pl.BlockSpec((tm, gw), lambda i, g: (i, g)), pl.BlockSpec((tm, gw), lambda i, g: (i, zb + g)),
                         pl.BlockSpec((1, gw), lambda i, g: (0, g))],
               out_specs=pl.BlockSpec((tm, gw), lambda i, g: (i, g)), out_shape=SDS((T, DI), BF16), name=name)(y, zarr, ng)


def _gate_norm_bwd(y, zview, ng, dyn, G, S, name):
    T, DI = y.shape
    zarr, zoff = zview
    gw = DI // G
    tm = _row_tile(S)
    zb = zoff // gw

    def body(y_ref, z_ref, g_ref, d_ref, dy_ref, dz_ref, dg_ref):
        @pl.when(pl.program_id(1) == 0)
        def _():
            dg_ref[...] = jnp.zeros_like(dg_ref)

        y_, z, d = y_ref[...], z_ref[...], d_ref[...]
        sz = _silu(z)
        yg = y_ * sz
        r = lax.rsqrt(jnp.mean(yg * yg, axis=-1, keepdims=True) + EPS)
        n = yg * r
        dn = d * g_ref[...]
        dyg = r * (dn - n * jnp.mean(dn * n, axis=-1, keepdims=True))
        dy_ref[...] = dyg * sz
        dz_ref[...] = (dyg * y_ * _dsilu(z)).astype(BF16)
        dg_ref[...] += _bsum(jnp.sum(d * n, axis=0, keepdims=True))

    return _pc(body, grid=(G, T // tm),
               in_specs=[pl.BlockSpec((tm, gw), lambda g, i: (i, g)), pl.BlockSpec((tm, gw), lambda g, i: (i, zb + g)),
                         pl.BlockSpec((1, gw), lambda g, i: (0, g)), pl.BlockSpec((tm, gw), lambda g, i: (i, g))],
               out_specs=[pl.BlockSpec((tm, gw), lambda g, i: (i, g)), pl.BlockSpec((tm, gw), lambda g, i: (i, g)),
                          pl.BlockSpec((8, gw), lambda g, i: (0, g))],
               out_shape=[SDS((T, DI), F32), SDS((T, DI), BF16), SDS((8, DI), F32)], name=name)(y, zarr, ng, dyn)


def _shortconv_fwd(proj, off_b, off_c, off_h, C, w8, S, name):
    K = SC_CONV_K
    _, ob, T, tm, tc, tps = _conv_geom((proj, off_b), C, S)
    oc, oh = off_c // tc, off_h // tc

    def body(b_ref, c_ref, h_ref, cp_ref, hp_ref, w_ref, o_ref, buf):
        first = (pl.program_id(0) % tps) == 0
        buf[0:8, :] = jnp.where(first, 0.0, cp_ref[...] * hp_ref[...])
        buf[8:, :] = c_ref[...] * h_ref[...]

        def strip(r0, carry):
            conv = _taps(buf[pl.ds(r0, STRIP + 8), :], w_ref, K, 8, STRIP)
            o_ref[pl.ds(r0, STRIP), :] = (b_ref[pl.ds(r0, STRIP), :] * conv).astype(BF16)
            return carry

        _strips(tm, strip)

    blk = lambda o: pl.BlockSpec((tm, tc), lambda i, j: (i, o + j))
    return _pc(body, grid=(T // tm, C // tc),
               in_specs=[blk(ob), blk(oc), blk(oh), _prev_spec(tm, tc, oc, "ij"), _prev_spec(tm, tc, oh, "ij"),
                         pl.BlockSpec((8, tc), lambda i, j: (0, j))],
               out_specs=pl.BlockSpec((tm, tc), lambda i, j: (i, j)), out_shape=SDS((T, C), BF16), name=name,
               scratch=(pltpu.VMEM((tm + 8, tc), F32),))(proj, proj, proj, proj, proj, w8)


def _shortconv_bwd(proj, off_b, off_c, off_h, C, w8, dsc, S, name):
    K = SC_CONV_K
    _, ob, T, tm, tc, tps = _conv_geom((proj, off_b), C, S)
    oc, oh = off_c // tc, off_h // tc

    def body(b_ref, c_ref, h_ref, cp_ref, hp_ref, bn_ref, d_ref, dn_ref, w_ref,
             db_ref, dc_ref, dh_ref, dw_ref, buf, buf2):
        i = pl.program_id(1)

        @pl.when(i == 0)
        def _():
            dw_ref[...] = jnp.zeros_like(dw_ref)

        first = (i % tps) == 0
        last = (i % tps) == tps - 1
        buf[0:8, :] = jnp.where(first, 0.0, cp_ref[...] * hp_ref[...])
        buf[8:, :] = c_ref[...] * h_ref[...]
        buf2[0:tm, :] = d_ref[...] * b_ref[...]
        buf2[tm:tm + 8, :] = jnp.where(last, 0.0, dn_ref[...] * bn_ref[...])

        def strip(r0, carry):
            rows = pl.ds(r0, STRIP)
            vwin = buf[pl.ds(r0, STRIP + 8), :]
            vs = [vwin[8 - j:8 - j + STRIP] for j in range(K)]
            conv = vs[0] * w_ref[K - 1:K, :]
            for j in range(1, K):
                conv = conv + vs[j] * w_ref[K - 1 - j:K - j, :]
            db_ref[rows, :] = (d_ref[rows, :] * conv).astype(BF16)
            dwin = buf2[pl.ds(r0, STRIP + 8), :]
            dv = _taps_t(dwin, w_ref, K, STRIP)
            dc_ref[rows, :] = (dv * h_ref[rows, :]).astype(BF16)
            dh_ref[rows, :] = (dv * c_ref[rows, :]).astype(BF16)
            dconv = dwin[0:STRIP]
            sums = [_fold8(dconv * vs[K - 1 - k]) for k in range(K)]
            return tuple(c + s for c, s in zip(carry, sums))

        acc = _strips(tm, strip, tuple(jnp.zeros((8, tc), F32) for _ in range(K)))
        dw_ref[...] += _rows8([jnp.sum(a, axis=0, keepdims=True) for a in acc])

    blk = lambda o: pl.BlockSpec((tm, tc), lambda j, i: (i, o + j))
    out = pl.BlockSpec((tm, tc), lambda j, i: (i, j))
    return _pc(body, grid=(C // tc, T // tm),
               in_specs=[blk(ob), blk(oc), blk(oh), _prev_spec(tm, tc, oc, "ji"), _prev_spec(tm, tc, oh, "ji"),
                         _next_spec(T, tm, tc, ob, "ji"), blk(0), _next_spec(T, tm, tc, 0, "ji"),
                         pl.BlockSpec((8, tc), lambda j, i: (0, j))],
               out_specs=[out, out, out, pl.BlockSpec((8, tc), lambda j, i: (0, j))],
               out_shape=[SDS((T, C), BF16)] * 3 + [SDS((8, C), F32)], name=name,
               scratch=(pltpu.VMEM((tm + 8, tc), F32), pltpu.VMEM((tm + 8, tc), F32)))(
                   proj, proj, proj, proj, proj, proj, dsc, dsc, w8)


def _merge_fwd(proj, off_g1, off_g2, y1, y2, S, name):
    T, D = y1.shape
    tm = _row_tile(S)
    o1, o2 = off_g1 // D, off_g2 // D
    assert off_g1 % D == 0 and off_g2 % D == 0

    def body(g1_ref, g2_ref, y1_ref, y2_ref, o_ref):
        def strip(r0, carry):
            rows = pl.ds(r0, STRIP)
            o_ref[rows, :] = (jax.nn.sigmoid(g1_ref[rows, :]) * y1_ref[rows, :]
                              + jax.nn.sigmoid(g2_ref[rows, :]) * y2_ref[rows, :]).astype(BF16)
            return carry

        _strips(tm, strip)

    row = pl.BlockSpec((tm, D), lambda i: (i, 0))
    return _pc(body, grid=(T // tm,),
               in_specs=[pl.BlockSpec((tm, D), lambda i: (i, o1)), pl.BlockSpec((tm, D), lambda i: (i, o2)), row, row],
               out_specs=row, out_shape=SDS((T, D), BF16), name=name)(proj, proj, y1, y2)


def _merge_bwd(proj, off_g1, off_g2, y1, y2, dm, S, name):
    T, D = y1.shape
    tm = _row_tile(S)
    o1, o2 = off_g1 // D, off_g2 // D

    def body(g1_ref, g2_ref, y1_ref, y2_ref, d_ref, dy1_ref, dy2_ref, dg1_ref, dg2_ref):
        def strip(r0, carry):
            rows = pl.ds(r0, STRIP)
            d = d_ref[rows, :]
            s1, s2 = jax.nn.sigmoid(g1_ref[rows, :]), jax.nn.sigmoid(g2_ref[rows, :])
            dy1_ref[rows, :] = (d * s1).astype(BF16)
            dy2_ref[rows, :] = (d * s2).astype(BF16)
            dg1_ref[rows, :] = (d * y1_ref[rows, :] * s1 * (1.0 - s1)).astype(BF16)
            dg2_ref[rows, :] = (d * y2_ref[rows, :] * s2 * (1.0 - s2)).astype(BF16)
            return carry

        _strips(tm, strip)

    row = pl.BlockSpec((tm, D), lambda i: (i, 0))
    return _pc(body, grid=(T // tm,),
               in_specs=[pl.BlockSpec((tm, D), lambda i: (i, o1)), pl.BlockSpec((tm, D), lambda i: (i, o2)), row, row, row],
               out_specs=[row] * 4, out_shape=[SDS((T, D), BF16)] * 4, name=name)(proj, proj, y1, y2, dm)


def _pad8(w):
    return jnp.pad(w, ((0, 8 - w.shape[0]), (0, 0)))


def _dims(w):
    D = w["mix_pre_g"].shape[-1]
    DI = w["ssd_norm_g"].shape[-1]
    H = w["ssd_dt_bias"].shape[-1]
    conv_dim = w["ssd_conv_b"].shape[-1]
    G = (conv_dim - DI) // (2 * N_STATE)
    F = w["w_down"].shape[0]
    return dict(D=D, DI=DI, H=H, P=DI // H, G=G, R=H // G, GN=G * N_STATE, CD=conv_dim, F=F)


def _proj_layout(d):
    D, DI, CD, H = d["D"], d["DI"], d["CD"], d["H"]
    o = dict(z=0, xbc=DI, scb=DI + CD, scc=DI + CD + D, sch=DI + CD + 2 * D, g1=DI + CD + 3 * D, g2=DI + CD + 4 * D,
             dt=DI + CD + 5 * D)
    o["sb"] = math.gcd(SEG_BLK, D, DI, d["GN"])
    assert o["sb"] % 128 == 0 and H <= o["sb"]
    o["np"] = o["dt"] + o["sb"]
    return o


def _glu_perm(a, F, inverse=False):
    lead = a.shape[:-1]
    nb = F // GLU_W
    if not inverse:
        return a.reshape(*lead, 2, nb, GLU_W).swapaxes(-3, -2).reshape(*lead, 2 * F)
    return a.reshape(*lead, nb, 2, GLU_W).swapaxes(-3, -2).reshape(*lead, 2 * F)


def _glu_perm_rows(a, F, inverse=False):
    nb, D = F // GLU_W, a.shape[1]
    shape = (nb, 2, GLU_W, D) if inverse else (2, nb, GLU_W, D)
    return a.reshape(shape).swapaxes(0, 1).reshape(2 * F, D)


def _prep_layer(w):
    d = _dims(w)
    D, DI, CD, H, G, R, F = d["D"], d["DI"], d["CD"], d["H"], d["G"], d["R"], d["F"]
    lay = _proj_layout(d)
    w_in = w["w_in"]
    used = lay["dt"] + H
    wcat = jnp.concatenate([w_in[:DI + CD], w_in[DI + CD + H:], w_in[DI + CD:DI + CD + H],
                            jnp.zeros((lay["np"] - used, D), w_in.dtype)], axis=0)
    hp = jnp.stack([w["ssd_dt_bias"], w["ssd_a_log"], w["ssd_d"]], 0).astype(F32)
    hpc = jnp.pad(hp.reshape(3, G, R).transpose(1, 0, 2), ((0, 0), (0, 5), (0, 0)))
    hpr = jnp.pad(hp[:2].reshape(2, G, R).transpose(1, 2, 0), ((0, 0), (0, 0), (0, 6)))
    row = lambda v: v.reshape(1, -1).astype(F32)
    return dict(
        d=d, lay=lay, ada_w=w["ada_w"].astype(BF16), ada_b=row(w["ada_b"]),
        mix_pre_g=row(w["mix_pre_g"]), mix_post_g=row(w["mix_post_g"]), wcat=wcat.astype(BF16),
        ssd_conv_w=_pad8(w["ssd_conv_w"].astype(F32)), ssd_conv_b=row(w["ssd_conv_b"]), hpc=hpc, hpr=hpr,
        ssd_norm_g=row(w["ssd_norm_g"]), w_ssd_out=w["w_ssd_out"].astype(BF16),
        sc_conv_w=_pad8(w["sc_conv_w"].astype(F32)), w_sc_out=w["w_sc_out"].astype(BF16), w_o=w["w_o"].astype(BF16),
        ffn_pre_g=row(w["ffn_pre_g"]), ffn_post_g=row(w["ffn_post_g"]),
        w_up=_glu_perm_rows(w["w_up"], F).astype(BF16), ffn_conv_w=_pad8(_glu_perm(w["ffn_conv_w"].astype(F32), F)),
        ffn_conv_b=_glu_perm(row(w["ffn_conv_b"]), F), w_down=w["w_down"].astype(BF16))


def _dt_layouts(proj, lay, d):
    T = proj.shape[0]
    dt = proj[:, lay["dt"]:lay["dt"] + d["H"]].reshape(T, d["G"], d["R"])
    return dt.transpose(1, 0, 2), dt.transpose(1, 2, 0)


def _layer_fwd(x, c8, p, S, li, gather=None):
    d, lay = p["d"], p["lay"]
    D, DI, G, R, P, GN, CD = d["D"], d["DI"], d["G"], d["R"], d["P"], d["GN"], d["CD"]
    nb = x.shape[0] // S
    nm = lambda s: f"l{li}_{s}"
    mod, cact = _modk(c8, p["ada_w"], p["ada_b"], nm("mod"))
    mod3 = mod[:nb].reshape(nb, 1, 6 * D)
    h = _norm_mod(x, p["mix_pre_g"], mod3, 1, 0, S, nm("norm1"))
    proj = _mm(h, p["wcat"], "nt", F32, nm("mm_in"), caps=(1024, 1536, 2048))
    pre = _conv_fwd((proj, lay["xbc"]), CD, p["ssd_conv_w"], p["ssd_conv_b"], SSD_CONV_K, S, nm("ssdconv"))
    dtc, dtr = _dt_layouts(proj, lay, d)
    offs = (0, DI, DI + GN)
    gathered = None
    if gather is None:
        y, hs = _ssd_fwd(pre, offs, dtc, dtr, p["hpc"], p["hpr"], G, R, P, S, nm("ssd"))
    else:
        (y, hs), (gathered,) = _ssd_fwd(pre, offs, dtc, dtr, p["hpc"], p["hpr"], G, R, P, S, nm("ssd"),
                                        side=_side_gather_direct(gather))
    yn = _gate_norm_fwd(y, (proj, lay["z"]), p["ssd_norm_g"], G, S, nm("gnorm"))
    sc = _shortconv_fwd(proj, lay["scb"], lay["scc"], lay["sch"], D, p["sc_conv_w"], S, nm("sconv"))
    if gather is None:
        y_ssd = _mm(yn, p["w_ssd_out"], "nn", F32, nm("mm_ssdout"))
    else:
        y_ssd, (gathered,) = _mm(yn, p["w_ssd_out"], "nn", F32, nm("mm_ssdout"), side=_side_gather_forward(gathered))
    y_sc = _mm(sc, p["w_sc_out"], "nn", F32, nm("mm_scout"))
    m = _merge_fwd(proj, lay["g1"], lay["g2"], y_ssd, y_sc, S, nm("merge"))
    mix = _mm(m, p["w_o"], "nn", F32, nm("mm_o"))
    x1 = _resid_post(x, mix, mod3, 2, p["mix_post_g"], S, nm("post1"))
    h2 = _norm_mod(x1, p["ffn_pre_g"], mod3, 4, 3, S, nm("norm2"))
    uu = _mm(h2, p["w_up"], "nt", F32, nm("mm_up"), caps=(1024, 1408, 2048))
    a = _ffn_act_fwd(uu, p["ffn_conv_w"], p["ffn_conv_b"], S, nm("ffnact"))
    f = _mm(a, p["w_down"], "nn", F32, nm("mm_down"), caps=(1024, 1024, 1408))
    x2 = _resid_post(x1, f, mod3, 5, p["ffn_post_g"], S, nm("post2"))
    saved = dict(x=x, h=h, proj=proj, pre=pre, dtc=dtc, dtr=dtr, y=y, hs=hs, yn=yn, sc=sc, y_ssd=y_ssd, y_sc=y_sc,
                 m=m, mix=mix, x1=x1, h2=h2, uu=uu, a=a, f=f, mod3=mod3, cact=cact)
    return x2, saved, gathered


def _seq_sum(acc, nb):
    return acc.reshape(nb, 8, -1)[:, 0, :]


def _layer_bwd(dx2, p, s, S, li, chip_sums=None):
    d, lay = p["d"], p["lay"]
    D, DI, G, R, P, GN, CD, H, F = d["D"], d["DI"], d["G"], d["R"], d["P"], d["GN"], d["CD"], d["H"], d["F"]
    nb = dx2.shape[0] // S
    nm = lambda t: f"l{li}_{t}"
    mod3 = s["mod3"]
    g = {}
    exchanged = None
    df, dgt2, dpg2 = _post_bwd(s["f"], mod3, 5, p["ffn_post_g"], dx2, S, nm("post2_b"))
    g["ffn_post_g"] = dpg2[0]
    da = _mm(df, p["w_down"], "nt", BF16, nm("mm_down_bi"), caps=(1024, 1408, 2048))
    g["w_down"] = _mm(s["a"], df, "tn", WGRAD,nm("mm_down_bw"), caps=(1408, 1024, 1024))
    if chip_sums is None:
        duu, cw = _ffn_act_bwd(s["uu"], da, p["ffn_conv_w"], p["ffn_conv_b"], S, nm("ffnact_b"))
    else:
        (duu, cw), (exchanged,) = _ffn_act_bwd(s["uu"], da, p["ffn_conv_w"], p["ffn_conv_b"], S, nm("ffnact_b"),
                                               side=_side_chip_exchange(chip_sums))
    g["ffn_conv_w"] = _glu_perm(cw[:FFN_CONV_K], F, inverse=True)
    g["ffn_conv_b"] = _glu_perm(cw[FFN_CONV_K], F, inverse=True)
    dh2 = _mm(duu, p["w_up"], "nn", F32, nm("mm_up_bi"), caps=(1024, 1024, 2816))
    g["w_up"] = _glu_perm_rows(_mm(duu, s["h2"], "tn", WGRAD,nm("mm_up_bw"), caps=(1408, 1024, 1024)), F, inverse=True)
    dx1, dg2, dsc2, dsh2 = _pre_bwd(s["x1"], p["ffn_pre_g"], mod3, 4, dh2, dx2, S, nm("norm2_b"))
    g["ffn_pre_g"] = dg2[0]
    dmix, dgt1, dpg1 = _post_bwd(s["mix"], mod3, 2, p["mix_post_g"], dx1, S, nm("post1_b"))
    g["mix_post_g"] = dpg1[0]
    dm = _mm(dmix, p["w_o"], "nt", F32, nm("mm_o_bi"))
    g["w_o"] = _mm(s["m"], dmix, "tn", WGRAD,nm("mm_o_bw"))
    proj = s["proj"]
    dy_ssd, dy_sc, dg1, dg2_ = _merge_bwd(proj, lay["g1"], lay["g2"], s["y_ssd"], s["y_sc"], dm, S, nm("merge_b"))
    dyn = _mm(dy_ssd, p["w_ssd_out"], "nt", F32, nm("mm_ssdout_bi"))
    g["w_ssd_out"] = _mm(s["yn"], dy_ssd, "tn", WGRAD,nm("mm_ssdout_bw"))
    dsc = _mm(dy_sc, p["w_sc_out"], "nt", F32, nm("mm_scout_bi"))
    g["w_sc_out"] = _mm(s["sc"], dy_sc, "tn", WGRAD,nm("mm_scout_bw"))
    dscb, dscc, dsch, scw = _shortconv_bwd(proj, lay["scb"], lay["scc"], lay["sch"], D, p["sc_conv_w"], dsc, S, nm("sconv_b"))
    g["sc_conv_w"] = scw[:SC_CONV_K]
    dy, dz, dng = _gate_norm_bwd(s["y"], (proj, lay["z"]), p["ssd_norm_g"], dyn, G, S, nm("gnorm_b"))
    g["ssd_norm_g"] = dng[0]
    offs = (0, DI, DI + GN)
    dpx, dpb, dpc, ddt, hpg = _ssd_bwd(s["pre"], offs, s["dtc"], s["dtr"], p["hpc"], p["hpr"], s["hs"], dy,
                                       G, R, P, S, nm("ssd_b"))
    g["ssd_dt_bias"], g["ssd_a_log"], g["ssd_d"] = hpg[:, 0, :].reshape(H), hpg[:, 1, :].reshape(H), hpg[:, 2, :].reshape(H)
    cws, dxbc = [], []
    for name, darr, off, C in (("x", dpx, 0, DI), ("b", dpb, DI, GN), ("c", dpc, DI + GN, GN)):
        w8 = p["ssd_conv_w"][:, off:off + C]
        cws.append(_conv_bwd_w((darr, 0), (proj, lay["xbc"] + off), C, SSD_CONV_K, S, nm(f"ssdconv_bw_{name}")))
        dxbc.append(_conv_bwd_in((darr, 0), C, w8, SSD_CONV_K, S, BF16, nm(f"ssdconv_bi_{name}")))
    cws = jnp.concatenate(cws, axis=1)
    g["ssd_conv_w"], g["ssd_conv_b"] = cws[:SSD_CONV_K], cws[SSD_CONV_K]
    T = dx2.shape[0]
    ddt_t = jnp.pad(ddt.transpose(1, 0, 2).reshape(T, H).astype(BF16), ((0, 0), (0, lay["sb"] - H)))
    dproj = [dz] + dxbc + [dscb, dscc, dsch, dg1, dg2_, ddt_t]
    dh = _mm_seg(dproj, p["wcat"], "nn", F32, nm("mm_in_bi"), lay["sb"])
    dwcat = _mm_seg(dproj, s["h"], "tn", WGRAD, nm("mm_in_bw"), lay["sb"], tk=1024)
    o = lay
    g["w_in"] = jnp.concatenate([dwcat[o["z"]:o["scb"]], dwcat[o["dt"]:o["dt"] + H], dwcat[o["scb"]:o["dt"]]], axis=0)
    dx, dg1_, dsc1, dsh1 = _pre_bwd(s["x"], p["mix_pre_g"], mod3, 1, dh, dx1, S, nm("norm1_b"))
    g["mix_pre_g"] = dg1_[0]
    dmod = jnp.concatenate([_seq_sum(t, nb) for t in (dsh1, dsc1, dgt1, dsh2, dsc2, dgt2)], axis=1)
    dmod8 = jnp.pad(dmod, ((0, MOD_ROWS - nb), (0, 0)))
    g["ada_b"] = _colsum(dmod8, nm("adab"))
    g["ada_w"] = _mm(dmod8, s["cact"], "tn", WGRAD, nm("mm_ada_bw"), caps=(1536, 1024, 2048))
    return dx, g, exchanged


def _colsum(a8, name):
    rows, C = a8.shape
    tc = _tile(C, 2048)

    def body(a_ref, o_ref):
        o_ref[...] = _bsum(jnp.sum(a_ref[...], axis=0, keepdims=True))

    return _pc(body, grid=(C // tc,), in_specs=[pl.BlockSpec((rows, tc), lambda j: (0, j))],
               out_specs=pl.BlockSpec((8, tc), lambda j: (0, j)), out_shape=SDS((8, C), F32), name=name)(a8)[0]


def _adam(gs, w, m, v, name):
    ns, R, W = gs.shape
    tr = _tile(R, 256, 8)

    def body(g_ref, w_ref, m_ref, v_ref, go_ref, d_ref, mo_ref, vo_ref):
        g = g_ref[0].astype(F32)
        for k in range(1, ns):
            g = g + g_ref[k].astype(F32)
        go_ref[...] = g
        d_ref[...], mo_ref[...], vo_ref[...] = _adam_update(g, w_ref[...], m_ref[...], v_ref[...])

    row = pl.BlockSpec((tr, W), lambda i: (i, 0))
    return _pc(body, grid=(R // tr,), in_specs=[pl.BlockSpec((ns, tr, W), lambda i: (0, i, 0)), row, row, row],
               out_specs=[row] * 4, out_shape=[SDS((R, W), F32)] * 4, name=name)(gs, w, m, v)


def _adam_update(g, w, m, v):
    c1 = 1.0 / (1.0 - ADAM_B1 ** ADAM_STEP)
    c2 = 1.0 / (1.0 - ADAM_B2 ** ADAM_STEP)
    m_ = ADAM_B1 * m + (1.0 - ADAM_B1) * g
    v_ = ADAM_B2 * v + (1.0 - ADAM_B2) * (g * g)
    return -ADAM_LR * ((m_ * c1) / (jnp.sqrt(v_ * c2) + ADAM_EPS) + ADAM_WD * w), m_, v_


def _adam_nat(g, w, m, v, name):
    depth, a, b = w.shape
    tr = _tile(a, 256, 8)

    def body(g_ref, w_ref, m_ref, v_ref, d_ref, mo_ref, vo_ref):
        d_ref[...], mo_ref[...], vo_ref[...] = _adam_update(g_ref[...], w_ref[...], m_ref[...], v_ref[...])

    blk = pl.BlockSpec((None, tr, b), lambda l, i: (l, i, 0))
    return _pc(body, grid=(depth, a // tr), in_specs=[blk] * 4, out_specs=[blk] * 3,
               out_shape=[SDS(w.shape, F32)] * 3, name=name)(g, w, m, v)


def _sum_chips(gs, name):
    ns, R, W = gs.shape
    tr = _tile(R, 256, 16)

    def body(g_ref, o_ref):
        acc = g_ref[0].astype(F32)
        for k in range(1, ns):
            acc = acc + g_ref[k].astype(F32)
        o_ref[...] = acc

    return _pc(body, grid=(R // tr,), in_specs=[pl.BlockSpec((ns, tr, W), lambda i: (0, i, 0))],
               out_specs=pl.BlockSpec((tr, W), lambda i: (i, 0)), out_shape=SDS((R, W), F32), name=name)(gs)


HBM_SPEC = pl.BlockSpec(memory_space=pltpu.HBM)
VMEM_SPEC = pl.BlockSpec(memory_space=pltpu.VMEM)


def _dev():
    return lax.axis_index("x"), lax.axis_index("y"), lax.axis_index("c")


def _allgather_big(loc, name):
    R, W = loc.shape

    def body(x_ref, out_ref, send_sems, recv_sems, local_sem):
        x, y, c = _dev()
        me, sibling = (x, y, c), (x, y, 1 - c)
        chips = [(1 - x, y), (x, 1 - y), (1 - x, 1 - y)]

        def slab(px, py, pc):
            return out_ref.at[4 * px + 2 * py + pc]

        def copy(k, block, to, src=None):
            return pltpu.make_async_remote_copy(
                src_ref=slab(*block) if src is None else src, dst_ref=slab(*block),
                send_sem=send_sems.at[k], recv_sem=recv_sems.at[k], device_id=to, device_id_type=MESH)

        mine = pltpu.make_async_copy(x_ref, slab(*me), local_sem)
        mine.start()
        first = [copy(0, me, sibling, src=x_ref)]
        first += [copy(1 + j, me, (*chip, c), src=x_ref) for j, chip in enumerate(chips)]
        for cp in first:
            cp.start()
        passed = [copy(4 + j, (*chip, c), sibling) for j, chip in enumerate(chips)]
        for j, chip in enumerate(chips):
            copy(1 + j, (*chip, c), me).wait_recv()
            passed[j].start()
        copy(0, sibling, me).wait_recv()
        for j, chip in enumerate(chips):
            copy(4 + j, (*chip, 1 - c), me).wait_recv()
        for cp in first + passed:
            cp.wait_send()
        mine.wait()

    return pl.pallas_call(
        body, out_shape=SDS((N_DEV, R, W), loc.dtype), in_specs=[HBM_SPEC], out_specs=HBM_SPEC,
        scratch_shapes=[pltpu.SemaphoreType.DMA((7,)), pltpu.SemaphoreType.DMA((7,)), pltpu.SemaphoreType.DMA],
        name=name)(loc)


def _dma_sems(n):
    return (pltpu.SemaphoreType.DMA((n,)), pltpu.SemaphoreType.DMA((n,)), pltpu.SemaphoreType.DMA)


def _side_gather_direct(loc):
    R, W = loc.shape

    def copies(ins, outs, sems):
        x, y, c = _dev()
        x_ref, out = ins[0], outs[0]
        me = 4 * x + 2 * y + c
        peers = [(x, y, 1 - c), (1 - x, y, c), (x, 1 - y, c), (1 - x, 1 - y, c)]
        mk = lambda k, p, dst: pltpu.make_async_remote_copy(
            src_ref=x_ref, dst_ref=out.at[dst], send_sem=sems[0].at[k], recv_sem=sems[1].at[k], device_id=p,
            device_id_type=MESH)
        sends = [mk(k, p, me) for k, p in enumerate(peers)]
        recvs = [mk(k, p, 4 * p[0] + 2 * p[1] + p[2]) for k, p in enumerate(peers)]
        return sends, recvs, pltpu.make_async_copy(x_ref, out.at[me], sems[2])

    def start(ins, outs, sems):
        sends, _, mine = copies(ins, outs, sems)
        mine.start()
        for cp in sends:
            cp.start()

    def wait(ins, outs, sems):
        sends, recvs, mine = copies(ins, outs, sems)
        for cp in recvs:
            cp.wait_recv()
        for cp in sends:
            cp.wait_send()
        mine.wait()

    return _Side((loc,), (SDS((N_DEV, R, W), loc.dtype),), _dma_sems(4), start, wait)


def _side_gather_forward(buf):
    def copies(ins, outs, sems):
        x, y, c = _dev()
        out = outs[0]
        chips = [(1 - x, y), (x, 1 - y), (1 - x, 1 - y)]
        mk = lambda k, src, dst: pltpu.make_async_remote_copy(
            src_ref=out.at[src], dst_ref=out.at[dst], send_sem=sems[0].at[k], recv_sem=sems[1].at[k],
            device_id=(x, y, 1 - c), device_id_type=MESH)
        mine = [4 * px + 2 * py + c for px, py in chips]
        theirs = [4 * px + 2 * py + (1 - c) for px, py in chips]
        return [mk(k, s, s) for k, s in enumerate(mine)], [mk(k, s, t) for k, (s, t) in enumerate(zip(mine, theirs))]

    def start(ins, outs, sems):
        for cp in copies(ins, outs, sems)[0]:
            cp.start()

    def wait(ins, outs, sems):
        sends, recvs = copies(ins, outs, sems)
        for cp in recvs:
            cp.wait_recv()
        for cp in sends:
            cp.wait_send()

    return _Side((buf,), (SDS(buf.shape, buf.dtype),), _dma_sems(3)[:2], start, wait, {0: 0})


def _side_chip_exchange(p):
    def copies(ins, outs, sems):
        x, y, c = _dev()
        p_ref, out = ins[0], outs[0]
        j0 = 2 * x + y
        chips = [(1 - x, y), (x, 1 - y), (1 - x, 1 - y)]
        mk = lambda k, chip, src, dst: pltpu.make_async_remote_copy(
            src_ref=p_ref.at[src], dst_ref=out.at[dst], send_sem=sems[0].at[k], recv_sem=sems[1].at[k],
            device_id=(*chip, c), device_id_type=MESH)
        sends = [mk(k, chip, 2 * chip[0] + chip[1], j0) for k, chip in enumerate(chips)]
        recvs = [mk(k, chip, j0, 2 * chip[0] + chip[1]) for k, chip in enumerate(chips)]
        return sends, recvs, pltpu.make_async_copy(p_ref.at[j0], out.at[j0], sems[2])

    def start(ins, outs, sems):
        sends, _, mine = copies(ins, outs, sems)
        mine.start()
        for cp in sends:
            cp.start()

    def wait(ins, outs, sems):
        sends, recvs, mine = copies(ins, outs, sems)
        for cp in recvs:
            cp.wait_recv()
        for cp in sends:
            cp.wait_send()
        mine.wait()

    return _Side((p,), (SDS(p.shape, p.dtype),), _dma_sems(3), start, wait)


def _rs_pair_exchange(g, name):
    nd, R, W = g.shape
    nj = nd // 2

    def body(g_ref, out_ref, send_sems, recv_sems):
        x, y, c = _dev()
        cps = [pltpu.make_async_remote_copy(src_ref=g_ref.at[2 * j + (1 - c)], dst_ref=out_ref.at[j],
                                            send_sem=send_sems.at[j], recv_sem=recv_sems.at[j],
                                            device_id=(x, y, 1 - c), device_id_type=MESH) for j in range(nj)]
        for cp in cps:
            cp.start()
        for cp in cps:
            cp.wait()

    return pl.pallas_call(
        body, out_shape=SDS((nj, R, W), g.dtype), in_specs=[HBM_SPEC], out_specs=HBM_SPEC,
        scratch_shapes=[pltpu.SemaphoreType.DMA((nj,)), pltpu.SemaphoreType.DMA((nj,))], name=name)(g)


def _add_pairs(g, ra, name):
    nd, R, W = g.shape
    nj = nd // 2
    tr = _tile(R, 256, 8)
    cidx = lax.axis_index("c").astype(jnp.int32).reshape(1)

    def body(c_ref, a_ref, b_ref, o_ref):
        o_ref[...] = (a_ref[...].astype(F32) + b_ref[...].astype(F32)).astype(o_ref.dtype)

    gs = pltpu.PrefetchScalarGridSpec(
        num_scalar_prefetch=1, grid=(nj, R // tr),
        in_specs=[pl.BlockSpec((None, tr, W), lambda j, i, cr: (2 * j + cr[0], i, 0)),
                  pl.BlockSpec((None, tr, W), lambda j, i, cr: (j, i, 0))],
        out_specs=pl.BlockSpec((None, tr, W), lambda j, i, cr: (j, i, 0)))
    return pl.pallas_call(body, grid_spec=gs, out_shape=SDS((nj, R, W), g.dtype), name=name,
                          compiler_params=pltpu.CompilerParams(vmem_limit_bytes=VMEM_LIMIT))(cidx, g, ra)


def _rs_chip_exchange(p, name):
    nj, R, W = p.shape

    def body(p_ref, out_ref, send_sems, recv_sems, local_sem):
        x, y, c = _dev()
        j0 = 2 * x + y
        chips = [(1 - x, y), (x, 1 - y), (1 - x, 1 - y)]
        mine = pltpu.make_async_copy(p_ref.at[j0], out_ref.at[j0], local_sem)
        mine.start()

        def copy(k, chip):
            return pltpu.make_async_remote_copy(
                src_ref=p_ref.at[2 * chip[0] + chip[1]], dst_ref=out_ref.at[j0],
                send_sem=send_sems.at[k], recv_sem=recv_sems.at[k], device_id=(*chip, c), device_id_type=MESH)

        sent = [copy(k, chip) for k, chip in enumerate(chips)]
        for cp in sent:
            cp.start()
        for k, chip in enumerate(chips):
            pltpu.make_async_remote_copy(
                src_ref=p_ref.at[j0], dst_ref=out_ref.at[2 * chip[0] + chip[1]],
                send_sem=send_sems.at[k], recv_sem=recv_sems.at[k], device_id=(*chip, c), device_id_type=MESH).wait_recv()
        for cp in sent:
            cp.wait_send()
        mine.wait()

    return pl.pallas_call(
        body, out_shape=SDS((nj, R, W), p.dtype), in_specs=[HBM_SPEC], out_specs=HBM_SPEC,
        scratch_shapes=[pltpu.SemaphoreType.DMA((3,)), pltpu.SemaphoreType.DMA((3,)), pltpu.SemaphoreType.DMA],
        name=name)(p)


def _allgather_small(v, name):
    R, W = v.shape

    def body(v_ref, out_ref, send_sems, recv_sems, local_sem):
        x, y, c = _dev()
        mine = pltpu.make_async_copy(v_ref, out_ref.at[4 * x + 2 * y + c], local_sem)
        mine.start()
        peers = []
        for k in range(1, N_DEV):
            px = 1 - x if k & 4 else x
            py = 1 - y if k & 2 else y
            pc_ = 1 - c if k & 1 else c
            peers.append((px, py, pc_))
        sent = [pltpu.make_async_remote_copy(
            src_ref=v_ref, dst_ref=out_ref.at[4 * x + 2 * y + c], send_sem=send_sems.at[k], recv_sem=recv_sems.at[k],
            device_id=peer, device_id_type=MESH) for k, peer in enumerate(peers)]
        for cp in sent:
            cp.start()
        for k, (px, py, pc_) in enumerate(peers):
            pltpu.make_async_remote_copy(
                src_ref=v_ref, dst_ref=out_ref.at[4 * px + 2 * py + pc_], send_sem=send_sems.at[k],
                recv_sem=recv_sems.at[k], device_id=(px, py, pc_), device_id_type=MESH).wait_recv()
        for cp in sent:
            cp.wait_send()
        mine.wait()

    return pl.pallas_call(
        body, out_shape=SDS((N_DEV, R, W), v.dtype), in_specs=[VMEM_SPEC], out_specs=VMEM_SPEC,
        scratch_shapes=[pltpu.SemaphoreType.DMA((7,)), pltpu.SemaphoreType.DMA((7,)), pltpu.SemaphoreType.DMA],
        name=name)(v)


def _sum_slabs(a, name):
    ns, R, W = a.shape

    def body(a_ref, o_ref):
        acc = a_ref[0]
        for k in range(1, ns):
            acc = acc + a_ref[k]
        o_ref[...] = acc

    return pl.pallas_call(body, out_shape=SDS((R, W), a.dtype), in_specs=[VMEM_SPEC], out_specs=VMEM_SPEC, name=name)(a)


BIG = (("ada_w", "col"), ("w_in", "col"), ("w_ssd_out", "row"), ("w_sc_out", "row"), ("w_o", "row"), ("w_up", "col"),
       ("w_down", "row"))
CONVW = ("ssd_conv_w", "sc_conv_w", "ffn_conv_w")
REPL = ("ada_b", "mix_pre_g", "mix_post_g", "ssd_conv_b", "ssd_dt_bias", "ssd_a_log", "ssd_d", "ssd_norm_g", "ffn_pre_g",
        "ffn_post_g", "ffn_conv_b")
WEIGHTS = ("ada_w", "ada_b", "mix_pre_g", "mix_post_g", "w_in", "ssd_conv_w", "ssd_conv_b", "ssd_dt_bias", "ssd_a_log",
           "ssd_d", "ssd_norm_g", "w_ssd_out", "sc_conv_w", "w_sc_out", "w_o", "ffn_pre_g", "ffn_post_g", "w_up",
           "ffn_conv_w", "ffn_conv_b", "w_down")


def _pad_rows(a, mult):
    r = a.shape[-2]
    pad = -r % mult
    return a if pad == 0 else jnp.pad(a, [(0, 0)] * (a.ndim - 2) + [(0, pad), (0, 0)])


def _flat_rows(parts, mult):
    flat = jnp.concatenate([p.reshape(-1) for p in parts])
    flat = jnp.pad(flat, (0, -flat.shape[0] % ROW_W))
    return _pad_rows(flat.reshape(-1, ROW_W), mult)


def _unflat(buf, shapes):
    flat = buf.reshape(-1)
    out, o = [], 0
    for shp in shapes:
        n = 1
        for s in shp:
            n *= s
        out.append(flat[o:o + n].reshape(shp))
        o += n
    return out


def _pack_big_local(get, l):
    return [_pad_rows((get(n)[l].T if kind == "col" else get(n)[l]).reshape(-1, ROW_W), SLAB_ALIGN) for n, kind in BIG]


def _big_rows(shapes):
    out, o = {}, 0
    for n, _ in BIG:
        r = shapes[n][1] * shapes[n][2] // ROW_W
        out[n] = (o, o + r)
        o += -(-r // SLAB_ALIGN) * SLAB_ALIGN
    return out, o


def kernel(x, c, ada_w, ada_b, mix_pre_g, mix_post_g, w_in, ssd_conv_w, ssd_conv_b, ssd_dt_bias, ssd_a_log, ssd_d, ssd_norm_g, w_ssd_out, sc_conv_w, w_sc_out, w_o, ffn_pre_g, ffn_post_g, w_up, ffn_conv_w, ffn_conv_b, w_down, loss_target, m_ada_w, m_ada_b, m_mix_pre_g, m_mix_post_g, m_w_in, m_ssd_conv_w, m_ssd_conv_b, m_ssd_dt_bias, m_ssd_a_log, m_ssd_d, m_ssd_norm_g, m_w_ssd_out, m_sc_conv_w, m_w_sc_out, m_w_o, m_ffn_pre_g, m_ffn_post_g, m_w_up, m_ffn_conv_w, m_ffn_conv_b, m_w_down, v_ada_w, v_ada_b, v_mix_pre_g, v_mix_post_g, v_w_in, v_ssd_conv_w, v_ssd_conv_b, v_ssd_dt_bias, v_ssd_a_log, v_ssd_d, v_ssd_norm_g, v_w_ssd_out, v_sc_conv_w, v_w_sc_out, v_w_o, v_ffn_pre_g, v_ffn_post_g, v_w_up, v_ffn_conv_w, v_ffn_conv_b, v_w_down):
    wl = dict(zip(WEIGHTS, (ada_w, ada_b, mix_pre_g, mix_post_g, w_in, ssd_conv_w, ssd_conv_b, ssd_dt_bias, ssd_a_log,
                            ssd_d, ssd_norm_g, w_ssd_out, sc_conv_w, w_sc_out, w_o, ffn_pre_g, ffn_post_g, w_up,
                            ffn_conv_w, ffn_conv_b, w_down)))
    ml = dict(zip(WEIGHTS, (m_ada_w, m_ada_b, m_mix_pre_g, m_mix_post_g, m_w_in, m_ssd_conv_w, m_ssd_conv_b,
                            m_ssd_dt_bias, m_ssd_a_log, m_ssd_d, m_ssd_norm_g, m_w_ssd_out, m_sc_conv_w, m_w_sc_out, m_w_o,
                            m_ffn_pre_g, m_ffn_post_g, m_w_up, m_ffn_conv_w, m_ffn_conv_b, m_w_down)))
    vl = dict(zip(WEIGHTS, (v_ada_w, v_ada_b, v_mix_pre_g, v_mix_post_g, v_w_in, v_ssd_conv_w, v_ssd_conv_b,
                            v_ssd_dt_bias, v_ssd_a_log, v_ssd_d, v_ssd_norm_g, v_w_ssd_out, v_sc_conv_w, v_w_sc_out, v_w_o,
                            v_ffn_pre_g, v_ffn_post_g, v_w_up, v_ffn_conv_w, v_ffn_conv_b, v_w_down)))
    depth = ada_w.shape[0]
    shapes = {n: wl[n].shape for n in WEIGHTS}
    me = 4 * lax.axis_index("x") + 2 * lax.axis_index("y") + lax.axis_index("c")

    rows, n_big = _big_rows(shapes)
    conv_flat = jnp.concatenate([wl[n][l].reshape(-1) for l in range(depth) for n in CONVW])
    n_conv = conv_flat.shape[0]
    conv_flat = jnp.pad(conv_flat, (0, -n_conv % (ROW_W // 2)))
    conv_rows = lax.bitcast_convert_type(conv_flat, BF16).reshape(-1, ROW_W)

    def local_rows(l):
        pieces = _pack_big_local(lambda n: wl[n].astype(BF16), l) + ([conv_rows] if l == 0 else [])
        return _pad_rows(jnp.concatenate(pieces, axis=0), ROW_PAD)

    def layer_weights(l, gathered):
        w = {n: wl[n][l] for n in REPL}
        for n, kind in BIG:
            a, b = shapes[n][1], shapes[n][2]
            blk = gathered[:, rows[n][0]:rows[n][1]]
            w[n] = blk.reshape(N_DEV * b, a) if kind == "col" else blk.reshape(N_DEV * a, b)
        for n in CONVW:
            w[n] = conv_full[(l, n)]
        return w

    gathered = _allgather_big(local_rows(0), "allgather_weights")
    conv_all = lax.bitcast_convert_type(
        gathered[:, n_big:n_big + conv_rows.shape[0]].reshape(N_DEV, -1, 2), F32)[:, :n_conv]
    conv_full, o = {}, 0
    for l in range(depth):
        for n in CONVW:
            k, cl = shapes[n][1], shapes[n][2]
            conv_full[(l, n)] = conv_all[:, o:o + k * cl].reshape(N_DEV, k, cl).transpose(1, 0, 2).reshape(k, N_DEV * cl)
            o += k * cl

    nb, S, D = x.shape
    T = nb * S
    act = x.reshape(T, D)
    c8 = jnp.pad(c, ((0, MOD_ROWS - nb), (0, 0)))
    preps, saved = [], []
    for l in range(depth):
        preps.append(_prep_layer(layer_weights(l, gathered)))
        act, s, gathered = _layer_fwd(act, c8, preps[l], S, l, gather=local_rows(l + 1) if l + 1 < depth else None)
        saved.append(s)
    dy, lacc = _loss(act, loss_target.reshape(T, D), S, "loss")
    loss_loc = lacc[0, 0]

    grads, from_chips, pending = [None] * depth, [None] * depth, None
    for l in reversed(range(depth)):
        dy, grads[l], got = _layer_bwd(dy, preps[l], saved[l], S, l, chip_sums=pending)
        if pending is not None:
            from_chips[l + 1] = got
        slabs = [_pad_rows(grads[l][n].astype(BF16).reshape(N_DEV, -1, ROW_W), SLAB_ALIGN) for n, _ in BIG]
        slabs.append(jnp.zeros((N_DEV, -n_big % ROW_PAD, ROW_W), BF16))
        gslab = jnp.concatenate(slabs, axis=1)
        from_sibling = _rs_pair_exchange(gslab, f"rs_pair_exchange_l{l}")
        pending = _add_pairs(gslab, from_sibling, f"rs_pair_add_l{l}")
    from_chips[0] = _rs_chip_exchange(pending, "rs_chip_exchange")
    dx = dy.reshape(nb, S, D)
    g_sums = [_sum_chips(from_chips[l], f"rs_chip_sum_l{l}") for l in range(depth)]

    def grad_of(l, n, kind):
        blk = g_sums[l][rows[n][0]:rows[n][1]]
        a, b = shapes[n][1], shapes[n][2]
        return blk.reshape(b, a).T if kind == "col" else blk.reshape(a, b)

    g_big = {n: jnp.stack([grad_of(l, n, kind) for l in range(depth)]) for n, kind in BIG}
    d_big, m_big, v_big = {}, {}, {}
    for n, _ in BIG:
        d_big[n], m_big[n], v_big[n] = _adam_nat(g_big[n], wl[n], ml[n], vl[n], f"adam_{n}")

    parts = [jnp.broadcast_to(loss_loc, (ROW_W,))]
    small_shapes = [(ROW_W,)]
    for l in range(depth):
        for n in REPL + CONVW:
            parts.append(grads[l][n])
            small_shapes.append(tuple(grads[l][n].shape))
    total = _sum_slabs(_allgather_small(_flat_rows(parts, 8), "allgather_small"), "sum_small")
    pieces = _unflat(total, small_shapes)
    loss = pieces[0][0]
    g_small, i = {}, 1
    for l in range(depth):
        for n in REPL + CONVW:
            gp = pieces[i]
            i += 1
            if n in CONVW:
                gp = lax.dynamic_slice_in_dim(gp, me * shapes[n][2], shapes[n][2], axis=1)
            g_small[(l, n)] = gp
    order = [(l, n) for l in range(depth) for n in REPL + CONVW]
    loc_shapes = [tuple(shapes[n][1:]) for _, n in order]
    packs = lambda f: _flat_rows([f(l, n) for l, n in order], 8)
    gs_small = packs(lambda l, n: g_small[(l, n)])
    _, d_sm, m_sm, v_sm = _adam(gs_small[None], packs(lambda l, n: wl[n][l]), packs(lambda l, n: ml[n][l]),
                                packs(lambda l, n: vl[n][l]), "adam_small")

    def unpack_small(buf):
        ps = _unflat(buf, loc_shapes)
        return {n: jnp.stack([ps[order.index((l, n))] for l in range(depth)]) for n in REPL + CONVW}

    outs = []
    for big, small in ((g_big, {n: jnp.stack([g_small[(l, n)] for l in range(depth)]) for n in REPL + CONVW}),
                       (d_big, unpack_small(d_sm)), (m_big, unpack_small(m_sm)), (v_big, unpack_small(v_sm))):
        merged = {**big, **small}
        outs += [merged[n] for n in WEIGHTS]
    return (loss, dx, *outs)
```

```python
import functools
import math
from typing import Callable, NamedTuple

import jax
import jax.numpy as jnp
from jax import lax
from jax.experimental import pallas as pl
from jax.experimental.pallas import tpu as pltpu

F32, BF16 = jnp.float32, jnp.bfloat16
WGRAD = BF16
SDS = jax.ShapeDtypeStruct
MESH = pl.DeviceIdType.MESH

EPS = 1e-6
N_STATE = 128
CHUNK = 128
SSD_CONV_K, SC_CONV_K, FFN_CONV_K = 4, 3, 3
N_DEV = 8
ROW_W = 1024
ROW_PAD = 32
SLAB_ALIGN = 16
SEG_BLK = 512
STRIP = 32
FFN_STRIP = 64
GLU_W = 256
MOD_ROWS = 128
VMEM_LIMIT = 48 * 2**20

ADAM_LR, ADAM_B1, ADAM_B2, ADAM_EPS, ADAM_WD, ADAM_STEP = 0.001, 0.9, 0.999, 1e-08, 0.01, 10

NT = (((1,), (1,)), ((), ()))
TN = (((0,), (0,)), ((), ()))
NN = (((1,), (0,)), ((), ()))


def _tile(n, cap, mult=128):
    best = None
    for t in range(mult, min(n, cap) + 1, mult):
        if n % t == 0:
            best = t
    return best if best is not None else n


class _Side(NamedTuple):
    operands: tuple
    out_shape: tuple
    scratch: tuple
    start: Callable
    wait: Callable
    aliases: dict = {}


def _pc(body, *, grid, in_specs, out_specs, out_shape, name, scratch=(), side=None):
    params = pltpu.CompilerParams(dimension_semantics=("arbitrary",) * len(grid), vmem_limit_bytes=VMEM_LIMIT)
    if side is None:
        return pl.pallas_call(body, grid=grid, in_specs=in_specs, out_specs=out_specs, out_shape=out_shape,
                              scratch_shapes=list(scratch), name=name, compiler_params=params)
    single = not isinstance(out_shape, (list, tuple))
    outs = [out_shape] if single else list(out_shape)
    ospecs = [out_specs] if single else list(out_specs)
    n_in, n_out, n_scr = len(in_specs), len(outs), len(scratch)
    s_in, s_out = len(side.operands), len(side.out_shape)

    def hosted(*refs):
        ins, refs = refs[:n_in], refs[n_in:]
        sins, refs = refs[:s_in], refs[s_in:]
        mouts, refs = refs[:n_out], refs[n_out:]
        souts, refs = refs[:s_out], refs[s_out:]
        scr, sems = refs[:n_scr], refs[n_scr:]
        first = functools.reduce(lambda a, b: a & b, [pl.program_id(a) == 0 for a in range(len(grid))])
        last = functools.reduce(lambda a, b: a & b, [pl.program_id(a) == grid[a] - 1 for a in range(len(grid))])

        @pl.when(first)
        def _():
            side.start(sins, souts, sems)

        body(*ins, *mouts, *scr)

        @pl.when(last)
        def _():
            side.wait(sins, souts, sems)

    call = pl.pallas_call(
        hosted, grid=grid, in_specs=list(in_specs) + [HBM_SPEC] * s_in, out_specs=ospecs + [HBM_SPEC] * s_out,
        out_shape=outs + list(side.out_shape), scratch_shapes=list(scratch) + list(side.scratch), name=name,
        input_output_aliases={n_in + k: n_out + v for k, v in side.aliases.items()}, compiler_params=params)

    def run(*args):
        res = call(*args, *side.operands)
        main = res[0] if single else list(res[:n_out])
        return main, list(res[n_out:])

    return run


def _silu(x):
    return x * jax.nn.sigmoid(x)


def _dsilu(x):
    s = jax.nn.sigmoid(x)
    return s * (1.0 + x * (1.0 - s))


def _softplus(x):
    return jnp.maximum(x, 0.0) + jnp.log(1.0 + jnp.exp(-jnp.abs(x)))


def _dot(a, b, dims=NN):
    return lax.dot_general(a, b, dims, preferred_element_type=F32)


def _bsum(v, rows=8):
    return jnp.broadcast_to(v, (rows, v.shape[1]))


def _mm(a, b, mode, out_dtype, name, caps=(1024, 1024, 2048), side=None):
    if mode == "nn":
        (M, K), (K2, N) = a.shape, b.shape
    elif mode == "nt":
        (M, K), (N, K2) = a.shape, b.shape
    else:
        (K, M), (K2, N) = a.shape, b.shape
    assert K == K2, (a.shape, b.shape, mode)
    tm, tn, tk = _tile(M, caps[0]), _tile(N, caps[1]), _tile(K, caps[2])
    nk = K // tk
    dims = {"nn": NN, "nt": NT, "tn": TN}[mode]
    if mode == "tn":
        a_spec = pl.BlockSpec((tk, tm), lambda i, j, k: (k, i))
    else:
        a_spec = pl.BlockSpec((tm, tk), lambda i, j, k: (i, k))
    if mode == "nt":
        b_spec = pl.BlockSpec((tn, tk), lambda i, j, k: (j, k))
    else:
        b_spec = pl.BlockSpec((tk, tn), lambda i, j, k: (k, j))

    def body(a_ref, b_ref, o_ref, *acc):
        part = _dot(a_ref[...].astype(BF16), b_ref[...].astype(BF16), dims)
        if nk == 1:
            o_ref[...] = part.astype(o_ref.dtype)
        else:
            acc_ref, = acc
            k = pl.program_id(2)

            @pl.when(k == 0)
            def _():
                acc_ref[...] = part

            @pl.when(k > 0)
            def _():
                acc_ref[...] += part

            @pl.when(k == nk - 1)
            def _():
                o_ref[...] = acc_ref[...].astype(o_ref.dtype)

    return _pc(body, grid=(M // tm, N // tn, nk), in_specs=[a_spec, b_spec],
               out_specs=pl.BlockSpec((tm, tn), lambda i, j, k: (i, j)),
               out_shape=SDS((M, N), out_dtype), name=name,
               scratch=() if nk == 1 else (pltpu.VMEM((tm, tn), F32),), side=side)(a, b)


def _mm_seg(segs, b, mode, out_dtype, name, blk, tile=1024, tk=2048):
    nblk = [a.shape[1] // blk for a in segs]
    assert all(a.shape[1] % blk == 0 for a in segs)
    start = [sum(nblk[:s]) for s in range(len(segs))]
    total = sum(nblk)
    ns = len(segs)
    N = b.shape[1]
    tn = _tile(N, tile)
    if mode == "nn":
        M = segs[0].shape[0]
        tm = _tile(M, tile)
        grid = (M // tm, N // tn, total)
        a_specs = [pl.BlockSpec((tm, blk), lambda i, j, k, k0=k0, n=n: (i, jnp.clip(k - k0, 0, n - 1)))
                   for k0, n in zip(start, nblk)]
        b_spec = pl.BlockSpec((blk, tn), lambda i, j, k: (k, j))
        out_rows, tmo, dims, seg_axis = M, tm, NN, 2
    else:
        K = segs[0].shape[0]
        tkk = _tile(K, tk)
        grid = (total, N // tn, K // tkk)
        a_specs = [pl.BlockSpec((tkk, blk), lambda i, j, k, i0=i0, n=n: (
            jnp.where((i >= i0) & (i < i0 + n), k, 0), jnp.clip(i - i0, 0, n - 1))) for i0, n in zip(start, nblk)]
        b_spec = pl.BlockSpec((tkk, tn), lambda i, j, k: (k, j))
        out_rows, tmo, seg_axis = total * blk, blk, 0
    nk = grid[2]
    acc_shape = (tm, tn) if mode == "nn" else (tn, blk)

    def body(*refs):
        a_refs, b_ref, o_ref, acc_ref = refs[:ns], refs[ns], refs[ns + 1], refs[ns + 2]
        k = pl.program_id(2)
        sel = pl.program_id(seg_axis)

        @pl.when(k == 0)
        def _():
            acc_ref[...] = jnp.zeros_like(acc_ref)

        for s in range(ns):
            @pl.when((sel >= start[s]) & (sel < start[s] + nblk[s]))
            def _(s=s):
                a_, b_ = a_refs[s][...].astype(BF16), b_ref[...].astype(BF16)
                acc_ref[...] += _dot(a_, b_, NN) if mode == "nn" else _dot(b_, a_, TN)

        @pl.when(k == nk - 1)
        def _():
            acc = acc_ref[...]
            o_ref[...] = (acc if mode == "nn" else acc.T).astype(o_ref.dtype)

    return _pc(body, grid=grid, in_specs=a_specs + [b_spec], out_specs=pl.BlockSpec((tmo, tn), lambda i, j, k: (i, j)),
               out_shape=SDS((out_rows, N), out_dtype), name=name, scratch=(pltpu.VMEM(acc_shape, F32),))(*segs, b)


def _modk(c8, ada_w, ada_b, name):
    rows, D = c8.shape
    N = ada_w.shape[0]
    tn = _tile(N, 1536)

    def body(c_ref, w_ref, b_ref, mod_ref, ca_ref):
        ca = _silu(c_ref[...]).astype(BF16)
        mod_ref[...] = _dot(ca, w_ref[...], NT) + b_ref[...]
        ca_ref[...] = ca

    return _pc(body, grid=(N // tn,),
               in_specs=[pl.BlockSpec((rows, D), lambda j: (0, 0)), pl.BlockSpec((tn, D), lambda j: (j, 0)),
                         pl.BlockSpec((1, tn), lambda j: (0, j))],
               out_specs=[pl.BlockSpec((rows, tn), lambda j: (0, j)), pl.BlockSpec((rows, D), lambda j: (0, 0))],
               out_shape=[SDS((rows, N), F32), SDS((rows, D), BF16)], name=name)(c8, ada_w, ada_b)


def _row_tile(S):
    return _tile(S, 512, 8)


def _strips(tm, fn, init=0, rows=None):
    rows = STRIP if rows is None else rows
    assert tm % rows == 0
    return lax.fori_loop(0, tm // rows, lambda r, c: fn(pl.multiple_of(r * rows, rows), c), init)


def _rows8(rows):
    pad = 8 - len(rows)
    return jnp.concatenate(rows + ([jnp.zeros((pad, rows[0].shape[1]), F32)] if pad else []), axis=0)


def _fold8(v):
    return jnp.sum(v.reshape(v.shape[0] // 8, 8, v.shape[1]), axis=0)


def _norm_mod(x, g, mod3, sc_seg, sh_seg, S, name):
    T, D = x.shape
    tm = _row_tile(S)
    tpb = S // tm

    def body(x_ref, g_ref, sc_ref, sh_ref, h_ref):
        x_ = x_ref[...]
        r = lax.rsqrt(jnp.mean(x_ * x_, axis=-1, keepdims=True) + EPS)
        h_ref[...] = ((x_ * r) * (g_ref[...] * (1.0 + sc_ref[...])) + sh_ref[...]).astype(BF16)

    return _pc(body, grid=(T // tm,),
               in_specs=[pl.BlockSpec((tm, D), lambda i: (i, 0)), pl.BlockSpec((1, D), lambda i: (0, 0)),
                         pl.BlockSpec((None, 1, D), lambda i: (i // tpb, 0, sc_seg)),
                         pl.BlockSpec((None, 1, D), lambda i: (i // tpb, 0, sh_seg))],
               out_specs=pl.BlockSpec((tm, D), lambda i: (i, 0)), out_shape=SDS((T, D), BF16), name=name)(x, g, mod3, mod3)


def _resid_post(x, fo, mod3, gt_seg, pg, S, name):
    T, D = x.shape
    tm = _row_tile(S)
    tpb = S // tm

    def body(x_ref, f_ref, gt_ref, pg_ref, o_ref):
        f = f_ref[...]
        r = lax.rsqrt(jnp.mean(f * f, axis=-1, keepdims=True) + EPS)
        o_ref[...] = x_ref[...] + (f * r) * (gt_ref[...] * pg_ref[...])

    return _pc(body, grid=(T // tm,),
               in_specs=[pl.BlockSpec((tm, D), lambda i: (i, 0)), pl.BlockSpec((tm, D), lambda i: (i, 0)),
                         pl.BlockSpec((None, 1, D), lambda i: (i // tpb, 0, gt_seg)),
                         pl.BlockSpec((1, D), lambda i: (0, 0))],
               out_specs=pl.BlockSpec((tm, D), lambda i: (i, 0)), out_shape=SDS((T, D), F32), name=name)(x, fo, mod3, pg)


def _post_bwd(fo, mod3, gt_seg, pg, dout, S, name):
    T, D = fo.shape
    tm = _row_tile(S)
    tpb = S // tm
    nb = T // S

    def body(f_ref, gt_ref, pg_ref, d_ref, df_ref, dgt_ref, dpg_ref):
        i = pl.program_id(0)

        @pl.when(i == 0)
        def _():
            dpg_ref[...] = jnp.zeros_like(dpg_ref)

        @pl.when(i % tpb == 0)
        def _():
            dgt_ref[...] = jnp.zeros_like(dgt_ref)

        f, d = f_ref[...], d_ref[...]
        r = lax.rsqrt(jnp.mean(f * f, axis=-1, keepdims=True) + EPS)
        n = f * r
        dn = d * (gt_ref[...] * pg_ref[...])
        df_ref[...] = (r * (dn - n * jnp.mean(dn * n, axis=-1, keepdims=True))).astype(df_ref.dtype)
        tot = jnp.sum(d * n, axis=0, keepdims=True)
        dgt_ref[...] += _bsum(tot * pg_ref[...])
        dpg_ref[...] += _bsum(tot * gt_ref[...])

    return _pc(body, grid=(T // tm,),
               in_specs=[pl.BlockSpec((tm, D), lambda i: (i, 0)),
                         pl.BlockSpec((None, 1, D), lambda i: (i // tpb, 0, gt_seg)),
                         pl.BlockSpec((1, D), lambda i: (0, 0)), pl.BlockSpec((tm, D), lambda i: (i, 0))],
               out_specs=[pl.BlockSpec((tm, D), lambda i: (i, 0)), pl.BlockSpec((8, D), lambda i: (i // tpb, 0)),
                          pl.BlockSpec((8, D), lambda i: (0, 0))],
               out_shape=[SDS((T, D), BF16), SDS((nb * 8, D), F32), SDS((8, D), F32)], name=name)(fo, mod3, pg, dout)


def _pre_bwd(x, g, mod3, sc_seg, dh, dout, S, name):
    T, D = x.shape
    tm = _row_tile(S)
    tpb = S // tm
    nb = T // S

    def body(x_ref, g_ref, sc_ref, dh_ref, d_ref, dx_ref, dg_ref, dsc_ref, dsh_ref):
        i = pl.program_id(0)

        @pl.when(i == 0)
        def _():
            dg_ref[...] = jnp.zeros_like(dg_ref)

        @pl.when(i % tpb == 0)
        def _():
            dsc_ref[...] = jnp.zeros_like(dsc_ref)
            dsh_ref[...] = jnp.zeros_like(dsh_ref)

        x_, dh_ = x_ref[...], dh_ref[...]
        r = lax.rsqrt(jnp.mean(x_ * x_, axis=-1, keepdims=True) + EPS)
        n = x_ * r
        dn = dh_ * (g_ref[...] * (1.0 + sc_ref[...]))
        dx_ref[...] = d_ref[...] + r * (dn - n * jnp.mean(dn * n, axis=-1, keepdims=True))
        dhn = jnp.sum(dh_ * n, axis=0, keepdims=True)
        dg_ref[...] += _bsum(dhn * (1.0 + sc_ref[...]))
        dsc_ref[...] += _bsum(dhn * g_ref[...])
        dsh_ref[...] += _bsum(jnp.sum(dh_, axis=0, keepdims=True))

    row = pl.BlockSpec((tm, D), lambda i: (i, 0))
    return _pc(body, grid=(T // tm,),
               in_specs=[row, pl.BlockSpec((1, D), lambda i: (0, 0)),
                         pl.BlockSpec((None, 1, D), lambda i: (i // tpb, 0, sc_seg)), row, row],
               out_specs=[row, pl.BlockSpec((8, D), lambda i: (0, 0)), pl.BlockSpec((8, D), lambda i: (i // tpb, 0)),
                          pl.BlockSpec((8, D), lambda i: (i // tpb, 0))],
               out_shape=[SDS((T, D), F32), SDS((8, D), F32), SDS((nb * 8, D), F32), SDS((nb * 8, D), F32)],
               name=name)(x, g, mod3, dh, dout)


def _loss(y, target, S, name):
    T, D = y.shape
    tm = _row_tile(S)

    def body(y_ref, t_ref, dy_ref, l_ref):
        @pl.when(pl.program_id(0) == 0)
        def _():
            l_ref[...] = jnp.zeros_like(l_ref)

        def strip(r0, carry):
            rows = pl.ds(r0, STRIP)
            e = y_ref[rows, :] - t_ref[rows, :]
            dy_ref[rows, :] = e * (1.0 / D)
            return carry + _fold8(e * e)

        acc = _strips(tm, strip, jnp.zeros((8, D), F32))
        l_ref[...] += jnp.broadcast_to(jnp.sum(acc, keepdims=True) * (0.5 / D), l_ref.shape)

    row = pl.BlockSpec((tm, D), lambda i: (i, 0))
    return _pc(body, grid=(T // tm,), in_specs=[row, row],
               out_specs=[row, pl.BlockSpec((8, 128), lambda i: (0, 0))],
               out_shape=[SDS((T, D), F32), SDS((8, 128), F32)], name=name)(y, target)


def _conv_geom(view, C, S):
    arr, off = view
    T = arr.shape[0]
    tm = _row_tile(S)
    tc = _tile(C, 512)
    assert off % tc == 0 and C % tc == 0
    return arr, off // tc, T, tm, tc, S // tm


def _prev_spec(tm, tc, ob, order):
    if order == "ij":
        return pl.BlockSpec((8, tc), lambda i, j: (jnp.maximum(i * (tm // 8) - 1, 0), ob + j))
    return pl.BlockSpec((8, tc), lambda j, i: (jnp.maximum(i * (tm // 8) - 1, 0), ob + j))


def _next_spec(T, tm, tc, ob, order):
    last = T // 8 - 1
    if order == "ij":
        return pl.BlockSpec((8, tc), lambda i, j: (jnp.minimum((i + 1) * (tm // 8), last), ob + j))
    return pl.BlockSpec((8, tc), lambda j, i: (jnp.minimum((i + 1) * (tm // 8), last), ob + j))


def _taps(win, w_ref, K, lead, rows):
    acc = win[lead:lead + rows] * w_ref[K - 1:K, :]
    for j in range(1, K):
        acc = acc + win[lead - j:lead - j + rows] * w_ref[K - 1 - j:K - j, :]
    return acc


def _taps_t(win, w_ref, K, rows):
    acc = win[0:rows] * w_ref[K - 1:K, :]
    for j in range(1, K):
        acc = acc + win[j:j + rows] * w_ref[K - 1 - j:K - j, :]
    return acc


def _conv_fwd(view, C, w8, b, K, S, name):
    arr, ob, T, tm, tc, tps = _conv_geom(view, C, S)

    def body(u_ref, p_ref, w_ref, b_ref, o_ref, buf):
        first = (pl.program_id(0) % tps) == 0
        buf[0:8, :] = jnp.where(first, 0.0, p_ref[...])
        buf[8:, :] = u_ref[...]

        def strip(r0, carry):
            win = buf[pl.ds(r0, STRIP + 8), :]
            o_ref[pl.ds(r0, STRIP), :] = _taps(win, w_ref, K, 8, STRIP) + b_ref[...]
            return carry

        _strips(tm, strip)

    return _pc(body, grid=(T // tm, C // tc),
               in_specs=[pl.BlockSpec((tm, tc), lambda i, j: (i, ob + j)), _prev_spec(tm, tc, ob, "ij"),
                         pl.BlockSpec((8, tc), lambda i, j: (0, j)), pl.BlockSpec((1, tc), lambda i, j: (0, j))],
               out_specs=pl.BlockSpec((tm, tc), lambda i, j: (i, j)), out_shape=SDS((T, C), F32), name=name,
               scratch=(pltpu.VMEM((tm + 8, tc), F32),))(arr, arr, w8, b)


def _conv_bwd_in(dview, C, w8, K, S, out_dtype, name):
    arr, ob, T, tm, tc, tps = _conv_geom(dview, C, S)

    def body(d_ref, n_ref, w_ref, o_ref, buf):
        last = (pl.program_id(0) % tps) == tps - 1
        buf[0:tm, :] = d_ref[...]
        buf[tm:tm + 8, :] = jnp.where(last, 0.0, n_ref[...])

        def strip(r0, carry):
            win = buf[pl.ds(r0, STRIP + 8), :]
            o_ref[pl.ds(r0, STRIP), :] = _taps_t(win, w_ref, K, STRIP).astype(o_ref.dtype)
            return carry

        _strips(tm, strip)

    return _pc(body, grid=(T // tm, C // tc),
               in_specs=[pl.BlockSpec((tm, tc), lambda i, j: (i, ob + j)), _next_spec(T, tm, tc, ob, "ij"),
                         pl.BlockSpec((8, tc), lambda i, j: (0, j))],
               out_specs=pl.BlockSpec((tm, tc), lambda i, j: (i, j)), out_shape=SDS((T, C), out_dtype), name=name,
               scratch=(pltpu.VMEM((tm + 8, tc), F32),))(arr, arr, w8)


def _conv_bwd_w(dview, uview, C, K, S, name):
    darr, dob, T, tm, tc, tps = _conv_geom(dview, C, S)
    uarr, uob, _, _, _, _ = _conv_geom(uview, C, S)

    def body(d_ref, u_ref, p_ref, o_ref, buf):
        i = pl.program_id(1)

        @pl.when(i == 0)
        def _():
            o_ref[...] = jnp.zeros_like(o_ref)

        first = (i % tps) == 0
        buf[0:8, :] = jnp.where(first, 0.0, p_ref[...])
        buf[8:, :] = u_ref[...]

        def strip(r0, carry):
            win = buf[pl.ds(r0, STRIP + 8), :]
            d = d_ref[pl.ds(r0, STRIP), :]
            sums = [_fold8(d * win[8 - (K - 1 - k):8 - (K - 1 - k) + STRIP]) for k in range(K)] + [_fold8(d)]
            return tuple(c + s for c, s in zip(carry, sums))

        acc = _strips(tm, strip, tuple(jnp.zeros((8, tc), F32) for _ in range(K + 1)))
        o_ref[...] += _rows8([jnp.sum(a, axis=0, keepdims=True) for a in acc])

    return _pc(body, grid=(C // tc, T // tm),
               in_specs=[pl.BlockSpec((tm, tc), lambda j, i: (i, dob + j)),
                         pl.BlockSpec((tm, tc), lambda j, i: (i, uob + j)), _prev_spec(tm, tc, uob, "ji")],
               out_specs=pl.BlockSpec((8, tc), lambda j, i: (0, j)), out_shape=SDS((8, C), F32), name=name,
               scratch=(pltpu.VMEM((tm + 8, tc), F32),))(darr, uarr, uarr)


def _ffn_act_fwd(uu, w8, b, S, name):
    K, gw = FFN_CONV_K, GLU_W
    T, F2 = uu.shape
    tm, tc = _row_tile(S), 2 * GLU_W
    tps = S // tm

    def body(u_ref, p_ref, w_ref, b_ref, a_ref, buf):
        first = (pl.program_id(0) % tps) == 0
        buf[0:8, :] = jnp.where(first, 0.0, p_ref[...])
        buf[8:, :] = u_ref[...]

        def strip(r0, carry):
            u = _taps(buf[pl.ds(r0, STRIP + 8), :], w_ref, K, 8, STRIP) + b_ref[...]
            a_ref[pl.ds(r0, STRIP), :] = (_silu(u[:, :gw]) * u[:, gw:]).astype(BF16)
            return carry

        _strips(tm, strip)

    return _pc(body, grid=(T // tm, F2 // tc),
               in_specs=[pl.BlockSpec((tm, tc), lambda i, j: (i, j)), _prev_spec(tm, tc, 0, "ij"),
                         pl.BlockSpec((8, tc), lambda i, j: (0, j)), pl.BlockSpec((1, tc), lambda i, j: (0, j))],
               out_specs=pl.BlockSpec((tm, gw), lambda i, j: (i, j)), out_shape=SDS((T, F2 // 2), BF16), name=name,
               scratch=(pltpu.VMEM((tm + 8, tc), F32),))(uu, uu, w8, b)


def _ffn_act_bwd(uu, da, w8, b, S, name, side=None):
    K, gw = FFN_CONV_K, GLU_W
    T, F2 = uu.shape
    tm, tc = _row_tile(S), 2 * GLU_W
    tps = S // tm
    last16 = T // 16 - 1

    def body(u_ref, p_ref, n_ref, da_ref, dan_ref, w_ref, b_ref, duu_ref, cw_ref, ubuf, dabuf):
        i = pl.program_id(1)

        @pl.when(i == 0)
        def _():
            cw_ref[...] = jnp.zeros_like(cw_ref)

        first = (i % tps) == 0
        last = (i % tps) == tps - 1
        ubuf[0:8, :] = jnp.where(first, 0.0, p_ref[...])
        ubuf[8:tm + 8, :] = u_ref[...]
        ubuf[tm + 8:tm + 16, :] = n_ref[...]
        dabuf[0:tm, :] = da_ref[...].astype(F32)
        dabuf[tm:tm + 8, :] = jnp.where(last, 0.0, dan_ref[...].astype(F32)[0:8, :])

        def strip(r0, carry):
            ext = FFN_STRIP + 8
            win = ubuf[pl.ds(r0, FFN_STRIP + 16), :]
            shifted = [win[8 - j:8 - j + ext] for j in range(K)]
            u = b_ref[...] + shifted[0] * w_ref[K - 1:K, :]
            for j in range(1, K):
                u = u + shifted[j] * w_ref[K - 1 - j:K - j, :]
            da_ = dabuf[pl.ds(r0, ext), :]
            g, v = u[:, :gw], u[:, gw:]
            du = jnp.concatenate([da_ * v * _dsilu(g), da_ * _silu(g)], axis=1)
            duu_ref[pl.ds(r0, FFN_STRIP), :] = _taps_t(du, w_ref, K, FFN_STRIP).astype(BF16)
            dmain = du[0:FFN_STRIP]
            sums = [_fold8(dmain * shifted[K - 1 - k][0:FFN_STRIP]) for k in range(K)] + [_fold8(dmain)]
            return tuple(c + s for c, s in zip(carry, sums))

        acc = _strips(tm, strip, tuple(jnp.zeros((8, tc), F32) for _ in range(K + 1)), rows=FFN_STRIP)
        cw_ref[...] += _rows8([jnp.sum(a, axis=0, keepdims=True) for a in acc])

    return _pc(body, grid=(F2 // tc, T // tm),
               in_specs=[pl.BlockSpec((tm, tc), lambda j, i: (i, j)), _prev_spec(tm, tc, 0, "ji"),
                         _next_spec(T, tm, tc, 0, "ji"), pl.BlockSpec((tm, gw), lambda j, i: (i, j)),
                         pl.BlockSpec((16, gw), lambda j, i: (jnp.minimum((i + 1) * (tm // 16), last16), j)),
                         pl.BlockSpec((8, tc), lambda j, i: (0, j)), pl.BlockSpec((1, tc), lambda j, i: (0, j))],
               out_specs=[pl.BlockSpec((tm, tc), lambda j, i: (i, j)), pl.BlockSpec((8, tc), lambda j, i: (0, j))],
               out_shape=[SDS((T, F2), BF16), SDS((8, F2), F32)], name=name,
               scratch=(pltpu.VMEM((tm + 16, tc), F32), pltpu.VMEM((tm + 8, gw), F32)), side=side)(
                   uu, uu, uu, da, da, w8, b)


def _ssd_common(dtc_raw, dtr_raw, hpc, hpr, L):
    dt_c = _softplus(dtc_raw + hpc[0:1, :])
    a_c = -jnp.exp(hpc[1:2, :])
    dt_r = _softplus(dtr_raw + hpr[:, 0:1])
    a_r = -jnp.exp(hpr[:, 1:2])
    li = lax.broadcasted_iota(jnp.int32, (L, L), 0)
    si = lax.broadcasted_iota(jnp.int32, (L, L), 1)
    low = li >= si
    upp = li <= si
    acs_c = _dotx(low, dt_c * a_c, split="b")
    acs_r = _dotx(dt_r * a_r, upp)
    return dt_c, a_c, acs_c, acs_r, low, upp


def _dotx(a, b, split="a", parts=3, dims=NN):
    val, one = (a, b) if split == "a" else (b, a)
    one = one.astype(BF16)
    acc, rem = None, val
    for i in range(parts):
        piece = rem.astype(BF16)
        t = _dot(piece, one, dims) if split == "a" else _dot(one, piece, dims)
        acc = t if acc is None else acc + t
        if i + 1 < parts:
            rem = rem - piece.astype(F32)
    return acc


def _head_maps(R, P, L):
    RP = R * P
    sel = (lax.broadcasted_iota(jnp.int32, (RP, R), 0) // P == lax.broadcasted_iota(jnp.int32, (RP, R), 1)).astype(F32)
    selt = (lax.broadcasted_iota(jnp.int32, (R, RP), 1) // P == lax.broadcasted_iota(jnp.int32, (R, RP), 0)).astype(F32)
    colb = (lax.broadcasted_iota(jnp.int32, (R, R * L), 1) // L == lax.broadcasted_iota(jnp.int32, (R, R * L), 0)).astype(F32)
    return sel, selt, colb


def _pair_diag(mats, rhs_b, R, P):
    lanes = 2 * P
    lo = lax.broadcasted_iota(jnp.int32, (mats[0].shape[0], lanes), 1) < P
    out = []
    for q in range(R // 2):
        rp = rhs_b[:, q * lanes:(q + 1) * lanes]
        out.append(jnp.where(lo, _dot(mats[2 * q], rp), _dot(mats[2 * q + 1], rp)))
    return jnp.concatenate(out, axis=1) if len(out) > 1 else out[0]


def _ssd_specs(pre, off_x, off_b, off_c, G, R, P, nb, nc, rev):
    L, N, RP = CHUNK, N_STATE, R * P
    cidx = (lambda c: nc - 1 - c) if rev else (lambda c: c)
    xb, bb, cb = off_x // RP, off_b // N, off_c // N
    assert off_x % RP == 0 and off_b % N == 0 and off_c % N == 0
    row = lambda b, c: b * nc + cidx(c)
    return dict(
        x=pl.BlockSpec((L, RP), lambda g, b, c: (row(b, c), xb + g)),
        b=pl.BlockSpec((L, N), lambda g, b, c: (row(b, c), bb + g)),
        c=pl.BlockSpec((L, N), lambda g, b, c: (row(b, c), cb + g)),
        dtc=pl.BlockSpec((None, L, R), lambda g, b, c: (g, row(b, c), 0)),
        dtr=pl.BlockSpec((None, R, L), lambda g, b, c: (g, 0, row(b, c))),
        hpc=pl.BlockSpec((None, 8, R), lambda g, b, c: (g, 0, 0)),
        hpr=pl.BlockSpec((None, R, 8), lambda g, b, c: (g, 0, 0)),
        y=pl.BlockSpec((L, RP), lambda g, b, c: (row(b, c), g)),
        bc=pl.BlockSpec((L, N), lambda g, b, c: (row(b, c), g)),
        hs=pl.BlockSpec((None, None, N, RP), lambda g, b, c: (row(b, c), g, 0, 0)),
    )


def _ssd_fwd(pre, offs, dtc, dtr, hpc, hpr, G, R, P, S, name, side=None):
    T = pre.shape[0]
    L, N, RP = CHUNK, N_STATE, R * P
    nc, nb = S // L, T // S
    sp = _ssd_specs(pre, *offs, G, R, P, nb, nc, False)

    def body(px_ref, pb_ref, pc_ref, dtc_ref, dtr_ref, hpc_ref, hpr_ref, y_ref, hs_ref, hst):
        @pl.when(pl.program_id(2) == 0)
        def _():
            hst[...] = jnp.zeros_like(hst)

        xs, bm, cm = _silu(px_ref[...]), _silu(pb_ref[...]), _silu(pc_ref[...])
        hpc_ = hpc_ref[...]
        dt_c, _, acs_c, acs_r, low, _ = _ssd_common(dtc_ref[...], dtr_ref[...], hpc_, hpr_ref[...], L)
        _, selt, colb = _head_maps(R, P, L)
        dt_e, a_e, hp_e = _dotx(dt_c, selt), _dotx(acs_c, selt), _dotx(hpc_, selt)
        a_bc = _dotx(acs_c, colb)
        a_last = a_e[L - 1:L, :]
        bb, cb = bm.astype(BF16), cm.astype(BF16)
        gm = _dot(cb, bb, NT)
        hprev = hst[...]
        hprev_b = hprev.astype(BF16)
        hs_ref[...] = hprev_b
        xdt = xs * dt_e
        xdt_b = xdt.astype(BF16)
        ms = []
        for r in range(R):
            dec = jnp.exp(jnp.where(low, a_bc[:, r * L:(r + 1) * L] - acs_r[r:r + 1, :], -jnp.inf))
            ms.append((gm * dec).astype(BF16))
        y = _pair_diag(ms, xdt_b, R, P) + _dot(cb, hprev_b) * jnp.exp(a_e) + hp_e[2:3, :] * xs
        y_ref[...] = y
        xw = (xdt * jnp.exp(a_last - a_e)).astype(BF16)
        hst[...] = hprev * jnp.exp(a_last) + _dot(bb, xw, TN)

    return _pc(body, grid=(G, nb, nc),
               in_specs=[sp["x"], sp["b"], sp["c"], sp["dtc"], sp["dtr"], sp["hpc"], sp["hpr"]],
               out_specs=[sp["y"], sp["hs"]],
               out_shape=[SDS((T, G * RP), F32), SDS((nb * nc, G, N, RP), BF16)], name=name,
               scratch=(pltpu.VMEM((N, RP), F32),), side=side)(pre, pre, pre, dtc, dtr, hpc, hpr)


def _ssd_bwd(pre, offs, dtc, dtr, hpc, hpr, hs, dy, G, R, P, S, name, side=None):
    T = pre.shape[0]
    L, N, RP = CHUNK, N_STATE, R * P
    nc, nb = S // L, T // S
    sp = _ssd_specs(pre, *offs, G, R, P, nb, nc, True)

    def body(px_ref, pb_ref, pc_ref, dtc_ref, dtr_ref, hpc_ref, hpr_ref, hs_ref, dy_ref,
             dpx_ref, dpb_ref, dpc_ref, ddt_ref, hpg_ref, dhst):
        bi, ci = pl.program_id(1), pl.program_id(2)

        @pl.when(ci == 0)
        def _():
            dhst[...] = jnp.zeros_like(dhst)

        @pl.when((bi == 0) & (ci == 0))
        def _():
            hpg_ref[...] = jnp.zeros_like(hpg_ref)

        px, pb, pcc = px_ref[...], pb_ref[...], pc_ref[...]
        xs, bm, cm = _silu(px), _silu(pb), _silu(pcc)
        hpc_ = hpc_ref[...]
        dtc_raw = dtc_ref[...]
        dt_c, a_c, acs_c, acs_r, low, upp = _ssd_common(dtc_raw, dtr_ref[...], hpc_, hpr_ref[...], L)
        sel, selt, colb = _head_maps(R, P, L)
        dt_e, a_e, hp_e = _dotx(dt_c, selt), _dotx(acs_c, selt), _dotx(hpc_, selt)
        a_bc = _dotx(acs_c, colb)
        a_last = a_e[L - 1:L, :]
        e_e, w_e = jnp.exp(a_e), jnp.exp(a_last - a_e)
        bb, cb = bm.astype(BF16), cm.astype(BF16)
        gm = _dot(cb, bb, NT)
        gmt = _dot(bb, cb, NT)
        hprev = hs_ref[...]
        dhn = dhst[...]
        dhn_b = dhn.astype(BF16)
        dy = dy_ref[...]
        dy_b = dy.astype(BF16)
        xdt = xs * dt_e
        xdt_b = xdt.astype(BF16)
        yoff = _dot(cb, hprev) * e_e
        dye_b = (dy * e_e).astype(BF16)
        dcm = _dot(dye_b, hprev, NT)
        dhst[...] = _dot(cb, dye_b, TN) + jnp.exp(a_last) * dhn
        dxdt_st = _dot(bb, dhn_b) * w_e
        dbm = _dot((xdt * w_e).astype(BF16), dhn_b, NT)
        lanes = 2 * P
        lo = lax.broadcasted_iota(jnp.int32, (L, lanes), 1) < P
        dg = jnp.zeros((L, L), F32)
        es, css = [], []
        for r in range(R):
            col_b, row = a_bc[:, r * L:(r + 1) * L], acs_r[r:r + 1, :]
            dec = jnp.exp(jnp.where(low, col_b - row, -jnp.inf))
            q = r // 2
            dyp = dy_b[:, q * lanes:(q + 1) * lanes]
            dyp = jnp.where(lo if r % 2 == 0 else ~lo, dyp, jnp.zeros_like(dyp))
            dm = _dot(dyp, xdt_b[:, q * lanes:(q + 1) * lanes], NT)
            dg = dg + dm * dec
            e = dm * (gm * dec)
            es.append(e)
            css.append(jnp.sum(e, axis=0, keepdims=True))
        dgb = dg.astype(BF16)
        dcm = dcm + _dot(dgb, bb)
        dbm = dbm + _dot(dgb, cb, TN)
        colbt = (lax.broadcasted_iota(jnp.int32, (R * L, R), 0) // L
                 == lax.broadcasted_iota(jnp.int32, (R * L, R), 1)).astype(F32)
        eye = (lax.broadcasted_iota(jnp.int32, (R, R), 0) == lax.broadcasted_iota(jnp.int32, (R, R), 1)).astype(F32)
        row_sums = _dotx(jnp.concatenate(es, axis=1), colbt)
        col_sums = _dotx(jnp.concatenate(css, axis=0), eye, dims=TN)
        mts = []
        for r in range(R):
            dect = jnp.exp(jnp.where(upp, acs_r[r:r + 1, :] - a_bc[:, r * L:(r + 1) * L], -jnp.inf))
            mts.append((gmt * dect).astype(BF16))
        dxdt = _pair_diag(mts, dy_b, R, P) + dxdt_st
        q_st = _dotx(xdt * dxdt_st, sel, parts=1)
        da = row_sums - col_sums + _dotx(dy * yoff, sel, parts=1) - q_st
        hh = jnp.sum(_dotx(dhn * hprev.astype(F32), sel, parts=1), axis=0, keepdims=True)
        da_last = jnp.exp(acs_c[L - 1:L, :]) * hh + jnp.sum(q_st, axis=0, keepdims=True)
        rowi = lax.broadcasted_iota(jnp.int32, (L, R), 0)
        da = da + jnp.where(rowi == L - 1, da_last, 0.0)
        dpx_ref[...] = (dxdt * dt_e + hp_e[2:3, :] * dy) * _dsilu(px)
        dpb_ref[...] = dbm * _dsilu(pb)
        dpc_ref[...] = dcm * _dsilu(pcc)
        dadt = _dotx(upp, da, split="b")
        ddt = _dotx(dxdt * xs, sel, parts=1) + dadt * a_c
        ddt_raw = ddt * jax.nn.sigmoid(dtc_raw + hpc_[0:1, :])
        ddt_ref[...] = ddt_raw
        d_a = jnp.sum(dadt * dt_c, axis=0, keepdims=True)
        d_d = jnp.sum(_dotx(dy * xs, sel, parts=1), axis=0, keepdims=True)
        rows = [jnp.sum(ddt_raw, axis=0, keepdims=True), d_a * a_c, d_d, jnp.zeros((5, R), F32)]
        hpg_ref[...] += jnp.concatenate(rows, axis=0)

    return _pc(body, grid=(G, nb, nc),
               in_specs=[sp["x"], sp["b"], sp["c"], sp["dtc"], sp["dtr"], sp["hpc"], sp["hpr"], sp["hs"], sp["y"]],
               out_specs=[sp["y"], sp["bc"], sp["bc"], sp["dtc"], pl.BlockSpec((None, 8, R), lambda g, b, c: (g, 0, 0))],
               out_shape=[SDS((T, G * RP), F32), SDS((T, G * N), F32), SDS((T, G * N), F32), SDS((G, T, R), F32),
                          SDS((G, 8, R), F32)], name=name,
               scratch=(pltpu.VMEM((N, RP), F32),), side=side)(pre, pre, pre, dtc, dtr, hpc, hpr, hs, dy)


def _gate_norm_fwd(y, zview, ng, G, S, name):
    T, DI = y.shape
    zarr, zoff = zview
    gw = DI // G
    tm = _row_tile(S)
    zb = zoff // gw
    assert zoff % gw == 0

    def body(y_ref, z_ref, g_ref, o_ref):
        yg = y_ref[...] * _silu(z_ref[...])
        r = lax.rsqrt(jnp.mean(yg * yg, axis=-1, keepdims=True) + EPS)
        o_ref[...] = (yg * r * g_ref[...]).astype(BF16)

    return _pc(body, grid=(T // tm, G),
               in_specs=[pl.BlockSpec((tm, gw), lambda i, g: (i, g)), pl.BlockSpec((tm, gw), lambda i, g: (i, zb + g)),
                         pl.BlockSpec((1, gw), lambda i, g: (0, g))],
               out_specs=pl.BlockSpec((tm, gw), lambda i, g: (i, g)), out_shape=SDS((T, DI), BF16), name=name)(y, zarr, ng)


def _gate_norm_bwd(y, zview, ng, dyn, G, S, name):
    T, DI = y.shape
    zarr, zoff = zview
    gw = DI // G
    tm = _row_tile(S)
    zb = zoff // gw

    def body(y_ref, z_ref, g_ref, d_ref, dy_ref, dz_ref, dg_ref):
        @pl.when(pl.program_id(1) == 0)
        def _():
            dg_ref[...] = jnp.zeros_like(dg_ref)

        y_, z, d = y_ref[...], z_ref[...], d_ref[...]
        sz = _silu(z)
        yg = y_ * sz
        r = lax.rsqrt(jnp.mean(yg * yg, axis=-1, keepdims=True) + EPS)
        n = yg * r
        dn = d * g_ref[...]
        dyg = r * (dn - n * jnp.mean(dn * n, axis=-1, keepdims=True))
        dy_ref[...] = dyg * sz
        dz_ref[...] = (dyg * y_ * _dsilu(z)).astype(BF16)
        dg_ref[...] += _bsum(jnp.sum(d * n, axis=0, keepdims=True))

    return _pc(body, grid=(G, T // tm),
               in_specs=[pl.BlockSpec((tm, gw), lambda g, i: (i, g)), pl.BlockSpec((tm, gw), lambda g, i: (i, zb + g)),
                         pl.BlockSpec((1, gw), lambda g, i: (0, g)), pl.BlockSpec((tm, gw), lambda g, i: (i, g))],
               out_specs=[pl.BlockSpec((tm, gw), lambda g, i: (i, g)), pl.BlockSpec((tm, gw), lambda g, i: (i, g)),
                          pl.BlockSpec((8, gw), lambda g, i: (0, g))],
               out_shape=[SDS((T, DI), F32), SDS((T, DI), BF16), SDS((8, DI), F32)], name=name)(y, zarr, ng, dyn)


def _shortconv_fwd(proj, off_b, off_c, off_h, C, w8, S, name):
    K = SC_CONV_K
    _, ob, T, tm, tc, tps = _conv_geom((proj, off_b), C, S)
    oc, oh = off_c // tc, off_h // tc

    def body(b_ref, c_ref, h_ref, cp_ref, hp_ref, w_ref, o_ref, buf):
        first = (pl.program_id(0) % tps) == 0
        buf[0:8, :] = jnp.where(first, 0.0, cp_ref[...] * hp_ref[...])
        buf[8:, :] = c_ref[...] * h_ref[...]

        def strip(r0, carry):
            conv = _taps(buf[pl.ds(r0, STRIP + 8), :], w_ref, K, 8, STRIP)
            o_ref[pl.ds(r0, STRIP), :] = (b_ref[pl.ds(r0, STRIP), :] * conv).astype(BF16)
            return carry

        _strips(tm, strip)

    blk = lambda o: pl.BlockSpec((tm, tc), lambda i, j: (i, o + j))
    return _pc(body, grid=(T // tm, C // tc),
               in_specs=[blk(ob), blk(oc), blk(oh), _prev_spec(tm, tc, oc, "ij"), _prev_spec(tm, tc, oh, "ij"),
                         pl.BlockSpec((8, tc), lambda i, j: (0, j))],
               out_specs=pl.BlockSpec((tm, tc), lambda i, j: (i, j)), out_shape=SDS((T, C), BF16), name=name,
               scratch=(pltpu.VMEM((tm + 8, tc), F32),))(proj, proj, proj, proj, proj, w8)


def _shortconv_bwd(proj, off_b, off_c, off_h, C, w8, dsc, S, name):
    K = SC_CONV_K
    _, ob, T, tm, tc, tps = _conv_geom((proj, off_b), C, S)
    oc, oh = off_c // tc, off_h // tc

    def body(b_ref, c_ref, h_ref, cp_ref, hp_ref, bn_ref, d_ref, dn_ref, w_ref,
             db_ref, dc_ref, dh_ref, dw_ref, buf, buf2):
        i = pl.program_id(1)

        @pl.when(i == 0)
        def _():
            dw_ref[...] = jnp.zeros_like(dw_ref)

        first = (i % tps) == 0
        last = (i % tps) == tps - 1
        buf[0:8, :] = jnp.where(first, 0.0, cp_ref[...] * hp_ref[...])
        buf[8:, :] = c_ref[...] * h_ref[...]
        buf2[0:tm, :] = d_ref[...] * b_ref[...]
        buf2[tm:tm + 8, :] = jnp.where(last, 0.0, dn_ref[...] * bn_ref[...])

        def strip(r0, carry):
            rows = pl.ds(r0, STRIP)
            vwin = buf[pl.ds(r0, STRIP + 8), :]
            vs = [vwin[8 - j:8 - j + STRIP] for j in range(K)]
            conv = vs[0] * w_ref[K - 1:K, :]
            for j in range(1, K):
                conv = conv + vs[j] * w_ref[K - 1 - j:K - j, :]
            db_ref[rows, :] = (d_ref[rows, :] * conv).astype(BF16)
            dwin = buf2[pl.ds(r0, STRIP + 8), :]
            dv = _taps_t(dwin, w_ref, K, STRIP)
            dc_ref[rows, :] = (dv * h_ref[rows, :]).astype(BF16)
            dh_ref[rows, :] = (dv * c_ref[rows, :]).astype(BF16)
            dconv = dwin[0:STRIP]
            sums = [_fold8(dconv * vs[K - 1 - k]) for k in range(K)]
            return tuple(c + s for c, s in zip(carry, sums))

        acc = _strips(tm, strip, tuple(jnp.zeros((8, tc), F32) for _ in range(K)))
        dw_ref[...] += _rows8([jnp.sum(a, axis=0, keepdims=True) for a in acc])

    blk = lambda o: pl.BlockSpec((tm, tc), lambda j, i: (i, o + j))
    out = pl.BlockSpec((tm, tc), lambda j, i: (i, j))
    return _pc(body, grid=(C // tc, T // tm),
               in_specs=[blk(ob), blk(oc), blk(oh), _prev_spec(tm, tc, oc, "ji"), _prev_spec(tm, tc, oh, "ji"),
                         _next_spec(T, tm, tc, ob, "ji"), blk(0), _next_spec(T, tm, tc, 0, "ji"),
                         pl.BlockSpec((8, tc), lambda j, i: (0, j))],
               out_specs=[out, out, out, pl.BlockSpec((8, tc), lambda j, i: (0, j))],
               out_shape=[SDS((T, C), BF16)] * 3 + [SDS((8, C), F32)], name=name,
               scratch=(pltpu.VMEM((tm + 8, tc), F32), pltpu.VMEM((tm + 8, tc), F32)))(
                   proj, proj, proj, proj, proj, proj, dsc, dsc, w8)


def _merge_fwd(proj, off_g1, off_g2, y1, y2, S, name):
    T, D = y1.shape
    tm = _row_tile(S)
    o1, o2 = off_g1 // D, off_g2 // D
    assert off_g1 % D == 0 and off_g2 % D == 0

    def body(g1_ref, g2_ref, y1_ref, y2_ref, o_ref):
        def strip(r0, carry):
            rows = pl.ds(r0, STRIP)
            o_ref[rows, :] = (jax.nn.sigmoid(g1_ref[rows, :]) * y1_ref[rows, :]
                              + jax.nn.sigmoid(g2_ref[rows, :]) * y2_ref[rows, :]).astype(BF16)
            return carry

        _strips(tm, strip)

    row = pl.BlockSpec((tm, D), lambda i: (i, 0))
    return _pc(body, grid=(T // tm,),
               in_specs=[pl.BlockSpec((tm, D), lambda i: (i, o1)), pl.BlockSpec((tm, D), lambda i: (i, o2)), row, row],
               out_specs=row, out_shape=SDS((T, D), BF16), name=name)(proj, proj, y1, y2)


def _merge_bwd(proj, off_g1, off_g2, y1, y2, dm, S, name):
    T, D = y1.shape
    tm = _row_tile(S)
    o1, o2 = off_g1 // D, off_g2 // D

    def body(g1_ref, g2_ref, y1_ref, y2_ref, d_ref, dy1_ref, dy2_ref, dg1_ref, dg2_ref):
        def strip(r0, carry):
            rows = pl.ds(r0, STRIP)
            d = d_ref[rows, :]
            s1, s2 = jax.nn.sigmoid(g1_ref[rows, :]), jax.nn.sigmoid(g2_ref[rows, :])
            dy1_ref[rows, :] = (d * s1).astype(BF16)
            dy2_ref[rows, :] = (d * s2).astype(BF16)
            dg1_ref[rows, :] = (d * y1_ref[rows, :] * s1 * (1.0 - s1)).astype(BF16)
            dg2_ref[rows, :] = (d * y2_ref[rows, :] * s2 * (1.0 - s2)).astype(BF16)
            return carry

        _strips(tm, strip)

    row = pl.BlockSpec((tm, D), lambda i: (i, 0))
    return _pc(body, grid=(T // tm,),
               in_specs=[pl.BlockSpec((tm, D), lambda i: (i, o1)), pl.BlockSpec((tm, D), lambda i: (i, o2)), row, row, row],
               out_specs=[row] * 4, out_shape=[SDS((T, D), BF16)] * 4, name=name)(proj, proj, y1, y2, dm)


def _pad8(w):
    return jnp.pad(w, ((0, 8 - w.shape[0]), (0, 0)))


def _dims(w):
    D = w["mix_pre_g"].shape[-1]
    DI = w["ssd_norm_g"].shape[-1]
    H = w["ssd_dt_bias"].shape[-1]
    conv_dim = w["ssd_conv_b"].shape[-1]
    G = (conv_dim - DI) // (2 * N_STATE)
    F = w["w_down"].shape[0]
    return dict(D=D, DI=DI, H=H, P=DI // H, G=G, R=H // G, GN=G * N_STATE, CD=conv_dim, F=F)


def _proj_layout(d):
    D, DI, CD, H = d["D"], d["DI"], d["CD"], d["H"]
    o = dict(z=0, xbc=DI, scb=DI + CD, scc=DI + CD + D, sch=DI + CD + 2 * D, g1=DI + CD + 3 * D, g2=DI + CD + 4 * D,
             dt=DI + CD + 5 * D)
    o["sb"] = math.gcd(SEG_BLK, D, DI, d["GN"])
    assert o["sb"] % 128 == 0 and H <= o["sb"]
    o["np"] = o["dt"] + o["sb"]
    return o


def _glu_perm(a, F, inverse=False):
    lead = a.shape[:-1]
    nb = F // GLU_W
    if not inverse:
        return a.reshape(*lead, 2, nb, GLU_W).swapaxes(-3, -2).reshape(*lead, 2 * F)
    return a.reshape(*lead, nb, 2, GLU_W).swapaxes(-3, -2).reshape(*lead, 2 * F)


def _glu_perm_rows(a, F, inverse=False):
    nb, D = F // GLU_W, a.shape[1]
    shape = (nb, 2, GLU_W, D) if inverse else (2, nb, GLU_W, D)
    return a.reshape(shape).swapaxes(0, 1).reshape(2 * F, D)


def _prep_layer(w):
    d = _dims(w)
    D, DI, CD, H, G, R, F = d["D"], d["DI"], d["CD"], d["H"], d["G"], d["R"], d["F"]
    lay = _proj_layout(d)
    w_in = w["w_in"]
    used = lay["dt"] + H
    wcat = jnp.concatenate([w_in[:DI + CD], w_in[DI + CD + H:], w_in[DI + CD:DI + CD + H],
                            jnp.zeros((lay["np"] - used, D), w_in.dtype)], axis=0)
    hp = jnp.stack([w["ssd_dt_bias"], w["ssd_a_log"], w["ssd_d"]], 0).astype(F32)
    hpc = jnp.pad(hp.reshape(3, G, R).transpose(1, 0, 2), ((0, 0), (0, 5), (0, 0)))
    hpr = jnp.pad(hp[:2].reshape(2, G, R).transpose(1, 2, 0), ((0, 0), (0, 0), (0, 6)))
    row = lambda v: v.reshape(1, -1).astype(F32)
    return dict(
        d=d, lay=lay, ada_w=w["ada_w"].astype(BF16), ada_b=row(w["ada_b"]),
        mix_pre_g=row(w["mix_pre_g"]), mix_post_g=row(w["mix_post_g"]), wcat=wcat.astype(BF16),
        ssd_conv_w=_pad8(w["ssd_conv_w"].astype(F32)), ssd_conv_b=row(w["ssd_conv_b"]), hpc=hpc, hpr=hpr,
        ssd_norm_g=row(w["ssd_norm_g"]), w_ssd_out=w["w_ssd_out"].astype(BF16),
        sc_conv_w=_pad8(w["sc_conv_w"].astype(F32)), w_sc_out=w["w_sc_out"].astype(BF16), w_o=w["w_o"].astype(BF16),
        ffn_pre_g=row(w["ffn_pre_g"]), ffn_post_g=row(w["ffn_post_g"]),
        w_up=_glu_perm_rows(w["w_up"], F).astype(BF16), ffn_conv_w=_pad8(_glu_perm(w["ffn_conv_w"].astype(F32), F)),
        ffn_conv_b=_glu_perm(row(w["ffn_conv_b"]), F), w_down=w["w_down"].astype(BF16))


def _dt_layouts(proj, lay, d):
    T = proj.shape[0]
    dt = proj[:, lay["dt"]:lay["dt"] + d["H"]].reshape(T, d["G"], d["R"])
    return dt.transpose(1, 0, 2), dt.transpose(1, 2, 0)


def _layer_fwd(x, c8, p, S, li, gather=None):
    d, lay = p["d"], p["lay"]
    D, DI, G, R, P, GN, CD = d["D"], d["DI"], d["G"], d["R"], d["P"], d["GN"], d["CD"]
    nb = x.shape[0] // S
    nm = lambda s: f"l{li}_{s}"
    mod, cact = _modk(c8, p["ada_w"], p["ada_b"], nm("mod"))
    mod3 = mod[:nb].reshape(nb, 1, 6 * D)
    h = _norm_mod(x, p["mix_pre_g"], mod3, 1, 0, S, nm("norm1"))
    proj = _mm(h, p["wcat"], "nt", F32, nm("mm_in"), caps=(1024, 1536, 2048))
    pre = _conv_fwd((proj, lay["xbc"]), CD, p["ssd_conv_w"], p["ssd_conv_b"], SSD_CONV_K, S, nm("ssdconv"))
    dtc, dtr = _dt_layouts(proj, lay, d)
    offs = (0, DI, DI + GN)
    gathered = None
    if gather is None:
        y, hs = _ssd_fwd(pre, offs, dtc, dtr, p["hpc"], p["hpr"], G, R, P, S, nm("ssd"))
    else:
        (y, hs), (gathered,) = _ssd_fwd(pre, offs, dtc, dtr, p["hpc"], p["hpr"], G, R, P, S, nm("ssd"),
                                        side=_side_gather_direct(gather))
    yn = _gate_norm_fwd(y, (proj, lay["z"]), p["ssd_norm_g"], G, S, nm("gnorm"))
    sc = _shortconv_fwd(proj, lay["scb"], lay["scc"], lay["sch"], D, p["sc_conv_w"], S, nm("sconv"))
    if gather is None:
        y_ssd = _mm(yn, p["w_ssd_out"], "nn", F32, nm("mm_ssdout"))
    else:
        y_ssd, (gathered,) = _mm(yn, p["w_ssd_out"], "nn", F32, nm("mm_ssdout"), side=_side_gather_forward(gathered))
    y_sc = _mm(sc, p["w_sc_out"], "nn", F32, nm("mm_scout"))
    m = _merge_fwd(proj, lay["g1"], lay["g2"], y_ssd, y_sc, S, nm("merge"))
    mix = _mm(m, p["w_o"], "nn", F32, nm("mm_o"))
    x1 = _resid_post(x, mix, mod3, 2, p["mix_post_g"], S, nm("post1"))
    h2 = _norm_mod(x1, p["ffn_pre_g"], mod3, 4, 3, S, nm("norm2"))
    uu = _mm(h2, p["w_up"], "nt", F32, nm("mm_up"), caps=(1024, 1408, 2048))
    a = _ffn_act_fwd(uu, p["ffn_conv_w"], p["ffn_conv_b"], S, nm("ffnact"))
    f = _mm(a, p["w_down"], "nn", F32, nm("mm_down"), caps=(1024, 1024, 1408))
    x2 = _resid_post(x1, f, mod3, 5, p["ffn_post_g"], S, nm("post2"))
    saved = dict(x=x, h=h, proj=proj, pre=pre, dtc=dtc, dtr=dtr, y=y, hs=hs, yn=yn, sc=sc, y_ssd=y_ssd, y_sc=y_sc,
                 m=m, mix=mix, x1=x1, h2=h2, uu=uu, a=a, f=f, mod3=mod3, cact=cact)
    return x2, saved, gathered


def _seq_sum(acc, nb):
    return acc.reshape(nb, 8, -1)[:, 0, :]


def _layer_bwd(dx2, p, s, S, li, chip_sums=None, early=None):
    d, lay = p["d"], p["lay"]
    D, DI, G, R, P, GN, CD, H, F = d["D"], d["DI"], d["G"], d["R"], d["P"], d["GN"], d["CD"], d["H"], d["F"]
    nb = dx2.shape[0] // S
    nm = lambda t: f"l{li}_{t}"
    mod3 = s["mod3"]
    g = {}
    exchanged = None
    df, dgt2, dpg2 = _post_bwd(s["f"], mod3, 5, p["ffn_post_g"], dx2, S, nm("post2_b"))
    g["ffn_post_g"] = dpg2[0]
    da = _mm(df, p["w_down"], "nt", BF16, nm("mm_down_bi"), caps=(1024, 1408, 2048))
    g["w_down"] = _mm(s["a"], df, "tn", WGRAD,nm("mm_down_bw"), caps=(1408, 1024, 1024))
    if chip_sums is None:
        duu, cw = _ffn_act_bwd(s["uu"], da, p["ffn_conv_w"], p["ffn_conv_b"], S, nm("ffnact_b"))
    else:
        (duu, cw), (exchanged,) = _ffn_act_bwd(s["uu"], da, p["ffn_conv_w"], p["ffn_conv_b"], S, nm("ffnact_b"),
                                               side=_side_chip_exchange(chip_sums))
    g["ffn_conv_w"] = _glu_perm(cw[:FFN_CONV_K], F, inverse=True)
    g["ffn_conv_b"] = _glu_perm(cw[FFN_CONV_K], F, inverse=True)
    dh2 = _mm(duu, p["w_up"], "nn", F32, nm("mm_up_bi"), caps=(1024, 1024, 2816))
    g["w_up"] = _glu_perm_rows(_mm(duu, s["h2"], "tn", WGRAD,nm("mm_up_bw"), caps=(1408, 1024, 1024)), F, inverse=True)
    dx1, dg2, dsc2, dsh2 = _pre_bwd(s["x1"], p["ffn_pre_g"], mod3, 4, dh2, dx2, S, nm("norm2_b"))
    g["ffn_pre_g"] = dg2[0]
    dmix, dgt1, dpg1 = _post_bwd(s["mix"], mod3, 2, p["mix_post_g"], dx1, S, nm("post1_b"))
    g["mix_post_g"] = dpg1[0]
    dm = _mm(dmix, p["w_o"], "nt", F32, nm("mm_o_bi"))
    g["w_o"] = _mm(s["m"], dmix, "tn", WGRAD,nm("mm_o_bw"))
    proj = s["proj"]
    dy_ssd, dy_sc, dg1, dg2_ = _merge_bwd(proj, lay["g1"], lay["g2"], s["y_ssd"], s["y_sc"], dm, S, nm("merge_b"))
    dyn = _mm(dy_ssd, p["w_ssd_out"], "nt", F32, nm("mm_ssdout_bi"))
    g["w_ssd_out"] = _mm(s["yn"], dy_ssd, "tn", WGRAD,nm("mm_ssdout_bw"))
    dsc = _mm(dy_sc, p["w_sc_out"], "nt", F32, nm("mm_scout_bi"))
    g["w_sc_out"] = _mm(s["sc"], dy_sc, "tn", WGRAD,nm("mm_scout_bw"))
    dscb, dscc, dsch, scw = _shortconv_bwd(proj, lay["scb"], lay["scc"], lay["sch"], D, p["sc_conv_w"], dsc, S, nm("sconv_b"))
    g["sc_conv_w"] = scw[:SC_CONV_K]
    dy, dz, dng = _gate_norm_bwd(s["y"], (proj, lay["z"]), p["ssd_norm_g"], dyn, G, S, nm("gnorm_b"))
    g["ssd_norm_g"] = dng[0]
    offs = (0, DI, DI + GN)
    early_side = None if early is None else _side_chip_exchange(early(g))
    res = _ssd_bwd(s["pre"], offs, s["dtc"], s["dtr"], p["hpc"], p["hpr"], s["hs"], dy, G, R, P, S, nm("ssd_b"),
                   side=early_side)
    (dpx, dpb, dpc, ddt, hpg), early_got = res if early is not None else (res, None)
    g["ssd_dt_bias"], g["ssd_a_log"], g["ssd_d"] = hpg[:, 0, :].reshape(H), hpg[:, 1, :].reshape(H), hpg[:, 2, :].reshape(H)
    cws, dxbc = [], []
    for name, darr, off, C in (("x", dpx, 0, DI), ("b", dpb, DI, GN), ("c", dpc, DI + GN, GN)):
        w8 = p["ssd_conv_w"][:, off:off + C]
        cws.append(_conv_bwd_w((darr, 0), (proj, lay["xbc"] + off), C, SSD_CONV_K, S, nm(f"ssdconv_bw_{name}")))
        dxbc.append(_conv_bwd_in((darr, 0), C, w8, SSD_CONV_K, S, BF16, nm(f"ssdconv_bi_{name}")))
    cws = jnp.concatenate(cws, axis=1)
    g["ssd_conv_w"], g["ssd_conv_b"] = cws[:SSD_CONV_K], cws[SSD_CONV_K]
    T = dx2.shape[0]
    ddt_t = jnp.pad(ddt.transpose(1, 0, 2).reshape(T, H).astype(BF16), ((0, 0), (0, lay["sb"] - H)))
    dproj = [dz] + dxbc + [dscb, dscc, dsch, dg1, dg2_, ddt_t]
    dh = _mm_seg(dproj, p["wcat"], "nn", F32, nm("mm_in_bi"), lay["sb"])
    dwcat = _mm_seg(dproj, s["h"], "tn", WGRAD, nm("mm_in_bw"), lay["sb"], tk=1024)
    o = lay
    g["w_in"] = jnp.concatenate([dwcat[o["z"]:o["scb"]], dwcat[o["dt"]:o["dt"] + H], dwcat[o["scb"]:o["dt"]]], axis=0)
    dx, dg1_, dsc1, dsh1 = _pre_bwd(s["x"], p["mix_pre_g"], mod3, 1, dh, dx1, S, nm("norm1_b"))
    g["mix_pre_g"] = dg1_[0]
    dmod = jnp.concatenate([_seq_sum(t, nb) for t in (dsh1, dsc1, dgt1, dsh2, dsc2, dgt2)], axis=1)
    dmod8 = jnp.pad(dmod, ((0, MOD_ROWS - nb), (0, 0)))
    g["ada_b"] = _colsum(dmod8, nm("adab"))
    g["ada_w"] = _mm(dmod8, s["cact"], "tn", WGRAD, nm("mm_ada_bw"), caps=(1536, 1024, 2048))
    return dx, g, exchanged, None if early_got is None else early_got[0]


def _colsum(a8, name):
    rows, C = a8.shape
    tc = _tile(C, 2048)

    def body(a_ref, o_ref):
        o_ref[...] = _bsum(jnp.sum(a_ref[...], axis=0, keepdims=True))

    return _pc(body, grid=(C // tc,), in_specs=[pl.BlockSpec((rows, tc), lambda j: (0, j))],
               out_specs=pl.BlockSpec((8, tc), lambda j: (0, j)), out_shape=SDS((8, C), F32), name=name)(a8)[0]


def _adam(gs, w, m, v, name):
    ns, R, W = gs.shape
    tr = _tile(R, 256, 8)

    def body(g_ref, w_ref, m_ref, v_ref, go_ref, d_ref, mo_ref, vo_ref):
        g = g_ref[0].astype(F32)
        for k in range(1, ns):
            g = g + g_ref[k].astype(F32)
        go_ref[...] = g
        d_ref[...], mo_ref[...], vo_ref[...] = _adam_update(g, w_ref[...], m_ref[...], v_ref[...])

    row = pl.BlockSpec((tr, W), lambda i: (i, 0))
    return _pc(body, grid=(R // tr,), in_specs=[pl.BlockSpec((ns, tr, W), lambda i: (0, i, 0)), row, row, row],
               out_specs=[row] * 4, out_shape=[SDS((R, W), F32)] * 4, name=name)(gs, w, m, v)


def _adam_update(g, w, m, v):
    c1 = 1.0 / (1.0 - ADAM_B1 ** ADAM_STEP)
    c2 = 1.0 / (1.0 - ADAM_B2 ** ADAM_STEP)
    m_ = ADAM_B1 * m + (1.0 - ADAM_B1) * g
    v_ = ADAM_B2 * v + (1.0 - ADAM_B2) * (g * g)
    return -ADAM_LR * ((m_ * c1) / (jnp.sqrt(v_ * c2) + ADAM_EPS) + ADAM_WD * w), m_, v_


def _adam_nat(g, w, m, v, name):
    depth, a, b = w.shape
    tr = _tile(a, 256, 8)

    def body(g_ref, w_ref, m_ref, v_ref, d_ref, mo_ref, vo_ref):
        d_ref[...], mo_ref[...], vo_ref[...] = _adam_update(g_ref[...], w_ref[...], m_ref[...], v_ref[...])

    blk = pl.BlockSpec((None, tr, b), lambda l, i: (l, i, 0))
    return _pc(body, grid=(depth, a // tr), in_specs=[blk] * 4, out_specs=[blk] * 3,
               out_shape=[SDS(w.shape, F32)] * 3, name=name)(g, w, m, v)


def _sum_chips(gs, name):
    ns, R, W = gs.shape
    tr = _tile(R, 256, 16)

    def body(g_ref, o_ref):
        acc = g_ref[0].astype(F32)
        for k in range(1, ns):
            acc = acc + g_ref[k].astype(F32)
        o_ref[...] = acc

    return _pc(body, grid=(R // tr,), in_specs=[pl.BlockSpec((ns, tr, W), lambda i: (0, i, 0))],
               out_specs=pl.BlockSpec((tr, W), lambda i: (i, 0)), out_shape=SDS((R, W), F32), name=name)(gs)


HBM_SPEC = pl.BlockSpec(memory_space=pltpu.HBM)
VMEM_SPEC = pl.BlockSpec(memory_space=pltpu.VMEM)


def _dev():
    return lax.axis_index("x"), lax.axis_index("y"), lax.axis_index("c")


def _allgather_big(loc, name):
    R, W = loc.shape

    def body(x_ref, out_ref, send_sems, recv_sems, local_sem):
        x, y, c = _dev()
        me, sibling = (x, y, c), (x, y, 1 - c)
        chips = [(1 - x, y), (x, 1 - y), (1 - x, 1 - y)]

        def slab(px, py, pc):
            return out_ref.at[4 * px + 2 * py + pc]

        def copy(k, block, to, src=None):
            return pltpu.make_async_remote_copy(
                src_ref=slab(*block) if src is None else src, dst_ref=slab(*block),
                send_sem=send_sems.at[k], recv_sem=recv_sems.at[k], device_id=to, device_id_type=MESH)

        mine = pltpu.make_async_copy(x_ref, slab(*me), local_sem)
        mine.start()
        first = [copy(0, me, sibling, src=x_ref)]
        first += [copy(1 + j, me, (*chip, c), src=x_ref) for j, chip in enumerate(chips)]
        for cp in first:
            cp.start()
        passed = [copy(4 + j, (*chip, c), sibling) for j, chip in enumerate(chips)]
        for j, chip in enumerate(chips):
            copy(1 + j, (*chip, c), me).wait_recv()
            passed[j].start()
        copy(0, sibling, me).wait_recv()
        for j, chip in enumerate(chips):
            copy(4 + j, (*chip, 1 - c), me).wait_recv()
        for cp in first + passed:
            cp.wait_send()
        mine.wait()

    return pl.pallas_call(
        body, out_shape=SDS((N_DEV, R, W), loc.dtype), in_specs=[HBM_SPEC], out_specs=HBM_SPEC,
        scratch_shapes=[pltpu.SemaphoreType.DMA((7,)), pltpu.SemaphoreType.DMA((7,)), pltpu.SemaphoreType.DMA],
        name=name)(loc)


def _dma_sems(n):
    return (pltpu.SemaphoreType.DMA((n,)), pltpu.SemaphoreType.DMA((n,)), pltpu.SemaphoreType.DMA)


def _side_gather_direct(loc):
    R, W = loc.shape

    def copies(ins, outs, sems):
        x, y, c = _dev()
        x_ref, out = ins[0], outs[0]
        me = 4 * x + 2 * y + c
        peers = [(x, y, 1 - c), (1 - x, y, c), (x, 1 - y, c), (1 - x, 1 - y, c)]
        mk = lambda k, p, dst: pltpu.make_async_remote_copy(
            src_ref=x_ref, dst_ref=out.at[dst], send_sem=sems[0].at[k], recv_sem=sems[1].at[k], device_id=p,
            device_id_type=MESH)
        sends = [mk(k, p, me) for k, p in enumerate(peers)]
        recvs = [mk(k, p, 4 * p[0] + 2 * p[1] + p[2]) for k, p in enumerate(peers)]
        return sends, recvs, pltpu.make_async_copy(x_ref, out.at[me], sems[2])

    def start(ins, outs, sems):
        sends, _, mine = copies(ins, outs, sems)
        mine.start()
        for cp in sends:
            cp.start()

    def wait(ins, outs, sems):
        sends, recvs, mine = copies(ins, outs, sems)
        for cp in recvs:
            cp.wait_recv()
        for cp in sends:
            cp.wait_send()
        mine.wait()

    return _Side((loc,), (SDS((N_DEV, R, W), loc.dtype),), _dma_sems(4), start, wait)


def _side_gather_forward(buf):
    def copies(ins, outs, sems):
        x, y, c = _dev()
        out = outs[0]
        chips = [(1 - x, y), (x, 1 - y), (1 - x, 1 - y)]
        mk = lambda k, src, dst: pltpu.make_async_remote_copy(
            src_ref=out.at[src], dst_ref=out.at[dst], send_sem=sems[0].at[k], recv_sem=sems[1].at[k],
            device_id=(x, y, 1 - c), device_id_type=MESH)
        mine = [4 * px + 2 * py + c for px, py in chips]
        theirs = [4 * px + 2 * py + (1 - c) for px, py in chips]
        return [mk(k, s, s) for k, s in enumerate(mine)], [mk(k, s, t) for k, (s, t) in enumerate(zip(mine, theirs))]

    def start(ins, outs, sems):
        for cp in copies(ins, outs, sems)[0]:
            cp.start()

    def wait(ins, outs, sems):
        sends, recvs = copies(ins, outs, sems)
        for cp in recvs:
            cp.wait_recv()
        for cp in sends:
            cp.wait_send()

    return _Side((buf,), (SDS(buf.shape, buf.dtype),), _dma_sems(3)[:2], start, wait, {0: 0})


def _side_chip_exchange(p):
    def copies(ins, outs, sems):
        x, y, c = _dev()
        p_ref, out = ins[0], outs[0]
        j0 = 2 * x + y
        chips = [(1 - x, y), (x, 1 - y), (1 - x, 1 - y)]
        mk = lambda k, chip, src, dst: pltpu.make_async_remote_copy(
            src_ref=p_ref.at[src], dst_ref=out.at[dst], send_sem=sems[0].at[k], recv_sem=sems[1].at[k],
            device_id=(*chip, c), device_id_type=MESH)
        sends = [mk(k, chip, 2 * chip[0] + chip[1], j0) for k, chip in enumerate(chips)]
        recvs = [mk(k, chip, j0, 2 * chip[0] + chip[1]) for k, chip in enumerate(chips)]
        return sends, recvs, pltpu.make_async_copy(p_ref.at[j0], out.at[j0], sems[2])

    def start(ins, outs, sems):
        sends, _, mine = copies(ins, outs, sems)
        mine.start()
        for cp in sends:
            cp.start()

    def wait(ins, outs, sems):
        sends, recvs, mine = copies(ins, outs, sems)
        for cp in recvs:
            cp.wait_recv()
        for cp in sends:
            cp.wait_send()
        mine.wait()

    return _Side((p,), (SDS(p.shape, p.dtype),), _dma_sems(3), start, wait)


def _rs_pair_exchange(g, name):
    nd, R, W = g.shape
    nj = nd // 2

    def body(g_ref, out_ref, send_sems, recv_sems):
        x, y, c = _dev()
        cps = [pltpu.make_async_remote_copy(src_ref=g_ref.at[2 * j + (1 - c)], dst_ref=out_ref.at[j],
                                            send_sem=send_sems.at[j], recv_sem=recv_sems.at[j],
                                            device_id=(x, y, 1 - c), device_id_type=MESH) for j in range(nj)]
        for cp in cps:
            cp.start()
        for cp in cps:
            cp.wait()

    return pl.pallas_call(
        body, out_shape=SDS((nj, R, W), g.dtype), in_specs=[HBM_SPEC], out_specs=HBM_SPEC,
        scratch_shapes=[pltpu.SemaphoreType.DMA((nj,)), pltpu.SemaphoreType.DMA((nj,))], name=name)(g)


def _add_pairs(g, ra, name):
    nd, R, W = g.shape
    nj = nd // 2
    tr = _tile(R, 256, 8)
    cidx = lax.axis_index("c").astype(jnp.int32).reshape(1)

    def body(c_ref, a_ref, b_ref, o_ref):
        o_ref[...] = (a_ref[...].astype(F32) + b_ref[...].astype(F32)).astype(o_ref.dtype)

    gs = pltpu.PrefetchScalarGridSpec(
        num_scalar_prefetch=1, grid=(nj, R // tr),
        in_specs=[pl.BlockSpec((None, tr, W), lambda j, i, cr: (2 * j + cr[0], i, 0)),
                  pl.BlockSpec((None, tr, W), lambda j, i, cr: (j, i, 0))],
        out_specs=pl.BlockSpec((None, tr, W), lambda j, i, cr: (j, i, 0)))
    return pl.pallas_call(body, grid_spec=gs, out_shape=SDS((nj, R, W), g.dtype), name=name,
                          compiler_params=pltpu.CompilerParams(vmem_limit_bytes=VMEM_LIMIT))(cidx, g, ra)


def _rs_chip_exchange(p, name):
    nj, R, W = p.shape

    def body(p_ref, out_ref, send_sems, recv_sems, local_sem):
        x, y, c = _dev()
        j0 = 2 * x + y
        chips = [(1 - x, y), (x, 1 - y), (1 - x, 1 - y)]
        mine = pltpu.make_async_copy(p_ref.at[j0], out_ref.at[j0], local_sem)
        mine.start()

        def copy(k, chip):
            return pltpu.make_async_remote_copy(
                src_ref=p_ref.at[2 * chip[0] + chip[1]], dst_ref=out_ref.at[j0],
                send_sem=send_sems.at[k], recv_sem=recv_sems.at[k], device_id=(*chip, c), device_id_type=MESH)

        sent = [copy(k, chip) for k, chip in enumerate(chips)]
        for cp in sent:
            cp.start()
        for k, chip in enumerate(chips):
            pltpu.make_async_remote_copy(
                src_ref=p_ref.at[j0], dst_ref=out_ref.at[2 * chip[0] + chip[1]],
                send_sem=send_sems.at[k], recv_sem=recv_sems.at[k], device_id=(*chip, c), device_id_type=MESH).wait_recv()
        for cp in sent:
            cp.wait_send()
        mine.wait()

    return pl.pallas_call(
        body, out_shape=SDS((nj, R, W), p.dtype), in_specs=[HBM_SPEC], out_specs=HBM_SPEC,
        scratch_shapes=[pltpu.SemaphoreType.DMA((3,)), pltpu.SemaphoreType.DMA((3,)), pltpu.SemaphoreType.DMA],
        name=name)(p)


def _allgather_small(v, name):
    R, W = v.shape

    def body(v_ref, out_ref, send_sems, recv_sems, local_sem):
        x, y, c = _dev()
        mine = pltpu.make_async_copy(v_ref, out_ref.at[4 * x + 2 * y + c], local_sem)
        mine.start()
        peers = []
        for k in range(1, N_DEV):
            px = 1 - x if k & 4 else x
            py = 1 - y if k & 2 else y
            pc_ = 1 - c if k & 1 else c
            peers.append((px, py, pc_))
        sent = [pltpu.make_async_remote_copy(
            src_ref=v_ref, dst_ref=out_ref.at[4 * x + 2 * y + c], send_sem=send_sems.at[k], recv_sem=recv_sems.at[k],
            device_id=peer, device_id_type=MESH) for k, peer in enumerate(peers)]
        for cp in sent:
            cp.start()
        for k, (px, py, pc_) in enumerate(peers):
            pltpu.make_async_remote_copy(
                src_ref=v_ref, dst_ref=out_ref.at[4 * px + 2 * py + pc_], send_sem=send_sems.at[k],
                recv_sem=recv_sems.at[k], device_id=(px, py, pc_), device_id_type=MESH).wait_recv()
        for cp in sent:
            cp.wait_send()
        mine.wait()

    return pl.pallas_call(
        body, out_shape=SDS((N_DEV, R, W), v.dtype), in_specs=[VMEM_SPEC], out_specs=VMEM_SPEC,
        scratch_shapes=[pltpu.SemaphoreType.DMA((7,)), pltpu.SemaphoreType.DMA((7,)), pltpu.SemaphoreType.DMA],
        name=name)(v)


def _sum_slabs(a, name):
    ns, R, W = a.shape

    def body(a_ref, o_ref):
        acc = a_ref[0]
        for k in range(1, ns):
            acc = acc + a_ref[k]
        o_ref[...] = acc

    return pl.pallas_call(body, out_shape=SDS((R, W), a.dtype), in_specs=[VMEM_SPEC], out_specs=VMEM_SPEC, name=name)(a)


BIG = (("ada_w", "col"), ("w_in", "col"), ("w_ssd_out", "row"), ("w_sc_out", "row"), ("w_o", "row"), ("w_up", "col"),
       ("w_down", "row"))
EARLY = ("w_ssd_out", "w_sc_out", "w_o", "w_up", "w_down")
LATE = ("ada_w", "w_in")
CONVW = ("ssd_conv_w", "sc_conv_w", "ffn_conv_w")
REPL = ("ada_b", "mix_pre_g", "mix_post_g", "ssd_conv_b", "ssd_dt_bias", "ssd_a_log", "ssd_d", "ssd_norm_g", "ffn_pre_g",
        "ffn_post_g", "ffn_conv_b")
WEIGHTS = ("ada_w", "ada_b", "mix_pre_g", "mix_post_g", "w_in", "ssd_conv_w", "ssd_conv_b", "ssd_dt_bias", "ssd_a_log",
           "ssd_d", "ssd_norm_g", "w_ssd_out", "sc_conv_w", "w_sc_out", "w_o", "ffn_pre_g", "ffn_post_g", "w_up",
           "ffn_conv_w", "ffn_conv_b", "w_down")


def _pad_rows(a, mult):
    r = a.shape[-2]
    pad = -r % mult
    return a if pad == 0 else jnp.pad(a, [(0, 0)] * (a.ndim - 2) + [(0, pad), (0, 0)])


def _flat_rows(parts, mult):
    flat = jnp.concatenate([p.reshape(-1) for p in parts])
    flat = jnp.pad(flat, (0, -flat.shape[0] % ROW_W))
    return _pad_rows(flat.reshape(-1, ROW_W), mult)


def _unflat(buf, shapes):
    flat = buf.reshape(-1)
    out, o = [], 0
    for shp in shapes:
        n = 1
        for s in shp:
            n *= s
        out.append(flat[o:o + n].reshape(shp))
        o += n
    return out


def _pack_big_local(get, l):
    return [_pad_rows((get(n)[l].T if kind == "col" else get(n)[l]).reshape(-1, ROW_W), SLAB_ALIGN) for n, kind in BIG]


def _big_rows(shapes, names=None):
    out, o = {}, 0
    for n in (names if names is not None else [n for n, _ in BIG]):
        r = shapes[n][1] * shapes[n][2] // ROW_W
        out[n] = (o, o + r)
        o += -(-r // SLAB_ALIGN) * SLAB_ALIGN
    return out, o


def kernel(x, c, ada_w, ada_b, mix_pre_g, mix_post_g, w_in, ssd_conv_w, ssd_conv_b, ssd_dt_bias, ssd_a_log, ssd_d, ssd_norm_g, w_ssd_out, sc_conv_w, w_sc_out, w_o, ffn_pre_g, ffn_post_g, w_up, ffn_conv_w, ffn_conv_b, w_down, loss_target, m_ada_w, m_ada_b, m_mix_pre_g, m_mix_post_g, m_w_in, m_ssd_conv_w, m_ssd_conv_b, m_ssd_dt_bias, m_ssd_a_log, m_ssd_d, m_ssd_norm_g, m_w_ssd_out, m_sc_conv_w, m_w_sc_out, m_w_o, m_ffn_pre_g, m_ffn_post_g, m_w_up, m_ffn_conv_w, m_ffn_conv_b, m_w_down, v_ada_w, v_ada_b, v_mix_pre_g, v_mix_post_g, v_w_in, v_ssd_conv_w, v_ssd_conv_b, v_ssd_dt_bias, v_ssd_a_log, v_ssd_d, v_ssd_norm_g, v_w_ssd_out, v_sc_conv_w, v_w_sc_out, v_w_o, v_ffn_pre_g, v_ffn_post_g, v_w_up, v_ffn_conv_w, v_ffn_conv_b, v_w_down):
    wl = dict(zip(WEIGHTS, (ada_w, ada_b, mix_pre_g, mix_post_g, w_in, ssd_conv_w, ssd_conv_b, ssd_dt_bias, ssd_a_log,
                            ssd_d, ssd_norm_g, w_ssd_out, sc_conv_w, w_sc_out, w_o, ffn_pre_g, ffn_post_g, w_up,
                            ffn_conv_w, ffn_conv_b, w_down)))
    ml = dict(zip(WEIGHTS, (m_ada_w, m_ada_b, m_mix_pre_g, m_mix_post_g, m_w_in, m_ssd_conv_w, m_ssd_conv_b,
                            m_ssd_dt_bias, m_ssd_a_log, m_ssd_d, m_ssd_norm_g, m_w_ssd_out, m_sc_conv_w, m_w_sc_out, m_w_o,
                            m_ffn_pre_g, m_ffn_post_g, m_w_up, m_ffn_conv_w, m_ffn_conv_b, m_w_down)))
    vl = dict(zip(WEIGHTS, (v_ada_w, v_ada_b, v_mix_pre_g, v_mix_post_g, v_w_in, v_ssd_conv_w, v_ssd_conv_b,
                            v_ssd_dt_bias, v_ssd_a_log, v_ssd_d, v_ssd_norm_g, v_w_ssd_out, v_sc_conv_w, v_w_sc_out, v_w_o,
                            v_ffn_pre_g, v_ffn_post_g, v_w_up, v_ffn_conv_w, v_ffn_conv_b, v_w_down)))
    depth = ada_w.shape[0]
    shapes = {n: wl[n].shape for n in WEIGHTS}
    me = 4 * lax.axis_index("x") + 2 * lax.axis_index("y") + lax.axis_index("c")

    rows, n_big = _big_rows(shapes)
    conv_flat = jnp.concatenate([wl[n][l].reshape(-1) for l in range(depth) for n in CONVW])
    n_conv = conv_flat.shape[0]
    conv_flat = jnp.pad(conv_flat, (0, -n_conv % (ROW_W // 2)))
    conv_rows = lax.bitcast_convert_type(conv_flat, BF16).reshape(-1, ROW_W)

    def local_rows(l):
        pieces = _pack_big_local(lambda n: wl[n].astype(BF16), l) + ([conv_rows] if l == 0 else [])
        return _pad_rows(jnp.concatenate(pieces, axis=0), ROW_PAD)

    def layer_weights(l, gathered):
        w = {n: wl[n][l] for n in REPL}
        for n, kind in BIG:
            a, b = shapes[n][1], shapes[n][2]
            blk = gathered[:, rows[n][0]:rows[n][1]]
            w[n] = blk.reshape(N_DEV * b, a) if kind == "col" else blk.reshape(N_DEV * a, b)
        for n in CONVW:
            w[n] = conv_full[(l, n)]
        return w

    gathered = _allgather_big(local_rows(0), "allgather_weights")
    conv_all = lax.bitcast_convert_type(
        gathered[:, n_big:n_big + conv_rows.shape[0]].reshape(N_DEV, -1, 2), F32)[:, :n_conv]
    conv_full, o = {}, 0
    for l in range(depth):
        for n in CONVW:
            k, cl = shapes[n][1], shapes[n][2]
            conv_full[(l, n)] = conv_all[:, o:o + k * cl].reshape(N_DEV, k, cl).transpose(1, 0, 2).reshape(k, N_DEV * cl)
            o += k * cl

    nb, S, D = x.shape
    T = nb * S
    act = x.reshape(T, D)
    c8 = jnp.pad(c, ((0, MOD_ROWS - nb), (0, 0)))
    preps, saved = [], []
    for l in range(depth):
        preps.append(_prep_layer(layer_weights(l, gathered)))
        act, s, gathered = _layer_fwd(act, c8, preps[l], S, l, gather=local_rows(l + 1) if l + 1 < depth else None)
        saved.append(s)
    dy, lacc = _loss(act, loss_target.reshape(T, D), S, "loss")
    loss_loc = lacc[0, 0]

    group_rows = {grp: _big_rows(shapes, names) for grp, names in (("early", EARLY), ("late", LATE))}

    def pair_sums(g, grp, names, l):
        slabs = [_pad_rows(g[n].astype(BF16).reshape(N_DEV, -1, ROW_W), SLAB_ALIGN) for n in names]
        slabs.append(jnp.zeros((N_DEV, -group_rows[grp][1] % ROW_PAD, ROW_W), BF16))
        gslab = jnp.concatenate(slabs, axis=1)
        from_sibling = _rs_pair_exchange(gslab, f"rs_pair_exchange_{grp}_l{l}")
        return _add_pairs(gslab, from_sibling, f"rs_pair_add_{grp}_l{l}")

    grads, pending = [None] * depth, None
    from_chips = {"early": [None] * depth, "late": [None] * depth}
    for l in reversed(range(depth)):
        dy, grads[l], got, from_chips["early"][l] = _layer_bwd(
            dy, preps[l], saved[l], S, l, chip_sums=pending, early=lambda g, l=l: pair_sums(g, "early", EARLY, l))
        if pending is not None:
            from_chips["late"][l + 1] = got
        pending = pair_sums(grads[l], "late", LATE, l)
    from_chips["late"][0] = _rs_chip_exchange(pending, "rs_chip_exchange")
    dx = dy.reshape(nb, S, D)
    g_sums = {grp: [_sum_chips(from_chips[grp][l], f"rs_chip_sum_{grp}_l{l}") for l in range(depth)]
              for grp in ("early", "late")}

    def grad_of(l, n, kind):
        grp = "early" if n in EARLY else "late"
        r0, r1 = group_rows[grp][0][n]
        blk = g_sums[grp][l][r0:r1]
        a, b = shapes[n][1], shapes[n][2]
        return blk.reshape(b, a).T if kind == "col" else blk.reshape(a, b)

    g_big = {n: jnp.stack([grad_of(l, n, kind) for l in range(depth)]) for n, kind in BIG}
    d_big, m_big, v_big = {}, {}, {}
    for n, _ in BIG:
        d_big[n], m_big[n], v_big[n] = _adam_nat(g_big[n], wl[n], ml[n], vl[n], f"adam_{n}")

    parts = [jnp.broadcast_to(loss_loc, (ROW_W,))]
    small_shapes = [(ROW_W,)]
    for l in range(depth):
        for n in REPL + CONVW:
            parts.append(grads[l][n])
            small_shapes.append(tuple(grads[l][n].shape))
    total = _sum_slabs(_allgather_small(_flat_rows(parts, 8), "allgather_small"), "sum_small")
    pieces = _unflat(total, small_shapes)
    loss = pieces[0][0]
    g_small, i = {}, 1
    for l in range(depth):
        for n in REPL + CONVW:
            gp = pieces[i]
            i += 1
            if n in CONVW:
                gp = lax.dynamic_slice_in_dim(gp, me * shapes[n][2], shapes[n][2], axis=1)
            g_small[(l, n)] = gp
    order = [(l, n) for l in range(depth) for n in REPL + CONVW]
    loc_shapes = [tuple(shapes[n][1:]) for _, n in order]
    packs = lambda f: _flat_rows([f(l, n) for l, n in order], 8)
    gs_small = packs(lambda l, n: g_small[(l, n)])
    _, d_sm, m_sm, v_sm = _adam(gs_small[None], packs(lambda l, n: wl[n][l]), packs(lambda l, n: ml[n][l]),
                                packs(lambda l, n: vl[n][l]), "adam_small")

    def unpack_small(buf):
        ps = _unflat(buf, loc_shapes)
        return {n: jnp.stack([ps[order.index((l, n))] for l in range(depth)]) for n in REPL + CONVW}

    outs = []
    for big, small in ((g_big, {n: jnp.stack([g_small[(l, n)] for l in range(depth)]) for n in REPL + CONVW}),
                       (d_big, unpack_small(d_sm)), (m_big, unpack_small(m_sm)), (v_big, unpack_small(v_sm))):
        merged = {**big, **small}
        outs += [merged[n] for n in WEIGHTS]
    return (loss, dx, *outs)
```

```python
import functools
import math
from typing import Callable, NamedTuple

import jax
import jax.numpy as jnp
from jax import lax
from jax.experimental import pallas as pl
from jax.experimental.pallas import tpu as pltpu

F32, BF16 = jnp.float32, jnp.bfloat16
WGRAD = BF16
SDS = jax.ShapeDtypeStruct
MESH = pl.DeviceIdType.MESH

EPS = 1e-6
N_STATE = 128
CHUNK = 128
SSD_CONV_K, SC_CONV_K, FFN_CONV_K = 4, 3, 3
N_DEV = 8
ROW_W = 1024
ROW_PAD = 32
SLAB_ALIGN = 16
SEG_BLK = 512
STRIP = 32
FFN_STRIP = 64
GLU_W = 256
MOD_ROWS = 128
VMEM_LIMIT = 48 * 2**20

ADAM_LR, ADAM_B1, ADAM_B2, ADAM_EPS, ADAM_WD, ADAM_STEP = 0.001, 0.9, 0.999, 1e-08, 0.01, 10

NT = (((1,), (1,)), ((), ()))
TN = (((0,), (0,)), ((), ()))
NN = (((1,), (0,)), ((), ()))


def _tile(n, cap, mult=128):
    best = None
    for t in range(mult, min(n, cap) + 1, mult):
        if n % t == 0:
            best = t
    return best if best is not None else n


class _Side(NamedTuple):
    operands: tuple
    out_shape: tuple
    scratch: tuple
    start: Callable
    wait: Callable
    aliases: dict = {}


def _pc(body, *, grid, in_specs, out_specs, out_shape, name, scratch=(), side=None):
    params = pltpu.CompilerParams(dimension_semantics=("arbitrary",) * len(grid), vmem_limit_bytes=VMEM_LIMIT)
    if side is None:
        return pl.pallas_call(body, grid=grid, in_specs=in_specs, out_specs=out_specs, out_shape=out_shape,
                              scratch_shapes=list(scratch), name=name, compiler_params=params)
    single = not isinstance(out_shape, (list, tuple))
    outs = [out_shape] if single else list(out_shape)
    ospecs = [out_specs] if single else list(out_specs)
    n_in, n_out, n_scr = len(in_specs), len(outs), len(scratch)
    s_in, s_out = len(side.operands), len(side.out_shape)

    def hosted(*refs):
        ins, refs = refs[:n_in], refs[n_in:]
        sins, refs = refs[:s_in], refs[s_in:]
        mouts, refs = refs[:n_out], refs[n_out:]
        souts, refs = refs[:s_out], refs[s_out:]
        scr, sems = refs[:n_scr], refs[n_scr:]
        first = functools.reduce(lambda a, b: a & b, [pl.program_id(a) == 0 for a in range(len(grid))])
        last = functools.reduce(lambda a, b: a & b, [pl.program_id(a) == grid[a] - 1 for a in range(len(grid))])

        @pl.when(first)
        def _():
            side.start(sins, souts, sems)

        body(*ins, *mouts, *scr)

        @pl.when(last)
        def _():
            side.wait(sins, souts, sems)

    call = pl.pallas_call(
        hosted, grid=grid, in_specs=list(in_specs) + [HBM_SPEC] * s_in, out_specs=ospecs + [HBM_SPEC] * s_out,
        out_shape=outs + list(side.out_shape), scratch_shapes=list(scratch) + list(side.scratch), name=name,
        input_output_aliases={n_in + k: n_out + v for k, v in side.aliases.items()}, compiler_params=params)

    def run(*args):
        res = call(*args, *side.operands)
        main = res[0] if single else list(res[:n_out])
        return main, list(res[n_out:])

    return run


def _silu(x):
    return x * jax.nn.sigmoid(x)


def _dsilu(x):
    s = jax.nn.sigmoid(x)
    return s * (1.0 + x * (1.0 - s))


def _softplus(x):
    return jnp.maximum(x, 0.0) + jnp.log(1.0 + jnp.exp(-jnp.abs(x)))


def _dot(a, b, dims=NN):
    return lax.dot_general(a, b, dims, preferred_element_type=F32)


def _bsum(v, rows=8):
    return jnp.broadcast_to(v, (rows, v.shape[1]))


def _mm(a, b, mode, out_dtype, name, caps=(1024, 1024, 2048), side=None):
    if mode == "nn":
        (M, K), (K2, N) = a.shape, b.shape
    elif mode == "nt":
        (M, K), (N, K2) = a.shape, b.shape
    else:
        (K, M), (K2, N) = a.shape, b.shape
    assert K == K2, (a.shape, b.shape, mode)
    tm, tn, tk = _tile(M, caps[0]), _tile(N, caps[1]), _tile(K, caps[2])
    nk = K // tk
    dims = {"nn": NN, "nt": NT, "tn": TN}[mode]
    if mode == "tn":
        a_spec = pl.BlockSpec((tk, tm), lambda i, j, k: (k, i))
    else:
        a_spec = pl.BlockSpec((tm, tk), lambda i, j, k: (i, k))
    if mode == "nt":
        b_spec = pl.BlockSpec((tn, tk), lambda i, j, k: (j, k))
    else:
        b_spec = pl.BlockSpec((tk, tn), lambda i, j, k: (k, j))

    def body(a_ref, b_ref, o_ref, *acc):
        part = _dot(a_ref[...].astype(BF16), b_ref[...].astype(BF16), dims)
        if nk == 1:
            o_ref[...] = part.astype(o_ref.dtype)
        else:
            acc_ref, = acc
            k = pl.program_id(2)

            @pl.when(k == 0)
            def _():
                acc_ref[...] = part

            @pl.when(k > 0)
            def _():
                acc_ref[...] += part

            @pl.when(k == nk - 1)
            def _():
                o_ref[...] = acc_ref[...].astype(o_ref.dtype)

    return _pc(body, grid=(M // tm, N // tn, nk), in_specs=[a_spec, b_spec],
               out_specs=pl.BlockSpec((tm, tn), lambda i, j, k: (i, j)),
               out_shape=SDS((M, N), out_dtype), name=name,
               scratch=() if nk == 1 else (pltpu.VMEM((tm, tn), F32),), side=side)(a, b)


def _mm_seg(segs, b, mode, out_dtype, name, blk, tile=1024, tk=2048):
    nblk = [a.shape[1] // blk for a in segs]
    assert all(a.shape[1] % blk == 0 for a in segs)
    start = [sum(nblk[:s]) for s in range(len(segs))]
    total = sum(nblk)
    ns = len(segs)
    N = b.shape[1]
    tn = _tile(N, tile)
    if mode == "nn":
        M = segs[0].shape[0]
        tm = _tile(M, tile)
        grid = (M // tm, N // tn, total)
        a_specs = [pl.BlockSpec((tm, blk), lambda i, j, k, k0=k0, n=n: (i, jnp.clip(k - k0, 0, n - 1)))
                   for k0, n in zip(start, nblk)]
        b_spec = pl.BlockSpec((blk, tn), lambda i, j, k: (k, j))
        out_rows, tmo, dims, seg_axis = M, tm, NN, 2
    else:
        K = segs[0].shape[0]
        tkk = _tile(K, tk)
        grid = (total, N // tn, K // tkk)
        a_specs = [pl.BlockSpec((tkk, blk), lambda i, j, k, i0=i0, n=n: (
            jnp.where((i >= i0) & (i < i0 + n), k, 0), jnp.clip(i - i0, 0, n - 1))) for i0, n in zip(start, nblk)]
        b_spec = pl.BlockSpec((tkk, tn), lambda i, j, k: (k, j))
        out_rows, tmo, seg_axis = total * blk, blk, 0
    nk = grid[2]
    acc_shape = (tm, tn) if mode == "nn" else (tn, blk)

    def body(*refs):
        a_refs, b_ref, o_ref, acc_ref = refs[:ns], refs[ns], refs[ns + 1], refs[ns + 2]
        k = pl.program_id(2)
        sel = pl.program_id(seg_axis)

        @pl.when(k == 0)
        def _():
            acc_ref[...] = jnp.zeros_like(acc_ref)

        for s in range(ns):
            @pl.when((sel >= start[s]) & (sel < start[s] + nblk[s]))
            def _(s=s):
                a_, b_ = a_refs[s][...].astype(BF16), b_ref[...].astype(BF16)
                acc_ref[...] += _dot(a_, b_, NN) if mode == "nn" else _dot(b_, a_, TN)

        @pl.when(k == nk - 1)
        def _():
            acc = acc_ref[...]
            o_ref[...] = (acc if mode == "nn" else acc.T).astype(o_ref.dtype)

    return _pc(body, grid=grid, in_specs=a_specs + [b_spec], out_specs=pl.BlockSpec((tmo, tn), lambda i, j, k: (i, j)),
               out_shape=SDS((out_rows, N), out_dtype), name=name, scratch=(pltpu.VMEM(acc_shape, F32),))(*segs, b)


def _modk(c8, ada_w, ada_b, name):
    rows, D = c8.shape
    N = ada_w.shape[0]
    tn = _tile(N, 1536)

    def body(c_ref, w_ref, b_ref, mod_ref, ca_ref):
        ca = _silu(c_ref[...]).astype(BF16)
        mod_ref[...] = _dot(ca, w_ref[...], NT) + b_ref[...]
        ca_ref[...] = ca

    return _pc(body, grid=(N // tn,),
               in_specs=[pl.BlockSpec((rows, D), lambda j: (0, 0)), pl.BlockSpec((tn, D), lambda j: (j, 0)),
                         pl.BlockSpec((1, tn), lambda j: (0, j))],
               out_specs=[pl.BlockSpec((rows, tn), lambda j: (0, j)), pl.BlockSpec((rows, D), lambda j: (0, 0))],
               out_shape=[SDS((rows, N), F32), SDS((rows, D), BF16)], name=name)(c8, ada_w, ada_b)


def _row_tile(S):
    return _tile(S, 512, 8)


def _strips(tm, fn, init=0, rows=None):
    rows = STRIP if rows is None else rows
    assert tm % rows == 0
    return lax.fori_loop(0, tm // rows, lambda r, c: fn(pl.multiple_of(r * rows, rows), c), init)


def _rows8(rows):
    pad = 8 - len(rows)
    return jnp.concatenate(rows + ([jnp.zeros((pad, rows[0].shape[1]), F32)] if pad else []), axis=0)


def _fold8(v):
    return jnp.sum(v.reshape(v.shape[0] // 8, 8, v.shape[1]), axis=0)


def _norm_mod(x, g, mod3, sc_seg, sh_seg, S, name):
    T, D = x.shape
    tm = _row_tile(S)
    tpb = S // tm

    def body(x_ref, g_ref, sc_ref, sh_ref, h_ref):
        x_ = x_ref[...]
        r = lax.rsqrt(jnp.mean(x_ * x_, axis=-1, keepdims=True) + EPS)
        h_ref[...] = ((x_ * r) * (g_ref[...] * (1.0 + sc_ref[...])) + sh_ref[...]).astype(BF16)

    return _pc(body, grid=(T // tm,),
               in_specs=[pl.BlockSpec((tm, D), lambda i: (i, 0)), pl.BlockSpec((1, D), lambda i: (0, 0)),
                         pl.BlockSpec((None, 1, D), lambda i: (i // tpb, 0, sc_seg)),
                         pl.BlockSpec((None, 1, D), lambda i: (i // tpb, 0, sh_seg))],
               out_specs=pl.BlockSpec((tm, D), lambda i: (i, 0)), out_shape=SDS((T, D), BF16), name=name)(x, g, mod3, mod3)


def _resid_post(x, fo, mod3, gt_seg, pg, S, name):
    T, D = x.shape
    tm = _row_tile(S)
    tpb = S // tm

    def body(x_ref, f_ref, gt_ref, pg_ref, o_ref):
        f = f_ref[...]
        r = lax.rsqrt(jnp.mean(f * f, axis=-1, keepdims=True) + EPS)
        o_ref[...] = x_ref[...] + (f * r) * (gt_ref[...] * pg_ref[...])

    return _pc(body, grid=(T // tm,),
               in_specs=[pl.BlockSpec((tm, D), lambda i: (i, 0)), pl.BlockSpec((tm, D), lambda i: (i, 0)),
                         pl.BlockSpec((None, 1, D), lambda i: (i // tpb, 0, gt_seg)),
                         pl.BlockSpec((1, D), lambda i: (0, 0))],
               out_specs=pl.BlockSpec((tm, D), lambda i: (i, 0)), out_shape=SDS((T, D), F32), name=name)(x, fo, mod3, pg)


def _post_bwd(fo, mod3, gt_seg, pg, dout, S, name):
    T, D = fo.shape
    tm = _row_tile(S)
    tpb = S // tm
    nb = T // S

    def body(f_ref, gt_ref, pg_ref, d_ref, df_ref, dgt_ref, dpg_ref):
        i = pl.program_id(0)

        @pl.when(i == 0)
        def _():
            dpg_ref[...] = jnp.zeros_like(dpg_ref)

        @pl.when(i % tpb == 0)
        def _():
            dgt_ref[...] = jnp.zeros_like(dgt_ref)

        f, d = f_ref[...], d_ref[...]
        r = lax.rsqrt(jnp.mean(f * f, axis=-1, keepdims=True) + EPS)
        n = f * r
        dn = d * (gt_ref[...] * pg_ref[...])
        df_ref[...] = (r * (dn - n * jnp.mean(dn * n, axis=-1, keepdims=True))).astype(df_ref.dtype)
        tot = jnp.sum(d * n, axis=0, keepdims=True)
        dgt_ref[...] += _bsum(tot * pg_ref[...])
        dpg_ref[...] += _bsum(tot * gt_ref[...])

    return _pc(body, grid=(T // tm,),
               in_specs=[pl.BlockSpec((tm, D), lambda i: (i, 0)),
                         pl.BlockSpec((None, 1, D), lambda i: (i // tpb, 0, gt_seg)),
                         pl.BlockSpec((1, D), lambda i: (0, 0)), pl.BlockSpec((tm, D), lambda i: (i, 0))],
               out_specs=[pl.BlockSpec((tm, D), lambda i: (i, 0)), pl.BlockSpec((8, D), lambda i: (i // tpb, 0)),
                          pl.BlockSpec((8, D), lambda i: (0, 0))],
               out_shape=[SDS((T, D), BF16), SDS((nb * 8, D), F32), SDS((8, D), F32)], name=name)(fo, mod3, pg, dout)


def _pre_bwd(x, g, mod3, sc_seg, dh, dout, S, name):
    T, D = x.shape
    tm = _row_tile(S)
    tpb = S // tm
    nb = T // S

    def body(x_ref, g_ref, sc_ref, dh_ref, d_ref, dx_ref, dg_ref, dsc_ref, dsh_ref):
        i = pl.program_id(0)

        @pl.when(i == 0)
        def _():
            dg_ref[...] = jnp.zeros_like(dg_ref)

        @pl.when(i % tpb == 0)
        def _():
            dsc_ref[...] = jnp.zeros_like(dsc_ref)
            dsh_ref[...] = jnp.zeros_like(dsh_ref)

        x_, dh_ = x_ref[...], dh_ref[...]
        r = lax.rsqrt(jnp.mean(x_ * x_, axis=-1, keepdims=True) + EPS)
        n = x_ * r
        dn = dh_ * (g_ref[...] * (1.0 + sc_ref[...]))
        dx_ref[...] = d_ref[...] + r * (dn - n * jnp.mean(dn * n, axis=-1, keepdims=True))
        dhn = jnp.sum(dh_ * n, axis=0, keepdims=True)
        dg_ref[...] += _bsum(dhn * (1.0 + sc_ref[...]))
        dsc_ref[...] += _bsum(dhn * g_ref[...])
        dsh_ref[...] += _bsum(jnp.sum(dh_, axis=0, keepdims=True))

    row = pl.BlockSpec((tm, D), lambda i: (i, 0))
    return _pc(body, grid=(T // tm,),
               in_specs=[row, pl.BlockSpec((1, D), lambda i: (0, 0)),
                         pl.BlockSpec((None, 1, D), lambda i: (i // tpb, 0, sc_seg)), row, row],
               out_specs=[row, pl.BlockSpec((8, D), lambda i: (0, 0)), pl.BlockSpec((8, D), lambda i: (i // tpb, 0)),
                          pl.BlockSpec((8, D), lambda i: (i // tpb, 0))],
               out_shape=[SDS((T, D), F32), SDS((8, D), F32), SDS((nb * 8, D), F32), SDS((nb * 8, D), F32)],
               name=name)(x, g, mod3, dh, dout)


def _loss(y, target, S, name):
    T, D = y.shape
    tm = _row_tile(S)

    def body(y_ref, t_ref, dy_ref, l_ref):
        @pl.when(pl.program_id(0) == 0)
        def _():
            l_ref[...] = jnp.zeros_like(l_ref)

        def strip(r0, carry):
            rows = pl.ds(r0, STRIP)
            e = y_ref[rows, :] - t_ref[rows, :]
            dy_ref[rows, :] = e * (1.0 / D)
            return carry + _fold8(e * e)

        acc = _strips(tm, strip, jnp.zeros((8, D), F32))
        l_ref[...] += jnp.broadcast_to(jnp.sum(acc, keepdims=True) * (0.5 / D), l_ref.shape)

    row = pl.BlockSpec((tm, D), lambda i: (i, 0))
    return _pc(body, grid=(T // tm,), in_specs=[row, row],
               out_specs=[row, pl.BlockSpec((8, 128), lambda i: (0, 0))],
               out_shape=[SDS((T, D), F32), SDS((8, 128), F32)], name=name)(y, target)


def _conv_geom(view, C, S):
    arr, off = view
    T = arr.shape[0]
    tm = _row_tile(S)
    tc = _tile(C, 512)
    assert off % tc == 0 and C % tc == 0
    return arr, off // tc, T, tm, tc, S // tm


def _prev_spec(tm, tc, ob, order):
    if order == "ij":
        return pl.BlockSpec((8, tc), lambda i, j: (jnp.maximum(i * (tm // 8) - 1, 0), ob + j))
    return pl.BlockSpec((8, tc), lambda j, i: (jnp.maximum(i * (tm // 8) - 1, 0), ob + j))


def _next_spec(T, tm, tc, ob, order):
    last = T // 8 - 1
    if order == "ij":
        return pl.BlockSpec((8, tc), lambda i, j: (jnp.minimum((i + 1) * (tm // 8), last), ob + j))
    return pl.BlockSpec((8, tc), lambda j, i: (jnp.minimum((i + 1) * (tm // 8), last), ob + j))


def _taps(win, w_ref, K, lead, rows):
    acc = win[lead:lead + rows] * w_ref[K - 1:K, :]
    for j in range(1, K):
        acc = acc + win[lead - j:lead - j + rows] * w_ref[K - 1 - j:K - j, :]
    return acc


def _taps_t(win, w_ref, K, rows):
    acc = win[0:rows] * w_ref[K - 1:K, :]
    for j in range(1, K):
        acc = acc + win[j:j + rows] * w_ref[K - 1 - j:K - j, :]
    return acc


def _conv_fwd(view, C, w8, b, K, S, name):
    arr, ob, T, tm, tc, tps = _conv_geom(view, C, S)

    def body(u_ref, p_ref, w_ref, b_ref, o_ref, buf):
        first = (pl.program_id(0) % tps) == 0
        buf[0:8, :] = jnp.where(first, 0.0, p_ref[...])
        buf[8:, :] = u_ref[...]

        def strip(r0, carry):
            win = buf[pl.ds(r0, STRIP + 8), :]
            o_ref[pl.ds(r0, STRIP), :] = _taps(win, w_ref, K, 8, STRIP) + b_ref[...]
            return carry

        _strips(tm, strip)

    return _pc(body, grid=(T // tm, C // tc),
               in_specs=[pl.BlockSpec((tm, tc), lambda i, j: (i, ob + j)), _prev_spec(tm, tc, ob, "ij"),
                         pl.BlockSpec((8, tc), lambda i, j: (0, j)), pl.BlockSpec((1, tc), lambda i, j: (0, j))],
               out_specs=pl.BlockSpec((tm, tc), lambda i, j: (i, j)), out_shape=SDS((T, C), F32), name=name,
               scratch=(pltpu.VMEM((tm + 8, tc), F32),))(arr, arr, w8, b)


def _conv_bwd_in(dview, C, w8, K, S, out_dtype, name):
    arr, ob, T, tm, tc, tps = _conv_geom(dview, C, S)

    def body(d_ref, n_ref, w_ref, o_ref, buf):
        last = (pl.program_id(0) % tps) == tps - 1
        buf[0:tm, :] = d_ref[...]
        buf[tm:tm + 8, :] = jnp.where(last, 0.0, n_ref[...])

        def strip(r0, carry):
            win = buf[pl.ds(r0, STRIP + 8), :]
            o_ref[pl.ds(r0, STRIP), :] = _taps_t(win, w_ref, K, STRIP).astype(o_ref.dtype)
            return carry

        _strips(tm, strip)

    return _pc(body, grid=(T // tm, C // tc),
               in_specs=[pl.BlockSpec((tm, tc), lambda i, j: (i, ob + j)), _next_spec(T, tm, tc, ob, "ij"),
                         pl.BlockSpec((8, tc), lambda i, j: (0, j))],
               out_specs=pl.BlockSpec((tm, tc), lambda i, j: (i, j)), out_shape=SDS((T, C), out_dtype), name=name,
               scratch=(pltpu.VMEM((tm + 8, tc), F32),))(arr, arr, w8)


def _conv_bwd_w(dview, uview, C, K, S, name):
    darr, dob, T, tm, tc, tps = _conv_geom(dview, C, S)
    uarr, uob, _, _, _, _ = _conv_geom(uview, C, S)

    def body(d_ref, u_ref, p_ref, o_ref, buf):
        i = pl.program_id(1)

        @pl.when(i == 0)
        def _():
            o_ref[...] = jnp.zeros_like(o_ref)

        first = (i % tps) == 0
        buf[0:8, :] = jnp.where(first, 0.0, p_ref[...])
        buf[8:, :] = u_ref[...]

        def strip(r0, carry):
            win = buf[pl.ds(r0, STRIP + 8), :]
            d = d_ref[pl.ds(r0, STRIP), :]
            sums = [_fold8(d * win[8 - (K - 1 - k):8 - (K - 1 - k) + STRIP]) for k in range(K)] + [_fold8(d)]
            return tuple(c + s for c, s in zip(carry, sums))

        acc = _strips(tm, strip, tuple(jnp.zeros((8, tc), F32) for _ in range(K + 1)))
        o_ref[...] += _rows8([jnp.sum(a, axis=0, keepdims=True) for a in acc])

    return _pc(body, grid=(C // tc, T // tm),
               in_specs=[pl.BlockSpec((tm, tc), lambda j, i: (i, dob + j)),
                         pl.BlockSpec((tm, tc), lambda j, i: (i, uob + j)), _prev_spec(tm, tc, uob, "ji")],
               out_specs=pl.BlockSpec((8, tc), lambda j, i: (0, j)), out_shape=SDS((8, C), F32), name=name,
               scratch=(pltpu.VMEM((tm + 8, tc), F32),))(darr, uarr, uarr)


def _ffn_act_fwd(uu, w8, b, S, name):
    K, gw = FFN_CONV_K, GLU_W
    T, F2 = uu.shape
    tm, tc = _row_tile(S), 2 * GLU_W
    tps = S // tm

    def body(u_ref, p_ref, w_ref, b_ref, a_ref, buf):
        first = (pl.program_id(0) % tps) == 0
        buf[0:8, :] = jnp.where(first, 0.0, p_ref[...])
        buf[8:, :] = u_ref[...]

        def strip(r0, carry):
            u = _taps(buf[pl.ds(r0, STRIP + 8), :], w_ref, K, 8, STRIP) + b_ref[...]
            a_ref[pl.ds(r0, STRIP), :] = (_silu(u[:, :gw]) * u[:, gw:]).astype(BF16)
            return carry

        _strips(tm, strip)

    return _pc(body, grid=(T // tm, F2 // tc),
               in_specs=[pl.BlockSpec((tm, tc), lambda i, j: (i, j)), _prev_spec(tm, tc, 0, "ij"),
                         pl.BlockSpec((8, tc), lambda i, j: (0, j)), pl.BlockSpec((1, tc), lambda i, j: (0, j))],
               out_specs=pl.BlockSpec((tm, gw), lambda i, j: (i, j)), out_shape=SDS((T, F2 // 2), BF16), name=name,
               scratch=(pltpu.VMEM((tm + 8, tc), F32),))(uu, uu, w8, b)


def _ffn_act_bwd(uu, da, w8, b, S, name, side=None):
    K, gw = FFN_CONV_K, GLU_W
    T, F2 = uu.shape
    tm, tc = _row_tile(S), 2 * GLU_W
    tps = S // tm
    last16 = T // 16 - 1

    def body(u_ref, p_ref, n_ref, da_ref, dan_ref, w_ref, b_ref, duu_ref, cw_ref, ubuf, dabuf):
        i = pl.program_id(1)

        @pl.when(i == 0)
        def _():
            cw_ref[...] = jnp.zeros_like(cw_ref)

        first = (i % tps) == 0
        last = (i % tps) == tps - 1
        ubuf[0:8, :] = jnp.where(first, 0.0, p_ref[...])
        ubuf[8:tm + 8, :] = u_ref[...]
        ubuf[tm + 8:tm + 16, :] = n_ref[...]
        dabuf[0:tm, :] = da_ref[...].astype(F32)
        dabuf[tm:tm + 8, :] = jnp.where(last, 0.0, dan_ref[...].astype(F32)[0:8, :])

        def strip(r0, carry):
            ext = FFN_STRIP + 8
            win = ubuf[pl.ds(r0, FFN_STRIP + 16), :]
            shifted = [win[8 - j:8 - j + ext] for j in range(K)]
            u = b_ref[...] + shifted[0] * w_ref[K - 1:K, :]
            for j in range(1, K):
                u = u + shifted[j] * w_ref[K - 1 - j:K - j, :]
            da_ = dabuf[pl.ds(r0, ext), :]
            g, v = u[:, :gw], u[:, gw:]
            du = jnp.concatenate([da_ * v * _dsilu(g), da_ * _silu(g)], axis=1)
            duu_ref[pl.ds(r0, FFN_STRIP), :] = _taps_t(du, w_ref, K, FFN_STRIP).astype(BF16)
            dmain = du[0:FFN_STRIP]
            sums = [_fold8(dmain * shifted[K - 1 - k][0:FFN_STRIP]) for k in range(K)] + [_fold8(dmain)]
            return tuple(c + s for c, s in zip(carry, sums))

        acc = _strips(tm, strip, tuple(jnp.zeros((8, tc), F32) for _ in range(K + 1)), rows=FFN_STRIP)
        cw_ref[...] += _rows8([jnp.sum(a, axis=0, keepdims=True) for a in acc])

    return _pc(body, grid=(F2 // tc, T // tm),
               in_specs=[pl.BlockSpec((tm, tc), lambda j, i: (i, j)), _prev_spec(tm, tc, 0, "ji"),
                         _next_spec(T, tm, tc, 0, "ji"), pl.BlockSpec((tm, gw), lambda j, i: (i, j)),
                         pl.BlockSpec((16, gw), lambda j, i: (jnp.minimum((i + 1) * (tm // 16), last16), j)),
                         pl.BlockSpec((8, tc), lambda j, i: (0, j)), pl.BlockSpec((1, tc), lambda j, i: (0, j))],
               out_specs=[pl.BlockSpec((tm, tc), lambda j, i: (i, j)), pl.BlockSpec((8, tc), lambda j, i: (0, j))],
               out_shape=[SDS((T, F2), BF16), SDS((8, F2), F32)], name=name,
               scratch=(pltpu.VMEM((tm + 16, tc), F32), pltpu.VMEM((tm + 8, gw), F32)), side=side)(
                   uu, uu, uu, da, da, w8, b)


def _ssd_common(dtc_raw, dtr_raw, hpc, hpr, L):
    dt_c = _softplus(dtc_raw + hpc[0:1, :])
    a_c = -jnp.exp(hpc[1:2, :])
    dt_r = _softplus(dtr_raw + hpr[:, 0:1])
    a_r = -jnp.exp(hpr[:, 1:2])
    li = lax.broadcasted_iota(jnp.int32, (L, L), 0)
    si = lax.broadcasted_iota(jnp.int32, (L, L), 1)
    low = li >= si
    upp = li <= si
    acs_c = _dotx(low, dt_c * a_c, split="b")
    acs_r = _dotx(dt_r * a_r, upp)
    return dt_c, a_c, acs_c, acs_r, low, upp


def _dotx(a, b, split="a", parts=3, dims=NN):
    val, one = (a, b) if split == "a" else (b, a)
    one = one.astype(BF16)
    acc, rem = None, val
    for i in range(parts):
        piece = rem.astype(BF16)
        t = _dot(piece, one, dims) if split == "a" else _dot(one, piece, dims)
        acc = t if acc is None else acc + t
        if i + 1 < parts:
            rem = rem - piece.astype(F32)
    return acc


def _head_maps(R, P, L):
    RP = R * P
    sel = (lax.broadcasted_iota(jnp.int32, (RP, R), 0) // P == lax.broadcasted_iota(jnp.int32, (RP, R), 1)).astype(F32)
    selt = (lax.broadcasted_iota(jnp.int32, (R, RP), 1) // P == lax.broadcasted_iota(jnp.int32, (R, RP), 0)).astype(F32)
    colb = (lax.broadcasted_iota(jnp.int32, (R, R * L), 1) // L == lax.broadcasted_iota(jnp.int32, (R, R * L), 0)).astype(F32)
    return sel, selt, colb


def _pair_diag(mats, rhs_b, R, P):
    lanes = 2 * P
    lo = lax.broadcasted_iota(jnp.int32, (mats[0].shape[0], lanes), 1) < P
    out = []
    for q in range(R // 2):
        rp = rhs_b[:, q * lanes:(q + 1) * lanes]
        out.append(jnp.where(lo, _dot(mats[2 * q], rp), _dot(mats[2 * q + 1], rp)))
    return jnp.concatenate(out, axis=1) if len(out) > 1 else out[0]


def _ssd_specs(pre, off_x, off_b, off_c, G, R, P, nb, nc, rev):
    L, N, RP = CHUNK, N_STATE, R * P
    cidx = (lambda c: nc - 1 - c) if rev else (lambda c: c)
    xb, bb, cb = off_x // RP, off_b // N, off_c // N
    assert off_x % RP == 0 and off_b % N == 0 and off_c % N == 0
    row = lambda b, c: b * nc + cidx(c)
    return dict(
        x=pl.BlockSpec((L, RP), lambda g, b, c: (row(b, c), xb + g)),
        b=pl.BlockSpec((L, N), lambda g, b, c: (row(b, c), bb + g)),
        c=pl.BlockSpec((L, N), lambda g, b, c: (row(b, c), cb + g)),
        dtc=pl.BlockSpec((None, L, R), lambda g, b, c: (g, row(b, c), 0)),
        dtr=pl.BlockSpec((None, R, L), lambda g, b, c: (g, 0, row(b, c))),
        hpc=pl.BlockSpec((None, 8, R), lambda g, b, c: (g, 0, 0)),
        hpr=pl.BlockSpec((None, R, 8), lambda g, b, c: (g, 0, 0)),
        y=pl.BlockSpec((L, RP), lambda g, b, c: (row(b, c), g)),
        bc=pl.BlockSpec((L, N), lambda g, b, c: (row(b, c), g)),
        hs=pl.BlockSpec((None, None, N, RP), lambda g, b, c: (row(b, c), g, 0, 0)),
    )


def _ssd_fwd(pre, offs, dtc, dtr, hpc, hpr, G, R, P, S, name, side=None):
    T = pre.shape[0]
    L, N, RP = CHUNK, N_STATE, R * P
    nc, nb = S // L, T // S
    sp = _ssd_specs(pre, *offs, G, R, P, nb, nc, False)

    def body(px_ref, pb_ref, pc_ref, dtc_ref, dtr_ref, hpc_ref, hpr_ref, y_ref, hs_ref, hst):
        @pl.when(pl.program_id(2) == 0)
        def _():
            hst[...] = jnp.zeros_like(hst)

        xs, bm, cm = _silu(px_ref[...]), _silu(pb_ref[...]), _silu(pc_ref[...])
        hpc_ = hpc_ref[...]
        dt_c, _, acs_c, acs_r, low, _ = _ssd_common(dtc_ref[...], dtr_ref[...], hpc_, hpr_ref[...], L)
        _, selt, colb = _head_maps(R, P, L)
        dt_e, a_e, hp_e = _dotx(dt_c, selt), _dotx(acs_c, selt), _dotx(hpc_, selt)
        a_bc = _dotx(acs_c, colb)
        a_last = a_e[L - 1:L, :]
        bb, cb = bm.astype(BF16), cm.astype(BF16)
        gm = _dot(cb, bb, NT)
        hprev = hst[...]
        hprev_b = hprev.astype(BF16)
        hs_ref[...] = hprev_b
        xdt = xs * dt_e
        xdt_b = xdt.astype(BF16)
        ms = []
        for r in range(R):
            dec = jnp.exp(jnp.where(low, a_bc[:, r * L:(r + 1) * L] - acs_r[r:r + 1, :], -jnp.inf))
            ms.append((gm * dec).astype(BF16))
        y = _pair_diag(ms, xdt_b, R, P) + _dot(cb, hprev_b) * jnp.exp(a_e) + hp_e[2:3, :] * xs
        y_ref[...] = y
        xw = (xdt * jnp.exp(a_last - a_e)).astype(BF16)
        hst[...] = hprev * jnp.exp(a_last) + _dot(bb, xw, TN)

    return _pc(body, grid=(G, nb, nc),
               in_specs=[sp["x"], sp["b"], sp["c"], sp["dtc"], sp["dtr"], sp["hpc"], sp["hpr"]],
               out_specs=[sp["y"], sp["hs"]],
               out_shape=[SDS((T, G * RP), F32), SDS((nb * nc, G, N, RP), BF16)], name=name,
               scratch=(pltpu.VMEM((N, RP), F32),), side=side)(pre, pre, pre, dtc, dtr, hpc, hpr)


def _ssd_bwd(pre, offs, dtc, dtr, hpc, hpr, hs, dy, G, R, P, S, name, side=None):
    T = pre.shape[0]
    L, N, RP = CHUNK, N_STATE, R * P
    nc, nb = S // L, T // S
    sp = _ssd_specs(pre, *offs, G, R, P, nb, nc, True)

    def body(px_ref, pb_ref, pc_ref, dtc_ref, dtr_ref, hpc_ref, hpr_ref, hs_ref, dy_ref,
             dpx_ref, dpb_ref, dpc_ref, ddt_ref, hpg_ref, dhst):
        bi, ci = pl.program_id(1), pl.program_id(2)

        @pl.when(ci == 0)
        def _():
            dhst[...] = jnp.zeros_like(dhst)

        @pl.when((bi == 0) & (ci == 0))
        def _():
            hpg_ref[...] = jnp.zeros_like(hpg_ref)

        px, pb, pcc = px_ref[...], pb_ref[...], pc_ref[...]
        xs, bm, cm = _silu(px), _silu(pb), _silu(pcc)
        hpc_ = hpc_ref[...]
        dtc_raw = dtc_ref[...]
        dt_c, a_c, acs_c, acs_r, low, upp = _ssd_common(dtc_raw, dtr_ref[...], hpc_, hpr_ref[...], L)
        sel, selt, colb = _head_maps(R, P, L)
        dt_e, a_e, hp_e = _dotx(dt_c, selt), _dotx(acs_c, selt), _dotx(hpc_, selt)
        a_bc = _dotx(acs_c, colb)
        a_last = a_e[L - 1:L, :]
        e_e, w_e = jnp.exp(a_e), jnp.exp(a_last - a_e)
        bb, cb = bm.astype(BF16), cm.astype(BF16)
        gm = _dot(cb, bb, NT)
        gmt = _dot(bb, cb, NT)
        hprev = hs_ref[...]
        dhn = dhst[...]
        dhn_b = dhn.astype(BF16)
        dy = dy_ref[...]
        dy_b = dy.astype(BF16)
        xdt = xs * dt_e
        xdt_b = xdt.astype(BF16)
        yoff = _dot(cb, hprev) * e_e
        dye_b = (dy * e_e).astype(BF16)
        dcm = _dot(dye_b, hprev, NT)
        dhst[...] = _dot(cb, dye_b, TN) + jnp.exp(a_last) * dhn
        dxdt_st = _dot(bb, dhn_b) * w_e
        dbm = _dot((xdt * w_e).astype(BF16), dhn_b, NT)
        lanes = 2 * P
        lo = lax.broadcasted_iota(jnp.int32, (L, lanes), 1) < P
        dg = jnp.zeros((L, L), F32)
        es, css = [], []
        for r in range(R):
            col_b, row = a_bc[:, r * L:(r + 1) * L], acs_r[r:r + 1, :]
            dec = jnp.exp(jnp.where(low, col_b - row, -jnp.inf))
            q = r // 2
            dyp = dy_b[:, q * lanes:(q + 1) * lanes]
            dyp = jnp.where(lo if r % 2 == 0 else ~lo, dyp, jnp.zeros_like(dyp))
            dm = _dot(dyp, xdt_b[:, q * lanes:(q + 1) * lanes], NT)
            dg = dg + dm * dec
            e = dm * (gm * dec)
            es.append(e)
            css.append(jnp.sum(e, axis=0, keepdims=True))
        dgb = dg.astype(BF16)
        dcm = dcm + _dot(dgb, bb)
        dbm = dbm + _dot(dgb, cb, TN)
        colbt = (lax.broadcasted_iota(jnp.int32, (R * L, R), 0) // L
                 == lax.broadcasted_iota(jnp.int32, (R * L, R), 1)).astype(F32)
        eye = (lax.broadcasted_iota(jnp.int32, (R, R), 0) == lax.broadcasted_iota(jnp.int32, (R, R), 1)).astype(F32)
        row_sums = _dotx(jnp.concatenate(es, axis=1), colbt)
        col_sums = _dotx(jnp.concatenate(css, axis=0), eye, dims=TN)
        mts = []
        for r in range(R):
            dect = jnp.exp(jnp.where(upp, acs_r[r:r + 1, :] - a_bc[:, r * L:(r + 1) * L], -jnp.inf))
            mts.append((gmt * dect).astype(BF16))
        dxdt = _pair_diag(mts, dy_b, R, P) + dxdt_st
        q_st = _dotx(xdt * dxdt_st, sel, parts=1)
        da = row_sums - col_sums + _dotx(dy * yoff, sel, parts=1) - q_st
        hh = jnp.sum(_dotx(dhn * hprev.astype(F32), sel, parts=1), axis=0, keepdims=True)
        da_last = jnp.exp(acs_c[L - 1:L, :]) * hh + jnp.sum(q_st, axis=0, keepdims=True)
        rowi = lax.broadcasted_iota(jnp.int32, (L, R), 0)
        da = da + jnp.where(rowi == L - 1, da_last, 0.0)
        dpx_ref[...] = (dxdt * dt_e + hp_e[2:3, :] * dy) * _dsilu(px)
        dpb_ref[...] = dbm * _dsilu(pb)
        dpc_ref[...] = dcm * _dsilu(pcc)
        dadt = _dotx(upp, da, split="b")
        ddt = _dotx(dxdt * xs, sel, parts=1) + dadt * a_c
        ddt_raw = ddt * jax.nn.sigmoid(dtc_raw + hpc_[0:1, :])
        ddt_ref[...] = ddt_raw
        d_a = jnp.sum(dadt * dt_c, axis=0, keepdims=True)
        d_d = jnp.sum(_dotx(dy * xs, sel, parts=1), axis=0, keepdims=True)
        rows = [jnp.sum(ddt_raw, axis=0, keepdims=True), d_a * a_c, d_d, jnp.zeros((5, R), F32)]
        hpg_ref[...] += jnp.concatenate(rows, axis=0)

    return _pc(body, grid=(G, nb, nc),
               in_specs=[sp["x"], sp["b"], sp["c"], sp["dtc"], sp["dtr"], sp["hpc"], sp["hpr"], sp["hs"], sp["y"]],
               out_specs=[sp["y"], sp["bc"], sp["bc"], sp["dtc"], pl.BlockSpec((None, 8, R), lambda g, b, c: (g, 0, 0))],
               out_shape=[SDS((T, G * RP), F32), SDS((T, G * N), F32), SDS((T, G * N), F32), SDS((G, T, R), F32),
                          SDS((G, 8, R), F32)], name=name,
               scratch=(pltpu.VMEM((N, RP), F32),), side=side)(pre, pre, pre, dtc, dtr, hpc, hpr, hs, dy)


def _gate_norm_fwd(y, zview, ng, G, S, name):
    T, DI = y.shape
    zarr, zoff = zview
    gw = DI // G
    tm = _row_tile(S)
    zb = zoff // gw
    assert zoff % gw == 0

    def body(y_ref, z_ref, g_ref, o_ref):
        yg = y_ref[...] * _silu(z_ref[...])
        r = lax.rsqrt(jnp.mean(yg * yg, axis=-1, keepdims=True) + EPS)
        o_ref[...] = (yg * r * g_ref[...]).astype(BF16)

    return _pc(body, grid=(T // tm, G),
               in_specs=[pl.BlockSpec((tm, gw), lambda i, g: (i, g)), pl.BlockSpec((tm, gw), lambda i, g: (i, zb + g)),
                         pl.BlockSpec((1, gw), lambda i, g: (0, g))],
               out_specs=pl.BlockSpec((tm, gw), lambda i, g: (i, g)), out_shape=SDS((T, DI), BF16), name=name)(y, zarr, ng)


def _gate_norm_bwd(y, zview, ng, dyn, G, S, name):
    T, DI = y.shape
    zarr, zoff = zview
    gw = DI // G
    tm = _row_tile(S)
    zb = zoff // gw

    def body(y_ref, z_ref, g_ref, d_ref, dy_ref, dz_ref, dg_ref):
        @pl.when(pl.program_id(1) == 0)
        def _():
            dg_ref[...] = jnp.zeros_like(dg_ref)

        y_, z, d = y_ref[...], z_ref[...], d_ref[...]
        sz = _silu(z)
        yg = y_ * sz
        r = lax.rsqrt(jnp.mean(yg * yg, axis=-1, keepdims=True) + EPS)
        n = yg * r
        dn = d * g_ref[...]
        dyg = r * (dn - n * jnp.mean(dn * n, axis=-1, keepdims=True))
        dy_ref[...] = dyg * sz
        dz_ref[...] = (dyg * y_ * _dsilu(z)).astype(BF16)
        dg_ref[...] += _bsum(jnp.sum(d * n, axis=0, keepdims=True))

    return _pc(body, grid=(G, T // tm),
               in_specs=[pl.BlockSpec((tm, gw), lambda g, i: (i, g)), pl.BlockSpec((tm, gw), lambda g, i: (i, zb + g)),
                         pl.BlockSpec((1, gw), lambda g, i: (0, g)), pl.BlockSpec((tm, gw), lambda g, i: (i, g))],
               out_specs=[pl.BlockSpec((tm, gw), lambda g, i: (i, g)), pl.BlockSpec((tm, gw), lambda g, i: (i, g)),
                          pl.BlockSpec((8, gw), lambda g, i: (0, g))],
               out_shape=[SDS((T, DI), F32), SDS((T, DI), BF16), SDS((8, DI), F32)], name=name)(y, zarr, ng, dyn)


def _shortconv_fwd(proj, off_b, off_c, off_h, C, w8, S, name):
    K = SC_CONV_K
    _, ob, T, tm, tc, tps = _conv_geom((proj, off_b), C, S)
    oc, oh = off_c // tc, off_h // tc

    def body(b_ref, c_ref, h_ref, cp_ref, hp_ref, w_ref, o_ref, buf):
        first = (pl.program_id(0) % tps) == 0
        buf[0:8, :] = jnp.where(first, 0.0, cp_ref[...] * hp_ref[...])
        buf[8:, :] = c_ref[...] * h_ref[...]

        def strip(r0, carry):
            conv = _taps(buf[pl.ds(r0, STRIP + 8), :], w_ref, K, 8, STRIP)
            o_ref[pl.ds(r0, STRIP), :] = (b_ref[pl.ds(r0, STRIP), :] * conv).astype(BF16)
            return carry

        _strips(tm, strip)

    blk = lambda o: pl.BlockSpec((tm, tc), lambda i, j: (i, o + j))
    return _pc(body, grid=(T // tm, C // tc),
               in_specs=[blk(ob), blk(oc), blk(oh), _prev_spec(tm, tc, oc, "ij"), _prev_spec(tm, tc, oh, "ij"),
                         pl.BlockSpec((8, tc), lambda i, j: (0, j))],
               out_specs=pl.BlockSpec((tm, tc), lambda i, j: (i, j)), out_shape=SDS((T, C), BF16), name=name,
               scratch=(pltpu.VMEM((tm + 8, tc), F32),))(proj, proj, proj, proj, proj, w8)


def _shortconv_bwd(proj, off_b, off_c, off_h, C, w8, dsc, S, name):
    K = SC_CONV_K
    _, ob, T, tm, tc, tps = _conv_geom((proj, off_b), C, S)
    oc, oh = off_c // tc, off_h // tc

    def body(b_ref, c_ref, h_ref, cp_ref, hp_ref, bn_ref, d_ref, dn_ref, w_ref,
             db_ref, dc_ref, dh_ref, dw_ref, buf, buf2):
        i = pl.program_id(1)

        @pl.when(i == 0)
        def _():
            dw_ref[...] = jnp.zeros_like(dw_ref)

        first = (i % tps) == 0
        last = (i % tps) == tps - 1
        buf[0:8, :] = jnp.where(first, 0.0, cp_ref[...] * hp_ref[...])
        buf[8:, :] = c_ref[...] * h_ref[...]
        buf2[0:tm, :] = d_ref[...] * b_ref[...]
        buf2[tm:tm + 8, :] = jnp.where(last, 0.0, dn_ref[...] * bn_ref[...])

        def strip(r0, carry):
            rows = pl.ds(r0, STRIP)
            vwin = buf[pl.ds(r0, STRIP + 8), :]
            vs = [vwin[8 - j:8 - j + STRIP] for j in range(K)]
            conv = vs[0] * w_ref[K - 1:K, :]
            for j in range(1, K):
                conv = conv + vs[j] * w_ref[K - 1 - j:K - j, :]
            db_ref[rows, :] = (d_ref[rows, :] * conv).astype(BF16)
            dwin = buf2[pl.ds(r0, STRIP + 8), :]
            dv = _taps_t(dwin, w_ref, K, STRIP)
            dc_ref[rows, :] = (dv * h_ref[rows, :]).astype(BF16)
            dh_ref[rows, :] = (dv * c_ref[rows, :]).astype(BF16)
            dconv = dwin[0:STRIP]
            sums = [_fold8(dconv * vs[K - 1 - k]) for k in range(K)]
            return tuple(c + s for c, s in zip(carry, sums))

        acc = _strips(tm, strip, tuple(jnp.zeros((8, tc), F32) for _ in range(K)))
        dw_ref[...] += _rows8([jnp.sum(a, axis=0, keepdims=True) for a in acc])

    blk = lambda o: pl.BlockSpec((tm, tc), lambda j, i: (i, o + j))
    out = pl.BlockSpec((tm, tc), lambda j, i: (i, j))
    return _pc(body, grid=(C // tc, T // tm),
               in_specs=[blk(ob), blk(oc), blk(oh), _prev_spec(tm, tc, oc, "ji"), _prev_spec(tm, tc, oh, "ji"),
                         _next_spec(T, tm, tc, ob, "ji"), blk(0), _next_spec(T, tm, tc, 0, "ji"),
                         pl.BlockSpec((8, tc), lambda j, i: (0, j))],
               out_specs=[out, out, out, pl.BlockSpec((8, tc), lambda j, i: (0, j))],
               out_shape=[SDS((T, C), BF16)] * 3 + [SDS((8, C), F32)], name=name,
               scratch=(pltpu.VMEM((tm + 8, tc), F32), pltpu.VMEM((tm + 8, tc), F32)))(
                   proj, proj, proj, proj, proj, proj, dsc, dsc, w8)


def _merge_fwd(proj, off_g1, off_g2, y1, y2, S, name):
    T, D = y1.shape
    tm = _row_tile(S)
    o1, o2 = off_g1 // D, off_g2 // D
    assert off_g1 % D == 0 and off_g2 % D == 0

    def body(g1_ref, g2_ref, y1_ref, y2_ref, o_ref):
        def strip(r0, carry):
            rows = pl.ds(r0, STRIP)
            o_ref[rows, :] = (jax.nn.sigmoid(g1_ref[rows, :]) * y1_ref[rows, :]
                              + jax.nn.sigmoid(g2_ref[rows, :]) * y2_ref[rows, :]).astype(BF16)
            return carry

        _strips(tm, strip)

    row = pl.BlockSpec((tm, D), lambda i: (i, 0))
    return _pc(body, grid=(T // tm,),
               in_specs=[pl.BlockSpec((tm, D), lambda i: (i, o1)), pl.BlockSpec((tm, D), lambda i: (i, o2)), row, row],
               out_specs=row, out_shape=SDS((T, D), BF16), name=name)(proj, proj, y1, y2)


def _merge_bwd(proj, off_g1, off_g2, y1, y2, dm, S, name):
    T, D = y1.shape
    tm = _row_tile(S)
    o1, o2 = off_g1 // D, off_g2 // D

    def body(g1_ref, g2_ref, y1_ref, y2_ref, d_ref, dy1_ref, dy2_ref, dg1_ref, dg2_ref):
        def strip(r0, carry):
            rows = pl.ds(r0, STRIP)
            d = d_ref[rows, :]
            s1, s2 = jax.nn.sigmoid(g1_ref[rows, :]), jax.nn.sigmoid(g2_ref[rows, :])
            dy1_ref[rows, :] = (d * s1).astype(BF16)
            dy2_ref[rows, :] = (d * s2).astype(BF16)
            dg1_ref[rows, :] = (d * y1_ref[rows, :] * s1 * (1.0 - s1)).astype(BF16)
            dg2_ref[rows, :] = (d * y2_ref[rows, :] * s2 * (1.0 - s2)).astype(BF16)
            return carry

        _strips(tm, strip)

    row = pl.BlockSpec((tm, D), lambda i: (i, 0))
    return _pc(body, grid=(T // tm,),
               in_specs=[pl.BlockSpec((tm, D), lambda i: (i, o1)), pl.BlockSpec((tm, D), lambda i: (i, o2)), row, row, row],
               out_specs=[row] * 4, out_shape=[SDS((T, D), BF16)] * 4, name=name)(proj, proj, y1, y2, dm)


def _pad8(w):
    return jnp.pad(w, ((0, 8 - w.shape[0]), (0, 0)))


def _dims(w):
    D = w["mix_pre_g"].shape[-1]
    DI = w["ssd_norm_g"].shape[-1]
    H = w["ssd_dt_bias"].shape[-1]
    conv_dim = w["ssd_conv_b"].shape[-1]
    G = (conv_dim - DI) // (2 * N_STATE)
    F = w["w_down"].shape[0]
    return dict(D=D, DI=DI, H=H, P=DI // H, G=G, R=H // G, GN=G * N_STATE, CD=conv_dim, F=F)


def _proj_layout(d):
    D, DI, CD, H = d["D"], d["DI"], d["CD"], d["H"]
    o = dict(z=0, xbc=DI, scb=DI + CD, scc=DI + CD + D, sch=DI + CD + 2 * D, g1=DI + CD + 3 * D, g2=DI + CD + 4 * D,
             dt=DI + CD + 5 * D)
    o["sb"] = math.gcd(SEG_BLK, D, DI, d["GN"])
    assert o["sb"] % 128 == 0 and H <= o["sb"]
    o["np"] = o["dt"] + o["sb"]
    return o


def _glu_perm(a, F, inverse=False):
    lead = a.shape[:-1]
    nb = F // GLU_W
    if not inverse:
        return a.reshape(*lead, 2, nb, GLU_W).swapaxes(-3, -2).reshape(*lead, 2 * F)
    return a.reshape(*lead, nb, 2, GLU_W).swapaxes(-3, -2).reshape(*lead, 2 * F)


def _glu_perm_rows(a, F, inverse=False):
    nb, D = F // GLU_W, a.shape[1]
    shape = (nb, 2, GLU_W, D) if inverse else (2, nb, GLU_W, D)
    return a.reshape(shape).swapaxes(0, 1).reshape(2 * F, D)


def _prep_layer(w):
    d = _dims(w)
    D, DI, CD, H, G, R, F = d["D"], d["DI"], d["CD"], d["H"], d["G"], d["R"], d["F"]
    lay = _proj_layout(d)
    w_in = w["w_in"]
    used = lay["dt"] + H
    wcat = jnp.concatenate([w_in[:DI + CD], w_in[DI + CD + H:], w_in[DI + CD:DI + CD + H],
                            jnp.zeros((lay["np"] - used, D), w_in.dtype)], axis=0)
    hp = jnp.stack([w["ssd_dt_bias"], w["ssd_a_log"], w["ssd_d"]], 0).astype(F32)
    hpc = jnp.pad(hp.reshape(3, G, R).transpose(1, 0, 2), ((0, 0), (0, 5), (0, 0)))
    hpr = jnp.pad(hp[:2].reshape(2, G, R).transpose(1, 2, 0), ((0, 0), (0, 0), (0, 6)))
    row = lambda v: v.reshape(1, -1).astype(F32)
    return dict(
        d=d, lay=lay, ada_w=w["ada_w"].astype(BF16), ada_b=row(w["ada_b"]),
        mix_pre_g=row(w["mix_pre_g"]), mix_post_g=row(w["mix_post_g"]), wcat=wcat.astype(BF16),
        ssd_conv_w=_pad8(w["ssd_conv_w"].astype(F32)), ssd_conv_b=row(w["ssd_conv_b"]), hpc=hpc, hpr=hpr,
        ssd_norm_g=row(w["ssd_norm_g"]), w_ssd_out=w["w_ssd_out"].astype(BF16),
        sc_conv_w=_pad8(w["sc_conv_w"].astype(F32)), w_sc_out=w["w_sc_out"].astype(BF16), w_o=w["w_o"].astype(BF16),
        ffn_pre_g=row(w["ffn_pre_g"]), ffn_post_g=row(w["ffn_post_g"]),
        w_up=_glu_perm_rows(w["w_up"], F).astype(BF16), ffn_conv_w=_pad8(_glu_perm(w["ffn_conv_w"].astype(F32), F)),
        ffn_conv_b=_glu_perm(row(w["ffn_conv_b"]), F), w_down=w["w_down"].astype(BF16))


def _dt_layouts(proj, lay, d):
    T = proj.shape[0]
    dt = proj[:, lay["dt"]:lay["dt"] + d["H"]].reshape(T, d["G"], d["R"])
    return dt.transpose(1, 0, 2), dt.transpose(1, 2, 0)


def _layer_fwd(x, c8, p, S, li, gather=None):
    d, lay = p["d"], p["lay"]
    D, DI, G, R, P, GN, CD = d["D"], d["DI"], d["G"], d["R"], d["P"], d["GN"], d["CD"]
    nb = x.shape[0] // S
    nm = lambda s: f"l{li}_{s}"
    mod, cact = _modk(c8, p["ada_w"], p["ada_b"], nm("mod"))
    mod3 = mod[:nb].reshape(nb, 1, 6 * D)
    h = _norm_mod(x, p["mix_pre_g"], mod3, 1, 0, S, nm("norm1"))
    proj = _mm(h, p["wcat"], "nt", F32, nm("mm_in"), caps=(1024, 1536, 2048))
    pre = _conv_fwd((proj, lay["xbc"]), CD, p["ssd_conv_w"], p["ssd_conv_b"], SSD_CONV_K, S, nm("ssdconv"))
    dtc, dtr = _dt_layouts(proj, lay, d)
    offs = (0, DI, DI + GN)
    gathered = None
    if gather is None:
        y, hs = _ssd_fwd(pre, offs, dtc, dtr, p["hpc"], p["hpr"], G, R, P, S, nm("ssd"))
    else:
        (y, hs), (gathered,) = _ssd_fwd(pre, offs, dtc, dtr, p["hpc"], p["hpr"], G, R, P, S, nm("ssd"),
                                        side=_side_gather_direct(gather))
    yn = _gate_norm_fwd(y, (proj, lay["z"]), p["ssd_norm_g"], G, S, nm("gnorm"))
    sc = _shortconv_fwd(proj, lay["scb"], lay["scc"], lay["sch"], D, p["sc_conv_w"], S, nm("sconv"))
    if gather is None:
        y_ssd = _mm(yn, p["w_ssd_out"], "nn", F32, nm("mm_ssdout"))
    else:
        y_ssd, (gathered,) = _mm(yn, p["w_ssd_out"], "nn", F32, nm("mm_ssdout"), side=_side_gather_forward(gathered))
    y_sc = _mm(sc, p["w_sc_out"], "nn", F32, nm("mm_scout"))
    m = _merge_fwd(proj, lay["g1"], lay["g2"], y_ssd, y_sc, S, nm("merge"))
    mix = _mm(m, p["w_o"], "nn", F32, nm("mm_o"))
    x1 = _resid_post(x, mix, mod3, 2, p["mix_post_g"], S, nm("post1"))
    h2 = _norm_mod(x1, p["ffn_pre_g"], mod3, 4, 3, S, nm("norm2"))
    uu = _mm(h2, p["w_up"], "nt", F32, nm("mm_up"), caps=(1024, 1408, 2048))
    a = _ffn_act_fwd(uu, p["ffn_conv_w"], p["ffn_conv_b"], S, nm("ffnact"))
    f = _mm(a, p["w_down"], "nn", F32, nm("mm_down"), caps=(1024, 1024, 1408))
    x2 = _resid_post(x1, f, mod3, 5, p["ffn_post_g"], S, nm("post2"))
    saved = dict(x=x, h=h, proj=proj, pre=pre, dtc=dtc, dtr=dtr, y=y, hs=hs, yn=yn, sc=sc, y_ssd=y_ssd, y_sc=y_sc,
                 m=m, mix=mix, x1=x1, h2=h2, uu=uu, a=a, f=f, mod3=mod3, cact=cact)
    return x2, saved, gathered


def _seq_sum(acc, nb):
    return acc.reshape(nb, 8, -1)[:, 0, :]


def _layer_bwd(dx2, p, s, S, li, chip_sums=None, early=None):
    d, lay = p["d"], p["lay"]
    D, DI, G, R, P, GN, CD, H, F = d["D"], d["DI"], d["G"], d["R"], d["P"], d["GN"], d["CD"], d["H"], d["F"]
    nb = dx2.shape[0] // S
    nm = lambda t: f"l{li}_{t}"
    mod3 = s["mod3"]
    g = {}
    exchanged = None
    df, dgt2, dpg2 = _post_bwd(s["f"], mod3, 5, p["ffn_post_g"], dx2, S, nm("post2_b"))
    g["ffn_post_g"] = dpg2[0]
    da = _mm(df, p["w_down"], "nt", BF16, nm("mm_down_bi"), caps=(1024, 1408, 2048))
    g["w_down"] = _mm(s["a"], df, "tn", WGRAD,nm("mm_down_bw"), caps=(1408, 1024, 1024))
    if chip_sums is None:
        duu, cw = _ffn_act_bwd(s["uu"], da, p["ffn_conv_w"], p["ffn_conv_b"], S, nm("ffnact_b"))
    else:
        (duu, cw), (exchanged,) = _ffn_act_bwd(s["uu"], da, p["ffn_conv_w"], p["ffn_conv_b"], S, nm("ffnact_b"),
                                               side=_side_chip_exchange(chip_sums))
    g["ffn_conv_w"] = _glu_perm(cw[:FFN_CONV_K], F, inverse=True)
    g["ffn_conv_b"] = _glu_perm(cw[FFN_CONV_K], F, inverse=True)
    dh2 = _mm(duu, p["w_up"], "nn", F32, nm("mm_up_bi"), caps=(1024, 1024, 2816))
    g["w_up"] = _glu_perm_rows(_mm(duu, s["h2"], "tn", WGRAD,nm("mm_up_bw"), caps=(1408, 1024, 1024)), F, inverse=True)
    dx1, dg2, dsc2, dsh2 = _pre_bwd(s["x1"], p["ffn_pre_g"], mod3, 4, dh2, dx2, S, nm("norm2_b"))
    g["ffn_pre_g"] = dg2[0]
    dmix, dgt1, dpg1 = _post_bwd(s["mix"], mod3, 2, p["mix_post_g"], dx1, S, nm("post1_b"))
    g["mix_post_g"] = dpg1[0]
    dm = _mm(dmix, p["w_o"], "nt", F32, nm("mm_o_bi"))
    g["w_o"] = _mm(s["m"], dmix, "tn", WGRAD,nm("mm_o_bw"))
    proj = s["proj"]
    dy_ssd, dy_sc, dg1, dg2_ = _merge_bwd(proj, lay["g1"], lay["g2"], s["y_ssd"], s["y_sc"], dm, S, nm("merge_b"))
    dyn = _mm(dy_ssd, p["w_ssd_out"], "nt", F32, nm("mm_ssdout_bi"))
    g["w_ssd_out"] = _mm(s["yn"], dy_ssd, "tn", WGRAD,nm("mm_ssdout_bw"))
    dsc = _mm(dy_sc, p["w_sc_out"], "nt", F32, nm("mm_scout_bi"))
    g["w_sc_out"] = _mm(s["sc"], dy_sc, "tn", WGRAD,nm("mm_scout_bw"))
    dscb, dscc, dsch, scw = _shortconv_bwd(proj, lay["scb"], lay["scc"], lay["sch"], D, p["sc_conv_w"], dsc, S, nm("sconv_b"))
    g["sc_conv_w"] = scw[:SC_CONV_K]
    dy, dz, dng = _gate_norm_bwd(s["y"], (proj, lay["z"]), p["ssd_norm_g"], dyn, G, S, nm("gnorm_b"))
    g["ssd_norm_g"] = dng[0]
    offs = (0, DI, DI + GN)
    early_side = None if early is None else _side_chip_exchange(early(g))
    res = _ssd_bwd(s["pre"], offs, s["dtc"], s["dtr"], p["hpc"], p["hpr"], s["hs"], dy, G, R, P, S, nm("ssd_b"),
                   side=early_side)
    (dpx, dpb, dpc, ddt, hpg), early_got = res if early is not None else (res, None)
    g["ssd_dt_bias"], g["ssd_a_log"], g["ssd_d"] = hpg[:, 0, :].reshape(H), hpg[:, 1, :].reshape(H), hpg[:, 2, :].reshape(H)
    cws, dxbc = [], []
    for name, darr, off, C in (("x", dpx, 0, DI), ("b", dpb, DI, GN), ("c", dpc, DI + GN, GN)):
        w8 = p["ssd_conv_w"][:, off:off + C]
        cws.append(_conv_bwd_w((darr, 0), (proj, lay["xbc"] + off), C, SSD_CONV_K, S, nm(f"ssdconv_bw_{name}")))
        dxbc.append(_conv_bwd_in((darr, 0), C, w8, SSD_CONV_K, S, BF16, nm(f"ssdconv_bi_{name}")))
    cws = jnp.concatenate(cws, axis=1)
    g["ssd_conv_w"], g["ssd_conv_b"] = cws[:SSD_CONV_K], cws[SSD_CONV_K]
    T = dx2.shape[0]
    ddt_t = jnp.pad(ddt.transpose(1, 0, 2).reshape(T, H).astype(BF16), ((0, 0), (0, lay["sb"] - H)))
    dproj = [dz] + dxbc + [dscb, dscc, dsch, dg1, dg2_, ddt_t]
    dh = _mm_seg(dproj, p["wcat"], "nn", F32, nm("mm_in_bi"), lay["sb"])
    dwcat = _mm_seg(dproj, s["h"], "tn", WGRAD, nm("mm_in_bw"), lay["sb"], tk=1024)
    o = lay
    g["w_in"] = jnp.concatenate([dwcat[o["z"]:o["scb"]], dwcat[o["dt"]:o["dt"] + H], dwcat[o["scb"]:o["dt"]]], axis=0)
    dx, dg1_, dsc1, dsh1 = _pre_bwd(s["x"], p["mix_pre_g"], mod3, 1, dh, dx1, S, nm("norm1_b"))
    g["mix_pre_g"] = dg1_[0]
    dmod = jnp.concatenate([_seq_sum(t, nb) for t in (dsh1, dsc1, dgt1, dsh2, dsc2, dgt2)], axis=1)
    dmod8 = jnp.pad(dmod, ((0, MOD_ROWS - nb), (0, 0)))
    g["ada_b"] = _colsum(dmod8, nm("adab"))
    g["ada_w"] = _mm(dmod8, s["cact"], "tn", WGRAD, nm("mm_ada_bw"), caps=(1536, 1024, 2048))
    return dx, g, exchanged, None if early_got is None else early_got[0]


def _colsum(a8, name):
    rows, C = a8.shape
    tc = _tile(C, 2048)

    def body(a_ref, o_ref):
        o_ref[...] = _bsum(jnp.sum(a_ref[...], axis=0, keepdims=True))

    return _pc(body, grid=(C // tc,), in_specs=[pl.BlockSpec((rows, tc), lambda j: (0, j))],
               out_specs=pl.BlockSpec((8, tc), lambda j: (0, j)), out_shape=SDS((8, C), F32), name=name)(a8)[0]


def _adam(gs, w, m, v, name):
    ns, R, W = gs.shape
    tr = _tile(R, 256, 8)

    def body(g_ref, w_ref, m_ref, v_ref, go_ref, d_ref, mo_ref, vo_ref):
        g = g_ref[0].astype(F32)
        for k in range(1, ns):
            g = g + g_ref[k].astype(F32)
        go_ref[...] = g
        d_ref[...], mo_ref[...], vo_ref[...] = _adam_update(g, w_ref[...], m_ref[...], v_ref[...])

    row = pl.BlockSpec((tr, W), lambda i: (i, 0))
    return _pc(body, grid=(R // tr,), in_specs=[pl.BlockSpec((ns, tr, W), lambda i: (0, i, 0)), row, row, row],
               out_specs=[row] * 4, out_shape=[SDS((R, W), F32)] * 4, name=name)(gs, w, m, v)


def _adam_update(g, w, m, v):
    c1 = 1.0 / (1.0 - ADAM_B1 ** ADAM_STEP)
    c2 = 1.0 / (1.0 - ADAM_B2 ** ADAM_STEP)
    m_ = ADAM_B1 * m + (1.0 - ADAM_B1) * g
    v_ = ADAM_B2 * v + (1.0 - ADAM_B2) * (g * g)
    return -ADAM_LR * ((m_ * c1) / (jnp.sqrt(v_ * c2) + ADAM_EPS) + ADAM_WD * w), m_, v_


def _adam_nat(g, w, m, v, name):
    depth, a, b = w.shape
    tr = _tile(a, 256, 8)

    def body(g_ref, w_ref, m_ref, v_ref, d_ref, mo_ref, vo_ref):
        d_ref[...], mo_ref[...], vo_ref[...] = _adam_update(g_ref[...], w_ref[...], m_ref[...], v_ref[...])

    blk = pl.BlockSpec((None, tr, b), lambda l, i: (l, i, 0))
    return _pc(body, grid=(depth, a // tr), in_specs=[blk] * 4, out_specs=[blk] * 3,
               out_shape=[SDS(w.shape, F32)] * 3, name=name)(g, w, m, v)


def _adam_mid(g, w, m, v, name):
    b, depth, a = w.shape
    tr = 128

    def body(g_ref, w_ref, m_ref, v_ref, d_ref, mo_ref, vo_ref):
        d_ref[...], mo_ref[...], vo_ref[...] = _adam_update(g_ref[...], w_ref[...], m_ref[...], v_ref[...])

    blk = pl.BlockSpec((tr, depth, a), lambda i: (i, 0, 0))
    return _pc(body, grid=(pl.cdiv(b, tr),), in_specs=[blk] * 4, out_specs=[blk] * 3,
               out_shape=[SDS(w.shape, F32)] * 3, name=name)(g, w, m, v)


def _sum_chips(gs, name):
    ns, R, W = gs.shape
    tr = _tile(R, 256, 16)

    def body(g_ref, o_ref):
        acc = g_ref[0].astype(F32)
        for k in range(1, ns):
            acc = acc + g_ref[k].astype(F32)
        o_ref[...] = acc

    return _pc(body, grid=(R // tr,), in_specs=[pl.BlockSpec((ns, tr, W), lambda i: (0, i, 0))],
               out_specs=pl.BlockSpec((tr, W), lambda i: (i, 0)), out_shape=SDS((R, W), F32), name=name)(gs)


HBM_SPEC = pl.BlockSpec(memory_space=pltpu.HBM)
VMEM_SPEC = pl.BlockSpec(memory_space=pltpu.VMEM)


def _dev():
    return lax.axis_index("x"), lax.axis_index("y"), lax.axis_index("c")


def _allgather_big(loc, name):
    R, W = loc.shape

    def body(x_ref, out_ref, send_sems, recv_sems, local_sem):
        x, y, c = _dev()
        me, sibling = (x, y, c), (x, y, 1 - c)
        chips = [(1 - x, y), (x, 1 - y), (1 - x, 1 - y)]

        def slab(px, py, pc):
            return out_ref.at[4 * px + 2 * py + pc]

        def copy(k, block, to, src=None):
            return pltpu.make_async_remote_copy(
                src_ref=slab(*block) if src is None else src, dst_ref=slab(*block),
                send_sem=send_sems.at[k], recv_sem=recv_sems.at[k], device_id=to, device_id_type=MESH)

        mine = pltpu.make_async_copy(x_ref, slab(*me), local_sem)
        mine.start()
        first = [copy(0, me, sibling, src=x_ref)]
        first += [copy(1 + j, me, (*chip, c), src=x_ref) for j, chip in enumerate(chips)]
        for cp in first:
            cp.start()
        passed = [copy(4 + j, (*chip, c), sibling) for j, chip in enumerate(chips)]
        for j, chip in enumerate(chips):
            copy(1 + j, (*chip, c), me).wait_recv()
            passed[j].start()
        copy(0, sibling, me).wait_recv()
        for j, chip in enumerate(chips):
            copy(4 + j, (*chip, 1 - c), me).wait_recv()
        for cp in first + passed:
            cp.wait_send()
        mine.wait()

    return pl.pallas_call(
        body, out_shape=SDS((N_DEV, R, W), loc.dtype), in_specs=[HBM_SPEC], out_specs=HBM_SPEC,
        scratch_shapes=[pltpu.SemaphoreType.DMA((7,)), pltpu.SemaphoreType.DMA((7,)), pltpu.SemaphoreType.DMA],
        name=name)(loc)


def _dma_sems(n):
    return (pltpu.SemaphoreType.DMA((n,)), pltpu.SemaphoreType.DMA((n,)), pltpu.SemaphoreType.DMA)


def _side_gather_direct(loc):
    R, W = loc.shape

    def copies(ins, outs, sems):
        x, y, c = _dev()
        x_ref, out = ins[0], outs[0]
        me = 4 * x + 2 * y + c
        peers = [(x, y, 1 - c), (1 - x, y, c), (x, 1 - y, c), (1 - x, 1 - y, c)]
        mk = lambda k, p, dst: pltpu.make_async_remote_copy(
            src_ref=x_ref, dst_ref=out.at[dst], send_sem=sems[0].at[k], recv_sem=sems[1].at[k], device_id=p,
            device_id_type=MESH)
        sends = [mk(k, p, me) for k, p in enumerate(peers)]
        recvs = [mk(k, p, 4 * p[0] + 2 * p[1] + p[2]) for k, p in enumerate(peers)]
        return sends, recvs, pltpu.make_async_copy(x_ref, out.at[me], sems[2])

    def start(ins, outs, sems):
        sends, _, mine = copies(ins, outs, sems)
        mine.start()
        for cp in sends:
            cp.start()

    def wait(ins, outs, sems):
        sends, recvs, mine = copies(ins, outs, sems)
        for cp in recvs:
            cp.wait_recv()
        for cp in sends:
            cp.wait_send()
        mine.wait()

    return _Side((loc,), (SDS((N_DEV, R, W), loc.dtype),), _dma_sems(4), start, wait)


def _side_gather_forward(buf):
    def copies(ins, outs, sems):
        x, y, c = _dev()
        out = outs[0]
        chips = [(1 - x, y), (x, 1 - y), (1 - x, 1 - y)]
        mk = lambda k, src, dst: pltpu.make_async_remote_copy(
            src_ref=out.at[src], dst_ref=out.at[dst], send_sem=sems[0].at[k], recv_sem=sems[1].at[k],
            device_id=(x, y, 1 - c), device_id_type=MESH)
        mine = [4 * px + 2 * py + c for px, py in chips]
        theirs = [4 * px + 2 * py + (1 - c) for px, py in chips]
        return [mk(k, s, s) for k, s in enumerate(mine)], [mk(k, s, t) for k, (s, t) in enumerate(zip(mine, theirs))]

    def start(ins, outs, sems):
        for cp in copies(ins, outs, sems)[0]:
            cp.start()

    def wait(ins, outs, sems):
        sends, recvs = copies(ins, outs, sems)
        for cp in recvs:
            cp.wait_recv()
        for cp in sends:
            cp.wait_send()

    return _Side((buf,), (SDS(buf.shape, buf.dtype),), _dma_sems(3)[:2], start, wait, {0: 0})


def _side_chip_exchange(p):
    def copies(ins, outs, sems):
        x, y, c = _dev()
        p_ref, out = ins[0], outs[0]
        j0 = 2 * x + y
        chips = [(1 - x, y), (x, 1 - y), (1 - x, 1 - y)]
        mk = lambda k, chip, src, dst: pltpu.make_async_remote_copy(
            src_ref=p_ref.at[src], dst_ref=out.at[dst], send_sem=sems[0].at[k], recv_sem=sems[1].at[k],
            device_id=(*chip, c), device_id_type=MESH)
        sends = [mk(k, chip, 2 * chip[0] + chip[1], j0) for k, chip in enumerate(chips)]
        recvs = [mk(k, chip, j0, 2 * chip[0] + chip[1]) for k, chip in enumerate(chips)]
        return sends, recvs, pltpu.make_async_copy(p_ref.at[j0], out.at[j0], sems[2])

    def start(ins, outs, sems):
        sends, _, mine = copies(ins, outs, sems)
        mine.start()
        for cp in sends:
            cp.start()

    def wait(ins, outs, sems):
        sends, recvs, mine = copies(ins, outs, sems)
        for cp in recvs:
            cp.wait_recv()
        for cp in sends:
            cp.wait_send()
        mine.wait()

    return _Side((p,), (SDS(p.shape, p.dtype),), _dma_sems(3), start, wait)


def _rs_pair_exchange(g, name):
    nd, R, W = g.shape
    nj = nd // 2

    def body(g_ref, out_ref, send_sems, recv_sems):
        x, y, c = _dev()
        cps = [pltpu.make_async_remote_copy(src_ref=g_ref.at[2 * j + (1 - c)], dst_ref=out_ref.at[j],
                                            send_sem=send_sems.at[j], recv_sem=recv_sems.at[j],
                                            device_id=(x, y, 1 - c), device_id_type=MESH) for j in range(nj)]
        for cp in cps:
            cp.start()
        for cp in cps:
            cp.wait()

    return pl.pallas_call(
        body, out_shape=SDS((nj, R, W), g.dtype), in_specs=[HBM_SPEC], out_specs=HBM_SPEC,
        scratch_shapes=[pltpu.SemaphoreType.DMA((nj,)), pltpu.SemaphoreType.DMA((nj,))], name=name)(g)


def _add_pairs(g, ra, name):
    nd, R, W = g.shape
    nj = nd // 2
    tr = _tile(R, 256, 8)
    cidx = lax.axis_index("c").astype(jnp.int32).reshape(1)

    def body(c_ref, a_ref, b_ref, o_ref):
        o_ref[...] = (a_ref[...].astype(F32) + b_ref[...].astype(F32)).astype(o_ref.dtype)

    gs = pltpu.PrefetchScalarGridSpec(
        num_scalar_prefetch=1, grid=(nj, R // tr),
        in_specs=[pl.BlockSpec((None, tr, W), lambda j, i, cr: (2 * j + cr[0], i, 0)),
                  pl.BlockSpec((None, tr, W), lambda j, i, cr: (j, i, 0))],
        out_specs=pl.BlockSpec((None, tr, W), lambda j, i, cr: (j, i, 0)))
    return pl.pallas_call(body, grid_spec=gs, out_shape=SDS((nj, R, W), g.dtype), name=name,
                          compiler_params=pltpu.CompilerParams(vmem_limit_bytes=VMEM_LIMIT))(cidx, g, ra)


def _rs_chip_exchange(p, name):
    nj, R, W = p.shape

    def body(p_ref, out_ref, send_sems, recv_sems, local_sem):
        x, y, c = _dev()
        j0 = 2 * x + y
        chips = [(1 - x, y), (x, 1 - y), (1 - x, 1 - y)]
        mine = pltpu.make_async_copy(p_ref.at[j0], out_ref.at[j0], local_sem)
        mine.start()

        def copy(k, chip):
            return pltpu.make_async_remote_copy(
                src_ref=p_ref.at[2 * chip[0] + chip[1]], dst_ref=out_ref.at[j0],
                send_sem=send_sems.at[k], recv_sem=recv_sems.at[k], device_id=(*chip, c), device_id_type=MESH)

        sent = [copy(k, chip) for k, chip in enumerate(chips)]
        for cp in sent:
            cp.start()
        for k, chip in enumerate(chips):
            pltpu.make_async_remote_copy(
                src_ref=p_ref.at[j0], dst_ref=out_ref.at[2 * chip[0] + chip[1]],
                send_sem=send_sems.at[k], recv_sem=recv_sems.at[k], device_id=(*chip, c), device_id_type=MESH).wait_recv()
        for cp in sent:
            cp.wait_send()
        mine.wait()

    return pl.pallas_call(
        body, out_shape=SDS((nj, R, W), p.dtype), in_specs=[HBM_SPEC], out_specs=HBM_SPEC,
        scratch_shapes=[pltpu.SemaphoreType.DMA((3,)), pltpu.SemaphoreType.DMA((3,)), pltpu.SemaphoreType.DMA],
        name=name)(p)


def _allgather_small(v, name):
    R, W = v.shape

    def body(v_ref, out_ref, send_sems, recv_sems, local_sem):
        x, y, c = _dev()
        mine = pltpu.make_async_copy(v_ref, out_ref.at[4 * x + 2 * y + c], local_sem)
        mine.start()
        peers = []
        for k in range(1, N_DEV):
            px = 1 - x if k & 4 else x
            py = 1 - y if k & 2 else y
            pc_ = 1 - c if k & 1 else c
            peers.append((px, py, pc_))
        sent = [pltpu.make_async_remote_copy(
            src_ref=v_ref, dst_ref=out_ref.at[4 * x + 2 * y + c], send_sem=send_sems.at[k], recv_sem=recv_sems.at[k],
            device_id=peer, device_id_type=MESH) for k, peer in enumerate(peers)]
        for cp in sent:
            cp.start()
        for k, (px, py, pc_) in enumerate(peers):
            pltpu.make_async_remote_copy(
                src_ref=v_ref, dst_ref=out_ref.at[4 * px + 2 * py + pc_], send_sem=send_sems.at[k],
                recv_sem=recv_sems.at[k], device_id=(px, py, pc_), device_id_type=MESH).wait_recv()
        for cp in sent:
            cp.wait_send()
        mine.wait()

    return pl.pallas_call(
        body, out_shape=SDS((N_DEV, R, W), v.dtype), in_specs=[VMEM_SPEC], out_specs=VMEM_SPEC,
        scratch_shapes=[pltpu.SemaphoreType.DMA((7,)), pltpu.SemaphoreType.DMA((7,)), pltpu.SemaphoreType.DMA],
        name=name)(v)


def _sum_slabs(a, name):
    ns, R, W = a.shape

    def body(a_ref, o_ref):
        acc = a_ref[0]
        for k in range(1, ns):
            acc = acc + a_ref[k]
        o_ref[...] = acc

    return pl.pallas_call(body, out_shape=SDS((R, W), a.dtype), in_specs=[VMEM_SPEC], out_specs=VMEM_SPEC, name=name)(a)


BIG = (("ada_w", "col"), ("w_in", "col"), ("w_ssd_out", "row"), ("w_sc_out", "row"), ("w_o", "row"), ("w_up", "col"),
       ("w_down", "row"))
EARLY = ("w_ssd_out", "w_sc_out", "w_o", "w_up", "w_down")
LATE = ("ada_w", "w_in")
MID_LAYOUT = ("w_in",)
SWAP_LAYOUT = ("w_up",)
CONVW = ("ssd_conv_w", "sc_conv_w", "ffn_conv_w")
REPL = ("ada_b", "mix_pre_g", "mix_post_g", "ssd_conv_b", "ssd_dt_bias", "ssd_a_log", "ssd_d", "ssd_norm_g", "ffn_pre_g",
        "ffn_post_g", "ffn_conv_b")
WEIGHTS = ("ada_w", "ada_b", "mix_pre_g", "mix_post_g", "w_in", "ssd_conv_w", "ssd_conv_b", "ssd_dt_bias", "ssd_a_log",
           "ssd_d", "ssd_norm_g", "w_ssd_out", "sc_conv_w", "w_sc_out", "w_o", "ffn_pre_g", "ffn_post_g", "w_up",
           "ffn_conv_w", "ffn_conv_b", "w_down")


def _pad_rows(a, mult):
    r = a.shape[-2]
    pad = -r % mult
    return a if pad == 0 else jnp.pad(a, [(0, 0)] * (a.ndim - 2) + [(0, pad), (0, 0)])


def _flat_rows(parts, mult):
    flat = jnp.concatenate([p.reshape(-1) for p in parts])
    flat = jnp.pad(flat, (0, -flat.shape[0] % ROW_W))
    return _pad_rows(flat.reshape(-1, ROW_W), mult)


def _unflat(buf, shapes):
    flat = buf.reshape(-1)
    out, o = [], 0
    for shp in shapes:
        n = 1
        for s in shp:
            n *= s
        out.append(flat[o:o + n].reshape(shp))
        o += n
    return out


def _pack_big_local(get, l):
    return [_pad_rows((get(n)[l].T if kind == "col" else get(n)[l]).reshape(-1, ROW_W), SLAB_ALIGN) for n, kind in BIG]


def _big_rows(shapes, names=None):
    out, o = {}, 0
    for n in (names if names is not None else [n for n, _ in BIG]):
        r = shapes[n][1] * shapes[n][2] // ROW_W
        out[n] = (o, o + r)
        o += -(-r // SLAB_ALIGN) * SLAB_ALIGN
    return out, o


def kernel(x, c, ada_w, ada_b, mix_pre_g, mix_post_g, w_in, ssd_conv_w, ssd_conv_b, ssd_dt_bias, ssd_a_log, ssd_d, ssd_norm_g, w_ssd_out, sc_conv_w, w_sc_out, w_o, ffn_pre_g, ffn_post_g, w_up, ffn_conv_w, ffn_conv_b, w_down, loss_target, m_ada_w, m_ada_b, m_mix_pre_g, m_mix_post_g, m_w_in, m_ssd_conv_w, m_ssd_conv_b, m_ssd_dt_bias, m_ssd_a_log, m_ssd_d, m_ssd_norm_g, m_w_ssd_out, m_sc_conv_w, m_w_sc_out, m_w_o, m_ffn_pre_g, m_ffn_post_g, m_w_up, m_ffn_conv_w, m_ffn_conv_b, m_w_down, v_ada_w, v_ada_b, v_mix_pre_g, v_mix_post_g, v_w_in, v_ssd_conv_w, v_ssd_conv_b, v_ssd_dt_bias, v_ssd_a_log, v_ssd_d, v_ssd_norm_g, v_w_ssd_out, v_sc_conv_w, v_w_sc_out, v_w_o, v_ffn_pre_g, v_ffn_post_g, v_w_up, v_ffn_conv_w, v_ffn_conv_b, v_w_down):
    wl = dict(zip(WEIGHTS, (ada_w, ada_b, mix_pre_g, mix_post_g, w_in, ssd_conv_w, ssd_conv_b, ssd_dt_bias, ssd_a_log,
                            ssd_d, ssd_norm_g, w_ssd_out, sc_conv_w, w_sc_out, w_o, ffn_pre_g, ffn_post_g, w_up,
                            ffn_conv_w, ffn_conv_b, w_down)))
    ml = dict(zip(WEIGHTS, (m_ada_w, m_ada_b, m_mix_pre_g, m_mix_post_g, m_w_in, m_ssd_conv_w, m_ssd_conv_b,
                            m_ssd_dt_bias, m_ssd_a_log, m_ssd_d, m_ssd_norm_g, m_w_ssd_out, m_sc_conv_w, m_w_sc_out, m_w_o,
                            m_ffn_pre_g, m_ffn_post_g, m_w_up, m_ffn_conv_w, m_ffn_conv_b, m_w_down)))
    vl = dict(zip(WEIGHTS, (v_ada_w, v_ada_b, v_mix_pre_g, v_mix_post_g, v_w_in, v_ssd_conv_w, v_ssd_conv_b,
                            v_ssd_dt_bias, v_ssd_a_log, v_ssd_d, v_ssd_norm_g, v_w_ssd_out, v_sc_conv_w, v_w_sc_out, v_w_o,
                            v_ffn_pre_g, v_ffn_post_g, v_w_up, v_ffn_conv_w, v_ffn_conv_b, v_w_down)))
    depth = ada_w.shape[0]
    shapes = {n: wl[n].shape for n in WEIGHTS}
    me = 4 * lax.axis_index("x") + 2 * lax.axis_index("y") + lax.axis_index("c")

    rows, n_big = _big_rows(shapes)
    conv_flat = jnp.concatenate([wl[n][l].reshape(-1) for l in range(depth) for n in CONVW])
    n_conv = conv_flat.shape[0]
    conv_flat = jnp.pad(conv_flat, (0, -n_conv % (ROW_W // 2)))
    conv_rows = lax.bitcast_convert_type(conv_flat, BF16).reshape(-1, ROW_W)

    def local_rows(l):
        pieces = _pack_big_local(lambda n: wl[n].astype(BF16), l) + ([conv_rows] if l == 0 else [])
        return _pad_rows(jnp.concatenate(pieces, axis=0), ROW_PAD)

    def layer_weights(l, gathered):
        w = {n: wl[n][l] for n in REPL}
        for n, kind in BIG:
            a, b = shapes[n][1], shapes[n][2]
            blk = gathered[:, rows[n][0]:rows[n][1]]
            w[n] = blk.reshape(N_DEV * b, a) if kind == "col" else blk.reshape(N_DEV * a, b)
        for n in CONVW:
            w[n] = conv_full[(l, n)]
        return w

    gathered = _allgather_big(local_rows(0), "allgather_weights")
    conv_all = lax.bitcast_convert_type(
        gathered[:, n_big:n_big + conv_rows.shape[0]].reshape(N_DEV, -1, 2), F32)[:, :n_conv]
    conv_full, o = {}, 0
    for l in range(depth):
        for n in CONVW:
            k, cl = shapes[n][1], shapes[n][2]
            conv_full[(l, n)] = conv_all[:, o:o + k * cl].reshape(N_DEV, k, cl).transpose(1, 0, 2).reshape(k, N_DEV * cl)
            o += k * cl

    nb, S, D = x.shape
    T = nb * S
    act = x.reshape(T, D)
    c8 = jnp.pad(c, ((0, MOD_ROWS - nb), (0, 0)))
    preps, saved = [], []
    for l in range(depth):
        preps.append(_prep_layer(layer_weights(l, gathered)))
        act, s, gathered = _layer_fwd(act, c8, preps[l], S, l, gather=local_rows(l + 1) if l + 1 < depth else None)
        saved.append(s)
    dy, lacc = _loss(act, loss_target.reshape(T, D), S, "loss")
    loss_loc = lacc[0, 0]

    group_rows = {grp: _big_rows(shapes, names) for grp, names in (("early", EARLY), ("late", LATE))}

    def pair_sums(g, grp, names, l):
        slabs = [_pad_rows(g[n].astype(BF16).reshape(N_DEV, -1, ROW_W), SLAB_ALIGN) for n in names]
        slabs.append(jnp.zeros((N_DEV, -group_rows[grp][1] % ROW_PAD, ROW_W), BF16))
        gslab = jnp.concatenate(slabs, axis=1)
        from_sibling = _rs_pair_exchange(gslab, f"rs_pair_exchange_{grp}_l{l}")
        return _add_pairs(gslab, from_sibling, f"rs_pair_add_{grp}_l{l}")

    grads, pending = [None] * depth, None
    from_chips = {"early": [None] * depth, "late": [None] * depth}
    for l in reversed(range(depth)):
        dy, grads[l], got, from_chips["early"][l] = _layer_bwd(
            dy, preps[l], saved[l], S, l, chip_sums=pending, early=lambda g, l=l: pair_sums(g, "early", EARLY, l))
        if pending is not None:
            from_chips["late"][l + 1] = got
        pending = pair_sums(grads[l], "late", LATE, l)
    from_chips["late"][0] = _rs_chip_exchange(pending, "rs_chip_exchange")
    dx = dy.reshape(nb, S, D)
    g_sums = {grp: [_sum_chips(from_chips[grp][l], f"rs_chip_sum_{grp}_l{l}") for l in range(depth)]
              for grp in ("early", "late")}

    def slab_of(l, n, kind):
        grp = "early" if n in EARLY else "late"
        r0, r1 = group_rows[grp][0][n]
        a, b = shapes[n][1], shapes[n][2]
        return g_sums[grp][l][r0:r1].reshape((b, a) if kind == "col" else (a, b))

    g_big, d_big, m_big, v_big = {}, {}, {}, {}
    for n, kind in BIG:
        if n in MID_LAYOUT:
            gm = jnp.stack([slab_of(l, n, kind) for l in range(depth)], axis=1)
            fwd, back = (lambda t: t.transpose(2, 0, 1)), (lambda t: t.transpose(1, 2, 0))
            res = [gm] + list(_adam_mid(gm, fwd(wl[n]), fwd(ml[n]), fwd(vl[n]), f"adam_{n}"))
        else:
            gt = jnp.stack([slab_of(l, n, kind) for l in range(depth)])
            fwd = back = (lambda t: t.swapaxes(1, 2)) if kind == "col" else (lambda t: t)
            if n in SWAP_LAYOUT:
                res = [gt] + list(_adam_nat(gt, fwd(wl[n]), fwd(ml[n]), fwd(vl[n]), f"adam_{n}"))
            else:
                gn = back(gt)
                res, back = [gn] + list(_adam_nat(gn, wl[n], ml[n], vl[n], f"adam_{n}")), (lambda t: t)
        g_big[n], d_big[n], m_big[n], v_big[n] = [back(t) for t in res]

    parts = [jnp.broadcast_to(loss_loc, (ROW_W,))]
    small_shapes = [(ROW_W,)]
    for l in range(depth):
        for n in REPL + CONVW:
            parts.append(grads[l][n])
            small_shapes.append(tuple(grads[l][n].shape))
    total = _sum_slabs(_allgather_small(_flat_rows(parts, 8), "allgather_small"), "sum_small")
    pieces = _unflat(total, small_shapes)
    loss = pieces[0][0]
    g_small, i = {}, 1
    for l in range(depth):
        for n in REPL + CONVW:
            gp = pieces[i]
            i += 1
            if n in CONVW:
                gp = lax.dynamic_slice_in_dim(gp, me * shapes[n][2], shapes[n][2], axis=1)
            g_small[(l, n)] = gp
    order = [(l, n) for l in range(depth) for n in REPL + CONVW]
    loc_shapes = [tuple(shapes[n][1:]) for _, n in order]
    packs = lambda f: _flat_rows([f(l, n) for l, n in order], 8)
    gs_small = packs(lambda l, n: g_small[(l, n)])
    _, d_sm, m_sm, v_sm = _adam(gs_small[None], packs(lambda l, n: wl[n][l]), packs(lambda l, n: ml[n][l]),
                                packs(lambda l, n: vl[n][l]), "adam_small")

    def unpack_small(buf):
        ps = _unflat(buf, loc_shapes)
        return {n: jnp.stack([ps[order.index((l, n))] for l in range(depth)]) for n in REPL + CONVW}

    outs = []
    for big, small in ((g_big, {n: jnp.stack([g_small[(l, n)] for l in range(depth)]) for n in REPL + CONVW}),
                       (d_big, unpack_small(d_sm)), (m_big, unpack_small(m_sm)), (v_big, unpack_small(v_sm))):
        merged = {**big, **small}
        outs += [merged[n] for n in WEIGHTS]
    return (loss, dx, *outs)
```

```python
import functools
import math
from typing import Callable, NamedTuple

import jax
import jax.numpy as jnp
from jax import lax
from jax.experimental import pallas as pl
from jax.experimental.pallas import tpu as pltpu

F32, BF16 = jnp.float32, jnp.bfloat16
WGRAD = BF16
SDS = jax.ShapeDtypeStruct
MESH = pl.DeviceIdType.MESH

EPS = 1e-6
N_STATE = 128
CHUNK = 128
SSD_CONV_K, SC_CONV_K, FFN_CONV_K = 4, 3, 3
N_DEV = 8
ROW_W = 1024
ROW_PAD = 32
SLAB_ALIGN = 16
SEG_BLK = 512
STRIP = 32
FFN_STRIP = 64
GLU_W = 256
MOD_ROWS = 128
VMEM_LIMIT = 48 * 2**20

ADAM_LR, ADAM_B1, ADAM_B2, ADAM_EPS, ADAM_WD, ADAM_STEP = 0.001, 0.9, 0.999, 1e-08, 0.01, 10

NT = (((1,), (1,)), ((), ()))
TN = (((0,), (0,)), ((), ()))
NN = (((1,), (0,)), ((), ()))


def _tile(n, cap, mult=128):
    best = None
    for t in range(mult, min(n, cap) + 1, mult):
        if n % t == 0:
            best = t
    return best if best is not None else n


class _Side(NamedTuple):
    operands: tuple
    out_shape: tuple
    scratch: tuple
    start: Callable
    wait: Callable
    aliases: dict = {}


def _pc(body, *, grid, in_specs, out_specs, out_shape, name, scratch=(), side=None):
    params = pltpu.CompilerParams(dimension_semantics=("arbitrary",) * len(grid), vmem_limit_bytes=VMEM_LIMIT)
    if side is None:
        return pl.pallas_call(body, grid=grid, in_specs=in_specs, out_specs=out_specs, out_shape=out_shape,
                              scratch_shapes=list(scratch), name=name, compiler_params=params)
    single = not isinstance(out_shape, (list, tuple))
    outs = [out_shape] if single else list(out_shape)
    ospecs = [out_specs] if single else list(out_specs)
    n_in, n_out, n_scr = len(in_specs), len(outs), len(scratch)
    s_in, s_out = len(side.operands), len(side.out_shape)

    def hosted(*refs):
        ins, refs = refs[:n_in], refs[n_in:]
        sins, refs = refs[:s_in], refs[s_in:]
        mouts, refs = refs[:n_out], refs[n_out:]
        souts, refs = refs[:s_out], refs[s_out:]
        scr, sems = refs[:n_scr], refs[n_scr:]
        first = functools.reduce(lambda a, b: a & b, [pl.program_id(a) == 0 for a in range(len(grid))])
        last = functools.reduce(lambda a, b: a & b, [pl.program_id(a) == grid[a] - 1 for a in range(len(grid))])

        @pl.when(first)
        def _():
            side.start(sins, souts, sems)

        body(*ins, *mouts, *scr)

        @pl.when(last)
        def _():
            side.wait(sins, souts, sems)

    call = pl.pallas_call(
        hosted, grid=grid, in_specs=list(in_specs) + [HBM_SPEC] * s_in, out_specs=ospecs + [HBM_SPEC] * s_out,
        out_shape=outs + list(side.out_shape), scratch_shapes=list(scratch) + list(side.scratch), name=name,
        input_output_aliases={n_in + k: n_out + v for k, v in side.aliases.items()}, compiler_params=params)

    def run(*args):
        res = call(*args, *side.operands)
        main = res[0] if single else list(res[:n_out])
        return main, list(res[n_out:])

    return run


def _silu(x):
    return x * jax.nn.sigmoid(x)


def _dsilu(x):
    s = jax.nn.sigmoid(x)
    return s * (1.0 + x * (1.0 - s))


def _softplus(x):
    return jnp.maximum(x, 0.0) + jnp.log(1.0 + jnp.exp(-jnp.abs(x)))


def _dot(a, b, dims=NN):
    return lax.dot_general(a, b, dims, preferred_element_type=F32)


def _bsum(v, rows=8):
    return jnp.broadcast_to(v, (rows, v.shape[1]))


def _mm(a, b, mode, out_dtype, name, caps=(1024, 1024, 2048), side=None):
    if mode == "nn":
        (M, K), (K2, N) = a.shape, b.shape
    elif mode == "nt":
        (M, K), (N, K2) = a.shape, b.shape
    else:
        (K, M), (K2, N) = a.shape, b.shape
    assert K == K2, (a.shape, b.shape, mode)
    tm, tn, tk = _tile(M, caps[0]), _tile(N, caps[1]), _tile(K, caps[2])
    nk = K // tk
    dims = {"nn": NN, "nt": NT, "tn": TN}[mode]
    if mode == "tn":
        a_spec = pl.BlockSpec((tk, tm), lambda i, j, k: (k, i))
    else:
        a_spec = pl.BlockSpec((tm, tk), lambda i, j, k: (i, k))
    if mode == "nt":
        b_spec = pl.BlockSpec((tn, tk), lambda i, j, k: (j, k))
    else:
        b_spec = pl.BlockSpec((tk, tn), lambda i, j, k: (k, j))

    def body(a_ref, b_ref, o_ref, *acc):
        part = _dot(a_ref[...].astype(BF16), b_ref[...].astype(BF16), dims)
        if nk == 1:
            o_ref[...] = part.astype(o_ref.dtype)
        else:
            acc_ref, = acc
            k = pl.program_id(2)

            @pl.when(k == 0)
            def _():
                acc_ref[...] = part

            @pl.when(k > 0)
            def _():
                acc_ref[...] += part

            @pl.when(k == nk - 1)
            def _():
                o_ref[...] = acc_ref[...].astype(o_ref.dtype)

    return _pc(body, grid=(M // tm, N // tn, nk), in_specs=[a_spec, b_spec],
               out_specs=pl.BlockSpec((tm, tn), lambda i, j, k: (i, j)),
               out_shape=SDS((M, N), out_dtype), name=name,
               scratch=() if nk == 1 else (pltpu.VMEM((tm, tn), F32),), side=side)(a, b)


def _mm_seg(segs, b, mode, out_dtype, name, blk, tile=1024, tk=2048):
    nblk = [a.shape[1] // blk for a in segs]
    assert all(a.shape[1] % blk == 0 for a in segs)
    start = [sum(nblk[:s]) for s in range(len(segs))]
    total = sum(nblk)
    ns = len(segs)
    N = b.shape[1]
    tn = _tile(N, tile)
    if mode == "nn":
        M = segs[0].shape[0]
        tm = _tile(M, tile)
        grid = (M // tm, N // tn, total)
        a_specs = [pl.BlockSpec((tm, blk), lambda i, j, k, k0=k0, n=n: (i, jnp.clip(k - k0, 0, n - 1)))
                   for k0, n in zip(start, nblk)]
        b_spec = pl.BlockSpec((blk, tn), lambda i, j, k: (k, j))
        out_rows, tmo, dims, seg_axis = M, tm, NN, 2
    else:
        K = segs[0].shape[0]
        tkk = _tile(K, tk)
        grid = (total, N // tn, K // tkk)
        a_specs = [pl.BlockSpec((tkk, blk), lambda i, j, k, i0=i0, n=n: (
            jnp.where((i >= i0) & (i < i0 + n), k, 0), jnp.clip(i - i0, 0, n - 1))) for i0, n in zip(start, nblk)]
        b_spec = pl.BlockSpec((tkk, tn), lambda i, j, k: (k, j))
        out_rows, tmo, seg_axis = total * blk, blk, 0
    nk = grid[2]
    acc_shape = (tm, tn) if mode == "nn" else (tn, blk)

    def body(*refs):
        a_refs, b_ref, o_ref, acc_ref = refs[:ns], refs[ns], refs[ns + 1], refs[ns + 2]
        k = pl.program_id(2)
        sel = pl.program_id(seg_axis)

        @pl.when(k == 0)
        def _():
            acc_ref[...] = jnp.zeros_like(acc_ref)

        for s in range(ns):
            @pl.when((sel >= start[s]) & (sel < start[s] + nblk[s]))
            def _(s=s):
                a_, b_ = a_refs[s][...].astype(BF16), b_ref[...].astype(BF16)
                acc_ref[...] += _dot(a_, b_, NN) if mode == "nn" else _dot(b_, a_, TN)

        @pl.when(k == nk - 1)
        def _():
            acc = acc_ref[...]
            o_ref[...] = (acc if mode == "nn" else acc.T).astype(o_ref.dtype)

    return _pc(body, grid=grid, in_specs=a_specs + [b_spec], out_specs=pl.BlockSpec((tmo, tn), lambda i, j, k: (i, j)),
               out_shape=SDS((out_rows, N), out_dtype), name=name, scratch=(pltpu.VMEM(acc_shape, F32),))(*segs, b)


def _modk(c8, ada_w, ada_b, name):
    rows, D = c8.shape
    N = ada_w.shape[0]
    tn = _tile(N, 1536)

    def body(c_ref, w_ref, b_ref, mod_ref, ca_ref):
        ca = _silu(c_ref[...]).astype(BF16)
        mod_ref[...] = _dot(ca, w_ref[...], NT) + b_ref[...]
        ca_ref[...] = ca

    return _pc(body, grid=(N // tn,),
               in_specs=[pl.BlockSpec((rows, D), lambda j: (0, 0)), pl.BlockSpec((tn, D), lambda j: (j, 0)),
                         pl.BlockSpec((1, tn), lambda j: (0, j))],
               out_specs=[pl.BlockSpec((rows, tn), lambda j: (0, j)), pl.BlockSpec((rows, D), lambda j: (0, 0))],
               out_shape=[SDS((rows, N), F32), SDS((rows, D), BF16)], name=name)(c8, ada_w, ada_b)


def _row_tile(S):
    return _tile(S, 512, 8)


def _strips(tm, fn, init=0, rows=None):
    rows = STRIP if rows is None else rows
    assert tm % rows == 0
    return lax.fori_loop(0, tm // rows, lambda r, c: fn(pl.multiple_of(r * rows, rows), c), init)


def _strips_prev(tm, rows, ref, prev, fn, init=0):
    carry = fn(0, jnp.concatenate([prev, ref[0:rows, :]], axis=0), init)

    def step(r, c):
        r0 = pl.multiple_of(r * rows, rows)
        return fn(r0, ref[pl.ds(pl.multiple_of(r0 - 8, 8), rows + 8), :], c)

    return lax.fori_loop(1, tm // rows, step, carry)


def _strips_next(tm, rows, ref, nxt, fn, init=0):
    def step(r, c):
        r0 = pl.multiple_of(r * rows, rows)
        return fn(r0, ref[pl.ds(r0, rows + 8), :], c)

    carry = lax.fori_loop(0, tm // rows - 1, step, init)
    return fn(tm - rows, jnp.concatenate([ref[tm - rows:tm, :], nxt], axis=0), carry)


def _rows8(rows):
    pad = 8 - len(rows)
    return jnp.concatenate(rows + ([jnp.zeros((pad, rows[0].shape[1]), F32)] if pad else []), axis=0)


def _fold8(v):
    return jnp.sum(v.reshape(v.shape[0] // 8, 8, v.shape[1]), axis=0)


def _norm_mod(x, g, mod3, sc_seg, sh_seg, S, name):
    T, D = x.shape
    tm = _row_tile(S)
    tpb = S // tm

    def body(x_ref, g_ref, sc_ref, sh_ref, h_ref):
        x_ = x_ref[...]
        r = lax.rsqrt(jnp.mean(x_ * x_, axis=-1, keepdims=True) + EPS)
        h_ref[...] = ((x_ * r) * (g_ref[...] * (1.0 + sc_ref[...])) + sh_ref[...]).astype(BF16)

    return _pc(body, grid=(T // tm,),
               in_specs=[pl.BlockSpec((tm, D), lambda i: (i, 0)), pl.BlockSpec((1, D), lambda i: (0, 0)),
                         pl.BlockSpec((None, 1, D), lambda i: (i // tpb, 0, sc_seg)),
                         pl.BlockSpec((None, 1, D), lambda i: (i // tpb, 0, sh_seg))],
               out_specs=pl.BlockSpec((tm, D), lambda i: (i, 0)), out_shape=SDS((T, D), BF16), name=name)(x, g, mod3, mod3)


def _resid_post(x, fo, mod3, gt_seg, pg, S, name):
    T, D = x.shape
    tm = _row_tile(S)
    tpb = S // tm

    def body(x_ref, f_ref, gt_ref, pg_ref, o_ref):
        f = f_ref[...]
        r = lax.rsqrt(jnp.mean(f * f, axis=-1, keepdims=True) + EPS)
        o_ref[...] = x_ref[...] + (f * r) * (gt_ref[...] * pg_ref[...])

    return _pc(body, grid=(T // tm,),
               in_specs=[pl.BlockSpec((tm, D), lambda i: (i, 0)), pl.BlockSpec((tm, D), lambda i: (i, 0)),
                         pl.BlockSpec((None, 1, D), lambda i: (i // tpb, 0, gt_seg)),
                         pl.BlockSpec((1, D), lambda i: (0, 0))],
               out_specs=pl.BlockSpec((tm, D), lambda i: (i, 0)), out_shape=SDS((T, D), F32), name=name)(x, fo, mod3, pg)


def _post_bwd(fo, mod3, gt_seg, pg, dout, S, name):
    T, D = fo.shape
    tm = _row_tile(S)
    tpb = S // tm
    nb = T // S

    def body(f_ref, gt_ref, pg_ref, d_ref, df_ref, dgt_ref, dpg_ref):
        i = pl.program_id(0)

        @pl.when(i == 0)
        def _():
            dpg_ref[...] = jnp.zeros_like(dpg_ref)

        @pl.when(i % tpb == 0)
        def _():
            dgt_ref[...] = jnp.zeros_like(dgt_ref)

        f, d = f_ref[...], d_ref[...]
        r = lax.rsqrt(jnp.mean(f * f, axis=-1, keepdims=True) + EPS)
        n = f * r
        dn = d * (gt_ref[...] * pg_ref[...])
        df_ref[...] = (r * (dn - n * jnp.mean(dn * n, axis=-1, keepdims=True))).astype(df_ref.dtype)
        tot = jnp.sum(d * n, axis=0, keepdims=True)
        dgt_ref[...] += _bsum(tot * pg_ref[...])
        dpg_ref[...] += _bsum(tot * gt_ref[...])

    return _pc(body, grid=(T // tm,),
               in_specs=[pl.BlockSpec((tm, D), lambda i: (i, 0)),
                         pl.BlockSpec((None, 1, D), lambda i: (i // tpb, 0, gt_seg)),
                         pl.BlockSpec((1, D), lambda i: (0, 0)), pl.BlockSpec((tm, D), lambda i: (i, 0))],
               out_specs=[pl.BlockSpec((tm, D), lambda i: (i, 0)), pl.BlockSpec((8, D), lambda i: (i // tpb, 0)),
                          pl.BlockSpec((8, D), lambda i: (0, 0))],
               out_shape=[SDS((T, D), BF16), SDS((nb * 8, D), F32), SDS((8, D), F32)], name=name)(fo, mod3, pg, dout)


def _pre_bwd(x, g, mod3, sc_seg, dh, dout, S, name):
    T, D = x.shape
    tm = _row_tile(S)
    tpb = S // tm
    nb = T // S

    def body(x_ref, g_ref, sc_ref, dh_ref, d_ref, dx_ref, dg_ref, dsc_ref, dsh_ref):
        i = pl.program_id(0)

        @pl.when(i == 0)
        def _():
            dg_ref[...] = jnp.zeros_like(dg_ref)

        @pl.when(i % tpb == 0)
        def _():
            dsc_ref[...] = jnp.zeros_like(dsc_ref)
            dsh_ref[...] = jnp.zeros_like(dsh_ref)

        x_, dh_ = x_ref[...], dh_ref[...]
        r = lax.rsqrt(jnp.mean(x_ * x_, axis=-1, keepdims=True) + EPS)
        n = x_ * r
        dn = dh_ * (g_ref[...] * (1.0 + sc_ref[...]))
        dx_ref[...] = d_ref[...] + r * (dn - n * jnp.mean(dn * n, axis=-1, keepdims=True))
        dhn = jnp.sum(dh_ * n, axis=0, keepdims=True)
        dg_ref[...] += _bsum(dhn * (1.0 + sc_ref[...]))
        dsc_ref[...] += _bsum(dhn * g_ref[...])
        dsh_ref[...] += _bsum(jnp.sum(dh_, axis=0, keepdims=True))

    row = pl.BlockSpec((tm, D), lambda i: (i, 0))
    return _pc(body, grid=(T // tm,),
               in_specs=[row, pl.BlockSpec((1, D), lambda i: (0, 0)),
                         pl.BlockSpec((None, 1, D), lambda i: (i // tpb, 0, sc_seg)), row, row],
               out_specs=[row, pl.BlockSpec((8, D), lambda i: (0, 0)), pl.BlockSpec((8, D), lambda i: (i // tpb, 0)),
                          pl.BlockSpec((8, D), lambda i: (i // tpb, 0))],
               out_shape=[SDS((T, D), F32), SDS((8, D), F32), SDS((nb * 8, D), F32), SDS((nb * 8, D), F32)],
               name=name)(x, g, mod3, dh, dout)


def _loss(y, target, S, name):
    T, D = y.shape
    tm = _row_tile(S)

    def body(y_ref, t_ref, dy_ref, l_ref):
        @pl.when(pl.program_id(0) == 0)
        def _():
            l_ref[...] = jnp.zeros_like(l_ref)

        def strip(r0, carry):
            rows = pl.ds(r0, STRIP)
            e = y_ref[rows, :] - t_ref[rows, :]
            dy_ref[rows, :] = e * (1.0 / D)
            return carry + _fold8(e * e)

        acc = _strips(tm, strip, jnp.zeros((8, D), F32))
        l_ref[...] += jnp.broadcast_to(jnp.sum(acc, keepdims=True) * (0.5 / D), l_ref.shape)

    row = pl.BlockSpec((tm, D), lambda i: (i, 0))
    return _pc(body, grid=(T // tm,), in_specs=[row, row],
               out_specs=[row, pl.BlockSpec((8, 128), lambda i: (0, 0))],
               out_shape=[SDS((T, D), F32), SDS((8, 128), F32)], name=name)(y, target)


def _conv_geom(view, C, S):
    arr, off = view
    T = arr.shape[0]
    tm = _row_tile(S)
    tc = _tile(C, 512)
    assert off % tc == 0 and C % tc == 0
    return arr, off // tc, T, tm, tc, S // tm


def _prev_spec(tm, tc, ob, order):
    if order == "ij":
        return pl.BlockSpec((8, tc), lambda i, j: (jnp.maximum(i * (tm // 8) - 1, 0), ob + j))
    return pl.BlockSpec((8, tc), lambda j, i: (jnp.maximum(i * (tm // 8) - 1, 0), ob + j))


def _next_spec(T, tm, tc, ob, order):
    last = T // 8 - 1
    if order == "ij":
        return pl.BlockSpec((8, tc), lambda i, j: (jnp.minimum((i + 1) * (tm // 8), last), ob + j))
    return pl.BlockSpec((8, tc), lambda j, i: (jnp.minimum((i + 1) * (tm // 8), last), ob + j))


def _taps(win, w_ref, K, lead, rows):
    acc = win[lead:lead + rows] * w_ref[K - 1:K, :]
    for j in range(1, K):
        acc = acc + win[lead - j:lead - j + rows] * w_ref[K - 1 - j:K - j, :]
    return acc


def _taps_t(win, w_ref, K, rows):
    acc = win[0:rows] * w_ref[K - 1:K, :]
    for j in range(1, K):
        acc = acc + win[j:j + rows] * w_ref[K - 1 - j:K - j, :]
    return acc


def _conv_fwd(view, C, w8, b, K, S, name):
    arr, ob, T, tm, tc, tps = _conv_geom(view, C, S)

    def body(u_ref, p_ref, w_ref, b_ref, o_ref):
        first = (pl.program_id(0) % tps) == 0

        def strip(r0, win, carry):
            o_ref[pl.ds(r0, STRIP), :] = _taps(win, w_ref, K, 8, STRIP) + b_ref[...]
            return carry

        _strips_prev(tm, STRIP, u_ref, jnp.where(first, 0.0, p_ref[...]), strip)

    return _pc(body, grid=(T // tm, C // tc),
               in_specs=[pl.BlockSpec((tm, tc), lambda i, j: (i, ob + j)), _prev_spec(tm, tc, ob, "ij"),
                         pl.BlockSpec((8, tc), lambda i, j: (0, j)), pl.BlockSpec((1, tc), lambda i, j: (0, j))],
               out_specs=pl.BlockSpec((tm, tc), lambda i, j: (i, j)), out_shape=SDS((T, C), F32), name=name)(
                   arr, arr, w8, b)


def _conv_bwd_in(dview, C, w8, K, S, out_dtype, name):
    arr, ob, T, tm, tc, tps = _conv_geom(dview, C, S)

    def body(d_ref, n_ref, w_ref, o_ref):
        last = (pl.program_id(0) % tps) == tps - 1

        def strip(r0, win, carry):
            o_ref[pl.ds(r0, STRIP), :] = _taps_t(win, w_ref, K, STRIP).astype(o_ref.dtype)
            return carry

        _strips_next(tm, STRIP, d_ref, jnp.where(last, 0.0, n_ref[...]), strip)

    return _pc(body, grid=(T // tm, C // tc),
               in_specs=[pl.BlockSpec((tm, tc), lambda i, j: (i, ob + j)), _next_spec(T, tm, tc, ob, "ij"),
                         pl.BlockSpec((8, tc), lambda i, j: (0, j))],
               out_specs=pl.BlockSpec((tm, tc), lambda i, j: (i, j)), out_shape=SDS((T, C), out_dtype), name=name)(
                   arr, arr, w8)


def _conv_bwd_w(dview, uview, C, K, S, name):
    darr, dob, T, tm, tc, tps = _conv_geom(dview, C, S)
    uarr, uob, _, _, _, _ = _conv_geom(uview, C, S)

    def body(d_ref, u_ref, p_ref, o_ref):
        i = pl.program_id(1)

        @pl.when(i == 0)
        def _():
            o_ref[...] = jnp.zeros_like(o_ref)

        first = (i % tps) == 0

        def strip(r0, win, carry):
            d = d_ref[pl.ds(r0, STRIP), :]
            sums = [_fold8(d * win[8 - (K - 1 - k):8 - (K - 1 - k) + STRIP]) for k in range(K)] + [_fold8(d)]
            return tuple(c + s for c, s in zip(carry, sums))

        acc = _strips_prev(tm, STRIP, u_ref, jnp.where(first, 0.0, p_ref[...]), strip,
                           tuple(jnp.zeros((8, tc), F32) for _ in range(K + 1)))
        o_ref[...] += _rows8([jnp.sum(a, axis=0, keepdims=True) for a in acc])

    return _pc(body, grid=(C // tc, T // tm),
               in_specs=[pl.BlockSpec((tm, tc), lambda j, i: (i, dob + j)),
                         pl.BlockSpec((tm, tc), lambda j, i: (i, uob + j)), _prev_spec(tm, tc, uob, "ji")],
               out_specs=pl.BlockSpec((8, tc), lambda j, i: (0, j)), out_shape=SDS((8, C), F32), name=name)(
                   darr, uarr, uarr)


def _ffn_act_fwd(uu, w8, b, S, name):
    K, gw = FFN_CONV_K, GLU_W
    T, F2 = uu.shape
    tm, tc = _row_tile(S), 2 * GLU_W
    tps = S // tm

    def body(u_ref, p_ref, w_ref, b_ref, a_ref):
        first = (pl.program_id(0) % tps) == 0

        def strip(r0, win, carry):
            u = _taps(win, w_ref, K, 8, STRIP) + b_ref[...]
            a_ref[pl.ds(r0, STRIP), :] = (_silu(u[:, :gw]) * u[:, gw:]).astype(BF16)
            return carry

        _strips_prev(tm, STRIP, u_ref, jnp.where(first, 0.0, p_ref[...]), strip)

    return _pc(body, grid=(T // tm, F2 // tc),
               in_specs=[pl.BlockSpec((tm, tc), lambda i, j: (i, j)), _prev_spec(tm, tc, 0, "ij"),
                         pl.BlockSpec((8, tc), lambda i, j: (0, j)), pl.BlockSpec((1, tc), lambda i, j: (0, j))],
               out_specs=pl.BlockSpec((tm, gw), lambda i, j: (i, j)), out_shape=SDS((T, F2 // 2), BF16), name=name)(
                   uu, uu, w8, b)


def _ffn_act_bwd(uu, da, w8, b, S, name, side=None):
    K, gw = FFN_CONV_K, GLU_W
    T, F2 = uu.shape
    tm, tc = _row_tile(S), 2 * GLU_W
    tps = S // tm
    last16 = T // 16 - 1

    def body(u_ref, p_ref, n_ref, da_ref, dan_ref, w_ref, b_ref, duu_ref, cw_ref, dabuf):
        i = pl.program_id(1)

        @pl.when(i == 0)
        def _():
            cw_ref[...] = jnp.zeros_like(cw_ref)

        first = (i % tps) == 0
        last = (i % tps) == tps - 1
        dabuf[0:tm, :] = da_ref[...].astype(F32)
        dabuf[tm:tm + 8, :] = jnp.where(last, 0.0, dan_ref[...].astype(F32)[0:8, :])
        fs, ext = FFN_STRIP, FFN_STRIP + 8

        def strip(r0, win, carry):
            shifted = [win[8 - j:8 - j + ext] for j in range(K)]
            u = b_ref[...] + shifted[0] * w_ref[K - 1:K, :]
            for j in range(1, K):
                u = u + shifted[j] * w_ref[K - 1 - j:K - j, :]
            da_ = dabuf[pl.ds(r0, ext), :]
            g, v = u[:, :gw], u[:, gw:]
            du = jnp.concatenate([da_ * v * _dsilu(g), da_ * _silu(g)], axis=1)
            duu_ref[pl.ds(r0, FFN_STRIP), :] = _taps_t(du, w_ref, K, FFN_STRIP).astype(BF16)
            dmain = du[0:FFN_STRIP]
            sums = [_fold8(dmain * shifted[K - 1 - k][0:FFN_STRIP]) for k in range(K)] + [_fold8(dmain)]
            return tuple(c + s for c, s in zip(carry, sums))

        acc = strip(0, jnp.concatenate([jnp.where(first, 0.0, p_ref[...]), u_ref[0:ext, :]], axis=0),
                    tuple(jnp.zeros((8, tc), F32) for _ in range(K + 1)))

        def step(r, c):
            r0 = pl.multiple_of(r * fs, fs)
            return strip(r0, u_ref[pl.ds(pl.multiple_of(r0 - 8, 8), fs + 16), :], c)

        acc = lax.fori_loop(1, tm // fs - 1, step, acc)
        acc = strip(tm - fs, jnp.concatenate([u_ref[tm - ext:tm, :], n_ref[...]], axis=0), acc)
        cw_ref[...] += _rows8([jnp.sum(a, axis=0, keepdims=True) for a in acc])

    return _pc(body, grid=(F2 // tc, T // tm),
               in_specs=[pl.BlockSpec((tm, tc), lambda j, i: (i, j)), _prev_spec(tm, tc, 0, "ji"),
                         _next_spec(T, tm, tc, 0, "ji"), pl.BlockSpec((tm, gw), lambda j, i: (i, j)),
                         pl.BlockSpec((16, gw), lambda j, i: (jnp.minimum((i + 1) * (tm // 16), last16), j)),
                         pl.BlockSpec((8, tc), lambda j, i: (0, j)), pl.BlockSpec((1, tc), lambda j, i: (0, j))],
               out_specs=[pl.BlockSpec((tm, tc), lambda j, i: (i, j)), pl.BlockSpec((8, tc), lambda j, i: (0, j))],
               out_shape=[SDS((T, F2), BF16), SDS((8, F2), F32)], name=name,
               scratch=(pltpu.VMEM((tm + 8, gw), F32),), side=side)(
                   uu, uu, uu, da, da, w8, b)


def _ssd_common(dtc_raw, dtr_raw, hpc, hpr, L):
    dt_c = _softplus(dtc_raw + hpc[0:1, :])
    a_c = -jnp.exp(hpc[1:2, :])
    dt_r = _softplus(dtr_raw + hpr[:, 0:1])
    a_r = -jnp.exp(hpr[:, 1:2])
    li = lax.broadcasted_iota(jnp.int32, (L, L), 0)
    si = lax.broadcasted_iota(jnp.int32, (L, L), 1)
    low = li >= si
    upp = li <= si
    acs_c = _dotx(low, dt_c * a_c, split="b")
    acs_r = _dotx(dt_r * a_r, upp)
    return dt_c, a_c, acs_c, acs_r, low, upp


def _dotx(a, b, split="a", parts=3, dims=NN):
    val, one = (a, b) if split == "a" else (b, a)
    one = one.astype(BF16)
    acc, rem = None, val
    for i in range(parts):
        piece = rem.astype(BF16)
        t = _dot(piece, one, dims) if split == "a" else _dot(one, piece, dims)
        acc = t if acc is None else acc + t
        if i + 1 < parts:
            rem = rem - piece.astype(F32)
    return acc


def _head_maps(R, P, L):
    RP = R * P
    sel = (lax.broadcasted_iota(jnp.int32, (RP, R), 0) // P == lax.broadcasted_iota(jnp.int32, (RP, R), 1)).astype(F32)
    selt = (lax.broadcasted_iota(jnp.int32, (R, RP), 1) // P == lax.broadcasted_iota(jnp.int32, (R, RP), 0)).astype(F32)
    colb = (lax.broadcasted_iota(jnp.int32, (R, R * L), 1) // L == lax.broadcasted_iota(jnp.int32, (R, R * L), 0)).astype(F32)
    return sel, selt, colb


def _pair_diag(mats, rhs_b, R, P):
    lanes = 2 * P
    lo = lax.broadcasted_iota(jnp.int32, (mats[0].shape[0], lanes), 1) < P
    out = []
    for q in range(R // 2):
        rp = rhs_b[:, q * lanes:(q + 1) * lanes]
        out.append(jnp.where(lo, _dot(mats[2 * q], rp), _dot(mats[2 * q + 1], rp)))
    return jnp.concatenate(out, axis=1) if len(out) > 1 else out[0]


def _ssd_specs(pre, off_x, off_b, off_c, G, R, P, nb, nc, rev):
    L, N, RP = CHUNK, N_STATE, R * P
    cidx = (lambda c: nc - 1 - c) if rev else (lambda c: c)
    xb, bb, cb = off_x // RP, off_b // N, off_c // N
    assert off_x % RP == 0 and off_b % N == 0 and off_c % N == 0
    row = lambda b, c: b * nc + cidx(c)
    return dict(
        x=pl.BlockSpec((L, RP), lambda g, b, c: (row(b, c), xb + g)),
        b=pl.BlockSpec((L, N), lambda g, b, c: (row(b, c), bb + g)),
        c=pl.BlockSpec((L, N), lambda g, b, c: (row(b, c), cb + g)),
        dtc=pl.BlockSpec((None, L, R), lambda g, b, c: (g, row(b, c), 0)),
        dtr=pl.BlockSpec((None, R, L), lambda g, b, c: (g, 0, row(b, c))),
        hpc=pl.BlockSpec((None, 8, R), lambda g, b, c: (g, 0, 0)),
        hpr=pl.BlockSpec((None, R, 8), lambda g, b, c: (g, 0, 0)),
        y=pl.BlockSpec((L, RP), lambda g, b, c: (row(b, c), g)),
        bc=pl.BlockSpec((L, N), lambda g, b, c: (row(b, c), g)),
        hs=pl.BlockSpec((None, None, N, RP), lambda g, b, c: (row(b, c), g, 0, 0)),
    )


def _ssd_fwd(pre, offs, dtc, dtr, hpc, hpr, G, R, P, S, name, side=None):
    T = pre.shape[0]
    L, N, RP = CHUNK, N_STATE, R * P
    nc, nb = S // L, T // S
    sp = _ssd_specs(pre, *offs, G, R, P, nb, nc, False)

    def body(px_ref, pb_ref, pc_ref, dtc_ref, dtr_ref, hpc_ref, hpr_ref, y_ref, hs_ref, hst):
        @pl.when(pl.program_id(2) == 0)
        def _():
            hst[...] = jnp.zeros_like(hst)

        xs, bm, cm = _silu(px_ref[...]), _silu(pb_ref[...]), _silu(pc_ref[...])
        hpc_ = hpc_ref[...]
        dt_c, _, acs_c, acs_r, low, _ = _ssd_common(dtc_ref[...], dtr_ref[...], hpc_, hpr_ref[...], L)
        _, selt, colb = _head_maps(R, P, L)
        dt_e, a_e, hp_e = _dotx(dt_c, selt), _dotx(acs_c, selt), _dotx(hpc_, selt)
        a_bc = _dotx(acs_c, colb)
        a_last = a_e[L - 1:L, :]
        bb, cb = bm.astype(BF16), cm.astype(BF16)
        gm = _dot(cb, bb, NT)
        hprev = hst[...]
        hprev_b = hprev.astype(BF16)
        hs_ref[...] = hprev_b
        xdt = xs * dt_e
        xdt_b = xdt.astype(BF16)
        ms = []
        for r in range(R):
            dec = jnp.exp(jnp.where(low, a_bc[:, r * L:(r + 1) * L] - acs_r[r:r + 1, :], -jnp.inf))
            ms.append((gm * dec).astype(BF16))
        y = _pair_diag(ms, xdt_b, R, P) + _dot(cb, hprev_b) * jnp.exp(a_e) + hp_e[2:3, :] * xs
        y_ref[...] = y
        xw = (xdt * jnp.exp(a_last - a_e)).astype(BF16)
        hst[...] = hprev * jnp.exp(a_last) + _dot(bb, xw, TN)

    return _pc(body, grid=(G, nb, nc),
               in_specs=[sp["x"], sp["b"], sp["c"], sp["dtc"], sp["dtr"], sp["hpc"], sp["hpr"]],
               out_specs=[sp["y"], sp["hs"]],
               out_shape=[SDS((T, G * RP), F32), SDS((nb * nc, G, N, RP), BF16)], name=name,
               scratch=(pltpu.VMEM((N, RP), F32),), side=side)(pre, pre, pre, dtc, dtr, hpc, hpr)


def _ssd_bwd(pre, offs, dtc, dtr, hpc, hpr, hs, dy, G, R, P, S, name, side=None):
    T = pre.shape[0]
    L, N, RP = CHUNK, N_STATE, R * P
    nc, nb = S // L, T // S
    sp = _ssd_specs(pre, *offs, G, R, P, nb, nc, True)

    def body(px_ref, pb_ref, pc_ref, dtc_ref, dtr_ref, hpc_ref, hpr_ref, hs_ref, dy_ref,
             dpx_ref, dpb_ref, dpc_ref, ddt_ref, hpg_ref, dhst):
        bi, ci = pl.program_id(1), pl.program_id(2)

        @pl.when(ci == 0)
        def _():
            dhst[...] = jnp.zeros_like(dhst)

        @pl.when((bi == 0) & (ci == 0))
        def _():
            hpg_ref[...] = jnp.zeros_like(hpg_ref)

        px, pb, pcc = px_ref[...], pb_ref[...], pc_ref[...]
        xs, bm, cm = _silu(px), _silu(pb), _silu(pcc)
        hpc_ = hpc_ref[...]
        dtc_raw = dtc_ref[...]
        dt_c, a_c, acs_c, acs_r, low, upp = _ssd_common(dtc_raw, dtr_ref[...], hpc_, hpr_ref[...], L)
        sel, selt, colb = _head_maps(R, P, L)
        dt_e, a_e, hp_e = _dotx(dt_c, selt), _dotx(acs_c, selt), _dotx(hpc_, selt)
        a_bc = _dotx(acs_c, colb)
        a_last = a_e[L - 1:L, :]
        e_e, w_e = jnp.exp(a_e), jnp.exp(a_last - a_e)
        bb, cb = bm.astype(BF16), cm.astype(BF16)
        gm = _dot(cb, bb, NT)
        gmt = _dot(bb, cb, NT)
        hprev = hs_ref[...]
        dhn = dhst[...]
        dhn_b = dhn.astype(BF16)
        dy = dy_ref[...]
        dy_b = dy.astype(BF16)
        xdt = xs * dt_e
        xdt_b = xdt.astype(BF16)
        yoff = _dot(cb, hprev) * e_e
        dye_b = (dy * e_e).astype(BF16)
        dcm = _dot(dye_b, hprev, NT)
        dhst[...] = _dot(cb, dye_b, TN) + jnp.exp(a_last) * dhn
        dxdt_st = _dot(bb, dhn_b) * w_e
        dbm = _dot((xdt * w_e).astype(BF16), dhn_b, NT)
        lanes = 2 * P
        lo = lax.broadcasted_iota(jnp.int32, (L, lanes), 1) < P
        dg = jnp.zeros((L, L), F32)
        es, css = [], []
        for r in range(R):
            col_b, row = a_bc[:, r * L:(r + 1) * L], acs_r[r:r + 1, :]
            dec = jnp.exp(jnp.where(low, col_b - row, -jnp.inf))
            q = r // 2
            dyp = dy_b[:, q * lanes:(q + 1) * lanes]
            dyp = jnp.where(lo if r % 2 == 0 else ~lo, dyp, jnp.zeros_like(dyp))
            dm = _dot(dyp, xdt_b[:, q * lanes:(q + 1) * lanes], NT)
            dg = dg + dm * dec
            e = dm * (gm * dec)
            es.append(e)
            css.append(jnp.sum(e, axis=0, keepdims=True))
        dgb = dg.astype(BF16)
        dcm = dcm + _dot(dgb, bb)
        dbm = dbm + _dot(dgb, cb, TN)
        colbt = (lax.broadcasted_iota(jnp.int32, (R * L, R), 0) // L
                 == lax.broadcasted_iota(jnp.int32, (R * L, R), 1)).astype(F32)
        eye = (lax.broadcasted_iota(jnp.int32, (R, R), 0) == lax.broadcasted_iota(jnp.int32, (R, R), 1)).astype(F32)
        row_sums = _dotx(jnp.concatenate(es, axis=1), colbt)
        col_sums = _dotx(jnp.concatenate(css, axis=0), eye, dims=TN)
        mts = []
        for r in range(R):
            dect = jnp.exp(jnp.where(upp, acs_r[r:r + 1, :] - a_bc[:, r * L:(r + 1) * L], -jnp.inf))
            mts.append((gmt * dect).astype(BF16))
        dxdt = _pair_diag(mts, dy_b, R, P) + dxdt_st
        q_st = _dotx(xdt * dxdt_st, sel, parts=1)
        da = row_sums - col_sums + _dotx(dy * yoff, sel, parts=1) - q_st
        hh = jnp.sum(_dotx(dhn * hprev.astype(F32), sel, parts=1), axis=0, keepdims=True)
        da_last = jnp.exp(acs_c[L - 1:L, :]) * hh + jnp.sum(q_st, axis=0, keepdims=True)
        rowi = lax.broadcasted_iota(jnp.int32, (L, R), 0)
        da = da + jnp.where(rowi == L - 1, da_last, 0.0)
        dpx_ref[...] = (dxdt * dt_e + hp_e[2:3, :] * dy) * _dsilu(px)
        dpb_ref[...] = dbm * _dsilu(pb)
        dpc_ref[...] = dcm * _dsilu(pcc)
        dadt = _dotx(upp, da, split="b")
        ddt = _dotx(dxdt * xs, sel, parts=1) + dadt * a_c
        ddt_raw = ddt * jax.nn.sigmoid(dtc_raw + hpc_[0:1, :])
        ddt_ref[...] = ddt_raw
        d_a = jnp.sum(dadt * dt_c, axis=0, keepdims=True)
        d_d = jnp.sum(_dotx(dy * xs, sel, parts=1), axis=0, keepdims=True)
        rows = [jnp.sum(ddt_raw, axis=0, keepdims=True), d_a * a_c, d_d, jnp.zeros((5, R), F32)]
        hpg_ref[...] += jnp.concatenate(rows, axis=0)

    return _pc(body, grid=(G, nb, nc),
               in_specs=[sp["x"], sp["b"], sp["c"], sp["dtc"], sp["dtr"], sp["hpc"], sp["hpr"], sp["hs"], sp["y"]],
               out_specs=[sp["y"], sp["bc"], sp["bc"], sp["dtc"], pl.BlockSpec((None, 8, R), lambda g, b, c: (g, 0, 0))],
               out_shape=[SDS((T, G * RP), F32), SDS((T, G * N), F32), SDS((T, G * N), F32), SDS((G, T, R), F32),
                          SDS((G, 8, R), F32)], name=name,
               scratch=(pltpu.VMEM((N, RP), F32),), side=side)(pre, pre, pre, dtc, dtr, hpc, hpr, hs, dy)


def _gate_norm_fwd(y, zview, ng, G, S, name):
    T, DI = y.shape
    zarr, zoff = zview
    gw = DI // G
    tm = _row_tile(S)
    zb = zoff // gw
    assert zoff % gw == 0

    def body(y_ref, z_ref, g_ref, o_ref):
        yg = y_ref[...] * _silu(z_ref[...])
        r = lax.rsqrt(jnp.mean(yg * yg, axis=-1, keepdims=True) + EPS)
        o_ref[...] = (yg * r * g_ref[...]).astype(BF16)

    return _pc(body, grid=(T // tm, G),
               in_specs=[pl.BlockSpec((tm, gw), lambda i, g: (i, g)), pl.BlockSpec((tm, gw), lambda i, g: (i, zb + g)),
                         pl.BlockSpec((1, gw), lambda i, g: (0, g))],
               out_specs=pl.BlockSpec((tm, gw), lambda i, g: (i, g)), out_shape=SDS((T, DI), BF16), name=name)(y, zarr, ng)


def _gate_norm_bwd(y, zview, ng, dyn, G, S, name):
    T, DI = y.shape
    zarr, zoff = zview
    gw = DI // G
    tm = _row_tile(S)
    zb = zoff // gw

    def body(y_ref, z_ref, g_ref, d_ref, dy_ref, dz_ref, dg_ref):
        @pl.when(pl.program_id(1) == 0)
        def _():
            dg_ref[...] = jnp.zeros_like(dg_ref)

        y_, z, d = y_ref[...], z_ref[...], d_ref[...]
        sz = _silu(z)
        yg = y_ * sz
        r = lax.rsqrt(jnp.mean(yg * yg, axis=-1, keepdims=True) + EPS)
        n = yg * r
        dn = d * g_ref[...]
        dyg = r * (dn - n * jnp.mean(dn * n, axis=-1, keepdims=True))
        dy_ref[...] = dyg * sz
        dz_ref[...] = (dyg * y_ * _dsilu(z)).astype(BF16)
        dg_ref[...] += _bsum(jnp.sum(d * n, axis=0, keepdims=True))

    return _pc(body, grid=(G, T // tm),
               in_specs=[pl.BlockSpec((tm, gw), lambda g, i: (i, g)), pl.BlockSpec((tm, gw), lambda g, i: (i, zb + g)),
                         pl.BlockSpec((1, gw), lambda g, i: (0, g)), pl.BlockSpec((tm, gw), lambda g, i: (i, g))],
               out_specs=[pl.BlockSpec((tm, gw), lambda g, i: (i, g)), pl.BlockSpec((tm, gw), lambda g, i: (i, g)),
                          pl.BlockSpec((8, gw), lambda g, i: (0, g))],
               out_shape=[SDS((T, DI), F32), SDS((T, DI), BF16), SDS((8, DI), F32)], name=name)(y, zarr, ng, dyn)


def _shortconv_fwd(proj, off_b, off_c, off_h, C, w8, S, name):
    K = SC_CONV_K
    _, ob, T, tm, tc, tps = _conv_geom((proj, off_b), C, S)
    oc, oh = off_c // tc, off_h // tc

    def body(b_ref, c_ref, h_ref, cp_ref, hp_ref, w_ref, o_ref, buf):
        first = (pl.program_id(0) % tps) == 0
        buf[0:8, :] = jnp.where(first, 0.0, cp_ref[...] * hp_ref[...])
        buf[8:, :] = c_ref[...] * h_ref[...]

        def strip(r0, carry):
            conv = _taps(buf[pl.ds(r0, STRIP + 8), :], w_ref, K, 8, STRIP)
            o_ref[pl.ds(r0, STRIP), :] = (b_ref[pl.ds(r0, STRIP), :] * conv).astype(BF16)
            return carry

        _strips(tm, strip)

    blk = lambda o: pl.BlockSpec((tm, tc), lambda i, j: (i, o + j))
    return _pc(body, grid=(T // tm, C // tc),
               in_specs=[blk(ob), blk(oc), blk(oh), _prev_spec(tm, tc, oc, "ij"), _prev_spec(tm, tc, oh, "ij"),
                         pl.BlockSpec((8, tc), lambda i, j: (0, j))],
               out_specs=pl.BlockSpec((tm, tc), lambda i, j: (i, j)), out_shape=SDS((T, C), BF16), name=name,
               scratch=(pltpu.VMEM((tm + 8, tc), F32),))(proj, proj, proj, proj, proj, w8)


def _shortconv_bwd(proj, off_b, off_c, off_h, C, w8, dsc, S, name):
    K = SC_CONV_K
    _, ob, T, tm, tc, tps = _conv_geom((proj, off_b), C, S)
    oc, oh = off_c // tc, off_h // tc

    def body(b_ref, c_ref, h_ref, cp_ref, hp_ref, bn_ref, d_ref, dn_ref, w_ref,
             db_ref, dc_ref, dh_ref, dw_ref, buf, buf2):
        i = pl.program_id(1)

        @pl.when(i == 0)
        def _():
            dw_ref[...] = jnp.zeros_like(dw_ref)

        first = (i % tps) == 0
        last = (i % tps) == tps - 1
        buf[0:8, :] = jnp.where(first, 0.0, cp_ref[...] * hp_ref[...])
        buf[8:, :] = c_ref[...] * h_ref[...]
        buf2[0:tm, :] = d_ref[...] * b_ref[...]
        buf2[tm:tm + 8, :] = jnp.where(last, 0.0, dn_ref[...] * bn_ref[...])

        def strip(r0, carry):
            rows = pl.ds(r0, STRIP)
            vwin = buf[pl.ds(r0, STRIP + 8), :]
            vs = [vwin[8 - j:8 - j + STRIP] for j in range(K)]
            conv = vs[0] * w_ref[K - 1:K, :]
            for j in range(1, K):
                conv = conv + vs[j] * w_ref[K - 1 - j:K - j, :]
            db_ref[rows, :] = (d_ref[rows, :] * conv).astype(BF16)
            dwin = buf2[pl.ds(r0, STRIP + 8), :]
            dv = _taps_t(dwin, w_ref, K, STRIP)
            dc_ref[rows, :] = (dv * h_ref[rows, :]).astype(BF16)
            dh_ref[rows, :] = (dv * c_ref[rows, :]).astype(BF16)
            dconv = dwin[0:STRIP]
            sums = [_fold8(dconv * vs[K - 1 - k]) for k in range(K)]
            return tuple(c + s for c, s in zip(carry, sums))

        acc = _strips(tm, strip, tuple(jnp.zeros((8, tc), F32) for _ in range(K)))
        dw_ref[...] += _rows8([jnp.sum(a, axis=0, keepdims=True) for a in acc])

    blk = lambda o: pl.BlockSpec((tm, tc), lambda j, i: (i, o + j))
    out = pl.BlockSpec((tm, tc), lambda j, i: (i, j))
    return _pc(body, grid=(C // tc, T // tm),
               in_specs=[blk(ob), blk(oc), blk(oh), _prev_spec(tm, tc, oc, "ji"), _prev_spec(tm, tc, oh, "ji"),
                         _next_spec(T, tm, tc, ob, "ji"), blk(0), _next_spec(T, tm, tc, 0, "ji"),
                         pl.BlockSpec((8, tc), lambda j, i: (0, j))],
               out_specs=[out, out, out, pl.BlockSpec((8, tc), lambda j, i: (0, j))],
               out_shape=[SDS((T, C), BF16)] * 3 + [SDS((8, C), F32)], name=name,
               scratch=(pltpu.VMEM((tm + 8, tc), F32), pltpu.VMEM((tm + 8, tc), F32)))(
                   proj, proj, proj, proj, proj, proj, dsc, dsc, w8)


def _merge_fwd(proj, off_g1, off_g2, y1, y2, S, name):
    T, D = y1.shape
    tm = _row_tile(S)
    o1, o2 = off_g1 // D, off_g2 // D
    assert off_g1 % D == 0 and off_g2 % D == 0

    def body(g1_ref, g2_ref, y1_ref, y2_ref, o_ref):
        def strip(r0, carry):
            rows = pl.ds(r0, STRIP)
            o_ref[rows, :] = (jax.nn.sigmoid(g1_ref[rows, :]) * y1_ref[rows, :]
                              + jax.nn.sigmoid(g2_ref[rows, :]) * y2_ref[rows, :]).astype(BF16)
            return carry

        _strips(tm, strip)

    row = pl.BlockSpec((tm, D), lambda i: (i, 0))
    return _pc(body, grid=(T // tm,),
               in_specs=[pl.BlockSpec((tm, D), lambda i: (i, o1)), pl.BlockSpec((tm, D), lambda i: (i, o2)), row, row],
               out_specs=row, out_shape=SDS((T, D), BF16), name=name)(proj, proj, y1, y2)


def _merge_bwd(proj, off_g1, off_g2, y1, y2, dm, S, name):
    T, D = y1.shape
    tm = _row_tile(S)
    o1, o2 = off_g1 // D, off_g2 // D

    def body(g1_ref, g2_ref, y1_ref, y2_ref, d_ref, dy1_ref, dy2_ref, dg1_ref, dg2_ref):
        def strip(r0, carry):
            rows = pl.ds(r0, STRIP)
            d = d_ref[rows, :]
            s1, s2 = jax.nn.sigmoid(g1_ref[rows, :]), jax.nn.sigmoid(g2_ref[rows, :])
            dy1_ref[rows, :] = (d * s1).astype(BF16)
            dy2_ref[rows, :] = (d * s2).astype(BF16)
            dg1_ref[rows, :] = (d * y1_ref[rows, :] * s1 * (1.0 - s1)).astype(BF16)
            dg2_ref[rows, :] = (d * y2_ref[rows, :] * s2 * (1.0 - s2)).astype(BF16)
            return carry

        _strips(tm, strip)

    row = pl.BlockSpec((tm, D), lambda i: (i, 0))
    return _pc(body, grid=(T // tm,),
               in_specs=[pl.BlockSpec((tm, D), lambda i: (i, o1)), pl.BlockSpec((tm, D), lambda i: (i, o2)), row, row, row],
               out_specs=[row] * 4, out_shape=[SDS((T, D), BF16)] * 4, name=name)(proj, proj, y1, y2, dm)


def _pad8(w):
    return jnp.pad(w, ((0, 8 - w.shape[0]), (0, 0)))


def _dims(w):
    D = w["mix_pre_g"].shape[-1]
    DI = w["ssd_norm_g"].shape[-1]
    H = w["ssd_dt_bias"].shape[-1]
    conv_dim = w["ssd_conv_b"].shape[-1]
    G = (conv_dim - DI) // (2 * N_STATE)
    F = w["w_down"].shape[0]
    return dict(D=D, DI=DI, H=H, P=DI // H, G=G, R=H // G, GN=G * N_STATE, CD=conv_dim, F=F)


def _proj_layout(d):
    D, DI, CD, H = d["D"], d["DI"], d["CD"], d["H"]
    o = dict(z=0, xbc=DI, scb=DI + CD, scc=DI + CD + D, sch=DI + CD + 2 * D, g1=DI + CD + 3 * D, g2=DI + CD + 4 * D,
             dt=DI + CD + 5 * D)
    o["sb"] = math.gcd(SEG_BLK, D, DI, d["GN"])
    assert o["sb"] % 128 == 0 and H <= o["sb"]
    o["np"] = o["dt"] + o["sb"]
    return o


def _glu_perm(a, F, inverse=False):
    lead = a.shape[:-1]
    nb = F // GLU_W
    if not inverse:
        return a.reshape(*lead, 2, nb, GLU_W).swapaxes(-3, -2).reshape(*lead, 2 * F)
    return a.reshape(*lead, nb, 2, GLU_W).swapaxes(-3, -2).reshape(*lead, 2 * F)


def _glu_perm_rows(a, F, inverse=False):
    nb, D = F // GLU_W, a.shape[1]
    shape = (nb, 2, GLU_W, D) if inverse else (2, nb, GLU_W, D)
    return a.reshape(shape).swapaxes(0, 1).reshape(2 * F, D)


def _prep_layer(w):
    d = _dims(w)
    D, DI, CD, H, G, R, F = d["D"], d["DI"], d["CD"], d["H"], d["G"], d["R"], d["F"]
    lay = _proj_layout(d)
    w_in = w["w_in"]
    used = lay["dt"] + H
    wcat = jnp.concatenate([w_in[:DI + CD], w_in[DI + CD + H:], w_in[DI + CD:DI + CD + H],
                            jnp.zeros((lay["np"] - used, D), w_in.dtype)], axis=0)
    hp = jnp.stack([w["ssd_dt_bias"], w["ssd_a_log"], w["ssd_d"]], 0).astype(F32)
    hpc = jnp.pad(hp.reshape(3, G, R).transpose(1, 0, 2), ((0, 0), (0, 5), (0, 0)))
    hpr = jnp.pad(hp[:2].reshape(2, G, R).transpose(1, 2, 0), ((0, 0), (0, 0), (0, 6)))
    row = lambda v: v.reshape(1, -1).astype(F32)
    return dict(
        d=d, lay=lay, ada_w=w["ada_w"].astype(BF16), ada_b=row(w["ada_b"]),
        mix_pre_g=row(w["mix_pre_g"]), mix_post_g=row(w["mix_post_g"]), wcat=wcat.astype(BF16),
        ssd_conv_w=_pad8(w["ssd_conv_w"].astype(F32)), ssd_conv_b=row(w["ssd_conv_b"]), hpc=hpc, hpr=hpr,
        ssd_norm_g=row(w["ssd_norm_g"]), w_ssd_out=w["w_ssd_out"].astype(BF16),
        sc_conv_w=_pad8(w["sc_conv_w"].astype(F32)), w_sc_out=w["w_sc_out"].astype(BF16), w_o=w["w_o"].astype(BF16),
        ffn_pre_g=row(w["ffn_pre_g"]), ffn_post_g=row(w["ffn_post_g"]),
        w_up=_glu_perm_rows(w["w_up"], F).astype(BF16), ffn_conv_w=_pad8(_glu_perm(w["ffn_conv_w"].astype(F32), F)),
        ffn_conv_b=_glu_perm(row(w["ffn_conv_b"]), F), w_down=w["w_down"].astype(BF16))


def _dt_layouts(proj, lay, d):
    T = proj.shape[0]
    dt = proj[:, lay["dt"]:lay["dt"] + d["H"]].reshape(T, d["G"], d["R"])
    return dt.transpose(1, 0, 2), dt.transpose(1, 2, 0)


def _layer_fwd(x, c8, p, S, li, gather=None):
    d, lay = p["d"], p["lay"]
    D, DI, G, R, P, GN, CD = d["D"], d["DI"], d["G"], d["R"], d["P"], d["GN"], d["CD"]
    nb = x.shape[0] // S
    nm = lambda s: f"l{li}_{s}"
    mod, cact = _modk(c8, p["ada_w"], p["ada_b"], nm("mod"))
    mod3 = mod[:nb].reshape(nb, 1, 6 * D)
    h = _norm_mod(x, p["mix_pre_g"], mod3, 1, 0, S, nm("norm1"))
    proj = _mm(h, p["wcat"], "nt", F32, nm("mm_in"), caps=(1024, 1536, 2048))
    pre = _conv_fwd((proj, lay["xbc"]), CD, p["ssd_conv_w"], p["ssd_conv_b"], SSD_CONV_K, S, nm("ssdconv"))
    dtc, dtr = _dt_layouts(proj, lay, d)
    offs = (0, DI, DI + GN)
    gathered = None
    if gather is None:
        y, hs = _ssd_fwd(pre, offs, dtc, dtr, p["hpc"], p["hpr"], G, R, P, S, nm("ssd"))
    else:
        (y, hs), (gathered,) = _ssd_fwd(pre, offs, dtc, dtr, p["hpc"], p["hpr"], G, R, P, S, nm("ssd"),
                                        side=_side_gather_direct(gather))
    yn = _gate_norm_fwd(y, (proj, lay["z"]), p["ssd_norm_g"], G, S, nm("gnorm"))
    sc = _shortconv_fwd(proj, lay["scb"], lay["scc"], lay["sch"], D, p["sc_conv_w"], S, nm("sconv"))
    if gather is None:
        y_ssd = _mm(yn, p["w_ssd_out"], "nn", F32, nm("mm_ssdout"))
    else:
        y_ssd, (gathered,) = _mm(yn, p["w_ssd_out"], "nn", F32, nm("mm_ssdout"), side=_side_gather_forward(gathered))
    y_sc = _mm(sc, p["w_sc_out"], "nn", F32, nm("mm_scout"))
    m = _merge_fwd(proj, lay["g1"], lay["g2"], y_ssd, y_sc, S, nm("merge"))
    mix = _mm(m, p["w_o"], "nn", F32, nm("mm_o"))
    x1 = _resid_post(x, mix, mod3, 2, p["mix_post_g"], S, nm("post1"))
    h2 = _norm_mod(x1, p["ffn_pre_g"], mod3, 4, 3, S, nm("norm2"))
    uu = _mm(h2, p["w_up"], "nt", F32, nm("mm_up"), caps=(1024, 1408, 2048))
    a = _ffn_act_fwd(uu, p["ffn_conv_w"], p["ffn_conv_b"], S, nm("ffnact"))
    f = _mm(a, p["w_down"], "nn", F32, nm("mm_down"), caps=(1024, 1024, 1408))
    x2 = _resid_post(x1, f, mod3, 5, p["ffn_post_g"], S, nm("post2"))
    saved = dict(x=x, h=h, proj=proj, pre=pre, dtc=dtc, dtr=dtr, y=y, hs=hs, yn=yn, sc=sc, y_ssd=y_ssd, y_sc=y_sc,
                 m=m, mix=mix, x1=x1, h2=h2, uu=uu, a=a, f=f, mod3=mod3, cact=cact)
    return x2, saved, gathered


def _seq_sum(acc, nb):
    return acc.reshape(nb, 8, -1)[:, 0, :]


def _layer_bwd(dx2, p, s, S, li, chip_sums=None, early=None):
    d, lay = p["d"], p["lay"]
    D, DI, G, R, P, GN, CD, H, F = d["D"], d["DI"], d["G"], d["R"], d["P"], d["GN"], d["CD"], d["H"], d["F"]
    nb = dx2.shape[0] // S
    nm = lambda t: f"l{li}_{t}"
    mod3 = s["mod3"]
    g = {}
    exchanged = None
    df, dgt2, dpg2 = _post_bwd(s["f"], mod3, 5, p["ffn_post_g"], dx2, S, nm("post2_b"))
    g["ffn_post_g"] = dpg2[0]
    da = _mm(df, p["w_down"], "nt", BF16, nm("mm_down_bi"), caps=(1024, 1408, 2048))
    g["w_down"] = _mm(s["a"], df, "tn", WGRAD,nm("mm_down_bw"), caps=(1408, 1024, 1024))
    if chip_sums is None:
        duu, cw = _ffn_act_bwd(s["uu"], da, p["ffn_conv_w"], p["ffn_conv_b"], S, nm("ffnact_b"))
    else:
        (duu, cw), (exchanged,) = _ffn_act_bwd(s["uu"], da, p["ffn_conv_w"], p["ffn_conv_b"], S, nm("ffnact_b"),
                                               side=_side_chip_exchange(chip_sums))
    g["ffn_conv_w"] = _glu_perm(cw[:FFN_CONV_K], F, inverse=True)
    g["ffn_conv_b"] = _glu_perm(cw[FFN_CONV_K], F, inverse=True)
    dh2 = _mm(duu, p["w_up"], "nn", F32, nm("mm_up_bi"), caps=(1024, 1024, 2816))
    g["w_up"] = _glu_perm_rows(_mm(duu, s["h2"], "tn", WGRAD,nm("mm_up_bw"), caps=(1408, 1024, 1024)), F, inverse=True)
    dx1, dg2, dsc2, dsh2 = _pre_bwd(s["x1"], p["ffn_pre_g"], mod3, 4, dh2, dx2, S, nm("norm2_b"))
    g["ffn_pre_g"] = dg2[0]
    dmix, dgt1, dpg1 = _post_bwd(s["mix"], mod3, 2, p["mix_post_g"], dx1, S, nm("post1_b"))
    g["mix_post_g"] = dpg1[0]
    dm = _mm(dmix, p["w_o"], "nt", F32, nm("mm_o_bi"))
    g["w_o"] = _mm(s["m"], dmix, "tn", WGRAD,nm("mm_o_bw"))
    proj = s["proj"]
    dy_ssd, dy_sc, dg1, dg2_ = _merge_bwd(proj, lay["g1"], lay["g2"], s["y_ssd"], s["y_sc"], dm, S, nm("merge_b"))
    dyn = _mm(dy_ssd, p["w_ssd_out"], "nt", F32, nm("mm_ssdout_bi"))
    g["w_ssd_out"] = _mm(s["yn"], dy_ssd, "tn", WGRAD,nm("mm_ssdout_bw"))
    dsc = _mm(dy_sc, p["w_sc_out"], "nt", F32, nm("mm_scout_bi"))
    g["w_sc_out"] = _mm(s["sc"], dy_sc, "tn", WGRAD,nm("mm_scout_bw"))
    dscb, dscc, dsch, scw = _shortconv_bwd(proj, lay["scb"], lay["scc"], lay["sch"], D, p["sc_conv_w"], dsc, S, nm("sconv_b"))
    g["sc_conv_w"] = scw[:SC_CONV_K]
    dy, dz, dng = _gate_norm_bwd(s["y"], (proj, lay["z"]), p["ssd_norm_g"], dyn, G, S, nm("gnorm_b"))
    g["ssd_norm_g"] = dng[0]
    offs = (0, DI, DI + GN)
    early_side = None if early is None else _side_chip_exchange(early(g))
    res = _ssd_bwd(s["pre"], offs, s["dtc"], s["dtr"], p["hpc"], p["hpr"], s["hs"], dy, G, R, P, S, nm("ssd_b"),
                   side=early_side)
    (dpx, dpb, dpc, ddt, hpg), early_got = res if early is not None else (res, None)
    g["ssd_dt_bias"], g["ssd_a_log"], g["ssd_d"] = hpg[:, 0, :].reshape(H), hpg[:, 1, :].reshape(H), hpg[:, 2, :].reshape(H)
    cws, dxbc = [], []
    for name, darr, off, C in (("x", dpx, 0, DI), ("b", dpb, DI, GN), ("c", dpc, DI + GN, GN)):
        w8 = p["ssd_conv_w"][:, off:off + C]
        cws.append(_conv_bwd_w((darr, 0), (proj, lay["xbc"] + off), C, SSD_CONV_K, S, nm(f"ssdconv_bw_{name}")))
        dxbc.append(_conv_bwd_in((darr, 0), C, w8, SSD_CONV_K, S, BF16, nm(f"ssdconv_bi_{name}")))
    cws = jnp.concatenate(cws, axis=1)
    g["ssd_conv_w"], g["ssd_conv_b"] = cws[:SSD_CONV_K], cws[SSD_CONV_K]
    T = dx2.shape[0]
    ddt_t = jnp.pad(ddt.transpose(1, 0, 2).reshape(T, H).astype(BF16), ((0, 0), (0, lay["sb"] - H)))
    dproj = [dz] + dxbc + [dscb, dscc, dsch, dg1, dg2_, ddt_t]
    dh = _mm_seg(dproj, p["wcat"], "nn", F32, nm("mm_in_bi"), lay["sb"])
    dwcat = _mm_seg(dproj, s["h"], "tn", WGRAD, nm("mm_in_bw"), lay["sb"], tk=1024)
    o = lay
    g["w_in"] = jnp.concatenate([dwcat[o["z"]:o["scb"]], dwcat[o["dt"]:o["dt"] + H], dwcat[o["scb"]:o["dt"]]], axis=0)
    dx, dg1_, dsc1, dsh1 = _pre_bwd(s["x"], p["mix_pre_g"], mod3, 1, dh, dx1, S, nm("norm1_b"))
    g["mix_pre_g"] = dg1_[0]
    dmod = jnp.concatenate([_seq_sum(t, nb) for t in (dsh1, dsc1, dgt1, dsh2, dsc2, dgt2)], axis=1)
    dmod8 = jnp.pad(dmod, ((0, MOD_ROWS - nb), (0, 0)))
    g["ada_b"] = _colsum(dmod8, nm("adab"))
    g["ada_w"] = _mm(dmod8, s["cact"], "tn", WGRAD, nm("mm_ada_bw"), caps=(1536, 1024, 2048))
    return dx, g, exchanged, None if early_got is None else early_got[0]


def _colsum(a8, name):
    rows, C = a8.shape
    tc = _tile(C, 2048)

    def body(a_ref, o_ref):
        o_ref[...] = _bsum(jnp.sum(a_ref[...], axis=0, keepdims=True))

    return _pc(body, grid=(C // tc,), in_specs=[pl.BlockSpec((rows, tc), lambda j: (0, j))],
               out_specs=pl.BlockSpec((8, tc), lambda j: (0, j)), out_shape=SDS((8, C), F32), name=name)(a8)[0]


def _adam(gs, w, m, v, name):
    ns, R, W = gs.shape
    tr = _tile(R, 256, 8)

    def body(g_ref, w_ref, m_ref, v_ref, go_ref, d_ref, mo_ref, vo_ref):
        g = g_ref[0].astype(F32)
        for k in range(1, ns):
            g = g + g_ref[k].astype(F32)
        go_ref[...] = g
        d_ref[...], mo_ref[...], vo_ref[...] = _adam_update(g, w_ref[...], m_ref[...], v_ref[...])

    row = pl.BlockSpec((tr, W), lambda i: (i, 0))
    return _pc(body, grid=(R // tr,), in_specs=[pl.BlockSpec((ns, tr, W), lambda i: (0, i, 0)), row, row, row],
               out_specs=[row] * 4, out_shape=[SDS((R, W), F32)] * 4, name=name)(gs, w, m, v)


def _adam_update(g, w, m, v):
    c1 = 1.0 / (1.0 - ADAM_B1 ** ADAM_STEP)
    c2 = 1.0 / (1.0 - ADAM_B2 ** ADAM_STEP)
    m_ = ADAM_B1 * m + (1.0 - ADAM_B1) * g
    v_ = ADAM_B2 * v + (1.0 - ADAM_B2) * (g * g)
    return -ADAM_LR * ((m_ * c1) / (jnp.sqrt(v_ * c2) + ADAM_EPS) + ADAM_WD * w), m_, v_


def _adam_nat(g, w, m, v, name):
    depth, a, b = w.shape
    tr = _tile(a, 256, 8)

    def body(g_ref, w_ref, m_ref, v_ref, d_ref, mo_ref, vo_ref):
        d_ref[...], mo_ref[...], vo_ref[...] = _adam_update(g_ref[...], w_ref[...], m_ref[...], v_ref[...])

    blk = pl.BlockSpec((None, tr, b), lambda l, i: (l, i, 0))
    return _pc(body, grid=(depth, a // tr), in_specs=[blk] * 4, out_specs=[blk] * 3,
               out_shape=[SDS(w.shape, F32)] * 3, name=name)(g, w, m, v)


def _adam_mid(g, w, m, v, name):
    b, depth, a = w.shape
    tr = 128

    def body(g_ref, w_ref, m_ref, v_ref, d_ref, mo_ref, vo_ref):
        d_ref[...], mo_ref[...], vo_ref[...] = _adam_update(g_ref[...], w_ref[...], m_ref[...], v_ref[...])

    blk = pl.BlockSpec((tr, depth, a), lambda i: (i, 0, 0))
    return _pc(body, grid=(pl.cdiv(b, tr),), in_specs=[blk] * 4, out_specs=[blk] * 3,
               out_shape=[SDS(w.shape, F32)] * 3, name=name)(g, w, m, v)


def _sum_chips(gs, name):
    ns, R, W = gs.shape
    tr = _tile(R, 256, 16)

    def body(g_ref, o_ref):
        acc = g_ref[0].astype(F32)
        for k in range(1, ns):
            acc = acc + g_ref[k].astype(F32)
        o_ref[...] = acc

    return _pc(body, grid=(R // tr,), in_specs=[pl.BlockSpec((ns, tr, W), lambda i: (0, i, 0))],
               out_specs=pl.BlockSpec((tr, W), lambda i: (i, 0)), out_shape=SDS((R, W), F32), name=name)(gs)


HBM_SPEC = pl.BlockSpec(memory_space=pltpu.HBM)
VMEM_SPEC = pl.BlockSpec(memory_space=pltpu.VMEM)


def _dev():
    return lax.axis_index("x"), lax.axis_index("y"), lax.axis_index("c")


def _allgather_big(loc, name):
    R, W = loc.shape

    def body(x_ref, out_ref, send_sems, recv_sems, local_sem):
        x, y, c = _dev()
        me, sibling = (x, y, c), (x, y, 1 - c)
        chips = [(1 - x, y), (x, 1 - y), (1 - x, 1 - y)]

        def slab(px, py, pc):
            return out_ref.at[4 * px + 2 * py + pc]

        def copy(k, block, to, src=None):
            return pltpu.make_async_remote_copy(
                src_ref=slab(*block) if src is None else src, dst_ref=slab(*block),
                send_sem=send_sems.at[k], recv_sem=recv_sems.at[k], device_id=to, device_id_type=MESH)

        mine = pltpu.make_async_copy(x_ref, slab(*me), local_sem)
        mine.start()
        first = [copy(0, me, sibling, src=x_ref)]
        first += [copy(1 + j, me, (*chip, c), src=x_ref) for j, chip in enumerate(chips)]
        for cp in first:
            cp.start()
        passed = [copy(4 + j, (*chip, c), sibling) for j, chip in enumerate(chips)]
        for j, chip in enumerate(chips):
            copy(1 + j, (*chip, c), me).wait_recv()
            passed[j].start()
        copy(0, sibling, me).wait_recv()
        for j, chip in enumerate(chips):
            copy(4 + j, (*chip, 1 - c), me).wait_recv()
        for cp in first + passed:
            cp.wait_send()
        mine.wait()

    return pl.pallas_call(
        body, out_shape=SDS((N_DEV, R, W), loc.dtype), in_specs=[HBM_SPEC], out_specs=HBM_SPEC,
        scratch_shapes=[pltpu.SemaphoreType.DMA((7,)), pltpu.SemaphoreType.DMA((7,)), pltpu.SemaphoreType.DMA],
        name=name)(loc)


def _dma_sems(n):
    return (pltpu.SemaphoreType.DMA((n,)), pltpu.SemaphoreType.DMA((n,)), pltpu.SemaphoreType.DMA)


def _side_gather_direct(loc):
    R, W = loc.shape

    def copies(ins, outs, sems):
        x, y, c = _dev()
        x_ref, out = ins[0], outs[0]
        me = 4 * x + 2 * y + c
        peers = [(x, y, 1 - c), (1 - x, y, c), (x, 1 - y, c), (1 - x, 1 - y, c)]
        mk = lambda k, p, dst: pltpu.make_async_remote_copy(
            src_ref=x_ref, dst_ref=out.at[dst], send_sem=sems[0].at[k], recv_sem=sems[1].at[k], device_id=p,
            device_id_type=MESH)
        sends = [mk(k, p, me) for k, p in enumerate(peers)]
        recvs = [mk(k, p, 4 * p[0] + 2 * p[1] + p[2]) for k, p in enumerate(peers)]
        return sends, recvs, pltpu.make_async_copy(x_ref, out.at[me], sems[2])

    def start(ins, outs, sems):
        sends, _, mine = copies(ins, outs, sems)
        mine.start()
        for cp in sends:
            cp.start()

    def wait(ins, outs, sems):
        sends, recvs, mine = copies(ins, outs, sems)
        for cp in recvs:
            cp.wait_recv()
        for cp in sends:
            cp.wait_send()
        mine.wait()

    return _Side((loc,), (SDS((N_DEV, R, W), loc.dtype),), _dma_sems(4), start, wait)


def _side_gather_forward(buf):
    def copies(ins, outs, sems):
        x, y, c = _dev()
        out = outs[0]
        chips = [(1 - x, y), (x, 1 - y), (1 - x, 1 - y)]
        mk = lambda k, src, dst: pltpu.make_async_remote_copy(
            src_ref=out.at[src], dst_ref=out.at[dst], send_sem=sems[0].at[k], recv_sem=sems[1].at[k],
            device_id=(x, y, 1 - c), device_id_type=MESH)
        mine = [4 * px + 2 * py + c for px, py in chips]
        theirs = [4 * px + 2 * py + (1 - c) for px, py in chips]
        return [mk(k, s, s) for k, s in enumerate(mine)], [mk(k, s, t) for k, (s, t) in enumerate(zip(mine, theirs))]

    def start(ins, outs, sems):
        for cp in copies(ins, outs, sems)[0]:
            cp.start()

    def wait(ins, outs, sems):
        sends, recvs = copies(ins, outs, sems)
        for cp in recvs:
            cp.wait_recv()
        for cp in sends:
            cp.wait_send()

    return _Side((buf,), (SDS(buf.shape, buf.dtype),), _dma_sems(3)[:2], start, wait, {0: 0})


def _side_chip_exchange(p):
    def copies(ins, outs, sems):
        x, y, c = _dev()
        p_ref, out = ins[0], outs[0]
        j0 = 2 * x + y
        chips = [(1 - x, y), (x, 1 - y), (1 - x, 1 - y)]
        mk = lambda k, chip, src, dst: pltpu.make_async_remote_copy(
            src_ref=p_ref.at[src], dst_ref=out.at[dst], send_sem=sems[0].at[k], recv_sem=sems[1].at[k],
            device_id=(*chip, c), device_id_type=MESH)
        sends = [mk(k, chip, 2 * chip[0] + chip[1], j0) for k, chip in enumerate(chips)]
        recvs = [mk(k, chip, j0, 2 * chip[0] + chip[1]) for k, chip in enumerate(chips)]
        return sends, recvs, pltpu.make_async_copy(p_ref.at[j0], out.at[j0], sems[2])

    def start(ins, outs, sems):
        sends, _, mine = copies(ins, outs, sems)
        mine.start()
        for cp in sends:
            cp.start()

    def wait(ins, outs, sems):
        sends, recvs, mine = copies(ins, outs, sems)
        for cp in recvs:
            cp.wait_recv()
        for cp in sends:
            cp.wait_send()
        mine.wait()

    return _Side((p,), (SDS(p.shape, p.dtype),), _dma_sems(3), start, wait)


def _rs_pair_exchange(g, name):
    nd, R, W = g.shape
    nj = nd // 2

    def body(g_ref, out_ref, send_sems, recv_sems):
        x, y, c = _dev()
        cps = [pltpu.make_async_remote_copy(src_ref=g_ref.at[2 * j + (1 - c)], dst_ref=out_ref.at[j],
                                            send_sem=send_sems.at[j], recv_sem=recv_sems.at[j],
                                            device_id=(x, y, 1 - c), device_id_type=MESH) for j in range(nj)]
        for cp in cps:
            cp.start()
        for cp in cps:
            cp.wait()

    return pl.pallas_call(
        body, out_shape=SDS((nj, R, W), g.dtype), in_specs=[HBM_SPEC], out_specs=HBM_SPEC,
        scratch_shapes=[pltpu.SemaphoreType.DMA((nj,)), pltpu.SemaphoreType.DMA((nj,))], name=name)(g)


def _add_pairs(g, ra, name):
    nd, R, W = g.shape
    nj = nd // 2
    tr = _tile(R, 256, 8)
    cidx = lax.axis_index("c").astype(jnp.int32).reshape(1)

    def body(c_ref, a_ref, b_ref, o_ref):
        o_ref[...] = (a_ref[...].astype(F32) + b_ref[...].astype(F32)).astype(o_ref.dtype)

    gs = pltpu.PrefetchScalarGridSpec(
        num_scalar_prefetch=1, grid=(nj, R // tr),
        in_specs=[pl.BlockSpec((None, tr, W), lambda j, i, cr: (2 * j + cr[0], i, 0)),
                  pl.BlockSpec((None, tr, W), lambda j, i, cr: (j, i, 0))],
        out_specs=pl.BlockSpec((None, tr, W), lambda j, i, cr: (j, i, 0)))
    return pl.pallas_call(body, grid_spec=gs, out_shape=SDS((nj, R, W), g.dtype), name=name,
                          compiler_params=pltpu.CompilerParams(vmem_limit_bytes=VMEM_LIMIT))(cidx, g, ra)


def _rs_chip_exchange(p, name):
    nj, R, W = p.shape

    def body(p_ref, out_ref, send_sems, recv_sems, local_sem):
        x, y, c = _dev()
        j0 = 2 * x + y
        chips = [(1 - x, y), (x, 1 - y), (1 - x, 1 - y)]
        mine = pltpu.make_async_copy(p_ref.at[j0], out_ref.at[j0], local_sem)
        mine.start()

        def copy(k, chip):
            return pltpu.make_async_remote_copy(
                src_ref=p_ref.at[2 * chip[0] + chip[1]], dst_ref=out_ref.at[j0],
                send_sem=send_sems.at[k], recv_sem=recv_sems.at[k], device_id=(*chip, c), device_id_type=MESH)

        sent = [copy(k, chip) for k, chip in enumerate(chips)]
        for cp in sent:
            cp.start()
        for k, chip in enumerate(chips):
            pltpu.make_async_remote_copy(
                src_ref=p_ref.at[j0], dst_ref=out_ref.at[2 * chip[0] + chip[1]],
                send_sem=send_sems.at[k], recv_sem=recv_sems.at[k], device_id=(*chip, c), device_id_type=MESH).wait_recv()
        for cp in sent:
            cp.wait_send()
        mine.wait()

    return pl.pallas_call(
        body, out_shape=SDS((nj, R, W), p.dtype), in_specs=[HBM_SPEC], out_specs=HBM_SPEC,
        scratch_shapes=[pltpu.SemaphoreType.DMA((3,)), pltpu.SemaphoreType.DMA((3,)), pltpu.SemaphoreType.DMA],
        name=name)(p)


def _allgather_small(v, name):
    R, W = v.shape

    def body(v_ref, out_ref, send_sems, recv_sems, local_sem):
        x, y, c = _dev()
        mine = pltpu.make_async_copy(v_ref, out_ref.at[4 * x + 2 * y + c], local_sem)
        mine.start()
        peers = []
        for k in range(1, N_DEV):
            px = 1 - x if k & 4 else x
            py = 1 - y if k & 2 else y
            pc_ = 1 - c if k & 1 else c
            peers.append((px, py, pc_))
        sent = [pltpu.make_async_remote_copy(
            src_ref=v_ref, dst_ref=out_ref.at[4 * x + 2 * y + c], send_sem=send_sems.at[k], recv_sem=recv_sems.at[k],
            device_id=peer, device_id_type=MESH) for k, peer in enumerate(peers)]
        for cp in sent:
            cp.start()
        for k, (px, py, pc_) in enumerate(peers):
            pltpu.make_async_remote_copy(
                src_ref=v_ref, dst_ref=out_ref.at[4 * px + 2 * py + pc_], send_sem=send_sems.at[k],
                recv_sem=recv_sems.at[k], device_id=(px, py, pc_), device_id_type=MESH).wait_recv()
        for cp in sent:
            cp.wait_send()
        mine.wait()

    return pl.pallas_call(
        body, out_shape=SDS((N_DEV, R, W), v.dtype), in_specs=[VMEM_SPEC], out_specs=VMEM_SPEC,
        scratch_shapes=[pltpu.SemaphoreType.DMA((7,)), pltpu.SemaphoreType.DMA((7,)), pltpu.SemaphoreType.DMA],
        name=name)(v)


def _sum_slabs(a, name):
    ns, R, W = a.shape

    def body(a_ref, o_ref):
        acc = a_ref[0]
        for k in range(1, ns):
            acc = acc + a_ref[k]
        o_ref[...] = acc

    return pl.pallas_call(body, out_shape=SDS((R, W), a.dtype), in_specs=[VMEM_SPEC], out_specs=VMEM_SPEC, name=name)(a)


BIG = (("ada_w", "col"), ("w_in", "col"), ("w_ssd_out", "row"), ("w_sc_out", "row"), ("w_o", "row"), ("w_up", "col"),
       ("w_down", "row"))
EARLY = ("w_ssd_out", "w_sc_out", "w_o", "w_up", "w_down")
LATE = ("ada_w", "w_in")
MID_LAYOUT = ("w_in",)
SWAP_LAYOUT = ("w_up",)
CONVW = ("ssd_conv_w", "sc_conv_w", "ffn_conv_w")
REPL = ("ada_b", "mix_pre_g", "mix_post_g", "ssd_conv_b", "ssd_dt_bias", "ssd_a_log", "ssd_d", "ssd_norm_g", "ffn_pre_g",
        "ffn_post_g", "ffn_conv_b")
WEIGHTS = ("ada_w", "ada_b", "mix_pre_g", "mix_post_g", "w_in", "ssd_conv_w", "ssd_conv_b", "ssd_dt_bias", "ssd_a_log",
           "ssd_d", "ssd_norm_g", "w_ssd_out", "sc_conv_w", "w_sc_out", "w_o", "ffn_pre_g", "ffn_post_g", "w_up",
           "ffn_conv_w", "ffn_conv_b", "w_down")


def _pad_rows(a, mult):
    r = a.shape[-2]
    pad = -r % mult
    return a if pad == 0 else jnp.pad(a, [(0, 0)] * (a.ndim - 2) + [(0, pad), (0, 0)])


def _flat_rows(parts, mult):
    flat = jnp.concatenate([p.reshape(-1) for p in parts])
    flat = jnp.pad(flat, (0, -flat.shape[0] % ROW_W))
    return _pad_rows(flat.reshape(-1, ROW_W), mult)


def _unflat(buf, shapes):
    flat = buf.reshape(-1)
    out, o = [], 0
    for shp in shapes:
        n = 1
        for s in shp:
            n *= s
        out.append(flat[o:o + n].reshape(shp))
        o += n
    return out


def _pack_big_local(get, l):
    return [_pad_rows((get(n)[l].T if kind == "col" else get(n)[l]).reshape(-1, ROW_W), SLAB_ALIGN) for n, kind in BIG]


def _big_rows(shapes, names=None):
    out, o = {}, 0
    for n in (names if names is not None else [n for n, _ in BIG]):
        r = shapes[n][1] * shapes[n][2] // ROW_W
        out[n] = (o, o + r)
        o += -(-r // SLAB_ALIGN) * SLAB_ALIGN
    return out, o


def kernel(x, c, ada_w, ada_b, mix_pre_g, mix_post_g, w_in, ssd_conv_w, ssd_conv_b, ssd_dt_bias, ssd_a_log, ssd_d, ssd_norm_g, w_ssd_out, sc_conv_w, w_sc_out, w_o, ffn_pre_g, ffn_post_g, w_up, ffn_conv_w, ffn_conv_b, w_down, loss_target, m_ada_w, m_ada_b, m_mix_pre_g, m_mix_post_g, m_w_in, m_ssd_conv_w, m_ssd_conv_b, m_ssd_dt_bias, m_ssd_a_log, m_ssd_d, m_ssd_norm_g, m_w_ssd_out, m_sc_conv_w, m_w_sc_out, m_w_o, m_ffn_pre_g, m_ffn_post_g, m_w_up, m_ffn_conv_w, m_ffn_conv_b, m_w_down, v_ada_w, v_ada_b, v_mix_pre_g, v_mix_post_g, v_w_in, v_ssd_conv_w, v_ssd_conv_b, v_ssd_dt_bias, v_ssd_a_log, v_ssd_d, v_ssd_norm_g, v_w_ssd_out, v_sc_conv_w, v_w_sc_out, v_w_o, v_ffn_pre_g, v_ffn_post_g, v_w_up, v_ffn_conv_w, v_ffn_conv_b, v_w_down):
    wl = dict(zip(WEIGHTS, (ada_w, ada_b, mix_pre_g, mix_post_g, w_in, ssd_conv_w, ssd_conv_b, ssd_dt_bias, ssd_a_log,
                            ssd_d, ssd_norm_g, w_ssd_out, sc_conv_w, w_sc_out, w_o, ffn_pre_g, ffn_post_g, w_up,
                            ffn_conv_w, ffn_conv_b, w_down)))
    ml = dict(zip(WEIGHTS, (m_ada_w, m_ada_b, m_mix_pre_g, m_mix_post_g, m_w_in, m_ssd_conv_w, m_ssd_conv_b,
                            m_ssd_dt_bias, m_ssd_a_log, m_ssd_d, m_ssd_norm_g, m_w_ssd_out, m_sc_conv_w, m_w_sc_out, m_w_o,
                            m_ffn_pre_g, m_ffn_post_g, m_w_up, m_ffn_conv_w, m_ffn_conv_b, m_w_down)))
    vl = dict(zip(WEIGHTS, (v_ada_w, v_ada_b, v_mix_pre_g, v_mix_post_g, v_w_in, v_ssd_conv_w, v_ssd_conv_b,
                            v_ssd_dt_bias, v_ssd_a_log, v_ssd_d, v_ssd_norm_g, v_w_ssd_out, v_sc_conv_w, v_w_sc_out, v_w_o,
                            v_ffn_pre_g, v_ffn_post_g, v_w_up, v_ffn_conv_w, v_ffn_conv_b, v_w_down)))
    depth = ada_w.shape[0]
    shapes = {n: wl[n].shape for n in WEIGHTS}
    me = 4 * lax.axis_index("x") + 2 * lax.axis_index("y") + lax.axis_index("c")

    rows, n_big = _big_rows(shapes)
    conv_flat = jnp.concatenate([wl[n][l].reshape(-1) for l in range(depth) for n in CONVW])
    n_conv = conv_flat.shape[0]
    conv_flat = jnp.pad(conv_flat, (0, -n_conv % (ROW_W // 2)))
    conv_rows = lax.bitcast_convert_type(conv_flat, BF16).reshape(-1, ROW_W)

    def local_rows(l):
        pieces = _pack_big_local(lambda n: wl[n].astype(BF16), l) + ([conv_rows] if l == 0 else [])
        return _pad_rows(jnp.concatenate(pieces, axis=0), ROW_PAD)

    def layer_weights(l, gathered):
        w = {n: wl[n][l] for n in REPL}
        for n, kind in BIG:
            a, b = shapes[n][1], shapes[n][2]
            blk = gathered[:, rows[n][0]:rows[n][1]]
            w[n] = blk.reshape(N_DEV * b, a) if kind == "col" else blk.reshape(N_DEV * a, b)
        for n in CONVW:
            w[n] = conv_full[(l, n)]
        return w

    gathered = _allgather_big(local_rows(0), "allgather_weights")
    conv_all = lax.bitcast_convert_type(
        gathered[:, n_big:n_big + conv_rows.shape[0]].reshape(N_DEV, -1, 2), F32)[:, :n_conv]
    conv_full, o = {}, 0
    for l in range(depth):
        for n in CONVW:
            k, cl = shapes[n][1], shapes[n][2]
            conv_full[(l, n)] = conv_all[:, o:o + k * cl].reshape(N_DEV, k, cl).transpose(1, 0, 2).reshape(k, N_DEV * cl)
            o += k * cl

    nb, S, D = x.shape
    T = nb * S
    act = x.reshape(T, D)
    c8 = jnp.pad(c, ((0, MOD_ROWS - nb), (0, 0)))
    preps, saved = [], []
    for l in range(depth):
        preps.append(_prep_layer(layer_weights(l, gathered)))
        act, s, gathered = _layer_fwd(act, c8, preps[l], S, l, gather=local_rows(l + 1) if l + 1 < depth else None)
        saved.append(s)
    dy, lacc = _loss(act, loss_target.reshape(T, D), S, "loss")
    loss_loc = lacc[0, 0]

    group_rows = {grp: _big_rows(shapes, names) for grp, names in (("early", EARLY), ("late", LATE))}

    def pair_sums(g, grp, names, l):
        slabs = [_pad_rows(g[n].astype(BF16).reshape(N_DEV, -1, ROW_W), SLAB_ALIGN) for n in names]
        slabs.append(jnp.zeros((N_DEV, -group_rows[grp][1] % ROW_PAD, ROW_W), BF16))
        gslab = jnp.concatenate(slabs, axis=1)
        from_sibling = _rs_pair_exchange(gslab, f"rs_pair_exchange_{grp}_l{l}")
        return _add_pairs(gslab, from_sibling, f"rs_pair_add_{grp}_l{l}")

    grads, pending = [None] * depth, None
    from_chips = {"early": [None] * depth, "late": [None] * depth}
    for l in reversed(range(depth)):
        dy, grads[l], got, from_chips["early"][l] = _layer_bwd(
            dy, preps[l], saved[l], S, l, chip_sums=pending, early=lambda g, l=l: pair_sums(g, "early", EARLY, l))
        if pending is not None:
            from_chips["late"][l + 1] = got
        pending = pair_sums(grads[l], "late", LATE, l)
    from_chips["late"][0] = _rs_chip_exchange(pending, "rs_chip_exchange")
    dx = dy.reshape(nb, S, D)
    g_sums = {grp: [_sum_chips(from_chips[grp][l], f"rs_chip_sum_{grp}_l{l}") for l in range(depth)]
              for grp in ("early", "late")}

    def slab_of(l, n, kind):
        grp = "early" if n in EARLY else "late"
        r0, r1 = group_rows[grp][0][n]
        a, b = shapes[n][1], shapes[n][2]
        return g_sums[grp][l][r0:r1].reshape((b, a) if kind == "col" else (a, b))

    g_big, d_big, m_big, v_big = {}, {}, {}, {}
    for n, kind in BIG:
        if n in MID_LAYOUT:
            gm = jnp.stack([slab_of(l, n, kind) for l in range(depth)], axis=1)
            fwd, back = (lambda t: t.transpose(2, 0, 1)), (lambda t: t.transpose(1, 2, 0))
            res = [gm] + list(_adam_mid(gm, fwd(wl[n]), fwd(ml[n]), fwd(vl[n]), f"adam_{n}"))
        else:
            gt = jnp.stack([slab_of(l, n, kind) for l in range(depth)])
            fwd = back = (lambda t: t.swapaxes(1, 2)) if kind == "col" else (lambda t: t)
            if n in SWAP_LAYOUT:
                res = [gt] + list(_adam_nat(gt, fwd(wl[n]), fwd(ml[n]), fwd(vl[n]), f"adam_{n}"))
            else:
                gn = back(gt)
                res, back = [gn] + list(_adam_nat(gn, wl[n], ml[n], vl[n], f"adam_{n}")), (lambda t: t)
        g_big[n], d_big[n], m_big[n], v_big[n] = [back(t) for t in res]

    parts = [jnp.broadcast_to(loss_loc, (ROW_W,))]
    small_shapes = [(ROW_W,)]
    for l in range(depth):
        for n in REPL + CONVW:
            parts.append(grads[l][n])
            small_shapes.append(tuple(grads[l][n].shape))
    total = _sum_slabs(_allgather_small(_flat_rows(parts, 8), "allgather_small"), "sum_small")
    pieces = _unflat(total, small_shapes)
    loss = pieces[0][0]
    g_small, i = {}, 1
    for l in range(depth):
        for n in REPL + CONVW:
            gp = pieces[i]
            i += 1
            if n in CONVW:
                gp = lax.dynamic_slice_in_dim(gp, me * shapes[n][2], shapes[n][2], axis=1)
            g_small[(l, n)] = gp
    order = [(l, n) for l in range(depth) for n in REPL + CONVW]
    loc_shapes = [tuple(shapes[n][1:]) for _, n in order]
    packs = lambda f: _flat_rows([f(l, n) for l, n in order], 8)
    gs_small = packs(lambda l, n: g_small[(l, n)])
    _, d_sm, m_sm, v_sm = _adam(gs_small[None], packs(lambda l, n: wl[n][l]), packs(lambda l, n: ml[n][l]),
                                packs(lambda l, n: vl[n][l]), "adam_small")

    def unpack_small(buf):
        ps = _unflat(buf, loc_shapes)
        return {n: jnp.stack([ps[order.index((l, n))] for l in range(depth)]) for n in REPL + CONVW}

    outs = []
    for big, small in ((g_big, {n: jnp.stack([g_small[(l, n)] for l in range(depth)]) for n in REPL + CONVW}),
                       (d_big, unpack_small(d_sm)), (m_big, unpack_small(m_sm)), (v_big, unpack_small(v_sm))):
        merged = {**big, **small}
        outs += [merged[n] for n in WEIGHTS]
    return (loss, dx, *outs)
```

```python
import functools
import math
from typing import Callable, NamedTuple

import jax
import jax.numpy as jnp
from jax import lax
from jax.experimental import pallas as pl
from jax.experimental.pallas import tpu as pltpu

F32, BF16 = jnp.float32, jnp.bfloat16
WGRAD = BF16
SDS = jax.ShapeDtypeStruct
MESH = pl.DeviceIdType.MESH

EPS = 1e-6
N_STATE = 128
CHUNK = 128
SSD_CONV_K, SC_CONV_K, FFN_CONV_K = 4, 3, 3
N_DEV = 8
ROW_W = 1024
ROW_PAD = 32
SLAB_ALIGN = 16
SEG_BLK = 512
STRIP = 32
FFN_STRIP = 64
GLU_W = 256
MOD_ROWS = 128
VMEM_LIMIT = 48 * 2**20

ADAM_LR, ADAM_B1, ADAM_B2, ADAM_EPS, ADAM_WD, ADAM_STEP = 0.001, 0.9, 0.999, 1e-08, 0.01, 10

NT = (((1,), (1,)), ((), ()))
TN = (((0,), (0,)), ((), ()))
NN = (((1,), (0,)), ((), ()))


def _tile(n, cap, mult=128):
    best = None
    for t in range(mult, min(n, cap) + 1, mult):
        if n % t == 0:
            best = t
    return best if best is not None else n


class _Side(NamedTuple):
    operands: tuple
    out_shape: tuple
    scratch: tuple
    start: Callable
    wait: Callable
    aliases: dict = {}


def _pc(body, *, grid, in_specs, out_specs, out_shape, name, scratch=(), side=None):
    params = pltpu.CompilerParams(dimension_semantics=("arbitrary",) * len(grid), vmem_limit_bytes=VMEM_LIMIT)
    if side is None:
        return pl.pallas_call(body, grid=grid, in_specs=in_specs, out_specs=out_specs, out_shape=out_shape,
                              scratch_shapes=list(scratch), name=name, compiler_params=params)
    single = not isinstance(out_shape, (list, tuple))
    outs = [out_shape] if single else list(out_shape)
    ospecs = [out_specs] if single else list(out_specs)
    n_in, n_out, n_scr = len(in_specs), len(outs), len(scratch)
    s_in, s_out = len(side.operands), len(side.out_shape)

    def hosted(*refs):
        ins, refs = refs[:n_in], refs[n_in:]
        sins, refs = refs[:s_in], refs[s_in:]
        mouts, refs = refs[:n_out], refs[n_out:]
        souts, refs = refs[:s_out], refs[s_out:]
        scr, sems = refs[:n_scr], refs[n_scr:]
        first = functools.reduce(lambda a, b: a & b, [pl.program_id(a) == 0 for a in range(len(grid))])
        last = functools.reduce(lambda a, b: a & b, [pl.program_id(a) == grid[a] - 1 for a in range(len(grid))])

        @pl.when(first)
        def _():
            side.start(sins, souts, sems)

        body(*ins, *mouts, *scr)

        @pl.when(last)
        def _():
            side.wait(sins, souts, sems)

    call = pl.pallas_call(
        hosted, grid=grid, in_specs=list(in_specs) + [HBM_SPEC] * s_in, out_specs=ospecs + [HBM_SPEC] * s_out,
        out_shape=outs + list(side.out_shape), scratch_shapes=list(scratch) + list(side.scratch), name=name,
        input_output_aliases={n_in + k: n_out + v for k, v in side.aliases.items()}, compiler_params=params)

    def run(*args):
        res = call(*args, *side.operands)
        main = res[0] if single else list(res[:n_out])
        return main, list(res[n_out:])

    return run


def _silu(x):
    return x * jax.nn.sigmoid(x)


def _dsilu(x):
    s = jax.nn.sigmoid(x)
    return s * (1.0 + x * (1.0 - s))


def _softplus(x):
    return jnp.maximum(x, 0.0) + jnp.log(1.0 + jnp.exp(-jnp.abs(x)))


def _dot(a, b, dims=NN):
    return lax.dot_general(a, b, dims, preferred_element_type=F32)


def _bsum(v, rows=8):
    return jnp.broadcast_to(v, (rows, v.shape[1]))


def _mm(a, b, mode, out_dtype, name, caps=(1024, 1024, 2048), side=None):
    if mode == "nn":
        (M, K), (K2, N) = a.shape, b.shape
    elif mode == "nt":
        (M, K), (N, K2) = a.shape, b.shape
    else:
        (K, M), (K2, N) = a.shape, b.shape
    assert K == K2, (a.shape, b.shape, mode)
    tm, tn, tk = _tile(M, caps[0]), _tile(N, caps[1]), _tile(K, caps[2])
    nk = K // tk
    dims = {"nn": NN, "nt": NT, "tn": TN}[mode]
    if mode == "tn":
        a_spec = pl.BlockSpec((tk, tm), lambda i, j, k: (k, i))
    else:
        a_spec = pl.BlockSpec((tm, tk), lambda i, j, k: (i, k))
    if mode == "nt":
        b_spec = pl.BlockSpec((tn, tk), lambda i, j, k: (j, k))
    else:
        b_spec = pl.BlockSpec((tk, tn), lambda i, j, k: (k, j))

    def body(a_ref, b_ref, o_ref, *acc):
        part = _dot(a_ref[...].astype(BF16), b_ref[...].astype(BF16), dims)
        if nk == 1:
            o_ref[...] = part.astype(o_ref.dtype)
        else:
            acc_ref, = acc
            k = pl.program_id(2)

            @pl.when(k == 0)
            def _():
                acc_ref[...] = part

            @pl.when(k > 0)
            def _():
                acc_ref[...] += part

            @pl.when(k == nk - 1)
            def _():
                o_ref[...] = acc_ref[...].astype(o_ref.dtype)

    return _pc(body, grid=(M // tm, N // tn, nk), in_specs=[a_spec, b_spec],
               out_specs=pl.BlockSpec((tm, tn), lambda i, j, k: (i, j)),
               out_shape=SDS((M, N), out_dtype), name=name,
               scratch=() if nk == 1 else (pltpu.VMEM((tm, tn), F32),), side=side)(a, b)


def _mm_seg(segs, b, mode, out_dtype, name, blk, tile=1024, tk=2048):
    nblk = [a.shape[1] // blk for a in segs]
    assert all(a.shape[1] % blk == 0 for a in segs)
    start = [sum(nblk[:s]) for s in range(len(segs))]
    total = sum(nblk)
    ns = len(segs)
    N = b.shape[1]
    tn = _tile(N, tile)
    if mode == "nn":
        M = segs[0].shape[0]
        tm = _tile(M, tile)
        grid = (M // tm, N // tn, total)
        a_specs = [pl.BlockSpec((tm, blk), lambda i, j, k, k0=k0, n=n: (i, jnp.clip(k - k0, 0, n - 1)))
                   for k0, n in zip(start, nblk)]
        b_spec = pl.BlockSpec((blk, tn), lambda i, j, k: (k, j))
        out_rows, tmo, dims, seg_axis = M, tm, NN, 2
    else:
        K = segs[0].shape[0]
        tkk = _tile(K, tk)
        grid = (total, N // tn, K // tkk)
        a_specs = [pl.BlockSpec((tkk, blk), lambda i, j, k, i0=i0, n=n: (
            jnp.where((i >= i0) & (i < i0 + n), k, 0), jnp.clip(i - i0, 0, n - 1))) for i0, n in zip(start, nblk)]
        b_spec = pl.BlockSpec((tkk, tn), lambda i, j, k: (k, j))
        out_rows, tmo, seg_axis = total * blk, blk, 0
    nk = grid[2]
    acc_shape = (tm, tn) if mode == "nn" else (tn, blk)

    def body(*refs):
        a_refs, b_ref, o_ref, acc_ref = refs[:ns], refs[ns], refs[ns + 1], refs[ns + 2]
        k = pl.program_id(2)
        sel = pl.program_id(seg_axis)

        @pl.when(k == 0)
        def _():
            acc_ref[...] = jnp.zeros_like(acc_ref)

        for s in range(ns):
            @pl.when((sel >= start[s]) & (sel < start[s] + nblk[s]))
            def _(s=s):
                a_, b_ = a_refs[s][...].astype(BF16), b_ref[...].astype(BF16)
                acc_ref[...] += _dot(a_, b_, NN) if mode == "nn" else _dot(b_, a_, TN)

        @pl.when(k == nk - 1)
        def _():
            acc = acc_ref[...]
            o_ref[...] = (acc if mode == "nn" else acc.T).astype(o_ref.dtype)

    return _pc(body, grid=grid, in_specs=a_specs + [b_spec], out_specs=pl.BlockSpec((tmo, tn), lambda i, j, k: (i, j)),
               out_shape=SDS((out_rows, N), out_dtype), name=name, scratch=(pltpu.VMEM(acc_shape, F32),))(*segs, b)


def _modk(c8, ada_w, ada_b, name):
    rows, D = c8.shape
    N = ada_w.shape[0]
    tn = _tile(N, 1536)

    def body(c_ref, w_ref, b_ref, mod_ref, ca_ref):
        ca = _silu(c_ref[...]).astype(BF16)
        mod_ref[...] = _dot(ca, w_ref[...], NT) + b_ref[...]
        ca_ref[...] = ca

    return _pc(body, grid=(N // tn,),
               in_specs=[pl.BlockSpec((rows, D), lambda j: (0, 0)), pl.BlockSpec((tn, D), lambda j: (j, 0)),
                         pl.BlockSpec((1, tn), lambda j: (0, j))],
               out_specs=[pl.BlockSpec((rows, tn), lambda j: (0, j)), pl.BlockSpec((rows, D), lambda j: (0, 0))],
               out_shape=[SDS((rows, N), F32), SDS((rows, D), BF16)], name=name)(c8, ada_w, ada_b)


def _row_tile(S):
    return _tile(S, 512, 8)


def _strip_row_tile(S):
    return _tile(S, 1024, FFN_STRIP)


def _strips(tm, fn, init=0, rows=None):
    rows = STRIP if rows is None else rows
    assert tm % rows == 0
    return lax.fori_loop(0, tm // rows, lambda r, c: fn(pl.multiple_of(r * rows, rows), c), init)


def _strips_prev(tm, rows, ref, prev, fn, init=0):
    carry = fn(0, jnp.concatenate([prev, ref[0:rows, :]], axis=0), init)

    def step(r, c):
        r0 = pl.multiple_of(r * rows, rows)
        return fn(r0, ref[pl.ds(pl.multiple_of(r0 - 8, 8), rows + 8), :], c)

    return lax.fori_loop(1, tm // rows, step, carry)


def _strips_next(tm, rows, ref, nxt, fn, init=0):
    def step(r, c):
        r0 = pl.multiple_of(r * rows, rows)
        return fn(r0, ref[pl.ds(r0, rows + 8), :], c)

    carry = lax.fori_loop(0, tm // rows - 1, step, init)
    return fn(tm - rows, jnp.concatenate([ref[tm - rows:tm, :], nxt], axis=0), carry)


def _rows8(rows):
    pad = 8 - len(rows)
    return jnp.concatenate(rows + ([jnp.zeros((pad, rows[0].shape[1]), F32)] if pad else []), axis=0)


def _fold8(v):
    return jnp.sum(v.reshape(v.shape[0] // 8, 8, v.shape[1]), axis=0)


def _norm_mod(x, g, mod3, sc_seg, sh_seg, S, name):
    T, D = x.shape
    tm = _row_tile(S)
    tpb = S // tm

    def body(x_ref, g_ref, sc_ref, sh_ref, h_ref):
        x_ = x_ref[...]
        r = lax.rsqrt(jnp.mean(x_ * x_, axis=-1, keepdims=True) + EPS)
        h_ref[...] = ((x_ * r) * (g_ref[...] * (1.0 + sc_ref[...])) + sh_ref[...]).astype(BF16)

    return _pc(body, grid=(T // tm,),
               in_specs=[pl.BlockSpec((tm, D), lambda i: (i, 0)), pl.BlockSpec((1, D), lambda i: (0, 0)),
                         pl.BlockSpec((None, 1, D), lambda i: (i // tpb, 0, sc_seg)),
                         pl.BlockSpec((None, 1, D), lambda i: (i // tpb, 0, sh_seg))],
               out_specs=pl.BlockSpec((tm, D), lambda i: (i, 0)), out_shape=SDS((T, D), BF16), name=name)(x, g, mod3, mod3)


def _resid_post(x, fo, mod3, gt_seg, pg, S, name):
    T, D = x.shape
    tm = _row_tile(S)
    tpb = S // tm

    def body(x_ref, f_ref, gt_ref, pg_ref, o_ref):
        f = f_ref[...]
        r = lax.rsqrt(jnp.mean(f * f, axis=-1, keepdims=True) + EPS)
        o_ref[...] = x_ref[...] + (f * r) * (gt_ref[...] * pg_ref[...])

    return _pc(body, grid=(T // tm,),
               in_specs=[pl.BlockSpec((tm, D), lambda i: (i, 0)), pl.BlockSpec((tm, D), lambda i: (i, 0)),
                         pl.BlockSpec((None, 1, D), lambda i: (i // tpb, 0, gt_seg)),
                         pl.BlockSpec((1, D), lambda i: (0, 0))],
               out_specs=pl.BlockSpec((tm, D), lambda i: (i, 0)), out_shape=SDS((T, D), F32), name=name)(x, fo, mod3, pg)


def _post_bwd(fo, mod3, gt_seg, pg, dout, S, name):
    T, D = fo.shape
    tm = _row_tile(S)
    tpb = S // tm
    nb = T // S

    def body(f_ref, gt_ref, pg_ref, d_ref, df_ref, dgt_ref, dpg_ref):
        i = pl.program_id(0)

        @pl.when(i == 0)
        def _():
            dpg_ref[...] = jnp.zeros_like(dpg_ref)

        @pl.when(i % tpb == 0)
        def _():
            dgt_ref[...] = jnp.zeros_like(dgt_ref)

        f, d = f_ref[...], d_ref[...]
        r = lax.rsqrt(jnp.mean(f * f, axis=-1, keepdims=True) + EPS)
        n = f * r
        dn = d * (gt_ref[...] * pg_ref[...])
        df_ref[...] = (r * (dn - n * jnp.mean(dn * n, axis=-1, keepdims=True))).astype(df_ref.dtype)
        tot = jnp.sum(d * n, axis=0, keepdims=True)
        dgt_ref[...] += _bsum(tot * pg_ref[...])
        dpg_ref[...] += _bsum(tot * gt_ref[...])

    return _pc(body, grid=(T // tm,),
               in_specs=[pl.BlockSpec((tm, D), lambda i: (i, 0)),
                         pl.BlockSpec((None, 1, D), lambda i: (i // tpb, 0, gt_seg)),
                         pl.BlockSpec((1, D), lambda i: (0, 0)), pl.BlockSpec((tm, D), lambda i: (i, 0))],
               out_specs=[pl.BlockSpec((tm, D), lambda i: (i, 0)), pl.BlockSpec((8, D), lambda i: (i // tpb, 0)),
                          pl.BlockSpec((8, D), lambda i: (0, 0))],
               out_shape=[SDS((T, D), BF16), SDS((nb * 8, D), F32), SDS((8, D), F32)], name=name)(fo, mod3, pg, dout)


def _pre_bwd(x, g, mod3, sc_seg, dh, dout, S, name):
    T, D = x.shape
    tm = _row_tile(S)
    tpb = S // tm
    nb = T // S

    def body(x_ref, g_ref, sc_ref, dh_ref, d_ref, dx_ref, dg_ref, dsc_ref, dsh_ref):
        i = pl.program_id(0)

        @pl.when(i == 0)
        def _():
            dg_ref[...] = jnp.zeros_like(dg_ref)

        @pl.when(i % tpb == 0)
        def _():
            dsc_ref[...] = jnp.zeros_like(dsc_ref)
            dsh_ref[...] = jnp.zeros_like(dsh_ref)

        x_, dh_ = x_ref[...], dh_ref[...]
        r = lax.rsqrt(jnp.mean(x_ * x_, axis=-1, keepdims=True) + EPS)
        n = x_ * r
        dn = dh_ * (g_ref[...] * (1.0 + sc_ref[...]))
        dx_ref[...] = d_ref[...] + r * (dn - n * jnp.mean(dn * n, axis=-1, keepdims=True))
        dhn = jnp.sum(dh_ * n, axis=0, keepdims=True)
        dg_ref[...] += _bsum(dhn * (1.0 + sc_ref[...]))
        dsc_ref[...] += _bsum(dhn * g_ref[...])
        dsh_ref[...] += _bsum(jnp.sum(dh_, axis=0, keepdims=True))

    row = pl.BlockSpec((tm, D), lambda i: (i, 0))
    return _pc(body, grid=(T // tm,),
               in_specs=[row, pl.BlockSpec((1, D), lambda i: (0, 0)),
                         pl.BlockSpec((None, 1, D), lambda i: (i // tpb, 0, sc_seg)), row, row],
               out_specs=[row, pl.BlockSpec((8, D), lambda i: (0, 0)), pl.BlockSpec((8, D), lambda i: (i // tpb, 0)),
                          pl.BlockSpec((8, D), lambda i: (i // tpb, 0))],
               out_shape=[SDS((T, D), F32), SDS((8, D), F32), SDS((nb * 8, D), F32), SDS((nb * 8, D), F32)],
               name=name)(x, g, mod3, dh, dout)


def _loss(y, target, S, name):
    T, D = y.shape
    tm = _row_tile(S)

    def body(y_ref, t_ref, dy_ref, l_ref):
        @pl.when(pl.program_id(0) == 0)
        def _():
            l_ref[...] = jnp.zeros_like(l_ref)

        def strip(r0, carry):
            rows = pl.ds(r0, STRIP)
            e = y_ref[rows, :] - t_ref[rows, :]
            dy_ref[rows, :] = e * (1.0 / D)
            return carry + _fold8(e * e)

        acc = _strips(tm, strip, jnp.zeros((8, D), F32))
        l_ref[...] += jnp.broadcast_to(jnp.sum(acc, keepdims=True) * (0.5 / D), l_ref.shape)

    row = pl.BlockSpec((tm, D), lambda i: (i, 0))
    return _pc(body, grid=(T // tm,), in_specs=[row, row],
               out_specs=[row, pl.BlockSpec((8, 128), lambda i: (0, 0))],
               out_shape=[SDS((T, D), F32), SDS((8, 128), F32)], name=name)(y, target)


def _conv_geom(view, C, S):
    arr, off = view
    T = arr.shape[0]
    tm = _strip_row_tile(S)
    tc = _tile(C, 512)
    assert off % tc == 0 and C % tc == 0
    return arr, off // tc, T, tm, tc, S // tm


def _prev_spec(tm, tc, ob, order):
    if order == "ij":
        return pl.BlockSpec((8, tc), lambda i, j: (jnp.maximum(i * (tm // 8) - 1, 0), ob + j))
    return pl.BlockSpec((8, tc), lambda j, i: (jnp.maximum(i * (tm // 8) - 1, 0), ob + j))


def _next_spec(T, tm, tc, ob, order):
    last = T // 8 - 1
    if order == "ij":
        return pl.BlockSpec((8, tc), lambda i, j: (jnp.minimum((i + 1) * (tm // 8), last), ob + j))
    return pl.BlockSpec((8, tc), lambda j, i: (jnp.minimum((i + 1) * (tm // 8), last), ob + j))


def _taps(win, w_ref, K, lead, rows):
    acc = win[lead:lead + rows] * w_ref[K - 1:K, :]
    for j in range(1, K):
        acc = acc + win[lead - j:lead - j + rows] * w_ref[K - 1 - j:K - j, :]
    return acc


def _taps_t(win, w_ref, K, rows):
    acc = win[0:rows] * w_ref[K - 1:K, :]
    for j in range(1, K):
        acc = acc + win[j:j + rows] * w_ref[K - 1 - j:K - j, :]
    return acc


def _conv_fwd(view, C, w8, b, K, S, name):
    arr, ob, T, tm, tc, tps = _conv_geom(view, C, S)

    def body(u_ref, p_ref, w_ref, b_ref, o_ref):
        first = (pl.program_id(0) % tps) == 0

        def strip(r0, win, carry):
            o_ref[pl.ds(r0, STRIP), :] = _taps(win, w_ref, K, 8, STRIP) + b_ref[...]
            return carry

        _strips_prev(tm, STRIP, u_ref, jnp.where(first, 0.0, p_ref[...]), strip)

    return _pc(body, grid=(T // tm, C // tc),
               in_specs=[pl.BlockSpec((tm, tc), lambda i, j: (i, ob + j)), _prev_spec(tm, tc, ob, "ij"),
                         pl.BlockSpec((8, tc), lambda i, j: (0, j)), pl.BlockSpec((1, tc), lambda i, j: (0, j))],
               out_specs=pl.BlockSpec((tm, tc), lambda i, j: (i, j)), out_shape=SDS((T, C), F32), name=name)(
                   arr, arr, w8, b)


def _conv_bwd_in(dview, C, w8, K, S, out_dtype, name):
    arr, ob, T, tm, tc, tps = _conv_geom(dview, C, S)

    def body(d_ref, n_ref, w_ref, o_ref):
        last = (pl.program_id(0) % tps) == tps - 1

        def strip(r0, win, carry):
            o_ref[pl.ds(r0, STRIP), :] = _taps_t(win, w_ref, K, STRIP).astype(o_ref.dtype)
            return carry

        _strips_next(tm, STRIP, d_ref, jnp.where(last, 0.0, n_ref[...]), strip)

    return _pc(body, grid=(T // tm, C // tc),
               in_specs=[pl.BlockSpec((tm, tc), lambda i, j: (i, ob + j)), _next_spec(T, tm, tc, ob, "ij"),
                         pl.BlockSpec((8, tc), lambda i, j: (0, j))],
               out_specs=pl.BlockSpec((tm, tc), lambda i, j: (i, j)), out_shape=SDS((T, C), out_dtype), name=name)(
                   arr, arr, w8)


def _conv_bwd_w(dview, uview, C, K, S, name):
    darr, dob, T, tm, tc, tps = _conv_geom(dview, C, S)
    uarr, uob, _, _, _, _ = _conv_geom(uview, C, S)

    def body(d_ref, u_ref, p_ref, o_ref):
        i = pl.program_id(1)

        @pl.when(i == 0)
        def _():
            o_ref[...] = jnp.zeros_like(o_ref)

        first = (i % tps) == 0

        def strip(r0, win, carry):
            d = d_ref[pl.ds(r0, STRIP), :]
            sums = [_fold8(d * win[8 - (K - 1 - k):8 - (K - 1 - k) + STRIP]) for k in range(K)] + [_fold8(d)]
            return tuple(c + s for c, s in zip(carry, sums))

        acc = _strips_prev(tm, STRIP, u_ref, jnp.where(first, 0.0, p_ref[...]), strip,
                           tuple(jnp.zeros((8, tc), F32) for _ in range(K + 1)))
        o_ref[...] += _rows8([jnp.sum(a, axis=0, keepdims=True) for a in acc])

    return _pc(body, grid=(C // tc, T // tm),
               in_specs=[pl.BlockSpec((tm, tc), lambda j, i: (i, dob + j)),
                         pl.BlockSpec((tm, tc), lambda j, i: (i, uob + j)), _prev_spec(tm, tc, uob, "ji")],
               out_specs=pl.BlockSpec((8, tc), lambda j, i: (0, j)), out_shape=SDS((8, C), F32), name=name)(
                   darr, uarr, uarr)


def _ffn_act_fwd(uu, w8, b, S, name):
    K, gw = FFN_CONV_K, GLU_W
    T, F2 = uu.shape
    tm, tc = _strip_row_tile(S), 2 * GLU_W
    tps = S // tm

    def body(u_ref, p_ref, w_ref, b_ref, a_ref):
        first = (pl.program_id(0) % tps) == 0

        def strip(r0, win, carry):
            u = _taps(win, w_ref, K, 8, STRIP) + b_ref[...]
            a_ref[pl.ds(r0, STRIP), :] = (_silu(u[:, :gw]) * u[:, gw:]).astype(BF16)
            return carry

        _strips_prev(tm, STRIP, u_ref, jnp.where(first, 0.0, p_ref[...]), strip)

    return _pc(body, grid=(T // tm, F2 // tc),
               in_specs=[pl.BlockSpec((tm, tc), lambda i, j: (i, j)), _prev_spec(tm, tc, 0, "ij"),
                         pl.BlockSpec((8, tc), lambda i, j: (0, j)), pl.BlockSpec((1, tc), lambda i, j: (0, j))],
               out_specs=pl.BlockSpec((tm, gw), lambda i, j: (i, j)), out_shape=SDS((T, F2 // 2), BF16), name=name)(
                   uu, uu, w8, b)


def _ffn_act_bwd(uu, da, w8, b, S, name, side=None):
    K, gw = FFN_CONV_K, GLU_W
    T, F2 = uu.shape
    tm, tc = _strip_row_tile(S), 2 * GLU_W
    tps = S // tm
    last16 = T // 16 - 1

    def body(u_ref, p_ref, n_ref, da_ref, dan_ref, w_ref, b_ref, duu_ref, cw_ref, dabuf):
        i = pl.program_id(1)

        @pl.when(i == 0)
        def _():
            cw_ref[...] = jnp.zeros_like(cw_ref)

        first = (i % tps) == 0
        last = (i % tps) == tps - 1
        dabuf[0:tm, :] = da_ref[...].astype(F32)
        dabuf[tm:tm + 8, :] = jnp.where(last, 0.0, dan_ref[...].astype(F32)[0:8, :])
        fs, ext = FFN_STRIP, FFN_STRIP + 8

        def strip(r0, win, carry):
            shifted = [win[8 - j:8 - j + ext] for j in range(K)]
            u = b_ref[...] + shifted[0] * w_ref[K - 1:K, :]
            for j in range(1, K):
                u = u + shifted[j] * w_ref[K - 1 - j:K - j, :]
            da_ = dabuf[pl.ds(r0, ext), :]
            g, v = u[:, :gw], u[:, gw:]
            du = jnp.concatenate([da_ * v * _dsilu(g), da_ * _silu(g)], axis=1)
            duu_ref[pl.ds(r0, FFN_STRIP), :] = _taps_t(du, w_ref, K, FFN_STRIP).astype(BF16)
            dmain = du[0:FFN_STRIP]
            sums = [_fold8(dmain * shifted[K - 1 - k][0:FFN_STRIP]) for k in range(K)] + [_fold8(dmain)]
            return tuple(c + s for c, s in zip(carry, sums))

        acc = strip(0, jnp.concatenate([jnp.where(first, 0.0, p_ref[...]), u_ref[0:ext, :]], axis=0),
                    tuple(jnp.zeros((8, tc), F32) for _ in range(K + 1)))

        def step(r, c):
            r0 = pl.multiple_of(r * fs, fs)
            return strip(r0, u_ref[pl.ds(pl.multiple_of(r0 - 8, 8), fs + 16), :], c)

        acc = lax.fori_loop(1, tm // fs - 1, step, acc)
        acc = strip(tm - fs, jnp.concatenate([u_ref[tm - ext:tm, :], n_ref[...]], axis=0), acc)
        cw_ref[...] += _rows8([jnp.sum(a, axis=0, keepdims=True) for a in acc])

    return _pc(body, grid=(F2 // tc, T // tm),
               in_specs=[pl.BlockSpec((tm, tc), lambda j, i: (i, j)), _prev_spec(tm, tc, 0, "ji"),
                         _next_spec(T, tm, tc, 0, "ji"), pl.BlockSpec((tm, gw), lambda j, i: (i, j)),
                         pl.BlockSpec((16, gw), lambda j, i: (jnp.minimum((i + 1) * (tm // 16), last16), j)),
                         pl.BlockSpec((8, tc), lambda j, i: (0, j)), pl.BlockSpec((1, tc), lambda j, i: (0, j))],
               out_specs=[pl.BlockSpec((tm, tc), lambda j, i: (i, j)), pl.BlockSpec((8, tc), lambda j, i: (0, j))],
               out_shape=[SDS((T, F2), BF16), SDS((8, F2), F32)], name=name,
               scratch=(pltpu.VMEM((tm + 8, gw), F32),), side=side)(
                   uu, uu, uu, da, da, w8, b)


def _ssd_common(dtc_raw, dtr_raw, hpc, hpr, L):
    dt_c = _softplus(dtc_raw + hpc[0:1, :])
    a_c = -jnp.exp(hpc[1:2, :])
    dt_r = _softplus(dtr_raw + hpr[:, 0:1])
    a_r = -jnp.exp(hpr[:, 1:2])
    li = lax.broadcasted_iota(jnp.int32, (L, L), 0)
    si = lax.broadcasted_iota(jnp.int32, (L, L), 1)
    low = li >= si
    upp = li <= si
    acs_c = _dotx(low, dt_c * a_c, split="b")
    acs_r = _dotx(dt_r * a_r, upp)
    return dt_c, a_c, acs_c, acs_r, low, upp


def _dotx(a, b, split="a", parts=3, dims=NN):
    val, one = (a, b) if split == "a" else (b, a)
    one = one.astype(BF16)
    acc, rem = None, val
    for i in range(parts):
        piece = rem.astype(BF16)
        t = _dot(piece, one, dims) if split == "a" else _dot(one, piece, dims)
        acc = t if acc is None else acc + t
        if i + 1 < parts:
            rem = rem - piece.astype(F32)
    return acc


def _head_maps(R, P, L):
    RP = R * P
    sel = (lax.broadcasted_iota(jnp.int32, (RP, R), 0) // P == lax.broadcasted_iota(jnp.int32, (RP, R), 1)).astype(F32)
    selt = (lax.broadcasted_iota(jnp.int32, (R, RP), 1) // P == lax.broadcasted_iota(jnp.int32, (R, RP), 0)).astype(F32)
    colb = (lax.broadcasted_iota(jnp.int32, (R, R * L), 1) // L == lax.broadcasted_iota(jnp.int32, (R, R * L), 0)).astype(F32)
    return sel, selt, colb


def _pair_diag(mats, rhs_b, R, P):
    lanes = 2 * P
    lo = lax.broadcasted_iota(jnp.int32, (mats[0].shape[0], lanes), 1) < P
    out = []
    for q in range(R // 2):
        rp = rhs_b[:, q * lanes:(q + 1) * lanes]
        out.append(jnp.where(lo, _dot(mats[2 * q], rp), _dot(mats[2 * q + 1], rp)))
    return jnp.concatenate(out, axis=1) if len(out) > 1 else out[0]


def _ssd_specs(pre, off_x, off_b, off_c, G, R, P, nb, nc, rev):
    L, N, RP = CHUNK, N_STATE, R * P
    cidx = (lambda c: nc - 1 - c) if rev else (lambda c: c)
    xb, bb, cb = off_x // RP, off_b // N, off_c // N
    assert off_x % RP == 0 and off_b % N == 0 and off_c % N == 0
    row = lambda b, c: b * nc + cidx(c)
    return dict(
        x=pl.BlockSpec((L, RP), lambda g, b, c: (row(b, c), xb + g)),
        b=pl.BlockSpec((L, N), lambda g, b, c: (row(b, c), bb + g)),
        c=pl.BlockSpec((L, N), lambda g, b, c: (row(b, c), cb + g)),
        dtc=pl.BlockSpec((None, L, R), lambda g, b, c: (g, row(b, c), 0)),
        dtr=pl.BlockSpec((None, R, L), lambda g, b, c: (g, 0, row(b, c))),
        hpc=pl.BlockSpec((None, 8, R), lambda g, b, c: (g, 0, 0)),
        hpr=pl.BlockSpec((None, R, 8), lambda g, b, c: (g, 0, 0)),
        y=pl.BlockSpec((L, RP), lambda g, b, c: (row(b, c), g)),
        bc=pl.BlockSpec((L, N), lambda g, b, c: (row(b, c), g)),
        hs=pl.BlockSpec((None, None, N, RP), lambda g, b, c: (row(b, c), g, 0, 0)),
    )


def _ssd_fwd(pre, offs, dtc, dtr, hpc, hpr, G, R, P, S, name, side=None):
    T = pre.shape[0]
    L, N, RP = CHUNK, N_STATE, R * P
    nc, nb = S // L, T // S
    sp = _ssd_specs(pre, *offs, G, R, P, nb, nc, False)

    def body(px_ref, pb_ref, pc_ref, dtc_ref, dtr_ref, hpc_ref, hpr_ref, y_ref, hs_ref, hst):
        @pl.when(pl.program_id(2) == 0)
        def _():
            hst[...] = jnp.zeros_like(hst)

        xs, bm, cm = _silu(px_ref[...]), _silu(pb_ref[...]), _silu(pc_ref[...])
        hpc_ = hpc_ref[...]
        dt_c, _, acs_c, acs_r, low, _ = _ssd_common(dtc_ref[...], dtr_ref[...], hpc_, hpr_ref[...], L)
        _, selt, colb = _head_maps(R, P, L)
        dt_e, a_e, hp_e = _dotx(dt_c, selt), _dotx(acs_c, selt), _dotx(hpc_, selt)
        a_bc = _dotx(acs_c, colb)
        a_last = a_e[L - 1:L, :]
        bb, cb = bm.astype(BF16), cm.astype(BF16)
        gm = _dot(cb, bb, NT)
        hprev = hst[...]
        hprev_b = hprev.astype(BF16)
        hs_ref[...] = hprev_b
        xdt = xs * dt_e
        xdt_b = xdt.astype(BF16)
        ms = []
        for r in range(R):
            dec = jnp.exp(jnp.where(low, a_bc[:, r * L:(r + 1) * L] - acs_r[r:r + 1, :], -jnp.inf))
            ms.append((gm * dec).astype(BF16))
        y = _pair_diag(ms, xdt_b, R, P) + _dot(cb, hprev_b) * jnp.exp(a_e) + hp_e[2:3, :] * xs
        y_ref[...] = y
        xw = (xdt * jnp.exp(a_last - a_e)).astype(BF16)
        hst[...] = hprev * jnp.exp(a_last) + _dot(bb, xw, TN)

    return _pc(body, grid=(G, nb, nc),
               in_specs=[sp["x"], sp["b"], sp["c"], sp["dtc"], sp["dtr"], sp["hpc"], sp["hpr"]],
               out_specs=[sp["y"], sp["hs"]],
               out_shape=[SDS((T, G * RP), F32), SDS((nb * nc, G, N, RP), BF16)], name=name,
               scratch=(pltpu.VMEM((N, RP), F32),), side=side)(pre, pre, pre, dtc, dtr, hpc, hpr)


def _ssd_bwd(pre, offs, dtc, dtr, hpc, hpr, hs, dy, G, R, P, S, name, side=None):
    T = pre.shape[0]
    L, N, RP = CHUNK, N_STATE, R * P
    nc, nb = S // L, T // S
    sp = _ssd_specs(pre, *offs, G, R, P, nb, nc, True)

    def body(px_ref, pb_ref, pc_ref, dtc_ref, dtr_ref, hpc_ref, hpr_ref, hs_ref, dy_ref,
             dpx_ref, dpb_ref, dpc_ref, ddt_ref, hpg_ref, dhst):
        bi, ci = pl.program_id(1), pl.program_id(2)

        @pl.when(ci == 0)
        def _():
            dhst[...] = jnp.zeros_like(dhst)

        @pl.when((bi == 0) & (ci == 0))
        def _():
            hpg_ref[...] = jnp.zeros_like(hpg_ref)

        px, pb, pcc = px_ref[...], pb_ref[...], pc_ref[...]
        xs, bm, cm = _silu(px), _silu(pb), _silu(pcc)
        hpc_ = hpc_ref[...]
        dtc_raw = dtc_ref[...]
        dt_c, a_c, acs_c, acs_r, low, upp = _ssd_common(dtc_raw, dtr_ref[...], hpc_, hpr_ref[...], L)
        sel, selt, colb = _head_maps(R, P, L)
        dt_e, a_e, hp_e = _dotx(dt_c, selt), _dotx(acs_c, selt), _dotx(hpc_, selt)
        a_bc = _dotx(acs_c, colb)
        a_last = a_e[L - 1:L, :]
        e_e, w_e = jnp.exp(a_e), jnp.exp(a_last - a_e)
        bb, cb = bm.astype(BF16), cm.astype(BF16)
        gm = _dot(cb, bb, NT)
        gmt = _dot(bb, cb, NT)
        hprev = hs_ref[...]
        dhn = dhst[...]
        dhn_b = dhn.astype(BF16)
        dy = dy_ref[...]
        dy_b = dy.astype(BF16)
        xdt = xs * dt_e
        xdt_b = xdt.astype(BF16)
        yoff = _dot(cb, hprev) * e_e
        dye_b = (dy * e_e).astype(BF16)
        dcm = _dot(dye_b, hprev, NT)
        dhst[...] = _dot(cb, dye_b, TN) + jnp.exp(a_last) * dhn
        dxdt_st = _dot(bb, dhn_b) * w_e
        dbm = _dot((xdt * w_e).astype(BF16), dhn_b, NT)
        lanes = 2 * P
        lo = lax.broadcasted_iota(jnp.int32, (L, lanes), 1) < P
        dg = jnp.zeros((L, L), F32)
        es, css = [], []
        for r in range(R):
            col_b, row = a_bc[:, r * L:(r + 1) * L], acs_r[r:r + 1, :]
            dec = jnp.exp(jnp.where(low, col_b - row, -jnp.inf))
            q = r // 2
            dyp = dy_b[:, q * lanes:(q + 1) * lanes]
            dyp = jnp.where(lo if r % 2 == 0 else ~lo, dyp, jnp.zeros_like(dyp))
            dm = _dot(dyp, xdt_b[:, q * lanes:(q + 1) * lanes], NT)
            dg = dg + dm * dec
            e = dm * (gm * dec)
            es.append(e)
            css.append(jnp.sum(e, axis=0, keepdims=True))
        dgb = dg.astype(BF16)
        dcm = dcm + _dot(dgb, bb)
        dbm = dbm + _dot(dgb, cb, TN)
        colbt = (lax.broadcasted_iota(jnp.int32, (R * L, R), 0) // L
                 == lax.broadcasted_iota(jnp.int32, (R * L, R), 1)).astype(F32)
        eye = (lax.broadcasted_iota(jnp.int32, (R, R), 0) == lax.broadcasted_iota(jnp.int32, (R, R), 1)).astype(F32)
        row_sums = _dotx(jnp.concatenate(es, axis=1), colbt)
        col_sums = _dotx(jnp.concatenate(css, axis=0), eye, dims=TN)
        mts = []
        for r in range(R):
            dect = jnp.exp(jnp.where(upp, acs_r[r:r + 1, :] - a_bc[:, r * L:(r + 1) * L], -jnp.inf))
            mts.append((gmt * dect).astype(BF16))
        dxdt = _pair_diag(mts, dy_b, R, P) + dxdt_st
        q_st = _dotx(xdt * dxdt_st, sel, parts=1)
        da = row_sums - col_sums + _dotx(dy * yoff, sel, parts=1) - q_st
        hh = jnp.sum(_dotx(dhn * hprev.astype(F32), sel, parts=1), axis=0, keepdims=True)
        da_last = jnp.exp(acs_c[L - 1:L, :]) * hh + jnp.sum(q_st, axis=0, keepdims=True)
        rowi = lax.broadcasted_iota(jnp.int32, (L, R), 0)
        da = da + jnp.where(rowi == L - 1, da_last, 0.0)
        dpx_ref[...] = (dxdt * dt_e + hp_e[2:3, :] * dy) * _dsilu(px)
        dpb_ref[...] = dbm * _dsilu(pb)
        dpc_ref[...] = dcm * _dsilu(pcc)
        dadt = _dotx(upp, da, split="b")
        ddt = _dotx(dxdt * xs, sel, parts=1) + dadt * a_c
        ddt_raw = ddt * jax.nn.sigmoid(dtc_raw + hpc_[0:1, :])
        ddt_ref[...] = ddt_raw
        d_a = jnp.sum(dadt * dt_c, axis=0, keepdims=True)
        d_d = jnp.sum(_dotx(dy * xs, sel, parts=1), axis=0, keepdims=True)
        rows = [jnp.sum(ddt_raw, axis=0, keepdims=True), d_a * a_c, d_d, jnp.zeros((5, R), F32)]
        hpg_ref[...] += jnp.concatenate(rows, axis=0)

    return _pc(body, grid=(G, nb, nc),
               in_specs=[sp["x"], sp["b"], sp["c"], sp["dtc"], sp["dtr"], sp["hpc"], sp["hpr"], sp["hs"], sp["y"]],
               out_specs=[sp["y"], sp["bc"], sp["bc"], sp["dtc"], pl.BlockSpec((None, 8, R), lambda g, b, c: (g, 0, 0))],
               out_shape=[SDS((T, G * RP), F32), SDS((T, G * N), F32), SDS((T, G * N), F32), SDS((G, T, R), F32),
                          SDS((G, 8, R), F32)], name=name,
               scratch=(pltpu.VMEM((N, RP), F32),), side=side)(pre, pre, pre, dtc, dtr, hpc, hpr, hs, dy)


def _gate_norm_fwd(y, zview, ng, G, S, name):
    T, DI = y.shape
    zarr, zoff = zview
    gw = DI // G
    tm = _row_tile(S)
    zb = zoff // gw
    assert zoff % gw == 0

    def body(y_ref, z_ref, g_ref, o_ref):
        yg = y_ref[...] * _silu(z_ref[...])
        r = lax.rsqrt(jnp.mean(yg * yg, axis=-1, keepdims=True) + EPS)
        o_ref[...] = (yg * r * g_ref[...]).astype(BF16)

    return _pc(body, grid=(T // tm, G),
               in_specs=[pl.BlockSpec((tm, gw), lambda i, g: (i, g)), pl.BlockSpec((tm, gw), lambda i, g: (i, zb + g)),
                         pl.BlockSpec((1, gw), lambda i, g: (0, g))],
               out_specs=pl.BlockSpec((tm, gw), lambda i, g: (i, g)), out_shape=SDS((T, DI), BF16), name=name)(y, zarr, ng)


def _gate_norm_bwd(y, zview, ng, dyn, G, S, name):
    T, DI = y.shape
    zarr, zoff = zview
    gw = DI // G
    tm = _row_tile(S)
    zb = zoff // gw

    def body(y_ref, z_ref, g_ref, d_ref, dy_ref, dz_ref, dg_ref):
        @pl.when(pl.program_id(1) == 0)
        def _():
            dg_ref[...] = jnp.zeros_like(dg_ref)

        y_, z, d = y_ref[...], z_ref[...], d_ref[...]
        sz = _silu(z)
        yg = y_ * sz
        r = lax.rsqrt(jnp.mean(yg * yg, axis=-1, keepdims=True) + EPS)
        n = yg * r
        dn = d * g_ref[...]
        dyg = r * (dn - n * jnp.mean(dn * n, axis=-1, keepdims=True))
        dy_ref[...] = dyg * sz
        dz_ref[...] = (dyg * y_ * _dsilu(z)).astype(BF16)
        dg_ref[...] += _bsum(jnp.sum(d * n, axis=0, keepdims=True))

    return _pc(body, grid=(G, T // tm),
               in_specs=[pl.BlockSpec((tm, gw), lambda g, i: (i, g)), pl.BlockSpec((tm, gw), lambda g, i: (i, zb + g)),
                         pl.BlockSpec((1, gw), lambda g, i: (0, g)), pl.BlockSpec((tm, gw), lambda g, i: (i, g))],
               out_specs=[pl.BlockSpec((tm, gw), lambda g, i: (i, g)), pl.BlockSpec((tm, gw), lambda g, i: (i, g)),
                          pl.BlockSpec((8, gw), lambda g, i: (0, g))],
               out_shape=[SDS((T, DI), F32), SDS((T, DI), BF16), SDS((8, DI), F32)], name=name)(y, zarr, ng, dyn)


def _shortconv_fwd(proj, off_b, off_c, off_h, C, w8, S, name):
    K = SC_CONV_K
    _, ob, T, tm, tc, tps = _conv_geom((proj, off_b), C, S)
    oc, oh = off_c // tc, off_h // tc

    def body(b_ref, c_ref, h_ref, cp_ref, hp_ref, w_ref, o_ref, buf):
        first = (pl.program_id(0) % tps) == 0
        buf[0:8, :] = jnp.where(first, 0.0, cp_ref[...] * hp_ref[...])
        buf[8:, :] = c_ref[...] * h_ref[...]

        def strip(r0, carry):
            conv = _taps(buf[pl.ds(r0, STRIP + 8), :], w_ref, K, 8, STRIP)
            o_ref[pl.ds(r0, STRIP), :] = (b_ref[pl.ds(r0, STRIP), :] * conv).astype(BF16)
            return carry

        _strips(tm, strip)

    blk = lambda o: pl.BlockSpec((tm, tc), lambda i, j: (i, o + j))
    return _pc(body, grid=(T // tm, C // tc),
               in_specs=[blk(ob), blk(oc), blk(oh), _prev_spec(tm, tc, oc, "ij"), _prev_spec(tm, tc, oh, "ij"),
                         pl.BlockSpec((8, tc), lambda i, j: (0, j))],
               out_specs=pl.BlockSpec((tm, tc), lambda i, j: (i, j)), out_shape=SDS((T, C), BF16), name=name,
               scratch=(pltpu.VMEM((tm + 8, tc), F32),))(proj, proj, proj, proj, proj, w8)


def _shortconv_bwd(proj, off_b, off_c, off_h, C, w8, dsc, S, name):
    K = SC_CONV_K
    _, ob, T, tm, tc, tps = _conv_geom((proj, off_b), C, S)
    oc, oh = off_c // tc, off_h // tc

    def body(b_ref, c_ref, h_ref, cp_ref, hp_ref, bn_ref, d_ref, dn_ref, w_ref,
             db_ref, dc_ref, dh_ref, dw_ref, buf, buf2):
        i = pl.program_id(1)

        @pl.when(i == 0)
        def _():
            dw_ref[...] = jnp.zeros_like(dw_ref)

        first = (i % tps) == 0
        last = (i % tps) == tps - 1
        buf[0:8, :] = jnp.where(first, 0.0, cp_ref[...] * hp_ref[...])
        buf[8:, :] = c_ref[...] * h_ref[...]
        buf2[0:tm, :] = d_ref[...] * b_ref[...]
        buf2[tm:tm + 8, :] = jnp.where(last, 0.0, dn_ref[...] * bn_ref[...])

        def strip(r0, carry):
            rows = pl.ds(r0, STRIP)
            vwin = buf[pl.ds(r0, STRIP + 8), :]
            vs = [vwin[8 - j:8 - j + STRIP] for j in range(K)]
            conv = vs[0] * w_ref[K - 1:K, :]
            for j in range(1, K):
                conv = conv + vs[j] * w_ref[K - 1 - j:K - j, :]
            db_ref[rows, :] = (d_ref[rows, :] * conv).astype(BF16)
            dwin = buf2[pl.ds(r0, STRIP + 8), :]
            dv = _taps_t(dwin, w_ref, K, STRIP)
            dc_ref[rows, :] = (dv * h_ref[rows, :]).astype(BF16)
            dh_ref[rows, :] = (dv * c_ref[rows, :]).astype(BF16)
            dconv = dwin[0:STRIP]
            sums = [_fold8(dconv * vs[K - 1 - k]) for k in range(K)]
            return tuple(c + s for c, s in zip(carry, sums))

        acc = _strips(tm, strip, tuple(jnp.zeros((8, tc), F32) for _ in range(K)))
        dw_ref[...] += _rows8([jnp.sum(a, axis=0, keepdims=True) for a in acc])

    blk = lambda o: pl.BlockSpec((tm, tc), lambda j, i: (i, o + j))
    out = pl.BlockSpec((tm, tc), lambda j, i: (i, j))
    return _pc(body, grid=(C // tc, T // tm),
               in_specs=[blk(ob), blk(oc), blk(oh), _prev_spec(tm, tc, oc, "ji"), _prev_spec(tm, tc, oh, "ji"),
                         _next_spec(T, tm, tc, ob, "ji"), blk(0), _next_spec(T, tm, tc, 0, "ji"),
                         pl.BlockSpec((8, tc), lambda j, i: (0, j))],
               out_specs=[out, out, out, pl.BlockSpec((8, tc), lambda j, i: (0, j))],
               out_shape=[SDS((T, C), BF16)] * 3 + [SDS((8, C), F32)], name=name,
               scratch=(pltpu.VMEM((tm + 8, tc), F32), pltpu.VMEM((tm + 8, tc), F32)))(
                   proj, proj, proj, proj, proj, proj, dsc, dsc, w8)


def _merge_fwd(proj, off_g1, off_g2, y1, y2, S, name):
    T, D = y1.shape
    tm = _row_tile(S)
    o1, o2 = off_g1 // D, off_g2 // D
    assert off_g1 % D == 0 and off_g2 % D == 0

    def body(g1_ref, g2_ref, y1_ref, y2_ref, o_ref):
        def strip(r0, carry):
            rows = pl.ds(r0, STRIP)
            o_ref[rows, :] = (jax.nn.sigmoid(g1_ref[rows, :]) * y1_ref[rows, :]
                              + jax.nn.sigmoid(g2_ref[rows, :]) * y2_ref[rows, :]).astype(BF16)
            return carry

        _strips(tm, strip)

    row = pl.BlockSpec((tm, D), lambda i: (i, 0))
    return _pc(body, grid=(T // tm,),
               in_specs=[pl.BlockSpec((tm, D), lambda i: (i, o1)), pl.BlockSpec((tm, D), lambda i: (i, o2)), row, row],
               out_specs=row, out_shape=SDS((T, D), BF16), name=name)(proj, proj, y1, y2)


def _merge_bwd(proj, off_g1, off_g2, y1, y2, dm, S, name):
    T, D = y1.shape
    tm = _row_tile(S)
    o1, o2 = off_g1 // D, off_g2 // D

    def body(g1_ref, g2_ref, y1_ref, y2_ref, d_ref, dy1_ref, dy2_ref, dg1_ref, dg2_ref):
        def strip(r0, carry):
            rows = pl.ds(r0, STRIP)
            d = d_ref[rows, :]
            s1, s2 = jax.nn.sigmoid(g1_ref[rows, :]), jax.nn.sigmoid(g2_ref[rows, :])
            dy1_ref[rows, :] = (d * s1).astype(BF16)
            dy2_ref[rows, :] = (d * s2).astype(BF16)
            dg1_ref[rows, :] = (d * y1_ref[rows, :] * s1 * (1.0 - s1)).astype(BF16)
            dg2_ref[rows, :] = (d * y2_ref[rows, :] * s2 * (1.0 - s2)).astype(BF16)
            return carry

        _strips(tm, strip)

    row = pl.BlockSpec((tm, D), lambda i: (i, 0))
    return _pc(body, grid=(T // tm,),
               in_specs=[pl.BlockSpec((tm, D), lambda i: (i, o1)), pl.BlockSpec((tm, D), lambda i: (i, o2)), row, row, row],
               out_specs=[row] * 4, out_shape=[SDS((T, D), BF16)] * 4, name=name)(proj, proj, y1, y2, dm)


def _pad8(w):
    return jnp.pad(w, ((0, 8 - w.shape[0]), (0, 0)))


def _dims(w):
    D = w["mix_pre_g"].shape[-1]
    DI = w["ssd_norm_g"].shape[-1]
    H = w["ssd_dt_bias"].shape[-1]
    conv_dim = w["ssd_conv_b"].shape[-1]
    G = (conv_dim - DI) // (2 * N_STATE)
    F = w["w_down"].shape[0]
    return dict(D=D, DI=DI, H=H, P=DI // H, G=G, R=H // G, GN=G * N_STATE, CD=conv_dim, F=F)


def _proj_layout(d):
    D, DI, CD, H = d["D"], d["DI"], d["CD"], d["H"]
    o = dict(z=0, xbc=DI, scb=DI + CD, scc=DI + CD + D, sch=DI + CD + 2 * D, g1=DI + CD + 3 * D, g2=DI + CD + 4 * D,
             dt=DI + CD + 5 * D)
    o["sb"] = math.gcd(SEG_BLK, D, DI, d["GN"])
    assert o["sb"] % 128 == 0 and H <= o["sb"]
    o["np"] = o["dt"] + o["sb"]
    return o


def _glu_perm(a, F, inverse=False):
    lead = a.shape[:-1]
    nb = F // GLU_W
    if not inverse:
        return a.reshape(*lead, 2, nb, GLU_W).swapaxes(-3, -2).reshape(*lead, 2 * F)
    return a.reshape(*lead, nb, 2, GLU_W).swapaxes(-3, -2).reshape(*lead, 2 * F)


def _glu_perm_rows(a, F, inverse=False):
    nb, D = F // GLU_W, a.shape[1]
    shape = (nb, 2, GLU_W, D) if inverse else (2, nb, GLU_W, D)
    return a.reshape(shape).swapaxes(0, 1).reshape(2 * F, D)


def _prep_layer(w):
    d = _dims(w)
    D, DI, CD, H, G, R, F = d["D"], d["DI"], d["CD"], d["H"], d["G"], d["R"], d["F"]
    lay = _proj_layout(d)
    w_in = w["w_in"]
    used = lay["dt"] + H
    wcat = jnp.concatenate([w_in[:DI + CD], w_in[DI + CD + H:], w_in[DI + CD:DI + CD + H],
                            jnp.zeros((lay["np"] - used, D), w_in.dtype)], axis=0)
    hp = jnp.stack([w["ssd_dt_bias"], w["ssd_a_log"], w["ssd_d"]], 0).astype(F32)
    hpc = jnp.pad(hp.reshape(3, G, R).transpose(1, 0, 2), ((0, 0), (0, 5), (0, 0)))
    hpr = jnp.pad(hp[:2].reshape(2, G, R).transpose(1, 2, 0), ((0, 0), (0, 0), (0, 6)))
    row = lambda v: v.reshape(1, -1).astype(F32)
    return dict(
        d=d, lay=lay, ada_w=w["ada_w"].astype(BF16), ada_b=row(w["ada_b"]),
        mix_pre_g=row(w["mix_pre_g"]), mix_post_g=row(w["mix_post_g"]), wcat=wcat.astype(BF16),
        ssd_conv_w=_pad8(w["ssd_conv_w"].astype(F32)), ssd_conv_b=row(w["ssd_conv_b"]), hpc=hpc, hpr=hpr,
        ssd_norm_g=row(w["ssd_norm_g"]), w_ssd_out=w["w_ssd_out"].astype(BF16),
        sc_conv_w=_pad8(w["sc_conv_w"].astype(F32)), w_sc_out=w["w_sc_out"].astype(BF16), w_o=w["w_o"].astype(BF16),
        ffn_pre_g=row(w["ffn_pre_g"]), ffn_post_g=row(w["ffn_post_g"]),
        w_up=_glu_perm_rows(w["w_up"], F).astype(BF16), ffn_conv_w=_pad8(_glu_perm(w["ffn_conv_w"].astype(F32), F)),
        ffn_conv_b=_glu_perm(row(w["ffn_conv_b"]), F), w_down=w["w_down"].astype(BF16))


def _dt_layouts(proj, lay, d):
    T = proj.shape[0]
    dt = proj[:, lay["dt"]:lay["dt"] + d["H"]].reshape(T, d["G"], d["R"])
    return dt.transpose(1, 0, 2), dt.transpose(1, 2, 0)


def _layer_fwd(x, c8, p, S, li, gather=None):
    d, lay = p["d"], p["lay"]
    D, DI, G, R, P, GN, CD = d["D"], d["DI"], d["G"], d["R"], d["P"], d["GN"], d["CD"]
    nb = x.shape[0] // S
    nm = lambda s: f"l{li}_{s}"
    mod, cact = _modk(c8, p["ada_w"], p["ada_b"], nm("mod"))
    mod3 = mod[:nb].reshape(nb, 1, 6 * D)
    h = _norm_mod(x, p["mix_pre_g"], mod3, 1, 0, S, nm("norm1"))
    proj = _mm(h, p["wcat"], "nt", F32, nm("mm_in"), caps=(1024, 1536, 2048))
    pre = _conv_fwd((proj, lay["xbc"]), CD, p["ssd_conv_w"], p["ssd_conv_b"], SSD_CONV_K, S, nm("ssdconv"))
    dtc, dtr = _dt_layouts(proj, lay, d)
    offs = (0, DI, DI + GN)
    gathered = None
    if gather is None:
        y, hs = _ssd_fwd(pre, offs, dtc, dtr, p["hpc"], p["hpr"], G, R, P, S, nm("ssd"))
    else:
        (y, hs), (gathered,) = _ssd_fwd(pre, offs, dtc, dtr, p["hpc"], p["hpr"], G, R, P, S, nm("ssd"),
                                        side=_side_gather_direct(gather))
    yn = _gate_norm_fwd(y, (proj, lay["z"]), p["ssd_norm_g"], G, S, nm("gnorm"))
    sc = _shortconv_fwd(proj, lay["scb"], lay["scc"], lay["sch"], D, p["sc_conv_w"], S, nm("sconv"))
    if gather is None:
        y_ssd = _mm(yn, p["w_ssd_out"], "nn", F32, nm("mm_ssdout"))
    else:
        y_ssd, (gathered,) = _mm(yn, p["w_ssd_out"], "nn", F32, nm("mm_ssdout"), side=_side_gather_forward(gathered))
    y_sc = _mm(sc, p["w_sc_out"], "nn", F32, nm("mm_scout"))
    m = _merge_fwd(proj, lay["g1"], lay["g2"], y_ssd, y_sc, S, nm("merge"))
    mix = _mm(m, p["w_o"], "nn", F32, nm("mm_o"))
    x1 = _resid_post(x, mix, mod3, 2, p["mix_post_g"], S, nm("post1"))
    h2 = _norm_mod(x1, p["ffn_pre_g"], mod3, 4, 3, S, nm("norm2"))
    uu = _mm(h2, p["w_up"], "nt", F32, nm("mm_up"), caps=(1024, 1408, 2048))
    a = _ffn_act_fwd(uu, p["ffn_conv_w"], p["ffn_conv_b"], S, nm("ffnact"))
    f = _mm(a, p["w_down"], "nn", F32, nm("mm_down"), caps=(1024, 1024, 1408))
    x2 = _resid_post(x1, f, mod3, 5, p["ffn_post_g"], S, nm("post2"))
    saved = dict(x=x, h=h, proj=proj, pre=pre, dtc=dtc, dtr=dtr, y=y, hs=hs, yn=yn, sc=sc, y_ssd=y_ssd, y_sc=y_sc,
                 m=m, mix=mix, x1=x1, h2=h2, uu=uu, a=a, f=f, mod3=mod3, cact=cact)
    return x2, saved, gathered


def _seq_sum(acc, nb):
    return acc.reshape(nb, 8, -1)[:, 0, :]


def _layer_bwd(dx2, p, s, S, li, chip_sums=None, early=None):
    d, lay = p["d"], p["lay"]
    D, DI, G, R, P, GN, CD, H, F = d["D"], d["DI"], d["G"], d["R"], d["P"], d["GN"], d["CD"], d["H"], d["F"]
    nb = dx2.shape[0] // S
    nm = lambda t: f"l{li}_{t}"
    mod3 = s["mod3"]
    g = {}
    exchanged = None
    df, dgt2, dpg2 = _post_bwd(s["f"], mod3, 5, p["ffn_post_g"], dx2, S, nm("post2_b"))
    g["ffn_post_g"] = dpg2[0]
    da = _mm(df, p["w_down"], "nt", BF16, nm("mm_down_bi"), caps=(1024, 1408, 2048))
    g["w_down"] = _mm(s["a"], df, "tn", WGRAD,nm("mm_down_bw"), caps=(1408, 1024, 1024))
    if chip_sums is None:
        duu, cw = _ffn_act_bwd(s["uu"], da, p["ffn_conv_w"], p["ffn_conv_b"], S, nm("ffnact_b"))
    else:
        (duu, cw), (exchanged,) = _ffn_act_bwd(s["uu"], da, p["ffn_conv_w"], p["ffn_conv_b"], S, nm("ffnact_b"),
                                               side=_side_chip_exchange(chip_sums))
    g["ffn_conv_w"] = _glu_perm(cw[:FFN_CONV_K], F, inverse=True)
    g["ffn_conv_b"] = _glu_perm(cw[FFN_CONV_K], F, inverse=True)
    dh2 = _mm(duu, p["w_up"], "nn", F32, nm("mm_up_bi"), caps=(1024, 1024, 2816))
    g["w_up"] = _glu_perm_rows(_mm(duu, s["h2"], "tn", WGRAD,nm("mm_up_bw"), caps=(1408, 1024, 1024)), F, inverse=True)
    dx1, dg2, dsc2, dsh2 = _pre_bwd(s["x1"], p["ffn_pre_g"], mod3, 4, dh2, dx2, S, nm("norm2_b"))
    g["ffn_pre_g"] = dg2[0]
    dmix, dgt1, dpg1 = _post_bwd(s["mix"], mod3, 2, p["mix_post_g"], dx1, S, nm("post1_b"))
    g["mix_post_g"] = dpg1[0]
    dm = _mm(dmix, p["w_o"], "nt", F32, nm("mm_o_bi"))
    g["w_o"] = _mm(s["m"], dmix, "tn", WGRAD,nm("mm_o_bw"))
    proj = s["proj"]
    dy_ssd, dy_sc, dg1, dg2_ = _merge_bwd(proj, lay["g1"], lay["g2"], s["y_ssd"], s["y_sc"], dm, S, nm("merge_b"))
    dyn = _mm(dy_ssd, p["w_ssd_out"], "nt", F32, nm("mm_ssdout_bi"))
    g["w_ssd_out"] = _mm(s["yn"], dy_ssd, "tn", WGRAD,nm("mm_ssdout_bw"))
    dsc = _mm(dy_sc, p["w_sc_out"], "nt", F32, nm("mm_scout_bi"))
    g["w_sc_out"] = _mm(s["sc"], dy_sc, "tn", WGRAD,nm("mm_scout_bw"))
    dscb, dscc, dsch, scw = _shortconv_bwd(proj, lay["scb"], lay["scc"], lay["sch"], D, p["sc_conv_w"], dsc, S, nm("sconv_b"))
    g["sc_conv_w"] = scw[:SC_CONV_K]
    dy, dz, dng = _gate_norm_bwd(s["y"], (proj, lay["z"]), p["ssd_norm_g"], dyn, G, S, nm("gnorm_b"))
    g["ssd_norm_g"] = dng[0]
    offs = (0, DI, DI + GN)
    early_side = None if early is None else _side_chip_exchange(early(g))
    res = _ssd_bwd(s["pre"], offs, s["dtc"], s["dtr"], p["hpc"], p["hpr"], s["hs"], dy, G, R, P, S, nm("ssd_b"),
                   side=early_side)
    (dpx, dpb, dpc, ddt, hpg), early_got = res if early is not None else (res, None)
    g["ssd_dt_bias"], g["ssd_a_log"], g["ssd_d"] = hpg[:, 0, :].reshape(H), hpg[:, 1, :].reshape(H), hpg[:, 2, :].reshape(H)
    cws, dxbc = [], []
    for name, darr, off, C in (("x", dpx, 0, DI), ("b", dpb, DI, GN), ("c", dpc, DI + GN, GN)):
        w8 = p["ssd_conv_w"][:, off:off + C]
        cws.append(_conv_bwd_w((darr, 0), (proj, lay["xbc"] + off), C, SSD_CONV_K, S, nm(f"ssdconv_bw_{name}")))
        dxbc.append(_conv_bwd_in((darr, 0), C, w8, SSD_CONV_K, S, BF16, nm(f"ssdconv_bi_{name}")))
    cws = jnp.concatenate(cws, axis=1)
    g["ssd_conv_w"], g["ssd_conv_b"] = cws[:SSD_CONV_K], cws[SSD_CONV_K]
    T = dx2.shape[0]
    ddt_t = jnp.pad(ddt.transpose(1, 0, 2).reshape(T, H).astype(BF16), ((0, 0), (0, lay["sb"] - H)))
    dproj = [dz] + dxbc + [dscb, dscc, dsch, dg1, dg2_, ddt_t]
    dh = _mm_seg(dproj, p["wcat"], "nn", F32, nm("mm_in_bi"), lay["sb"])
    dwcat = _mm_seg(dproj, s["h"], "tn", WGRAD, nm("mm_in_bw"), lay["sb"], tk=1024)
    o = lay
    g["w_in"] = jnp.concatenate([dwcat[o["z"]:o["scb"]], dwcat[o["dt"]:o["dt"] + H], dwcat[o["scb"]:o["dt"]]], axis=0)
    dx, dg1_, dsc1, dsh1 = _pre_bwd(s["x"], p["mix_pre_g"], mod3, 1, dh, dx1, S, nm("norm1_b"))
    g["mix_pre_g"] = dg1_[0]
    dmod = jnp.concatenate([_seq_sum(t, nb) for t in (dsh1, dsc1, dgt1, dsh2, dsc2, dgt2)], axis=1)
    dmod8 = jnp.pad(dmod, ((0, MOD_ROWS - nb), (0, 0)))
    g["ada_b"] = _colsum(dmod8, nm("adab"))
    g["ada_w"] = _mm(dmod8, s["cact"], "tn", WGRAD, nm("mm_ada_bw"), caps=(1536, 1024, 2048))
    return dx, g, exchanged, None if early_got is None else early_got[0]


def _colsum(a8, name):
    rows, C = a8.shape
    tc = _tile(C, 2048)

    def body(a_ref, o_ref):
        o_ref[...] = _bsum(jnp.sum(a_ref[...], axis=0, keepdims=True))

    return _pc(body, grid=(C // tc,), in_specs=[pl.BlockSpec((rows, tc), lambda j: (0, j))],
               out_specs=pl.BlockSpec((8, tc), lambda j: (0, j)), out_shape=SDS((8, C), F32), name=name)(a8)[0]


def _adam(gs, w, m, v, name):
    ns, R, W = gs.shape
    tr = _tile(R, 256, 8)

    def body(g_ref, w_ref, m_ref, v_ref, go_ref, d_ref, mo_ref, vo_ref):
        g = g_ref[0].astype(F32)
        for k in range(1, ns):
            g = g + g_ref[k].astype(F32)
        go_ref[...] = g
        d_ref[...], mo_ref[...], vo_ref[...] = _adam_update(g, w_ref[...], m_ref[...], v_ref[...])

    row = pl.BlockSpec((tr, W), lambda i: (i, 0))
    return _pc(body, grid=(R // tr,), in_specs=[pl.BlockSpec((ns, tr, W), lambda i: (0, i, 0)), row, row, row],
               out_specs=[row] * 4, out_shape=[SDS((R, W), F32)] * 4, name=name)(gs, w, m, v)


def _adam_update(g, w, m, v):
    c1 = 1.0 / (1.0 - ADAM_B1 ** ADAM_STEP)
    c2 = 1.0 / (1.0 - ADAM_B2 ** ADAM_STEP)
    m_ = ADAM_B1 * m + (1.0 - ADAM_B1) * g
    v_ = ADAM_B2 * v + (1.0 - ADAM_B2) * (g * g)
    return -ADAM_LR * ((m_ * c1) / (jnp.sqrt(v_ * c2) + ADAM_EPS) + ADAM_WD * w), m_, v_


def _adam_nat(g, w, m, v, name):
    depth, a, b = w.shape
    tr = _tile(a, 256, 8)

    def body(g_ref, w_ref, m_ref, v_ref, d_ref, mo_ref, vo_ref):
        d_ref[...], mo_ref[...], vo_ref[...] = _adam_update(g_ref[...], w_ref[...], m_ref[...], v_ref[...])

    blk = pl.BlockSpec((None, tr, b), lambda l, i: (l, i, 0))
    return _pc(body, grid=(depth, a // tr), in_specs=[blk] * 4, out_specs=[blk] * 3,
               out_shape=[SDS(w.shape, F32)] * 3, name=name)(g, w, m, v)


def _adam_mid(g, w, m, v, name):
    b, depth, a = w.shape
    tr = 128

    def body(g_ref, w_ref, m_ref, v_ref, d_ref, mo_ref, vo_ref):
        d_ref[...], mo_ref[...], vo_ref[...] = _adam_update(g_ref[...], w_ref[...], m_ref[...], v_ref[...])

    blk = pl.BlockSpec((tr, depth, a), lambda i: (i, 0, 0))
    return _pc(body, grid=(pl.cdiv(b, tr),), in_specs=[blk] * 4, out_specs=[blk] * 3,
               out_shape=[SDS(w.shape, F32)] * 3, name=name)(g, w, m, v)


def _sum_chips(gs, name):
    ns, R, W = gs.shape
    tr = _tile(R, 256, 16)

    def body(g_ref, o_ref):
        acc = g_ref[0].astype(F32)
        for k in range(1, ns):
            acc = acc + g_ref[k].astype(F32)
        o_ref[...] = acc

    return _pc(body, grid=(R // tr,), in_specs=[pl.BlockSpec((ns, tr, W), lambda i: (0, i, 0))],
               out_specs=pl.BlockSpec((tr, W), lambda i: (i, 0)), out_shape=SDS((R, W), F32), name=name)(gs)


HBM_SPEC = pl.BlockSpec(memory_space=pltpu.HBM)
VMEM_SPEC = pl.BlockSpec(memory_space=pltpu.VMEM)


def _dev():
    return lax.axis_index("x"), lax.axis_index("y"), lax.axis_index("c")


def _allgather_big(loc, name):
    R, W = loc.shape

    def body(x_ref, out_ref, send_sems, recv_sems, local_sem):
        x, y, c = _dev()
        me, sibling = (x, y, c), (x, y, 1 - c)
        chips = [(1 - x, y), (x, 1 - y), (1 - x, 1 - y)]

        def slab(px, py, pc):
            return out_ref.at[4 * px + 2 * py + pc]

        def copy(k, block, to, src=None):
            return pltpu.make_async_remote_copy(
                src_ref=slab(*block) if src is None else src, dst_ref=slab(*block),
                send_sem=send_sems.at[k], recv_sem=recv_sems.at[k], device_id=to, device_id_type=MESH)

        mine = pltpu.make_async_copy(x_ref, slab(*me), local_sem)
        mine.start()
        first = [copy(0, me, sibling, src=x_ref)]
        first += [copy(1 + j, me, (*chip, c), src=x_ref) for j, chip in enumerate(chips)]
        for cp in first:
            cp.start()
        passed = [copy(4 + j, (*chip, c), sibling) for j, chip in enumerate(chips)]
        for j, chip in enumerate(chips):
            copy(1 + j, (*chip, c), me).wait_recv()
            passed[j].start()
        copy(0, sibling, me).wait_recv()
        for j, chip in enumerate(chips):
            copy(4 + j, (*chip, 1 - c), me).wait_recv()
        for cp in first + passed:
            cp.wait_send()
        mine.wait()

    return pl.pallas_call(
        body, out_shape=SDS((N_DEV, R, W), loc.dtype), in_specs=[HBM_SPEC], out_specs=HBM_SPEC,
        scratch_shapes=[pltpu.SemaphoreType.DMA((7,)), pltpu.SemaphoreType.DMA((7,)), pltpu.SemaphoreType.DMA],
        name=name)(loc)


def _dma_sems(n):
    return (pltpu.SemaphoreType.DMA((n,)), pltpu.SemaphoreType.DMA((n,)), pltpu.SemaphoreType.DMA)


def _side_gather_direct(loc):
    R, W = loc.shape

    def copies(ins, outs, sems):
        x, y, c = _dev()
        x_ref, out = ins[0], outs[0]
        me = 4 * x + 2 * y + c
        peers = [(x, y, 1 - c), (1 - x, y, c), (x, 1 - y, c), (1 - x, 1 - y, c)]
        mk = lambda k, p, dst: pltpu.make_async_remote_copy(
            src_ref=x_ref, dst_ref=out.at[dst], send_sem=sems[0].at[k], recv_sem=sems[1].at[k], device_id=p,
            device_id_type=MESH)
        sends = [mk(k, p, me) for k, p in enumerate(peers)]
        recvs = [mk(k, p, 4 * p[0] + 2 * p[1] + p[2]) for k, p in enumerate(peers)]
        return sends, recvs, pltpu.make_async_copy(x_ref, out.at[me], sems[2])

    def start(ins, outs, sems):
        sends, _, mine = copies(ins, outs, sems)
        mine.start()
        for cp in sends:
            cp.start()

    def wait(ins, outs, sems):
        sends, recvs, mine = copies(ins, outs, sems)
        for cp in recvs:
            cp.wait_recv()
        for cp in sends:
            cp.wait_send()
        mine.wait()

    return _Side((loc,), (SDS((N_DEV, R, W), loc.dtype),), _dma_sems(4), start, wait)


def _side_gather_forward(buf):
    def copies(ins, outs, sems):
        x, y, c = _dev()
        out = outs[0]
        chips = [(1 - x, y), (x, 1 - y), (1 - x, 1 - y)]
        mk = lambda k, src, dst: pltpu.make_async_remote_copy(
            src_ref=out.at[src], dst_ref=out.at[dst], send_sem=sems[0].at[k], recv_sem=sems[1].at[k],
            device_id=(x, y, 1 - c), device_id_type=MESH)
        mine = [4 * px + 2 * py + c for px, py in chips]
        theirs = [4 * px + 2 * py + (1 - c) for px, py in chips]
        return [mk(k, s, s) for k, s in enumerate(mine)], [mk(k, s, t) for k, (s, t) in enumerate(zip(mine, theirs))]

    def start(ins, outs, sems):
        for cp in copies(ins, outs, sems)[0]:
            cp.start()

    def wait(ins, outs, sems):
        sends, recvs = copies(ins, outs, sems)
        for cp in recvs:
            cp.wait_recv()
        for cp in sends:
            cp.wait_send()

    return _Side((buf,), (SDS(buf.shape, buf.dtype),), _dma_sems(3)[:2], start, wait, {0: 0})


def _side_chip_exchange(p):
    def copies(ins, outs, sems):
        x, y, c = _dev()
        p_ref, out = ins[0], outs[0]
        j0 = 2 * x + y
        chips = [(1 - x, y), (x, 1 - y), (1 - x, 1 - y)]
        mk = lambda k, chip, src, dst: pltpu.make_async_remote_copy(
            src_ref=p_ref.at[src], dst_ref=out.at[dst], send_sem=sems[0].at[k], recv_sem=sems[1].at[k],
            device_id=(*chip, c), device_id_type=MESH)
        sends = [mk(k, chip, 2 * chip[0] + chip[1], j0) for k, chip in enumerate(chips)]
        recvs = [mk(k, chip, j0, 2 * chip[0] + chip[1]) for k, chip in enumerate(chips)]
        return sends, recvs, pltpu.make_async_copy(p_ref.at[j0], out.at[j0], sems[2])

    def start(ins, outs, sems):
        sends, _, mine = copies(ins, outs, sems)
        mine.start()
        for cp in sends:
            cp.start()

    def wait(ins, outs, sems):
        sends, recvs, mine = copies(ins, outs, sems)
        for cp in recvs:
            cp.wait_recv()
        for cp in sends:
            cp.wait_send()
        mine.wait()

    return _Side((p,), (SDS(p.shape, p.dtype),), _dma_sems(3), start, wait)


def _rs_pair_exchange(g, name):
    nd, R, W = g.shape
    nj = nd // 2

    def body(g_ref, out_ref, send_sems, recv_sems):
        x, y, c = _dev()
        cps = [pltpu.make_async_remote_copy(src_ref=g_ref.at[2 * j + (1 - c)], dst_ref=out_ref.at[j],
                                            send_sem=send_sems.at[j], recv_sem=recv_sems.at[j],
                                            device_id=(x, y, 1 - c), device_id_type=MESH) for j in range(nj)]
        for cp in cps:
            cp.start()
        for cp in cps:
            cp.wait()

    return pl.pallas_call(
        body, out_shape=SDS((nj, R, W), g.dtype), in_specs=[HBM_SPEC], out_specs=HBM_SPEC,
        scratch_shapes=[pltpu.SemaphoreType.DMA((nj,)), pltpu.SemaphoreType.DMA((nj,))], name=name)(g)


def _add_pairs(g, ra, name):
    nd, R, W = g.shape
    nj = nd // 2
    tr = _tile(R, 256, 8)
    cidx = lax.axis_index("c").astype(jnp.int32).reshape(1)

    def body(c_ref, a_ref, b_ref, o_ref):
        o_ref[...] = (a_ref[...].astype(F32) + b_ref[...].astype(F32)).astype(o_ref.dtype)

    gs = pltpu.PrefetchScalarGridSpec(
        num_scalar_prefetch=1, grid=(nj, R // tr),
        in_specs=[pl.BlockSpec((None, tr, W), lambda j, i, cr: (2 * j + cr[0], i, 0)),
                  pl.BlockSpec((None, tr, W), lambda j, i, cr: (j, i, 0))],
        out_specs=pl.BlockSpec((None, tr, W), lambda j, i, cr: (j, i, 0)))
    return pl.pallas_call(body, grid_spec=gs, out_shape=SDS((nj, R, W), g.dtype), name=name,
                          compiler_params=pltpu.CompilerParams(vmem_limit_bytes=VMEM_LIMIT))(cidx, g, ra)


def _rs_chip_exchange(p, name):
    nj, R, W = p.shape

    def body(p_ref, out_ref, send_sems, recv_sems, local_sem):
        x, y, c = _dev()
        j0 = 2 * x + y
        chips = [(1 - x, y), (x, 1 - y), (1 - x, 1 - y)]
        mine = pltpu.make_async_copy(p_ref.at[j0], out_ref.at[j0], local_sem)
        mine.start()

        def copy(k, chip):
            return pltpu.make_async_remote_copy(
                src_ref=p_ref.at[2 * chip[0] + chip[1]], dst_ref=out_ref.at[j0],
                send_sem=send_sems.at[k], recv_sem=recv_sems.at[k], device_id=(*chip, c), device_id_type=MESH)

        sent = [copy(k, chip) for k, chip in enumerate(chips)]
        for cp in sent:
            cp.start()
        for k, chip in enumerate(chips):
            pltpu.make_async_remote_copy(
                src_ref=p_ref.at[j0], dst_ref=out_ref.at[2 * chip[0] + chip[1]],
                send_sem=send_sems.at[k], recv_sem=recv_sems.at[k], device_id=(*chip, c), device_id_type=MESH).wait_recv()
        for cp in sent:
            cp.wait_send()
        mine.wait()

    return pl.pallas_call(
        body, out_shape=SDS((nj, R, W), p.dtype), in_specs=[HBM_SPEC], out_specs=HBM_SPEC,
        scratch_shapes=[pltpu.SemaphoreType.DMA((3,)), pltpu.SemaphoreType.DMA((3,)), pltpu.SemaphoreType.DMA],
        name=name)(p)


def _allgather_small(v, name):
    R, W = v.shape

    def body(v_ref, out_ref, send_sems, recv_sems, local_sem):
        x, y, c = _dev()
        mine = pltpu.make_async_copy(v_ref, out_ref.at[4 * x + 2 * y + c], local_sem)
        mine.start()
        peers = []
        for k in range(1, N_DEV):
            px = 1 - x if k & 4 else x
            py = 1 - y if k & 2 else y
            pc_ = 1 - c if k & 1 else c
            peers.append((px, py, pc_))
        sent = [pltpu.make_async_remote_copy(
            src_ref=v_ref, dst_ref=out_ref.at[4 * x + 2 * y + c], send_sem=send_sems.at[k], recv_sem=recv_sems.at[k],
            device_id=peer, device_id_type=MESH) for k, peer in enumerate(peers)]
        for cp in sent:
            cp.start()
        for k, (px, py, pc_) in enumerate(peers):
            pltpu.make_async_remote_copy(
                src_ref=v_ref, dst_ref=out_ref.at[4 * px + 2 * py + pc_], send_sem=send_sems.at[k],
                recv_sem=recv_sems.at[k], device_id=(px, py, pc_), device_id_type=MESH).wait_recv()
        for cp in sent:
            cp.wait_send()
        mine.wait()

    return pl.pallas_call(
        body, out_shape=SDS((N_DEV, R, W), v.dtype), in_specs=[VMEM_SPEC], out_specs=VMEM_SPEC,
        scratch_shapes=[pltpu.SemaphoreType.DMA((7,)), pltpu.SemaphoreType.DMA((7,)), pltpu.SemaphoreType.DMA],
        name=name)(v)


def _sum_slabs(a, name):
    ns, R, W = a.shape

    def body(a_ref, o_ref):
        acc = a_ref[0]
        for k in range(1, ns):
            acc = acc + a_ref[k]
        o_ref[...] = acc

    return pl.pallas_call(body, out_shape=SDS((R, W), a.dtype), in_specs=[VMEM_SPEC], out_specs=VMEM_SPEC, name=name)(a)


BIG = (("ada_w", "col"), ("w_in", "col"), ("w_ssd_out", "row"), ("w_sc_out", "row"), ("w_o", "row"), ("w_up", "col"),
       ("w_down", "row"))
EARLY = ("w_ssd_out", "w_sc_out", "w_o", "w_up", "w_down")
LATE = ("ada_w", "w_in")
MID_LAYOUT = ("w_in",)
SWAP_LAYOUT = ("w_up",)
CONVW = ("ssd_conv_w", "sc_conv_w", "ffn_conv_w")
REPL = ("ada_b", "mix_pre_g", "mix_post_g", "ssd_conv_b", "ssd_dt_bias", "ssd_a_log", "ssd_d", "ssd_norm_g", "ffn_pre_g",
        "ffn_post_g", "ffn_conv_b")
WEIGHTS = ("ada_w", "ada_b", "mix_pre_g", "mix_post_g", "w_in", "ssd_conv_w", "ssd_conv_b", "ssd_dt_bias", "ssd_a_log",
           "ssd_d", "ssd_norm_g", "w_ssd_out", "sc_conv_w", "w_sc_out", "w_o", "ffn_pre_g", "ffn_post_g", "w_up",
           "ffn_conv_w", "ffn_conv_b", "w_down")


def _pad_rows(a, mult):
    r = a.shape[-2]
    pad = -r % mult
    return a if pad == 0 else jnp.pad(a, [(0, 0)] * (a.ndim - 2) + [(0, pad), (0, 0)])


def _flat_rows(parts, mult):
    flat = jnp.concatenate([p.reshape(-1) for p in parts])
    flat = jnp.pad(flat, (0, -flat.shape[0] % ROW_W))
    return _pad_rows(flat.reshape(-1, ROW_W), mult)


def _unflat(buf, shapes):
    flat = buf.reshape(-1)
    out, o = [], 0
    for shp in shapes:
        n = 1
        for s in shp:
            n *= s
        out.append(flat[o:o + n].reshape(shp))
        o += n
    return out


def _pack_big_local(get, l):
    return [_pad_rows((get(n)[l].T if kind == "col" else get(n)[l]).reshape(-1, ROW_W), SLAB_ALIGN) for n, kind in BIG]


def _big_rows(shapes, names=None):
    out, o = {}, 0
    for n in (names if names is not None else [n for n, _ in BIG]):
        r = shapes[n][1] * shapes[n][2] // ROW_W
        out[n] = (o, o + r)
        o += -(-r // SLAB_ALIGN) * SLAB_ALIGN
    return out, o


def kernel(x, c, ada_w, ada_b, mix_pre_g, mix_post_g, w_in, ssd_conv_w, ssd_conv_b, ssd_dt_bias, ssd_a_log, ssd_d, ssd_norm_g, w_ssd_out, sc_conv_w, w_sc_out, w_o, ffn_pre_g, ffn_post_g, w_up, ffn_conv_w, ffn_conv_b, w_down, loss_target, m_ada_w, m_ada_b, m_mix_pre_g, m_mix_post_g, m_w_in, m_ssd_conv_w, m_ssd_conv_b, m_ssd_dt_bias, m_ssd_a_log, m_ssd_d, m_ssd_norm_g, m_w_ssd_out, m_sc_conv_w, m_w_sc_out, m_w_o, m_ffn_pre_g, m_ffn_post_g, m_w_up, m_ffn_conv_w, m_ffn_conv_b, m_w_down, v_ada_w, v_ada_b, v_mix_pre_g, v_mix_post_g, v_w_in, v_ssd_conv_w, v_ssd_conv_b, v_ssd_dt_bias, v_ssd_a_log, v_ssd_d, v_ssd_norm_g, v_w_ssd_out, v_sc_conv_w, v_w_sc_out, v_w_o, v_ffn_pre_g, v_ffn_post_g, v_w_up, v_ffn_conv_w, v_ffn_conv_b, v_w_down):
    wl = dict(zip(WEIGHTS, (ada_w, ada_b, mix_pre_g, mix_post_g, w_in, ssd_conv_w, ssd_conv_b, ssd_dt_bias, ssd_a_log,
                            ssd_d, ssd_norm_g, w_ssd_out, sc_conv_w, w_sc_out, w_o, ffn_pre_g, ffn_post_g, w_up,
                            ffn_conv_w, ffn_conv_b, w_down)))
    ml = dict(zip(WEIGHTS, (m_ada_w, m_ada_b, m_mix_pre_g, m_mix_post_g, m_w_in, m_ssd_conv_w, m_ssd_conv_b,
                            m_ssd_dt_bias, m_ssd_a_log, m_ssd_d, m_ssd_norm_g, m_w_ssd_out, m_sc_conv_w, m_w_sc_out, m_w_o,
                            m_ffn_pre_g, m_ffn_post_g, m_w_up, m_ffn_conv_w, m_ffn_conv_b, m_w_down)))
    vl = dict(zip(WEIGHTS, (v_ada_w, v_ada_b, v_mix_pre_g, v_mix_post_g, v_w_in, v_ssd_conv_w, v_ssd_conv_b,
                            v_ssd_dt_bias, v_ssd_a_log, v_ssd_d, v_ssd_norm_g, v_w_ssd_out, v_sc_conv_w, v_w_sc_out, v_w_o,
                            v_ffn_pre_g, v_ffn_post_g, v_w_up, v_ffn_conv_w, v_ffn_conv_b, v_w_down)))
    depth = ada_w.shape[0]
    shapes = {n: wl[n].shape for n in WEIGHTS}
    me = 4 * lax.axis_index("x") + 2 * lax.axis_index("y") + lax.axis_index("c")

    rows, n_big = _big_rows(shapes)
    conv_flat = jnp.concatenate([wl[n][l].reshape(-1) for l in range(depth) for n in CONVW])
    n_conv = conv_flat.shape[0]
    conv_flat = jnp.pad(conv_flat, (0, -n_conv % (ROW_W // 2)))
    conv_rows = lax.bitcast_convert_type(conv_flat, BF16).reshape(-1, ROW_W)

    def local_rows(l):
        pieces = _pack_big_local(lambda n: wl[n].astype(BF16), l) + ([conv_rows] if l == 0 else [])
        return _pad_rows(jnp.concatenate(pieces, axis=0), ROW_PAD)

    def layer_weights(l, gathered):
        w = {n: wl[n][l] for n in REPL}
        for n, kind in BIG:
            a, b = shapes[n][1], shapes[n][2]
            blk = gathered[:, rows[n][0]:rows[n][1]]
            w[n] = blk.reshape(N_DEV * b, a) if kind == "col" else blk.reshape(N_DEV * a, b)
        for n in CONVW:
            w[n] = conv_full[(l, n)]
        return w

    gathered = _allgather_big(local_rows(0), "allgather_weights")
    conv_all = lax.bitcast_convert_type(
        gathered[:, n_big:n_big + conv_rows.shape[0]].reshape(N_DEV, -1, 2), F32)[:, :n_conv]
    conv_full, o = {}, 0
    for l in range(depth):
        for n in CONVW:
            k, cl = shapes[n][1], shapes[n][2]
            conv_full[(l, n)] = conv_all[:, o:o + k * cl].reshape(N_DEV, k, cl).transpose(1, 0, 2).reshape(k, N_DEV * cl)
            o += k * cl

    nb, S, D = x.shape
    T = nb * S
    act = x.reshape(T, D)
    c8 = jnp.pad(c, ((0, MOD_ROWS - nb), (0, 0)))
    preps, saved = [], []
    for l in range(depth):
        preps.append(_prep_layer(layer_weights(l, gathered)))
        act, s, gathered = _layer_fwd(act, c8, preps[l], S, l, gather=local_rows(l + 1) if l + 1 < depth else None)
        saved.append(s)
    dy, lacc = _loss(act, loss_target.reshape(T, D), S, "loss")
    loss_loc = lacc[0, 0]

    group_rows = {grp: _big_rows(shapes, names) for grp, names in (("early", EARLY), ("late", LATE))}

    def pair_sums(g, grp, names, l):
        slabs = [_pad_rows(g[n].astype(BF16).reshape(N_DEV, -1, ROW_W), SLAB_ALIGN) for n in names]
        slabs.append(jnp.zeros((N_DEV, -group_rows[grp][1] % ROW_PAD, ROW_W), BF16))
        gslab = jnp.concatenate(slabs, axis=1)
        from_sibling = _rs_pair_exchange(gslab, f"rs_pair_exchange_{grp}_l{l}")
        return _add_pairs(gslab, from_sibling, f"rs_pair_add_{grp}_l{l}")

    grads, pending = [None] * depth, None
    from_chips = {"early": [None] * depth, "late": [None] * depth}
    for l in reversed(range(depth)):
        dy, grads[l], got, from_chips["early"][l] = _layer_bwd(
            dy, preps[l], saved[l], S, l, chip_sums=pending, early=lambda g, l=l: pair_sums(g, "early", EARLY, l))
        if pending is not None:
            from_chips["late"][l + 1] = got
        pending = pair_sums(grads[l], "late", LATE, l)
    from_chips["late"][0] = _rs_chip_exchange(pending, "rs_chip_exchange")
    dx = dy.reshape(nb, S, D)
    g_sums = {grp: [_sum_chips(from_chips[grp][l], f"rs_chip_sum_{grp}_l{l}") for l in range(depth)]
              for grp in ("early", "late")}

    def slab_of(l, n, kind):
        grp = "early" if n in EARLY else "late"
        r0, r1 = group_rows[grp][0][n]
        a, b = shapes[n][1], shapes[n][2]
        return g_sums[grp][l][r0:r1].reshape((b, a) if kind == "col" else (a, b))

    g_big, d_big, m_big, v_big = {}, {}, {}, {}
    for n, kind in BIG:
        if n in MID_LAYOUT:
            gm = jnp.stack([slab_of(l, n, kind) for l in range(depth)], axis=1)
            fwd, back = (lambda t: t.transpose(2, 0, 1)), (lambda t: t.transpose(1, 2, 0))
            res = [gm] + list(_adam_mid(gm, fwd(wl[n]), fwd(ml[n]), fwd(vl[n]), f"adam_{n}"))
        else:
            gt = jnp.stack([slab_of(l, n, kind) for l in range(depth)])
            fwd = back = (lambda t: t.swapaxes(1, 2)) if kind == "col" else (lambda t: t)
            if n in SWAP_LAYOUT:
                res = [gt] + list(_adam_nat(gt, fwd(wl[n]), fwd(ml[n]), fwd(vl[n]), f"adam_{n}"))
            else:
                gn = back(gt)
                res, back = [gn] + list(_adam_nat(gn, wl[n], ml[n], vl[n], f"adam_{n}")), (lambda t: t)
        g_big[n], d_big[n], m_big[n], v_big[n] = [back(t) for t in res]

    parts = [jnp.broadcast_to(loss_loc, (ROW_W,))]
    small_shapes = [(ROW_W,)]
    for l in range(depth):
        for n in REPL + CONVW:
            parts.append(grads[l][n])
            small_shapes.append(tuple(grads[l][n].shape))
    total = _sum_slabs(_allgather_small(_flat_rows(parts, 8), "allgather_small"), "sum_small")
    pieces = _unflat(total, small_shapes)
    loss = pieces[0][0]
    g_small, i = {}, 1
    for l in range(depth):
        for n in REPL + CONVW:
            gp = pieces[i]
            i += 1
            if n in CONVW:
                gp = lax.dynamic_slice_in_dim(gp, me * shapes[n][2], shapes[n][2], axis=1)
            g_small[(l, n)] = gp
    order = [(l, n) for l in range(depth) for n in REPL + CONVW]
    loc_shapes = [tuple(shapes[n][1:]) for _, n in order]
    packs = lambda f: _flat_rows([f(l, n) for l, n in order], 8)
    gs_small = packs(lambda l, n: g_small[(l, n)])
    _, d_sm, m_sm, v_sm = _adam(gs_small[None], packs(lambda l, n: wl[n][l]), packs(lambda l, n: ml[n][l]),
                                packs(lambda l, n: vl[n][l]), "adam_small")

    def unpack_small(buf):
        ps = _unflat(buf, loc_shapes)
        return {n: jnp.stack([ps[order.index((l, n))] for l in range(depth)]) for n in REPL + CONVW}

    outs = []
    for big, small in ((g_big, {n: jnp.stack([g_small[(l, n)] for l in range(depth)]) for n in REPL + CONVW}),
                       (d_big, unpack_small(d_sm)), (m_big, unpack_small(m_sm)), (v_big, unpack_small(v_sm))):
        merged = {**big, **small}
        outs += [merged[n] for n in WEIGHTS]
    return (loss, dx, *outs)
```

```python
import functools
import math
from typing import Callable, NamedTuple

import jax
import jax.numpy as jnp
from jax import lax
from jax.experimental import pallas as pl
from jax.experimental.pallas import tpu as pltpu

F32, BF16 = jnp.float32, jnp.bfloat16
WGRAD = BF16
SDS = jax.ShapeDtypeStruct
MESH = pl.DeviceIdType.MESH

EPS = 1e-6
N_STATE = 128
CHUNK = 128
SSD_CONV_K, SC_CONV_K, FFN_CONV_K = 4, 3, 3
N_DEV = 8
ROW_W = 1024
ROW_PAD = 32
SLAB_ALIGN = 16
SEG_BLK = 512
STRIP = 32
FFN_STRIP = 64
GLU_W = 256
MOD_ROWS = 128
VMEM_LIMIT = 48 * 2**20

ADAM_LR, ADAM_B1, ADAM_B2, ADAM_EPS, ADAM_WD, ADAM_STEP = 0.001, 0.9, 0.999, 1e-08, 0.01, 10

NT = (((1,), (1,)), ((), ()))
TN = (((0,), (0,)), ((), ()))
NN = (((1,), (0,)), ((), ()))


def _tile(n, cap, mult=128):
    best = None
    for t in range(mult, min(n, cap) + 1, mult):
        if n % t == 0:
            best = t
    return best if best is not None else n


class _Side(NamedTuple):
    operands: tuple
    out_shape: tuple
    scratch: tuple
    start: Callable
    wait: Callable
    aliases: dict = {}


def _pc(body, *, grid, in_specs, out_specs, out_shape, name, scratch=(), side=None):
    params = pltpu.CompilerParams(dimension_semantics=("arbitrary",) * len(grid), vmem_limit_bytes=VMEM_LIMIT)
    if side is None:
        return pl.pallas_call(body, grid=grid, in_specs=in_specs, out_specs=out_specs, out_shape=out_shape,
                              scratch_shapes=list(scratch), name=name, compiler_params=params)
    single = not isinstance(out_shape, (list, tuple))
    outs = [out_shape] if single else list(out_shape)
    ospecs = [out_specs] if single else list(out_specs)
    n_in, n_out, n_scr = len(in_specs), len(outs), len(scratch)
    s_in, s_out = len(side.operands), len(side.out_shape)

    def hosted(*refs):
        ins, refs = refs[:n_in], refs[n_in:]
        sins, refs = refs[:s_in], refs[s_in:]
        mouts, refs = refs[:n_out], refs[n_out:]
        souts, refs = refs[:s_out], refs[s_out:]
        scr, sems = refs[:n_scr], refs[n_scr:]
        first = functools.reduce(lambda a, b: a & b, [pl.program_id(a) == 0 for a in range(len(grid))])
        last = functools.reduce(lambda a, b: a & b, [pl.program_id(a) == grid[a] - 1 for a in range(len(grid))])

        @pl.when(first)
        def _():
            side.start(sins, souts, sems)

        body(*ins, *mouts, *scr)

        @pl.when(last)
        def _():
            side.wait(sins, souts, sems)

    call = pl.pallas_call(
        hosted, grid=grid, in_specs=list(in_specs) + [HBM_SPEC] * s_in, out_specs=ospecs + [HBM_SPEC] * s_out,
        out_shape=outs + list(side.out_shape), scratch_shapes=list(scratch) + list(side.scratch), name=name,
        input_output_aliases={n_in + k: n_out + v for k, v in side.aliases.items()}, compiler_params=params)

    def run(*args):
        res = call(*args, *side.operands)
        main = res[0] if single else list(res[:n_out])
        return main, list(res[n_out:])

    return run


def _silu(x):
    return x * jax.nn.sigmoid(x)


def _dsilu(x):
    s = jax.nn.sigmoid(x)
    return s * (1.0 + x * (1.0 - s))


def _softplus(x):
    return jnp.maximum(x, 0.0) + jnp.log(1.0 + jnp.exp(-jnp.abs(x)))


def _dot(a, b, dims=NN):
    return lax.dot_general(a, b, dims, preferred_element_type=F32)


def _bsum(v, rows=8):
    return jnp.broadcast_to(v, (rows, v.shape[1]))


def _mm(a, b, mode, out_dtype, name, caps=(1024, 1024, 2048), side=None):
    if mode == "nn":
        (M, K), (K2, N) = a.shape, b.shape
    elif mode == "nt":
        (M, K), (N, K2) = a.shape, b.shape
    else:
        (K, M), (K2, N) = a.shape, b.shape
    assert K == K2, (a.shape, b.shape, mode)
    tm, tn, tk = _tile(M, caps[0]), _tile(N, caps[1]), _tile(K, caps[2])
    nk = K // tk
    dims = {"nn": NN, "nt": NT, "tn": TN}[mode]
    if mode == "tn":
        a_spec = pl.BlockSpec((tk, tm), lambda i, j, k: (k, i))
    else:
        a_spec = pl.BlockSpec((tm, tk), lambda i, j, k: (i, k))
    if mode == "nt":
        b_spec = pl.BlockSpec((tn, tk), lambda i, j, k: (j, k))
    else:
        b_spec = pl.BlockSpec((tk, tn), lambda i, j, k: (k, j))

    def body(a_ref, b_ref, o_ref, *acc):
        part = _dot(a_ref[...].astype(BF16), b_ref[...].astype(BF16), dims)
        if nk == 1:
            o_ref[...] = part.astype(o_ref.dtype)
        else:
            acc_ref, = acc
            k = pl.program_id(2)

            @pl.when(k == 0)
            def _():
                acc_ref[...] = part

            @pl.when(k > 0)
            def _():
                acc_ref[...] += part

            @pl.when(k == nk - 1)
            def _():
                o_ref[...] = acc_ref[...].astype(o_ref.dtype)

    return _pc(body, grid=(M // tm, N // tn, nk), in_specs=[a_spec, b_spec],
               out_specs=pl.BlockSpec((tm, tn), lambda i, j, k: (i, j)),
               out_shape=SDS((M, N), out_dtype), name=name,
               scratch=() if nk == 1 else (pltpu.VMEM((tm, tn), F32),), side=side)(a, b)


def _mm_seg(segs, b, mode, out_dtype, name, blk, tile=1024, tk=2048):
    nblk = [a.shape[1] // blk for a in segs]
    assert all(a.shape[1] % blk == 0 for a in segs)
    start = [sum(nblk[:s]) for s in range(len(segs))]
    total = sum(nblk)
    ns = len(segs)
    N = b.shape[1]
    tn = _tile(N, tile)
    if mode == "nn":
        M = segs[0].shape[0]
        tm = _tile(M, tile)
        grid = (M // tm, N // tn, total)
        a_specs = [pl.BlockSpec((tm, blk), lambda i, j, k, k0=k0, n=n: (i, jnp.clip(k - k0, 0, n - 1)))
                   for k0, n in zip(start, nblk)]
        b_spec = pl.BlockSpec((blk, tn), lambda i, j, k: (k, j))
        out_rows, tmo, dims, seg_axis = M, tm, NN, 2
    else:
        K = segs[0].shape[0]
        tkk = _tile(K, tk)
        grid = (total, N // tn, K // tkk)
        a_specs = [pl.BlockSpec((tkk, blk), lambda i, j, k, i0=i0, n=n: (
            jnp.where((i >= i0) & (i < i0 + n), k, 0), jnp.clip(i - i0, 0, n - 1))) for i0, n in zip(start, nblk)]
        b_spec = pl.BlockSpec((tkk, tn), lambda i, j, k: (k, j))
        out_rows, tmo, seg_axis = total * blk, blk, 0
    nk = grid[2]
    acc_shape = (tm, tn) if mode == "nn" else (tn, blk)

    def body(*refs):
        a_refs, b_ref, o_ref, acc_ref = refs[:ns], refs[ns], refs[ns + 1], refs[ns + 2]
        k = pl.program_id(2)
        sel = pl.program_id(seg_axis)

        @pl.when(k == 0)
        def _():
            acc_ref[...] = jnp.zeros_like(acc_ref)

        for s in range(ns):
            @pl.when((sel >= start[s]) & (sel < start[s] + nblk[s]))
            def _(s=s):
                a_, b_ = a_refs[s][...].astype(BF16), b_ref[...].astype(BF16)
                acc_ref[...] += _dot(a_, b_, NN) if mode == "nn" else _dot(b_, a_, TN)

        @pl.when(k == nk - 1)
        def _():
            acc = acc_ref[...]
            o_ref[...] = (acc if mode == "nn" else acc.T).astype(o_ref.dtype)

    return _pc(body, grid=grid, in_specs=a_specs + [b_spec], out_specs=pl.BlockSpec((tmo, tn), lambda i, j, k: (i, j)),
               out_shape=SDS((out_rows, N), out_dtype), name=name, scratch=(pltpu.VMEM(acc_shape, F32),))(*segs, b)


def _modk(c8, ada_w, ada_b, name):
    rows, D = c8.shape
    N = ada_w.shape[0]
    tn = _tile(N, 1536)

    def body(c_ref, w_ref, b_ref, mod_ref, ca_ref):
        ca = _silu(c_ref[...]).astype(BF16)
        mod_ref[...] = _dot(ca, w_ref[...], NT) + b_ref[...]
        ca_ref[...] = ca

    return _pc(body, grid=(N // tn,),
               in_specs=[pl.BlockSpec((rows, D), lambda j: (0, 0)), pl.BlockSpec((tn, D), lambda j: (j, 0)),
                         pl.BlockSpec((1, tn), lambda j: (0, j))],
               out_specs=[pl.BlockSpec((rows, tn), lambda j: (0, j)), pl.BlockSpec((rows, D), lambda j: (0, 0))],
               out_shape=[SDS((rows, N), F32), SDS((rows, D), BF16)], name=name)(c8, ada_w, ada_b)


def _row_tile(S):
    return _tile(S, 512, 8)


def _strip_row_tile(S, cap=2048):
    return _tile(S, cap, FFN_STRIP)


def _strips(tm, fn, init=0, rows=None):
    rows = STRIP if rows is None else rows
    assert tm % rows == 0
    return lax.fori_loop(0, tm // rows, lambda r, c: fn(pl.multiple_of(r * rows, rows), c), init)


def _strips_prev(tm, rows, ref, prev, fn, init=0):
    carry = fn(0, jnp.concatenate([prev, ref[0:rows, :]], axis=0), init)

    def step(r, c):
        r0 = pl.multiple_of(r * rows, rows)
        return fn(r0, ref[pl.ds(pl.multiple_of(r0 - 8, 8), rows + 8), :], c)

    return lax.fori_loop(1, tm // rows, step, carry)


def _strips_next(tm, rows, ref, nxt, fn, init=0):
    def step(r, c):
        r0 = pl.multiple_of(r * rows, rows)
        return fn(r0, ref[pl.ds(r0, rows + 8), :], c)

    carry = lax.fori_loop(0, tm // rows - 1, step, init)
    return fn(tm - rows, jnp.concatenate([ref[tm - rows:tm, :], nxt], axis=0), carry)


def _rows8(rows):
    pad = 8 - len(rows)
    return jnp.concatenate(rows + ([jnp.zeros((pad, rows[0].shape[1]), F32)] if pad else []), axis=0)


def _fold8(v):
    return jnp.sum(v.reshape(v.shape[0] // 8, 8, v.shape[1]), axis=0)


def _norm_mod(x, g, mod3, sc_seg, sh_seg, S, name):
    T, D = x.shape
    tm = _row_tile(S)
    tpb = S // tm

    def body(x_ref, g_ref, sc_ref, sh_ref, h_ref):
        x_ = x_ref[...]
        r = lax.rsqrt(jnp.mean(x_ * x_, axis=-1, keepdims=True) + EPS)
        h_ref[...] = ((x_ * r) * (g_ref[...] * (1.0 + sc_ref[...])) + sh_ref[...]).astype(BF16)

    return _pc(body, grid=(T // tm,),
               in_specs=[pl.BlockSpec((tm, D), lambda i: (i, 0)), pl.BlockSpec((1, D), lambda i: (0, 0)),
                         pl.BlockSpec((None, 1, D), lambda i: (i // tpb, 0, sc_seg)),
                         pl.BlockSpec((None, 1, D), lambda i: (i // tpb, 0, sh_seg))],
               out_specs=pl.BlockSpec((tm, D), lambda i: (i, 0)), out_shape=SDS((T, D), BF16), name=name)(x, g, mod3, mod3)


def _resid_post(x, fo, mod3, gt_seg, pg, S, name):
    T, D = x.shape
    tm = _row_tile(S)
    tpb = S // tm

    def body(x_ref, f_ref, gt_ref, pg_ref, o_ref):
        f = f_ref[...]
        r = lax.rsqrt(jnp.mean(f * f, axis=-1, keepdims=True) + EPS)
        o_ref[...] = x_ref[...] + (f * r) * (gt_ref[...] * pg_ref[...])

    return _pc(body, grid=(T // tm,),
               in_specs=[pl.BlockSpec((tm, D), lambda i: (i, 0)), pl.BlockSpec((tm, D), lambda i: (i, 0)),
                         pl.BlockSpec((None, 1, D), lambda i: (i // tpb, 0, gt_seg)),
                         pl.BlockSpec((1, D), lambda i: (0, 0))],
               out_specs=pl.BlockSpec((tm, D), lambda i: (i, 0)), out_shape=SDS((T, D), F32), name=name)(x, fo, mod3, pg)


def _post_bwd(fo, mod3, gt_seg, pg, dout, S, name):
    T, D = fo.shape
    tm = _row_tile(S)
    tpb = S // tm
    nb = T // S

    def body(f_ref, gt_ref, pg_ref, d_ref, df_ref, dgt_ref, dpg_ref):
        i = pl.program_id(0)

        @pl.when(i == 0)
        def _():
            dpg_ref[...] = jnp.zeros_like(dpg_ref)

        @pl.when(i % tpb == 0)
        def _():
            dgt_ref[...] = jnp.zeros_like(dgt_ref)

        f, d = f_ref[...], d_ref[...]
        r = lax.rsqrt(jnp.mean(f * f, axis=-1, keepdims=True) + EPS)
        n = f * r
        dn = d * (gt_ref[...] * pg_ref[...])
        df_ref[...] = (r * (dn - n * jnp.mean(dn * n, axis=-1, keepdims=True))).astype(df_ref.dtype)
        tot = jnp.sum(d * n, axis=0, keepdims=True)
        dgt_ref[...] += _bsum(tot * pg_ref[...])
        dpg_ref[...] += _bsum(tot * gt_ref[...])

    return _pc(body, grid=(T // tm,),
               in_specs=[pl.BlockSpec((tm, D), lambda i: (i, 0)),
                         pl.BlockSpec((None, 1, D), lambda i: (i // tpb, 0, gt_seg)),
                         pl.BlockSpec((1, D), lambda i: (0, 0)), pl.BlockSpec((tm, D), lambda i: (i, 0))],
               out_specs=[pl.BlockSpec((tm, D), lambda i: (i, 0)), pl.BlockSpec((8, D), lambda i: (i // tpb, 0)),
                          pl.BlockSpec((8, D), lambda i: (0, 0))],
               out_shape=[SDS((T, D), BF16), SDS((nb * 8, D), F32), SDS((8, D), F32)], name=name)(fo, mod3, pg, dout)


def _pre_bwd(x, g, mod3, sc_seg, dh, dout, S, name):
    T, D = x.shape
    tm = _row_tile(S)
    tpb = S // tm
    nb = T // S

    def body(x_ref, g_ref, sc_ref, dh_ref, d_ref, dx_ref, dg_ref, dsc_ref, dsh_ref):
        i = pl.program_id(0)

        @pl.when(i == 0)
        def _():
            dg_ref[...] = jnp.zeros_like(dg_ref)

        @pl.when(i % tpb == 0)
        def _():
            dsc_ref[...] = jnp.zeros_like(dsc_ref)
            dsh_ref[...] = jnp.zeros_like(dsh_ref)

        x_, dh_ = x_ref[...], dh_ref[...]
        r = lax.rsqrt(jnp.mean(x_ * x_, axis=-1, keepdims=True) + EPS)
        n = x_ * r
        dn = dh_ * (g_ref[...] * (1.0 + sc_ref[...]))
        dx_ref[...] = d_ref[...] + r * (dn - n * jnp.mean(dn * n, axis=-1, keepdims=True))
        dhn = jnp.sum(dh_ * n, axis=0, keepdims=True)
        dg_ref[...] += _bsum(dhn * (1.0 + sc_ref[...]))
        dsc_ref[...] += _bsum(dhn * g_ref[...])
        dsh_ref[...] += _bsum(jnp.sum(dh_, axis=0, keepdims=True))

    row = pl.BlockSpec((tm, D), lambda i: (i, 0))
    return _pc(body, grid=(T // tm,),
               in_specs=[row, pl.BlockSpec((1, D), lambda i: (0, 0)),
                         pl.BlockSpec((None, 1, D), lambda i: (i // tpb, 0, sc_seg)), row, row],
               out_specs=[row, pl.BlockSpec((8, D), lambda i: (0, 0)), pl.BlockSpec((8, D), lambda i: (i // tpb, 0)),
                          pl.BlockSpec((8, D), lambda i: (i // tpb, 0))],
               out_shape=[SDS((T, D), F32), SDS((8, D), F32), SDS((nb * 8, D), F32), SDS((nb * 8, D), F32)],
               name=name)(x, g, mod3, dh, dout)


def _loss(y, target, S, name):
    T, D = y.shape
    tm = _row_tile(S)

    def body(y_ref, t_ref, dy_ref, l_ref):
        @pl.when(pl.program_id(0) == 0)
        def _():
            l_ref[...] = jnp.zeros_like(l_ref)

        def strip(r0, carry):
            rows = pl.ds(r0, STRIP)
            e = y_ref[rows, :] - t_ref[rows, :]
            dy_ref[rows, :] = e * (1.0 / D)
            return carry + _fold8(e * e)

        acc = _strips(tm, strip, jnp.zeros((8, D), F32))
        l_ref[...] += jnp.broadcast_to(jnp.sum(acc, keepdims=True) * (0.5 / D), l_ref.shape)

    row = pl.BlockSpec((tm, D), lambda i: (i, 0))
    return _pc(body, grid=(T // tm,), in_specs=[row, row],
               out_specs=[row, pl.BlockSpec((8, 128), lambda i: (0, 0))],
               out_shape=[SDS((T, D), F32), SDS((8, 128), F32)], name=name)(y, target)


def _conv_geom(view, C, S, cap=2048):
    arr, off = view
    T = arr.shape[0]
    tm = _strip_row_tile(S, cap)
    tc = _tile(C, 512)
    assert off % tc == 0 and C % tc == 0
    return arr, off // tc, T, tm, tc, S // tm


def _prev_spec(tm, tc, ob, order):
    if order == "ij":
        return pl.BlockSpec((8, tc), lambda i, j: (jnp.maximum(i * (tm // 8) - 1, 0), ob + j))
    return pl.BlockSpec((8, tc), lambda j, i: (jnp.maximum(i * (tm // 8) - 1, 0), ob + j))


def _next_spec(T, tm, tc, ob, order):
    last = T // 8 - 1
    if order == "ij":
        return pl.BlockSpec((8, tc), lambda i, j: (jnp.minimum((i + 1) * (tm // 8), last), ob + j))
    return pl.BlockSpec((8, tc), lambda j, i: (jnp.minimum((i + 1) * (tm // 8), last), ob + j))


def _taps(win, w_ref, K, lead, rows):
    acc = win[lead:lead + rows] * w_ref[K - 1:K, :]
    for j in range(1, K):
        acc = acc + win[lead - j:lead - j + rows] * w_ref[K - 1 - j:K - j, :]
    return acc


def _taps_t(win, w_ref, K, rows):
    acc = win[0:rows] * w_ref[K - 1:K, :]
    for j in range(1, K):
        acc = acc + win[j:j + rows] * w_ref[K - 1 - j:K - j, :]
    return acc


def _conv_fwd(view, C, w8, b, K, S, name):
    arr, ob, T, tm, tc, tps = _conv_geom(view, C, S)

    def body(u_ref, p_ref, w_ref, b_ref, o_ref):
        first = (pl.program_id(0) % tps) == 0

        def strip(r0, win, carry):
            o_ref[pl.ds(r0, STRIP), :] = _taps(win, w_ref, K, 8, STRIP) + b_ref[...]
            return carry

        _strips_prev(tm, STRIP, u_ref, jnp.where(first, 0.0, p_ref[...]), strip)

    return _pc(body, grid=(T // tm, C // tc),
               in_specs=[pl.BlockSpec((tm, tc), lambda i, j: (i, ob + j)), _prev_spec(tm, tc, ob, "ij"),
                         pl.BlockSpec((8, tc), lambda i, j: (0, j)), pl.BlockSpec((1, tc), lambda i, j: (0, j))],
               out_specs=pl.BlockSpec((tm, tc), lambda i, j: (i, j)), out_shape=SDS((T, C), F32), name=name)(
                   arr, arr, w8, b)


def _conv_bwd_in(dview, C, w8, K, S, out_dtype, name):
    arr, ob, T, tm, tc, tps = _conv_geom(dview, C, S)

    def body(d_ref, n_ref, w_ref, o_ref):
        last = (pl.program_id(0) % tps) == tps - 1

        def strip(r0, win, carry):
            o_ref[pl.ds(r0, STRIP), :] = _taps_t(win, w_ref, K, STRIP).astype(o_ref.dtype)
            return carry

        _strips_next(tm, STRIP, d_ref, jnp.where(last, 0.0, n_ref[...]), strip)

    return _pc(body, grid=(T // tm, C // tc),
               in_specs=[pl.BlockSpec((tm, tc), lambda i, j: (i, ob + j)), _next_spec(T, tm, tc, ob, "ij"),
                         pl.BlockSpec((8, tc), lambda i, j: (0, j))],
               out_specs=pl.BlockSpec((tm, tc), lambda i, j: (i, j)), out_shape=SDS((T, C), out_dtype), name=name)(
                   arr, arr, w8)


def _conv_bwd_w(dview, uview, C, K, S, name):
    darr, dob, T, tm, tc, tps = _conv_geom(dview, C, S)
    uarr, uob, _, _, _, _ = _conv_geom(uview, C, S)

    def body(d_ref, u_ref, p_ref, o_ref):
        i = pl.program_id(1)

        @pl.when(i == 0)
        def _():
            o_ref[...] = jnp.zeros_like(o_ref)

        first = (i % tps) == 0

        def strip(r0, win, carry):
            d = d_ref[pl.ds(r0, STRIP), :]
            sums = [_fold8(d * win[8 - (K - 1 - k):8 - (K - 1 - k) + STRIP]) for k in range(K)] + [_fold8(d)]
            return tuple(c + s for c, s in zip(carry, sums))

        acc = _strips_prev(tm, STRIP, u_ref, jnp.where(first, 0.0, p_ref[...]), strip,
                           tuple(jnp.zeros((8, tc), F32) for _ in range(K + 1)))
        o_ref[...] += _rows8([jnp.sum(a, axis=0, keepdims=True) for a in acc])

    return _pc(body, grid=(C // tc, T // tm),
               in_specs=[pl.BlockSpec((tm, tc), lambda j, i: (i, dob + j)),
                         pl.BlockSpec((tm, tc), lambda j, i: (i, uob + j)), _prev_spec(tm, tc, uob, "ji")],
               out_specs=pl.BlockSpec((8, tc), lambda j, i: (0, j)), out_shape=SDS((8, C), F32), name=name)(
                   darr, uarr, uarr)


def _ffn_act_fwd(uu, w8, b, S, name):
    K, gw = FFN_CONV_K, GLU_W
    T, F2 = uu.shape
    tm, tc = _strip_row_tile(S), 2 * GLU_W
    tps = S // tm

    def body(u_ref, p_ref, w_ref, b_ref, a_ref):
        first = (pl.program_id(0) % tps) == 0

        def strip(r0, win, carry):
            u = _taps(win, w_ref, K, 8, STRIP) + b_ref[...]
            a_ref[pl.ds(r0, STRIP), :] = (_silu(u[:, :gw]) * u[:, gw:]).astype(BF16)
            return carry

        _strips_prev(tm, STRIP, u_ref, jnp.where(first, 0.0, p_ref[...]), strip)

    return _pc(body, grid=(T // tm, F2 // tc),
               in_specs=[pl.BlockSpec((tm, tc), lambda i, j: (i, j)), _prev_spec(tm, tc, 0, "ij"),
                         pl.BlockSpec((8, tc), lambda i, j: (0, j)), pl.BlockSpec((1, tc), lambda i, j: (0, j))],
               out_specs=pl.BlockSpec((tm, gw), lambda i, j: (i, j)), out_shape=SDS((T, F2 // 2), BF16), name=name)(
                   uu, uu, w8, b)


def _ffn_act_bwd(uu, da, w8, b, S, name, side=None):
    K, gw = FFN_CONV_K, GLU_W
    T, F2 = uu.shape
    tm, tc = _strip_row_tile(S), 2 * GLU_W
    tps = S // tm
    last16 = T // 16 - 1

    def body(u_ref, p_ref, n_ref, da_ref, dan_ref, w_ref, b_ref, duu_ref, cw_ref, dabuf):
        i = pl.program_id(1)

        @pl.when(i == 0)
        def _():
            cw_ref[...] = jnp.zeros_like(cw_ref)

        first = (i % tps) == 0
        last = (i % tps) == tps - 1
        dabuf[0:tm, :] = da_ref[...].astype(F32)
        dabuf[tm:tm + 8, :] = jnp.where(last, 0.0, dan_ref[...].astype(F32)[0:8, :])
        fs, ext = FFN_STRIP, FFN_STRIP + 8

        def strip(r0, win, carry):
            shifted = [win[8 - j:8 - j + ext] for j in range(K)]
            u = b_ref[...] + shifted[0] * w_ref[K - 1:K, :]
            for j in range(1, K):
                u = u + shifted[j] * w_ref[K - 1 - j:K - j, :]
            da_ = dabuf[pl.ds(r0, ext), :]
            g, v = u[:, :gw], u[:, gw:]
            du = jnp.concatenate([da_ * v * _dsilu(g), da_ * _silu(g)], axis=1)
            duu_ref[pl.ds(r0, FFN_STRIP), :] = _taps_t(du, w_ref, K, FFN_STRIP).astype(BF16)
            dmain = du[0:FFN_STRIP]
            sums = [_fold8(dmain * shifted[K - 1 - k][0:FFN_STRIP]) for k in range(K)] + [_fold8(dmain)]
            return tuple(c + s for c, s in zip(carry, sums))

        acc = strip(0, jnp.concatenate([jnp.where(first, 0.0, p_ref[...]), u_ref[0:ext, :]], axis=0),
                    tuple(jnp.zeros((8, tc), F32) for _ in range(K + 1)))

        def step(r, c):
            r0 = pl.multiple_of(r * fs, fs)
            return strip(r0, u_ref[pl.ds(pl.multiple_of(r0 - 8, 8), fs + 16), :], c)

        acc = lax.fori_loop(1, tm // fs - 1, step, acc)
        acc = strip(tm - fs, jnp.concatenate([u_ref[tm - ext:tm, :], n_ref[...]], axis=0), acc)
        cw_ref[...] += _rows8([jnp.sum(a, axis=0, keepdims=True) for a in acc])

    return _pc(body, grid=(F2 // tc, T // tm),
               in_specs=[pl.BlockSpec((tm, tc), lambda j, i: (i, j)), _prev_spec(tm, tc, 0, "ji"),
                         _next_spec(T, tm, tc, 0, "ji"), pl.BlockSpec((tm, gw), lambda j, i: (i, j)),
                         pl.BlockSpec((16, gw), lambda j, i: (jnp.minimum((i + 1) * (tm // 16), last16), j)),
                         pl.BlockSpec((8, tc), lambda j, i: (0, j)), pl.BlockSpec((1, tc), lambda j, i: (0, j))],
               out_specs=[pl.BlockSpec((tm, tc), lambda j, i: (i, j)), pl.BlockSpec((8, tc), lambda j, i: (0, j))],
               out_shape=[SDS((T, F2), BF16), SDS((8, F2), F32)], name=name,
               scratch=(pltpu.VMEM((tm + 8, gw), F32),), side=side)(
                   uu, uu, uu, da, da, w8, b)


def _ssd_common(dtc_raw, dtr_raw, hpc, hpr, L):
    dt_c = _softplus(dtc_raw + hpc[0:1, :])
    a_c = -jnp.exp(hpc[1:2, :])
    dt_r = _softplus(dtr_raw + hpr[:, 0:1])
    a_r = -jnp.exp(hpr[:, 1:2])
    li = lax.broadcasted_iota(jnp.int32, (L, L), 0)
    si = lax.broadcasted_iota(jnp.int32, (L, L), 1)
    low = li >= si
    upp = li <= si
    acs_c = _dotx(low, dt_c * a_c, split="b")
    acs_r = _dotx(dt_r * a_r, upp)
    return dt_c, a_c, acs_c, acs_r, low, upp


def _dotx(a, b, split="a", parts=3, dims=NN):
    val, one = (a, b) if split == "a" else (b, a)
    one = one.astype(BF16)
    acc, rem = None, val
    for i in range(parts):
        piece = rem.astype(BF16)
        t = _dot(piece, one, dims) if split == "a" else _dot(one, piece, dims)
        acc = t if acc is None else acc + t
        if i + 1 < parts:
            rem = rem - piece.astype(F32)
    return acc


def _head_maps(R, P, L):
    RP = R * P
    sel = (lax.broadcasted_iota(jnp.int32, (RP, R), 0) // P == lax.broadcasted_iota(jnp.int32, (RP, R), 1)).astype(F32)
    selt = (lax.broadcasted_iota(jnp.int32, (R, RP), 1) // P == lax.broadcasted_iota(jnp.int32, (R, RP), 0)).astype(F32)
    colb = (lax.broadcasted_iota(jnp.int32, (R, R * L), 1) // L == lax.broadcasted_iota(jnp.int32, (R, R * L), 0)).astype(F32)
    return sel, selt, colb


def _pair_diag(mats, rhs_b, R, P):
    lanes = 2 * P
    lo = lax.broadcasted_iota(jnp.int32, (mats[0].shape[0], lanes), 1) < P
    out = []
    for q in range(R // 2):
        rp = rhs_b[:, q * lanes:(q + 1) * lanes]
        out.append(jnp.where(lo, _dot(mats[2 * q], rp), _dot(mats[2 * q + 1], rp)))
    return jnp.concatenate(out, axis=1) if len(out) > 1 else out[0]


def _ssd_specs(pre, off_x, off_b, off_c, G, R, P, nb, nc, rev):
    L, N, RP = CHUNK, N_STATE, R * P
    cidx = (lambda c: nc - 1 - c) if rev else (lambda c: c)
    xb, bb, cb = off_x // RP, off_b // N, off_c // N
    assert off_x % RP == 0 and off_b % N == 0 and off_c % N == 0
    row = lambda b, c: b * nc + cidx(c)
    return dict(
        x=pl.BlockSpec((L, RP), lambda g, b, c: (row(b, c), xb + g)),
        b=pl.BlockSpec((L, N), lambda g, b, c: (row(b, c), bb + g)),
        c=pl.BlockSpec((L, N), lambda g, b, c: (row(b, c), cb + g)),
        dtc=pl.BlockSpec((None, L, R), lambda g, b, c: (g, row(b, c), 0)),
        dtr=pl.BlockSpec((None, R, L), lambda g, b, c: (g, 0, row(b, c))),
        hpc=pl.BlockSpec((None, 8, R), lambda g, b, c: (g, 0, 0)),
        hpr=pl.BlockSpec((None, R, 8), lambda g, b, c: (g, 0, 0)),
        y=pl.BlockSpec((L, RP), lambda g, b, c: (row(b, c), g)),
        bc=pl.BlockSpec((L, N), lambda g, b, c: (row(b, c), g)),
        hs=pl.BlockSpec((None, None, N, RP), lambda g, b, c: (row(b, c), g, 0, 0)),
    )


def _ssd_fwd(pre, offs, dtc, dtr, hpc, hpr, G, R, P, S, name, side=None):
    T = pre.shape[0]
    L, N, RP = CHUNK, N_STATE, R * P
    nc, nb = S // L, T // S
    sp = _ssd_specs(pre, *offs, G, R, P, nb, nc, False)

    def body(px_ref, pb_ref, pc_ref, dtc_ref, dtr_ref, hpc_ref, hpr_ref, y_ref, hs_ref, hst):
        @pl.when(pl.program_id(2) == 0)
        def _():
            hst[...] = jnp.zeros_like(hst)

        xs, bm, cm = _silu(px_ref[...]), _silu(pb_ref[...]), _silu(pc_ref[...])
        hpc_ = hpc_ref[...]
        dt_c, _, acs_c, acs_r, low, _ = _ssd_common(dtc_ref[...], dtr_ref[...], hpc_, hpr_ref[...], L)
        _, selt, colb = _head_maps(R, P, L)
        dt_e, a_e, hp_e = _dotx(dt_c, selt), _dotx(acs_c, selt), _dotx(hpc_, selt)
        a_bc = _dotx(acs_c, colb)
        a_last = a_e[L - 1:L, :]
        bb, cb = bm.astype(BF16), cm.astype(BF16)
        gm = _dot(cb, bb, NT)
        hprev = hst[...]
        hprev_b = hprev.astype(BF16)
        hs_ref[...] = hprev_b
        xdt = xs * dt_e
        xdt_b = xdt.astype(BF16)
        ms = []
        for r in range(R):
            dec = jnp.exp(jnp.where(low, a_bc[:, r * L:(r + 1) * L] - acs_r[r:r + 1, :], -jnp.inf))
            ms.append((gm * dec).astype(BF16))
        y = _pair_diag(ms, xdt_b, R, P) + _dot(cb, hprev_b) * jnp.exp(a_e) + hp_e[2:3, :] * xs
        y_ref[...] = y
        xw = (xdt * jnp.exp(a_last - a_e)).astype(BF16)
        hst[...] = hprev * jnp.exp(a_last) + _dot(bb, xw, TN)

    return _pc(body, grid=(G, nb, nc),
               in_specs=[sp["x"], sp["b"], sp["c"], sp["dtc"], sp["dtr"], sp["hpc"], sp["hpr"]],
               out_specs=[sp["y"], sp["hs"]],
               out_shape=[SDS((T, G * RP), F32), SDS((nb * nc, G, N, RP), BF16)], name=name,
               scratch=(pltpu.VMEM((N, RP), F32),), side=side)(pre, pre, pre, dtc, dtr, hpc, hpr)


def _ssd_bwd(pre, offs, dtc, dtr, hpc, hpr, hs, dy, G, R, P, S, name, side=None):
    T = pre.shape[0]
    L, N, RP = CHUNK, N_STATE, R * P
    nc, nb = S // L, T // S
    sp = _ssd_specs(pre, *offs, G, R, P, nb, nc, True)

    def body(px_ref, pb_ref, pc_ref, dtc_ref, dtr_ref, hpc_ref, hpr_ref, hs_ref, dy_ref,
             dpx_ref, dpb_ref, dpc_ref, ddt_ref, hpg_ref, dhst):
        bi, ci = pl.program_id(1), pl.program_id(2)

        @pl.when(ci == 0)
        def _():
            dhst[...] = jnp.zeros_like(dhst)

        @pl.when((bi == 0) & (ci == 0))
        def _():
            hpg_ref[...] = jnp.zeros_like(hpg_ref)

        px, pb, pcc = px_ref[...], pb_ref[...], pc_ref[...]
        xs, bm, cm = _silu(px), _silu(pb), _silu(pcc)
        hpc_ = hpc_ref[...]
        dtc_raw = dtc_ref[...]
        dt_c, a_c, acs_c, acs_r, low, upp = _ssd_common(dtc_raw, dtr_ref[...], hpc_, hpr_ref[...], L)
        sel, selt, colb = _head_maps(R, P, L)
        dt_e, a_e, hp_e = _dotx(dt_c, selt), _dotx(acs_c, selt), _dotx(hpc_, selt)
        a_bc = _dotx(acs_c, colb)
        a_last = a_e[L - 1:L, :]
        e_e, w_e = jnp.exp(a_e), jnp.exp(a_last - a_e)
        bb, cb = bm.astype(BF16), cm.astype(BF16)
        gm = _dot(cb, bb, NT)
        gmt = _dot(bb, cb, NT)
        hprev = hs_ref[...]
        dhn = dhst[...]
        dhn_b = dhn.astype(BF16)
        dy = dy_ref[...]
        dy_b = dy.astype(BF16)
        xdt = xs * dt_e
        xdt_b = xdt.astype(BF16)
        yoff = _dot(cb, hprev) * e_e
        dye_b = (dy * e_e).astype(BF16)
        dcm = _dot(dye_b, hprev, NT)
        dhst[...] = _dot(cb, dye_b, TN) + jnp.exp(a_last) * dhn
        dxdt_st = _dot(bb, dhn_b) * w_e
        dbm = _dot((xdt * w_e).astype(BF16), dhn_b, NT)
        lanes = 2 * P
        lo = lax.broadcasted_iota(jnp.int32, (L, lanes), 1) < P
        dg = jnp.zeros((L, L), F32)
        es, css = [], []
        for r in range(R):
            col_b, row = a_bc[:, r * L:(r + 1) * L], acs_r[r:r + 1, :]
            dec = jnp.exp(jnp.where(low, col_b - row, -jnp.inf))
            q = r // 2
            dyp = dy_b[:, q * lanes:(q + 1) * lanes]
            dyp = jnp.where(lo if r % 2 == 0 else ~lo, dyp, jnp.zeros_like(dyp))
            dm = _dot(dyp, xdt_b[:, q * lanes:(q + 1) * lanes], NT)
            dg = dg + dm * dec
            e = dm * (gm * dec)
            es.append(e)
            css.append(jnp.sum(e, axis=0, keepdims=True))
        dgb = dg.astype(BF16)
        dcm = dcm + _dot(dgb, bb)
        dbm = dbm + _dot(dgb, cb, TN)
        colbt = (lax.broadcasted_iota(jnp.int32, (R * L, R), 0) // L
                 == lax.broadcasted_iota(jnp.int32, (R * L, R), 1)).astype(F32)
        eye = (lax.broadcasted_iota(jnp.int32, (R, R), 0) == lax.broadcasted_iota(jnp.int32, (R, R), 1)).astype(F32)
        row_sums = _dotx(jnp.concatenate(es, axis=1), colbt)
        col_sums = _dotx(jnp.concatenate(css, axis=0), eye, dims=TN)
        mts = []
        for r in range(R):
            dect = jnp.exp(jnp.where(upp, acs_r[r:r + 1, :] - a_bc[:, r * L:(r + 1) * L], -jnp.inf))
            mts.append((gmt * dect).astype(BF16))
        dxdt = _pair_diag(mts, dy_b, R, P) + dxdt_st
        q_st = _dotx(xdt * dxdt_st, sel, parts=1)
        da = row_sums - col_sums + _dotx(dy * yoff, sel, parts=1) - q_st
        hh = jnp.sum(_dotx(dhn * hprev.astype(F32), sel, parts=1), axis=0, keepdims=True)
        da_last = jnp.exp(acs_c[L - 1:L, :]) * hh + jnp.sum(q_st, axis=0, keepdims=True)
        rowi = lax.broadcasted_iota(jnp.int32, (L, R), 0)
        da = da + jnp.where(rowi == L - 1, da_last, 0.0)
        dpx_ref[...] = (dxdt * dt_e + hp_e[2:3, :] * dy) * _dsilu(px)
        dpb_ref[...] = dbm * _dsilu(pb)
        dpc_ref[...] = dcm * _dsilu(pcc)
        dadt = _dotx(upp, da, split="b")
        ddt = _dotx(dxdt * xs, sel, parts=1) + dadt * a_c
        ddt_raw = ddt * jax.nn.sigmoid(dtc_raw + hpc_[0:1, :])
        ddt_ref[...] = ddt_raw
        d_a = jnp.sum(dadt * dt_c, axis=0, keepdims=True)
        d_d = jnp.sum(_dotx(dy * xs, sel, parts=1), axis=0, keepdims=True)
        rows = [jnp.sum(ddt_raw, axis=0, keepdims=True), d_a * a_c, d_d, jnp.zeros((5, R), F32)]
        hpg_ref[...] += jnp.concatenate(rows, axis=0)

    return _pc(body, grid=(G, nb, nc),
               in_specs=[sp["x"], sp["b"], sp["c"], sp["dtc"], sp["dtr"], sp["hpc"], sp["hpr"], sp["hs"], sp["y"]],
               out_specs=[sp["y"], sp["bc"], sp["bc"], sp["dtc"], pl.BlockSpec((None, 8, R), lambda g, b, c: (g, 0, 0))],
               out_shape=[SDS((T, G * RP), F32), SDS((T, G * N), F32), SDS((T, G * N), F32), SDS((G, T, R), F32),
                          SDS((G, 8, R), F32)], name=name,
               scratch=(pltpu.VMEM((N, RP), F32),), side=side)(pre, pre, pre, dtc, dtr, hpc, hpr, hs, dy)


def _gate_norm_fwd(y, zview, ng, G, S, name):
    T, DI = y.shape
    zarr, zoff = zview
    gw = DI // G
    tm = _tile(S, 1024, 8)
    zb = zoff // gw
    assert zoff % gw == 0

    def body(y_ref, z_ref, g_ref, o_ref):
        yg = y_ref[...] * _silu(z_ref[...])
        r = lax.rsqrt(jnp.mean(yg * yg, axis=-1, keepdims=True) + EPS)
        o_ref[...] = (yg * r * g_ref[...]).astype(BF16)

    return _pc(body, grid=(T // tm, G),
               in_specs=[pl.BlockSpec((tm, gw), lambda i, g: (i, g)), pl.BlockSpec((tm, gw), lambda i, g: (i, zb + g)),
                         pl.BlockSpec((1, gw), lambda i, g: (0, g))],
               out_specs=pl.BlockSpec((tm, gw), lambda i, g: (i, g)), out_shape=SDS((T, DI), BF16), name=name)(y, zarr, ng)


def _gate_norm_bwd(y, zview, ng, dyn, G, S, name):
    T, DI = y.shape
    zarr, zoff = zview
    gw = DI // G
    tm = _tile(S, 1024, 8)
    zb = zoff // gw

    def body(y_ref, z_ref, g_ref, d_ref, dy_ref, dz_ref, dg_ref):
        @pl.when(pl.program_id(1) == 0)
        def _():
            dg_ref[...] = jnp.zeros_like(dg_ref)

        y_, z, d = y_ref[...], z_ref[...], d_ref[...]
        sz = _silu(z)
        yg = y_ * sz
        r = lax.rsqrt(jnp.mean(yg * yg, axis=-1, keepdims=True) + EPS)
        n = yg * r
        dn = d * g_ref[...]
        dyg = r * (dn - n * jnp.mean(dn * n, axis=-1, keepdims=True))
        dy_ref[...] = dyg * sz
        dz_ref[...] = (dyg * y_ * _dsilu(z)).astype(BF16)
        dg_ref[...] += _bsum(jnp.sum(d * n, axis=0, keepdims=True))

    return _pc(body, grid=(G, T // tm),
               in_specs=[pl.BlockSpec((tm, gw), lambda g, i: (i, g)), pl.BlockSpec((tm, gw), lambda g, i: (i, zb + g)),
                         pl.BlockSpec((1, gw), lambda g, i: (0, g)), pl.BlockSpec((tm, gw), lambda g, i: (i, g))],
               out_specs=[pl.BlockSpec((tm, gw), lambda g, i: (i, g)), pl.BlockSpec((tm, gw), lambda g, i: (i, g)),
                          pl.BlockSpec((8, gw), lambda g, i: (0, g))],
               out_shape=[SDS((T, DI), F32), SDS((T, DI), BF16), SDS((8, DI), F32)], name=name)(y, zarr, ng, dyn)


def _shortconv_fwd(proj, off_b, off_c, off_h, C, w8, S, name):
    K = SC_CONV_K
    _, ob, T, tm, tc, tps = _conv_geom((proj, off_b), C, S, cap=1024)
    oc, oh = off_c // tc, off_h // tc

    def body(b_ref, c_ref, h_ref, cp_ref, hp_ref, w_ref, o_ref, buf):
        first = (pl.program_id(0) % tps) == 0
        buf[0:8, :] = jnp.where(first, 0.0, cp_ref[...] * hp_ref[...])
        buf[8:, :] = c_ref[...] * h_ref[...]

        def strip(r0, carry):
            conv = _taps(buf[pl.ds(r0, STRIP + 8), :], w_ref, K, 8, STRIP)
            o_ref[pl.ds(r0, STRIP), :] = (b_ref[pl.ds(r0, STRIP), :] * conv).astype(BF16)
            return carry

        _strips(tm, strip)

    blk = lambda o: pl.BlockSpec((tm, tc), lambda i, j: (i, o + j))
    return _pc(body, grid=(T // tm, C // tc),
               in_specs=[blk(ob), blk(oc), blk(oh), _prev_spec(tm, tc, oc, "ij"), _prev_spec(tm, tc, oh, "ij"),
                         pl.BlockSpec((8, tc), lambda i, j: (0, j))],
               out_specs=pl.BlockSpec((tm, tc), lambda i, j: (i, j)), out_shape=SDS((T, C), BF16), name=name,
               scratch=(pltpu.VMEM((tm + 8, tc), F32),))(proj, proj, proj, proj, proj, w8)


def _shortconv_bwd(proj, off_b, off_c, off_h, C, w8, dsc, S, name):
    K = SC_CONV_K
    _, ob, T, tm, tc, tps = _conv_geom((proj, off_b), C, S, cap=1024)
    oc, oh = off_c // tc, off_h // tc

    def body(b_ref, c_ref, h_ref, cp_ref, hp_ref, bn_ref, d_ref, dn_ref, w_ref,
             db_ref, dc_ref, dh_ref, dw_ref, buf, buf2):
        i = pl.program_id(1)

        @pl.when(i == 0)
        def _():
            dw_ref[...] = jnp.zeros_like(dw_ref)

        first = (i % tps) == 0
        last = (i % tps) == tps - 1
        buf[0:8, :] = jnp.where(first, 0.0, cp_ref[...] * hp_ref[...])
        buf[8:, :] = c_ref[...] * h_ref[...]
        buf2[0:tm, :] = d_ref[...] * b_ref[...]
        buf2[tm:tm + 8, :] = jnp.where(last, 0.0, dn_ref[...] * bn_ref[...])

        def strip(r0, carry):
            rows = pl.ds(r0, STRIP)
            vwin = buf[pl.ds(r0, STRIP + 8), :]
            vs = [vwin[8 - j:8 - j + STRIP] for j in range(K)]
            conv = vs[0] * w_ref[K - 1:K, :]
            for j in range(1, K):
                conv = conv + vs[j] * w_ref[K - 1 - j:K - j, :]
            db_ref[rows, :] = (d_ref[rows, :] * conv).astype(BF16)
            dwin = buf2[pl.ds(r0, STRIP + 8), :]
            dv = _taps_t(dwin, w_ref, K, STRIP)
            dc_ref[rows, :] = (dv * h_ref[rows, :]).astype(BF16)
            dh_ref[rows, :] = (dv * c_ref[rows, :]).astype(BF16)
            dconv = dwin[0:STRIP]
            sums = [_fold8(dconv * vs[K - 1 - k]) for k in range(K)]
            return tuple(c + s for c, s in zip(carry, sums))

        acc = _strips(tm, strip, tuple(jnp.zeros((8, tc), F32) for _ in range(K)))
        dw_ref[...] += _rows8([jnp.sum(a, axis=0, keepdims=True) for a in acc])

    blk = lambda o: pl.BlockSpec((tm, tc), lambda j, i: (i, o + j))
    out = pl.BlockSpec((tm, tc), lambda j, i: (i, j))
    return _pc(body, grid=(C // tc, T // tm),
               in_specs=[blk(ob), blk(oc), blk(oh), _prev_spec(tm, tc, oc, "ji"), _prev_spec(tm, tc, oh, "ji"),
                         _next_spec(T, tm, tc, ob, "ji"), blk(0), _next_spec(T, tm, tc, 0, "ji"),
                         pl.BlockSpec((8, tc), lambda j, i: (0, j))],
               out_specs=[out, out, out, pl.BlockSpec((8, tc), lambda j, i: (0, j))],
               out_shape=[SDS((T, C), BF16)] * 3 + [SDS((8, C), F32)], name=name,
               scratch=(pltpu.VMEM((tm + 8, tc), F32), pltpu.VMEM((tm + 8, tc), F32)))(
                   proj, proj, proj, proj, proj, proj, dsc, dsc, w8)


def _merge_fwd(proj, off_g1, off_g2, y1, y2, S, name):
    T, D = y1.shape
    tm = _row_tile(S)
    o1, o2 = off_g1 // D, off_g2 // D
    assert off_g1 % D == 0 and off_g2 % D == 0

    def body(g1_ref, g2_ref, y1_ref, y2_ref, o_ref):
        def strip(r0, carry):
            rows = pl.ds(r0, STRIP)
            o_ref[rows, :] = (jax.nn.sigmoid(g1_ref[rows, :]) * y1_ref[rows, :]
                              + jax.nn.sigmoid(g2_ref[rows, :]) * y2_ref[rows, :]).astype(BF16)
            return carry

        _strips(tm, strip)

    row = pl.BlockSpec((tm, D), lambda i: (i, 0))
    return _pc(body, grid=(T // tm,),
               in_specs=[pl.BlockSpec((tm, D), lambda i: (i, o1)), pl.BlockSpec((tm, D), lambda i: (i, o2)), row, row],
               out_specs=row, out_shape=SDS((T, D), BF16), name=name)(proj, proj, y1, y2)


def _merge_bwd(proj, off_g1, off_g2, y1, y2, dm, S, name):
    T, D = y1.shape
    tm = _row_tile(S)
    o1, o2 = off_g1 // D, off_g2 // D

    def body(g1_ref, g2_ref, y1_ref, y2_ref, d_ref, dy1_ref, dy2_ref, dg1_ref, dg2_ref):
        def strip(r0, carry):
            rows = pl.ds(r0, STRIP)
            d = d_ref[rows, :]
            s1, s2 = jax.nn.sigmoid(g1_ref[rows, :]), jax.nn.sigmoid(g2_ref[rows, :])
            dy1_ref[rows, :] = (d * s1).astype(BF16)
            dy2_ref[rows, :] = (d * s2).astype(BF16)
            dg1_ref[rows, :] = (d * y1_ref[rows, :] * s1 * (1.0 - s1)).astype(BF16)
            dg2_ref[rows, :] = (d * y2_ref[rows, :] * s2 * (1.0 - s2)).astype(BF16)
            return carry

        _strips(tm, strip)

    row = pl.BlockSpec((tm, D), lambda i: (i, 0))
    return _pc(body, grid=(T // tm,),
               in_specs=[pl.BlockSpec((tm, D), lambda i: (i, o1)), pl.BlockSpec((tm, D), lambda i: (i, o2)), row, row, row],
               out_specs=[row] * 4, out_shape=[SDS((T, D), BF16)] * 4, name=name)(proj, proj, y1, y2, dm)


def _pad8(w):
    return jnp.pad(w, ((0, 8 - w.shape[0]), (0, 0)))


def _dims(w):
    D = w["mix_pre_g"].shape[-1]
    DI = w["ssd_norm_g"].shape[-1]
    H = w["ssd_dt_bias"].shape[-1]
    conv_dim = w["ssd_conv_b"].shape[-1]
    G = (conv_dim - DI) // (2 * N_STATE)
    F = w["w_down"].shape[0]
    return dict(D=D, DI=DI, H=H, P=DI // H, G=G, R=H // G, GN=G * N_STATE, CD=conv_dim, F=F)


def _proj_layout(d):
    D, DI, CD, H = d["D"], d["DI"], d["CD"], d["H"]
    o = dict(z=0, xbc=DI, scb=DI + CD, scc=DI + CD + D, sch=DI + CD + 2 * D, g1=DI + CD + 3 * D, g2=DI + CD + 4 * D,
             dt=DI + CD + 5 * D)
    o["sb"] = math.gcd(SEG_BLK, D, DI, d["GN"])
    assert o["sb"] % 128 == 0 and H <= o["sb"]
    o["np"] = o["dt"] + o["sb"]
    return o


def _glu_perm(a, F, inverse=False):
    lead = a.shape[:-1]
    nb = F // GLU_W
    if not inverse:
        return a.reshape(*lead, 2, nb, GLU_W).swapaxes(-3, -2).reshape(*lead, 2 * F)
    return a.reshape(*lead, nb, 2, GLU_W).swapaxes(-3, -2).reshape(*lead, 2 * F)


def _glu_perm_rows(a, F, inverse=False):
    nb, D = F // GLU_W, a.shape[1]
    shape = (nb, 2, GLU_W, D) if inverse else (2, nb, GLU_W, D)
    return a.reshape(shape).swapaxes(0, 1).reshape(2 * F, D)


def _prep_layer(w):
    d = _dims(w)
    D, DI, CD, H, G, R, F = d["D"], d["DI"], d["CD"], d["H"], d["G"], d["R"], d["F"]
    lay = _proj_layout(d)
    w_in = w["w_in"]
    used = lay["dt"] + H
    wcat = jnp.concatenate([w_in[:DI + CD], w_in[DI + CD + H:], w_in[DI + CD:DI + CD + H],
                            jnp.zeros((lay["np"] - used, D), w_in.dtype)], axis=0)
    hp = jnp.stack([w["ssd_dt_bias"], w["ssd_a_log"], w["ssd_d"]], 0).astype(F32)
    hpc = jnp.pad(hp.reshape(3, G, R).transpose(1, 0, 2), ((0, 0), (0, 5), (0, 0)))
    hpr = jnp.pad(hp[:2].reshape(2, G, R).transpose(1, 2, 0), ((0, 0), (0, 0), (0, 6)))
    row = lambda v: v.reshape(1, -1).astype(F32)
    return dict(
        d=d, lay=lay, ada_w=w["ada_w"].astype(BF16), ada_b=row(w["ada_b"]),
        mix_pre_g=row(w["mix_pre_g"]), mix_post_g=row(w["mix_post_g"]), wcat=wcat.astype(BF16),
        ssd_conv_w=_pad8(w["ssd_conv_w"].astype(F32)), ssd_conv_b=row(w["ssd_conv_b"]), hpc=hpc, hpr=hpr,
        ssd_norm_g=row(w["ssd_norm_g"]), w_ssd_out=w["w_ssd_out"].astype(BF16),
        sc_conv_w=_pad8(w["sc_conv_w"].astype(F32)), w_sc_out=w["w_sc_out"].astype(BF16), w_o=w["w_o"].astype(BF16),
        ffn_pre_g=row(w["ffn_pre_g"]), ffn_post_g=row(w["ffn_post_g"]),
        w_up=_glu_perm_rows(w["w_up"], F).astype(BF16), ffn_conv_w=_pad8(_glu_perm(w["ffn_conv_w"].astype(F32), F)),
        ffn_conv_b=_glu_perm(row(w["ffn_conv_b"]), F), w_down=w["w_down"].astype(BF16))


def _dt_layouts(proj, lay, d):
    T = proj.shape[0]
    dt = proj[:, lay["dt"]:lay["dt"] + d["H"]].reshape(T, d["G"], d["R"])
    return dt.transpose(1, 0, 2), dt.transpose(1, 2, 0)


def _layer_fwd(x, c8, p, S, li, gather=None):
    d, lay = p["d"], p["lay"]
    D, DI, G, R, P, GN, CD = d["D"], d["DI"], d["G"], d["R"], d["P"], d["GN"], d["CD"]
    nb = x.shape[0] // S
    nm = lambda s: f"l{li}_{s}"
    mod, cact = _modk(c8, p["ada_w"], p["ada_b"], nm("mod"))
    mod3 = mod[:nb].reshape(nb, 1, 6 * D)
    h = _norm_mod(x, p["mix_pre_g"], mod3, 1, 0, S, nm("norm1"))
    proj = _mm(h, p["wcat"], "nt", F32, nm("mm_in"), caps=(1024, 1536, 2048))
    pre = _conv_fwd((proj, lay["xbc"]), CD, p["ssd_conv_w"], p["ssd_conv_b"], SSD_CONV_K, S, nm("ssdconv"))
    dtc, dtr = _dt_layouts(proj, lay, d)
    offs = (0, DI, DI + GN)
    gathered = None
    if gather is None:
        y, hs = _ssd_fwd(pre, offs, dtc, dtr, p["hpc"], p["hpr"], G, R, P, S, nm("ssd"))
    else:
        (y, hs), (gathered,) = _ssd_fwd(pre, offs, dtc, dtr, p["hpc"], p["hpr"], G, R, P, S, nm("ssd"),
                                        side=_side_gather_direct(gather))
    yn = _gate_norm_fwd(y, (proj, lay["z"]), p["ssd_norm_g"], G, S, nm("gnorm"))
    sc = _shortconv_fwd(proj, lay["scb"], lay["scc"], lay["sch"], D, p["sc_conv_w"], S, nm("sconv"))
    if gather is None:
        y_ssd = _mm(yn, p["w_ssd_out"], "nn", F32, nm("mm_ssdout"))
    else:
        y_ssd, (gathered,) = _mm(yn, p["w_ssd_out"], "nn", F32, nm("mm_ssdout"), side=_side_gather_forward(gathered))
    y_sc = _mm(sc, p["w_sc_out"], "nn", F32, nm("mm_scout"))
    m = _merge_fwd(proj, lay["g1"], lay["g2"], y_ssd, y_sc, S, nm("merge"))
    mix = _mm(m, p["w_o"], "nn", F32, nm("mm_o"))
    x1 = _resid_post(x, mix, mod3, 2, p["mix_post_g"], S, nm("post1"))
    h2 = _norm_mod(x1, p["ffn_pre_g"], mod3, 4, 3, S, nm("norm2"))
    uu = _mm(h2, p["w_up"], "nt", F32, nm("mm_up"), caps=(1024, 1408, 2048))
    a = _ffn_act_fwd(uu, p["ffn_conv_w"], p["ffn_conv_b"], S, nm("ffnact"))
    f = _mm(a, p["w_down"], "nn", F32, nm("mm_down"), caps=(1024, 1024, 1408))
    x2 = _resid_post(x1, f, mod3, 5, p["ffn_post_g"], S, nm("post2"))
    saved = dict(x=x, h=h, proj=proj, pre=pre, dtc=dtc, dtr=dtr, y=y, hs=hs, yn=yn, sc=sc, y_ssd=y_ssd, y_sc=y_sc,
                 m=m, mix=mix, x1=x1, h2=h2, uu=uu, a=a, f=f, mod3=mod3, cact=cact)
    return x2, saved, gathered


def _seq_sum(acc, nb):
    return acc.reshape(nb, 8, -1)[:, 0, :]


def _layer_bwd(dx2, p, s, S, li, chip_sums=None, early=None):
    d, lay = p["d"], p["lay"]
    D, DI, G, R, P, GN, CD, H, F = d["D"], d["DI"], d["G"], d["R"], d["P"], d["GN"], d["CD"], d["H"], d["F"]
    nb = dx2.shape[0] // S
    nm = lambda t: f"l{li}_{t}"
    mod3 = s["mod3"]
    g = {}
    exchanged = None
    df, dgt2, dpg2 = _post_bwd(s["f"], mod3, 5, p["ffn_post_g"], dx2, S, nm("post2_b"))
    g["ffn_post_g"] = dpg2[0]
    da = _mm(df, p["w_down"], "nt", BF16, nm("mm_down_bi"), caps=(1024, 1408, 2048))
    g["w_down"] = _mm(s["a"], df, "tn", WGRAD,nm("mm_down_bw"), caps=(1408, 1024, 1024))
    if chip_sums is None:
        duu, cw = _ffn_act_bwd(s["uu"], da, p["ffn_conv_w"], p["ffn_conv_b"], S, nm("ffnact_b"))
    else:
        (duu, cw), (exchanged,) = _ffn_act_bwd(s["uu"], da, p["ffn_conv_w"], p["ffn_conv_b"], S, nm("ffnact_b"),
                                               side=_side_chip_exchange(chip_sums))
    g["ffn_conv_w"] = _glu_perm(cw[:FFN_CONV_K], F, inverse=True)
    g["ffn_conv_b"] = _glu_perm(cw[FFN_CONV_K], F, inverse=True)
    dh2 = _mm(duu, p["w_up"], "nn", F32, nm("mm_up_bi"), caps=(1024, 1024, 2816))
    g["w_up"] = _glu_perm_rows(_mm(duu, s["h2"], "tn", WGRAD,nm("mm_up_bw"), caps=(1408, 1024, 1024)), F, inverse=True)
    dx1, dg2, dsc2, dsh2 = _pre_bwd(s["x1"], p["ffn_pre_g"], mod3, 4, dh2, dx2, S, nm("norm2_b"))
    g["ffn_pre_g"] = dg2[0]
    dmix, dgt1, dpg1 = _post_bwd(s["mix"], mod3, 2, p["mix_post_g"], dx1, S, nm("post1_b"))
    g["mix_post_g"] = dpg1[0]
    dm = _mm(dmix, p["w_o"], "nt", F32, nm("mm_o_bi"))
    g["w_o"] = _mm(s["m"], dmix, "tn", WGRAD,nm("mm_o_bw"))
    proj = s["proj"]
    dy_ssd, dy_sc, dg1, dg2_ = _merge_bwd(proj, lay["g1"], lay["g2"], s["y_ssd"], s["y_sc"], dm, S, nm("merge_b"))
    dyn = _mm(dy_ssd, p["w_ssd_out"], "nt", F32, nm("mm_ssdout_bi"))
    g["w_ssd_out"] = _mm(s["yn"], dy_ssd, "tn", WGRAD,nm("mm_ssdout_bw"))
    dsc = _mm(dy_sc, p["w_sc_out"], "nt", F32, nm("mm_scout_bi"))
    g["w_sc_out"] = _mm(s["sc"], dy_sc, "tn", WGRAD,nm("mm_scout_bw"))
    dscb, dscc, dsch, scw = _shortconv_bwd(proj, lay["scb"], lay["scc"], lay["sch"], D, p["sc_conv_w"], dsc, S, nm("sconv_b"))
    g["sc_conv_w"] = scw[:SC_CONV_K]
    dy, dz, dng = _gate_norm_bwd(s["y"], (proj, lay["z"]), p["ssd_norm_g"], dyn, G, S, nm("gnorm_b"))
    g["ssd_norm_g"] = dng[0]
    offs = (0, DI, DI + GN)
    early_side = None if early is None else _side_chip_exchange(early(g))
    res = _ssd_bwd(s["pre"], offs, s["dtc"], s["dtr"], p["hpc"], p["hpr"], s["hs"], dy, G, R, P, S, nm("ssd_b"),
                   side=early_side)
    (dpx, dpb, dpc, ddt, hpg), early_got = res if early is not None else (res, None)
    g["ssd_dt_bias"], g["ssd_a_log"], g["ssd_d"] = hpg[:, 0, :].reshape(H), hpg[:, 1, :].reshape(H), hpg[:, 2, :].reshape(H)
    cws, dxbc = [], []
    for name, darr, off, C in (("x", dpx, 0, DI), ("b", dpb, DI, GN), ("c", dpc, DI + GN, GN)):
        w8 = p["ssd_conv_w"][:, off:off + C]
        cws.append(_conv_bwd_w((darr, 0), (proj, lay["xbc"] + off), C, SSD_CONV_K, S, nm(f"ssdconv_bw_{name}")))
        dxbc.append(_conv_bwd_in((darr, 0), C, w8, SSD_CONV_K, S, BF16, nm(f"ssdconv_bi_{name}")))
    cws = jnp.concatenate(cws, axis=1)
    g["ssd_conv_w"], g["ssd_conv_b"] = cws[:SSD_CONV_K], cws[SSD_CONV_K]
    T = dx2.shape[0]
    ddt_t = jnp.pad(ddt.transpose(1, 0, 2).reshape(T, H).astype(BF16), ((0, 0), (0, lay["sb"] - H)))
    dproj = [dz] + dxbc + [dscb, dscc, dsch, dg1, dg2_, ddt_t]
    dh = _mm_seg(dproj, p["wcat"], "nn", F32, nm("mm_in_bi"), lay["sb"])
    dwcat = _mm_seg(dproj, s["h"], "tn", WGRAD, nm("mm_in_bw"), lay["sb"], tk=1024)
    o = lay
    g["w_in"] = jnp.concatenate([dwcat[o["z"]:o["scb"]], dwcat[o["dt"]:o["dt"] + H], dwcat[o["scb"]:o["dt"]]], axis=0)
    dx, dg1_, dsc1, dsh1 = _pre_bwd(s["x"], p["mix_pre_g"], mod3, 1, dh, dx1, S, nm("norm1_b"))
    g["mix_pre_g"] = dg1_[0]
    dmod = jnp.concatenate([_seq_sum(t, nb) for t in (dsh1, dsc1, dgt1, dsh2, dsc2, dgt2)], axis=1)
    dmod8 = jnp.pad(dmod, ((0, MOD_ROWS - nb), (0, 0)))
    g["ada_b"] = _colsum(dmod8, nm("adab"))
    g["ada_w"] = _mm(dmod8, s["cact"], "tn", WGRAD, nm("mm_ada_bw"), caps=(1536, 1024, 2048))
    return dx, g, exchanged, None if early_got is None else early_got[0]


def _colsum(a8, name):
    rows, C = a8.shape
    tc = _tile(C, 2048)

    def body(a_ref, o_ref):
        o_ref[...] = _bsum(jnp.sum(a_ref[...], axis=0, keepdims=True))

    return _pc(body, grid=(C // tc,), in_specs=[pl.BlockSpec((rows, tc), lambda j: (0, j))],
               out_specs=pl.BlockSpec((8, tc), lambda j: (0, j)), out_shape=SDS((8, C), F32), name=name)(a8)[0]


def _adam(gs, w, m, v, name):
    ns, R, W = gs.shape
    tr = _tile(R, 256, 8)

    def body(g_ref, w_ref, m_ref, v_ref, go_ref, d_ref, mo_ref, vo_ref):
        g = g_ref[0].astype(F32)
        for k in range(1, ns):
            g = g + g_ref[k].astype(F32)
        go_ref[...] = g
        d_ref[...], mo_ref[...], vo_ref[...] = _adam_update(g, w_ref[...], m_ref[...], v_ref[...])

    row = pl.BlockSpec((tr, W), lambda i: (i, 0))
    return _pc(body, grid=(R // tr,), in_specs=[pl.BlockSpec((ns, tr, W), lambda i: (0, i, 0)), row, row, row],
               out_specs=[row] * 4, out_shape=[SDS((R, W), F32)] * 4, name=name)(gs, w, m, v)


def _adam_update(g, w, m, v):
    c1 = 1.0 / (1.0 - ADAM_B1 ** ADAM_STEP)
    c2 = 1.0 / (1.0 - ADAM_B2 ** ADAM_STEP)
    m_ = ADAM_B1 * m + (1.0 - ADAM_B1) * g
    v_ = ADAM_B2 * v + (1.0 - ADAM_B2) * (g * g)
    return -ADAM_LR * ((m_ * c1) / (jnp.sqrt(v_ * c2) + ADAM_EPS) + ADAM_WD * w), m_, v_


def _adam_nat(g, w, m, v, name):
    depth, a, b = w.shape
    tr = _tile(a, 256, 8)

    def body(g_ref, w_ref, m_ref, v_ref, d_ref, mo_ref, vo_ref):
        d_ref[...], mo_ref[...], vo_ref[...] = _adam_update(g_ref[...], w_ref[...], m_ref[...], v_ref[...])

    blk = pl.BlockSpec((None, tr, b), lambda l, i: (l, i, 0))
    return _pc(body, grid=(depth, a // tr), in_specs=[blk] * 4, out_specs=[blk] * 3,
               out_shape=[SDS(w.shape, F32)] * 3, name=name)(g, w, m, v)


def _adam_mid(g, w, m, v, name):
    b, depth, a = w.shape
    tr = 128

    def body(g_ref, w_ref, m_ref, v_ref, d_ref, mo_ref, vo_ref):
        d_ref[...], mo_ref[...], vo_ref[...] = _adam_update(g_ref[...], w_ref[...], m_ref[...], v_ref[...])

    blk = pl.BlockSpec((tr, depth, a), lambda i: (i, 0, 0))
    return _pc(body, grid=(pl.cdiv(b, tr),), in_specs=[blk] * 4, out_specs=[blk] * 3,
               out_shape=[SDS(w.shape, F32)] * 3, name=name)(g, w, m, v)


def _sum_chips(gs, name):
    ns, R, W = gs.shape
    tr = _tile(R, 256, 16)

    def body(g_ref, o_ref):
        acc = g_ref[0].astype(F32)
        for k in range(1, ns):
            acc = acc + g_ref[k].astype(F32)
        o_ref[...] = acc

    return _pc(body, grid=(R // tr,), in_specs=[pl.BlockSpec((ns, tr, W), lambda i: (0, i, 0))],
               out_specs=pl.BlockSpec((tr, W), lambda i: (i, 0)), out_shape=SDS((R, W), F32), name=name)(gs)


HBM_SPEC = pl.BlockSpec(memory_space=pltpu.HBM)
VMEM_SPEC = pl.BlockSpec(memory_space=pltpu.VMEM)


def _dev():
    return lax.axis_index("x"), lax.axis_index("y"), lax.axis_index("c")


def _allgather_big(loc, name):
    R, W = loc.shape

    def body(x_ref, out_ref, send_sems, recv_sems, local_sem):
        x, y, c = _dev()
        me, sibling = (x, y, c), (x, y, 1 - c)
        chips = [(1 - x, y), (x, 1 - y), (1 - x, 1 - y)]

        def slab(px, py, pc):
            return out_ref.at[4 * px + 2 * py + pc]

        def copy(k, block, to, src=None):
            return pltpu.make_async_remote_copy(
                src_ref=slab(*block) if src is None else src, dst_ref=slab(*block),
                send_sem=send_sems.at[k], recv_sem=recv_sems.at[k], device_id=to, device_id_type=MESH)

        mine = pltpu.make_async_copy(x_ref, slab(*me), local_sem)
        mine.start()
        first = [copy(0, me, sibling, src=x_ref)]
        first += [copy(1 + j, me, (*chip, c), src=x_ref) for j, chip in enumerate(chips)]
        for cp in first:
            cp.start()
        passed = [copy(4 + j, (*chip, c), sibling) for j, chip in enumerate(chips)]
        for j, chip in enumerate(chips):
            copy(1 + j, (*chip, c), me).wait_recv()
            passed[j].start()
        copy(0, sibling, me).wait_recv()
        for j, chip in enumerate(chips):
            copy(4 + j, (*chip, 1 - c), me).wait_recv()
        for cp in first + passed:
            cp.wait_send()
        mine.wait()

    return pl.pallas_call(
        body, out_shape=SDS((N_DEV, R, W), loc.dtype), in_specs=[HBM_SPEC], out_specs=HBM_SPEC,
        scratch_shapes=[pltpu.SemaphoreType.DMA((7,)), pltpu.SemaphoreType.DMA((7,)), pltpu.SemaphoreType.DMA],
        name=name)(loc)


def _dma_sems(n):
    return (pltpu.SemaphoreType.DMA((n,)), pltpu.SemaphoreType.DMA((n,)), pltpu.SemaphoreType.DMA)


def _side_gather_direct(loc):
    R, W = loc.shape

    def copies(ins, outs, sems):
        x, y, c = _dev()
        x_ref, out = ins[0], outs[0]
        me = 4 * x + 2 * y + c
        peers = [(x, y, 1 - c), (1 - x, y, c), (x, 1 - y, c), (1 - x, 1 - y, c)]
        mk = lambda k, p, dst: pltpu.make_async_remote_copy(
            src_ref=x_ref, dst_ref=out.at[dst], send_sem=sems[0].at[k], recv_sem=sems[1].at[k], device_id=p,
            device_id_type=MESH)
        sends = [mk(k, p, me) for k, p in enumerate(peers)]
        recvs = [mk(k, p, 4 * p[0] + 2 * p[1] + p[2]) for k, p in enumerate(peers)]
        return sends, recvs, pltpu.make_async_copy(x_ref, out.at[me], sems[2])

    def start(ins, outs, sems):
        sends, _, mine = copies(ins, outs, sems)
        mine.start()
        for cp in sends:
            cp.start()

    def wait(ins, outs, sems):
        sends, recvs, mine = copies(ins, outs, sems)
        for cp in recvs:
            cp.wait_recv()
        for cp in sends:
            cp.wait_send()
        mine.wait()

    return _Side((loc,), (SDS((N_DEV, R, W), loc.dtype),), _dma_sems(4), start, wait)


def _side_gather_forward(buf):
    def copies(ins, outs, sems):
        x, y, c = _dev()
        out = outs[0]
        chips = [(1 - x, y), (x, 1 - y), (1 - x, 1 - y)]
        mk = lambda k, src, dst: pltpu.make_async_remote_copy(
            src_ref=out.at[src], dst_ref=out.at[dst], send_sem=sems[0].at[k], recv_sem=sems[1].at[k],
            device_id=(x, y, 1 - c), device_id_type=MESH)
        mine = [4 * px + 2 * py + c for px, py in chips]
        theirs = [4 * px + 2 * py + (1 - c) for px, py in chips]
        return [mk(k, s, s) for k, s in enumerate(mine)], [mk(k, s, t) for k, (s, t) in enumerate(zip(mine, theirs))]

    def start(ins, outs, sems):
        for cp in copies(ins, outs, sems)[0]:
            cp.start()

    def wait(ins, outs, sems):
        sends, recvs = copies(ins, outs, sems)
        for cp in recvs:
            cp.wait_recv()
        for cp in sends:
            cp.wait_send()

    return _Side((buf,), (SDS(buf.shape, buf.dtype),), _dma_sems(3)[:2], start, wait, {0: 0})


def _side_chip_exchange(p):
    def copies(ins, outs, sems):
        x, y, c = _dev()
        p_ref, out = ins[0], outs[0]
        j0 = 2 * x + y
        chips = [(1 - x, y), (x, 1 - y), (1 - x, 1 - y)]
        mk = lambda k, chip, src, dst: pltpu.make_async_remote_copy(
            src_ref=p_ref.at[src], dst_ref=out.at[dst], send_sem=sems[0].at[k], recv_sem=sems[1].at[k],
            device_id=(*chip, c), device_id_type=MESH)
        sends = [mk(k, chip, 2 * chip[0] + chip[1], j0) for k, chip in enumerate(chips)]
        recvs = [mk(k, chip, j0, 2 * chip[0] + chip[1]) for k, chip in enumerate(chips)]
        return sends, recvs, pltpu.make_async_copy(p_ref.at[j0], out.at[j0], sems[2])

    def start(ins, outs, sems):
        sends, _, mine = copies(ins, outs, sems)
        mine.start()
        for cp in sends:
            cp.start()

    def wait(ins, outs, sems):
        sends, recvs, mine = copies(ins, outs, sems)
        for cp in recvs:
            cp.wait_recv()
        for cp in sends:
            cp.wait_send()
        mine.wait()

    return _Side((p,), (SDS(p.shape, p.dtype),), _dma_sems(3), start, wait)


def _rs_pair_exchange(g, name):
    nd, R, W = g.shape
    nj = nd // 2

    def body(g_ref, out_ref, send_sems, recv_sems):
        x, y, c = _dev()
        cps = [pltpu.make_async_remote_copy(src_ref=g_ref.at[2 * j + (1 - c)], dst_ref=out_ref.at[j],
                                            send_sem=send_sems.at[j], recv_sem=recv_sems.at[j],
                                            device_id=(x, y, 1 - c), device_id_type=MESH) for j in range(nj)]
        for cp in cps:
            cp.start()
        for cp in cps:
            cp.wait()

    return pl.pallas_call(
        body, out_shape=SDS((nj, R, W), g.dtype), in_specs=[HBM_SPEC], out_specs=HBM_SPEC,
        scratch_shapes=[pltpu.SemaphoreType.DMA((nj,)), pltpu.SemaphoreType.DMA((nj,))], name=name)(g)


def _add_pairs(g, ra, name):
    nd, R, W = g.shape
    nj = nd // 2
    tr = _tile(R, 256, 8)
    cidx = lax.axis_index("c").astype(jnp.int32).reshape(1)

    def body(c_ref, a_ref, b_ref, o_ref):
        o_ref[...] = (a_ref[...].astype(F32) + b_ref[...].astype(F32)).astype(o_ref.dtype)

    gs = pltpu.PrefetchScalarGridSpec(
        num_scalar_prefetch=1, grid=(nj, R // tr),
        in_specs=[pl.BlockSpec((None, tr, W), lambda j, i, cr: (2 * j + cr[0], i, 0)),
                  pl.BlockSpec((None, tr, W), lambda j, i, cr: (j, i, 0))],
        out_specs=pl.BlockSpec((None, tr, W), lambda j, i, cr: (j, i, 0)))
    return pl.pallas_call(body, grid_spec=gs, out_shape=SDS((nj, R, W), g.dtype), name=name,
                          compiler_params=pltpu.CompilerParams(vmem_limit_bytes=VMEM_LIMIT))(cidx, g, ra)


def _rs_chip_exchange(p, name):
    nj, R, W = p.shape

    def body(p_ref, out_ref, send_sems, recv_sems, local_sem):
        x, y, c = _dev()
        j0 = 2 * x + y
        chips = [(1 - x, y), (x, 1 - y), (1 - x, 1 - y)]
        mine = pltpu.make_async_copy(p_ref.at[j0], out_ref.at[j0], local_sem)
        mine.start()

        def copy(k, chip):
            return pltpu.make_async_remote_copy(
                src_ref=p_ref.at[2 * chip[0] + chip[1]], dst_ref=out_ref.at[j0],
                send_sem=send_sems.at[k], recv_sem=recv_sems.at[k], device_id=(*chip, c), device_id_type=MESH)

        sent = [copy(k, chip) for k, chip in enumerate(chips)]
        for cp in sent:
            cp.start()
        for k, chip in enumerate(chips):
            pltpu.make_async_remote_copy(
                src_ref=p_ref.at[j0], dst_ref=out_ref.at[2 * chip[0] + chip[1]],
                send_sem=send_sems.at[k], recv_sem=recv_sems.at[k], device_id=(*chip, c), device_id_type=MESH).wait_recv()
        for cp in sent:
            cp.wait_send()
        mine.wait()

    return pl.pallas_call(
        body, out_shape=SDS((nj, R, W), p.dtype), in_specs=[HBM_SPEC], out_specs=HBM_SPEC,
        scratch_shapes=[pltpu.SemaphoreType.DMA((3,)), pltpu.SemaphoreType.DMA((3,)), pltpu.SemaphoreType.DMA],
        name=name)(p)


def _allgather_small(v, name):
    R, W = v.shape

    def body(v_ref, out_ref, send_sems, recv_sems, local_sem):
        x, y, c = _dev()
        mine = pltpu.make_async_copy(v_ref, out_ref.at[4 * x + 2 * y + c], local_sem)
        mine.start()
        peers = []
        for k in range(1, N_DEV):
            px = 1 - x if k & 4 else x
            py = 1 - y if k & 2 else y
            pc_ = 1 - c if k & 1 else c
            peers.append((px, py, pc_))
        sent = [pltpu.make_async_remote_copy(
            src_ref=v_ref, dst_ref=out_ref.at[4 * x + 2 * y + c], send_sem=send_sems.at[k], recv_sem=recv_sems.at[k],
            device_id=peer, device_id_type=MESH) for k, peer in enumerate(peers)]
        for cp in sent:
            cp.start()
        for k, (px, py, pc_) in enumerate(peers):
            pltpu.make_async_remote_copy(
                src_ref=v_ref, dst_ref=out_ref.at[4 * px + 2 * py + pc_], send_sem=send_sems.at[k],
                recv_sem=recv_sems.at[k], device_id=(px, py, pc_), device_id_type=MESH).wait_recv()
        for cp in sent:
            cp.wait_send()
        mine.wait()

    return pl.pallas_call(
        body, out_shape=SDS((N_DEV, R, W), v.dtype), in_specs=[VMEM_SPEC], out_specs=VMEM_SPEC,
        scratch_shapes=[pltpu.SemaphoreType.DMA((7,)), pltpu.SemaphoreType.DMA((7,)), pltpu.SemaphoreType.DMA],
        name=name)(v)


def _sum_slabs(a, name):
    ns, R, W = a.shape

    def body(a_ref, o_ref):
        acc = a_ref[0]
        for k in range(1, ns):
            acc = acc + a_ref[k]
        o_ref[...] = acc

    return pl.pallas_call(body, out_shape=SDS((R, W), a.dtype), in_specs=[VMEM_SPEC], out_specs=VMEM_SPEC, name=name)(a)


BIG = (("ada_w", "col"), ("w_in", "col"), ("w_ssd_out", "row"), ("w_sc_out", "row"), ("w_o", "row"), ("w_up", "col"),
       ("w_down", "row"))
EARLY = ("w_ssd_out", "w_sc_out", "w_o", "w_up", "w_down")
LATE = ("ada_w", "w_in")
MID_LAYOUT = ("w_in",)
SWAP_LAYOUT = ("w_up",)
CONVW = ("ssd_conv_w", "sc_conv_w", "ffn_conv_w")
REPL = ("ada_b", "mix_pre_g", "mix_post_g", "ssd_conv_b", "ssd_dt_bias", "ssd_a_log", "ssd_d", "ssd_norm_g", "ffn_pre_g",
        "ffn_post_g", "ffn_conv_b")
WEIGHTS = ("ada_w", "ada_b", "mix_pre_g", "mix_post_g", "w_in", "ssd_conv_w", "ssd_conv_b", "ssd_dt_bias", "ssd_a_log",
           "ssd_d", "ssd_norm_g", "w_ssd_out", "sc_conv_w", "w_sc_out", "w_o", "ffn_pre_g", "ffn_post_g", "w_up",
           "ffn_conv_w", "ffn_conv_b", "w_down")


def _pad_rows(a, mult):
    r = a.shape[-2]
    pad = -r % mult
    return a if pad == 0 else jnp.pad(a, [(0, 0)] * (a.ndim - 2) + [(0, pad), (0, 0)])


def _flat_rows(parts, mult):
    flat = jnp.concatenate([p.reshape(-1) for p in parts])
    flat = jnp.pad(flat, (0, -flat.shape[0] % ROW_W))
    return _pad_rows(flat.reshape(-1, ROW_W), mult)


def _unflat(buf, shapes):
    flat = buf.reshape(-1)
    out, o = [], 0
    for shp in shapes:
        n = 1
        for s in shp:
            n *= s
        out.append(flat[o:o + n].reshape(shp))
        o += n
    return out


def _pack_big_local(get, l):
    return [_pad_rows((get(n)[l].T if kind == "col" else get(n)[l]).reshape(-1, ROW_W), SLAB_ALIGN) for n, kind in BIG]


def _big_rows(shapes, names=None):
    out, o = {}, 0
    for n in (names if names is not None else [n for n, _ in BIG]):
        r = shapes[n][1] * shapes[n][2] // ROW_W
        out[n] = (o, o + r)
        o += -(-r // SLAB_ALIGN) * SLAB_ALIGN
    return out, o


def kernel(x, c, ada_w, ada_b, mix_pre_g, mix_post_g, w_in, ssd_conv_w, ssd_conv_b, ssd_dt_bias, ssd_a_log, ssd_d, ssd_norm_g, w_ssd_out, sc_conv_w, w_sc_out, w_o, ffn_pre_g, ffn_post_g, w_up, ffn_conv_w, ffn_conv_b, w_down, loss_target, m_ada_w, m_ada_b, m_mix_pre_g, m_mix_post_g, m_w_in, m_ssd_conv_w, m_ssd_conv_b, m_ssd_dt_bias, m_ssd_a_log, m_ssd_d, m_ssd_norm_g, m_w_ssd_out, m_sc_conv_w, m_w_sc_out, m_w_o, m_ffn_pre_g, m_ffn_post_g, m_w_up, m_ffn_conv_w, m_ffn_conv_b, m_w_down, v_ada_w, v_ada_b, v_mix_pre_g, v_mix_post_g, v_w_in, v_ssd_conv_w, v_ssd_conv_b, v_ssd_dt_bias, v_ssd_a_log, v_ssd_d, v_ssd_norm_g, v_w_ssd_out, v_sc_conv_w, v_w_sc_out, v_w_o, v_ffn_pre_g, v_ffn_post_g, v_w_up, v_ffn_conv_w, v_ffn_conv_b, v_w_down):
    wl = dict(zip(WEIGHTS, (ada_w, ada_b, mix_pre_g, mix_post_g, w_in, ssd_conv_w, ssd_conv_b, ssd_dt_bias, ssd_a_log,
                            ssd_d, ssd_norm_g, w_ssd_out, sc_conv_w, w_sc_out, w_o, ffn_pre_g, ffn_post_g, w_up,
                            ffn_conv_w, ffn_conv_b, w_down)))
    ml = dict(zip(WEIGHTS, (m_ada_w, m_ada_b, m_mix_pre_g, m_mix_post_g, m_w_in, m_ssd_conv_w, m_ssd_conv_b,
                            m_ssd_dt_bias, m_ssd_a_log, m_ssd_d, m_ssd_norm_g, m_w_ssd_out, m_sc_conv_w, m_w_sc_out, m_w_o,
                            m_ffn_pre_g, m_ffn_post_g, m_w_up, m_ffn_conv_w, m_ffn_conv_b, m_w_down)))
    vl = dict(zip(WEIGHTS, (v_ada_w, v_ada_b, v_mix_pre_g, v_mix_post_g, v_w_in, v_ssd_conv_w, v_ssd_conv_b,
                            v_ssd_dt_bias, v_ssd_a_log, v_ssd_d, v_ssd_norm_g, v_w_ssd_out, v_sc_conv_w, v_w_sc_out, v_w_o,
                            v_ffn_pre_g, v_ffn_post_g, v_w_up, v_ffn_conv_w, v_ffn_conv_b, v_w_down)))
    depth = ada_w.shape[0]
    shapes = {n: wl[n].shape for n in WEIGHTS}
    me = 4 * lax.axis_index("x") + 2 * lax.axis_index("y") + lax.axis_index("c")

    rows, n_big = _big_rows(shapes)
    conv_flat = jnp.concatenate([wl[n][l].reshape(-1) for l in range(depth) for n in CONVW])
    n_conv = conv_flat.shape[0]
    conv_flat = jnp.pad(conv_flat, (0, -n_conv % (ROW_W // 2)))
    conv_rows = lax.bitcast_convert_type(conv_flat, BF16).reshape(-1, ROW_W)

    def local_rows(l):
        pieces = _pack_big_local(lambda n: wl[n].astype(BF16), l) + ([conv_rows] if l == 0 else [])
        return _pad_rows(jnp.concatenate(pieces, axis=0), ROW_PAD)

    def layer_weights(l, gathered):
        w = {n: wl[n][l] for n in REPL}
        for n, kind in BIG:
            a, b = shapes[n][1], shapes[n][2]
            blk = gathered[:, rows[n][0]:rows[n][1]]
            w[n] = blk.reshape(N_DEV * b, a) if kind == "col" else blk.reshape(N_DEV * a, b)
        for n in CONVW:
            w[n] = conv_full[(l, n)]
        return w

    gathered = _allgather_big(local_rows(0), "allgather_weights")
    conv_all = lax.bitcast_convert_type(
        gathered[:, n_big:n_big + conv_rows.shape[0]].reshape(N_DEV, -1, 2), F32)[:, :n_conv]
    conv_full, o = {}, 0
    for l in range(depth):
        for n in CONVW:
            k, cl = shapes[n][1], shapes[n][2]
            conv_full[(l, n)] = conv_all[:, o:o + k * cl].reshape(N_DEV, k, cl).transpose(1, 0, 2).reshape(k, N_DEV * cl)
            o += k * cl

    nb, S, D = x.shape
    T = nb * S
    act = x.reshape(T, D)
    c8 = jnp.pad(c, ((0, MOD_ROWS - nb), (0, 0)))
    preps, saved = [], []
    for l in range(depth):
        preps.append(_prep_layer(layer_weights(l, gathered)))
        act, s, gathered = _layer_fwd(act, c8, preps[l], S, l, gather=local_rows(l + 1) if l + 1 < depth else None)
        saved.append(s)
    dy, lacc = _loss(act, loss_target.reshape(T, D), S, "loss")
    loss_loc = lacc[0, 0]

    group_rows = {grp: _big_rows(shapes, names) for grp, names in (("early", EARLY), ("late", LATE))}

    def pair_sums(g, grp, names, l):
        slabs = [_pad_rows(g[n].astype(BF16).reshape(N_DEV, -1, ROW_W), SLAB_ALIGN) for n in names]
        slabs.append(jnp.zeros((N_DEV, -group_rows[grp][1] % ROW_PAD, ROW_W), BF16))
        gslab = jnp.concatenate(slabs, axis=1)
        from_sibling = _rs_pair_exchange(gslab, f"rs_pair_exchange_{grp}_l{l}")
        return _add_pairs(gslab, from_sibling, f"rs_pair_add_{grp}_l{l}")

    grads, pending = [None] * depth, None
    from_chips = {"early": [None] * depth, "late": [None] * depth}
    for l in reversed(range(depth)):
        dy, grads[l], got, from_chips["early"][l] = _layer_bwd(
            dy, preps[l], saved[l], S, l, chip_sums=pending, early=lambda g, l=l: pair_sums(g, "early", EARLY, l))
        if pending is not None:
            from_chips["late"][l + 1] = got
        pending = pair_sums(grads[l], "late", LATE, l)
    from_chips["late"][0] = _rs_chip_exchange(pending, "rs_chip_exchange")
    dx = dy.reshape(nb, S, D)
    g_sums = {grp: [_sum_chips(from_chips[grp][l], f"rs_chip_sum_{grp}_l{l}") for l in range(depth)]
              for grp in ("early", "late")}

    def slab_of(l, n, kind):
        grp = "early" if n in EARLY else "late"
        r0, r1 = group_rows[grp][0][n]
        a, b = shapes[n][1], shapes[n][2]
        return g_sums[grp][l][r0:r1].reshape((b, a) if kind == "col" else (a, b))

    g_big, d_big, m_big, v_big = {}, {}, {}, {}
    for n, kind in BIG:
        if n in MID_LAYOUT:
            gm = jnp.stack([slab_of(l, n, kind) for l in range(depth)], axis=1)
            fwd, back = (lambda t: t.transpose(2, 0, 1)), (lambda t: t.transpose(1, 2, 0))
            res = [gm] + list(_adam_mid(gm, fwd(wl[n]), fwd(ml[n]), fwd(vl[n]), f"adam_{n}"))
        else:
            gt = jnp.stack([slab_of(l, n, kind) for l in range(depth)])
            fwd = back = (lambda t: t.swapaxes(1, 2)) if kind == "col" else (lambda t: t)
            if n in SWAP_LAYOUT:
                res = [gt] + list(_adam_nat(gt, fwd(wl[n]), fwd(ml[n]), fwd(vl[n]), f"adam_{n}"))
            else:
                gn = back(gt)
                res, back = [gn] + list(_adam_nat(gn, wl[n], ml[n], vl[n], f"adam_{n}")), (lambda t: t)
        g_big[n], d_big[n], m_big[n], v_big[n] = [back(t) for t in res]

    parts = [jnp.broadcast_to(loss_loc, (ROW_W,))]
    small_shapes = [(ROW_W,)]
    for l in range(depth):
        for n in REPL + CONVW:
            parts.append(grads[l][n])
            small_shapes.append(tuple(grads[l][n].shape))
    total = _sum_slabs(_allgather_small(_flat_rows(parts, 8), "allgather_small"), "sum_small")
    pieces = _unflat(total, small_shapes)
    loss = pieces[0][0]
    g_small, i = {}, 1
    for l in range(depth):
        for n in REPL + CONVW:
            gp = pieces[i]
            i += 1
            if n in CONVW:
                gp = lax.dynamic_slice_in_dim(gp, me * shapes[n][2], shapes[n][2], axis=1)
            g_small[(l, n)] = gp
    order = [(l, n) for l in range(depth) for n in REPL + CONVW]
    loc_shapes = [tuple(shapes[n][1:]) for _, n in order]
    packs = lambda f: _flat_rows([f(l, n) for l, n in order], 8)
    gs_small = packs(lambda l, n: g_small[(l, n)])
    _, d_sm, m_sm, v_sm = _adam(gs_small[None], packs(lambda l, n: wl[n][l]), packs(lambda l, n: ml[n][l]),
                                packs(lambda l, n: vl[n][l]), "adam_small")

    def unpack_small(buf):
        ps = _unflat(buf, loc_shapes)
        return {n: jnp.stack([ps[order.index((l, n))] for l in range(depth)]) for n in REPL + CONVW}

    outs = []
    for big, small in ((g_big, {n: jnp.stack([g_small[(l, n)] for l in range(depth)]) for n in REPL + CONVW}),
                       (d_big, unpack_small(d_sm)), (m_big, unpack_small(m_sm)), (v_big, unpack_small(v_sm))):
        merged = {**big, **small}
        outs += [merged[n] for n in WEIGHTS]
    return (loss, dx, *outs)
```

```python
import functools
import math
from typing import Callable, NamedTuple

import jax
import jax.numpy as jnp
from jax import lax
from jax.experimental import pallas as pl
from jax.experimental.pallas import tpu as pltpu

F32, BF16 = jnp.float32, jnp.bfloat16
WGRAD = BF16
SDS = jax.ShapeDtypeStruct
MESH = pl.DeviceIdType.MESH

EPS = 1e-6
N_STATE = 128
CHUNK = 128
SSD_STEP_CHUNKS = 2
SSD_CONV_K, SC_CONV_K, FFN_CONV_K = 4, 3, 3
N_DEV = 8
ROW_W = 1024
ROW_PAD = 32
SLAB_ALIGN = 16
SEG_BLK = 512
STRIP = 32
FFN_STRIP = 64
GLU_W = 256
MOD_ROWS = 128
VMEM_LIMIT = 48 * 2**20

ADAM_LR, ADAM_B1, ADAM_B2, ADAM_EPS, ADAM_WD, ADAM_STEP = 0.001, 0.9, 0.999, 1e-08, 0.01, 10

NT = (((1,), (1,)), ((), ()))
TN = (((0,), (0,)), ((), ()))
NN = (((1,), (0,)), ((), ()))


def _tile(n, cap, mult=128):
    best = None
    for t in range(mult, min(n, cap) + 1, mult):
        if n % t == 0:
            best = t
    return best if best is not None else n


class _Side(NamedTuple):
    operands: tuple
    out_shape: tuple
    scratch: tuple
    start: Callable
    wait: Callable
    aliases: dict = {}


def _pc(body, *, grid, in_specs, out_specs, out_shape, name, scratch=(), side=None):
    params = pltpu.CompilerParams(dimension_semantics=("arbitrary",) * len(grid), vmem_limit_bytes=VMEM_LIMIT)
    if side is None:
        return pl.pallas_call(body, grid=grid, in_specs=in_specs, out_specs=out_specs, out_shape=out_shape,
                              scratch_shapes=list(scratch), name=name, compiler_params=params)
    single = not isinstance(out_shape, (list, tuple))
    outs = [out_shape] if single else list(out_shape)
    ospecs = [out_specs] if single else list(out_specs)
    n_in, n_out, n_scr = len(in_specs), len(outs), len(scratch)
    s_in, s_out = len(side.operands), len(side.out_shape)

    def hosted(*refs):
        ins, refs = refs[:n_in], refs[n_in:]
        sins, refs = refs[:s_in], refs[s_in:]
        mouts, refs = refs[:n_out], refs[n_out:]
        souts, refs = refs[:s_out], refs[s_out:]
        scr, sems = refs[:n_scr], refs[n_scr:]
        first = functools.reduce(lambda a, b: a & b, [pl.program_id(a) == 0 for a in range(len(grid))])
        last = functools.reduce(lambda a, b: a & b, [pl.program_id(a) == grid[a] - 1 for a in range(len(grid))])

        @pl.when(first)
        def _():
            side.start(sins, souts, sems)

        body(*ins, *mouts, *scr)

        @pl.when(last)
        def _():
            side.wait(sins, souts, sems)

    call = pl.pallas_call(
        hosted, grid=grid, in_specs=list(in_specs) + [HBM_SPEC] * s_in, out_specs=ospecs + [HBM_SPEC] * s_out,
        out_shape=outs + list(side.out_shape), scratch_shapes=list(scratch) + list(side.scratch), name=name,
        input_output_aliases={n_in + k: n_out + v for k, v in side.aliases.items()}, compiler_params=params)

    def run(*args):
        res = call(*args, *side.operands)
        main = res[0] if single else list(res[:n_out])
        return main, list(res[n_out:])

    return run


def _silu(x):
    return x * jax.nn.sigmoid(x)


def _dsilu(x):
    s = jax.nn.sigmoid(x)
    return s * (1.0 + x * (1.0 - s))


def _softplus(x):
    return jnp.maximum(x, 0.0) + jnp.log(1.0 + jnp.exp(-jnp.abs(x)))


def _dot(a, b, dims=NN):
    return lax.dot_general(a, b, dims, preferred_element_type=F32)


def _bsum(v, rows=8):
    return jnp.broadcast_to(v, (rows, v.shape[1]))


def _mm(a, b, mode, out_dtype, name, caps=(1024, 1024, 2048), side=None):
    if mode == "nn":
        (M, K), (K2, N) = a.shape, b.shape
    elif mode == "nt":
        (M, K), (N, K2) = a.shape, b.shape
    else:
        (K, M), (K2, N) = a.shape, b.shape
    assert K == K2, (a.shape, b.shape, mode)
    tm, tn, tk = _tile(M, caps[0]), _tile(N, caps[1]), _tile(K, caps[2])
    nk = K // tk
    dims = {"nn": NN, "nt": NT, "tn": TN}[mode]
    if mode == "tn":
        a_spec = pl.BlockSpec((tk, tm), lambda i, j, k: (k, i))
    else:
        a_spec = pl.BlockSpec((tm, tk), lambda i, j, k: (i, k))
    if mode == "nt":
        b_spec = pl.BlockSpec((tn, tk), lambda i, j, k: (j, k))
    else:
        b_spec = pl.BlockSpec((tk, tn), lambda i, j, k: (k, j))

    def body(a_ref, b_ref, o_ref, *acc):
        part = _dot(a_ref[...].astype(BF16), b_ref[...].astype(BF16), dims)
        if nk == 1:
            o_ref[...] = part.astype(o_ref.dtype)
        else:
            acc_ref, = acc
            k = pl.program_id(2)

            @pl.when(k == 0)
            def _():
                acc_ref[...] = part

            @pl.when(k > 0)
            def _():
                acc_ref[...] += part

            @pl.when(k == nk - 1)
            def _():
                o_ref[...] = acc_ref[...].astype(o_ref.dtype)

    return _pc(body, grid=(M // tm, N // tn, nk), in_specs=[a_spec, b_spec],
               out_specs=pl.BlockSpec((tm, tn), lambda i, j, k: (i, j)),
               out_shape=SDS((M, N), out_dtype), name=name,
               scratch=() if nk == 1 else (pltpu.VMEM((tm, tn), F32),), side=side)(a, b)


def _mm_seg(segs, b, mode, out_dtype, name, blk, tile=1024, tk=2048):
    nblk = [a.shape[1] // blk for a in segs]
    assert all(a.shape[1] % blk == 0 for a in segs)
    start = [sum(nblk[:s]) for s in range(len(segs))]
    total = sum(nblk)
    ns = len(segs)
    N = b.shape[1]
    tn = _tile(N, tile)
    if mode == "nn":
        M = segs[0].shape[0]
        tm = _tile(M, tile)
        grid = (M // tm, N // tn, total)
        a_specs = [pl.BlockSpec((tm, blk), lambda i, j, k, k0=k0, n=n: (i, jnp.clip(k - k0, 0, n - 1)))
                   for k0, n in zip(start, nblk)]
        b_spec = pl.BlockSpec((blk, tn), lambda i, j, k: (k, j))
        out_rows, tmo, dims, seg_axis = M, tm, NN, 2
    else:
        K = segs[0].shape[0]
        tkk = _tile(K, tk)
        grid = (total, N // tn, K // tkk)
        a_specs = [pl.BlockSpec((tkk, blk), lambda i, j, k, i0=i0, n=n: (
            jnp.where((i >= i0) & (i < i0 + n), k, 0), jnp.clip(i - i0, 0, n - 1))) for i0, n in zip(start, nblk)]
        b_spec = pl.BlockSpec((tkk, tn), lambda i, j, k: (k, j))
        out_rows, tmo, seg_axis = total * blk, blk, 0
    nk = grid[2]
    acc_shape = (tm, tn) if mode == "nn" else (tn, blk)

    def body(*refs):
        a_refs, b_ref, o_ref, acc_ref = refs[:ns], refs[ns], refs[ns + 1], refs[ns + 2]
        k = pl.program_id(2)
        sel = pl.program_id(seg_axis)

        @pl.when(k == 0)
        def _():
            acc_ref[...] = jnp.zeros_like(acc_ref)

        for s in range(ns):
            @pl.when((sel >= start[s]) & (sel < start[s] + nblk[s]))
            def _(s=s):
                a_, b_ = a_refs[s][...].astype(BF16), b_ref[...].astype(BF16)
                acc_ref[...] += _dot(a_, b_, NN) if mode == "nn" else _dot(b_, a_, TN)

        @pl.when(k == nk - 1)
        def _():
            acc = acc_ref[...]
            o_ref[...] = (acc if mode == "nn" else acc.T).astype(o_ref.dtype)

    return _pc(body, grid=grid, in_specs=a_specs + [b_spec], out_specs=pl.BlockSpec((tmo, tn), lambda i, j, k: (i, j)),
               out_shape=SDS((out_rows, N), out_dtype), name=name, scratch=(pltpu.VMEM(acc_shape, F32),))(*segs, b)


def _modk(c8, ada_w, ada_b, name):
    rows, D = c8.shape
    N = ada_w.shape[0]
    tn = _tile(N, 1536)

    def body(c_ref, w_ref, b_ref, mod_ref, ca_ref):
        ca = _silu(c_ref[...]).astype(BF16)
        mod_ref[...] = _dot(ca, w_ref[...], NT) + b_ref[...]
        ca_ref[...] = ca

    return _pc(body, grid=(N // tn,),
               in_specs=[pl.BlockSpec((rows, D), lambda j: (0, 0)), pl.BlockSpec((tn, D), lambda j: (j, 0)),
                         pl.BlockSpec((1, tn), lambda j: (0, j))],
               out_specs=[pl.BlockSpec((rows, tn), lambda j: (0, j)), pl.BlockSpec((rows, D), lambda j: (0, 0))],
               out_shape=[SDS((rows, N), F32), SDS((rows, D), BF16)], name=name)(c8, ada_w, ada_b)


def _row_tile(S):
    return _tile(S, 512, 8)


def _strip_row_tile(S, cap=2048):
    return _tile(S, cap, FFN_STRIP)


def _strips(tm, fn, init=0, rows=None):
    rows = STRIP if rows is None else rows
    assert tm % rows == 0
    return lax.fori_loop(0, tm // rows, lambda r, c: fn(pl.multiple_of(r * rows, rows), c), init)


def _strips_prev(tm, rows, ref, prev, fn, init=0):
    carry = fn(0, jnp.concatenate([prev, ref[0:rows, :]], axis=0), init)

    def step(r, c):
        r0 = pl.multiple_of(r * rows, rows)
        return fn(r0, ref[pl.ds(pl.multiple_of(r0 - 8, 8), rows + 8), :], c)

    return lax.fori_loop(1, tm // rows, step, carry)


def _strips_next(tm, rows, ref, nxt, fn, init=0):
    def step(r, c):
        r0 = pl.multiple_of(r * rows, rows)
        return fn(r0, ref[pl.ds(r0, rows + 8), :], c)

    carry = lax.fori_loop(0, tm // rows - 1, step, init)
    return fn(tm - rows, jnp.concatenate([ref[tm - rows:tm, :], nxt], axis=0), carry)


def _rows8(rows):
    pad = 8 - len(rows)
    return jnp.concatenate(rows + ([jnp.zeros((pad, rows[0].shape[1]), F32)] if pad else []), axis=0)


def _fold8(v):
    return jnp.sum(v.reshape(v.shape[0] // 8, 8, v.shape[1]), axis=0)


def _norm_mod(x, g, mod3, sc_seg, sh_seg, S, name):
    T, D = x.shape
    tm = _row_tile(S)
    tpb = S // tm

    def body(x_ref, g_ref, sc_ref, sh_ref, h_ref):
        x_ = x_ref[...]
        r = lax.rsqrt(jnp.mean(x_ * x_, axis=-1, keepdims=True) + EPS)
        h_ref[...] = ((x_ * r) * (g_ref[...] * (1.0 + sc_ref[...])) + sh_ref[...]).astype(BF16)

    return _pc(body, grid=(T // tm,),
               in_specs=[pl.BlockSpec((tm, D), lambda i: (i, 0)), pl.BlockSpec((1, D), lambda i: (0, 0)),
                         pl.BlockSpec((None, 1, D), lambda i: (i // tpb, 0, sc_seg)),
                         pl.BlockSpec((None, 1, D), lambda i: (i // tpb, 0, sh_seg))],
               out_specs=pl.BlockSpec((tm, D), lambda i: (i, 0)), out_shape=SDS((T, D), BF16), name=name)(x, g, mod3, mod3)


def _resid_post(x, fo, mod3, gt_seg, pg, S, name):
    T, D = x.shape
    tm = _row_tile(S)
    tpb = S // tm

    def body(x_ref, f_ref, gt_ref, pg_ref, o_ref):
        f = f_ref[...]
        r = lax.rsqrt(jnp.mean(f * f, axis=-1, keepdims=True) + EPS)
        o_ref[...] = x_ref[...] + (f * r) * (gt_ref[...] * pg_ref[...])

    return _pc(body, grid=(T // tm,),
               in_specs=[pl.BlockSpec((tm, D), lambda i: (i, 0)), pl.BlockSpec((tm, D), lambda i: (i, 0)),
                         pl.BlockSpec((None, 1, D), lambda i: (i // tpb, 0, gt_seg)),
                         pl.BlockSpec((1, D), lambda i: (0, 0))],
               out_specs=pl.BlockSpec((tm, D), lambda i: (i, 0)), out_shape=SDS((T, D), F32), name=name)(x, fo, mod3, pg)


def _post_bwd(fo, mod3, gt_seg, pg, dout, S, name):
    T, D = fo.shape
    tm = _row_tile(S)
    tpb = S // tm
    nb = T // S

    def body(f_ref, gt_ref, pg_ref, d_ref, df_ref, dgt_ref, dpg_ref):
        i = pl.program_id(0)

        @pl.when(i == 0)
        def _():
            dpg_ref[...] = jnp.zeros_like(dpg_ref)

        @pl.when(i % tpb == 0)
        def _():
            dgt_ref[...] = jnp.zeros_like(dgt_ref)

        f, d = f_ref[...], d_ref[...]
        r = lax.rsqrt(jnp.mean(f * f, axis=-1, keepdims=True) + EPS)
        n = f * r
        dn = d * (gt_ref[...] * pg_ref[...])
        df_ref[...] = (r * (dn - n * jnp.mean(dn * n, axis=-1, keepdims=True))).astype(df_ref.dtype)
        tot = jnp.sum(d * n, axis=0, keepdims=True)
        dgt_ref[...] += _bsum(tot * pg_ref[...])
        dpg_ref[...] += _bsum(tot * gt_ref[...])

    return _pc(body, grid=(T // tm,),
               in_specs=[pl.BlockSpec((tm, D), lambda i: (i, 0)),
                         pl.BlockSpec((None, 1, D), lambda i: (i // tpb, 0, gt_seg)),
                         pl.BlockSpec((1, D), lambda i: (0, 0)), pl.BlockSpec((tm, D), lambda i: (i, 0))],
               out_specs=[pl.BlockSpec((tm, D), lambda i: (i, 0)), pl.BlockSpec((8, D), lambda i: (i // tpb, 0)),
                          pl.BlockSpec((8, D), lambda i: (0, 0))],
               out_shape=[SDS((T, D), BF16), SDS((nb * 8, D), F32), SDS((8, D), F32)], name=name)(fo, mod3, pg, dout)


def _pre_bwd(x, g, mod3, sc_seg, dh, dout, S, name):
    T, D = x.shape
    tm = _row_tile(S)
    tpb = S // tm
    nb = T // S

    def body(x_ref, g_ref, sc_ref, dh_ref, d_ref, dx_ref, dg_ref, dsc_ref, dsh_ref):
        i = pl.program_id(0)

        @pl.when(i == 0)
        def _():
            dg_ref[...] = jnp.zeros_like(dg_ref)

        @pl.when(i % tpb == 0)
        def _():
            dsc_ref[...] = jnp.zeros_like(dsc_ref)
            dsh_ref[...] = jnp.zeros_like(dsh_ref)

        x_, dh_ = x_ref[...], dh_ref[...]
        r = lax.rsqrt(jnp.mean(x_ * x_, axis=-1, keepdims=True) + EPS)
        n = x_ * r
        dn = dh_ * (g_ref[...] * (1.0 + sc_ref[...]))
        dx_ref[...] = d_ref[...] + r * (dn - n * jnp.mean(dn * n, axis=-1, keepdims=True))
        dhn = jnp.sum(dh_ * n, axis=0, keepdims=True)
        dg_ref[...] += _bsum(dhn * (1.0 + sc_ref[...]))
        dsc_ref[...] += _bsum(dhn * g_ref[...])
        dsh_ref[...] += _bsum(jnp.sum(dh_, axis=0, keepdims=True))

    row = pl.BlockSpec((tm, D), lambda i: (i, 0))
    return _pc(body, grid=(T // tm,),
               in_specs=[row, pl.BlockSpec((1, D), lambda i: (0, 0)),
                         pl.BlockSpec((None, 1, D), lambda i: (i // tpb, 0, sc_seg)), row, row],
               out_specs=[row, pl.BlockSpec((8, D), lambda i: (0, 0)), pl.BlockSpec((8, D), lambda i: (i // tpb, 0)),
                          pl.BlockSpec((8, D), lambda i: (i // tpb, 0))],
               out_shape=[SDS((T, D), F32), SDS((8, D), F32), SDS((nb * 8, D), F32), SDS((nb * 8, D), F32)],
               name=name)(x, g, mod3, dh, dout)


def _loss(y, target, S, name):
    T, D = y.shape
    tm = _row_tile(S)

    def body(y_ref, t_ref, dy_ref, l_ref):
        @pl.when(pl.program_id(0) == 0)
        def _():
            l_ref[...] = jnp.zeros_like(l_ref)

        def strip(r0, carry):
            rows = pl.ds(r0, STRIP)
            e = y_ref[rows, :] - t_ref[rows, :]
            dy_ref[rows, :] = e * (1.0 / D)
            return carry + _fold8(e * e)

        acc = _strips(tm, strip, jnp.zeros((8, D), F32))
        l_ref[...] += jnp.broadcast_to(jnp.sum(acc, keepdims=True) * (0.5 / D), l_ref.shape)

    row = pl.BlockSpec((tm, D), lambda i: (i, 0))
    return _pc(body, grid=(T // tm,), in_specs=[row, row],
               out_specs=[row, pl.BlockSpec((8, 128), lambda i: (0, 0))],
               out_shape=[SDS((T, D), F32), SDS((8, 128), F32)], name=name)(y, target)


def _conv_geom(view, C, S, cap=2048):
    arr, off = view
    T = arr.shape[0]
    tm = _strip_row_tile(S, cap)
    tc = _tile(C, 512)
    assert off % tc == 0 and C % tc == 0
    return arr, off // tc, T, tm, tc, S // tm


def _prev_spec(tm, tc, ob, order):
    if order == "ij":
        return pl.BlockSpec((8, tc), lambda i, j: (jnp.maximum(i * (tm // 8) - 1, 0), ob + j))
    return pl.BlockSpec((8, tc), lambda j, i: (jnp.maximum(i * (tm // 8) - 1, 0), ob + j))


def _next_spec(T, tm, tc, ob, order):
    last = T // 8 - 1
    if order == "ij":
        return pl.BlockSpec((8, tc), lambda i, j: (jnp.minimum((i + 1) * (tm // 8), last), ob + j))
    return pl.BlockSpec((8, tc), lambda j, i: (jnp.minimum((i + 1) * (tm // 8), last), ob + j))


def _taps(win, w_ref, K, lead, rows):
    acc = win[lead:lead + rows] * w_ref[K - 1:K, :]
    for j in range(1, K):
        acc = acc + win[lead - j:lead - j + rows] * w_ref[K - 1 - j:K - j, :]
    return acc


def _taps_t(win, w_ref, K, rows):
    acc = win[0:rows] * w_ref[K - 1:K, :]
    for j in range(1, K):
        acc = acc + win[j:j + rows] * w_ref[K - 1 - j:K - j, :]
    return acc


def _conv_fwd(view, C, w8, b, K, S, name):
    arr, ob, T, tm, tc, tps = _conv_geom(view, C, S)

    def body(u_ref, p_ref, w_ref, b_ref, o_ref):
        first = (pl.program_id(0) % tps) == 0

        def strip(r0, win, carry):
            o_ref[pl.ds(r0, STRIP), :] = _taps(win, w_ref, K, 8, STRIP) + b_ref[...]
            return carry

        _strips_prev(tm, STRIP, u_ref, jnp.where(first, 0.0, p_ref[...]), strip)

    return _pc(body, grid=(T // tm, C // tc),
               in_specs=[pl.BlockSpec((tm, tc), lambda i, j: (i, ob + j)), _prev_spec(tm, tc, ob, "ij"),
                         pl.BlockSpec((8, tc), lambda i, j: (0, j)), pl.BlockSpec((1, tc), lambda i, j: (0, j))],
               out_specs=pl.BlockSpec((tm, tc), lambda i, j: (i, j)), out_shape=SDS((T, C), F32), name=name)(
                   arr, arr, w8, b)


def _conv_bwd_in(dview, C, w8, K, S, out_dtype, name):
    arr, ob, T, tm, tc, tps = _conv_geom(dview, C, S)

    def body(d_ref, n_ref, w_ref, o_ref):
        last = (pl.program_id(0) % tps) == tps - 1

        def strip(r0, win, carry):
            o_ref[pl.ds(r0, STRIP), :] = _taps_t(win, w_ref, K, STRIP).astype(o_ref.dtype)
            return carry

        _strips_next(tm, STRIP, d_ref, jnp.where(last, 0.0, n_ref[...]), strip)

    return _pc(body, grid=(T // tm, C // tc),
               in_specs=[pl.BlockSpec((tm, tc), lambda i, j: (i, ob + j)), _next_spec(T, tm, tc, ob, "ij"),
                         pl.BlockSpec((8, tc), lambda i, j: (0, j))],
               out_specs=pl.BlockSpec((tm, tc), lambda i, j: (i, j)), out_shape=SDS((T, C), out_dtype), name=name)(
                   arr, arr, w8)


def _conv_bwd_w(dview, uview, C, K, S, name):
    darr, dob, T, tm, tc, tps = _conv_geom(dview, C, S)
    uarr, uob, _, _, _, _ = _conv_geom(uview, C, S)

    def body(d_ref, u_ref, p_ref, o_ref):
        i = pl.program_id(1)

        @pl.when(i == 0)
        def _():
            o_ref[...] = jnp.zeros_like(o_ref)

        first = (i % tps) == 0

        def strip(r0, win, carry):
            d = d_ref[pl.ds(r0, STRIP), :]
            sums = [_fold8(d * win[8 - (K - 1 - k):8 - (K - 1 - k) + STRIP]) for k in range(K)] + [_fold8(d)]
            return tuple(c + s for c, s in zip(carry, sums))

        acc = _strips_prev(tm, STRIP, u_ref, jnp.where(first, 0.0, p_ref[...]), strip,
                           tuple(jnp.zeros((8, tc), F32) for _ in range(K + 1)))
        o_ref[...] += _rows8([jnp.sum(a, axis=0, keepdims=True) for a in acc])

    return _pc(body, grid=(C // tc, T // tm),
               in_specs=[pl.BlockSpec((tm, tc), lambda j, i: (i, dob + j)),
                         pl.BlockSpec((tm, tc), lambda j, i: (i, uob + j)), _prev_spec(tm, tc, uob, "ji")],
               out_specs=pl.BlockSpec((8, tc), lambda j, i: (0, j)), out_shape=SDS((8, C), F32), name=name)(
                   darr, uarr, uarr)


def _ffn_act_fwd(uu, w8, b, S, name):
    K, gw = FFN_CONV_K, GLU_W
    T, F2 = uu.shape
    tm, tc = _strip_row_tile(S), 2 * GLU_W
    tps = S // tm

    def body(u_ref, p_ref, w_ref, b_ref, a_ref):
        first = (pl.program_id(0) % tps) == 0

        def strip(r0, win, carry):
            u = _taps(win, w_ref, K, 8, STRIP) + b_ref[...]
            a_ref[pl.ds(r0, STRIP), :] = (_silu(u[:, :gw]) * u[:, gw:]).astype(BF16)
            return carry

        _strips_prev(tm, STRIP, u_ref, jnp.where(first, 0.0, p_ref[...]), strip)

    return _pc(body, grid=(T // tm, F2 // tc),
               in_specs=[pl.BlockSpec((tm, tc), lambda i, j: (i, j)), _prev_spec(tm, tc, 0, "ij"),
                         pl.BlockSpec((8, tc), lambda i, j: (0, j)), pl.BlockSpec((1, tc), lambda i, j: (0, j))],
               out_specs=pl.BlockSpec((tm, gw), lambda i, j: (i, j)), out_shape=SDS((T, F2 // 2), BF16), name=name)(
                   uu, uu, w8, b)


def _ffn_act_bwd(uu, da, w8, b, S, name, side=None):
    K, gw = FFN_CONV_K, GLU_W
    T, F2 = uu.shape
    tm, tc = _strip_row_tile(S), 2 * GLU_W
    tps = S // tm
    last16 = T // 16 - 1

    def body(u_ref, p_ref, n_ref, da_ref, dan_ref, w_ref, b_ref, duu_ref, cw_ref, dabuf):
        i = pl.program_id(1)

        @pl.when(i == 0)
        def _():
            cw_ref[...] = jnp.zeros_like(cw_ref)

        first = (i % tps) == 0
        last = (i % tps) == tps - 1
        dabuf[0:tm, :] = da_ref[...].astype(F32)
        dabuf[tm:tm + 8, :] = jnp.where(last, 0.0, dan_ref[...].astype(F32)[0:8, :])
        fs, ext = FFN_STRIP, FFN_STRIP + 8

        def strip(r0, win, carry):
            shifted = [win[8 - j:8 - j + ext] for j in range(K)]
            u = b_ref[...] + shifted[0] * w_ref[K - 1:K, :]
            for j in range(1, K):
                u = u + shifted[j] * w_ref[K - 1 - j:K - j, :]
            da_ = dabuf[pl.ds(r0, ext), :]
            g, v = u[:, :gw], u[:, gw:]
            du = jnp.concatenate([da_ * v * _dsilu(g), da_ * _silu(g)], axis=1)
            duu_ref[pl.ds(r0, FFN_STRIP), :] = _taps_t(du, w_ref, K, FFN_STRIP).astype(BF16)
            dmain = du[0:FFN_STRIP]
            sums = [_fold8(dmain * shifted[K - 1 - k][0:FFN_STRIP]) for k in range(K)] + [_fold8(dmain)]
            return tuple(c + s for c, s in zip(carry, sums))

        acc = strip(0, jnp.concatenate([jnp.where(first, 0.0, p_ref[...]), u_ref[0:ext, :]], axis=0),
                    tuple(jnp.zeros((8, tc), F32) for _ in range(K + 1)))

        def step(r, c):
            r0 = pl.multiple_of(r * fs, fs)
            return strip(r0, u_ref[pl.ds(pl.multiple_of(r0 - 8, 8), fs + 16), :], c)

        acc = lax.fori_loop(1, tm // fs - 1, step, acc)
        acc = strip(tm - fs, jnp.concatenate([u_ref[tm - ext:tm, :], n_ref[...]], axis=0), acc)
        cw_ref[...] += _rows8([jnp.sum(a, axis=0, keepdims=True) for a in acc])

    return _pc(body, grid=(F2 // tc, T // tm),
               in_specs=[pl.BlockSpec((tm, tc), lambda j, i: (i, j)), _prev_spec(tm, tc, 0, "ji"),
                         _next_spec(T, tm, tc, 0, "ji"), pl.BlockSpec((tm, gw), lambda j, i: (i, j)),
                         pl.BlockSpec((16, gw), lambda j, i: (jnp.minimum((i + 1) * (tm // 16), last16), j)),
                         pl.BlockSpec((8, tc), lambda j, i: (0, j)), pl.BlockSpec((1, tc), lambda j, i: (0, j))],
               out_specs=[pl.BlockSpec((tm, tc), lambda j, i: (i, j)), pl.BlockSpec((8, tc), lambda j, i: (0, j))],
               out_shape=[SDS((T, F2), BF16), SDS((8, F2), F32)], name=name,
               scratch=(pltpu.VMEM((tm + 8, gw), F32),), side=side)(
                   uu, uu, uu, da, da, w8, b)


def _ssd_common(dtc_raw, dtr_raw, hpc, hpr, L):
    dt_c = _softplus(dtc_raw + hpc[0:1, :])
    a_c = -jnp.exp(hpc[1:2, :])
    dt_r = _softplus(dtr_raw + hpr[:, 0:1])
    a_r = -jnp.exp(hpr[:, 1:2])
    li = lax.broadcasted_iota(jnp.int32, (L, L), 0)
    si = lax.broadcasted_iota(jnp.int32, (L, L), 1)
    low = li >= si
    upp = li <= si
    acs_c = _dotx(low, dt_c * a_c, split="b")
    acs_r = _dotx(dt_r * a_r, upp)
    return dt_c, a_c, acs_c, acs_r, low, upp


def _dotx(a, b, split="a", parts=3, dims=NN):
    val, one = (a, b) if split == "a" else (b, a)
    one = one.astype(BF16)
    acc, rem = None, val
    for i in range(parts):
        piece = rem.astype(BF16)
        t = _dot(piece, one, dims) if split == "a" else _dot(one, piece, dims)
        acc = t if acc is None else acc + t
        if i + 1 < parts:
            rem = rem - piece.astype(F32)
    return acc


def _head_maps(R, P, L):
    RP = R * P
    sel = (lax.broadcasted_iota(jnp.int32, (RP, R), 0) // P == lax.broadcasted_iota(jnp.int32, (RP, R), 1)).astype(F32)
    selt = (lax.broadcasted_iota(jnp.int32, (R, RP), 1) // P == lax.broadcasted_iota(jnp.int32, (R, RP), 0)).astype(F32)
    colb = (lax.broadcasted_iota(jnp.int32, (R, R * L), 1) // L == lax.broadcasted_iota(jnp.int32, (R, R * L), 0)).astype(F32)
    return sel, selt, colb


def _pair_diag(mats, rhs_b, R, P):
    lanes = 2 * P
    lo = lax.broadcasted_iota(jnp.int32, (mats[0].shape[0], lanes), 1) < P
    out = []
    for q in range(R // 2):
        rp = rhs_b[:, q * lanes:(q + 1) * lanes]
        out.append(jnp.where(lo, _dot(mats[2 * q], rp), _dot(mats[2 * q + 1], rp)))
    return jnp.concatenate(out, axis=1) if len(out) > 1 else out[0]


def _ssd_specs(pre, off_x, off_b, off_c, G, R, P, nb, nc, rev):
    N, RP, W = N_STATE, R * P, SSD_STEP_CHUNKS * CHUNK
    ns = nc // SSD_STEP_CHUNKS
    assert nc % SSD_STEP_CHUNKS == 0
    cidx = (lambda c: ns - 1 - c) if rev else (lambda c: c)
    xb, bb, cb = off_x // RP, off_b // N, off_c // N
    assert off_x % RP == 0 and off_b % N == 0 and off_c % N == 0
    row = lambda b, c: b * ns + cidx(c)
    return dict(
        x=pl.BlockSpec((W, RP), lambda g, b, c: (row(b, c), xb + g)),
        b=pl.BlockSpec((W, N), lambda g, b, c: (row(b, c), bb + g)),
        c=pl.BlockSpec((W, N), lambda g, b, c: (row(b, c), cb + g)),
        dtc=pl.BlockSpec((None, W, R), lambda g, b, c: (g, row(b, c), 0)),
        dtr=pl.BlockSpec((None, R, W), lambda g, b, c: (g, 0, row(b, c))),
        hpc=pl.BlockSpec((None, 8, R), lambda g, b, c: (g, 0, 0)),
        hpr=pl.BlockSpec((None, R, 8), lambda g, b, c: (g, 0, 0)),
        y=pl.BlockSpec((W, RP), lambda g, b, c: (row(b, c), g)),
        bc=pl.BlockSpec((W, N), lambda g, b, c: (row(b, c), g)),
        hs=pl.BlockSpec((SSD_STEP_CHUNKS, None, N, RP), lambda g, b, c: (row(b, c), g, 0, 0)),
    )


def _ssd_fwd(pre, offs, dtc, dtr, hpc, hpr, G, R, P, S, name, side=None):
    T = pre.shape[0]
    L, N, RP = CHUNK, N_STATE, R * P
    nc, nb = S // L, T // S
    sp = _ssd_specs(pre, *offs, G, R, P, nb, nc, False)

    def body(px_ref, pb_ref, pc_ref, dtc_ref, dtr_ref, hpc_ref, hpr_ref, y_ref, hs_ref, hst):
        @pl.when(pl.program_id(2) == 0)
        def _():
            hst[...] = jnp.zeros_like(hst)

        hpc_ = hpc_ref[...]
        _, selt, colb = _head_maps(R, P, L)
        hp_e = _dotx(hpc_, selt)
        for k in range(SSD_STEP_CHUNKS):
            rs = slice(k * L, (k + 1) * L)
            xs, bm, cm = _silu(px_ref[rs, :]), _silu(pb_ref[rs, :]), _silu(pc_ref[rs, :])
            dt_c, _, acs_c, acs_r, low, _ = _ssd_common(dtc_ref[rs, :], dtr_ref[:, rs], hpc_, hpr_ref[...], L)
            dt_e, a_e = _dotx(dt_c, selt), _dotx(acs_c, selt)
            a_bc = _dotx(acs_c, colb)
            a_last = a_e[L - 1:L, :]
            bb, cb = bm.astype(BF16), cm.astype(BF16)
            gm = _dot(cb, bb, NT)
            hprev = hst[...]
            hprev_b = hprev.astype(BF16)
            hs_ref[k] = hprev_b
            xdt = xs * dt_e
            xdt_b = xdt.astype(BF16)
            ms = []
            for r in range(R):
                dec = jnp.exp(jnp.where(low, a_bc[:, r * L:(r + 1) * L] - acs_r[r:r + 1, :], -jnp.inf))
                ms.append((gm * dec).astype(BF16))
            y_ref[rs, :] = _pair_diag(ms, xdt_b, R, P) + _dot(cb, hprev_b) * jnp.exp(a_e) + hp_e[2:3, :] * xs
            xw = (xdt * jnp.exp(a_last - a_e)).astype(BF16)
            hst[...] = hprev * jnp.exp(a_last) + _dot(bb, xw, TN)

    return _pc(body, grid=(G, nb, nc // SSD_STEP_CHUNKS),
               in_specs=[sp["x"], sp["b"], sp["c"], sp["dtc"], sp["dtr"], sp["hpc"], sp["hpr"]],
               out_specs=[sp["y"], sp["hs"]],
               out_shape=[SDS((T, G * RP), F32), SDS((nb * nc, G, N, RP), BF16)], name=name,
               scratch=(pltpu.VMEM((N, RP), F32),), side=side)(pre, pre, pre, dtc, dtr, hpc, hpr)


def _ssd_bwd(pre, offs, dtc, dtr, hpc, hpr, hs, dy, G, R, P, S, name, side=None):
    T = pre.shape[0]
    L, N, RP = CHUNK, N_STATE, R * P
    nc, nb = S // L, T // S
    sp = _ssd_specs(pre, *offs, G, R, P, nb, nc, True)

    def chunk(px_ref, pb_ref, pc_ref, dtc_ref, dtr_ref, hpc_ref, hpr_ref, hs_ref, dy_ref,
              dpx_ref, dpb_ref, dpc_ref, ddt_ref, hpg_ref, dhst):
        px, pb, pcc = px_ref[...], pb_ref[...], pc_ref[...]
        xs, bm, cm = _silu(px), _silu(pb), _silu(pcc)
        hpc_ = hpc_ref[...]
        dtc_raw = dtc_ref[...]
        dt_c, a_c, acs_c, acs_r, low, upp = _ssd_common(dtc_raw, dtr_ref[...], hpc_, hpr_ref[...], L)
        sel, selt, colb = _head_maps(R, P, L)
        dt_e, a_e, hp_e = _dotx(dt_c, selt), _dotx(acs_c, selt), _dotx(hpc_, selt)
        a_bc = _dotx(acs_c, colb)
        a_last = a_e[L - 1:L, :]
        e_e, w_e = jnp.exp(a_e), jnp.exp(a_last - a_e)
        bb, cb = bm.astype(BF16), cm.astype(BF16)
        gm = _dot(cb, bb, NT)
        gmt = _dot(bb, cb, NT)
        hprev = hs_ref[...]
        dhn = dhst[...]
        dhn_b = dhn.astype(BF16)
        dy = dy_ref[...]
        dy_b = dy.astype(BF16)
        xdt = xs * dt_e
        xdt_b = xdt.astype(BF16)
        yoff = _dot(cb, hprev) * e_e
        dye_b = (dy * e_e).astype(BF16)
        dcm = _dot(dye_b, hprev, NT)
        dhst[...] = _dot(cb, dye_b, TN) + jnp.exp(a_last) * dhn
        dxdt_st = _dot(bb, dhn_b) * w_e
        dbm = _dot((xdt * w_e).astype(BF16), dhn_b, NT)
        lanes = 2 * P
        lo = lax.broadcasted_iota(jnp.int32, (L, lanes), 1) < P
        dg = jnp.zeros((L, L), F32)
        es, css = [], []
        for r in range(R):
            col_b, row = a_bc[:, r * L:(r + 1) * L], acs_r[r:r + 1, :]
            dec = jnp.exp(jnp.where(low, col_b - row, -jnp.inf))
            q = r // 2
            dyp = dy_b[:, q * lanes:(q + 1) * lanes]
            dyp = jnp.where(lo if r % 2 == 0 else ~lo, dyp, jnp.zeros_like(dyp))
            dm = _dot(dyp, xdt_b[:, q * lanes:(q + 1) * lanes], NT)
            dg = dg + dm * dec
            e = dm * (gm * dec)
            es.append(e)
            css.append(jnp.sum(e, axis=0, keepdims=True))
        dgb = dg.astype(BF16)
        dcm = dcm + _dot(dgb, bb)
        dbm = dbm + _dot(dgb, cb, TN)
        colbt = (lax.broadcasted_iota(jnp.int32, (R * L, R), 0) // L
                 == lax.broadcasted_iota(jnp.int32, (R * L, R), 1)).astype(F32)
        eye = (lax.broadcasted_iota(jnp.int32, (R, R), 0) == lax.broadcasted_iota(jnp.int32, (R, R), 1)).astype(F32)
        row_sums = _dotx(jnp.concatenate(es, axis=1), colbt)
        col_sums = _dotx(jnp.concatenate(css, axis=0), eye, dims=TN)
        mts = []
        for r in range(R):
            dect = jnp.exp(jnp.where(upp, acs_r[r:r + 1, :] - a_bc[:, r * L:(r + 1) * L], -jnp.inf))
            mts.append((gmt * dect).astype(BF16))
        dxdt = _pair_diag(mts, dy_b, R, P) + dxdt_st
        q_st = _dotx(xdt * dxdt_st, sel, parts=1)
        da = row_sums - col_sums + _dotx(dy * yoff, sel, parts=1) - q_st
        hh = jnp.sum(_dotx(dhn * hprev.astype(F32), sel, parts=1), axis=0, keepdims=True)
        da_last = jnp.exp(acs_c[L - 1:L, :]) * hh + jnp.sum(q_st, axis=0, keepdims=True)
        rowi = lax.broadcasted_iota(jnp.int32, (L, R), 0)
        da = da + jnp.where(rowi == L - 1, da_last, 0.0)
        dpx_ref[...] = (dxdt * dt_e + hp_e[2:3, :] * dy) * _dsilu(px)
        dpb_ref[...] = dbm * _dsilu(pb)
        dpc_ref[...] = dcm * _dsilu(pcc)
        dadt = _dotx(upp, da, split="b")
        ddt = _dotx(dxdt * xs, sel, parts=1) + dadt * a_c
        ddt_raw = ddt * jax.nn.sigmoid(dtc_raw + hpc_[0:1, :])
        ddt_ref[...] = ddt_raw
        d_a = jnp.sum(dadt * dt_c, axis=0, keepdims=True)
        d_d = jnp.sum(_dotx(dy * xs, sel, parts=1), axis=0, keepdims=True)
        rows = [jnp.sum(ddt_raw, axis=0, keepdims=True), d_a * a_c, d_d, jnp.zeros((5, R), F32)]
        hpg_ref[...] += jnp.concatenate(rows, axis=0)

    def body(px_ref, pb_ref, pc_ref, dtc_ref, dtr_ref, hpc_ref, hpr_ref, hs_ref, dy_ref,
             dpx_ref, dpb_ref, dpc_ref, ddt_ref, hpg_ref, dhst):
        bi, ci = pl.program_id(1), pl.program_id(2)

        @pl.when(ci == 0)
        def _():
            dhst[...] = jnp.zeros_like(dhst)

        @pl.when((bi == 0) & (ci == 0))
        def _():
            hpg_ref[...] = jnp.zeros_like(hpg_ref)

        for k in reversed(range(SSD_STEP_CHUNKS)):
            rs = pl.ds(k * L, L)
            chunk(px_ref.at[rs, :], pb_ref.at[rs, :], pc_ref.at[rs, :], dtc_ref.at[rs, :], dtr_ref.at[:, rs], hpc_ref,
                  hpr_ref, hs_ref.at[k], dy_ref.at[rs, :], dpx_ref.at[rs, :], dpb_ref.at[rs, :], dpc_ref.at[rs, :],
                  ddt_ref.at[rs, :], hpg_ref, dhst)

    return _pc(body, grid=(G, nb, nc // SSD_STEP_CHUNKS),
               in_specs=[sp["x"], sp["b"], sp["c"], sp["dtc"], sp["dtr"], sp["hpc"], sp["hpr"], sp["hs"], sp["y"]],
               out_specs=[sp["y"], sp["bc"], sp["bc"], sp["dtc"], pl.BlockSpec((None, 8, R), lambda g, b, c: (g, 0, 0))],
               out_shape=[SDS((T, G * RP), F32), SDS((T, G * N), F32), SDS((T, G * N), F32), SDS((G, T, R), F32),
                          SDS((G, 8, R), F32)], name=name,
               scratch=(pltpu.VMEM((N, RP), F32),), side=side)(pre, pre, pre, dtc, dtr, hpc, hpr, hs, dy)


def _gate_norm_fwd(y, zview, ng, G, S, name):
    T, DI = y.shape
    zarr, zoff = zview
    gw = DI // G
    tm = _tile(S, 1024, 8)
    zb = zoff // gw
    assert zoff % gw == 0

    def body(y_ref, z_ref, g_ref, o_ref):
        yg = y_ref[...] * _silu(z_ref[...])
        r = lax.rsqrt(jnp.mean(yg * yg, axis=-1, keepdims=True) + EPS)
        o_ref[...] = (yg * r * g_ref[...]).astype(BF16)

    return _pc(body, grid=(T // tm, G),
               in_specs=[pl.BlockSpec((tm, gw), lambda i, g: (i, g)), pl.BlockSpec((tm, gw), lambda i, g: (i, zb + g)),
                         pl.BlockSpec((1, gw), lambda i, g: (0, g))],
               out_specs=pl.BlockSpec((tm, gw), lambda i, g: (i, g)), out_shape=SDS((T, DI), BF16), name=name)(y, zarr, ng)


def _gate_norm_bwd(y, zview, ng, dyn, G, S, name):
    T, DI = y.shape
    zarr, zoff = zview
    gw = DI // G
    tm = _tile(S, 1024, 8)
    zb = zoff // gw

    def body(y_ref, z_ref, g_ref, d_ref, dy_ref, dz_ref, dg_ref):
        @pl.when(pl.program_id(1) == 0)
        def _():
            dg_ref[...] = jnp.zeros_like(dg_ref)

        y_, z, d = y_ref[...], z_ref[...], d_ref[...]
        sz = _silu(z)
        yg = y_ * sz
        r = lax.rsqrt(jnp.mean(yg * yg, axis=-1, keepdims=True) + EPS)
        n = yg * r
        dn = d * g_ref[...]
        dyg = r * (dn - n * jnp.mean(dn * n, axis=-1, keepdims=True))
        dy_ref[...] = dyg * sz
        dz_ref[...] = (dyg * y_ * _dsilu(z)).astype(BF16)
        dg_ref[...] += _bsum(jnp.sum(d * n, axis=0, keepdims=True))

    return _pc(body, grid=(G, T // tm),
               in_specs=[pl.BlockSpec((tm, gw), lambda g, i: (i, g)), pl.BlockSpec((tm, gw), lambda g, i: (i, zb + g)),
                         pl.BlockSpec((1, gw), lambda g, i: (0, g)), pl.BlockSpec((tm, gw), lambda g, i: (i, g))],
               out_specs=[pl.BlockSpec((tm, gw), lambda g, i: (i, g)), pl.BlockSpec((tm, gw), lambda g, i: (i, g)),
                          pl.BlockSpec((8, gw), lambda g, i: (0, g))],
               out_shape=[SDS((T, DI), F32), SDS((T, DI), BF16), SDS((8, DI), F32)], name=name)(y, zarr, ng, dyn)


def _shortconv_fwd(proj, off_b, off_c, off_h, C, w8, S, name):
    K = SC_CONV_K
    _, ob, T, tm, tc, tps = _conv_geom((proj, off_b), C, S, cap=1024)
    oc, oh = off_c // tc, off_h // tc

    def body(b_ref, c_ref, h_ref, cp_ref, hp_ref, w_ref, o_ref, buf):
        first = (pl.program_id(0) % tps) == 0
        buf[0:8, :] = jnp.where(first, 0.0, cp_ref[...] * hp_ref[...])
        buf[8:, :] = c_ref[...] * h_ref[...]

        def strip(r0, carry):
            conv = _taps(buf[pl.ds(r0, STRIP + 8), :], w_ref, K, 8, STRIP)
            o_ref[pl.ds(r0, STRIP), :] = (b_ref[pl.ds(r0, STRIP), :] * conv).astype(BF16)
            return carry

        _strips(tm, strip)

    blk = lambda o: pl.BlockSpec((tm, tc), lambda i, j: (i, o + j))
    return _pc(body, grid=(T // tm, C // tc),
               in_specs=[blk(ob), blk(oc), blk(oh), _prev_spec(tm, tc, oc, "ij"), _prev_spec(tm, tc, oh, "ij"),
                         pl.BlockSpec((8, tc), lambda i, j: (0, j))],
               out_specs=pl.BlockSpec((tm, tc), lambda i, j: (i, j)), out_shape=SDS((T, C), BF16), name=name,
               scratch=(pltpu.VMEM((tm + 8, tc), F32),))(proj, proj, proj, proj, proj, w8)


def _shortconv_bwd(proj, off_b, off_c, off_h, C, w8, dsc, S, name):
    K = SC_CONV_K
    _, ob, T, tm, tc, tps = _conv_geom((proj, off_b), C, S, cap=1024)
    oc, oh = off_c // tc, off_h // tc

    def body(b_ref, c_ref, h_ref, cp_ref, hp_ref, bn_ref, d_ref, dn_ref, w_ref,
             db_ref, dc_ref, dh_ref, dw_ref, buf, buf2):
        i = pl.program_id(1)

        @pl.when(i == 0)
        def _():
            dw_ref[...] = jnp.zeros_like(dw_ref)

        first = (i % tps) == 0
        last = (i % tps) == tps - 1
        buf[0:8, :] = jnp.where(first, 0.0, cp_ref[...] * hp_ref[...])
        buf[8:, :] = c_ref[...] * h_ref[...]
        buf2[0:tm, :] = d_ref[...] * b_ref[...]
        buf2[tm:tm + 8, :] = jnp.where(last, 0.0, dn_ref[...] * bn_ref[...])

        def strip(r0, carry):
            rows = pl.ds(r0, STRIP)
            vwin = buf[pl.ds(r0, STRIP + 8), :]
            vs = [vwin[8 - j:8 - j + STRIP] for j in range(K)]
            conv = vs[0] * w_ref[K - 1:K, :]
            for j in range(1, K):
                conv = conv + vs[j] * w_ref[K - 1 - j:K - j, :]
            db_ref[rows, :] = (d_ref[rows, :] * conv).astype(BF16)
            dwin = buf2[pl.ds(r0, STRIP + 8), :]
            dv = _taps_t(dwin, w_ref, K, STRIP)
            dc_ref[rows, :] = (dv * h_ref[rows, :]).astype(BF16)
            dh_ref[rows, :] = (dv * c_ref[rows, :]).astype(BF16)
            dconv = dwin[0:STRIP]
            sums = [_fold8(dconv * vs[K - 1 - k]) for k in range(K)]
            return tuple(c + s for c, s in zip(carry, sums))

        acc = _strips(tm, strip, tuple(jnp.zeros((8, tc), F32) for _ in range(K)))
        dw_ref[...] += _rows8([jnp.sum(a, axis=0, keepdims=True) for a in acc])

    blk = lambda o: pl.BlockSpec((tm, tc), lambda j, i: (i, o + j))
    out = pl.BlockSpec((tm, tc), lambda j, i: (i, j))
    return _pc(body, grid=(C // tc, T // tm),
               in_specs=[blk(ob), blk(oc), blk(oh), _prev_spec(tm, tc, oc, "ji"), _prev_spec(tm, tc, oh, "ji"),
                         _next_spec(T, tm, tc, ob, "ji"), blk(0), _next_spec(T, tm, tc, 0, "ji"),
                         pl.BlockSpec((8, tc), lambda j, i: (0, j))],
               out_specs=[out, out, out, pl.BlockSpec((8, tc), lambda j, i: (0, j))],
               out_shape=[SDS((T, C), BF16)] * 3 + [SDS((8, C), F32)], name=name,
               scratch=(pltpu.VMEM((tm + 8, tc), F32), pltpu.VMEM((tm + 8, tc), F32)))(
                   proj, proj, proj, proj, proj, proj, dsc, dsc, w8)


def _merge_fwd(proj, off_g1, off_g2, y1, y2, S, name):
    T, D = y1.shape
    tm = _row_tile(S)
    o1, o2 = off_g1 // D, off_g2 // D
    assert off_g1 % D == 0 and off_g2 % D == 0

    def body(g1_ref, g2_ref, y1_ref, y2_ref, o_ref):
        def strip(r0, carry):
            rows = pl.ds(r0, STRIP)
            o_ref[rows, :] = (jax.nn.sigmoid(g1_ref[rows, :]) * y1_ref[rows, :]
                              + jax.nn.sigmoid(g2_ref[rows, :]) * y2_ref[rows, :]).astype(BF16)
            return carry

        _strips(tm, strip)

    row = pl.BlockSpec((tm, D), lambda i: (i, 0))
    return _pc(body, grid=(T // tm,),
               in_specs=[pl.BlockSpec((tm, D), lambda i: (i, o1)), pl.BlockSpec((tm, D), lambda i: (i, o2)), row, row],
               out_specs=row, out_shape=SDS((T, D), BF16), name=name)(proj, proj, y1, y2)


def _merge_bwd(proj, off_g1, off_g2, y1, y2, dm, S, name):
    T, D = y1.shape
    tm = _row_tile(S)
    o1, o2 = off_g1 // D, off_g2 // D

    def body(g1_ref, g2_ref, y1_ref, y2_ref, d_ref, dy1_ref, dy2_ref, dg1_ref, dg2_ref):
        def strip(r0, carry):
            rows = pl.ds(r0, STRIP)
            d = d_ref[rows, :]
            s1, s2 = jax.nn.sigmoid(g1_ref[rows, :]), jax.nn.sigmoid(g2_ref[rows, :])
            dy1_ref[rows, :] = (d * s1).astype(BF16)
            dy2_ref[rows, :] = (d * s2).astype(BF16)
            dg1_ref[rows, :] = (d * y1_ref[rows, :] * s1 * (1.0 - s1)).astype(BF16)
            dg2_ref[rows, :] = (d * y2_ref[rows, :] * s2 * (1.0 - s2)).astype(BF16)
            return carry

        _strips(tm, strip)

    row = pl.BlockSpec((tm, D), lambda i: (i, 0))
    return _pc(body, grid=(T // tm,),
               in_specs=[pl.BlockSpec((tm, D), lambda i: (i, o1)), pl.BlockSpec((tm, D), lambda i: (i, o2)), row, row, row],
               out_specs=[row] * 4, out_shape=[SDS((T, D), BF16)] * 4, name=name)(proj, proj, y1, y2, dm)


def _pad8(w):
    return jnp.pad(w, ((0, 8 - w.shape[0]), (0, 0)))


def _dims(w):
    D = w["mix_pre_g"].shape[-1]
    DI = w["ssd_norm_g"].shape[-1]
    H = w["ssd_dt_bias"].shape[-1]
    conv_dim = w["ssd_conv_b"].shape[-1]
    G = (conv_dim - DI) // (2 * N_STATE)
    F = w["w_down"].shape[0]
    return dict(D=D, DI=DI, H=H, P=DI // H, G=G, R=H // G, GN=G * N_STATE, CD=conv_dim, F=F)


def _proj_layout(d):
    D, DI, CD, H = d["D"], d["DI"], d["CD"], d["H"]
    o = dict(z=0, xbc=DI, scb=DI + CD, scc=DI + CD + D, sch=DI + CD + 2 * D, g1=DI + CD + 3 * D, g2=DI + CD + 4 * D,
             dt=DI + CD + 5 * D)
    o["sb"] = math.gcd(SEG_BLK, D, DI, d["GN"])
    assert o["sb"] % 128 == 0 and H <= o["sb"]
    o["np"] = o["dt"] + o["sb"]
    return o


def _glu_perm(a, F, inverse=False):
    lead = a.shape[:-1]
    nb = F // GLU_W
    if not inverse:
        return a.reshape(*lead, 2, nb, GLU_W).swapaxes(-3, -2).reshape(*lead, 2 * F)
    return a.reshape(*lead, nb, 2, GLU_W).swapaxes(-3, -2).reshape(*lead, 2 * F)


def _glu_perm_rows(a, F, inverse=False):
    nb, D = F // GLU_W, a.shape[1]
    shape = (nb, 2, GLU_W, D) if inverse else (2, nb, GLU_W, D)
    return a.reshape(shape).swapaxes(0, 1).reshape(2 * F, D)


def _prep_layer(w):
    d = _dims(w)
    D, DI, CD, H, G, R, F = d["D"], d["DI"], d["CD"], d["H"], d["G"], d["R"], d["F"]
    lay = _proj_layout(d)
    w_in = w["w_in"]
    used = lay["dt"] + H
    wcat = jnp.concatenate([w_in[:DI + CD], w_in[DI + CD + H:], w_in[DI + CD:DI + CD + H],
                            jnp.zeros((lay["np"] - used, D), w_in.dtype)], axis=0)
    hp = jnp.stack([w["ssd_dt_bias"], w["ssd_a_log"], w["ssd_d"]], 0).astype(F32)
    hpc = jnp.pad(hp.reshape(3, G, R).transpose(1, 0, 2), ((0, 0), (0, 5), (0, 0)))
    hpr = jnp.pad(hp[:2].reshape(2, G, R).transpose(1, 2, 0), ((0, 0), (0, 0), (0, 6)))
    row = lambda v: v.reshape(1, -1).astype(F32)
    return dict(
        d=d, lay=lay, ada_w=w["ada_w"].astype(BF16), ada_b=row(w["ada_b"]),
        mix_pre_g=row(w["mix_pre_g"]), mix_post_g=row(w["mix_post_g"]), wcat=wcat.astype(BF16),
        ssd_conv_w=_pad8(w["ssd_conv_w"].astype(F32)), ssd_conv_b=row(w["ssd_conv_b"]), hpc=hpc, hpr=hpr,
        ssd_norm_g=row(w["ssd_norm_g"]), w_ssd_out=w["w_ssd_out"].astype(BF16),
        sc_conv_w=_pad8(w["sc_conv_w"].astype(F32)), w_sc_out=w["w_sc_out"].astype(BF16), w_o=w["w_o"].astype(BF16),
        ffn_pre_g=row(w["ffn_pre_g"]), ffn_post_g=row(w["ffn_post_g"]),
        w_up=_glu_perm_rows(w["w_up"], F).astype(BF16), ffn_conv_w=_pad8(_glu_perm(w["ffn_conv_w"].astype(F32), F)),
        ffn_conv_b=_glu_perm(row(w["ffn_conv_b"]), F), w_down=w["w_down"].astype(BF16))


def _dt_layouts(proj, lay, d):
    T = proj.shape[0]
    dt = proj[:, lay["dt"]:lay["dt"] + d["H"]].reshape(T, d["G"], d["R"])
    return dt.transpose(1, 0, 2), dt.transpose(1, 2, 0)


def _layer_fwd(x, c8, p, S, li, gather=None):
    d, lay = p["d"], p["lay"]
    D, DI, G, R, P, GN, CD = d["D"], d["DI"], d["G"], d["R"], d["P"], d["GN"], d["CD"]
    nb = x.shape[0] // S
    nm = lambda s: f"l{li}_{s}"
    mod, cact = _modk(c8, p["ada_w"], p["ada_b"], nm("mod"))
    mod3 = mod[:nb].reshape(nb, 1, 6 * D)
    h = _norm_mod(x, p["mix_pre_g"], mod3, 1, 0, S, nm("norm1"))
    proj = _mm(h, p["wcat"], "nt", F32, nm("mm_in"), caps=(1024, 1536, 2048))
    pre = _conv_fwd((proj, lay["xbc"]), CD, p["ssd_conv_w"], p["ssd_conv_b"], SSD_CONV_K, S, nm("ssdconv"))
    dtc, dtr = _dt_layouts(proj, lay, d)
    offs = (0, DI, DI + GN)
    gathered = None
    if gather is None:
        y, hs = _ssd_fwd(pre, offs, dtc, dtr, p["hpc"], p["hpr"], G, R, P, S, nm("ssd"))
    else:
        (y, hs), (gathered,) = _ssd_fwd(pre, offs, dtc, dtr, p["hpc"], p["hpr"], G, R, P, S, nm("ssd"),
                                        side=_side_gather_direct(gather))
    yn = _gate_norm_fwd(y, (proj, lay["z"]), p["ssd_norm_g"], G, S, nm("gnorm"))
    sc = _shortconv_fwd(proj, lay["scb"], lay["scc"], lay["sch"], D, p["sc_conv_w"], S, nm("sconv"))
    if gather is None:
        y_ssd = _mm(yn, p["w_ssd_out"], "nn", F32, nm("mm_ssdout"))
    else:
        y_ssd, (gathered,) = _mm(yn, p["w_ssd_out"], "nn", F32, nm("mm_ssdout"), side=_side_gather_forward(gathered))
    y_sc = _mm(sc, p["w_sc_out"], "nn", F32, nm("mm_scout"))
    m = _merge_fwd(proj, lay["g1"], lay["g2"], y_ssd, y_sc, S, nm("merge"))
    mix = _mm(m, p["w_o"], "nn", F32, nm("mm_o"))
    x1 = _resid_post(x, mix, mod3, 2, p["mix_post_g"], S, nm("post1"))
    h2 = _norm_mod(x1, p["ffn_pre_g"], mod3, 4, 3, S, nm("norm2"))
    uu = _mm(h2, p["w_up"], "nt", F32, nm("mm_up"), caps=(1024, 1408, 2048))
    a = _ffn_act_fwd(uu, p["ffn_conv_w"], p["ffn_conv_b"], S, nm("ffnact"))
    f = _mm(a, p["w_down"], "nn", F32, nm("mm_down"), caps=(1024, 1024, 1408))
    x2 = _resid_post(x1, f, mod3, 5, p["ffn_post_g"], S, nm("post2"))
    saved = dict(x=x, h=h, proj=proj, pre=pre, dtc=dtc, dtr=dtr, y=y, hs=hs, yn=yn, sc=sc, y_ssd=y_ssd, y_sc=y_sc,
                 m=m, mix=mix, x1=x1, h2=h2, uu=uu, a=a, f=f, mod3=mod3, cact=cact)
    return x2, saved, gathered


def _seq_sum(acc, nb):
    return acc.reshape(nb, 8, -1)[:, 0, :]


def _layer_bwd(dx2, p, s, S, li, chip_sums=None, early=None):
    d, lay = p["d"], p["lay"]
    D, DI, G, R, P, GN, CD, H, F = d["D"], d["DI"], d["G"], d["R"], d["P"], d["GN"], d["CD"], d["H"], d["F"]
    nb = dx2.shape[0] // S
    nm = lambda t: f"l{li}_{t}"
    mod3 = s["mod3"]
    g = {}
    exchanged = None
    df, dgt2, dpg2 = _post_bwd(s["f"], mod3, 5, p["ffn_post_g"], dx2, S, nm("post2_b"))
    g["ffn_post_g"] = dpg2[0]
    da = _mm(df, p["w_down"], "nt", BF16, nm("mm_down_bi"), caps=(1024, 1408, 2048))
    g["w_down"] = _mm(s["a"], df, "tn", WGRAD,nm("mm_down_bw"), caps=(1408, 1024, 1024))
    if chip_sums is None:
        duu, cw = _ffn_act_bwd(s["uu"], da, p["ffn_conv_w"], p["ffn_conv_b"], S, nm("ffnact_b"))
    else:
        (duu, cw), (exchanged,) = _ffn_act_bwd(s["uu"], da, p["ffn_conv_w"], p["ffn_conv_b"], S, nm("ffnact_b"),
                                               side=_side_chip_exchange(chip_sums))
    g["ffn_conv_w"] = _glu_perm(cw[:FFN_CONV_K], F, inverse=True)
    g["ffn_conv_b"] = _glu_perm(cw[FFN_CONV_K], F, inverse=True)
    dh2 = _mm(duu, p["w_up"], "nn", F32, nm("mm_up_bi"), caps=(1024, 1024, 2816))
    g["w_up"] = _glu_perm_rows(_mm(duu, s["h2"], "tn", WGRAD,nm("mm_up_bw"), caps=(1408, 1024, 1024)), F, inverse=True)
    dx1, dg2, dsc2, dsh2 = _pre_bwd(s["x1"], p["ffn_pre_g"], mod3, 4, dh2, dx2, S, nm("norm2_b"))
    g["ffn_pre_g"] = dg2[0]
    dmix, dgt1, dpg1 = _post_bwd(s["mix"], mod3, 2, p["mix_post_g"], dx1, S, nm("post1_b"))
    g["mix_post_g"] = dpg1[0]
    dm = _mm(dmix, p["w_o"], "nt", F32, nm("mm_o_bi"))
    g["w_o"] = _mm(s["m"], dmix, "tn", WGRAD,nm("mm_o_bw"))
    proj = s["proj"]
    dy_ssd, dy_sc, dg1, dg2_ = _merge_bwd(proj, lay["g1"], lay["g2"], s["y_ssd"], s["y_sc"], dm, S, nm("merge_b"))
    dyn = _mm(dy_ssd, p["w_ssd_out"], "nt", F32, nm("mm_ssdout_bi"))
    g["w_ssd_out"] = _mm(s["yn"], dy_ssd, "tn", WGRAD,nm("mm_ssdout_bw"))
    dsc = _mm(dy_sc, p["w_sc_out"], "nt", F32, nm("mm_scout_bi"))
    g["w_sc_out"] = _mm(s["sc"], dy_sc, "tn", WGRAD,nm("mm_scout_bw"))
    dscb, dscc, dsch, scw = _shortconv_bwd(proj, lay["scb"], lay["scc"], lay["sch"], D, p["sc_conv_w"], dsc, S, nm("sconv_b"))
    g["sc_conv_w"] = scw[:SC_CONV_K]
    dy, dz, dng = _gate_norm_bwd(s["y"], (proj, lay["z"]), p["ssd_norm_g"], dyn, G, S, nm("gnorm_b"))
    g["ssd_norm_g"] = dng[0]
    offs = (0, DI, DI + GN)
    early_side = None if early is None else _side_chip_exchange(early(g))
    res = _ssd_bwd(s["pre"], offs, s["dtc"], s["dtr"], p["hpc"], p["hpr"], s["hs"], dy, G, R, P, S, nm("ssd_b"),
                   side=early_side)
    (dpx, dpb, dpc, ddt, hpg), early_got = res if early is not None else (res, None)
    g["ssd_dt_bias"], g["ssd_a_log"], g["ssd_d"] = hpg[:, 0, :].reshape(H), hpg[:, 1, :].reshape(H), hpg[:, 2, :].reshape(H)
    cws, dxbc = [], []
    for name, darr, off, C in (("x", dpx, 0, DI), ("b", dpb, DI, GN), ("c", dpc, DI + GN, GN)):
        w8 = p["ssd_conv_w"][:, off:off + C]
        cws.append(_conv_bwd_w((darr, 0), (proj, lay["xbc"] + off), C, SSD_CONV_K, S, nm(f"ssdconv_bw_{name}")))
        dxbc.append(_conv_bwd_in((darr, 0), C, w8, SSD_CONV_K, S, BF16, nm(f"ssdconv_bi_{name}")))
    cws = jnp.concatenate(cws, axis=1)
    g["ssd_conv_w"], g["ssd_conv_b"] = cws[:SSD_CONV_K], cws[SSD_CONV_K]
    T = dx2.shape[0]
    ddt_t = jnp.pad(ddt.transpose(1, 0, 2).reshape(T, H).astype(BF16), ((0, 0), (0, lay["sb"] - H)))
    dproj = [dz] + dxbc + [dscb, dscc, dsch, dg1, dg2_, ddt_t]
    dh = _mm_seg(dproj, p["wcat"], "nn", F32, nm("mm_in_bi"), lay["sb"])
    dwcat = _mm_seg(dproj, s["h"], "tn", WGRAD, nm("mm_in_bw"), lay["sb"], tk=1024)
    o = lay
    g["w_in"] = jnp.concatenate([dwcat[o["z"]:o["scb"]], dwcat[o["dt"]:o["dt"] + H], dwcat[o["scb"]:o["dt"]]], axis=0)
    dx, dg1_, dsc1, dsh1 = _pre_bwd(s["x"], p["mix_pre_g"], mod3, 1, dh, dx1, S, nm("norm1_b"))
    g["mix_pre_g"] = dg1_[0]
    dmod = jnp.concatenate([_seq_sum(t, nb) for t in (dsh1, dsc1, dgt1, dsh2, dsc2, dgt2)], axis=1)
    dmod8 = jnp.pad(dmod, ((0, MOD_ROWS - nb), (0, 0)))
    g["ada_b"] = _colsum(dmod8, nm("adab"))
    g["ada_w"] = _mm(dmod8, s["cact"], "tn", WGRAD, nm("mm_ada_bw"), caps=(1536, 1024, 2048))
    return dx, g, exchanged, None if early_got is None else early_got[0]


def _colsum(a8, name):
    rows, C = a8.shape
    tc = _tile(C, 2048)

    def body(a_ref, o_ref):
        o_ref[...] = _bsum(jnp.sum(a_ref[...], axis=0, keepdims=True))

    return _pc(body, grid=(C // tc,), in_specs=[pl.BlockSpec((rows, tc), lambda j: (0, j))],
               out_specs=pl.BlockSpec((8, tc), lambda j: (0, j)), out_shape=SDS((8, C), F32), name=name)(a8)[0]


def _adam(gs, w, m, v, name):
    ns, R, W = gs.shape
    tr = _tile(R, 256, 8)

    def body(g_ref, w_ref, m_ref, v_ref, go_ref, d_ref, mo_ref, vo_ref):
        g = g_ref[0].astype(F32)
        for k in range(1, ns):
            g = g + g_ref[k].astype(F32)
        go_ref[...] = g
        d_ref[...], mo_ref[...], vo_ref[...] = _adam_update(g, w_ref[...], m_ref[...], v_ref[...])

    row = pl.BlockSpec((tr, W), lambda i: (i, 0))
    return _pc(body, grid=(R // tr,), in_specs=[pl.BlockSpec((ns, tr, W), lambda i: (0, i, 0)), row, row, row],
               out_specs=[row] * 4, out_shape=[SDS((R, W), F32)] * 4, name=name)(gs, w, m, v)


def _adam_update(g, w, m, v):
    c1 = 1.0 / (1.0 - ADAM_B1 ** ADAM_STEP)
    c2 = 1.0 / (1.0 - ADAM_B2 ** ADAM_STEP)
    m_ = ADAM_B1 * m + (1.0 - ADAM_B1) * g
    v_ = ADAM_B2 * v + (1.0 - ADAM_B2) * (g * g)
    return -ADAM_LR * ((m_ * c1) / (jnp.sqrt(v_ * c2) + ADAM_EPS) + ADAM_WD * w), m_, v_


def _adam_nat(g, w, m, v, name):
    depth, a, b = w.shape
    tr = _tile(a, 256, 8)

    def body(g_ref, w_ref, m_ref, v_ref, d_ref, mo_ref, vo_ref):
        d_ref[...], mo_ref[...], vo_ref[...] = _adam_update(g_ref[...], w_ref[...], m_ref[...], v_ref[...])

    blk = pl.BlockSpec((None, tr, b), lambda l, i: (l, i, 0))
    return _pc(body, grid=(depth, a // tr), in_specs=[blk] * 4, out_specs=[blk] * 3,
               out_shape=[SDS(w.shape, F32)] * 3, name=name)(g, w, m, v)


def _adam_mid(g, w, m, v, name):
    b, depth, a = w.shape
    tr = 128

    def body(g_ref, w_ref, m_ref, v_ref, d_ref, mo_ref, vo_ref):
        d_ref[...], mo_ref[...], vo_ref[...] = _adam_update(g_ref[...], w_ref[...], m_ref[...], v_ref[...])

    blk = pl.BlockSpec((tr, depth, a), lambda i: (i, 0, 0))
    return _pc(body, grid=(pl.cdiv(b, tr),), in_specs=[blk] * 4, out_specs=[blk] * 3,
               out_shape=[SDS(w.shape, F32)] * 3, name=name)(g, w, m, v)


def _sum_chips(gs, name):
    ns, R, W = gs.shape
    tr = _tile(R, 256, 16)

    def body(g_ref, o_ref):
        acc = g_ref[0].astype(F32)
        for k in range(1, ns):
            acc = acc + g_ref[k].astype(F32)
        o_ref[...] = acc

    return _pc(body, grid=(R // tr,), in_specs=[pl.BlockSpec((ns, tr, W), lambda i: (0, i, 0))],
               out_specs=pl.BlockSpec((tr, W), lambda i: (i, 0)), out_shape=SDS((R, W), F32), name=name)(gs)


HBM_SPEC = pl.BlockSpec(memory_space=pltpu.HBM)
VMEM_SPEC = pl.BlockSpec(memory_space=pltpu.VMEM)


def _dev():
    return lax.axis_index("x"), lax.axis_index("y"), lax.axis_index("c")


def _allgather_big(loc, name):
    R, W = loc.shape

    def body(x_ref, out_ref, send_sems, recv_sems, local_sem):
        x, y, c = _dev()
        me, sibling = (x, y, c), (x, y, 1 - c)
        chips = [(1 - x, y), (x, 1 - y), (1 - x, 1 - y)]

        def slab(px, py, pc):
            return out_ref.at[4 * px + 2 * py + pc]

        def copy(k, block, to, src=None):
            return pltpu.make_async_remote_copy(
                src_ref=slab(*block) if src is None else src, dst_ref=slab(*block),
                send_sem=send_sems.at[k], recv_sem=recv_sems.at[k], device_id=to, device_id_type=MESH)

        mine = pltpu.make_async_copy(x_ref, slab(*me), local_sem)
        mine.start()
        first = [copy(0, me, sibling, src=x_ref)]
        first += [copy(1 + j, me, (*chip, c), src=x_ref) for j, chip in enumerate(chips)]
        for cp in first:
            cp.start()
        passed = [copy(4 + j, (*chip, c), sibling) for j, chip in enumerate(chips)]
        for j, chip in enumerate(chips):
            copy(1 + j, (*chip, c), me).wait_recv()
            passed[j].start()
        copy(0, sibling, me).wait_recv()
        for j, chip in enumerate(chips):
            copy(4 + j, (*chip, 1 - c), me).wait_recv()
        for cp in first + passed:
            cp.wait_send()
        mine.wait()

    return pl.pallas_call(
        body, out_shape=SDS((N_DEV, R, W), loc.dtype), in_specs=[HBM_SPEC], out_specs=HBM_SPEC,
        scratch_shapes=[pltpu.SemaphoreType.DMA((7,)), pltpu.SemaphoreType.DMA((7,)), pltpu.SemaphoreType.DMA],
        name=name)(loc)


def _dma_sems(n):
    return (pltpu.SemaphoreType.DMA((n,)), pltpu.SemaphoreType.DMA((n,)), pltpu.SemaphoreType.DMA)


def _side_gather_direct(loc):
    R, W = loc.shape

    def copies(ins, outs, sems):
        x, y, c = _dev()
        x_ref, out = ins[0], outs[0]
        me = 4 * x + 2 * y + c
        peers = [(x, y, 1 - c), (1 - x, y, c), (x, 1 - y, c), (1 - x, 1 - y, c)]
        mk = lambda k, p, dst: pltpu.make_async_remote_copy(
            src_ref=x_ref, dst_ref=out.at[dst], send_sem=sems[0].at[k], recv_sem=sems[1].at[k], device_id=p,
            device_id_type=MESH)
        sends = [mk(k, p, me) for k, p in enumerate(peers)]
        recvs = [mk(k, p, 4 * p[0] + 2 * p[1] + p[2]) for k, p in enumerate(peers)]
        return sends, recvs, pltpu.make_async_copy(x_ref, out.at[me], sems[2])

    def start(ins, outs, sems):
        sends, _, mine = copies(ins, outs, sems)
        mine.start()
        for cp in sends:
            cp.start()

    def wait(ins, outs, sems):
        sends, recvs, mine = copies(ins, outs, sems)
        for cp in recvs:
            cp.wait_recv()
        for cp in sends:
            cp.wait_send()
        mine.wait()

    return _Side((loc,), (SDS((N_DEV, R, W), loc.dtype),), _dma_sems(4), start, wait)


def _side_gather_forward(buf):
    def copies(ins, outs, sems):
        x, y, c = _dev()
        out = outs[0]
        chips = [(1 - x, y), (x, 1 - y), (1 - x, 1 - y)]
        mk = lambda k, src, dst: pltpu.make_async_remote_copy(
            src_ref=out.at[src], dst_ref=out.at[dst], send_sem=sems[0].at[k], recv_sem=sems[1].at[k],
            device_id=(x, y, 1 - c), device_id_type=MESH)
        mine = [4 * px + 2 * py + c for px, py in chips]
        theirs = [4 * px + 2 * py + (1 - c) for px, py in chips]
        return [mk(k, s, s) for k, s in enumerate(mine)], [mk(k, s, t) for k, (s, t) in enumerate(zip(mine, theirs))]

    def start(ins, outs, sems):
        for cp in copies(ins, outs, sems)[0]:
            cp.start()

    def wait(ins, outs, sems):
        sends, recvs = copies(ins, outs, sems)
        for cp in recvs:
            cp.wait_recv()
        for cp in sends:
            cp.wait_send()

    return _Side((buf,), (SDS(buf.shape, buf.dtype),), _dma_sems(3)[:2], start, wait, {0: 0})


def _side_chip_exchange(p):
    def copies(ins, outs, sems):
        x, y, c = _dev()
        p_ref, out = ins[0], outs[0]
        j0 = 2 * x + y
        chips = [(1 - x, y), (x, 1 - y), (1 - x, 1 - y)]
        mk = lambda k, chip, src, dst: pltpu.make_async_remote_copy(
            src_ref=p_ref.at[src], dst_ref=out.at[dst], send_sem=sems[0].at[k], recv_sem=sems[1].at[k],
            device_id=(*chip, c), device_id_type=MESH)
        sends = [mk(k, chip, 2 * chip[0] + chip[1], j0) for k, chip in enumerate(chips)]
        recvs = [mk(k, chip, j0, 2 * chip[0] + chip[1]) for k, chip in enumerate(chips)]
        return sends, recvs, pltpu.make_async_copy(p_ref.at[j0], out.at[j0], sems[2])

    def start(ins, outs, sems):
        sends, _, mine = copies(ins, outs, sems)
        mine.start()
        for cp in sends:
            cp.start()

    def wait(ins, outs, sems):
        sends, recvs, mine = copies(ins, outs, sems)
        for cp in recvs:
            cp.wait_recv()
        for cp in sends:
            cp.wait_send()
        mine.wait()

    return _Side((p,), (SDS(p.shape, p.dtype),), _dma_sems(3), start, wait)


def _rs_pair_exchange(g, name):
    nd, R, W = g.shape
    nj = nd // 2

    def body(g_ref, out_ref, send_sems, recv_sems):
        x, y, c = _dev()
        cps = [pltpu.make_async_remote_copy(src_ref=g_ref.at[2 * j + (1 - c)], dst_ref=out_ref.at[j],
                                            send_sem=send_sems.at[j], recv_sem=recv_sems.at[j],
                                            device_id=(x, y, 1 - c), device_id_type=MESH) for j in range(nj)]
        for cp in cps:
            cp.start()
        for cp in cps:
            cp.wait()

    return pl.pallas_call(
        body, out_shape=SDS((nj, R, W), g.dtype), in_specs=[HBM_SPEC], out_specs=HBM_SPEC,
        scratch_shapes=[pltpu.SemaphoreType.DMA((nj,)), pltpu.SemaphoreType.DMA((nj,))], name=name)(g)


def _add_pairs(g, ra, name):
    nd, R, W = g.shape
    nj = nd // 2
    tr = _tile(R, 256, 8)
    cidx = lax.axis_index("c").astype(jnp.int32).reshape(1)

    def body(c_ref, a_ref, b_ref, o_ref):
        o_ref[...] = (a_ref[...].astype(F32) + b_ref[...].astype(F32)).astype(o_ref.dtype)

    gs = pltpu.PrefetchScalarGridSpec(
        num_scalar_prefetch=1, grid=(nj, R // tr),
        in_specs=[pl.BlockSpec((None, tr, W), lambda j, i, cr: (2 * j + cr[0], i, 0)),
                  pl.BlockSpec((None, tr, W), lambda j, i, cr: (j, i, 0))],
        out_specs=pl.BlockSpec((None, tr, W), lambda j, i, cr: (j, i, 0)))
    return pl.pallas_call(body, grid_spec=gs, out_shape=SDS((nj, R, W), g.dtype), name=name,
                          compiler_params=pltpu.CompilerParams(vmem_limit_bytes=VMEM_LIMIT))(cidx, g, ra)


def _rs_chip_exchange(p, name):
    nj, R, W = p.shape

    def body(p_ref, out_ref, send_sems, recv_sems, local_sem):
        x, y, c = _dev()
        j0 = 2 * x + y
        chips = [(1 - x, y), (x, 1 - y), (1 - x, 1 - y)]
        mine = pltpu.make_async_copy(p_ref.at[j0], out_ref.at[j0], local_sem)
        mine.start()

        def copy(k, chip):
            return pltpu.make_async_remote_copy(
                src_ref=p_ref.at[2 * chip[0] + chip[1]], dst_ref=out_ref.at[j0],
                send_sem=send_sems.at[k], recv_sem=recv_sems.at[k], device_id=(*chip, c), device_id_type=MESH)

        sent = [copy(k, chip) for k, chip in enumerate(chips)]
        for cp in sent:
            cp.start()
        for k, chip in enumerate(chips):
            pltpu.make_async_remote_copy(
                src_ref=p_ref.at[j0], dst_ref=out_ref.at[2 * chip[0] + chip[1]],
                send_sem=send_sems.at[k], recv_sem=recv_sems.at[k], device_id=(*chip, c), device_id_type=MESH).wait_recv()
        for cp in sent:
            cp.wait_send()
        mine.wait()

    return pl.pallas_call(
        body, out_shape=SDS((nj, R, W), p.dtype), in_specs=[HBM_SPEC], out_specs=HBM_SPEC,
        scratch_shapes=[pltpu.SemaphoreType.DMA((3,)), pltpu.SemaphoreType.DMA((3,)), pltpu.SemaphoreType.DMA],
        name=name)(p)


def _allgather_small(v, name):
    R, W = v.shape

    def body(v_ref, out_ref, send_sems, recv_sems, local_sem):
        x, y, c = _dev()
        mine = pltpu.make_async_copy(v_ref, out_ref.at[4 * x + 2 * y + c], local_sem)
        mine.start()
        peers = []
        for k in range(1, N_DEV):
            px = 1 - x if k & 4 else x
            py = 1 - y if k & 2 else y
            pc_ = 1 - c if k & 1 else c
            peers.append((px, py, pc_))
        sent = [pltpu.make_async_remote_copy(
            src_ref=v_ref, dst_ref=out_ref.at[4 * x + 2 * y + c], send_sem=send_sems.at[k], recv_sem=recv_sems.at[k],
            device_id=peer, device_id_type=MESH) for k, peer in enumerate(peers)]
        for cp in sent:
            cp.start()
        for k, (px, py, pc_) in enumerate(peers):
            pltpu.make_async_remote_copy(
                src_ref=v_ref, dst_ref=out_ref.at[4 * px + 2 * py + pc_], send_sem=send_sems.at[k],
                recv_sem=recv_sems.at[k], device_id=(px, py, pc_), device_id_type=MESH).wait_recv()
        for cp in sent:
            cp.wait_send()
        mine.wait()

    return pl.pallas_call(
        body, out_shape=SDS((N_DEV, R, W), v.dtype), in_specs=[VMEM_SPEC], out_specs=VMEM_SPEC,
        scratch_shapes=[pltpu.SemaphoreType.DMA((7,)), pltpu.SemaphoreType.DMA((7,)), pltpu.SemaphoreType.DMA],
        name=name)(v)


def _sum_slabs(a, name):
    ns, R, W = a.shape

    def body(a_ref, o_ref):
        acc = a_ref[0]
        for k in range(1, ns):
            acc = acc + a_ref[k]
        o_ref[...] = acc

    return pl.pallas_call(body, out_shape=SDS((R, W), a.dtype), in_specs=[VMEM_SPEC], out_specs=VMEM_SPEC, name=name)(a)


BIG = (("ada_w", "col"), ("w_in", "col"), ("w_ssd_out", "row"), ("w_sc_out", "row"), ("w_o", "row"), ("w_up", "col"),
       ("w_down", "row"))
EARLY = ("w_ssd_out", "w_sc_out", "w_o", "w_up", "w_down")
LATE = ("ada_w", "w_in")
MID_LAYOUT = ("w_in",)
SWAP_LAYOUT = ("w_up",)
CONVW = ("ssd_conv_w", "sc_conv_w", "ffn_conv_w")
REPL = ("ada_b", "mix_pre_g", "mix_post_g", "ssd_conv_b", "ssd_dt_bias", "ssd_a_log", "ssd_d", "ssd_norm_g", "ffn_pre_g",
        "ffn_post_g", "ffn_conv_b")
WEIGHTS = ("ada_w", "ada_b", "mix_pre_g", "mix_post_g", "w_in", "ssd_conv_w", "ssd_conv_b", "ssd_dt_bias", "ssd_a_log",
           "ssd_d", "ssd_norm_g", "w_ssd_out", "sc_conv_w", "w_sc_out", "w_o", "ffn_pre_g", "ffn_post_g", "w_up",
           "ffn_conv_w", "ffn_conv_b", "w_down")


def _pad_rows(a, mult):
    r = a.shape[-2]
    pad = -r % mult
    return a if pad == 0 else jnp.pad(a, [(0, 0)] * (a.ndim - 2) + [(0, pad), (0, 0)])


def _flat_rows(parts, mult):
    flat = jnp.concatenate([p.reshape(-1) for p in parts])
    flat = jnp.pad(flat, (0, -flat.shape[0] % ROW_W))
    return _pad_rows(flat.reshape(-1, ROW_W), mult)


def _unflat(buf, shapes):
    flat = buf.reshape(-1)
    out, o = [], 0
    for shp in shapes:
        n = 1
        for s in shp:
            n *= s
        out.append(flat[o:o + n].reshape(shp))
        o += n
    return out


def _pack_big_local(get, l):
    return [_pad_rows((get(n)[l].T if kind == "col" else get(n)[l]).reshape(-1, ROW_W), SLAB_ALIGN) for n, kind in BIG]


def _big_rows(shapes, names=None):
    out, o = {}, 0
    for n in (names if names is not None else [n for n, _ in BIG]):
        r = shapes[n][1] * shapes[n][2] // ROW_W
        out[n] = (o, o + r)
        o += -(-r // SLAB_ALIGN) * SLAB_ALIGN
    return out, o


def kernel(x, c, ada_w, ada_b, mix_pre_g, mix_post_g, w_in, ssd_conv_w, ssd_conv_b, ssd_dt_bias, ssd_a_log, ssd_d, ssd_norm_g, w_ssd_out, sc_conv_w, w_sc_out, w_o, ffn_pre_g, ffn_post_g, w_up, ffn_conv_w, ffn_conv_b, w_down, loss_target, m_ada_w, m_ada_b, m_mix_pre_g, m_mix_post_g, m_w_in, m_ssd_conv_w, m_ssd_conv_b, m_ssd_dt_bias, m_ssd_a_log, m_ssd_d, m_ssd_norm_g, m_w_ssd_out, m_sc_conv_w, m_w_sc_out, m_w_o, m_ffn_pre_g, m_ffn_post_g, m_w_up, m_ffn_conv_w, m_ffn_conv_b, m_w_down, v_ada_w, v_ada_b, v_mix_pre_g, v_mix_post_g, v_w_in, v_ssd_conv_w, v_ssd_conv_b, v_ssd_dt_bias, v_ssd_a_log, v_ssd_d, v_ssd_norm_g, v_w_ssd_out, v_sc_conv_w, v_w_sc_out, v_w_o, v_ffn_pre_g, v_ffn_post_g, v_w_up, v_ffn_conv_w, v_ffn_conv_b, v_w_down):
    wl = dict(zip(WEIGHTS, (ada_w, ada_b, mix_pre_g, mix_post_g, w_in, ssd_conv_w, ssd_conv_b, ssd_dt_bias, ssd_a_log,
                            ssd_d, ssd_norm_g, w_ssd_out, sc_conv_w, w_sc_out, w_o, ffn_pre_g, ffn_post_g, w_up,
                            ffn_conv_w, ffn_conv_b, w_down)))
    ml = dict(zip(WEIGHTS, (m_ada_w, m_ada_b, m_mix_pre_g, m_mix_post_g, m_w_in, m_ssd_conv_w, m_ssd_conv_b,
                            m_ssd_dt_bias, m_ssd_a_log, m_ssd_d, m_ssd_norm_g, m_w_ssd_out, m_sc_conv_w, m_w_sc_out, m_w_o,
                            m_ffn_pre_g, m_ffn_post_g, m_w_up, m_ffn_conv_w, m_ffn_conv_b, m_w_down)))
    vl = dict(zip(WEIGHTS, (v_ada_w, v_ada_b, v_mix_pre_g, v_mix_post_g, v_w_in, v_ssd_conv_w, v_ssd_conv_b,
                            v_ssd_dt_bias, v_ssd_a_log, v_ssd_d, v_ssd_norm_g, v_w_ssd_out, v_sc_conv_w, v_w_sc_out, v_w_o,
                            v_ffn_pre_g, v_ffn_post_g, v_w_up, v_ffn_conv_w, v_ffn_conv_b, v_w_down)))
    depth = ada_w.shape[0]
    shapes = {n: wl[n].shape for n in WEIGHTS}
    me = 4 * lax.axis_index("x") + 2 * lax.axis_index("y") + lax.axis_index("c")

    rows, n_big = _big_rows(shapes)
    conv_flat = jnp.concatenate([wl[n][l].reshape(-1) for l in range(depth) for n in CONVW])
    n_conv = conv_flat.shape[0]
    conv_flat = jnp.pad(conv_flat, (0, -n_conv % (ROW_W // 2)))
    conv_rows = lax.bitcast_convert_type(conv_flat, BF16).reshape(-1, ROW_W)

    def local_rows(l):
        pieces = _pack_big_local(lambda n: wl[n].astype(BF16), l) + ([conv_rows] if l == 0 else [])
        return _pad_rows(jnp.concatenate(pieces, axis=0), ROW_PAD)

    def layer_weights(l, gathered):
        w = {n: wl[n][l] for n in REPL}
        for n, kind in BIG:
            a, b = shapes[n][1], shapes[n][2]
            blk = gathered[:, rows[n][0]:rows[n][1]]
            w[n] = blk.reshape(N_DEV * b, a) if kind == "col" else blk.reshape(N_DEV * a, b)
        for n in CONVW:
            w[n] = conv_full[(l, n)]
        return w

    gathered = _allgather_big(local_rows(0), "allgather_weights")
    conv_all = lax.bitcast_convert_type(
        gathered[:, n_big:n_big + conv_rows.shape[0]].reshape(N_DEV, -1, 2), F32)[:, :n_conv]
    conv_full, o = {}, 0
    for l in range(depth):
        for n in CONVW:
            k, cl = shapes[n][1], shapes[n][2]
            conv_full[(l, n)] = conv_all[:, o:o + k * cl].reshape(N_DEV, k, cl).transpose(1, 0, 2).reshape(k, N_DEV * cl)
            o += k * cl

    nb, S, D = x.shape
    T = nb * S
    act = x.reshape(T, D)
    c8 = jnp.pad(c, ((0, MOD_ROWS - nb), (0, 0)))
    preps, saved = [], []
    for l in range(depth):
        preps.append(_prep_layer(layer_weights(l, gathered)))
        act, s, gathered = _layer_fwd(act, c8, preps[l], S, l, gather=local_rows(l + 1) if l + 1 < depth else None)
        saved.append(s)
    dy, lacc = _loss(act, loss_target.reshape(T, D), S, "loss")
    loss_loc = lacc[0, 0]

    group_rows = {grp: _big_rows(shapes, names) for grp, names in (("early", EARLY), ("late", LATE))}

    def pair_sums(g, grp, names, l):
        slabs = [_pad_rows(g[n].astype(BF16).reshape(N_DEV, -1, ROW_W), SLAB_ALIGN) for n in names]
        slabs.append(jnp.zeros((N_DEV, -group_rows[grp][1] % ROW_PAD, ROW_W), BF16))
        gslab = jnp.concatenate(slabs, axis=1)
        from_sibling = _rs_pair_exchange(gslab, f"rs_pair_exchange_{grp}_l{l}")
        return _add_pairs(gslab, from_sibling, f"rs_pair_add_{grp}_l{l}")

    grads, pending = [None] * depth, None
    from_chips = {"early": [None] * depth, "late": [None] * depth}
    for l in reversed(range(depth)):
        dy, grads[l], got, from_chips["early"][l] = _layer_bwd(
            dy, preps[l], saved[l], S, l, chip_sums=pending, early=lambda g, l=l: pair_sums(g, "early", EARLY, l))
        if pending is not None:
            from_chips["late"][l + 1] = got
        pending = pair_sums(grads[l], "late", LATE, l)
    from_chips["late"][0] = _rs_chip_exchange(pending, "rs_chip_exchange")
    dx = dy.reshape(nb, S, D)
    g_sums = {grp: [_sum_chips(from_chips[grp][l], f"rs_chip_sum_{grp}_l{l}") for l in range(depth)]
              for grp in ("early", "late")}

    def slab_of(l, n, kind):
        grp = "early" if n in EARLY else "late"
        r0, r1 = group_rows[grp][0][n]
        a, b = shapes[n][1], shapes[n][2]
        return g_sums[grp][l][r0:r1].reshape((b, a) if kind == "col" else (a, b))

    g_big, d_big, m_big, v_big = {}, {}, {}, {}
    for n, kind in BIG:
        if n in MID_LAYOUT:
            gm = jnp.stack([slab_of(l, n, kind) for l in range(depth)], axis=1)
            fwd, back = (lambda t: t.transpose(2, 0, 1)), (lambda t: t.transpose(1, 2, 0))
            res = [gm] + list(_adam_mid(gm, fwd(wl[n]), fwd(ml[n]), fwd(vl[n]), f"adam_{n}"))
        else:
            gt = jnp.stack([slab_of(l, n, kind) for l in range(depth)])
            fwd = back = (lambda t: t.swapaxes(1, 2)) if kind == "col" else (lambda t: t)
            if n in SWAP_LAYOUT:
                res = [gt] + list(_adam_nat(gt, fwd(wl[n]), fwd(ml[n]), fwd(vl[n]), f"adam_{n}"))
            else:
                gn = back(gt)
                res, back = [gn] + list(_adam_nat(gn, wl[n], ml[n], vl[n], f"adam_{n}")), (lambda t: t)
        g_big[n], d_big[n], m_big[n], v_big[n] = [back(t) for t in res]

    parts = [jnp.broadcast_to(loss_loc, (ROW_W,))]
    small_shapes = [(ROW_W,)]
    for l in range(depth):
        for n in REPL + CONVW:
            parts.append(grads[l][n])
            small_shapes.append(tuple(grads[l][n].shape))
    total = _sum_slabs(_allgather_small(_flat_rows(parts, 8), "allgather_small"), "sum_small")
    pieces = _unflat(total, small_shapes)
    loss = pieces[0][0]
    g_small, i = {}, 1
    for l in range(depth):
        for n in REPL + CONVW:
            gp = pieces[i]
            i += 1
            if n in CONVW:
                gp = lax.dynamic_slice_in_dim(gp, me * shapes[n][2], shapes[n][2], axis=1)
            g_small[(l, n)] = gp
    order = [(l, n) for l in range(depth) for n in REPL + CONVW]
    loc_shapes = [tuple(shapes[n][1:]) for _, n in order]
    packs = lambda f: _flat_rows([f(l, n) for l, n in order], 8)
    gs_small = packs(lambda l, n: g_small[(l, n)])
    _, d_sm, m_sm, v_sm = _adam(gs_small[None], packs(lambda l, n: wl[n][l]), packs(lambda l, n: ml[n][l]),
                                packs(lambda l, n: vl[n][l]), "adam_small")

    def unpack_small(buf):
        ps = _unflat(buf, loc_shapes)
        return {n: jnp.stack([ps[order.index((l, n))] for l in range(depth)]) for n in REPL + CONVW}

    outs = []
    for big, small in ((g_big, {n: jnp.stack([g_small[(l, n)] for l in range(depth)]) for n in REPL + CONVW}),
                       (d_big, unpack_small(d_sm)), (m_big, unpack_small(m_sm)), (v_big, unpack_small(v_sm))):
        merged = {**big, **small}
        outs += [merged[n] for n in WEIGHTS]
    return (loss, dx, *outs)
```

```python
import functools
import math
from typing import Callable, NamedTuple

import jax
import jax.numpy as jnp
from jax import lax
from jax.experimental import pallas as pl
from jax.experimental.pallas import tpu as pltpu

F32, BF16 = jnp.float32, jnp.bfloat16
WGRAD = BF16
SDS = jax.ShapeDtypeStruct
MESH = pl.DeviceIdType.MESH

EPS = 1e-6
N_STATE = 128
CHUNK = 128
SSD_STEP_CHUNKS = 4
SSD_CONV_K, SC_CONV_K, FFN_CONV_K = 4, 3, 3
N_DEV = 8
ROW_W = 1024
ROW_PAD = 32
SLAB_ALIGN = 16
SEG_BLK = 512
STRIP = 32
FFN_STRIP = 64
GLU_W = 256
MOD_ROWS = 128
VMEM_LIMIT = 48 * 2**20

ADAM_LR, ADAM_B1, ADAM_B2, ADAM_EPS, ADAM_WD, ADAM_STEP = 0.001, 0.9, 0.999, 1e-08, 0.01, 10

NT = (((1,), (1,)), ((), ()))
TN = (((0,), (0,)), ((), ()))
NN = (((1,), (0,)), ((), ()))


def _tile(n, cap, mult=128):
    best = None
    for t in range(mult, min(n, cap) + 1, mult):
        if n % t == 0:
            best = t
    return best if best is not None else n


class _Side(NamedTuple):
    operands: tuple
    out_shape: tuple
    scratch: tuple
    start: Callable
    wait: Callable
    aliases: dict = {}


def _pc(body, *, grid, in_specs, out_specs, out_shape, name, scratch=(), side=None):
    params = pltpu.CompilerParams(dimension_semantics=("arbitrary",) * len(grid), vmem_limit_bytes=VMEM_LIMIT)
    if side is None:
        return pl.pallas_call(body, grid=grid, in_specs=in_specs, out_specs=out_specs, out_shape=out_shape,
                              scratch_shapes=list(scratch), name=name, compiler_params=params)
    single = not isinstance(out_shape, (list, tuple))
    outs = [out_shape] if single else list(out_shape)
    ospecs = [out_specs] if single else list(out_specs)
    n_in, n_out, n_scr = len(in_specs), len(outs), len(scratch)
    s_in, s_out = len(side.operands), len(side.out_shape)

    def hosted(*refs):
        ins, refs = refs[:n_in], refs[n_in:]
        sins, refs = refs[:s_in], refs[s_in:]
        mouts, refs = refs[:n_out], refs[n_out:]
        souts, refs = refs[:s_out], refs[s_out:]
        scr, sems = refs[:n_scr], refs[n_scr:]
        first = functools.reduce(lambda a, b: a & b, [pl.program_id(a) == 0 for a in range(len(grid))])
        last = functools.reduce(lambda a, b: a & b, [pl.program_id(a) == grid[a] - 1 for a in range(len(grid))])

        @pl.when(first)
        def _():
            side.start(sins, souts, sems)

        body(*ins, *mouts, *scr)

        @pl.when(last)
        def _():
            side.wait(sins, souts, sems)

    call = pl.pallas_call(
        hosted, grid=grid, in_specs=list(in_specs) + [HBM_SPEC] * s_in, out_specs=ospecs + [HBM_SPEC] * s_out,
        out_shape=outs + list(side.out_shape), scratch_shapes=list(scratch) + list(side.scratch), name=name,
        input_output_aliases={n_in + k: n_out + v for k, v in side.aliases.items()}, compiler_params=params)

    def run(*args):
        res = call(*args, *side.operands)
        main = res[0] if single else list(res[:n_out])
        return main, list(res[n_out:])

    return run


def _silu(x):
    return x * jax.nn.sigmoid(x)


def _dsilu(x):
    s = jax.nn.sigmoid(x)
    return s * (1.0 + x * (1.0 - s))


def _softplus(x):
    return jnp.maximum(x, 0.0) + jnp.log(1.0 + jnp.exp(-jnp.abs(x)))


def _dot(a, b, dims=NN):
    return lax.dot_general(a, b, dims, preferred_element_type=F32)


def _bsum(v, rows=8):
    return jnp.broadcast_to(v, (rows, v.shape[1]))


def _mm(a, b, mode, out_dtype, name, caps=(1024, 1024, 2048), side=None):
    if mode == "nn":
        (M, K), (K2, N) = a.shape, b.shape
    elif mode == "nt":
        (M, K), (N, K2) = a.shape, b.shape
    else:
        (K, M), (K2, N) = a.shape, b.shape
    assert K == K2, (a.shape, b.shape, mode)
    tm, tn, tk = _tile(M, caps[0]), _tile(N, caps[1]), _tile(K, caps[2])
    nk = K // tk
    dims = {"nn": NN, "nt": NT, "tn": TN}[mode]
    if mode == "tn":
        a_spec = pl.BlockSpec((tk, tm), lambda i, j, k: (k, i))
    else:
        a_spec = pl.BlockSpec((tm, tk), lambda i, j, k: (i, k))
    if mode == "nt":
        b_spec = pl.BlockSpec((tn, tk), lambda i, j, k: (j, k))
    else:
        b_spec = pl.BlockSpec((tk, tn), lambda i, j, k: (k, j))

    def body(a_ref, b_ref, o_ref, *acc):
        part = _dot(a_ref[...].astype(BF16), b_ref[...].astype(BF16), dims)
        if nk == 1:
            o_ref[...] = part.astype(o_ref.dtype)
        else:
            acc_ref, = acc
            k = pl.program_id(2)

            @pl.when(k == 0)
            def _():
                acc_ref[...] = part

            @pl.when(k > 0)
            def _():
                acc_ref[...] += part

            @pl.when(k == nk - 1)
            def _():
                o_ref[...] = acc_ref[...].astype(o_ref.dtype)

    return _pc(body, grid=(M // tm, N // tn, nk), in_specs=[a_spec, b_spec],
               out_specs=pl.BlockSpec((tm, tn), lambda i, j, k: (i, j)),
               out_shape=SDS((M, N), out_dtype), name=name,
               scratch=() if nk == 1 else (pltpu.VMEM((tm, tn), F32),), side=side)(a, b)


def _mm_seg(segs, b, mode, out_dtype, name, blk, tile=1024, tk=2048):
    nblk = [a.shape[1] // blk for a in segs]
    assert all(a.shape[1] % blk == 0 for a in segs)
    start = [sum(nblk[:s]) for s in range(len(segs))]
    total = sum(nblk)
    ns = len(segs)
    N = b.shape[1]
    tn = _tile(N, tile)
    if mode == "nn":
        M = segs[0].shape[0]
        tm = _tile(M, tile)
        grid = (M // tm, N // tn, total)
        a_specs = [pl.BlockSpec((tm, blk), lambda i, j, k, k0=k0, n=n: (i, jnp.clip(k - k0, 0, n - 1)))
                   for k0, n in zip(start, nblk)]
        b_spec = pl.BlockSpec((blk, tn), lambda i, j, k: (k, j))
        out_rows, tmo, dims, seg_axis = M, tm, NN, 2
    else:
        K = segs[0].shape[0]
        tkk = _tile(K, tk)
        grid = (total, N // tn, K // tkk)
        a_specs = [pl.BlockSpec((tkk, blk), lambda i, j, k, i0=i0, n=n: (
            jnp.where((i >= i0) & (i < i0 + n), k, 0), jnp.clip(i - i0, 0, n - 1))) for i0, n in zip(start, nblk)]
        b_spec = pl.BlockSpec((tkk, tn), lambda i, j, k: (k, j))
        out_rows, tmo, seg_axis = total * blk, blk, 0
    nk = grid[2]
    acc_shape = (tm, tn) if mode == "nn" else (tn, blk)

    def body(*refs):
        a_refs, b_ref, o_ref, acc_ref = refs[:ns], refs[ns], refs[ns + 1], refs[ns + 2]
        k = pl.program_id(2)
        sel = pl.program_id(seg_axis)

        @pl.when(k == 0)
        def _():
            acc_ref[...] = jnp.zeros_like(acc_ref)

        for s in range(ns):
            @pl.when((sel >= start[s]) & (sel < start[s] + nblk[s]))
            def _(s=s):
                a_, b_ = a_refs[s][...].astype(BF16), b_ref[...].astype(BF16)
                acc_ref[...] += _dot(a_, b_, NN) if mode == "nn" else _dot(b_, a_, TN)

        @pl.when(k == nk - 1)
        def _():
            acc = acc_ref[...]
            o_ref[...] = (acc if mode == "nn" else acc.T).astype(o_ref.dtype)

    return _pc(body, grid=grid, in_specs=a_specs + [b_spec], out_specs=pl.BlockSpec((tmo, tn), lambda i, j, k: (i, j)),
               out_shape=SDS((out_rows, N), out_dtype), name=name, scratch=(pltpu.VMEM(acc_shape, F32),))(*segs, b)


def _modk(c8, ada_w, ada_b, name):
    rows, D = c8.shape
    N = ada_w.shape[0]
    tn = _tile(N, 1536)

    def body(c_ref, w_ref, b_ref, mod_ref, ca_ref):
        ca = _silu(c_ref[...]).astype(BF16)
        mod_ref[...] = _dot(ca, w_ref[...], NT) + b_ref[...]
        ca_ref[...] = ca

    return _pc(body, grid=(N // tn,),
               in_specs=[pl.BlockSpec((rows, D), lambda j: (0, 0)), pl.BlockSpec((tn, D), lambda j: (j, 0)),
                         pl.BlockSpec((1, tn), lambda j: (0, j))],
               out_specs=[pl.BlockSpec((rows, tn), lambda j: (0, j)), pl.BlockSpec((rows, D), lambda j: (0, 0))],
               out_shape=[SDS((rows, N), F32), SDS((rows, D), BF16)], name=name)(c8, ada_w, ada_b)


def _row_tile(S):
    return _tile(S, 512, 8)


def _strip_row_tile(S, cap=2048):
    return _tile(S, cap, FFN_STRIP)


def _strips(tm, fn, init=0, rows=None):
    rows = STRIP if rows is None else rows
    assert tm % rows == 0
    return lax.fori_loop(0, tm // rows, lambda r, c: fn(pl.multiple_of(r * rows, rows), c), init)


def _strips_prev(tm, rows, ref, prev, fn, init=0):
    carry = fn(0, jnp.concatenate([prev, ref[0:rows, :]], axis=0), init)

    def step(r, c):
        r0 = pl.multiple_of(r * rows, rows)
        return fn(r0, ref[pl.ds(pl.multiple_of(r0 - 8, 8), rows + 8), :], c)

    return lax.fori_loop(1, tm // rows, step, carry)


def _strips_next(tm, rows, ref, nxt, fn, init=0):
    def step(r, c):
        r0 = pl.multiple_of(r * rows, rows)
        return fn(r0, ref[pl.ds(r0, rows + 8), :], c)

    carry = lax.fori_loop(0, tm // rows - 1, step, init)
    return fn(tm - rows, jnp.concatenate([ref[tm - rows:tm, :], nxt], axis=0), carry)


def _rows8(rows):
    pad = 8 - len(rows)
    return jnp.concatenate(rows + ([jnp.zeros((pad, rows[0].shape[1]), F32)] if pad else []), axis=0)


def _fold8(v):
    return jnp.sum(v.reshape(v.shape[0] // 8, 8, v.shape[1]), axis=0)


def _norm_mod(x, g, mod3, sc_seg, sh_seg, S, name):
    T, D = x.shape
    tm = _row_tile(S)
    tpb = S // tm

    def body(x_ref, g_ref, sc_ref, sh_ref, h_ref):
        x_ = x_ref[...]
        r = lax.rsqrt(jnp.mean(x_ * x_, axis=-1, keepdims=True) + EPS)
        h_ref[...] = ((x_ * r) * (g_ref[...] * (1.0 + sc_ref[...])) + sh_ref[...]).astype(BF16)

    return _pc(body, grid=(T // tm,),
               in_specs=[pl.BlockSpec((tm, D), lambda i: (i, 0)), pl.BlockSpec((1, D), lambda i: (0, 0)),
                         pl.BlockSpec((None, 1, D), lambda i: (i // tpb, 0, sc_seg)),
                         pl.BlockSpec((None, 1, D), lambda i: (i // tpb, 0, sh_seg))],
               out_specs=pl.BlockSpec((tm, D), lambda i: (i, 0)), out_shape=SDS((T, D), BF16), name=name)(x, g, mod3, mod3)


def _resid_post(x, fo, mod3, gt_seg, pg, S, name):
    T, D = x.shape
    tm = _row_tile(S)
    tpb = S // tm

    def body(x_ref, f_ref, gt_ref, pg_ref, o_ref):
        f = f_ref[...]
        r = lax.rsqrt(jnp.mean(f * f, axis=-1, keepdims=True) + EPS)
        o_ref[...] = x_ref[...] + (f * r) * (gt_ref[...] * pg_ref[...])

    return _pc(body, grid=(T // tm,),
               in_specs=[pl.BlockSpec((tm, D), lambda i: (i, 0)), pl.BlockSpec((tm, D), lambda i: (i, 0)),
                         pl.BlockSpec((None, 1, D), lambda i: (i // tpb, 0, gt_seg)),
                         pl.BlockSpec((1, D), lambda i: (0, 0))],
               out_specs=pl.BlockSpec((tm, D), lambda i: (i, 0)), out_shape=SDS((T, D), F32), name=name)(x, fo, mod3, pg)


def _post_bwd(fo, mod3, gt_seg, pg, dout, S, name):
    T, D = fo.shape
    tm = _row_tile(S)
    tpb = S // tm
    nb = T // S

    def body(f_ref, gt_ref, pg_ref, d_ref, df_ref, dgt_ref, dpg_ref):
        i = pl.program_id(0)

        @pl.when(i == 0)
        def _():
            dpg_ref[...] = jnp.zeros_like(dpg_ref)

        @pl.when(i % tpb == 0)
        def _():
            dgt_ref[...] = jnp.zeros_like(dgt_ref)

        f, d = f_ref[...], d_ref[...]
        r = lax.rsqrt(jnp.mean(f * f, axis=-1, keepdims=True) + EPS)
        n = f * r
        dn = d * (gt_ref[...] * pg_ref[...])
        df_ref[...] = (r * (dn - n * jnp.mean(dn * n, axis=-1, keepdims=True))).astype(df_ref.dtype)
        tot = jnp.sum(d * n, axis=0, keepdims=True)
        dgt_ref[...] += _bsum(tot * pg_ref[...])
        dpg_ref[...] += _bsum(tot * gt_ref[...])

    return _pc(body, grid=(T // tm,),
               in_specs=[pl.BlockSpec((tm, D), lambda i: (i, 0)),
                         pl.BlockSpec((None, 1, D), lambda i: (i // tpb, 0, gt_seg)),
                         pl.BlockSpec((1, D), lambda i: (0, 0)), pl.BlockSpec((tm, D), lambda i: (i, 0))],
               out_specs=[pl.BlockSpec((tm, D), lambda i: (i, 0)), pl.BlockSpec((8, D), lambda i: (i // tpb, 0)),
                          pl.BlockSpec((8, D), lambda i: (0, 0))],
               out_shape=[SDS((T, D), BF16), SDS((nb * 8, D), F32), SDS((8, D), F32)], name=name)(fo, mod3, pg, dout)


def _pre_bwd(x, g, mod3, sc_seg, dh, dout, S, name):
    T, D = x.shape
    tm = _row_tile(S)
    tpb = S // tm
    nb = T // S

    def body(x_ref, g_ref, sc_ref, dh_ref, d_ref, dx_ref, dg_ref, dsc_ref, dsh_ref):
        i = pl.program_id(0)

        @pl.when(i == 0)
        def _():
            dg_ref[...] = jnp.zeros_like(dg_ref)

        @pl.when(i % tpb == 0)
        def _():
            dsc_ref[...] = jnp.zeros_like(dsc_ref)
            dsh_ref[...] = jnp.zeros_like(dsh_ref)

        x_, dh_ = x_ref[...], dh_ref[...]
        r = lax.rsqrt(jnp.mean(x_ * x_, axis=-1, keepdims=True) + EPS)
        n = x_ * r
        dn = dh_ * (g_ref[...] * (1.0 + sc_ref[...]))
        dx_ref[...] = d_ref[...] + r * (dn - n * jnp.mean(dn * n, axis=-1, keepdims=True))
        dhn = jnp.sum(dh_ * n, axis=0, keepdims=True)
        dg_ref[...] += _bsum(dhn * (1.0 + sc_ref[...]))
        dsc_ref[...] += _bsum(dhn * g_ref[...])
        dsh_ref[...] += _bsum(jnp.sum(dh_, axis=0, keepdims=True))

    row = pl.BlockSpec((tm, D), lambda i: (i, 0))
    return _pc(body, grid=(T // tm,),
               in_specs=[row, pl.BlockSpec((1, D), lambda i: (0, 0)),
                         pl.BlockSpec((None, 1, D), lambda i: (i // tpb, 0, sc_seg)), row, row],
               out_specs=[row, pl.BlockSpec((8, D), lambda i: (0, 0)), pl.BlockSpec((8, D), lambda i: (i // tpb, 0)),
                          pl.BlockSpec((8, D), lambda i: (i // tpb, 0))],
               out_shape=[SDS((T, D), F32), SDS((8, D), F32), SDS((nb * 8, D), F32), SDS((nb * 8, D), F32)],
               name=name)(x, g, mod3, dh, dout)


def _loss(y, target, S, name):
    T, D = y.shape
    tm = _row_tile(S)

    def body(y_ref, t_ref, dy_ref, l_ref):
        @pl.when(pl.program_id(0) == 0)
        def _():
            l_ref[...] = jnp.zeros_like(l_ref)

        def strip(r0, carry):
            rows = pl.ds(r0, STRIP)
            e = y_ref[rows, :] - t_ref[rows, :]
            dy_ref[rows, :] = e * (1.0 / D)
            return carry + _fold8(e * e)

        acc = _strips(tm, strip, jnp.zeros((8, D), F32))
        l_ref[...] += jnp.broadcast_to(jnp.sum(acc, keepdims=True) * (0.5 / D), l_ref.shape)

    row = pl.BlockSpec((tm, D), lambda i: (i, 0))
    return _pc(body, grid=(T // tm,), in_specs=[row, row],
               out_specs=[row, pl.BlockSpec((8, 128), lambda i: (0, 0))],
               out_shape=[SDS((T, D), F32), SDS((8, 128), F32)], name=name)(y, target)


def _conv_geom(view, C, S, cap=2048):
    arr, off = view
    T = arr.shape[0]
    tm = _strip_row_tile(S, cap)
    tc = _tile(C, 512)
    assert off % tc == 0 and C % tc == 0
    return arr, off // tc, T, tm, tc, S // tm


def _prev_spec(tm, tc, ob, order):
    if order == "ij":
        return pl.BlockSpec((8, tc), lambda i, j: (jnp.maximum(i * (tm // 8) - 1, 0), ob + j))
    return pl.BlockSpec((8, tc), lambda j, i: (jnp.maximum(i * (tm // 8) - 1, 0), ob + j))


def _next_spec(T, tm, tc, ob, order):
    last = T // 8 - 1
    if order == "ij":
        return pl.BlockSpec((8, tc), lambda i, j: (jnp.minimum((i + 1) * (tm // 8), last), ob + j))
    return pl.BlockSpec((8, tc), lambda j, i: (jnp.minimum((i + 1) * (tm // 8), last), ob + j))


def _taps(win, w_ref, K, lead, rows):
    acc = win[lead:lead + rows] * w_ref[K - 1:K, :]
    for j in range(1, K):
        acc = acc + win[lead - j:lead - j + rows] * w_ref[K - 1 - j:K - j, :]
    return acc


def _taps_t(win, w_ref, K, rows):
    acc = win[0:rows] * w_ref[K - 1:K, :]
    for j in range(1, K):
        acc = acc + win[j:j + rows] * w_ref[K - 1 - j:K - j, :]
    return acc


def _conv_fwd(view, C, w8, b, K, S, name):
    arr, ob, T, tm, tc, tps = _conv_geom(view, C, S)

    def body(u_ref, p_ref, w_ref, b_ref, o_ref):
        first = (pl.program_id(0) % tps) == 0

        def strip(r0, win, carry):
            o_ref[pl.ds(r0, STRIP), :] = _taps(win, w_ref, K, 8, STRIP) + b_ref[...]
            return carry

        _strips_prev(tm, STRIP, u_ref, jnp.where(first, 0.0, p_ref[...]), strip)

    return _pc(body, grid=(T // tm, C // tc),
               in_specs=[pl.BlockSpec((tm, tc), lambda i, j: (i, ob + j)), _prev_spec(tm, tc, ob, "ij"),
                         pl.BlockSpec((8, tc), lambda i, j: (0, j)), pl.BlockSpec((1, tc), lambda i, j: (0, j))],
               out_specs=pl.BlockSpec((tm, tc), lambda i, j: (i, j)), out_shape=SDS((T, C), F32), name=name)(
                   arr, arr, w8, b)


def _conv_bwd_in(dview, C, w8, K, S, out_dtype, name):
    arr, ob, T, tm, tc, tps = _conv_geom(dview, C, S)

    def body(d_ref, n_ref, w_ref, o_ref):
        last = (pl.program_id(0) % tps) == tps - 1

        def strip(r0, win, carry):
            o_ref[pl.ds(r0, STRIP), :] = _taps_t(win, w_ref, K, STRIP).astype(o_ref.dtype)
            return carry

        _strips_next(tm, STRIP, d_ref, jnp.where(last, 0.0, n_ref[...]), strip)

    return _pc(body, grid=(T // tm, C // tc),
               in_specs=[pl.BlockSpec((tm, tc), lambda i, j: (i, ob + j)), _next_spec(T, tm, tc, ob, "ij"),
                         pl.BlockSpec((8, tc), lambda i, j: (0, j))],
               out_specs=pl.BlockSpec((tm, tc), lambda i, j: (i, j)), out_shape=SDS((T, C), out_dtype), name=name)(
                   arr, arr, w8)


def _conv_bwd_w(dview, uview, C, K, S, name):
    darr, dob, T, tm, tc, tps = _conv_geom(dview, C, S)
    uarr, uob, _, _, _, _ = _conv_geom(uview, C, S)

    def body(d_ref, u_ref, p_ref, o_ref):
        i = pl.program_id(1)

        @pl.when(i == 0)
        def _():
            o_ref[...] = jnp.zeros_like(o_ref)

        first = (i % tps) == 0

        def strip(r0, win, carry):
            d = d_ref[pl.ds(r0, STRIP), :]
            sums = [_fold8(d * win[8 - (K - 1 - k):8 - (K - 1 - k) + STRIP]) for k in range(K)] + [_fold8(d)]
            return tuple(c + s for c, s in zip(carry, sums))

        acc = _strips_prev(tm, STRIP, u_ref, jnp.where(first, 0.0, p_ref[...]), strip,
                           tuple(jnp.zeros((8, tc), F32) for _ in range(K + 1)))
        o_ref[...] += _rows8([jnp.sum(a, axis=0, keepdims=True) for a in acc])

    return _pc(body, grid=(C // tc, T // tm),
               in_specs=[pl.BlockSpec((tm, tc), lambda j, i: (i, dob + j)),
                         pl.BlockSpec((tm, tc), lambda j, i: (i, uob + j)), _prev_spec(tm, tc, uob, "ji")],
               out_specs=pl.BlockSpec((8, tc), lambda j, i: (0, j)), out_shape=SDS((8, C), F32), name=name)(
                   darr, uarr, uarr)


def _ffn_act_fwd(uu, w8, b, S, name):
    K, gw = FFN_CONV_K, GLU_W
    T, F2 = uu.shape
    tm, tc = _strip_row_tile(S), 2 * GLU_W
    tps = S // tm

    def body(u_ref, p_ref, w_ref, b_ref, a_ref):
        first = (pl.program_id(0) % tps) == 0

        def strip(r0, win, carry):
            u = _taps(win, w_ref, K, 8, STRIP) + b_ref[...]
            a_ref[pl.ds(r0, STRIP), :] = (_silu(u[:, :gw]) * u[:, gw:]).astype(BF16)
            return carry

        _strips_prev(tm, STRIP, u_ref, jnp.where(first, 0.0, p_ref[...]), strip)

    return _pc(body, grid=(T // tm, F2 // tc),
               in_specs=[pl.BlockSpec((tm, tc), lambda i, j: (i, j)), _prev_spec(tm, tc, 0, "ij"),
                         pl.BlockSpec((8, tc), lambda i, j: (0, j)), pl.BlockSpec((1, tc), lambda i, j: (0, j))],
               out_specs=pl.BlockSpec((tm, gw), lambda i, j: (i, j)), out_shape=SDS((T, F2 // 2), BF16), name=name)(
                   uu, uu, w8, b)


def _ffn_act_bwd(uu, da, w8, b, S, name, side=None):
    K, gw = FFN_CONV_K, GLU_W
    T, F2 = uu.shape
    tm, tc = _strip_row_tile(S), 2 * GLU_W
    tps = S // tm
    last16 = T // 16 - 1

    def body(u_ref, p_ref, n_ref, da_ref, dan_ref, w_ref, b_ref, duu_ref, cw_ref, dabuf):
        i = pl.program_id(1)

        @pl.when(i == 0)
        def _():
            cw_ref[...] = jnp.zeros_like(cw_ref)

        first = (i % tps) == 0
        last = (i % tps) == tps - 1
        dabuf[0:tm, :] = da_ref[...].astype(F32)
        dabuf[tm:tm + 8, :] = jnp.where(last, 0.0, dan_ref[...].astype(F32)[0:8, :])
        fs, ext = FFN_STRIP, FFN_STRIP + 8

        def strip(r0, win, carry):
            shifted = [win[8 - j:8 - j + ext] for j in range(K)]
            u = b_ref[...] + shifted[0] * w_ref[K - 1:K, :]
            for j in range(1, K):
                u = u + shifted[j] * w_ref[K - 1 - j:K - j, :]
            da_ = dabuf[pl.ds(r0, ext), :]
            g, v = u[:, :gw], u[:, gw:]
            du = jnp.concatenate([da_ * v * _dsilu(g), da_ * _silu(g)], axis=1)
            duu_ref[pl.ds(r0, FFN_STRIP), :] = _taps_t(du, w_ref, K, FFN_STRIP).astype(BF16)
            dmain = du[0:FFN_STRIP]
            sums = [_fold8(dmain * shifted[K - 1 - k][0:FFN_STRIP]) for k in range(K)] + [_fold8(dmain)]
            return tuple(c + s for c, s in zip(carry, sums))

        acc = strip(0, jnp.concatenate([jnp.where(first, 0.0, p_ref[...]), u_ref[0:ext, :]], axis=0),
                    tuple(jnp.zeros((8, tc), F32) for _ in range(K + 1)))

        def step(r, c):
            r0 = pl.multiple_of(r * fs, fs)
            return strip(r0, u_ref[pl.ds(pl.multiple_of(r0 - 8, 8), fs + 16), :], c)

        acc = lax.fori_loop(1, tm // fs - 1, step, acc)
        acc = strip(tm - fs, jnp.concatenate([u_ref[tm - ext:tm, :], n_ref[...]], axis=0), acc)
        cw_ref[...] += _rows8([jnp.sum(a, axis=0, keepdims=True) for a in acc])

    return _pc(body, grid=(F2 // tc, T // tm),
               in_specs=[pl.BlockSpec((tm, tc), lambda j, i: (i, j)), _prev_spec(tm, tc, 0, "ji"),
                         _next_spec(T, tm, tc, 0, "ji"), pl.BlockSpec((tm, gw), lambda j, i: (i, j)),
                         pl.BlockSpec((16, gw), lambda j, i: (jnp.minimum((i + 1) * (tm // 16), last16), j)),
                         pl.BlockSpec((8, tc), lambda j, i: (0, j)), pl.BlockSpec((1, tc), lambda j, i: (0, j))],
               out_specs=[pl.BlockSpec((tm, tc), lambda j, i: (i, j)), pl.BlockSpec((8, tc), lambda j, i: (0, j))],
               out_shape=[SDS((T, F2), BF16), SDS((8, F2), F32)], name=name,
               scratch=(pltpu.VMEM((tm + 8, gw), F32),), side=side)(
                   uu, uu, uu, da, da, w8, b)


def _ssd_common(dtc_raw, dtr_raw, hpc, hpr, L):
    dt_c = _softplus(dtc_raw + hpc[0:1, :])
    a_c = -jnp.exp(hpc[1:2, :])
    dt_r = _softplus(dtr_raw + hpr[:, 0:1])
    a_r = -jnp.exp(hpr[:, 1:2])
    li = lax.broadcasted_iota(jnp.int32, (L, L), 0)
    si = lax.broadcasted_iota(jnp.int32, (L, L), 1)
    low = li >= si
    upp = li <= si
    acs_c = _dotx(low, dt_c * a_c, split="b")
    acs_r = _dotx(dt_r * a_r, upp)
    return dt_c, a_c, acs_c, acs_r, low, upp


def _dotx(a, b, split="a", parts=3, dims=NN):
    val, one = (a, b) if split == "a" else (b, a)
    one = one.astype(BF16)
    acc, rem = None, val
    for i in range(parts):
        piece = rem.astype(BF16)
        t = _dot(piece, one, dims) if split == "a" else _dot(one, piece, dims)
        acc = t if acc is None else acc + t
        if i + 1 < parts:
            rem = rem - piece.astype(F32)
    return acc


def _head_maps(R, P, L):
    RP = R * P
    sel = (lax.broadcasted_iota(jnp.int32, (RP, R), 0) // P == lax.broadcasted_iota(jnp.int32, (RP, R), 1)).astype(F32)
    selt = (lax.broadcasted_iota(jnp.int32, (R, RP), 1) // P == lax.broadcasted_iota(jnp.int32, (R, RP), 0)).astype(F32)
    colb = (lax.broadcasted_iota(jnp.int32, (R, R * L), 1) // L == lax.broadcasted_iota(jnp.int32, (R, R * L), 0)).astype(F32)
    return sel, selt, colb


def _pair_diag(mats, rhs_b, R, P):
    lanes = 2 * P
    lo = lax.broadcasted_iota(jnp.int32, (mats[0].shape[0], lanes), 1) < P
    out = []
    for q in range(R // 2):
        rp = rhs_b[:, q * lanes:(q + 1) * lanes]
        out.append(jnp.where(lo, _dot(mats[2 * q], rp), _dot(mats[2 * q + 1], rp)))
    return jnp.concatenate(out, axis=1) if len(out) > 1 else out[0]


def _ssd_specs(pre, off_x, off_b, off_c, G, R, P, nb, nc, rev):
    N, RP, W = N_STATE, R * P, SSD_STEP_CHUNKS * CHUNK
    ns = nc // SSD_STEP_CHUNKS
    assert nc % SSD_STEP_CHUNKS == 0
    cidx = (lambda c: ns - 1 - c) if rev else (lambda c: c)
    xb, bb, cb = off_x // RP, off_b // N, off_c // N
    assert off_x % RP == 0 and off_b % N == 0 and off_c % N == 0
    row = lambda b, c: b * ns + cidx(c)
    return dict(
        x=pl.BlockSpec((W, RP), lambda g, b, c: (row(b, c), xb + g)),
        b=pl.BlockSpec((W, N), lambda g, b, c: (row(b, c), bb + g)),
        c=pl.BlockSpec((W, N), lambda g, b, c: (row(b, c), cb + g)),
        dtc=pl.BlockSpec((None, W, R), lambda g, b, c: (g, row(b, c), 0)),
        dtr=pl.BlockSpec((None, R, W), lambda g, b, c: (g, 0, row(b, c))),
        hpc=pl.BlockSpec((None, 8, R), lambda g, b, c: (g, 0, 0)),
        hpr=pl.BlockSpec((None, R, 8), lambda g, b, c: (g, 0, 0)),
        y=pl.BlockSpec((W, RP), lambda g, b, c: (row(b, c), g)),
        bc=pl.BlockSpec((W, N), lambda g, b, c: (row(b, c), g)),
        hs=pl.BlockSpec((SSD_STEP_CHUNKS, None, N, RP), lambda g, b, c: (row(b, c), g, 0, 0)),
    )


def _ssd_fwd(pre, offs, dtc, dtr, hpc, hpr, G, R, P, S, name, side=None):
    T = pre.shape[0]
    L, N, RP = CHUNK, N_STATE, R * P
    nc, nb = S // L, T // S
    sp = _ssd_specs(pre, *offs, G, R, P, nb, nc, False)

    def body(px_ref, pb_ref, pc_ref, dtc_ref, dtr_ref, hpc_ref, hpr_ref, y_ref, hs_ref, hst):
        @pl.when(pl.program_id(2) == 0)
        def _():
            hst[...] = jnp.zeros_like(hst)

        hpc_ = hpc_ref[...]
        _, selt, colb = _head_maps(R, P, L)
        hp_e = _dotx(hpc_, selt)
        for k in range(SSD_STEP_CHUNKS):
            rs = slice(k * L, (k + 1) * L)
            xs, bm, cm = _silu(px_ref[rs, :]), _silu(pb_ref[rs, :]), _silu(pc_ref[rs, :])
            dt_c, _, acs_c, acs_r, low, _ = _ssd_common(dtc_ref[rs, :], dtr_ref[:, rs], hpc_, hpr_ref[...], L)
            dt_e, a_e = _dotx(dt_c, selt), _dotx(acs_c, selt)
            a_bc = _dotx(acs_c, colb)
            a_last = a_e[L - 1:L, :]
            bb, cb = bm.astype(BF16), cm.astype(BF16)
            gm = _dot(cb, bb, NT)
            hprev = hst[...]
            hprev_b = hprev.astype(BF16)
            hs_ref[k] = hprev_b
            xdt = xs * dt_e
            xdt_b = xdt.astype(BF16)
            ms = []
            for r in range(R):
                dec = jnp.exp(jnp.where(low, a_bc[:, r * L:(r + 1) * L] - acs_r[r:r + 1, :], -jnp.inf))
                ms.append((gm * dec).astype(BF16))
            y_ref[rs, :] = _pair_diag(ms, xdt_b, R, P) + _dot(cb, hprev_b) * jnp.exp(a_e) + hp_e[2:3, :] * xs
            xw = (xdt * jnp.exp(a_last - a_e)).astype(BF16)
            hst[...] = hprev * jnp.exp(a_last) + _dot(bb, xw, TN)

    return _pc(body, grid=(G, nb, nc // SSD_STEP_CHUNKS),
               in_specs=[sp["x"], sp["b"], sp["c"], sp["dtc"], sp["dtr"], sp["hpc"], sp["hpr"]],
               out_specs=[sp["y"], sp["hs"]],
               out_shape=[SDS((T, G * RP), F32), SDS((nb * nc, G, N, RP), BF16)], name=name,
               scratch=(pltpu.VMEM((N, RP), F32),), side=side)(pre, pre, pre, dtc, dtr, hpc, hpr)


def _ssd_bwd(pre, offs, dtc, dtr, hpc, hpr, hs, dy, G, R, P, S, name, side=None):
    T = pre.shape[0]
    L, N, RP = CHUNK, N_STATE, R * P
    nc, nb = S // L, T // S
    sp = _ssd_specs(pre, *offs, G, R, P, nb, nc, True)

    def chunk(px_ref, pb_ref, pc_ref, dtc_ref, dtr_ref, hpc_ref, hpr_ref, hs_ref, dy_ref,
              dpx_ref, dpb_ref, dpc_ref, ddt_ref, hpg_ref, dhst):
        px, pb, pcc = px_ref[...], pb_ref[...], pc_ref[...]
        xs, bm, cm = _silu(px), _silu(pb), _silu(pcc)
        hpc_ = hpc_ref[...]
        dtc_raw = dtc_ref[...]
        dt_c, a_c, acs_c, acs_r, low, upp = _ssd_common(dtc_raw, dtr_ref[...], hpc_, hpr_ref[...], L)
        sel, selt, colb = _head_maps(R, P, L)
        dt_e, a_e, hp_e = _dotx(dt_c, selt), _dotx(acs_c, selt), _dotx(hpc_, selt)
        a_bc = _dotx(acs_c, colb)
        a_last = a_e[L - 1:L, :]
        e_e, w_e = jnp.exp(a_e), jnp.exp(a_last - a_e)
        bb, cb = bm.astype(BF16), cm.astype(BF16)
        gm = _dot(cb, bb, NT)
        gmt = _dot(bb, cb, NT)
        hprev = hs_ref[...]
        dhn = dhst[...]
        dhn_b = dhn.astype(BF16)
        dy = dy_ref[...]
        dy_b = dy.astype(BF16)
        xdt = xs * dt_e
        xdt_b = xdt.astype(BF16)
        yoff = _dot(cb, hprev) * e_e
        dye_b = (dy * e_e).astype(BF16)
        dcm = _dot(dye_b, hprev, NT)
        dhst[...] = _dot(cb, dye_b, TN) + jnp.exp(a_last) * dhn
        dxdt_st = _dot(bb, dhn_b) * w_e
        dbm = _dot((xdt * w_e).astype(BF16), dhn_b, NT)
        lanes = 2 * P
        lo = lax.broadcasted_iota(jnp.int32, (L, lanes), 1) < P
        dg = jnp.zeros((L, L), F32)
        es, css = [], []
        for r in range(R):
            col_b, row = a_bc[:, r * L:(r + 1) * L], acs_r[r:r + 1, :]
            dec = jnp.exp(jnp.where(low, col_b - row, -jnp.inf))
            q = r // 2
            dyp = dy_b[:, q * lanes:(q + 1) * lanes]
            dyp = jnp.where(lo if r % 2 == 0 else ~lo, dyp, jnp.zeros_like(dyp))
            dm = _dot(dyp, xdt_b[:, q * lanes:(q + 1) * lanes], NT)
            dg = dg + dm * dec
            e = dm * (gm * dec)
            es.append(e)
            css.append(jnp.sum(e, axis=0, keepdims=True))
        dgb = dg.astype(BF16)
        dcm = dcm + _dot(dgb, bb)
        dbm = dbm + _dot(dgb, cb, TN)
        colbt = (lax.broadcasted_iota(jnp.int32, (R * L, R), 0) // L
                 == lax.broadcasted_iota(jnp.int32, (R * L, R), 1)).astype(F32)
        eye = (lax.broadcasted_iota(jnp.int32, (R, R), 0) == lax.broadcasted_iota(jnp.int32, (R, R), 1)).astype(F32)
        row_sums = _dotx(jnp.concatenate(es, axis=1), colbt)
        col_sums = _dotx(jnp.concatenate(css, axis=0), eye, dims=TN)
        mts = []
        for r in range(R):
            dect = jnp.exp(jnp.where(upp, acs_r[r:r + 1, :] - a_bc[:, r * L:(r + 1) * L], -jnp.inf))
            mts.append((gmt * dect).astype(BF16))
        dxdt = _pair_diag(mts, dy_b, R, P) + dxdt_st
        q_st = _dotx(xdt * dxdt_st, sel, parts=1)
        da = row_sums - col_sums + _dotx(dy * yoff, sel, parts=1) - q_st
        hh = jnp.sum(_dotx(dhn * hprev.astype(F32), sel, parts=1), axis=0, keepdims=True)
        da_last = jnp.exp(acs_c[L - 1:L, :]) * hh + jnp.sum(q_st, axis=0, keepdims=True)
        rowi = lax.broadcasted_iota(jnp.int32, (L, R), 0)
        da = da + jnp.where(rowi == L - 1, da_last, 0.0)
        dpx_ref[...] = (dxdt * dt_e + hp_e[2:3, :] * dy) * _dsilu(px)
        dpb_ref[...] = dbm * _dsilu(pb)
        dpc_ref[...] = dcm * _dsilu(pcc)
        dadt = _dotx(upp, da, split="b")
        ddt = _dotx(dxdt * xs, sel, parts=1) + dadt * a_c
        ddt_raw = ddt * jax.nn.sigmoid(dtc_raw + hpc_[0:1, :])
        ddt_ref[...] = ddt_raw
        d_a = jnp.sum(dadt * dt_c, axis=0, keepdims=True)
        d_d = jnp.sum(_dotx(dy * xs, sel, parts=1), axis=0, keepdims=True)
        rows = [jnp.sum(ddt_raw, axis=0, keepdims=True), d_a * a_c, d_d, jnp.zeros((5, R), F32)]
        hpg_ref[...] += jnp.concatenate(rows, axis=0)

    def body(px_ref, pb_ref, pc_ref, dtc_ref, dtr_ref, hpc_ref, hpr_ref, hs_ref, dy_ref,
             dpx_ref, dpb_ref, dpc_ref, ddt_ref, hpg_ref, dhst):
        bi, ci = pl.program_id(1), pl.program_id(2)

        @pl.when(ci == 0)
        def _():
            dhst[...] = jnp.zeros_like(dhst)

        @pl.when((bi == 0) & (ci == 0))
        def _():
            hpg_ref[...] = jnp.zeros_like(hpg_ref)

        for k in reversed(range(SSD_STEP_CHUNKS)):
            rs = pl.ds(k * L, L)
            chunk(px_ref.at[rs, :], pb_ref.at[rs, :], pc_ref.at[rs, :], dtc_ref.at[rs, :], dtr_ref.at[:, rs], hpc_ref,
                  hpr_ref, hs_ref.at[k], dy_ref.at[rs, :], dpx_ref.at[rs, :], dpb_ref.at[rs, :], dpc_ref.at[rs, :],
                  ddt_ref.at[rs, :], hpg_ref, dhst)

    return _pc(body, grid=(G, nb, nc // SSD_STEP_CHUNKS),
               in_specs=[sp["x"], sp["b"], sp["c"], sp["dtc"], sp["dtr"], sp["hpc"], sp["hpr"], sp["hs"], sp["y"]],
               out_specs=[sp["y"], sp["bc"], sp["bc"], sp["dtc"], pl.BlockSpec((None, 8, R), lambda g, b, c: (g, 0, 0))],
               out_shape=[SDS((T, G * RP), F32), SDS((T, G * N), F32), SDS((T, G * N), F32), SDS((G, T, R), F32),
                          SDS((G, 8, R), F32)], name=name,
               scratch=(pltpu.VMEM((N, RP), F32),), side=side)(pre, pre, pre, dtc, dtr, hpc, hpr, hs, dy)


def _gate_norm_fwd(y, zview, ng, G, S, name):
    T, DI = y.shape
    zarr, zoff = zview
    gw = DI // G
    tm = _tile(S, 1024, 8)
    zb = zoff // gw
    assert zoff % gw == 0

    def body(y_ref, z_ref, g_ref, o_ref):
        yg = y_ref[...] * _silu(z_ref[...])
        r = lax.rsqrt(jnp.mean(yg * yg, axis=-1, keepdims=True) + EPS)
        o_ref[...] = (yg * r * g_ref[...]).astype(BF16)

    return _pc(body, grid=(T // tm, G),
               in_specs=[pl.BlockSpec((tm, gw), lambda i, g: (i, g)), pl.BlockSpec((tm, gw), lambda i, g: (i, zb + g)),
                         pl.BlockSpec((1, gw), lambda i, g: (0, g))],
               out_specs=pl.BlockSpec((tm, gw), lambda i, g: (i, g)), out_shape=SDS((T, DI), BF16), name=name)(y, zarr, ng)


def _gate_norm_bwd(y, zview, ng, dyn, G, S, name):
    T, DI = y.shape
    zarr, zoff = zview
    gw = DI // G
    tm = _tile(S, 1024, 8)
    zb = zoff // gw

    def body(y_ref, z_ref, g_ref, d_ref, dy_ref, dz_ref, dg_ref):
        @pl.when(pl.program_id(1) == 0)
        def _():
            dg_ref[...] = jnp.zeros_like(dg_ref)

        y_, z, d = y_ref[...], z_ref[...], d_ref[...]
        sz = _silu(z)
        yg = y_ * sz
        r = lax.rsqrt(jnp.mean(yg * yg, axis=-1, keepdims=True) + EPS)
        n = yg * r
        dn = d * g_ref[...]
        dyg = r * (dn - n * jnp.mean(dn * n, axis=-1, keepdims=True))
        dy_ref[...] = dyg * sz
        dz_ref[...] = (dyg * y_ * _dsilu(z)).astype(BF16)
        dg_ref[...] += _bsum(jnp.sum(d * n, axis=0, keepdims=True))

    return _pc(body, grid=(G, T // tm),
               in_specs=[pl.BlockSpec((tm, gw), lambda g, i: (i, g)), pl.BlockSpec((tm, gw), lambda g, i: (i, zb + g)),
                         pl.BlockSpec((1, gw), lambda g, i: (0, g)), pl.BlockSpec((tm, gw), lambda g, i: (i, g))],
               out_specs=[pl.BlockSpec((tm, gw), lambda g, i: (i, g)), pl.BlockSpec((tm, gw), lambda g, i: (i, g)),
                          pl.BlockSpec((8, gw), lambda g, i: (0, g))],
               out_shape=[SDS((T, DI), F32), SDS((T, DI), BF16), SDS((8, DI), F32)], name=name)(y, zarr, ng, dyn)


def _shortconv_fwd(proj, off_b, off_c, off_h, C, w8, S, name):
    K = SC_CONV_K
    _, ob, T, tm, tc, tps = _conv_geom((proj, off_b), C, S, cap=1024)
    oc, oh = off_c // tc, off_h // tc

    def body(b_ref, c_ref, h_ref, cp_ref, hp_ref, w_ref, o_ref, buf):
        first = (pl.program_id(0) % tps) == 0
        buf[0:8, :] = jnp.where(first, 0.0, cp_ref[...] * hp_ref[...])
        buf[8:, :] = c_ref[...] * h_ref[...]

        def strip(r0, carry):
            conv = _taps(buf[pl.ds(r0, STRIP + 8), :], w_ref, K, 8, STRIP)
            o_ref[pl.ds(r0, STRIP), :] = (b_ref[pl.ds(r0, STRIP), :] * conv).astype(BF16)
            return carry

        _strips(tm, strip)

    blk = lambda o: pl.BlockSpec((tm, tc), lambda i, j: (i, o + j))
    return _pc(body, grid=(T // tm, C // tc),
               in_specs=[blk(ob), blk(oc), blk(oh), _prev_spec(tm, tc, oc, "ij"), _prev_spec(tm, tc, oh, "ij"),
                         pl.BlockSpec((8, tc), lambda i, j: (0, j))],
               out_specs=pl.BlockSpec((tm, tc), lambda i, j: (i, j)), out_shape=SDS((T, C), BF16), name=name,
               scratch=(pltpu.VMEM((tm + 8, tc), F32),))(proj, proj, proj, proj, proj, w8)


def _shortconv_bwd(proj, off_b, off_c, off_h, C, w8, dsc, S, name):
    K = SC_CONV_K
    _, ob, T, tm, tc, tps = _conv_geom((proj, off_b), C, S, cap=1024)
    oc, oh = off_c // tc, off_h // tc

    def body(b_ref, c_ref, h_ref, cp_ref, hp_ref, bn_ref, d_ref, dn_ref, w_ref,
             db_ref, dc_ref, dh_ref, dw_ref, buf, buf2):
        i = pl.program_id(1)

        @pl.when(i == 0)
        def _():
            dw_ref[...] = jnp.zeros_like(dw_ref)

        first = (i % tps) == 0
        last = (i % tps) == tps - 1
        buf[0:8, :] = jnp.where(first, 0.0, cp_ref[...] * hp_ref[...])
        buf[8:, :] = c_ref[...] * h_ref[...]
        buf2[0:tm, :] = d_ref[...] * b_ref[...]
        buf2[tm:tm + 8, :] = jnp.where(last, 0.0, dn_ref[...] * bn_ref[...])

        def strip(r0, carry):
            rows = pl.ds(r0, STRIP)
            vwin = buf[pl.ds(r0, STRIP + 8), :]
            vs = [vwin[8 - j:8 - j + STRIP] for j in range(K)]
            conv = vs[0] * w_ref[K - 1:K, :]
            for j in range(1, K):
                conv = conv + vs[j] * w_ref[K - 1 - j:K - j, :]
            db_ref[rows, :] = (d_ref[rows, :] * conv).astype(BF16)
            dwin = buf2[pl.ds(r0, STRIP + 8), :]
            dv = _taps_t(dwin, w_ref, K, STRIP)
            dc_ref[rows, :] = (dv * h_ref[rows, :]).astype(BF16)
            dh_ref[rows, :] = (dv * c_ref[rows, :]).astype(BF16)
            dconv = dwin[0:STRIP]
            sums = [_fold8(dconv * vs[K - 1 - k]) for k in range(K)]
            return tuple(c + s for c, s in zip(carry, sums))

        acc = _strips(tm, strip, tuple(jnp.zeros((8, tc), F32) for _ in range(K)))
        dw_ref[...] += _rows8([jnp.sum(a, axis=0, keepdims=True) for a in acc])

    blk = lambda o: pl.BlockSpec((tm, tc), lambda j, i: (i, o + j))
    out = pl.BlockSpec((tm, tc), lambda j, i: (i, j))
    return _pc(body, grid=(C // tc, T // tm),
               in_specs=[blk(ob), blk(oc), blk(oh), _prev_spec(tm, tc, oc, "ji"), _prev_spec(tm, tc, oh, "ji"),
                         _next_spec(T, tm, tc, ob, "ji"), blk(0), _next_spec(T, tm, tc, 0, "ji"),
                         pl.BlockSpec((8, tc), lambda j, i: (0, j))],
               out_specs=[out, out, out, pl.BlockSpec((8, tc), lambda j, i: (0, j))],
               out_shape=[SDS((T, C), BF16)] * 3 + [SDS((8, C), F32)], name=name,
               scratch=(pltpu.VMEM((tm + 8, tc), F32), pltpu.VMEM((tm + 8, tc), F32)))(
                   proj, proj, proj, proj, proj, proj, dsc, dsc, w8)


def _merge_fwd(proj, off_g1, off_g2, y1, y2, S, name):
    T, D = y1.shape
    tm = _row_tile(S)
    o1, o2 = off_g1 // D, off_g2 // D
    assert off_g1 % D == 0 and off_g2 % D == 0

    def body(g1_ref, g2_ref, y1_ref, y2_ref, o_ref):
        def strip(r0, carry):
            rows = pl.ds(r0, STRIP)
            o_ref[rows, :] = (jax.nn.sigmoid(g1_ref[rows, :]) * y1_ref[rows, :]
                              + jax.nn.sigmoid(g2_ref[rows, :]) * y2_ref[rows, :]).astype(BF16)
            return carry

        _strips(tm, strip)

    row = pl.BlockSpec((tm, D), lambda i: (i, 0))
    return _pc(body, grid=(T // tm,),
               in_specs=[pl.BlockSpec((tm, D), lambda i: (i, o1)), pl.BlockSpec((tm, D), lambda i: (i, o2)), row, row],
               out_specs=row, out_shape=SDS((T, D), BF16), name=name)(proj, proj, y1, y2)


def _merge_bwd(proj, off_g1, off_g2, y1, y2, dm, S, name):
    T, D = y1.shape
    tm = _row_tile(S)
    o1, o2 = off_g1 // D, off_g2 // D

    def body(g1_ref, g2_ref, y1_ref, y2_ref, d_ref, dy1_ref, dy2_ref, dg1_ref, dg2_ref):
        def strip(r0, carry):
            rows = pl.ds(r0, STRIP)
            d = d_ref[rows, :]
            s1, s2 = jax.nn.sigmoid(g1_ref[rows, :]), jax.nn.sigmoid(g2_ref[rows, :])
            dy1_ref[rows, :] = (d * s1).astype(BF16)
            dy2_ref[rows, :] = (d * s2).astype(BF16)
            dg1_ref[rows, :] = (d * y1_ref[rows, :] * s1 * (1.0 - s1)).astype(BF16)
            dg2_ref[rows, :] = (d * y2_ref[rows, :] * s2 * (1.0 - s2)).astype(BF16)
            return carry

        _strips(tm, strip)

    row = pl.BlockSpec((tm, D), lambda i: (i, 0))
    return _pc(body, grid=(T // tm,),
               in_specs=[pl.BlockSpec((tm, D), lambda i: (i, o1)), pl.BlockSpec((tm, D), lambda i: (i, o2)), row, row, row],
               out_specs=[row] * 4, out_shape=[SDS((T, D), BF16)] * 4, name=name)(proj, proj, y1, y2, dm)


def _pad8(w):
    return jnp.pad(w, ((0, 8 - w.shape[0]), (0, 0)))


def _dims(w):
    D = w["mix_pre_g"].shape[-1]
    DI = w["ssd_norm_g"].shape[-1]
    H = w["ssd_dt_bias"].shape[-1]
    conv_dim = w["ssd_conv_b"].shape[-1]
    G = (conv_dim - DI) // (2 * N_STATE)
    F = w["w_down"].shape[0]
    return dict(D=D, DI=DI, H=H, P=DI // H, G=G, R=H // G, GN=G * N_STATE, CD=conv_dim, F=F)


def _proj_layout(d):
    D, DI, CD, H = d["D"], d["DI"], d["CD"], d["H"]
    o = dict(z=0, xbc=DI, scb=DI + CD, scc=DI + CD + D, sch=DI + CD + 2 * D, g1=DI + CD + 3 * D, g2=DI + CD + 4 * D,
             dt=DI + CD + 5 * D)
    o["sb"] = math.gcd(SEG_BLK, D, DI, d["GN"])
    assert o["sb"] % 128 == 0 and H <= o["sb"]
    o["np"] = o["dt"] + o["sb"]
    return o


def _glu_perm(a, F, inverse=False):
    lead = a.shape[:-1]
    nb = F // GLU_W
    if not inverse:
        return a.reshape(*lead, 2, nb, GLU_W).swapaxes(-3, -2).reshape(*lead, 2 * F)
    return a.reshape(*lead, nb, 2, GLU_W).swapaxes(-3, -2).reshape(*lead, 2 * F)


def _glu_perm_rows(a, F, inverse=False):
    nb, D = F // GLU_W, a.shape[1]
    shape = (nb, 2, GLU_W, D) if inverse else (2, nb, GLU_W, D)
    return a.reshape(shape).swapaxes(0, 1).reshape(2 * F, D)


def _prep_layer(w):
    d = _dims(w)
    D, DI, CD, H, G, R, F = d["D"], d["DI"], d["CD"], d["H"], d["G"], d["R"], d["F"]
    lay = _proj_layout(d)
    w_in = w["w_in"]
    used = lay["dt"] + H
    wcat = jnp.concatenate([w_in[:DI + CD], w_in[DI + CD + H:], w_in[DI + CD:DI + CD + H],
                            jnp.zeros((lay["np"] - used, D), w_in.dtype)], axis=0)
    hp = jnp.stack([w["ssd_dt_bias"], w["ssd_a_log"], w["ssd_d"]], 0).astype(F32)
    hpc = jnp.pad(hp.reshape(3, G, R).transpose(1, 0, 2), ((0, 0), (0, 5), (0, 0)))
    hpr = jnp.pad(hp[:2].reshape(2, G, R).transpose(1, 2, 0), ((0, 0), (0, 0), (0, 6)))
    row = lambda v: v.reshape(1, -1).astype(F32)
    return dict(
        d=d, lay=lay, ada_w=w["ada_w"].astype(BF16), ada_b=row(w["ada_b"]),
        mix_pre_g=row(w["mix_pre_g"]), mix_post_g=row(w["mix_post_g"]), wcat=wcat.astype(BF16),
        ssd_conv_w=_pad8(w["ssd_conv_w"].astype(F32)), ssd_conv_b=row(w["ssd_conv_b"]), hpc=hpc, hpr=hpr,
        ssd_norm_g=row(w["ssd_norm_g"]), w_ssd_out=w["w_ssd_out"].astype(BF16),
        sc_conv_w=_pad8(w["sc_conv_w"].astype(F32)), w_sc_out=w["w_sc_out"].astype(BF16), w_o=w["w_o"].astype(BF16),
        ffn_pre_g=row(w["ffn_pre_g"]), ffn_post_g=row(w["ffn_post_g"]),
        w_up=_glu_perm_rows(w["w_up"], F).astype(BF16), ffn_conv_w=_pad8(_glu_perm(w["ffn_conv_w"].astype(F32), F)),
        ffn_conv_b=_glu_perm(row(w["ffn_conv_b"]), F), w_down=w["w_down"].astype(BF16))


def _dt_layouts(proj, lay, d):
    T = proj.shape[0]
    dt = proj[:, lay["dt"]:lay["dt"] + d["H"]].reshape(T, d["G"], d["R"])
    return dt.transpose(1, 0, 2), dt.transpose(1, 2, 0)


def _layer_fwd(x, c8, p, S, li, gather=None):
    d, lay = p["d"], p["lay"]
    D, DI, G, R, P, GN, CD = d["D"], d["DI"], d["G"], d["R"], d["P"], d["GN"], d["CD"]
    nb = x.shape[0] // S
    nm = lambda s: f"l{li}_{s}"
    mod, cact = _modk(c8, p["ada_w"], p["ada_b"], nm("mod"))
    mod3 = mod[:nb].reshape(nb, 1, 6 * D)
    h = _norm_mod(x, p["mix_pre_g"], mod3, 1, 0, S, nm("norm1"))
    proj = _mm(h, p["wcat"], "nt", F32, nm("mm_in"), caps=(1024, 1536, 2048))
    pre = _conv_fwd((proj, lay["xbc"]), CD, p["ssd_conv_w"], p["ssd_conv_b"], SSD_CONV_K, S, nm("ssdconv"))
    dtc, dtr = _dt_layouts(proj, lay, d)
    offs = (0, DI, DI + GN)
    gathered = None
    if gather is None:
        y, hs = _ssd_fwd(pre, offs, dtc, dtr, p["hpc"], p["hpr"], G, R, P, S, nm("ssd"))
    else:
        (y, hs), (gathered,) = _ssd_fwd(pre, offs, dtc, dtr, p["hpc"], p["hpr"], G, R, P, S, nm("ssd"),
                                        side=_side_gather_direct(gather))
    yn = _gate_norm_fwd(y, (proj, lay["z"]), p["ssd_norm_g"], G, S, nm("gnorm"))
    sc = _shortconv_fwd(proj, lay["scb"], lay["scc"], lay["sch"], D, p["sc_conv_w"], S, nm("sconv"))
    if gather is None:
        y_ssd = _mm(yn, p["w_ssd_out"], "nn", F32, nm("mm_ssdout"))
    else:
        y_ssd, (gathered,) = _mm(yn, p["w_ssd_out"], "nn", F32, nm("mm_ssdout"), side=_side_gather_forward(gathered))
    y_sc = _mm(sc, p["w_sc_out"], "nn", F32, nm("mm_scout"))
    m = _merge_fwd(proj, lay["g1"], lay["g2"], y_ssd, y_sc, S, nm("merge"))
    mix = _mm(m, p["w_o"], "nn", F32, nm("mm_o"))
    x1 = _resid_post(x, mix, mod3, 2, p["mix_post_g"], S, nm("post1"))
    h2 = _norm_mod(x1, p["ffn_pre_g"], mod3, 4, 3, S, nm("norm2"))
    uu = _mm(h2, p["w_up"], "nt", F32, nm("mm_up"), caps=(1024, 1408, 2048))
    a = _ffn_act_fwd(uu, p["ffn_conv_w"], p["ffn_conv_b"], S, nm("ffnact"))
    f = _mm(a, p["w_down"], "nn", F32, nm("mm_down"), caps=(1024, 1024, 1408))
    x2 = _resid_post(x1, f, mod3, 5, p["ffn_post_g"], S, nm("post2"))
    saved = dict(x=x, h=h, proj=proj, pre=pre, dtc=dtc, dtr=dtr, y=y, hs=hs, yn=yn, sc=sc, y_ssd=y_ssd, y_sc=y_sc,
                 m=m, mix=mix, x1=x1, h2=h2, uu=uu, a=a, f=f, mod3=mod3, cact=cact)
    return x2, saved, gathered


def _seq_sum(acc, nb):
    return acc.reshape(nb, 8, -1)[:, 0, :]


def _layer_bwd(dx2, p, s, S, li, chip_sums=None, early=None):
    d, lay = p["d"], p["lay"]
    D, DI, G, R, P, GN, CD, H, F = d["D"], d["DI"], d["G"], d["R"], d["P"], d["GN"], d["CD"], d["H"], d["F"]
    nb = dx2.shape[0] // S
    nm = lambda t: f"l{li}_{t}"
    mod3 = s["mod3"]
    g = {}
    exchanged = None
    df, dgt2, dpg2 = _post_bwd(s["f"], mod3, 5, p["ffn_post_g"], dx2, S, nm("post2_b"))
    g["ffn_post_g"] = dpg2[0]
    da = _mm(df, p["w_down"], "nt", BF16, nm("mm_down_bi"), caps=(1024, 1408, 2048))
    g["w_down"] = _mm(s["a"], df, "tn", WGRAD,nm("mm_down_bw"), caps=(1408, 1024, 1024))
    if chip_sums is None:
        duu, cw = _ffn_act_bwd(s["uu"], da, p["ffn_conv_w"], p["ffn_conv_b"], S, nm("ffnact_b"))
    else:
        (duu, cw), (exchanged,) = _ffn_act_bwd(s["uu"], da, p["ffn_conv_w"], p["ffn_conv_b"], S, nm("ffnact_b"),
                                               side=_side_chip_exchange(chip_sums))
    g["ffn_conv_w"] = _glu_perm(cw[:FFN_CONV_K], F, inverse=True)
    g["ffn_conv_b"] = _glu_perm(cw[FFN_CONV_K], F, inverse=True)
    dh2 = _mm(duu, p["w_up"], "nn", F32, nm("mm_up_bi"), caps=(1024, 1024, 2816))
    g["w_up"] = _glu_perm_rows(_mm(duu, s["h2"], "tn", WGRAD,nm("mm_up_bw"), caps=(1408, 1024, 1024)), F, inverse=True)
    dx1, dg2, dsc2, dsh2 = _pre_bwd(s["x1"], p["ffn_pre_g"], mod3, 4, dh2, dx2, S, nm("norm2_b"))
    g["ffn_pre_g"] = dg2[0]
    dmix, dgt1, dpg1 = _post_bwd(s["mix"], mod3, 2, p["mix_post_g"], dx1, S, nm("post1_b"))
    g["mix_post_g"] = dpg1[0]
    dm = _mm(dmix, p["w_o"], "nt", F32, nm("mm_o_bi"))
    g["w_o"] = _mm(s["m"], dmix, "tn", WGRAD,nm("mm_o_bw"))
    proj = s["proj"]
    dy_ssd, dy_sc, dg1, dg2_ = _merge_bwd(proj, lay["g1"], lay["g2"], s["y_ssd"], s["y_sc"], dm, S, nm("merge_b"))
    dyn = _mm(dy_ssd, p["w_ssd_out"], "nt", F32, nm("mm_ssdout_bi"))
    g["w_ssd_out"] = _mm(s["yn"], dy_ssd, "tn", WGRAD,nm("mm_ssdout_bw"))
    dsc = _mm(dy_sc, p["w_sc_out"], "nt", F32, nm("mm_scout_bi"))
    g["w_sc_out"] = _mm(s["sc"], dy_sc, "tn", WGRAD,nm("mm_scout_bw"))
    dscb, dscc, dsch, scw = _shortconv_bwd(proj, lay["scb"], lay["scc"], lay["sch"], D, p["sc_conv_w"], dsc, S, nm("sconv_b"))
    g["sc_conv_w"] = scw[:SC_CONV_K]
    dy, dz, dng = _gate_norm_bwd(s["y"], (proj, lay["z"]), p["ssd_norm_g"], dyn, G, S, nm("gnorm_b"))
    g["ssd_norm_g"] = dng[0]
    offs = (0, DI, DI + GN)
    early_side = None if early is None else _side_chip_exchange(early(g))
    res = _ssd_bwd(s["pre"], offs, s["dtc"], s["dtr"], p["hpc"], p["hpr"], s["hs"], dy, G, R, P, S, nm("ssd_b"),
                   side=early_side)
    (dpx, dpb, dpc, ddt, hpg), early_got = res if early is not None else (res, None)
    g["ssd_dt_bias"], g["ssd_a_log"], g["ssd_d"] = hpg[:, 0, :].reshape(H), hpg[:, 1, :].reshape(H), hpg[:, 2, :].reshape(H)
    cws, dxbc = [], []
    for name, darr, off, C in (("x", dpx, 0, DI), ("b", dpb, DI, GN), ("c", dpc, DI + GN, GN)):
        w8 = p["ssd_conv_w"][:, off:off + C]
        cws.append(_conv_bwd_w((darr, 0), (proj, lay["xbc"] + off), C, SSD_CONV_K, S, nm(f"ssdconv_bw_{name}")))
        dxbc.append(_conv_bwd_in((darr, 0), C, w8, SSD_CONV_K, S, BF16, nm(f"ssdconv_bi_{name}")))
    cws = jnp.concatenate(cws, axis=1)
    g["ssd_conv_w"], g["ssd_conv_b"] = cws[:SSD_CONV_K], cws[SSD_CONV_K]
    T = dx2.shape[0]
    ddt_t = jnp.pad(ddt.transpose(1, 0, 2).reshape(T, H).astype(BF16), ((0, 0), (0, lay["sb"] - H)))
    dproj = [dz] + dxbc + [dscb, dscc, dsch, dg1, dg2_, ddt_t]
    dh = _mm_seg(dproj, p["wcat"], "nn", F32, nm("mm_in_bi"), lay["sb"])
    dwcat = _mm_seg(dproj, s["h"], "tn", WGRAD, nm("mm_in_bw"), lay["sb"], tk=1024)
    o = lay
    g["w_in"] = jnp.concatenate([dwcat[o["z"]:o["scb"]], dwcat[o["dt"]:o["dt"] + H], dwcat[o["scb"]:o["dt"]]], axis=0)
    dx, dg1_, dsc1, dsh1 = _pre_bwd(s["x"], p["mix_pre_g"], mod3, 1, dh, dx1, S, nm("norm1_b"))
    g["mix_pre_g"] = dg1_[0]
    dmod = jnp.concatenate([_seq_sum(t, nb) for t in (dsh1, dsc1, dgt1, dsh2, dsc2, dgt2)], axis=1)
    dmod8 = jnp.pad(dmod, ((0, MOD_ROWS - nb), (0, 0)))
    g["ada_b"] = _colsum(dmod8, nm("adab"))
    g["ada_w"] = _mm(dmod8, s["cact"], "tn", WGRAD, nm("mm_ada_bw"), caps=(1536, 1024, 2048))
    return dx, g, exchanged, None if early_got is None else early_got[0]


def _colsum(a8, name):
    rows, C = a8.shape
    tc = _tile(C, 2048)

    def body(a_ref, o_ref):
        o_ref[...] = _bsum(jnp.sum(a_ref[...], axis=0, keepdims=True))

    return _pc(body, grid=(C // tc,), in_specs=[pl.BlockSpec((rows, tc), lambda j: (0, j))],
               out_specs=pl.BlockSpec((8, tc), lambda j: (0, j)), out_shape=SDS((8, C), F32), name=name)(a8)[0]


def _adam(gs, w, m, v, name):
    ns, R, W = gs.shape
    tr = _tile(R, 256, 8)

    def body(g_ref, w_ref, m_ref, v_ref, go_ref, d_ref, mo_ref, vo_ref):
        g = g_ref[0].astype(F32)
        for k in range(1, ns):
            g = g + g_ref[k].astype(F32)
        go_ref[...] = g
        d_ref[...], mo_ref[...], vo_ref[...] = _adam_update(g, w_ref[...], m_ref[...], v_ref[...])

    row = pl.BlockSpec((tr, W), lambda i: (i, 0))
    return _pc(body, grid=(R // tr,), in_specs=[pl.BlockSpec((ns, tr, W), lambda i: (0, i, 0)), row, row, row],
               out_specs=[row] * 4, out_shape=[SDS((R, W), F32)] * 4, name=name)(gs, w, m, v)


def _adam_update(g, w, m, v):
    c1 = 1.0 / (1.0 - ADAM_B1 ** ADAM_STEP)
    c2 = 1.0 / (1.0 - ADAM_B2 ** ADAM_STEP)
    m_ = ADAM_B1 * m + (1.0 - ADAM_B1) * g
    v_ = ADAM_B2 * v + (1.0 - ADAM_B2) * (g * g)
    return -ADAM_LR * ((m_ * c1) / (jnp.sqrt(v_ * c2) + ADAM_EPS) + ADAM_WD * w), m_, v_


def _adam_nat(g, w, m, v, name):
    depth, a, b = w.shape
    tr = _tile(a, 256, 8)

    def body(g_ref, w_ref, m_ref, v_ref, d_ref, mo_ref, vo_ref):
        d_ref[...], mo_ref[...], vo_ref[...] = _adam_update(g_ref[...], w_ref[...], m_ref[...], v_ref[...])

    blk = pl.BlockSpec((None, tr, b), lambda l, i: (l, i, 0))
    return _pc(body, grid=(depth, a // tr), in_specs=[blk] * 4, out_specs=[blk] * 3,
               out_shape=[SDS(w.shape, F32)] * 3, name=name)(g, w, m, v)


def _adam_mid(g, w, m, v, name):
    b, depth, a = w.shape
    tr = 128

    def body(g_ref, w_ref, m_ref, v_ref, d_ref, mo_ref, vo_ref):
        d_ref[...], mo_ref[...], vo_ref[...] = _adam_update(g_ref[...], w_ref[...], m_ref[...], v_ref[...])

    blk = pl.BlockSpec((tr, depth, a), lambda i: (i, 0, 0))
    return _pc(body, grid=(pl.cdiv(b, tr),), in_specs=[blk] * 4, out_specs=[blk] * 3,
               out_shape=[SDS(w.shape, F32)] * 3, name=name)(g, w, m, v)


def _sum_chips(gs, name):
    ns, R, W = gs.shape
    tr = _tile(R, 256, 16)

    def body(g_ref, o_ref):
        acc = g_ref[0].astype(F32)
        for k in range(1, ns):
            acc = acc + g_ref[k].astype(F32)
        o_ref[...] = acc

    return _pc(body, grid=(R // tr,), in_specs=[pl.BlockSpec((ns, tr, W), lambda i: (0, i, 0))],
               out_specs=pl.BlockSpec((tr, W), lambda i: (i, 0)), out_shape=SDS((R, W), F32), name=name)(gs)


HBM_SPEC = pl.BlockSpec(memory_space=pltpu.HBM)
VMEM_SPEC = pl.BlockSpec(memory_space=pltpu.VMEM)


def _dev():
    return lax.axis_index("x"), lax.axis_index("y"), lax.axis_index("c")


def _allgather_big(loc, name):
    R, W = loc.shape

    def body(x_ref, out_ref, send_sems, recv_sems, local_sem):
        x, y, c = _dev()
        me, sibling = (x, y, c), (x, y, 1 - c)
        chips = [(1 - x, y), (x, 1 - y), (1 - x, 1 - y)]

        def slab(px, py, pc):
            return out_ref.at[4 * px + 2 * py + pc]

        def copy(k, block, to, src=None):
            return pltpu.make_async_remote_copy(
                src_ref=slab(*block) if src is None else src, dst_ref=slab(*block),
                send_sem=send_sems.at[k], recv_sem=recv_sems.at[k], device_id=to, device_id_type=MESH)

        mine = pltpu.make_async_copy(x_ref, slab(*me), local_sem)
        mine.start()
        first = [copy(0, me, sibling, src=x_ref)]
        first += [copy(1 + j, me, (*chip, c), src=x_ref) for j, chip in enumerate(chips)]
        for cp in first:
            cp.start()
        passed = [copy(4 + j, (*chip, c), sibling) for j, chip in enumerate(chips)]
        for j, chip in enumerate(chips):
            copy(1 + j, (*chip, c), me).wait_recv()
            passed[j].start()
        copy(0, sibling, me).wait_recv()
        for j, chip in enumerate(chips):
            copy(4 + j, (*chip, 1 - c), me).wait_recv()
        for cp in first + passed:
            cp.wait_send()
        mine.wait()

    return pl.pallas_call(
        body, out_shape=SDS((N_DEV, R, W), loc.dtype), in_specs=[HBM_SPEC], out_specs=HBM_SPEC,
        scratch_shapes=[pltpu.SemaphoreType.DMA((7,)), pltpu.SemaphoreType.DMA((7,)), pltpu.SemaphoreType.DMA],
        name=name)(loc)


def _dma_sems(n):
    return (pltpu.SemaphoreType.DMA((n,)), pltpu.SemaphoreType.DMA((n,)), pltpu.SemaphoreType.DMA)


def _side_gather_direct(loc):
    R, W = loc.shape

    def copies(ins, outs, sems):
        x, y, c = _dev()
        x_ref, out = ins[0], outs[0]
        me = 4 * x + 2 * y + c
        peers = [(x, y, 1 - c), (1 - x, y, c), (x, 1 - y, c), (1 - x, 1 - y, c)]
        mk = lambda k, p, dst: pltpu.make_async_remote_copy(
            src_ref=x_ref, dst_ref=out.at[dst], send_sem=sems[0].at[k], recv_sem=sems[1].at[k], device_id=p,
            device_id_type=MESH)
        sends = [mk(k, p, me) for k, p in enumerate(peers)]
        recvs = [mk(k, p, 4 * p[0] + 2 * p[1] + p[2]) for k, p in enumerate(peers)]
        return sends, recvs, pltpu.make_async_copy(x_ref, out.at[me], sems[2])

    def start(ins, outs, sems):
        sends, _, mine = copies(ins, outs, sems)
        mine.start()
        for cp in sends:
            cp.start()

    def wait(ins, outs, sems):
        sends, recvs, mine = copies(ins, outs, sems)
        for cp in recvs:
            cp.wait_recv()
        for cp in sends:
            cp.wait_send()
        mine.wait()

    return _Side((loc,), (SDS((N_DEV, R, W), loc.dtype),), _dma_sems(4), start, wait)


def _side_gather_forward(buf):
    def copies(ins, outs, sems):
        x, y, c = _dev()
        out = outs[0]
        chips = [(1 - x, y), (x, 1 - y), (1 - x, 1 - y)]
        mk = lambda k, src, dst: pltpu.make_async_remote_copy(
            src_ref=out.at[src], dst_ref=out.at[dst], send_sem=sems[0].at[k], recv_sem=sems[1].at[k],
            device_id=(x, y, 1 - c), device_id_type=MESH)
        mine = [4 * px + 2 * py + c for px, py in chips]
        theirs = [4 * px + 2 * py + (1 - c) for px, py in chips]
        return [mk(k, s, s) for k, s in enumerate(mine)], [mk(k, s, t) for k, (s, t) in enumerate(zip(mine, theirs))]

    def start(ins, outs, sems):
        for cp in copies(ins, outs, sems)[0]:
            cp.start()

    def wait(ins, outs, sems):
        sends, recvs = copies(ins, outs, sems)
        for cp in recvs:
            cp.wait_recv()
        for cp in sends:
            cp.wait_send()

    return _Side((buf,), (SDS(buf.shape, buf.dtype),), _dma_sems(3)[:2], start, wait, {0: 0})


def _side_chip_exchange(p):
    def copies(ins, outs, sems):
        x, y, c = _dev()
        p_ref, out = ins[0], outs[0]
        j0 = 2 * x + y
        chips = [(1 - x, y), (x, 1 - y), (1 - x, 1 - y)]
        mk = lambda k, chip, src, dst: pltpu.make_async_remote_copy(
            src_ref=p_ref.at[src], dst_ref=out.at[dst], send_sem=sems[0].at[k], recv_sem=sems[1].at[k],
            device_id=(*chip, c), device_id_type=MESH)
        sends = [mk(k, chip, 2 * chip[0] + chip[1], j0) for k, chip in enumerate(chips)]
        recvs = [mk(k, chip, j0, 2 * chip[0] + chip[1]) for k, chip in enumerate(chips)]
        return sends, recvs, pltpu.make_async_copy(p_ref.at[j0], out.at[j0], sems[2])

    def start(ins, outs, sems):
        sends, _, mine = copies(ins, outs, sems)
        mine.start()
        for cp in sends:
            cp.start()

    def wait(ins, outs, sems):
        sends, recvs, mine = copies(ins, outs, sems)
        for cp in recvs:
            cp.wait_recv()
        for cp in sends:
            cp.wait_send()
        mine.wait()

    return _Side((p,), (SDS(p.shape, p.dtype),), _dma_sems(3), start, wait)


def _rs_pair_exchange(g, name):
    nd, R, W = g.shape
    nj = nd // 2

    def body(g_ref, out_ref, send_sems, recv_sems):
        x, y, c = _dev()
        cps = [pltpu.make_async_remote_copy(src_ref=g_ref.at[2 * j + (1 - c)], dst_ref=out_ref.at[j],
                                            send_sem=send_sems.at[j], recv_sem=recv_sems.at[j],
                                            device_id=(x, y, 1 - c), device_id_type=MESH) for j in range(nj)]
        for cp in cps:
            cp.start()
        for cp in cps:
            cp.wait()

    return pl.pallas_call(
        body, out_shape=SDS((nj, R, W), g.dtype), in_specs=[HBM_SPEC], out_specs=HBM_SPEC,
        scratch_shapes=[pltpu.SemaphoreType.DMA((nj,)), pltpu.SemaphoreType.DMA((nj,))], name=name)(g)


def _add_pairs(g, ra, name):
    nd, R, W = g.shape
    nj = nd // 2
    tr = _tile(R, 256, 8)
    cidx = lax.axis_index("c").astype(jnp.int32).reshape(1)

    def body(c_ref, a_ref, b_ref, o_ref):
        o_ref[...] = (a_ref[...].astype(F32) + b_ref[...].astype(F32)).astype(o_ref.dtype)

    gs = pltpu.PrefetchScalarGridSpec(
        num_scalar_prefetch=1, grid=(nj, R // tr),
        in_specs=[pl.BlockSpec((None, tr, W), lambda j, i, cr: (2 * j + cr[0], i, 0)),
                  pl.BlockSpec((None, tr, W), lambda j, i, cr: (j, i, 0))],
        out_specs=pl.BlockSpec((None, tr, W), lambda j, i, cr: (j, i, 0)))
    return pl.pallas_call(body, grid_spec=gs, out_shape=SDS((nj, R, W), g.dtype), name=name,
                          compiler_params=pltpu.CompilerParams(vmem_limit_bytes=VMEM_LIMIT))(cidx, g, ra)


def _rs_chip_exchange(p, name):
    nj, R, W = p.shape

    def body(p_ref, out_ref, send_sems, recv_sems, local_sem):
        x, y, c = _dev()
        j0 = 2 * x + y
        chips = [(1 - x, y), (x, 1 - y), (1 - x, 1 - y)]
        mine = pltpu.make_async_copy(p_ref.at[j0], out_ref.at[j0], local_sem)
        mine.start()

        def copy(k, chip):
            return pltpu.make_async_remote_copy(
                src_ref=p_ref.at[2 * chip[0] + chip[1]], dst_ref=out_ref.at[j0],
                send_sem=send_sems.at[k], recv_sem=recv_sems.at[k], device_id=(*chip, c), device_id_type=MESH)

        sent = [copy(k, chip) for k, chip in enumerate(chips)]
        for cp in sent:
            cp.start()
        for k, chip in enumerate(chips):
            pltpu.make_async_remote_copy(
                src_ref=p_ref.at[j0], dst_ref=out_ref.at[2 * chip[0] + chip[1]],
                send_sem=send_sems.at[k], recv_sem=recv_sems.at[k], device_id=(*chip, c), device_id_type=MESH).wait_recv()
        for cp in sent:
            cp.wait_send()
        mine.wait()

    return pl.pallas_call(
        body, out_shape=SDS((nj, R, W), p.dtype), in_specs=[HBM_SPEC], out_specs=HBM_SPEC,
        scratch_shapes=[pltpu.SemaphoreType.DMA((3,)), pltpu.SemaphoreType.DMA((3,)), pltpu.SemaphoreType.DMA],
        name=name)(p)


def _allgather_small(v, name):
    R, W = v.shape

    def body(v_ref, out_ref, send_sems, recv_sems, local_sem):
        x, y, c = _dev()
        mine = pltpu.make_async_copy(v_ref, out_ref.at[4 * x + 2 * y + c], local_sem)
        mine.start()
        peers = []
        for k in range(1, N_DEV):
            px = 1 - x if k & 4 else x
            py = 1 - y if k & 2 else y
            pc_ = 1 - c if k & 1 else c
            peers.append((px, py, pc_))
        sent = [pltpu.make_async_remote_copy(
            src_ref=v_ref, dst_ref=out_ref.at[4 * x + 2 * y + c], send_sem=send_sems.at[k], recv_sem=recv_sems.at[k],
            device_id=peer, device_id_type=MESH) for k, peer in enumerate(peers)]
        for cp in sent:
            cp.start()
        for k, (px, py, pc_) in enumerate(peers):
            pltpu.make_async_remote_copy(
                src_ref=v_ref, dst_ref=out_ref.at[4 * px + 2 * py + pc_], send_sem=send_sems.at[k],
                recv_sem=recv_sems.at[k], device_id=(px, py, pc_), device_id_type=MESH).wait_recv()
        for cp in sent:
            cp.wait_send()
        mine.wait()

    return pl.pallas_call(
        body, out_shape=SDS((N_DEV, R, W), v.dtype), in_specs=[VMEM_SPEC], out_specs=VMEM_SPEC,
        scratch_shapes=[pltpu.SemaphoreType.DMA((7,)), pltpu.SemaphoreType.DMA((7,)), pltpu.SemaphoreType.DMA],
        name=name)(v)


def _sum_slabs(a, name):
    ns, R, W = a.shape

    def body(a_ref, o_ref):
        acc = a_ref[0]
        for k in range(1, ns):
            acc = acc + a_ref[k]
        o_ref[...] = acc

    return pl.pallas_call(body, out_shape=SDS((R, W), a.dtype), in_specs=[VMEM_SPEC], out_specs=VMEM_SPEC, name=name)(a)


BIG = (("ada_w", "col"), ("w_in", "col"), ("w_ssd_out", "row"), ("w_sc_out", "row"), ("w_o", "row"), ("w_up", "col"),
       ("w_down", "row"))
EARLY = ("w_ssd_out", "w_sc_out", "w_o", "w_up", "w_down")
LATE = ("ada_w", "w_in")
MID_LAYOUT = ("w_in",)
SWAP_LAYOUT = ("w_up",)
CONVW = ("ssd_conv_w", "sc_conv_w", "ffn_conv_w")
REPL = ("ada_b", "mix_pre_g", "mix_post_g", "ssd_conv_b", "ssd_dt_bias", "ssd_a_log", "ssd_d", "ssd_norm_g", "ffn_pre_g",
        "ffn_post_g", "ffn_conv_b")
WEIGHTS = ("ada_w", "ada_b", "mix_pre_g", "mix_post_g", "w_in", "ssd_conv_w", "ssd_conv_b", "ssd_dt_bias", "ssd_a_log",
           "ssd_d", "ssd_norm_g", "w_ssd_out", "sc_conv_w", "w_sc_out", "w_o", "ffn_pre_g", "ffn_post_g", "w_up",
           "ffn_conv_w", "ffn_conv_b", "w_down")


def _pad_rows(a, mult):
    r = a.shape[-2]
    pad = -r % mult
    return a if pad == 0 else jnp.pad(a, [(0, 0)] * (a.ndim - 2) + [(0, pad), (0, 0)])


def _flat_rows(parts, mult):
    flat = jnp.concatenate([p.reshape(-1) for p in parts])
    flat = jnp.pad(flat, (0, -flat.shape[0] % ROW_W))
    return _pad_rows(flat.reshape(-1, ROW_W), mult)


def _unflat(buf, shapes):
    flat = buf.reshape(-1)
    out, o = [], 0
    for shp in shapes:
        n = 1
        for s in shp:
            n *= s
        out.append(flat[o:o + n].reshape(shp))
        o += n
    return out


def _pack_big_local(get, l):
    return [_pad_rows((get(n)[l].T if kind == "col" else get(n)[l]).reshape(-1, ROW_W), SLAB_ALIGN) for n, kind in BIG]


def _big_rows(shapes, names=None):
    out, o = {}, 0
    for n in (names if names is not None else [n for n, _ in BIG]):
        r = shapes[n][1] * shapes[n][2] // ROW_W
        out[n] = (o, o + r)
        o += -(-r // SLAB_ALIGN) * SLAB_ALIGN
    return out, o


def kernel(x, c, ada_w, ada_b, mix_pre_g, mix_post_g, w_in, ssd_conv_w, ssd_conv_b, ssd_dt_bias, ssd_a_log, ssd_d, ssd_norm_g, w_ssd_out, sc_conv_w, w_sc_out, w_o, ffn_pre_g, ffn_post_g, w_up, ffn_conv_w, ffn_conv_b, w_down, loss_target, m_ada_w, m_ada_b, m_mix_pre_g, m_mix_post_g, m_w_in, m_ssd_conv_w, m_ssd_conv_b, m_ssd_dt_bias, m_ssd_a_log, m_ssd_d, m_ssd_norm_g, m_w_ssd_out, m_sc_conv_w, m_w_sc_out, m_w_o, m_ffn_pre_g, m_ffn_post_g, m_w_up, m_ffn_conv_w, m_ffn_conv_b, m_w_down, v_ada_w, v_ada_b, v_mix_pre_g, v_mix_post_g, v_w_in, v_ssd_conv_w, v_ssd_conv_b, v_ssd_dt_bias, v_ssd_a_log, v_ssd_d, v_ssd_norm_g, v_w_ssd_out, v_sc_conv_w, v_w_sc_out, v_w_o, v_ffn_pre_g, v_ffn_post_g, v_w_up, v_ffn_conv_w, v_ffn_conv_b, v_w_down):
    wl = dict(zip(WEIGHTS, (ada_w, ada_b, mix_pre_g, mix_post_g, w_in, ssd_conv_w, ssd_conv_b, ssd_dt_bias, ssd_a_log,
                            ssd_d, ssd_norm_g, w_ssd_out, sc_conv_w, w_sc_out, w_o, ffn_pre_g, ffn_post_g, w_up,
                            ffn_conv_w, ffn_conv_b, w_down)))
    ml = dict(zip(WEIGHTS, (m_ada_w, m_ada_b, m_mix_pre_g, m_mix_post_g, m_w_in, m_ssd_conv_w, m_ssd_conv_b,
                            m_ssd_dt_bias, m_ssd_a_log, m_ssd_d, m_ssd_norm_g, m_w_ssd_out, m_sc_conv_w, m_w_sc_out, m_w_o,
                            m_ffn_pre_g, m_ffn_post_g, m_w_up, m_ffn_conv_w, m_ffn_conv_b, m_w_down)))
    vl = dict(zip(WEIGHTS, (v_ada_w, v_ada_b, v_mix_pre_g, v_mix_post_g, v_w_in, v_ssd_conv_w, v_ssd_conv_b,
                            v_ssd_dt_bias, v_ssd_a_log, v_ssd_d, v_ssd_norm_g, v_w_ssd_out, v_sc_conv_w, v_w_sc_out, v_w_o,
                            v_ffn_pre_g, v_ffn_post_g, v_w_up, v_ffn_conv_w, v_ffn_conv_b, v_w_down)))
    depth = ada_w.shape[0]
    shapes = {n: wl[n].shape for n in WEIGHTS}
    me = 4 * lax.axis_index("x") + 2 * lax.axis_index("y") + lax.axis_index("c")

    rows, n_big = _big_rows(shapes)
    conv_flat = jnp.concatenate([wl[n][l].reshape(-1) for l in range(depth) for n in CONVW])
    n_conv = conv_flat.shape[0]
    conv_flat = jnp.pad(conv_flat, (0, -n_conv % (ROW_W // 2)))
    conv_rows = lax.bitcast_convert_type(conv_flat, BF16).reshape(-1, ROW_W)

    def local_rows(l):
        pieces = _pack_big_local(lambda n: wl[n].astype(BF16), l) + ([conv_rows] if l == 0 else [])
        return _pad_rows(jnp.concatenate(pieces, axis=0), ROW_PAD)

    def layer_weights(l, gathered):
        w = {n: wl[n][l] for n in REPL}
        for n, kind in BIG:
            a, b = shapes[n][1], shapes[n][2]
            blk = gathered[:, rows[n][0]:rows[n][1]]
            w[n] = blk.reshape(N_DEV * b, a) if kind == "col" else blk.reshape(N_DEV * a, b)
        for n in CONVW:
            w[n] = conv_full[(l, n)]
        return w

    gathered = _allgather_big(local_rows(0), "allgather_weights")
    conv_all = lax.bitcast_convert_type(
        gathered[:, n_big:n_big + conv_rows.shape[0]].reshape(N_DEV, -1, 2), F32)[:, :n_conv]
    conv_full, o = {}, 0
    for l in range(depth):
        for n in CONVW:
            k, cl = shapes[n][1], shapes[n][2]
            conv_full[(l, n)] = conv_all[:, o:o + k * cl].reshape(N_DEV, k, cl).transpose(1, 0, 2).reshape(k, N_DEV * cl)
            o += k * cl

    nb, S, D = x.shape
    T = nb * S
    act = x.reshape(T, D)
    c8 = jnp.pad(c, ((0, MOD_ROWS - nb), (0, 0)))
    preps, saved = [], []
    for l in range(depth):
        preps.append(_prep_layer(layer_weights(l, gathered)))
        act, s, gathered = _layer_fwd(act, c8, preps[l], S, l, gather=local_rows(l + 1) if l + 1 < depth else None)
        saved.append(s)
    dy, lacc = _loss(act, loss_target.reshape(T, D), S, "loss")
    loss_loc = lacc[0, 0]

    group_rows = {grp: _big_rows(shapes, names) for grp, names in (("early", EARLY), ("late", LATE))}

    def pair_sums(g, grp, names, l):
        slabs = [_pad_rows(g[n].astype(BF16).reshape(N_DEV, -1, ROW_W), SLAB_ALIGN) for n in names]
        slabs.append(jnp.zeros((N_DEV, -group_rows[grp][1] % ROW_PAD, ROW_W), BF16))
        gslab = jnp.concatenate(slabs, axis=1)
        from_sibling = _rs_pair_exchange(gslab, f"rs_pair_exchange_{grp}_l{l}")
        return _add_pairs(gslab, from_sibling, f"rs_pair_add_{grp}_l{l}")

    grads, pending = [None] * depth, None
    from_chips = {"early": [None] * depth, "late": [None] * depth}
    for l in reversed(range(depth)):
        dy, grads[l], got, from_chips["early"][l] = _layer_bwd(
            dy, preps[l], saved[l], S, l, chip_sums=pending, early=lambda g, l=l: pair_sums(g, "early", EARLY, l))
        if pending is not None:
            from_chips["late"][l + 1] = got
        pending = pair_sums(grads[l], "late", LATE, l)
    from_chips["late"][0] = _rs_chip_exchange(pending, "rs_chip_exchange")
    dx = dy.reshape(nb, S, D)
    g_sums = {grp: [_sum_chips(from_chips[grp][l], f"rs_chip_sum_{grp}_l{l}") for l in range(depth)]
              for grp in ("early", "late")}

    def slab_of(l, n, kind):
        grp = "early" if n in EARLY else "late"
        r0, r1 = group_rows[grp][0][n]
        a, b = shapes[n][1], shapes[n][2]
        return g_sums[grp][l][r0:r1].reshape((b, a) if kind == "col" else (a, b))

    g_big, d_big, m_big, v_big = {}, {}, {}, {}
    for n, kind in BIG:
        if n in MID_LAYOUT:
            gm = jnp.stack([slab_of(l, n, kind) for l in range(depth)], axis=1)
            fwd, back = (lambda t: t.transpose(2, 0, 1)), (lambda t: t.transpose(1, 2, 0))
            res = [gm] + list(_adam_mid(gm, fwd(wl[n]), fwd(ml[n]), fwd(vl[n]), f"adam_{n}"))
        else:
            gt = jnp.stack([slab_of(l, n, kind) for l in range(depth)])
            fwd = back = (lambda t: t.swapaxes(1, 2)) if kind == "col" else (lambda t: t)
            if n in SWAP_LAYOUT:
                res = [gt] + list(_adam_nat(gt, fwd(wl[n]), fwd(ml[n]), fwd(vl[n]), f"adam_{n}"))
            else:
                gn = back(gt)
                res, back = [gn] + list(_adam_nat(gn, wl[n], ml[n], vl[n], f"adam_{n}")), (lambda t: t)
        g_big[n], d_big[n], m_big[n], v_big[n] = [back(t) for t in res]

    parts = [jnp.broadcast_to(loss_loc, (ROW_W,))]
    small_shapes = [(ROW_W,)]
    for l in range(depth):
        for n in REPL + CONVW:
            parts.append(grads[l][n])
            small_shapes.append(tuple(grads[l][n].shape))
    total = _sum_slabs(_allgather_small(_flat_rows(parts, 8), "allgather_small"), "sum_small")
    pieces = _unflat(total, small_shapes)
    loss = pieces[0][0]
    g_small, i = {}, 1
    for l in range(depth):
        for n in REPL + CONVW:
            gp = pieces[i]
            i += 1
            if n in CONVW:
                gp = lax.dynamic_slice_in_dim(gp, me * shapes[n][2], shapes[n][2], axis=1)
            g_small[(l, n)] = gp
    order = [(l, n) for l in range(depth) for n in REPL + CONVW]
    loc_shapes = [tuple(shapes[n][1:]) for _, n in order]
    packs = lambda f: _flat_rows([f(l, n) for l, n in order], 8)
    gs_small = packs(lambda l, n: g_small[(l, n)])
    _, d_sm, m_sm, v_sm = _adam(gs_small[None], packs(lambda l, n: wl[n][l]), packs(lambda l, n: ml[n][l]),
                                packs(lambda l, n: vl[n][l]), "adam_small")

    def unpack_small(buf):
        ps = _unflat(buf, loc_shapes)
        return {n: jnp.stack([ps[order.index((l, n))] for l in range(depth)]) for n in REPL + CONVW}

    outs = []
    for big, small in ((g_big, {n: jnp.stack([g_small[(l, n)] for l in range(depth)]) for n in REPL + CONVW}),
                       (d_big, unpack_small(d_sm)), (m_big, unpack_small(m_sm)), (v_big, unpack_small(v_sm))):
        merged = {**big, **small}
        outs += [merged[n] for n in WEIGHTS]
    return (loss, dx, *outs)
```

```python
import functools
import math
from typing import Callable, NamedTuple

import jax
import jax.numpy as jnp
from jax import lax
from jax.experimental import pallas as pl
from jax.experimental.pallas import tpu as pltpu

F32, BF16 = jnp.float32, jnp.bfloat16
WGRAD = BF16
SDS = jax.ShapeDtypeStruct
MESH = pl.DeviceIdType.MESH

EPS = 1e-6
N_STATE = 128
CHUNK = 128
SSD_STEP_CHUNKS = 8
SSD_CONV_K, SC_CONV_K, FFN_CONV_K = 4, 3, 3
N_DEV = 8
ROW_W = 1024
ROW_PAD = 32
SLAB_ALIGN = 16
SEG_BLK = 512
STRIP = 32
FFN_STRIP = 64
GLU_W = 256
MOD_ROWS = 128
VMEM_LIMIT = 48 * 2**20

ADAM_LR, ADAM_B1, ADAM_B2, ADAM_EPS, ADAM_WD, ADAM_STEP = 0.001, 0.9, 0.999, 1e-08, 0.01, 10

NT = (((1,), (1,)), ((), ()))
TN = (((0,), (0,)), ((), ()))
NN = (((1,), (0,)), ((), ()))


def _tile(n, cap, mult=128):
    best = None
    for t in range(mult, min(n, cap) + 1, mult):
        if n % t == 0:
            best = t
    return best if best is not None else n


class _Side(NamedTuple):
    operands: tuple
    out_shape: tuple
    scratch: tuple
    start: Callable
    wait: Callable
    aliases: dict = {}


def _pc(body, *, grid, in_specs, out_specs, out_shape, name, scratch=(), side=None):
    params = pltpu.CompilerParams(dimension_semantics=("arbitrary",) * len(grid), vmem_limit_bytes=VMEM_LIMIT)
    if side is None:
        return pl.pallas_call(body, grid=grid, in_specs=in_specs, out_specs=out_specs, out_shape=out_shape,
                              scratch_shapes=list(scratch), name=name, compiler_params=params)
    single = not isinstance(out_shape, (list, tuple))
    outs = [out_shape] if single else list(out_shape)
    ospecs = [out_specs] if single else list(out_specs)
    n_in, n_out, n_scr = len(in_specs), len(outs), len(scratch)
    s_in, s_out = len(side.operands), len(side.out_shape)

    def hosted(*refs):
        ins, refs = refs[:n_in], refs[n_in:]
        sins, refs = refs[:s_in], refs[s_in:]
        mouts, refs = refs[:n_out], refs[n_out:]
        souts, refs = refs[:s_out], refs[s_out:]
        scr, sems = refs[:n_scr], refs[n_scr:]
        first = functools.reduce(lambda a, b: a & b, [pl.program_id(a) == 0 for a in range(len(grid))])
        last = functools.reduce(lambda a, b: a & b, [pl.program_id(a) == grid[a] - 1 for a in range(len(grid))])

        @pl.when(first)
        def _():
            side.start(sins, souts, sems)

        body(*ins, *mouts, *scr)

        @pl.when(last)
        def _():
            side.wait(sins, souts, sems)

    call = pl.pallas_call(
        hosted, grid=grid, in_specs=list(in_specs) + [HBM_SPEC] * s_in, out_specs=ospecs + [HBM_SPEC] * s_out,
        out_shape=outs + list(side.out_shape), scratch_shapes=list(scratch) + list(side.scratch), name=name,
        input_output_aliases={n_in + k: n_out + v for k, v in side.aliases.items()}, compiler_params=params)

    def run(*args):
        res = call(*args, *side.operands)
        main = res[0] if single else list(res[:n_out])
        return main, list(res[n_out:])

    return run


def _silu(x):
    return x * jax.nn.sigmoid(x)


def _dsilu(x):
    s = jax.nn.sigmoid(x)
    return s * (1.0 + x * (1.0 - s))


def _softplus(x):
    return jnp.maximum(x, 0.0) + jnp.log(1.0 + jnp.exp(-jnp.abs(x)))


def _dot(a, b, dims=NN):
    return lax.dot_general(a, b, dims, preferred_element_type=F32)


def _bsum(v, rows=8):
    return jnp.broadcast_to(v, (rows, v.shape[1]))


def _mm(a, b, mode, out_dtype, name, caps=(1024, 1024, 2048), side=None):
    if mode == "nn":
        (M, K), (K2, N) = a.shape, b.shape
    elif mode == "nt":
        (M, K), (N, K2) = a.shape, b.shape
    else:
        (K, M), (K2, N) = a.shape, b.shape
    assert K == K2, (a.shape, b.shape, mode)
    tm, tn, tk = _tile(M, caps[0]), _tile(N, caps[1]), _tile(K, caps[2])
    nk = K // tk
    dims = {"nn": NN, "nt": NT, "tn": TN}[mode]
    if mode == "tn":
        a_spec = pl.BlockSpec((tk, tm), lambda i, j, k: (k, i))
    else:
        a_spec = pl.BlockSpec((tm, tk), lambda i, j, k: (i, k))
    if mode == "nt":
        b_spec = pl.BlockSpec((tn, tk), lambda i, j, k: (j, k))
    else:
        b_spec = pl.BlockSpec((tk, tn), lambda i, j, k: (k, j))

    def body(a_ref, b_ref, o_ref, *acc):
        part = _dot(a_ref[...].astype(BF16), b_ref[...].astype(BF16), dims)
        if nk == 1:
            o_ref[...] = part.astype(o_ref.dtype)
        else:
            acc_ref, = acc
            k = pl.program_id(2)

            @pl.when(k == 0)
            def _():
                acc_ref[...] = part

            @pl.when(k > 0)
            def _():
                acc_ref[...] += part

            @pl.when(k == nk - 1)
            def _():
                o_ref[...] = acc_ref[...].astype(o_ref.dtype)

    return _pc(body, grid=(M // tm, N // tn, nk), in_specs=[a_spec, b_spec],
               out_specs=pl.BlockSpec((tm, tn), lambda i, j, k: (i, j)),
               out_shape=SDS((M, N), out_dtype), name=name,
               scratch=() if nk == 1 else (pltpu.VMEM((tm, tn), F32),), side=side)(a, b)


def _mm_seg(segs, b, mode, out_dtype, name, blk, tile=1024, tk=2048):
    nblk = [a.shape[1] // blk for a in segs]
    assert all(a.shape[1] % blk == 0 for a in segs)
    start = [sum(nblk[:s]) for s in range(len(segs))]
    total = sum(nblk)
    ns = len(segs)
    N = b.shape[1]
    tn = _tile(N, tile)
    if mode == "nn":
        M = segs[0].shape[0]
        tm = _tile(M, tile)
        grid = (M // tm, N // tn, total)
        a_specs = [pl.BlockSpec((tm, blk), lambda i, j, k, k0=k0, n=n: (i, jnp.clip(k - k0, 0, n - 1)))
                   for k0, n in zip(start, nblk)]
        b_spec = pl.BlockSpec((blk, tn), lambda i, j, k: (k, j))
        out_rows, tmo, dims, seg_axis = M, tm, NN, 2
    else:
        K = segs[0].shape[0]
        tkk = _tile(K, tk)
        grid = (total, N // tn, K // tkk)
        a_specs = [pl.BlockSpec((tkk, blk), lambda i, j, k, i0=i0, n=n: (
            jnp.where((i >= i0) & (i < i0 + n), k, 0), jnp.clip(i - i0, 0, n - 1))) for i0, n in zip(start, nblk)]
        b_spec = pl.BlockSpec((tkk, tn), lambda i, j, k: (k, j))
        out_rows, tmo, seg_axis = total * blk, blk, 0
    nk = grid[2]
    acc_shape = (tm, tn) if mode == "nn" else (tn, blk)

    def body(*refs):
        a_refs, b_ref, o_ref, acc_ref = refs[:ns], refs[ns], refs[ns + 1], refs[ns + 2]
        k = pl.program_id(2)
        sel = pl.program_id(seg_axis)

        @pl.when(k == 0)
        def _():
            acc_ref[...] = jnp.zeros_like(acc_ref)

        for s in range(ns):
            @pl.when((sel >= start[s]) & (sel < start[s] + nblk[s]))
            def _(s=s):
                a_, b_ = a_refs[s][...].astype(BF16), b_ref[...].astype(BF16)
                acc_ref[...] += _dot(a_, b_, NN) if mode == "nn" else _dot(b_, a_, TN)

        @pl.when(k == nk - 1)
        def _():
            acc = acc_ref[...]
            o_ref[...] = (acc if mode == "nn" else acc.T).astype(o_ref.dtype)

    return _pc(body, grid=grid, in_specs=a_specs + [b_spec], out_specs=pl.BlockSpec((tmo, tn), lambda i, j, k: (i, j)),
               out_shape=SDS((out_rows, N), out_dtype), name=name, scratch=(pltpu.VMEM(acc_shape, F32),))(*segs, b)


def _modk(c8, ada_w, ada_b, name):
    rows, D = c8.shape
    N = ada_w.shape[0]
    tn = _tile(N, 1536)

    def body(c_ref, w_ref, b_ref, mod_ref, ca_ref):
        ca = _silu(c_ref[...]).astype(BF16)
        mod_ref[...] = _dot(ca, w_ref[...], NT) + b_ref[...]
        ca_ref[...] = ca

    return _pc(body, grid=(N // tn,),
               in_specs=[pl.BlockSpec((rows, D), lambda j: (0, 0)), pl.BlockSpec((tn, D), lambda j: (j, 0)),
                         pl.BlockSpec((1, tn), lambda j: (0, j))],
               out_specs=[pl.BlockSpec((rows, tn), lambda j: (0, j)), pl.BlockSpec((rows, D), lambda j: (0, 0))],
               out_shape=[SDS((rows, N), F32), SDS((rows, D), BF16)], name=name)(c8, ada_w, ada_b)


def _row_tile(S):
    return _tile(S, 512, 8)


def _strip_row_tile(S, cap=2048):
    return _tile(S, cap, FFN_STRIP)


def _strips(tm, fn, init=0, rows=None):
    rows = STRIP if rows is None else rows
    assert tm % rows == 0
    return lax.fori_loop(0, tm // rows, lambda r, c: fn(pl.multiple_of(r * rows, rows), c), init)


def _strips_prev(tm, rows, ref, prev, fn, init=0):
    carry = fn(0, jnp.concatenate([prev, ref[0:rows, :]], axis=0), init)

    def step(r, c):
        r0 = pl.multiple_of(r * rows, rows)
        return fn(r0, ref[pl.ds(pl.multiple_of(r0 - 8, 8), rows + 8), :], c)

    return lax.fori_loop(1, tm // rows, step, carry)


def _strips_next(tm, rows, ref, nxt, fn, init=0):
    def step(r, c):
        r0 = pl.multiple_of(r * rows, rows)
        return fn(r0, ref[pl.ds(r0, rows + 8), :], c)

    carry = lax.fori_loop(0, tm // rows - 1, step, init)
    return fn(tm - rows, jnp.concatenate([ref[tm - rows:tm, :], nxt], axis=0), carry)


def _rows8(rows):
    pad = 8 - len(rows)
    return jnp.concatenate(rows + ([jnp.zeros((pad, rows[0].shape[1]), F32)] if pad else []), axis=0)


def _fold8(v):
    return jnp.sum(v.reshape(v.shape[0] // 8, 8, v.shape[1]), axis=0)


def _norm_mod(x, g, mod3, sc_seg, sh_seg, S, name):
    T, D = x.shape
    tm = _row_tile(S)
    tpb = S // tm

    def body(x_ref, g_ref, sc_ref, sh_ref, h_ref):
        x_ = x_ref[...]
        r = lax.rsqrt(jnp.mean(x_ * x_, axis=-1, keepdims=True) + EPS)
        h_ref[...] = ((x_ * r) * (g_ref[...] * (1.0 + sc_ref[...])) + sh_ref[...]).astype(BF16)

    return _pc(body, grid=(T // tm,),
               in_specs=[pl.BlockSpec((tm, D), lambda i: (i, 0)), pl.BlockSpec((1, D), lambda i: (0, 0)),
                         pl.BlockSpec((None, 1, D), lambda i: (i // tpb, 0, sc_seg)),
                         pl.BlockSpec((None, 1, D), lambda i: (i // tpb, 0, sh_seg))],
               out_specs=pl.BlockSpec((tm, D), lambda i: (i, 0)), out_shape=SDS((T, D), BF16), name=name)(x, g, mod3, mod3)


def _resid_post(x, fo, mod3, gt_seg, pg, S, name):
    T, D = x.shape
    tm = _row_tile(S)
    tpb = S // tm

    def body(x_ref, f_ref, gt_ref, pg_ref, o_ref):
        f = f_ref[...]
        r = lax.rsqrt(jnp.mean(f * f, axis=-1, keepdims=True) + EPS)
        o_ref[...] = x_ref[...] + (f * r) * (gt_ref[...] * pg_ref[...])

    return _pc(body, grid=(T // tm,),
               in_specs=[pl.BlockSpec((tm, D), lambda i: (i, 0)), pl.BlockSpec((tm, D), lambda i: (i, 0)),
                         pl.BlockSpec((None, 1, D), lambda i: (i // tpb, 0, gt_seg)),
                         pl.BlockSpec((1, D), lambda i: (0, 0))],
               out_specs=pl.BlockSpec((tm, D), lambda i: (i, 0)), out_shape=SDS((T, D), F32), name=name)(x, fo, mod3, pg)


def _post_bwd(fo, mod3, gt_seg, pg, dout, S, name):
    T, D = fo.shape
    tm = _row_tile(S)
    tpb = S // tm
    nb = T // S

    def body(f_ref, gt_ref, pg_ref, d_ref, df_ref, dgt_ref, dpg_ref):
        i = pl.program_id(0)

        @pl.when(i == 0)
        def _():
            dpg_ref[...] = jnp.zeros_like(dpg_ref)

        @pl.when(i % tpb == 0)
        def _():
            dgt_ref[...] = jnp.zeros_like(dgt_ref)

        f, d = f_ref[...], d_ref[...]
        r = lax.rsqrt(jnp.mean(f * f, axis=-1, keepdims=True) + EPS)
        n = f * r
        dn = d * (gt_ref[...] * pg_ref[...])
        df_ref[...] = (r * (dn - n * jnp.mean(dn * n, axis=-1, keepdims=True))).astype(df_ref.dtype)
        tot = jnp.sum(d * n, axis=0, keepdims=True)
        dgt_ref[...] += _bsum(tot * pg_ref[...])
        dpg_ref[...] += _bsum(tot * gt_ref[...])

    return _pc(body, grid=(T // tm,),
               in_specs=[pl.BlockSpec((tm, D), lambda i: (i, 0)),
                         pl.BlockSpec((None, 1, D), lambda i: (i // tpb, 0, gt_seg)),
                         pl.BlockSpec((1, D), lambda i: (0, 0)), pl.BlockSpec((tm, D), lambda i: (i, 0))],
               out_specs=[pl.BlockSpec((tm, D), lambda i: (i, 0)), pl.BlockSpec((8, D), lambda i: (i // tpb, 0)),
                          pl.BlockSpec((8, D), lambda i: (0, 0))],
               out_shape=[SDS((T, D), BF16), SDS((nb * 8, D), F32), SDS((8, D), F32)], name=name)(fo, mod3, pg, dout)


def _pre_bwd(x, g, mod3, sc_seg, dh, dout, S, name):
    T, D = x.shape
    tm = _row_tile(S)
    tpb = S // tm
    nb = T // S

    def body(x_ref, g_ref, sc_ref, dh_ref, d_ref, dx_ref, dg_ref, dsc_ref, dsh_ref):
        i = pl.program_id(0)

        @pl.when(i == 0)
        def _():
            dg_ref[...] = jnp.zeros_like(dg_ref)

        @pl.when(i % tpb == 0)
        def _():
            dsc_ref[...] = jnp.zeros_like(dsc_ref)
            dsh_ref[...] = jnp.zeros_like(dsh_ref)

        x_, dh_ = x_ref[...], dh_ref[...]
        r = lax.rsqrt(jnp.mean(x_ * x_, axis=-1, keepdims=True) + EPS)
        n = x_ * r
        dn = dh_ * (g_ref[...] * (1.0 + sc_ref[...]))
        dx_ref[...] = d_ref[...] + r * (dn - n * jnp.mean(dn * n, axis=-1, keepdims=True))
        dhn = jnp.sum(dh_ * n, axis=0, keepdims=True)
        dg_ref[...] += _bsum(dhn * (1.0 + sc_ref[...]))
        dsc_ref[...] += _bsum(dhn * g_ref[...])
        dsh_ref[...] += _bsum(jnp.sum(dh_, axis=0, keepdims=True))

    row = pl.BlockSpec((tm, D), lambda i: (i, 0))
    return _pc(body, grid=(T // tm,),
               in_specs=[row, pl.BlockSpec((1, D), lambda i: (0, 0)),
                         pl.BlockSpec((None, 1, D), lambda i: (i // tpb, 0, sc_seg)), row, row],
               out_specs=[row, pl.BlockSpec((8, D), lambda i: (0, 0)), pl.BlockSpec((8, D), lambda i: (i // tpb, 0)),
                          pl.BlockSpec((8, D), lambda i: (i // tpb, 0))],
               out_shape=[SDS((T, D), F32), SDS((8, D), F32), SDS((nb * 8, D), F32), SDS((nb * 8, D), F32)],
               name=name)(x, g, mod3, dh, dout)


def _loss(y, target, S, name):
    T, D = y.shape
    tm = _row_tile(S)

    def body(y_ref, t_ref, dy_ref, l_ref):
        @pl.when(pl.program_id(0) == 0)
        def _():
            l_ref[...] = jnp.zeros_like(l_ref)

        def strip(r0, carry):
            rows = pl.ds(r0, STRIP)
            e = y_ref[rows, :] - t_ref[rows, :]
            dy_ref[rows, :] = e * (1.0 / D)
            return carry + _fold8(e * e)

        acc = _strips(tm, strip, jnp.zeros((8, D), F32))
        l_ref[...] += jnp.broadcast_to(jnp.sum(acc, keepdims=True) * (0.5 / D), l_ref.shape)

    row = pl.BlockSpec((tm, D), lambda i: (i, 0))
    return _pc(body, grid=(T // tm,), in_specs=[row, row],
               out_specs=[row, pl.BlockSpec((8, 128), lambda i: (0, 0))],
               out_shape=[SDS((T, D), F32), SDS((8, 128), F32)], name=name)(y, target)


def _conv_geom(view, C, S, cap=2048):
    arr, off = view
    T = arr.shape[0]
    tm = _strip_row_tile(S, cap)
    tc = _tile(C, 512)
    assert off % tc == 0 and C % tc == 0
    return arr, off // tc, T, tm, tc, S // tm


def _prev_spec(tm, tc, ob, order):
    if order == "ij":
        return pl.BlockSpec((8, tc), lambda i, j: (jnp.maximum(i * (tm // 8) - 1, 0), ob + j))
    return pl.BlockSpec((8, tc), lambda j, i: (jnp.maximum(i * (tm // 8) - 1, 0), ob + j))


def _next_spec(T, tm, tc, ob, order):
    last = T // 8 - 1
    if order == "ij":
        return pl.BlockSpec((8, tc), lambda i, j: (jnp.minimum((i + 1) * (tm // 8), last), ob + j))
    return pl.BlockSpec((8, tc), lambda j, i: (jnp.minimum((i + 1) * (tm // 8), last), ob + j))


def _taps(win, w_ref, K, lead, rows):
    acc = win[lead:lead + rows] * w_ref[K - 1:K, :]
    for j in range(1, K):
        acc = acc + win[lead - j:lead - j + rows] * w_ref[K - 1 - j:K - j, :]
    return acc


def _taps_t(win, w_ref, K, rows):
    acc = win[0:rows] * w_ref[K - 1:K, :]
    for j in range(1, K):
        acc = acc + win[j:j + rows] * w_ref[K - 1 - j:K - j, :]
    return acc


def _conv_fwd(view, C, w8, b, K, S, name):
    arr, ob, T, tm, tc, tps = _conv_geom(view, C, S)

    def body(u_ref, p_ref, w_ref, b_ref, o_ref):
        first = (pl.program_id(0) % tps) == 0

        def strip(r0, win, carry):
            o_ref[pl.ds(r0, STRIP), :] = _taps(win, w_ref, K, 8, STRIP) + b_ref[...]
            return carry

        _strips_prev(tm, STRIP, u_ref, jnp.where(first, 0.0, p_ref[...]), strip)

    return _pc(body, grid=(T // tm, C // tc),
               in_specs=[pl.BlockSpec((tm, tc), lambda i, j: (i, ob + j)), _prev_spec(tm, tc, ob, "ij"),
                         pl.BlockSpec((8, tc), lambda i, j: (0, j)), pl.BlockSpec((1, tc), lambda i, j: (0, j))],
               out_specs=pl.BlockSpec((tm, tc), lambda i, j: (i, j)), out_shape=SDS((T, C), F32), name=name)(
                   arr, arr, w8, b)


def _conv_bwd_in(dview, C, w8, K, S, out_dtype, name):
    arr, ob, T, tm, tc, tps = _conv_geom(dview, C, S)

    def body(d_ref, n_ref, w_ref, o_ref):
        last = (pl.program_id(0) % tps) == tps - 1

        def strip(r0, win, carry):
            o_ref[pl.ds(r0, STRIP), :] = _taps_t(win, w_ref, K, STRIP).astype(o_ref.dtype)
            return carry

        _strips_next(tm, STRIP, d_ref, jnp.where(last, 0.0, n_ref[...]), strip)

    return _pc(body, grid=(T // tm, C // tc),
               in_specs=[pl.BlockSpec((tm, tc), lambda i, j: (i, ob + j)), _next_spec(T, tm, tc, ob, "ij"),
                         pl.BlockSpec((8, tc), lambda i, j: (0, j))],
               out_specs=pl.BlockSpec((tm, tc), lambda i, j: (i, j)), out_shape=SDS((T, C), out_dtype), name=name)(
                   arr, arr, w8)


def _conv_bwd_w(dview, uview, C, K, S, name):
    darr, dob, T, tm, tc, tps = _conv_geom(dview, C, S)
    uarr, uob, _, _, _, _ = _conv_geom(uview, C, S)

    def body(d_ref, u_ref, p_ref, o_ref):
        i = pl.program_id(1)

        @pl.when(i == 0)
        def _():
            o_ref[...] = jnp.zeros_like(o_ref)

        first = (i % tps) == 0

        def strip(r0, win, carry):
            d = d_ref[pl.ds(r0, STRIP), :]
            sums = [_fold8(d * win[8 - (K - 1 - k):8 - (K - 1 - k) + STRIP]) for k in range(K)] + [_fold8(d)]
            return tuple(c + s for c, s in zip(carry, sums))

        acc = _strips_prev(tm, STRIP, u_ref, jnp.where(first, 0.0, p_ref[...]), strip,
                           tuple(jnp.zeros((8, tc), F32) for _ in range(K + 1)))
        o_ref[...] += _rows8([jnp.sum(a, axis=0, keepdims=True) for a in acc])

    return _pc(body, grid=(C // tc, T // tm),
               in_specs=[pl.BlockSpec((tm, tc), lambda j, i: (i, dob + j)),
                         pl.BlockSpec((tm, tc), lambda j, i: (i, uob + j)), _prev_spec(tm, tc, uob, "ji")],
               out_specs=pl.BlockSpec((8, tc), lambda j, i: (0, j)), out_shape=SDS((8, C), F32), name=name)(
                   darr, uarr, uarr)


def _ffn_act_fwd(uu, w8, b, S, name):
    K, gw = FFN_CONV_K, GLU_W
    T, F2 = uu.shape
    tm, tc = _strip_row_tile(S), 2 * GLU_W
    tps = S // tm

    def body(u_ref, p_ref, w_ref, b_ref, a_ref):
        first = (pl.program_id(0) % tps) == 0

        def strip(r0, win, carry):
            u = _taps(win, w_ref, K, 8, STRIP) + b_ref[...]
            a_ref[pl.ds(r0, STRIP), :] = (_silu(u[:, :gw]) * u[:, gw:]).astype(BF16)
            return carry

        _strips_prev(tm, STRIP, u_ref, jnp.where(first, 0.0, p_ref[...]), strip)

    return _pc(body, grid=(T // tm, F2 // tc),
               in_specs=[pl.BlockSpec((tm, tc), lambda i, j: (i, j)), _prev_spec(tm, tc, 0, "ij"),
                         pl.BlockSpec((8, tc), lambda i, j: (0, j)), pl.BlockSpec((1, tc), lambda i, j: (0, j))],
               out_specs=pl.BlockSpec((tm, gw), lambda i, j: (i, j)), out_shape=SDS((T, F2 // 2), BF16), name=name)(
                   uu, uu, w8, b)


def _ffn_act_bwd(uu, da, w8, b, S, name, side=None):
    K, gw = FFN_CONV_K, GLU_W
    T, F2 = uu.shape
    tm, tc = _strip_row_tile(S), 2 * GLU_W
    tps = S // tm
    last16 = T // 16 - 1

    def body(u_ref, p_ref, n_ref, da_ref, dan_ref, w_ref, b_ref, duu_ref, cw_ref, dabuf):
        i = pl.program_id(1)

        @pl.when(i == 0)
        def _():
            cw_ref[...] = jnp.zeros_like(cw_ref)

        first = (i % tps) == 0
        last = (i % tps) == tps - 1
        dabuf[0:tm, :] = da_ref[...].astype(F32)
        dabuf[tm:tm + 8, :] = jnp.where(last, 0.0, dan_ref[...].astype(F32)[0:8, :])
        fs, ext = FFN_STRIP, FFN_STRIP + 8

        def strip(r0, win, carry):
            shifted = [win[8 - j:8 - j + ext] for j in range(K)]
            u = b_ref[...] + shifted[0] * w_ref[K - 1:K, :]
            for j in range(1, K):
                u = u + shifted[j] * w_ref[K - 1 - j:K - j, :]
            da_ = dabuf[pl.ds(r0, ext), :]
            g, v = u[:, :gw], u[:, gw:]
            du = jnp.concatenate([da_ * v * _dsilu(g), da_ * _silu(g)], axis=1)
            duu_ref[pl.ds(r0, FFN_STRIP), :] = _taps_t(du, w_ref, K, FFN_STRIP).astype(BF16)
            dmain = du[0:FFN_STRIP]
            sums = [_fold8(dmain * shifted[K - 1 - k][0:FFN_STRIP]) for k in range(K)] + [_fold8(dmain)]
            return tuple(c + s for c, s in zip(carry, sums))

        acc = strip(0, jnp.concatenate([jnp.where(first, 0.0, p_ref[...]), u_ref[0:ext, :]], axis=0),
                    tuple(jnp.zeros((8, tc), F32) for _ in range(K + 1)))

        def step(r, c):
            r0 = pl.multiple_of(r * fs, fs)
            return strip(r0, u_ref[pl.ds(pl.multiple_of(r0 - 8, 8), fs + 16), :], c)

        acc = lax.fori_loop(1, tm // fs - 1, step, acc)
        acc = strip(tm - fs, jnp.concatenate([u_ref[tm - ext:tm, :], n_ref[...]], axis=0), acc)
        cw_ref[...] += _rows8([jnp.sum(a, axis=0, keepdims=True) for a in acc])

    return _pc(body, grid=(F2 // tc, T // tm),
               in_specs=[pl.BlockSpec((tm, tc), lambda j, i: (i, j)), _prev_spec(tm, tc, 0, "ji"),
                         _next_spec(T, tm, tc, 0, "ji"), pl.BlockSpec((tm, gw), lambda j, i: (i, j)),
                         pl.BlockSpec((16, gw), lambda j, i: (jnp.minimum((i + 1) * (tm // 16), last16), j)),
                         pl.BlockSpec((8, tc), lambda j, i: (0, j)), pl.BlockSpec((1, tc), lambda j, i: (0, j))],
               out_specs=[pl.BlockSpec((tm, tc), lambda j, i: (i, j)), pl.BlockSpec((8, tc), lambda j, i: (0, j))],
               out_shape=[SDS((T, F2), BF16), SDS((8, F2), F32)], name=name,
               scratch=(pltpu.VMEM((tm + 8, gw), F32),), side=side)(
                   uu, uu, uu, da, da, w8, b)


def _ssd_common(dtc_raw, dtr_raw, hpc, hpr, L):
    dt_c = _softplus(dtc_raw + hpc[0:1, :])
    a_c = -jnp.exp(hpc[1:2, :])
    dt_r = _softplus(dtr_raw + hpr[:, 0:1])
    a_r = -jnp.exp(hpr[:, 1:2])
    li = lax.broadcasted_iota(jnp.int32, (L, L), 0)
    si = lax.broadcasted_iota(jnp.int32, (L, L), 1)
    low = li >= si
    upp = li <= si
    acs_c = _dotx(low, dt_c * a_c, split="b")
    acs_r = _dotx(dt_r * a_r, upp)
    return dt_c, a_c, acs_c, acs_r, low, upp


def _dotx(a, b, split="a", parts=3, dims=NN):
    val, one = (a, b) if split == "a" else (b, a)
    one = one.astype(BF16)
    acc, rem = None, val
    for i in range(parts):
        piece = rem.astype(BF16)
        t = _dot(piece, one, dims) if split == "a" else _dot(one, piece, dims)
        acc = t if acc is None else acc + t
        if i + 1 < parts:
            rem = rem - piece.astype(F32)
    return acc


def _head_maps(R, P, L):
    RP = R * P
    sel = (lax.broadcasted_iota(jnp.int32, (RP, R), 0) // P == lax.broadcasted_iota(jnp.int32, (RP, R), 1)).astype(F32)
    selt = (lax.broadcasted_iota(jnp.int32, (R, RP), 1) // P == lax.broadcasted_iota(jnp.int32, (R, RP), 0)).astype(F32)
    colb = (lax.broadcasted_iota(jnp.int32, (R, R * L), 1) // L == lax.broadcasted_iota(jnp.int32, (R, R * L), 0)).astype(F32)
    return sel, selt, colb


def _pair_diag(mats, rhs_b, R, P):
    lanes = 2 * P
    lo = lax.broadcasted_iota(jnp.int32, (mats[0].shape[0], lanes), 1) < P
    out = []
    for q in range(R // 2):
        rp = rhs_b[:, q * lanes:(q + 1) * lanes]
        out.append(jnp.where(lo, _dot(mats[2 * q], rp), _dot(mats[2 * q + 1], rp)))
    return jnp.concatenate(out, axis=1) if len(out) > 1 else out[0]


def _ssd_specs(pre, off_x, off_b, off_c, G, R, P, nb, nc, rev):
    N, RP, W = N_STATE, R * P, SSD_STEP_CHUNKS * CHUNK
    ns = nc // SSD_STEP_CHUNKS
    assert nc % SSD_STEP_CHUNKS == 0
    cidx = (lambda c: ns - 1 - c) if rev else (lambda c: c)
    xb, bb, cb = off_x // RP, off_b // N, off_c // N
    assert off_x % RP == 0 and off_b % N == 0 and off_c % N == 0
    row = lambda b, c: b * ns + cidx(c)
    return dict(
        x=pl.BlockSpec((W, RP), lambda g, b, c: (row(b, c), xb + g)),
        b=pl.BlockSpec((W, N), lambda g, b, c: (row(b, c), bb + g)),
        c=pl.BlockSpec((W, N), lambda g, b, c: (row(b, c), cb + g)),
        dtc=pl.BlockSpec((None, W, R), lambda g, b, c: (g, row(b, c), 0)),
        dtr=pl.BlockSpec((None, R, W), lambda g, b, c: (g, 0, row(b, c))),
        hpc=pl.BlockSpec((None, 8, R), lambda g, b, c: (g, 0, 0)),
        hpr=pl.BlockSpec((None, R, 8), lambda g, b, c: (g, 0, 0)),
        y=pl.BlockSpec((W, RP), lambda g, b, c: (row(b, c), g)),
        bc=pl.BlockSpec((W, N), lambda g, b, c: (row(b, c), g)),
        hs=pl.BlockSpec((SSD_STEP_CHUNKS, None, N, RP), lambda g, b, c: (row(b, c), g, 0, 0)),
    )


def _ssd_fwd(pre, offs, dtc, dtr, hpc, hpr, G, R, P, S, name, side=None):
    T = pre.shape[0]
    L, N, RP = CHUNK, N_STATE, R * P
    nc, nb = S // L, T // S
    sp = _ssd_specs(pre, *offs, G, R, P, nb, nc, False)

    def body(px_ref, pb_ref, pc_ref, dtc_ref, dtr_ref, hpc_ref, hpr_ref, y_ref, hs_ref, hst):
        @pl.when(pl.program_id(2) == 0)
        def _():
            hst[...] = jnp.zeros_like(hst)

        hpc_ = hpc_ref[...]
        _, selt, colb = _head_maps(R, P, L)
        hp_e = _dotx(hpc_, selt)
        for k in range(SSD_STEP_CHUNKS):
            rs = slice(k * L, (k + 1) * L)
            xs, bm, cm = _silu(px_ref[rs, :]), _silu(pb_ref[rs, :]), _silu(pc_ref[rs, :])
            dt_c, _, acs_c, acs_r, low, _ = _ssd_common(dtc_ref[rs, :], dtr_ref[:, rs], hpc_, hpr_ref[...], L)
            dt_e, a_e = _dotx(dt_c, selt), _dotx(acs_c, selt)
            a_bc = _dotx(acs_c, colb)
            a_last = a_e[L - 1:L, :]
            bb, cb = bm.astype(BF16), cm.astype(BF16)
            gm = _dot(cb, bb, NT)
            hprev = hst[...]
            hprev_b = hprev.astype(BF16)
            hs_ref[k] = hprev_b
            xdt = xs * dt_e
            xdt_b = xdt.astype(BF16)
            ms = []
            for r in range(R):
                dec = jnp.exp(jnp.where(low, a_bc[:, r * L:(r + 1) * L] - acs_r[r:r + 1, :], -jnp.inf))
                ms.append((gm * dec).astype(BF16))
            y_ref[rs, :] = _pair_diag(ms, xdt_b, R, P) + _dot(cb, hprev_b) * jnp.exp(a_e) + hp_e[2:3, :] * xs
            xw = (xdt * jnp.exp(a_last - a_e)).astype(BF16)
            hst[...] = hprev * jnp.exp(a_last) + _dot(bb, xw, TN)

    return _pc(body, grid=(G, nb, nc // SSD_STEP_CHUNKS),
               in_specs=[sp["x"], sp["b"], sp["c"], sp["dtc"], sp["dtr"], sp["hpc"], sp["hpr"]],
               out_specs=[sp["y"], sp["hs"]],
               out_shape=[SDS((T, G * RP), F32), SDS((nb * nc, G, N, RP), BF16)], name=name,
               scratch=(pltpu.VMEM((N, RP), F32),), side=side)(pre, pre, pre, dtc, dtr, hpc, hpr)


def _ssd_bwd(pre, offs, dtc, dtr, hpc, hpr, hs, dy, G, R, P, S, name, side=None):
    T = pre.shape[0]
    L, N, RP = CHUNK, N_STATE, R * P
    nc, nb = S // L, T // S
    sp = _ssd_specs(pre, *offs, G, R, P, nb, nc, True)

    def chunk(px_ref, pb_ref, pc_ref, dtc_ref, dtr_ref, hpc_ref, hpr_ref, hs_ref, dy_ref,
              dpx_ref, dpb_ref, dpc_ref, ddt_ref, hpg_ref, dhst):
        px, pb, pcc = px_ref[...], pb_ref[...], pc_ref[...]
        xs, bm, cm = _silu(px), _silu(pb), _silu(pcc)
        hpc_ = hpc_ref[...]
        dtc_raw = dtc_ref[...]
        dt_c, a_c, acs_c, acs_r, low, upp = _ssd_common(dtc_raw, dtr_ref[...], hpc_, hpr_ref[...], L)
        sel, selt, colb = _head_maps(R, P, L)
        dt_e, a_e, hp_e = _dotx(dt_c, selt), _dotx(acs_c, selt), _dotx(hpc_, selt)
        a_bc = _dotx(acs_c, colb)
        a_last = a_e[L - 1:L, :]
        e_e, w_e = jnp.exp(a_e), jnp.exp(a_last - a_e)
        bb, cb = bm.astype(BF16), cm.astype(BF16)
        gm = _dot(cb, bb, NT)
        gmt = _dot(bb, cb, NT)
        hprev = hs_ref[...]
        dhn = dhst[...]
        dhn_b = dhn.astype(BF16)
        dy = dy_ref[...]
        dy_b = dy.astype(BF16)
        xdt = xs * dt_e
        xdt_b = xdt.astype(BF16)
        yoff = _dot(cb, hprev) * e_e
        dye_b = (dy * e_e).astype(BF16)
        dcm = _dot(dye_b, hprev, NT)
        dhst[...] = _dot(cb, dye_b, TN) + jnp.exp(a_last) * dhn
        dxdt_st = _dot(bb, dhn_b) * w_e
        dbm = _dot((xdt * w_e).astype(BF16), dhn_b, NT)
        lanes = 2 * P
        lo = lax.broadcasted_iota(jnp.int32, (L, lanes), 1) < P
        dg = jnp.zeros((L, L), F32)
        es, css = [], []
        for r in range(R):
            col_b, row = a_bc[:, r * L:(r + 1) * L], acs_r[r:r + 1, :]
            dec = jnp.exp(jnp.where(low, col_b - row, -jnp.inf))
            q = r // 2
            dyp = dy_b[:, q * lanes:(q + 1) * lanes]
            dyp = jnp.where(lo if r % 2 == 0 else ~lo, dyp, jnp.zeros_like(dyp))
            dm = _dot(dyp, xdt_b[:, q * lanes:(q + 1) * lanes], NT)
            dg = dg + dm * dec
            e = dm * (gm * dec)
            es.append(e)
            css.append(jnp.sum(e, axis=0, keepdims=True))
        dgb = dg.astype(BF16)
        dcm = dcm + _dot(dgb, bb)
        dbm = dbm + _dot(dgb, cb, TN)
        colbt = (lax.broadcasted_iota(jnp.int32, (R * L, R), 0) // L
                 == lax.broadcasted_iota(jnp.int32, (R * L, R), 1)).astype(F32)
        eye = (lax.broadcasted_iota(jnp.int32, (R, R), 0) == lax.broadcasted_iota(jnp.int32, (R, R), 1)).astype(F32)
        row_sums = _dotx(jnp.concatenate(es, axis=1), colbt)
        col_sums = _dotx(jnp.concatenate(css, axis=0), eye, dims=TN)
        mts = []
        for r in range(R):
            dect = jnp.exp(jnp.where(upp, acs_r[r:r + 1, :] - a_bc[:, r * L:(r + 1) * L], -jnp.inf))
            mts.append((gmt * dect).astype(BF16))
        dxdt = _pair_diag(mts, dy_b, R, P) + dxdt_st
        q_st = _dotx(xdt * dxdt_st, sel, parts=1)
        da = row_sums - col_sums + _dotx(dy * yoff, sel, parts=1) - q_st
        hh = jnp.sum(_dotx(dhn * hprev.astype(F32), sel, parts=1), axis=0, keepdims=True)
        da_last = jnp.exp(acs_c[L - 1:L, :]) * hh + jnp.sum(q_st, axis=0, keepdims=True)
        rowi = lax.broadcasted_iota(jnp.int32, (L, R), 0)
        da = da + jnp.where(rowi == L - 1, da_last, 0.0)
        dpx_ref[...] = (dxdt * dt_e + hp_e[2:3, :] * dy) * _dsilu(px)
        dpb_ref[...] = dbm * _dsilu(pb)
        dpc_ref[...] = dcm * _dsilu(pcc)
        dadt = _dotx(upp, da, split="b")
        ddt = _dotx(dxdt * xs, sel, parts=1) + dadt * a_c
        ddt_raw = ddt * jax.nn.sigmoid(dtc_raw + hpc_[0:1, :])
        ddt_ref[...] = ddt_raw
        d_a = jnp.sum(dadt * dt_c, axis=0, keepdims=True)
        d_d = jnp.sum(_dotx(dy * xs, sel, parts=1), axis=0, keepdims=True)
        rows = [jnp.sum(ddt_raw, axis=0, keepdims=True), d_a * a_c, d_d, jnp.zeros((5, R), F32)]
        hpg_ref[...] += jnp.concatenate(rows, axis=0)

    def body(px_ref, pb_ref, pc_ref, dtc_ref, dtr_ref, hpc_ref, hpr_ref, hs_ref, dy_ref,
             dpx_ref, dpb_ref, dpc_ref, ddt_ref, hpg_ref, dhst):
        bi, ci = pl.program_id(1), pl.program_id(2)

        @pl.when(ci == 0)
        def _():
            dhst[...] = jnp.zeros_like(dhst)

        @pl.when((bi == 0) & (ci == 0))
        def _():
            hpg_ref[...] = jnp.zeros_like(hpg_ref)

        for k in reversed(range(SSD_STEP_CHUNKS)):
            rs = pl.ds(k * L, L)
            chunk(px_ref.at[rs, :], pb_ref.at[rs, :], pc_ref.at[rs, :], dtc_ref.at[rs, :], dtr_ref.at[:, rs], hpc_ref,
                  hpr_ref, hs_ref.at[k], dy_ref.at[rs, :], dpx_ref.at[rs, :], dpb_ref.at[rs, :], dpc_ref.at[rs, :],
                  ddt_ref.at[rs, :], hpg_ref, dhst)

    return _pc(body, grid=(G, nb, nc // SSD_STEP_CHUNKS),
               in_specs=[sp["x"], sp["b"], sp["c"], sp["dtc"], sp["dtr"], sp["hpc"], sp["hpr"], sp["hs"], sp["y"]],
               out_specs=[sp["y"], sp["bc"], sp["bc"], sp["dtc"], pl.BlockSpec((None, 8, R), lambda g, b, c: (g, 0, 0))],
               out_shape=[SDS((T, G * RP), F32), SDS((T, G * N), F32), SDS((T, G * N), F32), SDS((G, T, R), F32),
                          SDS((G, 8, R), F32)], name=name,
               scratch=(pltpu.VMEM((N, RP), F32),), side=side)(pre, pre, pre, dtc, dtr, hpc, hpr, hs, dy)


def _gate_norm_fwd(y, zview, ng, G, S, name):
    T, DI = y.shape
    zarr, zoff = zview
    gw = DI // G
    tm = _tile(S, 1024, 8)
    zb = zoff // gw
    assert zoff % gw == 0

    def body(y_ref, z_ref, g_ref, o_ref):
        yg = y_ref[...] * _silu(z_ref[...])
        r = lax.rsqrt(jnp.mean(yg * yg, axis=-1, keepdims=True) + EPS)
        o_ref[...] = (yg * r * g_ref[...]).astype(BF16)

    return _pc(body, grid=(T // tm, G),
               in_specs=[pl.BlockSpec((tm, gw), lambda i, g: (i, g)), pl.BlockSpec((tm, gw), lambda i, g: (i, zb + g)),
                         pl.BlockSpec((1, gw), lambda i, g: (0, g))],
               out_specs=pl.BlockSpec((tm, gw), lambda i, g: (i, g)), out_shape=SDS((T, DI), BF16), name=name)(y, zarr, ng)


def _gate_norm_bwd(y, zview, ng, dyn, G, S, name):
    T, DI = y.shape
    zarr, zoff = zview
    gw = DI // G
    tm = _tile(S, 1024, 8)
    zb = zoff // gw

    def body(y_ref, z_ref, g_ref, d_ref, dy_ref, dz_ref, dg_ref):
        @pl.when(pl.program_id(1) == 0)
        def _():
            dg_ref[...] = jnp.zeros_like(dg_ref)

        y_, z, d = y_ref[...], z_ref[...], d_ref[...]
        sz = _silu(z)
        yg = y_ * sz
        r = lax.rsqrt(jnp.mean(yg * yg, axis=-1, keepdims=True) + EPS)
        n = yg * r
        dn = d * g_ref[...]
        dyg = r * (dn - n * jnp.mean(dn * n, axis=-1, keepdims=True))
        dy_ref[...] = dyg * sz
        dz_ref[...] = (dyg * y_ * _dsilu(z)).astype(BF16)
        dg_ref[...] += _bsum(jnp.sum(d * n, axis=0, keepdims=True))

    return _pc(body, grid=(G, T // tm),
               in_specs=[pl.BlockSpec((tm, gw), lambda g, i: (i, g)), pl.BlockSpec((tm, gw), lambda g, i: (i, zb + g)),
                         pl.BlockSpec((1, gw), lambda g, i: (0, g)), pl.BlockSpec((tm, gw), lambda g, i: (i, g))],
               out_specs=[pl.BlockSpec((tm, gw), lambda g, i: (i, g)), pl.BlockSpec((tm, gw), lambda g, i: (i, g)),
                          pl.BlockSpec((8, gw), lambda g, i: (0, g))],
               out_shape=[SDS((T, DI), F32), SDS((T, DI), BF16), SDS((8, DI), F32)], name=name)(y, zarr, ng, dyn)


def _shortconv_fwd(proj, off_b, off_c, off_h, C, w8, S, name):
    K = SC_CONV_K
    _, ob, T, tm, tc, tps = _conv_geom((proj, off_b), C, S, cap=1024)
    oc, oh = off_c // tc, off_h // tc

    def body(b_ref, c_ref, h_ref, cp_ref, hp_ref, w_ref, o_ref, buf):
        first = (pl.program_id(0) % tps) == 0
        buf[0:8, :] = jnp.where(first, 0.0, cp_ref[...] * hp_ref[...])
        buf[8:, :] = c_ref[...] * h_ref[...]

        def strip(r0, carry):
            conv = _taps(buf[pl.ds(r0, STRIP + 8), :], w_ref, K, 8, STRIP)
            o_ref[pl.ds(r0, STRIP), :] = (b_ref[pl.ds(r0, STRIP), :] * conv).astype(BF16)
            return carry

        _strips(tm, strip)

    blk = lambda o: pl.BlockSpec((tm, tc), lambda i, j: (i, o + j))
    return _pc(body, grid=(T // tm, C // tc),
               in_specs=[blk(ob), blk(oc), blk(oh), _prev_spec(tm, tc, oc, "ij"), _prev_spec(tm, tc, oh, "ij"),
                         pl.BlockSpec((8, tc), lambda i, j: (0, j))],
               out_specs=pl.BlockSpec((tm, tc), lambda i, j: (i, j)), out_shape=SDS((T, C), BF16), name=name,
               scratch=(pltpu.VMEM((tm + 8, tc), F32),))(proj, proj, proj, proj, proj, w8)


def _shortconv_bwd(proj, off_b, off_c, off_h, C, w8, dsc, S, name):
    K = SC_CONV_K
    _, ob, T, tm, tc, tps = _conv_geom((proj, off_b), C, S, cap=1024)
    oc, oh = off_c // tc, off_h // tc

    def body(b_ref, c_ref, h_ref, cp_ref, hp_ref, bn_ref, d_ref, dn_ref, w_ref,
             db_ref, dc_ref, dh_ref, dw_ref, buf, buf2):
        i = pl.program_id(1)

        @pl.when(i == 0)
        def _():
            dw_ref[...] = jnp.zeros_like(dw_ref)

        first = (i % tps) == 0
        last = (i % tps) == tps - 1
        buf[0:8, :] = jnp.where(first, 0.0, cp_ref[...] * hp_ref[...])
        buf[8:, :] = c_ref[...] * h_ref[...]
        buf2[0:tm, :] = d_ref[...] * b_ref[...]
        buf2[tm:tm + 8, :] = jnp.where(last, 0.0, dn_ref[...] * bn_ref[...])

        def strip(r0, carry):
            rows = pl.ds(r0, STRIP)
            vwin = buf[pl.ds(r0, STRIP + 8), :]
            vs = [vwin[8 - j:8 - j + STRIP] for j in range(K)]
            conv = vs[0] * w_ref[K - 1:K, :]
            for j in range(1, K):
                conv = conv + vs[j] * w_ref[K - 1 - j:K - j, :]
            db_ref[rows, :] = (d_ref[rows, :] * conv).astype(BF16)
            dwin = buf2[pl.ds(r0, STRIP + 8), :]
            dv = _taps_t(dwin, w_ref, K, STRIP)
            dc_ref[rows, :] = (dv * h_ref[rows, :]).astype(BF16)
            dh_ref[rows, :] = (dv * c_ref[rows, :]).astype(BF16)
            dconv = dwin[0:STRIP]
            sums = [_fold8(dconv * vs[K - 1 - k]) for k in range(K)]
            return tuple(c + s for c, s in zip(carry, sums))

        acc = _strips(tm, strip, tuple(jnp.zeros((8, tc), F32) for _ in range(K)))
        dw_ref[...] += _rows8([jnp.sum(a, axis=0, keepdims=True) for a in acc])

    blk = lambda o: pl.BlockSpec((tm, tc), lambda j, i: (i, o + j))
    out = pl.BlockSpec((tm, tc), lambda j, i: (i, j))
    return _pc(body, grid=(C // tc, T // tm),
               in_specs=[blk(ob), blk(oc), blk(oh), _prev_spec(tm, tc, oc, "ji"), _prev_spec(tm, tc, oh, "ji"),
                         _next_spec(T, tm, tc, ob, "ji"), blk(0), _next_spec(T, tm, tc, 0, "ji"),
                         pl.BlockSpec((8, tc), lambda j, i: (0, j))],
               out_specs=[out, out, out, pl.BlockSpec((8, tc), lambda j, i: (0, j))],
               out_shape=[SDS((T, C), BF16)] * 3 + [SDS((8, C), F32)], name=name,
               scratch=(pltpu.VMEM((tm + 8, tc), F32), pltpu.VMEM((tm + 8, tc), F32)))(
                   proj, proj, proj, proj, proj, proj, dsc, dsc, w8)


def _merge_fwd(proj, off_g1, off_g2, y1, y2, S, name):
    T, D = y1.shape
    tm = _row_tile(S)
    o1, o2 = off_g1 // D, off_g2 // D
    assert off_g1 % D == 0 and off_g2 % D == 0

    def body(g1_ref, g2_ref, y1_ref, y2_ref, o_ref):
        def strip(r0, carry):
            rows = pl.ds(r0, STRIP)
            o_ref[rows, :] = (jax.nn.sigmoid(g1_ref[rows, :]) * y1_ref[rows, :]
                              + jax.nn.sigmoid(g2_ref[rows, :]) * y2_ref[rows, :]).astype(BF16)
            return carry

        _strips(tm, strip)

    row = pl.BlockSpec((tm, D), lambda i: (i, 0))
    return _pc(body, grid=(T // tm,),
               in_specs=[pl.BlockSpec((tm, D), lambda i: (i, o1)), pl.BlockSpec((tm, D), lambda i: (i, o2)), row, row],
               out_specs=row, out_shape=SDS((T, D), BF16), name=name)(proj, proj, y1, y2)


def _merge_bwd(proj, off_g1, off_g2, y1, y2, dm, S, name):
    T, D = y1.shape
    tm = _row_tile(S)
    o1, o2 = off_g1 // D, off_g2 // D

    def body(g1_ref, g2_ref, y1_ref, y2_ref, d_ref, dy1_ref, dy2_ref, dg1_ref, dg2_ref):
        def strip(r0, carry):
            rows = pl.ds(r0, STRIP)
            d = d_ref[rows, :]
            s1, s2 = jax.nn.sigmoid(g1_ref[rows, :]), jax.nn.sigmoid(g2_ref[rows, :])
            dy1_ref[rows, :] = (d * s1).astype(BF16)
            dy2_ref[rows, :] = (d * s2).astype(BF16)
            dg1_ref[rows, :] = (d * y1_ref[rows, :] * s1 * (1.0 - s1)).astype(BF16)
            dg2_ref[rows, :] = (d * y2_ref[rows, :] * s2 * (1.0 - s2)).astype(BF16)
            return carry

        _strips(tm, strip)

    row = pl.BlockSpec((tm, D), lambda i: (i, 0))
    return _pc(body, grid=(T // tm,),
               in_specs=[pl.BlockSpec((tm, D), lambda i: (i, o1)), pl.BlockSpec((tm, D), lambda i: (i, o2)), row, row, row],
               out_specs=[row] * 4, out_shape=[SDS((T, D), BF16)] * 4, name=name)(proj, proj, y1, y2, dm)


def _pad8(w):
    return jnp.pad(w, ((0, 8 - w.shape[0]), (0, 0)))


def _dims(w):
    D = w["mix_pre_g"].shape[-1]
    DI = w["ssd_norm_g"].shape[-1]
    H = w["ssd_dt_bias"].shape[-1]
    conv_dim = w["ssd_conv_b"].shape[-1]
    G = (conv_dim - DI) // (2 * N_STATE)
    F = w["w_down"].shape[0]
    return dict(D=D, DI=DI, H=H, P=DI // H, G=G, R=H // G, GN=G * N_STATE, CD=conv_dim, F=F)


def _proj_layout(d):
    D, DI, CD, H = d["D"], d["DI"], d["CD"], d["H"]
    o = dict(z=0, xbc=DI, scb=DI + CD, scc=DI + CD + D, sch=DI + CD + 2 * D, g1=DI + CD + 3 * D, g2=DI + CD + 4 * D,
             dt=DI + CD + 5 * D)
    o["sb"] = math.gcd(SEG_BLK, D, DI, d["GN"])
    assert o["sb"] % 128 == 0 and H <= o["sb"]
    o["np"] = o["dt"] + o["sb"]
    return o


def _glu_perm(a, F, inverse=False):
    lead = a.shape[:-1]
    nb = F // GLU_W
    if not inverse:
        return a.reshape(*lead, 2, nb, GLU_W).swapaxes(-3, -2).reshape(*lead, 2 * F)
    return a.reshape(*lead, nb, 2, GLU_W).swapaxes(-3, -2).reshape(*lead, 2 * F)


def _glu_perm_rows(a, F, inverse=False):
    nb, D = F // GLU_W, a.shape[1]
    shape = (nb, 2, GLU_W, D) if inverse else (2, nb, GLU_W, D)
    return a.reshape(shape).swapaxes(0, 1).reshape(2 * F, D)


def _prep_layer(w):
    d = _dims(w)
    D, DI, CD, H, G, R, F = d["D"], d["DI"], d["CD"], d["H"], d["G"], d["R"], d["F"]
    lay = _proj_layout(d)
    w_in = w["w_in"]
    used = lay["dt"] + H
    wcat = jnp.concatenate([w_in[:DI + CD], w_in[DI + CD + H:], w_in[DI + CD:DI + CD + H],
                            jnp.zeros((lay["np"] - used, D), w_in.dtype)], axis=0)
    hp = jnp.stack([w["ssd_dt_bias"], w["ssd_a_log"], w["ssd_d"]], 0).astype(F32)
    hpc = jnp.pad(hp.reshape(3, G, R).transpose(1, 0, 2), ((0, 0), (0, 5), (0, 0)))
    hpr = jnp.pad(hp[:2].reshape(2, G, R).transpose(1, 2, 0), ((0, 0), (0, 0), (0, 6)))
    row = lambda v: v.reshape(1, -1).astype(F32)
    return dict(
        d=d, lay=lay, ada_w=w["ada_w"].astype(BF16), ada_b=row(w["ada_b"]),
        mix_pre_g=row(w["mix_pre_g"]), mix_post_g=row(w["mix_post_g"]), wcat=wcat.astype(BF16),
        ssd_conv_w=_pad8(w["ssd_conv_w"].astype(F32)), ssd_conv_b=row(w["ssd_conv_b"]), hpc=hpc, hpr=hpr,
        ssd_norm_g=row(w["ssd_norm_g"]), w_ssd_out=w["w_ssd_out"].astype(BF16),
        sc_conv_w=_pad8(w["sc_conv_w"].astype(F32)), w_sc_out=w["w_sc_out"].astype(BF16), w_o=w["w_o"].astype(BF16),
        ffn_pre_g=row(w["ffn_pre_g"]), ffn_post_g=row(w["ffn_post_g"]),
        w_up=_glu_perm_rows(w["w_up"], F).astype(BF16), ffn_conv_w=_pad8(_glu_perm(w["ffn_conv_w"].astype(F32), F)),
        ffn_conv_b=_glu_perm(row(w["ffn_conv_b"]), F), w_down=w["w_down"].astype(BF16))


def _dt_layouts(proj, lay, d):
    T = proj.shape[0]
    dt = proj[:, lay["dt"]:lay["dt"] + d["H"]].reshape(T, d["G"], d["R"])
    return dt.transpose(1, 0, 2), dt.transpose(1, 2, 0)


def _layer_fwd(x, c8, p, S, li, gather=None):
    d, lay = p["d"], p["lay"]
    D, DI, G, R, P, GN, CD = d["D"], d["DI"], d["G"], d["R"], d["P"], d["GN"], d["CD"]
    nb = x.shape[0] // S
    nm = lambda s: f"l{li}_{s}"
    mod, cact = _modk(c8, p["ada_w"], p["ada_b"], nm("mod"))
    mod3 = mod[:nb].reshape(nb, 1, 6 * D)
    h = _norm_mod(x, p["mix_pre_g"], mod3, 1, 0, S, nm("norm1"))
    proj = _mm(h, p["wcat"], "nt", F32, nm("mm_in"), caps=(1024, 1536, 2048))
    pre = _conv_fwd((proj, lay["xbc"]), CD, p["ssd_conv_w"], p["ssd_conv_b"], SSD_CONV_K, S, nm("ssdconv"))
    dtc, dtr = _dt_layouts(proj, lay, d)
    offs = (0, DI, DI + GN)
    gathered = None
    if gather is None:
        y, hs = _ssd_fwd(pre, offs, dtc, dtr, p["hpc"], p["hpr"], G, R, P, S, nm("ssd"))
    else:
        (y, hs), (gathered,) = _ssd_fwd(pre, offs, dtc, dtr, p["hpc"], p["hpr"], G, R, P, S, nm("ssd"),
                                        side=_side_gather_direct(gather))
    yn = _gate_norm_fwd(y, (proj, lay["z"]), p["ssd_norm_g"], G, S, nm("gnorm"))
    sc = _shortconv_fwd(proj, lay["scb"], lay["scc"], lay["sch"], D, p["sc_conv_w"], S, nm("sconv"))
    if gather is None:
        y_ssd = _mm(yn, p["w_ssd_out"], "nn", F32, nm("mm_ssdout"))
    else:
        y_ssd, (gathered,) = _mm(yn, p["w_ssd_out"], "nn", F32, nm("mm_ssdout"), side=_side_gather_forward(gathered))
    y_sc = _mm(sc, p["w_sc_out"], "nn", F32, nm("mm_scout"))
    m = _merge_fwd(proj, lay["g1"], lay["g2"], y_ssd, y_sc, S, nm("merge"))
    mix = _mm(m, p["w_o"], "nn", F32, nm("mm_o"))
    x1 = _resid_post(x, mix, mod3, 2, p["mix_post_g"], S, nm("post1"))
    h2 = _norm_mod(x1, p["ffn_pre_g"], mod3, 4, 3, S, nm("norm2"))
    uu = _mm(h2, p["w_up"], "nt", F32, nm("mm_up"), caps=(1024, 1408, 2048))
    a = _ffn_act_fwd(uu, p["ffn_conv_w"], p["ffn_conv_b"], S, nm("ffnact"))
    f = _mm(a, p["w_down"], "nn", F32, nm("mm_down"), caps=(1024, 1024, 1408))
    x2 = _resid_post(x1, f, mod3, 5, p["ffn_post_g"], S, nm("post2"))
    saved = dict(x=x, h=h, proj=proj, pre=pre, dtc=dtc, dtr=dtr, y=y, hs=hs, yn=yn, sc=sc, y_ssd=y_ssd, y_sc=y_sc,
                 m=m, mix=mix, x1=x1, h2=h2, uu=uu, a=a, f=f, mod3=mod3, cact=cact)
    return x2, saved, gathered


def _seq_sum(acc, nb):
    return acc.reshape(nb, 8, -1)[:, 0, :]


def _layer_bwd(dx2, p, s, S, li, chip_sums=None, early=None):
    d, lay = p["d"], p["lay"]
    D, DI, G, R, P, GN, CD, H, F = d["D"], d["DI"], d["G"], d["R"], d["P"], d["GN"], d["CD"], d["H"], d["F"]
    nb = dx2.shape[0] // S
    nm = lambda t: f"l{li}_{t}"
    mod3 = s["mod3"]
    g = {}
    exchanged = None
    df, dgt2, dpg2 = _post_bwd(s["f"], mod3, 5, p["ffn_post_g"], dx2, S, nm("post2_b"))
    g["ffn_post_g"] = dpg2[0]
    da = _mm(df, p["w_down"], "nt", BF16, nm("mm_down_bi"), caps=(1024, 1408, 2048))
    g["w_down"] = _mm(s["a"], df, "tn", WGRAD,nm("mm_down_bw"), caps=(1408, 1024, 1024))
    if chip_sums is None:
        duu, cw = _ffn_act_bwd(s["uu"], da, p["ffn_conv_w"], p["ffn_conv_b"], S, nm("ffnact_b"))
    else:
        (duu, cw), (exchanged,) = _ffn_act_bwd(s["uu"], da, p["ffn_conv_w"], p["ffn_conv_b"], S, nm("ffnact_b"),
                                               side=_side_chip_exchange(chip_sums))
    g["ffn_conv_w"] = _glu_perm(cw[:FFN_CONV_K], F, inverse=True)
    g["ffn_conv_b"] = _glu_perm(cw[FFN_CONV_K], F, inverse=True)
    dh2 = _mm(duu, p["w_up"], "nn", F32, nm("mm_up_bi"), caps=(1024, 1024, 2816))
    g["w_up"] = _glu_perm_rows(_mm(duu, s["h2"], "tn", WGRAD,nm("mm_up_bw"), caps=(1408, 1024, 1024)), F, inverse=True)
    dx1, dg2, dsc2, dsh2 = _pre_bwd(s["x1"], p["ffn_pre_g"], mod3, 4, dh2, dx2, S, nm("norm2_b"))
    g["ffn_pre_g"] = dg2[0]
    dmix, dgt1, dpg1 = _post_bwd(s["mix"], mod3, 2, p["mix_post_g"], dx1, S, nm("post1_b"))
    g["mix_post_g"] = dpg1[0]
    dm = _mm(dmix, p["w_o"], "nt", F32, nm("mm_o_bi"))
    g["w_o"] = _mm(s["m"], dmix, "tn", WGRAD,nm("mm_o_bw"))
    proj = s["proj"]
    dy_ssd, dy_sc, dg1, dg2_ = _merge_bwd(proj, lay["g1"], lay["g2"], s["y_ssd"], s["y_sc"], dm, S, nm("merge_b"))
    dyn = _mm(dy_ssd, p["w_ssd_out"], "nt", F32, nm("mm_ssdout_bi"))
    g["w_ssd_out"] = _mm(s["yn"], dy_ssd, "tn", WGRAD,nm("mm_ssdout_bw"))
    dsc = _mm(dy_sc, p["w_sc_out"], "nt", F32, nm("mm_scout_bi"))
    g["w_sc_out"] = _mm(s["sc"], dy_sc, "tn", WGRAD,nm("mm_scout_bw"))
    dscb, dscc, dsch, scw = _shortconv_bwd(proj, lay["scb"], lay["scc"], lay["sch"], D, p["sc_conv_w"], dsc, S, nm("sconv_b"))
    g["sc_conv_w"] = scw[:SC_CONV_K]
    dy, dz, dng = _gate_norm_bwd(s["y"], (proj, lay["z"]), p["ssd_norm_g"], dyn, G, S, nm("gnorm_b"))
    g["ssd_norm_g"] = dng[0]
    offs = (0, DI, DI + GN)
    early_side = None if early is None else _side_chip_exchange(early(g))
    res = _ssd_bwd(s["pre"], offs, s["dtc"], s["dtr"], p["hpc"], p["hpr"], s["hs"], dy, G, R, P, S, nm("ssd_b"),
                   side=early_side)
    (dpx, dpb, dpc, ddt, hpg), early_got = res if early is not None else (res, None)
    g["ssd_dt_bias"], g["ssd_a_log"], g["ssd_d"] = hpg[:, 0, :].reshape(H), hpg[:, 1, :].reshape(H), hpg[:, 2, :].reshape(H)
    cws, dxbc = [], []
    for name, darr, off, C in (("x", dpx, 0, DI), ("b", dpb, DI, GN), ("c", dpc, DI + GN, GN)):
        w8 = p["ssd_conv_w"][:, off:off + C]
        cws.append(_conv_bwd_w((darr, 0), (proj, lay["xbc"] + off), C, SSD_CONV_K, S, nm(f"ssdconv_bw_{name}")))
        dxbc.append(_conv_bwd_in((darr, 0), C, w8, SSD_CONV_K, S, BF16, nm(f"ssdconv_bi_{name}")))
    cws = jnp.concatenate(cws, axis=1)
    g["ssd_conv_w"], g["ssd_conv_b"] = cws[:SSD_CONV_K], cws[SSD_CONV_K]
    T = dx2.shape[0]
    ddt_t = jnp.pad(ddt.transpose(1, 0, 2).reshape(T, H).astype(BF16), ((0, 0), (0, lay["sb"] - H)))
    dproj = [dz] + dxbc + [dscb, dscc, dsch, dg1, dg2_, ddt_t]
    dh = _mm_seg(dproj, p["wcat"], "nn", F32, nm("mm_in_bi"), lay["sb"])
    dwcat = _mm_seg(dproj, s["h"], "tn", WGRAD, nm("mm_in_bw"), lay["sb"], tk=1024)
    o = lay
    g["w_in"] = jnp.concatenate([dwcat[o["z"]:o["scb"]], dwcat[o["dt"]:o["dt"] + H], dwcat[o["scb"]:o["dt"]]], axis=0)
    dx, dg1_, dsc1, dsh1 = _pre_bwd(s["x"], p["mix_pre_g"], mod3, 1, dh, dx1, S, nm("norm1_b"))
    g["mix_pre_g"] = dg1_[0]
    dmod = jnp.concatenate([_seq_sum(t, nb) for t in (dsh1, dsc1, dgt1, dsh2, dsc2, dgt2)], axis=1)
    dmod8 = jnp.pad(dmod, ((0, MOD_ROWS - nb), (0, 0)))
    g["ada_b"] = _colsum(dmod8, nm("adab"))
    g["ada_w"] = _mm(dmod8, s["cact"], "tn", WGRAD, nm("mm_ada_bw"), caps=(1536, 1024, 2048))
    return dx, g, exchanged, None if early_got is None else early_got[0]


def _colsum(a8, name):
    rows, C = a8.shape
    tc = _tile(C, 2048)

    def body(a_ref, o_ref):
        o_ref[...] = _bsum(jnp.sum(a_ref[...], axis=0, keepdims=True))

    return _pc(body, grid=(C // tc,), in_specs=[pl.BlockSpec((rows, tc), lambda j: (0, j))],
               out_specs=pl.BlockSpec((8, tc), lambda j: (0, j)), out_shape=SDS((8, C), F32), name=name)(a8)[0]


def _adam(gs, w, m, v, name):
    ns, R, W = gs.shape
    tr = _tile(R, 256, 8)

    def body(g_ref, w_ref, m_ref, v_ref, go_ref, d_ref, mo_ref, vo_ref):
        g = g_ref[0].astype(F32)
        for k in range(1, ns):
            g = g + g_ref[k].astype(F32)
        go_ref[...] = g
        d_ref[...], mo_ref[...], vo_ref[...] = _adam_update(g, w_ref[...], m_ref[...], v_ref[...])

    row = pl.BlockSpec((tr, W), lambda i: (i, 0))
    return _pc(body, grid=(R // tr,), in_specs=[pl.BlockSpec((ns, tr, W), lambda i: (0, i, 0)), row, row, row],
               out_specs=[row] * 4, out_shape=[SDS((R, W), F32)] * 4, name=name)(gs, w, m, v)


def _adam_update(g, w, m, v):
    c1 = 1.0 / (1.0 - ADAM_B1 ** ADAM_STEP)
    c2 = 1.0 / (1.0 - ADAM_B2 ** ADAM_STEP)
    m_ = ADAM_B1 * m + (1.0 - ADAM_B1) * g
    v_ = ADAM_B2 * v + (1.0 - ADAM_B2) * (g * g)
    return -ADAM_LR * ((m_ * c1) / (jnp.sqrt(v_ * c2) + ADAM_EPS) + ADAM_WD * w), m_, v_


def _adam_nat(g, w, m, v, name):
    depth, a, b = w.shape
    tr = _tile(a, 256, 8)

    def body(g_ref, w_ref, m_ref, v_ref, d_ref, mo_ref, vo_ref):
        d_ref[...], mo_ref[...], vo_ref[...] = _adam_update(g_ref[...], w_ref[...], m_ref[...], v_ref[...])

    blk = pl.BlockSpec((None, tr, b), lambda l, i: (l, i, 0))
    return _pc(body, grid=(depth, a // tr), in_specs=[blk] * 4, out_specs=[blk] * 3,
               out_shape=[SDS(w.shape, F32)] * 3, name=name)(g, w, m, v)


def _adam_mid(g, w, m, v, name):
    b, depth, a = w.shape
    tr = 128

    def body(g_ref, w_ref, m_ref, v_ref, d_ref, mo_ref, vo_ref):
        d_ref[...], mo_ref[...], vo_ref[...] = _adam_update(g_ref[...], w_ref[...], m_ref[...], v_ref[...])

    blk = pl.BlockSpec((tr, depth, a), lambda i: (i, 0, 0))
    return _pc(body, grid=(pl.cdiv(b, tr),), in_specs=[blk] * 4, out_specs=[blk] * 3,
               out_shape=[SDS(w.shape, F32)] * 3, name=name)(g, w, m, v)


def _sum_chips(gs, name):
    ns, R, W = gs.shape
    tr = _tile(R, 256, 16)

    def body(g_ref, o_ref):
        acc = g_ref[0].astype(F32)
        for k in range(1, ns):
            acc = acc + g_ref[k].astype(F32)
        o_ref[...] = acc

    return _pc(body, grid=(R // tr,), in_specs=[pl.BlockSpec((ns, tr, W), lambda i: (0, i, 0))],
               out_specs=pl.BlockSpec((tr, W), lambda i: (i, 0)), out_shape=SDS((R, W), F32), name=name)(gs)


HBM_SPEC = pl.BlockSpec(memory_space=pltpu.HBM)
VMEM_SPEC = pl.BlockSpec(memory_space=pltpu.VMEM)


def _dev():
    return lax.axis_index("x"), lax.axis_index("y"), lax.axis_index("c")


def _allgather_big(loc, name):
    R, W = loc.shape

    def body(x_ref, out_ref, send_sems, recv_sems, local_sem):
        x, y, c = _dev()
        me, sibling = (x, y, c), (x, y, 1 - c)
        chips = [(1 - x, y), (x, 1 - y), (1 - x, 1 - y)]

        def slab(px, py, pc):
            return out_ref.at[4 * px + 2 * py + pc]

        def copy(k, block, to, src=None):
            return pltpu.make_async_remote_copy(
                src_ref=slab(*block) if src is None else src, dst_ref=slab(*block),
                send_sem=send_sems.at[k], recv_sem=recv_sems.at[k], device_id=to, device_id_type=MESH)

        mine = pltpu.make_async_copy(x_ref, slab(*me), local_sem)
        mine.start()
        first = [copy(0, me, sibling, src=x_ref)]
        first += [copy(1 + j, me, (*chip, c), src=x_ref) for j, chip in enumerate(chips)]
        for cp in first:
            cp.start()
        passed = [copy(4 + j, (*chip, c), sibling) for j, chip in enumerate(chips)]
        for j, chip in enumerate(chips):
            copy(1 + j, (*chip, c), me).wait_recv()
            passed[j].start()
        copy(0, sibling, me).wait_recv()
        for j, chip in enumerate(chips):
            copy(4 + j, (*chip, 1 - c), me).wait_recv()
        for cp in first + passed:
            cp.wait_send()
        mine.wait()

    return pl.pallas_call(
        body, out_shape=SDS((N_DEV, R, W), loc.dtype), in_specs=[HBM_SPEC], out_specs=HBM_SPEC,
        scratch_shapes=[pltpu.SemaphoreType.DMA((7,)), pltpu.SemaphoreType.DMA((7,)), pltpu.SemaphoreType.DMA],
        name=name)(loc)


def _dma_sems(n):
    return (pltpu.SemaphoreType.DMA((n,)), pltpu.SemaphoreType.DMA((n,)), pltpu.SemaphoreType.DMA)


def _side_gather_direct(loc):
    R, W = loc.shape

    def copies(ins, outs, sems):
        x, y, c = _dev()
        x_ref, out = ins[0], outs[0]
        me = 4 * x + 2 * y + c
        peers = [(x, y, 1 - c), (1 - x, y, c), (x, 1 - y, c), (1 - x, 1 - y, c)]
        mk = lambda k, p, dst: pltpu.make_async_remote_copy(
            src_ref=x_ref, dst_ref=out.at[dst], send_sem=sems[0].at[k], recv_sem=sems[1].at[k], device_id=p,
            device_id_type=MESH)
        sends = [mk(k, p, me) for k, p in enumerate(peers)]
        recvs = [mk(k, p, 4 * p[0] + 2 * p[1] + p[2]) for k, p in enumerate(peers)]
        return sends, recvs, pltpu.make_async_copy(x_ref, out.at[me], sems[2])

    def start(ins, outs, sems):
        sends, _, mine = copies(ins, outs, sems)
        mine.start()
        for cp in sends:
            cp.start()

    def wait(ins, outs, sems):
        sends, recvs, mine = copies(ins, outs, sems)
        for cp in recvs:
            cp.wait_recv()
        for cp in sends:
            cp.wait_send()
        mine.wait()

    return _Side((loc,), (SDS((N_DEV, R, W), loc.dtype),), _dma_sems(4), start, wait)


def _side_gather_forward(buf):
    def copies(ins, outs, sems):
        x, y, c = _dev()
        out = outs[0]
        chips = [(1 - x, y), (x, 1 - y), (1 - x, 1 - y)]
        mk = lambda k, src, dst: pltpu.make_async_remote_copy(
            src_ref=out.at[src], dst_ref=out.at[dst], send_sem=sems[0].at[k], recv_sem=sems[1].at[k],
            device_id=(x, y, 1 - c), device_id_type=MESH)
        mine = [4 * px + 2 * py + c for px, py in chips]
        theirs = [4 * px + 2 * py + (1 - c) for px, py in chips]
        return [mk(k, s, s) for k, s in enumerate(mine)], [mk(k, s, t) for k, (s, t) in enumerate(zip(mine, theirs))]

    def start(ins, outs, sems):
        for cp in copies(ins, outs, sems)[0]:
            cp.start()

    def wait(ins, outs, sems):
        sends, recvs = copies(ins, outs, sems)
        for cp in recvs:
            cp.wait_recv()
        for cp in sends:
            cp.wait_send()

    return _Side((buf,), (SDS(buf.shape, buf.dtype),), _dma_sems(3)[:2], start, wait, {0: 0})


def _side_chip_exchange(p):
    def copies(ins, outs, sems):
        x, y, c = _dev()
        p_ref, out = ins[0], outs[0]
        j0 = 2 * x + y
        chips = [(1 - x, y), (x, 1 - y), (1 - x, 1 - y)]
        mk = lambda k, chip, src, dst: pltpu.make_async_remote_copy(
            src_ref=p_ref.at[src], dst_ref=out.at[dst], send_sem=sems[0].at[k], recv_sem=sems[1].at[k],
            device_id=(*chip, c), device_id_type=MESH)
        sends = [mk(k, chip, 2 * chip[0] + chip[1], j0) for k, chip in enumerate(chips)]
        recvs = [mk(k, chip, j0, 2 * chip[0] + chip[1]) for k, chip in enumerate(chips)]
        return sends, recvs, pltpu.make_async_copy(p_ref.at[j0], out.at[j0], sems[2])

    def start(ins, outs, sems):
        sends, _, mine = copies(ins, outs, sems)
        mine.start()
        for cp in sends:
            cp.start()

    def wait(ins, outs, sems):
        sends, recvs, mine = copies(ins, outs, sems)
        for cp in recvs:
            cp.wait_recv()
        for cp in sends:
            cp.wait_send()
        mine.wait()

    return _Side((p,), (SDS(p.shape, p.dtype),), _dma_sems(3), start, wait)


def _rs_pair_exchange(g, name):
    nd, R, W = g.shape
    nj = nd // 2

    def body(g_ref, out_ref, send_sems, recv_sems):
        x, y, c = _dev()
        cps = [pltpu.make_async_remote_copy(src_ref=g_ref.at[2 * j + (1 - c)], dst_ref=out_ref.at[j],
                                            send_sem=send_sems.at[j], recv_sem=recv_sems.at[j],
                                            device_id=(x, y, 1 - c), device_id_type=MESH) for j in range(nj)]
        for cp in cps:
            cp.start()
        for cp in cps:
            cp.wait()

    return pl.pallas_call(
        body, out_shape=SDS((nj, R, W), g.dtype), in_specs=[HBM_SPEC], out_specs=HBM_SPEC,
        scratch_shapes=[pltpu.SemaphoreType.DMA((nj,)), pltpu.SemaphoreType.DMA((nj,))], name=name)(g)


def _add_pairs(g, ra, name):
    nd, R, W = g.shape
    nj = nd // 2
    tr = _tile(R, 256, 8)
    cidx = lax.axis_index("c").astype(jnp.int32).reshape(1)

    def body(c_ref, a_ref, b_ref, o_ref):
        o_ref[...] = (a_ref[...].astype(F32) + b_ref[...].astype(F32)).astype(o_ref.dtype)

    gs = pltpu.PrefetchScalarGridSpec(
        num_scalar_prefetch=1, grid=(nj, R // tr),
        in_specs=[pl.BlockSpec((None, tr, W), lambda j, i, cr: (2 * j + cr[0], i, 0)),
                  pl.BlockSpec((None, tr, W), lambda j, i, cr: (j, i, 0))],
        out_specs=pl.BlockSpec((None, tr, W), lambda j, i, cr: (j, i, 0)))
    return pl.pallas_call(body, grid_spec=gs, out_shape=SDS((nj, R, W), g.dtype), name=name,
                          compiler_params=pltpu.CompilerParams(vmem_limit_bytes=VMEM_LIMIT))(cidx, g, ra)


def _rs_chip_exchange(p, name):
    nj, R, W = p.shape

    def body(p_ref, out_ref, send_sems, recv_sems, local_sem):
        x, y, c = _dev()
        j0 = 2 * x + y
        chips = [(1 - x, y), (x, 1 - y), (1 - x, 1 - y)]
        mine = pltpu.make_async_copy(p_ref.at[j0], out_ref.at[j0], local_sem)
        mine.start()

        def copy(k, chip):
            return pltpu.make_async_remote_copy(
                src_ref=p_ref.at[2 * chip[0] + chip[1]], dst_ref=out_ref.at[j0],
                send_sem=send_sems.at[k], recv_sem=recv_sems.at[k], device_id=(*chip, c), device_id_type=MESH)

        sent = [copy(k, chip) for k, chip in enumerate(chips)]
        for cp in sent:
            cp.start()
        for k, chip in enumerate(chips):
            pltpu.make_async_remote_copy(
                src_ref=p_ref.at[j0], dst_ref=out_ref.at[2 * chip[0] + chip[1]],
                send_sem=send_sems.at[k], recv_sem=recv_sems.at[k], device_id=(*chip, c), device_id_type=MESH).wait_recv()
        for cp in sent:
            cp.wait_send()
        mine.wait()

    return pl.pallas_call(
        body, out_shape=SDS((nj, R, W), p.dtype), in_specs=[HBM_SPEC], out_specs=HBM_SPEC,
        scratch_shapes=[pltpu.SemaphoreType.DMA((3,)), pltpu.SemaphoreType.DMA((3,)), pltpu.SemaphoreType.DMA],
        name=name)(p)


def _allgather_small(v, name):
    R, W = v.shape

    def body(v_ref, out_ref, send_sems, recv_sems, local_sem):
        x, y, c = _dev()
        mine = pltpu.make_async_copy(v_ref, out_ref.at[4 * x + 2 * y + c], local_sem)
        mine.start()
        peers = []
        for k in range(1, N_DEV):
            px = 1 - x if k & 4 else x
            py = 1 - y if k & 2 else y
            pc_ = 1 - c if k & 1 else c
            peers.append((px, py, pc_))
        sent = [pltpu.make_async_remote_copy(
            src_ref=v_ref, dst_ref=out_ref.at[4 * x + 2 * y + c], send_sem=send_sems.at[k], recv_sem=recv_sems.at[k],
            device_id=peer, device_id_type=MESH) for k, peer in enumerate(peers)]
        for cp in sent:
            cp.start()
        for k, (px, py, pc_) in enumerate(peers):
            pltpu.make_async_remote_copy(
                src_ref=v_ref, dst_ref=out_ref.at[4 * px + 2 * py + pc_], send_sem=send_sems.at[k],
                recv_sem=recv_sems.at[k], device_id=(px, py, pc_), device_id_type=MESH).wait_recv()
        for cp in sent:
            cp.wait_send()
        mine.wait()

    return pl.pallas_call(
        body, out_shape=SDS((N_DEV, R, W), v.dtype), in_specs=[VMEM_SPEC], out_specs=VMEM_SPEC,
        scratch_shapes=[pltpu.SemaphoreType.DMA((7,)), pltpu.SemaphoreType.DMA((7,)), pltpu.SemaphoreType.DMA],
        name=name)(v)


def _sum_slabs(a, name):
    ns, R, W = a.shape

    def body(a_ref, o_ref):
        acc = a_ref[0]
        for k in range(1, ns):
            acc = acc + a_ref[k]
        o_ref[...] = acc

    return pl.pallas_call(body, out_shape=SDS((R, W), a.dtype), in_specs=[VMEM_SPEC], out_specs=VMEM_SPEC, name=name)(a)


BIG = (("ada_w", "col"), ("w_in", "col"), ("w_ssd_out", "row"), ("w_sc_out", "row"), ("w_o", "row"), ("w_up", "col"),
       ("w_down", "row"))
EARLY = ("w_ssd_out", "w_sc_out", "w_o", "w_up", "w_down")
LATE = ("ada_w", "w_in")
MID_LAYOUT = ("w_in",)
SWAP_LAYOUT = ("w_up",)
CONVW = ("ssd_conv_w", "sc_conv_w", "ffn_conv_w")
REPL = ("ada_b", "mix_pre_g", "mix_post_g", "ssd_conv_b", "ssd_dt_bias", "ssd_a_log", "ssd_d", "ssd_norm_g", "ffn_pre_g",
        "ffn_post_g", "ffn_conv_b")
WEIGHTS = ("ada_w", "ada_b", "mix_pre_g", "mix_post_g", "w_in", "ssd_conv_w", "ssd_conv_b", "ssd_dt_bias", "ssd_a_log",
           "ssd_d", "ssd_norm_g", "w_ssd_out", "sc_conv_w", "w_sc_out", "w_o", "ffn_pre_g", "ffn_post_g", "w_up",
           "ffn_conv_w", "ffn_conv_b", "w_down")


def _pad_rows(a, mult):
    r = a.shape[-2]
    pad = -r % mult
    return a if pad == 0 else jnp.pad(a, [(0, 0)] * (a.ndim - 2) + [(0, pad), (0, 0)])


def _flat_rows(parts, mult):
    flat = jnp.concatenate([p.reshape(-1) for p in parts])
    flat = jnp.pad(flat, (0, -flat.shape[0] % ROW_W))
    return _pad_rows(flat.reshape(-1, ROW_W), mult)


def _unflat(buf, shapes):
    flat = buf.reshape(-1)
    out, o = [], 0
    for shp in shapes:
        n = 1
        for s in shp:
            n *= s
        out.append(flat[o:o + n].reshape(shp))
        o += n
    return out


def _pack_big_local(get, l):
    return [_pad_rows((get(n)[l].T if kind == "col" else get(n)[l]).reshape(-1, ROW_W), SLAB_ALIGN) for n, kind in BIG]


def _big_rows(shapes, names=None):
    out, o = {}, 0
    for n in (names if names is not None else [n for n, _ in BIG]):
        r = shapes[n][1] * shapes[n][2] // ROW_W
        out[n] = (o, o + r)
        o += -(-r // SLAB_ALIGN) * SLAB_ALIGN
    return out, o


def kernel(x, c, ada_w, ada_b, mix_pre_g, mix_post_g, w_in, ssd_conv_w, ssd_conv_b, ssd_dt_bias, ssd_a_log, ssd_d, ssd_norm_g, w_ssd_out, sc_conv_w, w_sc_out, w_o, ffn_pre_g, ffn_post_g, w_up, ffn_conv_w, ffn_conv_b, w_down, loss_target, m_ada_w, m_ada_b, m_mix_pre_g, m_mix_post_g, m_w_in, m_ssd_conv_w, m_ssd_conv_b, m_ssd_dt_bias, m_ssd_a_log, m_ssd_d, m_ssd_norm_g, m_w_ssd_out, m_sc_conv_w, m_w_sc_out, m_w_o, m_ffn_pre_g, m_ffn_post_g, m_w_up, m_ffn_conv_w, m_ffn_conv_b, m_w_down, v_ada_w, v_ada_b, v_mix_pre_g, v_mix_post_g, v_w_in, v_ssd_conv_w, v_ssd_conv_b, v_ssd_dt_bias, v_ssd_a_log, v_ssd_d, v_ssd_norm_g, v_w_ssd_out, v_sc_conv_w, v_w_sc_out, v_w_o, v_ffn_pre_g, v_ffn_post_g, v_w_up, v_ffn_conv_w, v_ffn_conv_b, v_w_down):
    wl = dict(zip(WEIGHTS, (ada_w, ada_b, mix_pre_g, mix_post_g, w_in, ssd_conv_w, ssd_conv_b, ssd_dt_bias, ssd_a_log,
                            ssd_d, ssd_norm_g, w_ssd_out, sc_conv_w, w_sc_out, w_o, ffn_pre_g, ffn_post_g, w_up,
                            ffn_conv_w, ffn_conv_b, w_down)))
    ml = dict(zip(WEIGHTS, (m_ada_w, m_ada_b, m_mix_pre_g, m_mix_post_g, m_w_in, m_ssd_conv_w, m_ssd_conv_b,
                            m_ssd_dt_bias, m_ssd_a_log, m_ssd_d, m_ssd_norm_g, m_w_ssd_out, m_sc_conv_w, m_w_sc_out, m_w_o,
                            m_ffn_pre_g, m_ffn_post_g, m_w_up, m_ffn_conv_w, m_ffn_conv_b, m_w_down)))
    vl = dict(zip(WEIGHTS, (v_ada_w, v_ada_b, v_mix_pre_g, v_mix_post_g, v_w_in, v_ssd_conv_w, v_ssd_conv_b,
                            v_ssd_dt_bias, v_ssd_a_log, v_ssd_d, v_ssd_norm_g, v_w_ssd_out, v_sc_conv_w, v_w_sc_out, v_w_o,
                            v_ffn_pre_g, v_ffn_post_g, v_w_up, v_ffn_conv_w, v_ffn_conv_b, v_w_down)))
    depth = ada_w.shape[0]
    shapes = {n: wl[n].shape for n in WEIGHTS}
    me = 4 * lax.axis_index("x") + 2 * lax.axis_index("y") + lax.axis_index("c")

    rows, n_big = _big_rows(shapes)
    conv_flat = jnp.concatenate([wl[n][l].reshape(-1) for l in range(depth) for n in CONVW])
    n_conv = conv_flat.shape[0]
    conv_flat = jnp.pad(conv_flat, (0, -n_conv % (ROW_W // 2)))
    conv_rows = lax.bitcast_convert_type(conv_flat, BF16).reshape(-1, ROW_W)

    def local_rows(l):
        pieces = _pack_big_local(lambda n: wl[n].astype(BF16), l) + ([conv_rows] if l == 0 else [])
        return _pad_rows(jnp.concatenate(pieces, axis=0), ROW_PAD)

    def layer_weights(l, gathered):
        w = {n: wl[n][l] for n in REPL}
        for n, kind in BIG:
            a, b = shapes[n][1], shapes[n][2]
            blk = gathered[:, rows[n][0]:rows[n][1]]
            w[n] = blk.reshape(N_DEV * b, a) if kind == "col" else blk.reshape(N_DEV * a, b)
        for n in CONVW:
            w[n] = conv_full[(l, n)]
        return w

    gathered = _allgather_big(local_rows(0), "allgather_weights")
    conv_all = lax.bitcast_convert_type(
        gathered[:, n_big:n_big + conv_rows.shape[0]].reshape(N_DEV, -1, 2), F32)[:, :n_conv]
    conv_full, o = {}, 0
    for l in range(depth):
        for n in CONVW:
            k, cl = shapes[n][1], shapes[n][2]
            conv_full[(l, n)] = conv_all[:, o:o + k * cl].reshape(N_DEV, k, cl).transpose(1, 0, 2).reshape(k, N_DEV * cl)
            o += k * cl

    nb, S, D = x.shape
    T = nb * S
    act = x.reshape(T, D)
    c8 = jnp.pad(c, ((0, MOD_ROWS - nb), (0, 0)))
    preps, saved = [], []
    for l in range(depth):
        preps.append(_prep_layer(layer_weights(l, gathered)))
        act, s, gathered = _layer_fwd(act, c8, preps[l], S, l, gather=local_rows(l + 1) if l + 1 < depth else None)
        saved.append(s)
    dy, lacc = _loss(act, loss_target.reshape(T, D), S, "loss")
    loss_loc = lacc[0, 0]

    group_rows = {grp: _big_rows(shapes, names) for grp, names in (("early", EARLY), ("late", LATE))}

    def pair_sums(g, grp, names, l):
        slabs = [_pad_rows(g[n].astype(BF16).reshape(N_DEV, -1, ROW_W), SLAB_ALIGN) for n in names]
        slabs.append(jnp.zeros((N_DEV, -group_rows[grp][1] % ROW_PAD, ROW_W), BF16))
        gslab = jnp.concatenate(slabs, axis=1)
        from_sibling = _rs_pair_exchange(gslab, f"rs_pair_exchange_{grp}_l{l}")
        return _add_pairs(gslab, from_sibling, f"rs_pair_add_{grp}_l{l}")

    grads, pending = [None] * depth, None
    from_chips = {"early": [None] * depth, "late": [None] * depth}
    for l in reversed(range(depth)):
        dy, grads[l], got, from_chips["early"][l] = _layer_bwd(
            dy, preps[l], saved[l], S, l, chip_sums=pending, early=lambda g, l=l: pair_sums(g, "early", EARLY, l))
        if pending is not None:
            from_chips["late"][l + 1] = got
        pending = pair_sums(grads[l], "late", LATE, l)
    from_chips["late"][0] = _rs_chip_exchange(pending, "rs_chip_exchange")
    dx = dy.reshape(nb, S, D)
    g_sums = {grp: [_sum_chips(from_chips[grp][l], f"rs_chip_sum_{grp}_l{l}") for l in range(depth)]
              for grp in ("early", "late")}

    def slab_of(l, n, kind):
        grp = "early" if n in EARLY else "late"
        r0, r1 = group_rows[grp][0][n]
        a, b = shapes[n][1], shapes[n][2]
        return g_sums[grp][l][r0:r1].reshape((b, a) if kind == "col" else (a, b))

    g_big, d_big, m_big, v_big = {}, {}, {}, {}
    for n, kind in BIG:
        if n in MID_LAYOUT:
            gm = jnp.stack([slab_of(l, n, kind) for l in range(depth)], axis=1)
            fwd, back = (lambda t: t.transpose(2, 0, 1)), (lambda t: t.transpose(1, 2, 0))
            res = [gm] + list(_adam_mid(gm, fwd(wl[n]), fwd(ml[n]), fwd(vl[n]), f"adam_{n}"))
        else:
            gt = jnp.stack([slab_of(l, n, kind) for l in range(depth)])
            fwd = back = (lambda t: t.swapaxes(1, 2)) if kind == "col" else (lambda t: t)
            if n in SWAP_LAYOUT:
                res = [gt] + list(_adam_nat(gt, fwd(wl[n]), fwd(ml[n]), fwd(vl[n]), f"adam_{n}"))
            else:
                gn = back(gt)
                res, back = [gn] + list(_adam_nat(gn, wl[n], ml[n], vl[n], f"adam_{n}")), (lambda t: t)
        g_big[n], d_big[n], m_big[n], v_big[n] = [back(t) for t in res]

    parts = [jnp.broadcast_to(loss_loc, (ROW_W,))]
    small_shapes = [(ROW_W,)]
    for l in range(depth):
        for n in REPL + CONVW:
            parts.append(grads[l][n])
            small_shapes.append(tuple(grads[l][n].shape))
    total = _sum_slabs(_allgather_small(_flat_rows(parts, 8), "allgather_small"), "sum_small")
    pieces = _unflat(total, small_shapes)
    loss = pieces[0][0]
    g_small, i = {}, 1
    for l in range(depth):
        for n in REPL + CONVW:
            gp = pieces[i]
            i += 1
            if n in CONVW:
                gp = lax.dynamic_slice_in_dim(gp, me * shapes[n][2], shapes[n][2], axis=1)
            g_small[(l, n)] = gp
    order = [(l, n) for l in range(depth) for n in REPL + CONVW]
    loc_shapes = [tuple(shapes[n][1:]) for _, n in order]
    packs = lambda f: _flat_rows([f(l, n) for l, n in order], 8)
    gs_small = packs(lambda l, n: g_small[(l, n)])
    _, d_sm, m_sm, v_sm = _adam(gs_small[None], packs(lambda l, n: wl[n][l]), packs(lambda l, n: ml[n][l]),
                                packs(lambda l, n: vl[n][l]), "adam_small")

    def unpack_small(buf):
        ps = _unflat(buf, loc_shapes)
        return {n: jnp.stack([ps[order.index((l, n))] for l in range(depth)]) for n in REPL + CONVW}

    outs = []
    for big, small in ((g_big, {n: jnp.stack([g_small[(l, n)] for l in range(depth)]) for n in REPL + CONVW}),
                       (d_big, unpack_small(d_sm)), (m_big, unpack_small(m_sm)), (v_big, unpack_small(v_sm))):
        merged = {**big, **small}
        outs += [merged[n] for n in WEIGHTS]
    return (loss, dx, *outs)
```
